```python
import math
import jax
import jax.numpy as jnp
from jax import lax
import numpy as np

D_MODEL = 1024
BATCH = 16
SEQ = 2048
DEPTH = 2
DEC_BATCH = 128
DEC_SEQ = 1
PAST_LEN = 16384
PAGE_SIZE = 128

N_MIX_LAYERS = (DEPTH + 1) // 2
N_CONV_LAYERS = DEPTH // 2

A_HEADS = 8
A_KV_HEADS = 2
A_GROUP = A_HEADS // A_KV_HEADS
A_HEAD_DIM = 64
WINDOW = 128
REL_BUCKETS = 32
REL_MAX_DIST = 128

B_HEADS = 4
B_DK = 64
B_DV = 128
B_CONV = 4
MLSTM_CHUNK = 64
FORGET_BIAS_LO = 3.0
FORGET_BIAS_HI = 6.0

C_WIDTH = D_MODEL
C_KERNEL = 31

N_GROUPS = 4
EXPERTS_PER_GROUP = 8
N_EXPERTS = N_GROUPS * EXPERTS_PER_GROUP
TOP_K = 2
EXPERT_FF = D_MODEL // 2
EXPERT_BLOCK = 128

EPS = 1e-6

A_Q = A_HEADS * A_HEAD_DIM
A_KV = A_KV_HEADS * A_HEAD_DIM
B_QK = B_HEADS * B_DK
B_V = B_HEADS * B_DV
MIX_WIDTH = A_Q + B_V
IN_COLS = A_Q + 2 * A_KV + 2 * B_QK + B_V + 2 * B_HEADS + B_V
SPLITS = (A_Q, A_Q + A_KV, A_Q + 2 * A_KV, A_Q + 2 * A_KV + 2 * B_QK,
          A_Q + 2 * A_KV + 2 * B_QK + B_V, A_Q + 2 * A_KV + 2 * B_QK + B_V + 2 * B_HEADS)

kernel_name = 'hybrid_swa_mlstm_conformer_hmoe_step'


def rmsnorm(x, g):
    xf = x.astype(jnp.float32)
    y = xf * lax.rsqrt(jnp.mean(xf * xf, axis=-1, keepdims=True) + EPS)
    return (y * g.astype(jnp.float32)).astype(x.dtype)


def layernorm(x, g, b):
    xf = x.astype(jnp.float32)
    xc = xf - jnp.mean(xf, axis=-1, keepdims=True)
    y = xc * lax.rsqrt(jnp.mean(xc * xc, axis=-1, keepdims=True) + EPS)
    return (y * g.astype(jnp.float32) + b.astype(jnp.float32)).astype(x.dtype)


def causal_dwconv(x_ext, w):
    ch = x_ext.shape[-1]
    return lax.conv_general_dilated(x_ext, w.astype(x_ext.dtype)[:, None, :], window_strides=(1,), padding='VALID',
                                    dimension_numbers=('NWC', 'WIO', 'NWC'), feature_group_count=ch)


def t5_bucket(dist):
    exact = REL_BUCKETS // 2
    d = jnp.maximum(dist, 0)
    large = exact + (jnp.log(jnp.maximum(d, 1).astype(jnp.float32) / exact)
                     / math.log(REL_MAX_DIST / exact) * (REL_BUCKETS - exact)).astype(jnp.int32)
    return jnp.where(d < exact, d, jnp.minimum(large, REL_BUCKETS - 1))


def window_attention(q, k, v, dist, valid, sinks, rel_table):
    logits = jnp.einsum('bnqkgd,bnskd->bnkgqs', q, k).astype(jnp.float32) * (A_HEAD_DIM ** -0.5)
    bias = rel_table.astype(jnp.float32)[t5_bucket(dist)]
    bias = jnp.moveaxis(bias, -1, 0).reshape(A_KV_HEADS, A_GROUP, dist.shape[0], dist.shape[1])
    logits = jnp.where(valid[:, None, None], logits + bias, -jnp.inf)
    sink = sinks.astype(jnp.float32).reshape(A_KV_HEADS, A_GROUP, 1, 1)
    mx = jnp.maximum(jnp.max(logits, axis=-1, keepdims=True), sink)
    p = jnp.exp(logits - mx)
    probs = p / (jnp.sum(p, axis=-1, keepdims=True) + jnp.exp(sink - mx))
    return jnp.einsum('bnkgqs,bnskd->bnqkgd', probs.astype(v.dtype), v)


def mlstm_chunk(carry, inp):
    c0, n0, m0 = carry
    q, k, v, ig, lf = inp
    L = q.shape[2]
    b = jnp.cumsum(lf, axis=-1)
    a = b + m0[..., None]
    causal = jnp.tril(jnp.ones((L, L), dtype=bool))
    d = jnp.where(causal, b[..., :, None] - b[..., None, :] + ig[..., None, :], -jnp.inf)
    m = jnp.maximum(a, jnp.max(d, axis=-1))
    dw = jnp.exp(d - m[..., None])
    aw = jnp.exp(a - m)
    s = jnp.einsum('bhtd,bhsd->bhts', q, k) * dw
    num = jnp.einsum('bhts,bhsv->bhtv', s, v) + aw[..., None] * jnp.einsum('bhtd,bhdv->bhtv', q, c0)
    den = jnp.sum(s, axis=-1) + aw * jnp.einsum('bhtd,bhd->bht', q, n0)
    h = num / jnp.maximum(jnp.abs(den), jnp.exp(-m))[..., None]
    w_last = dw[..., -1, :]
    decay = aw[..., -1]
    c1 = decay[..., None, None] * c0 + jnp.einsum('bhs,bhsd,bhsv->bhdv', w_last, k, v)
    n1 = decay[..., None] * n0 + jnp.einsum('bhs,bhsd->bhd', w_last, k)
    return (c1, n1, m[..., -1]), h


def ab_mixer(h, state, w_in, b_gates, w_qk_conv, sinks, rel_table, g_mnorm, w_out, prompt):
    bsz, T, _ = h.shape
    z = h @ w_in
    qa, ka, va, qk_pre, v_pre, gates, og = jnp.split(z, SPLITS, axis=-1)
    qa = qa.reshape(bsz, T, A_KV_HEADS, A_GROUP, A_HEAD_DIM)
    ka = ka.reshape(bsz, T, A_KV_HEADS, A_HEAD_DIM)
    va = va.reshape(bsz, T, A_KV_HEADS, A_HEAD_DIM)
    if prompt:
        nb = T // WINDOW
        qb = qa.reshape(bsz, nb, WINDOW, A_KV_HEADS, A_GROUP, A_HEAD_DIM)
        kb = ka.reshape(bsz, nb, WINDOW, A_KV_HEADS, A_HEAD_DIM)
        vb = va.reshape(bsz, nb, WINDOW, A_KV_HEADS, A_HEAD_DIM)

        def band(t):
            prev = jnp.concatenate([jnp.zeros_like(t[:, :1]), t[:, :-1]], axis=1)
            return jnp.concatenate([prev, t], axis=2)

        dist = WINDOW + jnp.arange(WINDOW)[:, None] - jnp.arange(2 * WINDOW)[None, :]
        kpos = (jnp.arange(nb)[:, None] - 1) * WINDOW + jnp.arange(2 * WINDOW)[None, :]
        valid = ((dist >= 0) & (dist <= WINDOW))[None] & (kpos >= 0)[:, None, :]
        att = window_attention(qb, band(kb), band(vb), dist, valid, sinks, rel_table)
        new_k, new_v = ka[:, -WINDOW:], va[:, -WINDOW:]
        qk_ext = jnp.pad(qk_pre, ((0, 0), (B_CONV - 1, 0), (0, 0)))
    else:
        win_k, win_v, c0, n0, m0, conv_buf = state
        k_all = jnp.concatenate([win_k.astype(ka.dtype), ka], axis=1)
        v_all = jnp.concatenate([win_v.astype(va.dtype), va], axis=1)
        n_buf = win_k.shape[1]
        dist = n_buf + jnp.arange(T)[:, None] - jnp.arange(n_buf + T)[None, :]
        valid = ((dist >= 0) & (dist <= WINDOW))[None]
        att = window_attention(qa[:, None], k_all[:, None], v_all[:, None], dist, valid, sinks, rel_table)[:, 0]
        new_k, new_v = k_all[:, -n_buf:], v_all[:, -n_buf:]
        qk_ext = jnp.concatenate([conv_buf.astype(qk_pre.dtype), qk_pre], axis=1)
    att = att.reshape(bsz, T, A_Q)

    qk = jax.nn.silu(causal_dwconv(qk_ext, w_qk_conv))
    new_conv = qk_ext[:, -(B_CONV - 1):]
    q, k = jnp.split(qk.astype(jnp.float32), 2, axis=-1)

    def to_heads(t, dh):
        return t.reshape(bsz, T, B_HEADS, dh).transpose(0, 2, 1, 3)

    q = to_heads(q, B_DK)
    k = to_heads(k, B_DK) * (B_DK ** -0.5)
    v = to_heads(v_pre.astype(jnp.float32), B_DV)
    gates = gates.astype(jnp.float32) + b_gates.astype(jnp.float32)
    ig = gates[..., :B_HEADS].transpose(0, 2, 1)
    lf = jax.nn.log_sigmoid(gates[..., B_HEADS:]).transpose(0, 2, 1)
    if prompt:
        nc = T // MLSTM_CHUNK

        def chunks(t):
            return jnp.moveaxis(t.reshape((bsz, B_HEADS, nc, MLSTM_CHUNK) + t.shape[3:]), 2, 0)

        init = (jnp.zeros((bsz, B_HEADS, B_DK, B_DV), jnp.float32),
                jnp.zeros((bsz, B_HEADS, B_DK), jnp.float32),
                jnp.full((bsz, B_HEADS), -jnp.inf, jnp.float32))
        (c1, n1, m1), hs = lax.scan(mlstm_chunk, init, (chunks(q), chunks(k), chunks(v), chunks(ig), chunks(lf)))
        hs = jnp.moveaxis(hs, 0, 2).reshape(bsz, B_HEADS, T, B_DV)
    else:
        carry = (c0.astype(jnp.float32), n0.astype(jnp.float32), m0.astype(jnp.float32))
        (c1, n1, m1), hs = mlstm_chunk(carry, (q, k, v, ig, lf))
    hs = hs.transpose(0, 2, 1, 3)
    hs = hs * lax.rsqrt(jnp.mean(hs * hs, axis=-1, keepdims=True) + EPS) * g_mnorm.astype(jnp.float32).reshape(B_HEADS, B_DV)
    out_b = (jax.nn.sigmoid(og.astype(jnp.float32)) * hs.reshape(bsz, T, B_V)).astype(h.dtype)

    y = jnp.concatenate([att.astype(h.dtype), out_b], axis=-1) @ w_out
    dt = h.dtype
    return y, (new_k, new_v, c1.astype(dt), n1.astype(dt), m1.astype(dt), new_conv)


def conformer_conv(h, conv_buf, w_pw1, b_pw1, w_dw, b_dw, ln_g, ln_b, w_pw2, b_pw2, prompt):
    u = h @ w_pw1 + b_pw1
    a, gate = jnp.split(u, 2, axis=-1)
    u = a * jax.nn.sigmoid(gate)
    if prompt:
        ext = jnp.pad(u, ((0, 0), (C_KERNEL - 1, 0), (0, 0)))
    else:
        ext = jnp.concatenate([conv_buf.astype(u.dtype), u], axis=1)
    y = causal_dwconv(ext, w_dw) + b_dw
    y = jax.nn.silu(layernorm(y, ln_g, ln_b))
    return y @ w_pw2 + b_pw2, ext[:, -(C_KERNEL - 1):]


def hier_moe(x, w_rg, b_rg, w_re, b_re, w_eg, w_eu, w_ed):
    bsz, T, D = x.shape
    n_tok = bsz * T
    xt = x.reshape(n_tok, D)
    xf = xt.astype(jnp.float32)
    g_logits = xf @ w_rg.astype(jnp.float32) + b_rg.astype(jnp.float32)
    g_idx = jnp.argmax(g_logits, axis=-1)
    g_gate = jnp.take_along_axis(jax.nn.softmax(g_logits, axis=-1), g_idx[:, None], axis=1)
    e_logits = jnp.einsum('nd,gde->nge', xf, w_re.astype(jnp.float32)) + b_re.astype(jnp.float32)
    e_logits = jnp.take_along_axis(e_logits, g_idx[:, None, None], axis=1)[:, 0]
    top_v, top_j = lax.top_k(e_logits, TOP_K)
    gate = jax.nn.softmax(top_v, axis=-1) * g_gate
    expert = (g_idx[:, None] * EXPERTS_PER_GROUP + top_j).astype(jnp.int32)

    n_assign = n_tok * TOP_K
    e_flat = expert.reshape(-1)
    tok_flat = jnp.repeat(jnp.arange(n_tok, dtype=jnp.int32), TOP_K)
    w_flat = gate.reshape(-1)
    order = jnp.argsort(e_flat)
    e_s, tok_s, w_s = e_flat[order], tok_flat[order], w_flat[order]
    counts = jnp.bincount(e_flat, length=N_EXPERTS).astype(jnp.int32)
    starts = jnp.cumsum(counts) - counts
    padded = (counts + EXPERT_BLOCK - 1) // EXPERT_BLOCK * EXPERT_BLOCK
    pends = jnp.cumsum(padded)
    pstarts = pends - padded
    n_blocks = (n_assign + N_EXPERTS * (EXPERT_BLOCK - 1)) // EXPERT_BLOCK
    n_slots = n_blocks * EXPERT_BLOCK
    dest = pstarts[e_s] + jnp.arange(n_assign, dtype=jnp.int32) - starts[e_s]
    slot_tok = jnp.full((n_slots,), n_tok, jnp.int32).at[dest].set(tok_s)
    slot_w = jnp.zeros((n_slots,), jnp.float32).at[dest].set(w_s)
    block_e = jnp.minimum(jnp.searchsorted(pends, jnp.arange(n_blocks, dtype=jnp.int32) * EXPERT_BLOCK, side='right'),
                          N_EXPERTS - 1)
    x_pad = jnp.concatenate([xt, jnp.zeros((1, D), xt.dtype)], axis=0)
    xb = x_pad[slot_tok].reshape(n_blocks, EXPERT_BLOCK, D)

    def expert_block(args):
        xblk, e = args
        hid = jax.nn.silu(xblk @ w_eg[e]) * (xblk @ w_eu[e])
        return hid @ w_ed[e]

    yb = lax.map(expert_block, (xb, block_e)).reshape(n_slots, D)
    out = jnp.zeros((n_tok + 1, D), jnp.float32).at[slot_tok].add(yb.astype(jnp.float32) * slot_w[:, None])[:n_tok]
    return out.astype(x.dtype).reshape(bsz, T, D)


def trunk(x, caches, weights, prompt):
    (norm_mix, norm_ffn, norm_final, rel_bias_table, w_in_mix, b_mlstm_gates, w_mlstm_qk_conv, attn_sinks,
     g_mlstm_norm, w_out_mix, w_pw1, b_pw1, w_dw, b_dw, ln_conv_g, ln_conv_b, w_pw2, b_pw2,
     w_router_group, b_router_group, w_router_expert, b_router_expert, w_expert_gate, w_expert_up,
     w_expert_down) = weights
    mix_states, conv_states = [], []
    for layer in range(DEPTH):
        li = layer // 2
        h = rmsnorm(x, norm_mix[layer])
        if layer % 2 == 0:
            st = None if prompt else (caches[0][li], caches[1][li], caches[2][li], caches[3][li], caches[4][li], caches[5][li])
            y, ns = ab_mixer(h, st, w_in_mix[li], b_mlstm_gates[li], w_mlstm_qk_conv[li], attn_sinks[li],
                             rel_bias_table, g_mlstm_norm[li], w_out_mix[li], prompt)
            mix_states.append(ns)
        else:
            y, nbuf = conformer_conv(h, None if prompt else caches[6][li], w_pw1[li], b_pw1[li], w_dw[li], b_dw[li],
                                     ln_conv_g[li], ln_conv_b[li], w_pw2[li], b_pw2[li], prompt)
            conv_states.append(nbuf)
        x = x + y
        x = x + hier_moe(rmsnorm(x, norm_ffn[layer]), w_router_group[layer], b_router_group[layer],
                         w_router_expert[layer], b_router_expert[layer], w_expert_gate[layer],
                         w_expert_up[layer], w_expert_down[layer])
    wk, wv, mc, mn, mm, mconv = (jnp.stack(s) for s in zip(*mix_states))
    return (rmsnorm(x, norm_final), wk, wv, mc, mn, mm, mconv, jnp.stack(conv_states))


def setup_inputs(seed: int = 0) -> dict:
    key = jax.random.key(seed)
    it = iter(jax.random.split(key, 40))

    def nrm(shape, scale):
        return jax.random.normal(next(it), shape, jnp.float32) * scale

    win_buf = min(WINDOW, PAST_LEN)
    forget_bias = jnp.broadcast_to(jnp.linspace(FORGET_BIAS_LO, FORGET_BIAS_HI, B_HEADS), (N_MIX_LAYERS, B_HEADS))
    return {
        'x_prompt': nrm((BATCH, SEQ, D_MODEL), 1.0),
        'x_sample': nrm((DEC_BATCH, DEC_SEQ, D_MODEL), 1.0),
        'cache_win_k': nrm((N_MIX_LAYERS, DEC_BATCH, win_buf, A_KV_HEADS, A_HEAD_DIM), 1.0),
        'cache_win_v': nrm((N_MIX_LAYERS, DEC_BATCH, win_buf, A_KV_HEADS, A_HEAD_DIM), 1.0),
        'state_mlstm_c': nrm((N_MIX_LAYERS, DEC_BATCH, B_HEADS, B_DK, B_DV), 0.5),
        'state_mlstm_n': nrm((N_MIX_LAYERS, DEC_BATCH, B_HEADS, B_DK), 0.5),
        'state_mlstm_m': 1.0 + nrm((N_MIX_LAYERS, DEC_BATCH, B_HEADS), 0.5),
        'state_mlstm_conv': nrm((N_MIX_LAYERS, DEC_BATCH, B_CONV - 1, 2 * B_QK), 1.0),
        'state_conv': nrm((N_CONV_LAYERS, DEC_BATCH, C_KERNEL - 1, C_WIDTH), 0.5),
        'norm_mix': 1.0 + nrm((DEPTH, D_MODEL), 0.02),
        'norm_ffn': 1.0 + nrm((DEPTH, D_MODEL), 0.02),
        'norm_final': 1.0 + nrm((D_MODEL,), 0.02),
        'rel_bias_table': nrm((REL_BUCKETS, A_HEADS), 0.5),
        'w_in_mix': nrm((N_MIX_LAYERS, D_MODEL, IN_COLS), D_MODEL ** -0.5),
        'b_mlstm_gates': jnp.concatenate([nrm((N_MIX_LAYERS, B_HEADS), 0.1),
                                          forget_bias + nrm((N_MIX_LAYERS, B_HEADS), 0.1)], axis=-1),
        'w_mlstm_qk_conv': nrm((N_MIX_LAYERS, B_CONV, 2 * B_QK), B_CONV ** -0.5),
        'attn_sinks': nrm((N_MIX_LAYERS, A_HEADS), 0.5),
        'g_mlstm_norm': 1.0 + nrm((N_MIX_LAYERS, B_V), 0.02),
        'w_out_mix': nrm((N_MIX_LAYERS, MIX_WIDTH, D_MODEL), MIX_WIDTH ** -0.5),
        'w_pw1': nrm((N_CONV_LAYERS, D_MODEL, 2 * C_WIDTH), D_MODEL ** -0.5),
        'b_pw1': nrm((N_CONV_LAYERS, 2 * C_WIDTH), 0.02),
        'w_dw': nrm((N_CONV_LAYERS, C_KERNEL, C_WIDTH), C_KERNEL ** -0.5),
        'b_dw': nrm((N_CONV_LAYERS, C_WIDTH), 0.02),
        'ln_conv_g': 1.0 + nrm((N_CONV_LAYERS, C_WIDTH), 0.02),
        'ln_conv_b': nrm((N_CONV_LAYERS, C_WIDTH), 0.02),
        'w_pw2': nrm((N_CONV_LAYERS, C_WIDTH, D_MODEL), C_WIDTH ** -0.5),
        'b_pw2': nrm((N_CONV_LAYERS, D_MODEL), 0.02),
        'w_router_group': nrm((DEPTH, D_MODEL, N_GROUPS), D_MODEL ** -0.5),
        'b_router_group': nrm((DEPTH, N_GROUPS), 0.01),
        'w_router_expert': nrm((DEPTH, N_GROUPS, D_MODEL, EXPERTS_PER_GROUP), D_MODEL ** -0.5),
        'b_router_expert': nrm((DEPTH, N_GROUPS, EXPERTS_PER_GROUP), 0.01),
        'w_expert_gate': nrm((DEPTH, N_EXPERTS, D_MODEL, EXPERT_FF), D_MODEL ** -0.5),
        'w_expert_up': nrm((DEPTH, N_EXPERTS, D_MODEL, EXPERT_FF), D_MODEL ** -0.5),
        'w_expert_down': nrm((DEPTH, N_EXPERTS, EXPERT_FF, D_MODEL), EXPERT_FF ** -0.5),
    }


def reference(x_prompt, x_sample, cache_win_k, cache_win_v, state_mlstm_c, state_mlstm_n, state_mlstm_m,
              state_mlstm_conv, state_conv, norm_mix, norm_ffn, norm_final, rel_bias_table, w_in_mix,
              b_mlstm_gates, w_mlstm_qk_conv, attn_sinks, g_mlstm_norm, w_out_mix, w_pw1, b_pw1, w_dw, b_dw,
              ln_conv_g, ln_conv_b, w_pw2, b_pw2, w_router_group, b_router_group, w_router_expert,
              b_router_expert, w_expert_gate, w_expert_up, w_expert_down):
    weights = (norm_mix, norm_ffn, norm_final, rel_bias_table, w_in_mix, b_mlstm_gates, w_mlstm_qk_conv,
               attn_sinks, g_mlstm_norm, w_out_mix, w_pw1, b_pw1, w_dw, b_dw, ln_conv_g, ln_conv_b, w_pw2,
               b_pw2, w_router_group, b_router_group, w_router_expert, b_router_expert, w_expert_gate,
               w_expert_up, w_expert_down)
    caches = (cache_win_k, cache_win_v, state_mlstm_c, state_mlstm_n, state_mlstm_m, state_mlstm_conv, state_conv)
    y_prompt, p_wk, p_wv, p_c, p_n, p_m, p_mconv, p_conv = trunk(x_prompt, None, weights, True)
    y_sample, s_wk, s_wv, s_c, s_n, s_m, s_mconv, s_conv = trunk(x_sample, caches, weights, False)
    return (y_prompt, y_sample, p_wk, p_wv, p_c, p_n, p_m, p_mconv, p_conv,
            s_wk, s_wv, s_c, s_n, s_m, s_mconv, s_conv)
```

```python
import functools
import math

import numpy as np
import jax
import jax.numpy as jnp
from jax import lax
from jax.experimental import pallas as pl
from jax.experimental.pallas import tpu as pltpu

F32 = jnp.float32
BF16 = jnp.bfloat16
I32 = jnp.int32
HI = lax.Precision.HIGHEST
NEG_INF = float("-inf")

LANES = 128
SUBLANES = 8
VMEM_LIMIT = 56 * 1024 * 1024

D_MODEL = 1024
A_HEADS = 8
A_KV_HEADS = 2
A_GROUP = A_HEADS // A_KV_HEADS
A_HEAD_DIM = 64
WINDOW = 128
REL_BUCKETS = 32
REL_MAX_DIST = 128
B_HEADS = 4
B_DK = 64
B_DV = 128
B_CONV = 4
C_KERNEL = 31
N_GROUPS = 4
EXPERTS_PER_GROUP = 8
N_EXPERTS = N_GROUPS * EXPERTS_PER_GROUP
TOP_K = 2
EXPERT_FF = D_MODEL // 2
EXPERT_BLOCK = 128
EPS = 1e-6

A_Q = A_HEADS * A_HEAD_DIM
A_KV = A_KV_HEADS * A_HEAD_DIM
B_QK = B_HEADS * B_DK
B_V = B_HEADS * B_DV
ROW_TILES = D_MODEL // LANES

Z_QA, Z_QK, Z_VB, Z_OG, Z_KA, Z_VA, Z_GATES = 0, 512, 1024, 1536, 2048, 2176, 2304
Z_COLS = 2432
MLSTM_CHUNK = 128


def _cparams(*sem):
    return pltpu.CompilerParams(dimension_semantics=sem, vmem_limit_bytes=VMEM_LIMIT)


def _rms(x, g):
    return x * lax.rsqrt(jnp.mean(x * x, axis=-1, keepdims=True) + EPS) * g


def _mm(a, w, precise):
    if precise:
        return jnp.dot(a.astype(F32), w, precision=HI, preferred_element_type=F32)
    return jnp.dot(a.astype(BF16), w, preferred_element_type=F32)


def _t5_buckets(dist):
    exact = REL_BUCKETS // 2
    d = np.maximum(dist, 0)
    large = exact + (np.log(np.maximum(d, 1).astype(np.float32) / exact)
                     / math.log(REL_MAX_DIST / exact) * (REL_BUCKETS - exact)).astype(np.int32)
    return np.where(d < exact, d, np.minimum(large, REL_BUCKETS - 1)).astype(np.int32)


def _norm_proj_body(x_ref, g_ref, w_ref, o_ref, *, precise):
    h = _rms(x_ref[...], g_ref[...])
    o_ref[...] = _mm(h, w_ref[...], precise)


def norm_proj(x, g, w, *, precise):
    rows, d = x.shape
    cols = w.shape[1]
    tm = min(rows, 128 if precise else 512)
    return pl.pallas_call(
        functools.partial(_norm_proj_body, precise=precise),
        grid=(rows // tm,),
        in_specs=[pl.BlockSpec((tm, d), lambda i: (i, 0)),
                  pl.BlockSpec((1, d), lambda i: (0, 0)),
                  pl.BlockSpec((d, cols), lambda i: (0, 0))],
        out_specs=pl.BlockSpec((tm, cols), lambda i: (i, 0)),
        out_shape=jax.ShapeDtypeStruct((rows, cols), F32),
        compiler_params=_cparams("parallel"),
        name="norm_proj",
    )(x, g, w)


def _attn_prompt_body(tab_ref, sink_ref, bkt_ref, q_ref, kp_ref, kc_ref, vp_ref, vc_ref, o_ref, bias_ref):
    b = pl.program_id(0)
    n = pl.program_id(1)

    @pl.when((b == 0) & (n == 0))
    def _():
        bk = bkt_ref[...]
        for h in range(A_HEADS):
            acc = jnp.full(bk.shape, NEG_INF, F32)
            for t in range(REL_BUCKETS):
                acc = jnp.where(bk == t, tab_ref[t, h], acc)
            bias_ref[h] = acc

    q = q_ref[...] * (A_HEAD_DIM ** -0.5)
    kb = jnp.concatenate([kp_ref[...], kc_ref[...]], axis=0).astype(BF16)
    vb = jnp.concatenate([vp_ref[...], vc_ref[...]], axis=0).astype(BF16)
    col = lax.broadcasted_iota(I32, (WINDOW, 2 * WINDOW), 1)
    dead = col < jnp.where(n == 0, WINDOW, 0)
    outs = []
    for h in range(A_HEADS):
        kvh = h // A_GROUP
        qh = q[:, h * A_HEAD_DIM:(h + 1) * A_HEAD_DIM].astype(BF16)
        kh = kb[:, kvh * A_HEAD_DIM:(kvh + 1) * A_HEAD_DIM]
        vh = vb[:, kvh * A_HEAD_DIM:(kvh + 1) * A_HEAD_DIM]
        s = lax.dot_general(qh, kh, (((1,), (1,)), ((), ())), preferred_element_type=F32)
        s = jnp.where(dead, NEG_INF, s + bias_ref[h])
        sink = sink_ref[h]
        mx = jnp.maximum(jnp.max(s, axis=-1, keepdims=True), sink)
        p = jnp.exp(s - mx)
        den = jnp.sum(p, axis=-1, keepdims=True) + jnp.exp(sink - mx)
        outs.append(jnp.dot(p.astype(BF16), vh, preferred_element_type=F32) / den)
    o_ref[...] = jnp.concatenate(outs, axis=1).astype(o_ref.dtype)


def attn_prompt(z, rel_table, sinks, bsz, seq):
    nb = seq // WINDOW
    dist = WINDOW + np.arange(WINDOW)[:, None] - np.arange(2 * WINDOW)[None, :]
    bkt = np.where((dist >= 0) & (dist <= WINDOW), _t5_buckets(dist), -1).astype(np.int32)
    kcol, vcol = Z_KA // LANES, Z_VA // LANES
    smem = pl.BlockSpec(memory_space=pltpu.SMEM)

    def cur(c):
        return pl.BlockSpec((WINDOW, LANES), lambda b, n: (b * nb + n, c))

    def prev(c):
        return pl.BlockSpec((WINDOW, LANES), lambda b, n: (b * nb + jnp.maximum(n - 1, 0), c))

    return pl.pallas_call(
        _attn_prompt_body,
        grid=(bsz, nb),
        in_specs=[smem, smem,
                  pl.BlockSpec((WINDOW, 2 * WINDOW), lambda b, n: (0, 0)),
                  pl.BlockSpec((WINDOW, A_Q), lambda b, n: (b * nb + n, 0)),
                  prev(kcol), cur(kcol), prev(vcol), cur(vcol)],
        out_specs=pl.BlockSpec((WINDOW, A_Q), lambda b, n: (b * nb + n, 0)),
        out_shape=jax.ShapeDtypeStruct((bsz * seq, A_Q), BF16),
        scratch_shapes=[pltpu.VMEM((A_HEADS, WINDOW, 2 * WINDOW), F32)],
        compiler_params=_cparams("arbitrary", "arbitrary"),
        name="attn_prompt",
    )(rel_table, sinks, jnp.asarray(bkt), z, z, z, z, z)


def _log_sigmoid(x):
    return -(jnp.maximum(-x, 0.0) + jnp.log1p(jnp.exp(-jnp.abs(x))))


def _mlstm_prompt_body(qk_ref, v_ref, g_ref, og_ref, cw_ref, bg_ref, gn_ref,
                       ob_ref, c_out, n_out, m_out,
                       c_sc, n_sc, m_sc, hist_sc, *, L, nc):
    ci = pl.program_id(1)

    @pl.when(ci == 0)
    def _():
        c_sc[...] = jnp.zeros(c_sc.shape, F32)
        n_sc[...] = jnp.zeros(n_sc.shape, F32)
        m_sc[...] = jnp.full(m_sc.shape, NEG_INF, F32)
        hist_sc[...] = jnp.zeros(hist_sc.shape, F32)

    cur = qk_ref[...]
    ext = jnp.concatenate([hist_sc[...], cur], axis=0)
    cw = cw_ref[...]
    off = SUBLANES - (B_CONV - 1)
    conv = ext[off:off + L] * cw[0:1]
    for j in range(1, B_CONV):
        conv = conv + ext[off + j:off + j + L] * cw[j:j + 1]
    hist_sc[...] = cur[L - SUBLANES:L]
    qk = conv * jax.nn.sigmoid(conv)
    q_all = qk[:, :B_QK]
    k_all = qk[:, B_QK:] * (B_DK ** -0.5)
    v_all = v_ref[...]
    og = og_ref[...]
    gn = gn_ref[...]

    G = g_ref[...] + bg_ref[...]
    lf = _log_sigmoid(G)
    row = lax.broadcasted_iota(I32, (L, L), 0)
    colm = lax.broadcasted_iota(I32, (L, L), 1)
    causal = colm <= row
    Bc = jnp.dot(causal.astype(F32), lf, precision=HI, preferred_element_type=F32)
    BT = Bc.T
    GT = G.T

    outs = []
    for h in range(B_HEADS):
        qh = q_all[:, h * B_DK:(h + 1) * B_DK]
        kh = k_all[:, h * B_DK:(h + 1) * B_DK]
        vh = v_all[:, h * B_DV:(h + 1) * B_DV]
        b_col = Bc[:, B_HEADS + h:B_HEADS + h + 1]
        ig_col = G[:, h:h + 1]
        b_row = BT[B_HEADS + h:B_HEADS + h + 1, :]
        ig_row = GT[h:h + 1, :]
        c0 = c_sc[h]
        n0 = n_sc[h:h + 1, :]
        m0 = m_sc[h:h + 1, 0:1]
        a = b_col + m0
        d = jnp.where(causal, b_col - b_row + ig_row, NEG_INF)
        m = jnp.maximum(a, jnp.max(d, axis=-1, keepdims=True))
        dw = jnp.exp(d - m)
        aw = jnp.exp(a - m)
        qb = qh.astype(BF16)
        vb = vh.astype(BF16)
        s = lax.dot_general(qb, kh.astype(BF16), (((1,), (1,)), ((), ())), preferred_element_type=F32) * dw
        num = (jnp.dot(s.astype(BF16), vb, preferred_element_type=F32)
               + aw * jnp.dot(qb, c0.astype(BF16), preferred_element_type=F32))
        den = jnp.sum(s, axis=-1, keepdims=True) + aw * jnp.sum(qh * n0, axis=-1, keepdims=True)
        hh = num / jnp.maximum(jnp.abs(den), jnp.exp(-m))
        m_last = m[L - 1:L, :]
        wl = jnp.exp(b_col[L - 1:L, :] - b_col + ig_col - m_last)
        decay = aw[L - 1:L, :]
        kw = kh * wl
        c_sc[h] = decay * c0 + lax.dot_general(kw.astype(BF16), vb, (((0,), (0,)), ((), ())),
                                               preferred_element_type=F32)
        n_sc[h:h + 1, :] = decay * n0 + jnp.sum(kw, axis=0, keepdims=True)
        m_sc[h:h + 1, :] = jnp.broadcast_to(m_last, (1, LANES))
        hn = hh * lax.rsqrt(jnp.mean(hh * hh, axis=-1, keepdims=True) + EPS) * gn[:, h * B_DV:(h + 1) * B_DV]
        outs.append(jax.nn.sigmoid(og[:, h * B_DV:(h + 1) * B_DV]) * hn)
    ob_ref[...] = jnp.concatenate(outs, axis=1).astype(ob_ref.dtype)

    @pl.when(ci == nc - 1)
    def _():
        c_out[0] = c_sc[...]
        n_out[0] = n_sc[...]
        m_out[0] = m_sc[...]


def mlstm_prompt(z, conv_w, b_gates_pad, g_norm, bsz, seq):
    L = MLSTM_CHUNK
    nc = seq // L

    def zspec(width, colblk):
        return pl.BlockSpec((L, width), lambda b, c: (b * nc + c, colblk))

    const = lambda shape: pl.BlockSpec(shape, lambda b, c: (0,) * len(shape))
    return pl.pallas_call(
        functools.partial(_mlstm_prompt_body, L=L, nc=nc),
        grid=(bsz, nc),
        in_specs=[zspec(2 * B_QK, Z_QK // (2 * B_QK)), zspec(B_V, Z_VB // B_V), zspec(LANES, Z_GATES // LANES),
                  zspec(B_V, Z_OG // B_V), const((B_CONV, 2 * B_QK)), const((1, LANES)), const((1, B_V))],
        out_specs=[pl.BlockSpec((L, B_V), lambda b, c: (b * nc + c, 0)),
                   pl.BlockSpec((1, B_HEADS, B_DK, B_DV), lambda b, c: (b, 0, 0, 0)),
                   pl.BlockSpec((1, SUBLANES, B_DK), lambda b, c: (b, 0, 0)),
                   pl.BlockSpec((1, SUBLANES, LANES), lambda b, c: (b, 0, 0))],
        out_shape=[jax.ShapeDtypeStruct((bsz * seq, B_V), BF16),
                   jax.ShapeDtypeStruct((bsz, B_HEADS, B_DK, B_DV), F32),
                   jax.ShapeDtypeStruct((bsz, SUBLANES, B_DK), F32),
                   jax.ShapeDtypeStruct((bsz, SUBLANES, LANES), F32)],
        scratch_shapes=[pltpu.VMEM((B_HEADS, B_DK, B_DV), F32), pltpu.VMEM((SUBLANES, B_DK), F32),
                        pltpu.VMEM((SUBLANES, LANES), F32), pltpu.VMEM((SUBLANES, 2 * B_QK), F32)],
        compiler_params=_cparams("arbitrary", "arbitrary"),
        name="mlstm_prompt",
    )(z, z, z, z, conv_w, b_gates_pad, g_norm)


SAMPLE_TILE = 8


def _row_to_col(row, n):
    eye = lax.broadcasted_iota(I32, (n, n), 0) == lax.broadcasted_iota(I32, (n, n), 1)
    return jnp.sum(jnp.where(eye, jnp.broadcast_to(row, (n, n)), 0.0), axis=1, keepdims=True)


def _mix_sample_body(tab_ref, sink_ref, bkt_ref, z_ref, ck_ref, cv_ref, c0_ref, n0_ref, m0_ref, cb_ref,
                     cw_ref, bg_ref, gn_ref,
                     att_ref, ob_ref, nk_ref, nv_ref, c1_ref, n1_ref, m1_ref, ncb_ref, *, tb):
    bk = bkt_ref[...]
    cw = cw_ref[...]
    gn = gn_ref[...]
    zero_half = jnp.zeros((1, A_HEAD_DIM), F32)
    att_rows, ob_rows, m_rows = [], [], []
    for i in range(tb):
        zr = z_ref[i:i + 1, :]
        q_att = zr[:, Z_QA:Z_QA + A_Q] * (A_HEAD_DIM ** -0.5)
        ka = zr[:, Z_KA:Z_KA + A_KV]
        va = zr[:, Z_VA:Z_VA + A_KV]
        kc = ck_ref[i]
        vc = cv_ref[i]
        nk_ref[i] = jnp.concatenate([kc[1:], ka], axis=0)
        nv_ref[i] = jnp.concatenate([vc[1:], va], axis=0)
        heads = []
        for h in range(A_HEADS):
            kvh = h // A_GROUP
            qh = q_att[:, h * A_HEAD_DIM:(h + 1) * A_HEAD_DIM]
            qrow = jnp.concatenate([qh, zero_half] if kvh == 0 else [zero_half, qh], axis=1)
            bias = jnp.full(bk.shape, 0.0, F32)
            for t in range(REL_BUCKETS):
                bias = jnp.where(bk == t, tab_ref[t, h], bias)
            lc = jnp.sum(kc * qrow, axis=-1, keepdims=True) + bias
            ln = jnp.sum(ka * qrow, axis=-1, keepdims=True) + tab_ref[0, h]
            sink = sink_ref[h]
            mx = jnp.maximum(jnp.maximum(jnp.max(lc, axis=0, keepdims=True), ln), sink)
            pc = jnp.exp(lc - mx)
            pn = jnp.exp(ln - mx)
            den = jnp.sum(pc, axis=0, keepdims=True) + pn + jnp.exp(sink - mx)
            o = (jnp.sum(pc * vc, axis=0, keepdims=True) + pn * va) / den
            heads.append(o[:, kvh * A_HEAD_DIM:(kvh + 1) * A_HEAD_DIM])
        att_rows.append(jnp.concatenate(heads, axis=1))

        qk_pre = zr[:, Z_QK:Z_QK + 2 * B_QK]
        hist = cb_ref[i]
        conv = qk_pre * cw[B_CONV - 1:B_CONV]
        for j in range(B_CONV - 1):
            conv = conv + hist[j:j + 1] * cw[j:j + 1]
        ncb_ref[i] = jnp.concatenate([hist[1:], qk_pre], axis=0)
        qk = conv * jax.nn.sigmoid(conv)
        G = zr[:, Z_GATES:Z_GATES + LANES] + bg_ref[...]
        lfr = _log_sigmoid(G)
        og = zr[:, Z_OG:Z_OG + B_V]
        v_pre = zr[:, Z_VB:Z_VB + B_V]
        obs, ms = [], []
        for h in range(B_HEADS):
            qh = qk[:, h * B_DK:(h + 1) * B_DK]
            kh = qk[:, B_QK + h * B_DK:B_QK + (h + 1) * B_DK] * (B_DK ** -0.5)
            vh = v_pre[:, h * B_DV:(h + 1) * B_DV]
            ig = G[:, h:h + 1]
            lf = lfr[:, B_HEADS + h:B_HEADS + h + 1]
            c0 = c0_ref[i, h]
            n0 = n0_ref[i, h:h + 1, :]
            m0 = m0_ref[i:i + 1, h:h + 1]
            a = lf + m0
            m = jnp.maximum(a, ig)
            dw = jnp.exp(ig - m)
            aw = jnp.exp(a - m)
            s = jnp.sum(qh * kh, axis=-1, keepdims=True) * dw
            q_col = _row_to_col(qh, B_DK)
            k_col = _row_to_col(kh, B_DK)
            num = s * vh + aw * jnp.sum(q_col * c0, axis=0, keepdims=True)
            den = s + aw * jnp.sum(qh * n0, axis=-1, keepdims=True)
            hh = num / jnp.maximum(jnp.abs(den), jnp.exp(-m))
            c1_ref[i, h] = aw * c0 + dw * (k_col * vh)
            n1_ref[i, h:h + 1, :] = aw * n0 + dw * kh
            ms.append(m)
            hn = hh * lax.rsqrt(jnp.mean(hh * hh, axis=-1, keepdims=True) + EPS) * gn[:, h * B_DV:(h + 1) * B_DV]
            obs.append(jax.nn.sigmoid(og[:, h * B_DV:(h + 1) * B_DV]) * hn)
        ob_rows.append(jnp.concatenate(obs, axis=1))
        lane = lax.broadcasted_iota(I32, (1, LANES), 1)
        mrow = jnp.zeros((1, LANES), F32)
        for h in range(B_HEADS):
            mrow = jnp.where(lane == h, ms[h], mrow)
        m_rows.append(mrow)
    att_ref[...] = jnp.concatenate(att_rows, axis=0)
    ob_ref[...] = jnp.concatenate(ob_rows, axis=0)
    m1_ref[...] = jnp.concatenate(m_rows, axis=0)


def mix_sample(z, rel_table, sinks, ck, cv, c0, n0, m0, conv_buf, conv_w, b_gates_pad, g_norm):
    nb = z.shape[0]
    tb = min(SAMPLE_TILE, nb)
    n_buf = ck.shape[1]
    bkt = _t5_buckets(n_buf - np.arange(n_buf))[:, None]
    smem = pl.BlockSpec(memory_space=pltpu.SMEM)
    const = lambda shape: pl.BlockSpec(shape, lambda i: (0,) * len(shape))
    lead = lambda shape: pl.BlockSpec((tb,) + shape, lambda i: (i,) + (0,) * len(shape))
    return pl.pallas_call(
        functools.partial(_mix_sample_body, tb=tb),
        grid=(nb // tb,),
        in_specs=[smem, smem, const((n_buf, 1)), lead((Z_COLS,)), lead((n_buf, A_KV)), lead((n_buf, A_KV)),
                  lead((B_HEADS, B_DK, B_DV)), lead((B_HEADS, B_DK)), lead((B_HEADS,)),
                  lead((B_CONV - 1, 2 * B_QK)), const((B_CONV, 2 * B_QK)), const((1, LANES)), const((1, B_V))],
        out_specs=[lead((A_Q,)), lead((B_V,)), lead((n_buf, A_KV)), lead((n_buf, A_KV)),
                   lead((B_HEADS, B_DK, B_DV)), lead((B_HEADS, B_DK)), lead((LANES,)),
                   lead((B_CONV - 1, 2 * B_QK))],
        out_shape=[jax.ShapeDtypeStruct((nb, A_Q), F32), jax.ShapeDtypeStruct((nb, B_V), F32),
                   jax.ShapeDtypeStruct(ck.shape, F32), jax.ShapeDtypeStruct(cv.shape, F32),
                   jax.ShapeDtypeStruct(c0.shape, F32), jax.ShapeDtypeStruct(n0.shape, F32),
                   jax.ShapeDtypeStruct((nb, LANES), F32), jax.ShapeDtypeStruct(conv_buf.shape, F32)],
        compiler_params=_cparams("parallel"),
        name="mix_sample",
    )(rel_table, sinks, jnp.asarray(bkt), z, ck, cv, c0, n0, m0, conv_buf, conv_w, b_gates_pad, g_norm)


def _store_row_tiles(ref, val, rows):
    for s in range(ROW_TILES):
        ref[pl.ds(s, rows, stride=ROW_TILES), :] = val[:, s * LANES:(s + 1) * LANES]


def _load_row_tiles(ref, rows, start=0, stride=ROW_TILES):
    return jnp.concatenate([ref[pl.ds(start + s, rows, stride=stride), :] for s in range(ROW_TILES)], axis=1)


def _route(logits):
    lane = lax.broadcasted_iota(I32, logits.shape, 1)
    big = jnp.int32(1 << 20)
    gl = jnp.where(lane < N_GROUPS, logits, NEG_INF)
    gmax = jnp.max(gl, axis=-1, keepdims=True)
    gidx = jnp.min(jnp.where(gl == gmax, lane, big), axis=-1, keepdims=True)
    g_gate = 1.0 / jnp.sum(jnp.exp(gl - gmax), axis=-1, keepdims=True)
    lo = N_GROUPS + gidx * EXPERTS_PER_GROUP
    el = jnp.where((lane >= lo) & (lane < lo + EXPERTS_PER_GROUP), logits, NEG_INF)
    v1 = jnp.max(el, axis=-1, keepdims=True)
    i1 = jnp.min(jnp.where(el == v1, lane, big), axis=-1, keepdims=True)
    el2 = jnp.where(lane == i1, NEG_INF, el)
    v2 = jnp.max(el2, axis=-1, keepdims=True)
    i2 = jnp.min(jnp.where(el2 == v2, lane, big), axis=-1, keepdims=True)
    t = jnp.exp(v2 - v1)
    w1 = g_gate / (1.0 + t)
    w2 = g_gate * t / (1.0 + t)
    eid = jnp.where(lane == 0, i1 - N_GROUPS, jnp.where(lane == 1, i2 - N_GROUPS, 0))
    gate = jnp.where(lane == 0, w1, jnp.where(lane == 1, w2, 0.0))
    return eid, gate


def _proj_router_body(*refs, n_in, has_bias, precise, tm):
    a_refs = refs[:n_in]
    w_refs = refs[n_in:2 * n_in]
    k = 2 * n_in
    bias_ref = refs[k] if has_bias else None
    k += 1 if has_bias else 0
    x_ref, g_ref, wr_ref, br_ref, x1_ref, h8_ref, eid_ref, gate_ref = refs[k:]
    acc = x_ref[...]
    if has_bias:
        acc = acc + bias_ref[...]
    for a_ref, w_ref in zip(a_refs, w_refs):
        acc = acc + _mm(a_ref[...], w_ref[...], precise)
    x1_ref[...] = acc
    h = _rms(acc, g_ref[...])
    _store_row_tiles(h8_ref, h, tm)
    logits = jnp.dot(h, wr_ref[...], precision=HI, preferred_element_type=F32) + br_ref[...]
    eid, gate = _route(logits)
    eid_ref[...] = eid
    gate_ref[...] = gate


def proj_router(a_list, w_list, bias, x, g, wr, br, *, precise):
    rows, d = x.shape
    tm = min(rows, 128 if precise else 512)
    n_in = len(a_list)
    row_spec = lambda width: pl.BlockSpec((tm, width), lambda i: (i, 0))
    const = lambda shape: pl.BlockSpec(shape, lambda i: (0,) * len(shape))
    in_specs = [row_spec(a.shape[1]) for a in a_list] + [const(w.shape) for w in w_list]
    args = list(a_list) + list(w_list)
    if bias is not None:
        in_specs.append(const((1, d)))
        args.append(bias)
    in_specs += [row_spec(d), const((1, d)), const((d, LANES)), const((1, LANES))]
    args += [x, g, wr, br]
    return pl.pallas_call(
        functools.partial(_proj_router_body, n_in=n_in, has_bias=bias is not None, precise=precise, tm=tm),
        grid=(rows // tm,),
        in_specs=in_specs,
        out_specs=[row_spec(d), pl.BlockSpec((tm * ROW_TILES, LANES), lambda i: (i, 0)),
                   row_spec(LANES), row_spec(LANES)],
        out_shape=[jax.ShapeDtypeStruct((rows, d), F32), jax.ShapeDtypeStruct((rows * ROW_TILES, LANES), F32),
                   jax.ShapeDtypeStruct((rows, LANES), I32), jax.ShapeDtypeStruct((rows, LANES), F32)],
        compiler_params=_cparams("parallel"),
        name="proj_router",
    )(*args)


def _experts_body(be_ref, src_ref, srcn_ref, dst_ref, h8_hbm, wg_ref, wu_ref, wd_ref, yt_hbm,
                  xbuf, ybuf, gsem, ssem, *, nblk, precise):
    j = pl.program_id(0)
    slot = j % 2

    def tile_rows(r):
        return pl.ds(pl.multiple_of(r * ROW_TILES, ROW_TILES), ROW_TILES)

    def gather_row(idx_ref, buf_slot):
        def one(r, c):
            pltpu.make_async_copy(h8_hbm.at[tile_rows(idx_ref[0, 0, r])], xbuf.at[buf_slot, tile_rows(r)],
                                  gsem.at[buf_slot]).start()
            return c
        lax.fori_loop(0, EXPERT_BLOCK, one, 0)

    @pl.when(j == 0)
    def _():
        gather_row(src_ref, 0)

    @pl.when(j + 1 < nblk)
    def _():
        gather_row(srcn_ref, 1 - slot)

    pltpu.make_async_copy(xbuf.at[slot], xbuf.at[slot], gsem.at[slot]).wait()
    x = _load_row_tiles(xbuf.at[slot], EXPERT_BLOCK)
    gt = _mm(x, wg_ref[0], precise)
    up = _mm(x, wu_ref[0], precise)
    hid = gt * jax.nn.sigmoid(gt) * up
    y = _mm(hid, wd_ref[0], precise)

    @pl.when(j > 0)
    def _():
        pltpu.make_async_copy(ybuf, ybuf, ssem).wait()

    _store_row_tiles(ybuf, y, EXPERT_BLOCK)

    def scatter_one(r, c):
        pltpu.make_async_copy(ybuf.at[tile_rows(r)], yt_hbm.at[tile_rows(dst_ref[0, 0, r])], ssem).start()
        return c
    lax.fori_loop(0, EXPERT_BLOCK, scatter_one, 0)

    @pl.when(j == nblk - 1)
    def _():
        pltpu.make_async_copy(ybuf, ybuf, ssem).wait()


def experts(h8, block_e, slot_src, slot_dst, wg, wu, wd, *, precise):
    nblk = slot_src.shape[0]
    n_slots = nblk * EXPERT_BLOCK
    d, ff = wg.shape[1], wg.shape[2]
    idx_spec = lambda f: pl.BlockSpec((1, 1, EXPERT_BLOCK), f, memory_space=pltpu.SMEM)
    gs = pltpu.PrefetchScalarGridSpec(
        num_scalar_prefetch=1,
        grid=(nblk,),
        in_specs=[idx_spec(lambda j, be: (j, 0, 0)),
                  idx_spec(lambda j, be: (jnp.minimum(j + 1, nblk - 1), 0, 0)),
                  idx_spec(lambda j, be: (j, 0, 0)),
                  pl.BlockSpec(memory_space=pl.ANY),
                  pl.BlockSpec((1, d, ff), lambda j, be: (be[j], 0, 0)),
                  pl.BlockSpec((1, d, ff), lambda j, be: (be[j], 0, 0)),
                  pl.BlockSpec((1, ff, d), lambda j, be: (be[j], 0, 0))],
        out_specs=pl.BlockSpec(memory_space=pl.ANY),
        scratch_shapes=[pltpu.VMEM((2, EXPERT_BLOCK * ROW_TILES, LANES), F32),
                        pltpu.VMEM((EXPERT_BLOCK * ROW_TILES, LANES), F32),
                        pltpu.SemaphoreType.DMA((2,)), pltpu.SemaphoreType.DMA(())],
    )
    return pl.pallas_call(
        functools.partial(_experts_body, nblk=nblk, precise=precise),
        grid_spec=gs,
        out_shape=jax.ShapeDtypeStruct((n_slots * ROW_TILES, LANES), F32),
        compiler_params=_cparams("arbitrary"),
        name="experts",
    )(block_e, slot_src, slot_src, slot_dst, h8, wg, wu, wd)


def moe_plan(eid):
    n_tok = eid.shape[0]
    n_assign = n_tok * TOP_K
    nblk = (n_assign + N_EXPERTS * (EXPERT_BLOCK - 1)) // EXPERT_BLOCK
    n_slots = nblk * EXPERT_BLOCK
    e_flat = eid.reshape(-1)
    order = jnp.argsort(e_flat).astype(I32)
    counts = jnp.sum((e_flat[:, None] == jnp.arange(N_EXPERTS, dtype=I32)[None, :]).astype(I32), axis=0)
    starts = jnp.cumsum(counts) - counts
    padded = (counts + EXPERT_BLOCK - 1) // EXPERT_BLOCK * EXPERT_BLOCK
    pends = jnp.cumsum(padded)
    pstarts = pends - padded
    block_e = jnp.minimum(jnp.searchsorted(pends, jnp.arange(nblk, dtype=I32) * EXPERT_BLOCK, side='right'),
                          N_EXPERTS - 1).astype(I32)
    slot = jnp.arange(n_slots, dtype=I32)
    e_slot = jnp.repeat(block_e, EXPERT_BLOCK)
    rank = slot - pstarts[e_slot]
    valid = rank < counts[e_slot]
    a = order[jnp.clip(starts[e_slot] + rank, 0, n_assign - 1)]
    spare = n_assign + jnp.cumsum((~valid).astype(I32)) - 1
    slot_src = jnp.where(valid, a // TOP_K, 0).astype(I32)
    slot_dst = jnp.where(valid, a, spare).astype(I32)
    return block_e, slot_src.reshape(nblk, 1, EXPERT_BLOCK), slot_dst.reshape(nblk, 1, EXPERT_BLOCK)


def _combine(x_ref, yt_ref, gate_ref, tm):
    gate = gate_ref[...]
    y0 = _load_row_tiles(yt_ref, tm, 0, TOP_K * ROW_TILES)
    y1 = _load_row_tiles(yt_ref, tm, ROW_TILES, TOP_K * ROW_TILES)
    return x_ref[...] + (y0 * gate[:, 0:1] + y1 * gate[:, 1:2])


def _combine_glu_body(x_ref, yt_ref, gate_ref, g_ref, w_ref, b_ref, x2_ref, u_ref, *, tm, precise):
    x2 = _combine(x_ref, yt_ref, gate_ref, tm)
    x2_ref[...] = x2
    zz = _mm(_rms(x2, g_ref[...]), w_ref[...], precise) + b_ref[...]
    half = zz.shape[1] // 2
    u_ref[...] = zz[:, :half] * jax.nn.sigmoid(zz[:, half:])


def _combine_final_body(x_ref, yt_ref, gate_ref, g_ref, o_ref, *, tm):
    o_ref[...] = _rms(_combine(x_ref, yt_ref, gate_ref, tm), g_ref[...])


def combine_glu(x, yt, gate, g, w, b, *, precise):
    rows, d = x.shape
    tm = min(rows, 128 if precise else 256)
    cols = w.shape[1]
    row_spec = lambda width: pl.BlockSpec((tm, width), lambda i: (i, 0))
    const = lambda shape: pl.BlockSpec(shape, lambda i: (0,) * len(shape))
    return pl.pallas_call(
        functools.partial(_combine_glu_body, tm=tm, precise=precise),
        grid=(rows // tm,),
        in_specs=[row_spec(d), pl.BlockSpec((tm * TOP_K * ROW_TILES, LANES), lambda i: (i, 0)), row_spec(LANES),
                  const((1, d)), const((d, cols)), const((1, cols))],
        out_specs=[row_spec(d), row_spec(cols // 2)],
        out_shape=[jax.ShapeDtypeStruct((rows, d), F32), jax.ShapeDtypeStruct((rows, cols // 2), F32)],
        compiler_params=_cparams("parallel"),
        name="combine_glu",
    )(x, yt, gate, g, w, b)


def combine_final(x, yt, gate, g):
    rows, d = x.shape
    tm = min(rows, 256)
    row_spec = lambda width: pl.BlockSpec((tm, width), lambda i: (i, 0))
    return pl.pallas_call(
        functools.partial(_combine_final_body, tm=tm),
        grid=(rows // tm,),
        in_specs=[row_spec(d), pl.BlockSpec((tm * TOP_K * ROW_TILES, LANES), lambda i: (i, 0)), row_spec(LANES),
                  pl.BlockSpec((1, d), lambda i: (0, 0))],
        out_specs=row_spec(d),
        out_shape=jax.ShapeDtypeStruct((rows, d), F32),
        compiler_params=_cparams("parallel"),
        name="combine_final",
    )(x, yt, gate, g)


CONV_TILE = 256
CONV_HIST = 32


def _ln_swish(y, g, b):
    yc = y - jnp.mean(y, axis=-1, keepdims=True)
    yn = yc * lax.rsqrt(jnp.mean(yc * yc, axis=-1, keepdims=True) + EPS) * g + b
    return yn * jax.nn.sigmoid(yn)


def _dwconv_prompt_body(u_ref, w_ref, bdw_ref, g_ref, b_ref, o_ref, hist_sc, *, tt):
    t = pl.program_id(1)

    @pl.when(t == 0)
    def _():
        hist_sc[...] = jnp.zeros(hist_sc.shape, F32)

    cur = u_ref[...]
    ext = jnp.concatenate([hist_sc[...], cur], axis=0)
    w = w_ref[...]
    off = CONV_HIST - (C_KERNEL - 1)
    acc = ext[off:off + tt] * w[0:1]
    for j in range(1, C_KERNEL):
        acc = acc + ext[off + j:off + j + tt] * w[j:j + 1]
    hist_sc[...] = cur[tt - CONV_HIST:tt]
    o_ref[...] = _ln_swish(acc + bdw_ref[...], g_ref[...], b_ref[...]).astype(o_ref.dtype)


def dwconv_prompt(u, w, b_dw, ln_g, ln_b, bsz, seq):
    tt = min(CONV_TILE, seq)
    nt = seq // tt
    d = u.shape[1]
    const = lambda shape: pl.BlockSpec(shape, lambda b, t: (0,) * len(shape))
    return pl.pallas_call(
        functools.partial(_dwconv_prompt_body, tt=tt),
        grid=(bsz, nt),
        in_specs=[pl.BlockSpec((tt, d), lambda b, t: (b * nt + t, 0)), const((C_KERNEL, d)), const((1, d)),
                  const((1, d)), const((1, d))],
        out_specs=pl.BlockSpec((tt, d), lambda b, t: (b * nt + t, 0)),
        out_shape=jax.ShapeDtypeStruct((bsz * seq, d), BF16),
        scratch_shapes=[pltpu.VMEM((CONV_HIST, d), F32)],
        compiler_params=_cparams("arbitrary", "arbitrary"),
        name="dwconv_prompt",
    )(u, w, b_dw, ln_g, ln_b)


def _dwconv_sample_body(u_ref, buf_ref, w_ref, bdw_ref, g_ref, b_ref, o_ref, nbuf_ref, *, tb):
    w = w_ref[...]
    rows = []
    for i in range(tb):
        hist = buf_ref[i]
        ur = u_ref[i:i + 1, :]
        rows.append(jnp.sum(hist * w[:C_KERNEL - 1], axis=0, keepdims=True) + ur * w[C_KERNEL - 1:C_KERNEL])
        nbuf_ref[i] = jnp.concatenate([hist[1:], ur], axis=0)
    y = jnp.concatenate(rows, axis=0) + bdw_ref[...]
    o_ref[...] = _ln_swish(y, g_ref[...], b_ref[...])


def dwconv_sample(u, buf, w, b_dw, ln_g, ln_b):
    nb, d = u.shape
    tb = min(SAMPLE_TILE, nb)
    const = lambda shape: pl.BlockSpec(shape, lambda i: (0,) * len(shape))
    return pl.pallas_call(
        functools.partial(_dwconv_sample_body, tb=tb),
        grid=(nb // tb,),
        in_specs=[pl.BlockSpec((tb, d), lambda i: (i, 0)), pl.BlockSpec((tb, C_KERNEL - 1, d), lambda i: (i, 0, 0)),
                  const((C_KERNEL, d)), const((1, d)), const((1, d)), const((1, d))],
        out_specs=[pl.BlockSpec((tb, d), lambda i: (i, 0)), pl.BlockSpec((tb, C_KERNEL - 1, d), lambda i: (i, 0, 0))],
        out_shape=[jax.ShapeDtypeStruct((nb, d), F32), jax.ShapeDtypeStruct(buf.shape, F32)],
        compiler_params=_cparams("parallel"),
        name="dwconv_sample",
    )(u, buf, w, b_dw, ln_g, ln_b)


def _moe(h8, eid128, wg, wu, wd, *, precise):
    block_e, slot_src, slot_dst = moe_plan(eid128[:, :TOP_K])
    return experts(h8, block_e, slot_src, slot_dst, wg, wu, wd, precise=precise)


def _trunk(x, caches, p, *, prompt):
    bsz, seq, d = x.shape
    rows = bsz * seq
    precise = not prompt
    wdt = F32 if precise else BF16
    xf = x.reshape(rows, d)
    row = lambda v: v.reshape(1, -1).astype(F32)

    z = norm_proj(xf, row(p['norm_mix'][0]), p['w_in'].astype(wdt), precise=precise)
    if prompt:
        att = attn_prompt(z, p['rel_table'], p['sinks'], bsz, seq)
        out_b, c1, n1, m1 = mlstm_prompt(z, p['conv_w'], p['b_gates'], p['g_mnorm'], bsz, seq)
        z3 = z.reshape(bsz, seq, Z_COLS)
        new_k = z3[:, seq - WINDOW:, Z_KA:Z_KA + A_KV].reshape(bsz, WINDOW, A_KV_HEADS, A_HEAD_DIM)
        new_v = z3[:, seq - WINDOW:, Z_VA:Z_VA + A_KV].reshape(bsz, WINDOW, A_KV_HEADS, A_HEAD_DIM)
        new_conv = z3[:, seq - (B_CONV - 1):, Z_QK:Z_QK + 2 * B_QK]
        n1 = n1[:, :B_HEADS]
        m1 = m1[:, :B_HEADS, 0]
    else:
        ck, cv, c0, n0, m0, cbuf = caches[:6]
        n_buf = ck.shape[1]
        att, out_b, new_k, new_v, c1, n1, m1, new_conv = mix_sample(
            z, p['rel_table'], p['sinks'], ck.reshape(bsz, n_buf, A_KV), cv.reshape(bsz, n_buf, A_KV),
            c0, n0, m0, cbuf, p['conv_w'], p['b_gates'], p['g_mnorm'])
        new_k = new_k.reshape(bsz, n_buf, A_KV_HEADS, A_HEAD_DIM)
        new_v = new_v.reshape(bsz, n_buf, A_KV_HEADS, A_HEAD_DIM)
        m1 = m1[:, :B_HEADS]
    w_out = p['w_out'].astype(wdt)
    x1, h8, eid, gate = proj_router([att, out_b], [w_out[:A_Q], w_out[A_Q:]], None, xf,
                                    row(p['norm_ffn'][0]), p['w_router'][0], p['b_router'][0], precise=precise)
    yt = _moe(h8, eid, p['w_eg'][0].astype(wdt), p['w_eu'][0].astype(wdt), p['w_ed'][0].astype(wdt),
              precise=precise)

    x2, u = combine_glu(x1, yt, gate, row(p['norm_mix'][1]), p['w_pw1'].astype(wdt), row(p['b_pw1']),
                        precise=precise)
    if prompt:
        yc = dwconv_prompt(u, p['w_dw'], row(p['b_dw']), row(p['ln_g']), row(p['ln_b']), bsz, seq)
        new_cbuf = u.reshape(bsz, seq, d)[:, seq - (C_KERNEL - 1):]
    else:
        yc, new_cbuf = dwconv_sample(u, caches[6], p['w_dw'], row(p['b_dw']), row(p['ln_g']), row(p['ln_b']))
    x3, h8, eid, gate = proj_router([yc], [p['w_pw2'].astype(wdt)], row(p['b_pw2']), x2,
                                    row(p['norm_ffn'][1]), p['w_router'][1], p['b_router'][1], precise=precise)
    yt = _moe(h8, eid, p['w_eg'][1].astype(wdt), p['w_eu'][1].astype(wdt), p['w_ed'][1].astype(wdt),
              precise=precise)
    y = combine_final(x3, yt, gate, row(p['norm_final']))
    add_layer = lambda t: t[None]
    return (y.reshape(bsz, seq, d),) + tuple(add_layer(t) for t in (new_k, new_v, c1, n1, m1, new_conv, new_cbuf))


def kernel(x_prompt, x_sample, cache_win_k, cache_win_v, state_mlstm_c, state_mlstm_n, state_mlstm_m, state_mlstm_conv, state_conv, norm_mix, norm_ffn, norm_final, rel_bias_table, w_in_mix, b_mlstm_gates, w_mlstm_qk_conv, attn_sinks, g_mlstm_norm, w_out_mix, w_pw1, b_pw1, w_dw, b_dw, ln_conv_g, ln_conv_b, w_pw2, b_pw2, w_router_group, b_router_group, w_router_expert, b_router_expert, w_expert_gate, w_expert_up, w_expert_down):
    w_in = w_in_mix[0]
    s_q, s_k, s_v, s_qk, s_vb, s_g = A_Q, A_Q + A_KV, A_Q + 2 * A_KV, A_Q + 2 * A_KV + 2 * B_QK, \
        A_Q + 2 * A_KV + 2 * B_QK + B_V, A_Q + 2 * A_KV + 2 * B_QK + B_V + 2 * B_HEADS
    w_in_r = jnp.concatenate([w_in[:, :s_q], w_in[:, s_v:s_qk], w_in[:, s_qk:s_vb], w_in[:, s_g:],
                              w_in[:, s_q:s_k], w_in[:, s_k:s_v], w_in[:, s_vb:s_g],
                              jnp.zeros((D_MODEL, LANES - 2 * B_HEADS), F32)], axis=1)
    b_gates = jnp.concatenate([b_mlstm_gates[0], jnp.zeros((LANES - 2 * B_HEADS,), F32)]).reshape(1, LANES)
    depth = w_router_group.shape[0]
    w_re = jnp.transpose(w_router_expert, (0, 2, 1, 3)).reshape(depth, D_MODEL, N_EXPERTS)
    w_router = jnp.concatenate([w_router_group, w_re,
                                jnp.zeros((depth, D_MODEL, LANES - N_GROUPS - N_EXPERTS), F32)], axis=-1)
    b_router = jnp.concatenate([b_router_group, b_router_expert.reshape(depth, N_EXPERTS),
                                jnp.zeros((depth, LANES - N_GROUPS - N_EXPERTS), F32)], axis=-1)[:, None, :]
    p = dict(norm_mix=norm_mix, norm_ffn=norm_ffn, norm_final=norm_final, rel_table=rel_bias_table,
             sinks=attn_sinks[0], w_in=w_in_r, b_gates=b_gates, conv_w=w_mlstm_qk_conv[0],
             g_mnorm=g_mlstm_norm[0].reshape(1, B_V), w_out=w_out_mix[0], w_pw1=w_pw1[0], b_pw1=b_pw1[0],
             w_dw=w_dw[0], b_dw=b_dw[0], ln_g=ln_conv_g[0], ln_b=ln_conv_b[0], w_pw2=w_pw2[0], b_pw2=b_pw2[0],
             w_router=w_router, b_router=b_router, w_eg=w_expert_gate, w_eu=w_expert_up, w_ed=w_expert_down)
    caches = (cache_win_k[0], cache_win_v[0], state_mlstm_c[0], state_mlstm_n[0], state_mlstm_m[0],
              state_mlstm_conv[0], state_conv[0])
    out_p = _trunk(x_prompt, None, p, prompt=True)
    out_s = _trunk(x_sample, caches, p, prompt=False)
    return (out_p[0], out_s[0]) + out_p[1:] + out_s[1:]
```

```python
import functools
import math

import numpy as np
import jax
import jax.numpy as jnp
from jax import lax
from jax.experimental import pallas as pl
from jax.experimental.pallas import tpu as pltpu

F32 = jnp.float32
BF16 = jnp.bfloat16
I32 = jnp.int32
HI = lax.Precision.HIGHEST
NEG_INF = float("-inf")

LANES = 128
SUBLANES = 8
VMEM_LIMIT = 56 * 1024 * 1024

D_MODEL = 1024
A_HEADS = 8
A_KV_HEADS = 2
A_GROUP = A_HEADS // A_KV_HEADS
A_HEAD_DIM = 64
WINDOW = 128
REL_BUCKETS = 32
REL_MAX_DIST = 128
B_HEADS = 4
B_DK = 64
B_DV = 128
B_CONV = 4
C_KERNEL = 31
N_GROUPS = 4
EXPERTS_PER_GROUP = 8
N_EXPERTS = N_GROUPS * EXPERTS_PER_GROUP
TOP_K = 2
EXPERT_FF = D_MODEL // 2
EXPERT_BLOCK = 128
EPS = 1e-6

A_Q = A_HEADS * A_HEAD_DIM
A_KV = A_KV_HEADS * A_HEAD_DIM
B_QK = B_HEADS * B_DK
B_V = B_HEADS * B_DV
ROW_TILES = D_MODEL // LANES

Z_QA, Z_QK, Z_VB, Z_OG, Z_KA, Z_VA, Z_GATES = 0, 512, 1024, 1536, 2048, 2176, 2304
Z_COLS = 2432
MLSTM_CHUNK = 128


def _cparams(*sem):
    return pltpu.CompilerParams(dimension_semantics=sem, vmem_limit_bytes=VMEM_LIMIT)


def _rms(x, g):
    return x * lax.rsqrt(jnp.mean(x * x, axis=-1, keepdims=True) + EPS) * g


def _mm(a, w, precise):
    if precise:
        return jnp.dot(a.astype(F32), w, precision=HI, preferred_element_type=F32)
    return jnp.dot(a.astype(BF16), w, preferred_element_type=F32)


def _t5_buckets(dist):
    exact = REL_BUCKETS // 2
    d = np.maximum(dist, 0)
    large = exact + (np.log(np.maximum(d, 1).astype(np.float32) / exact)
                     / math.log(REL_MAX_DIST / exact) * (REL_BUCKETS - exact)).astype(np.int32)
    return np.where(d < exact, d, np.minimum(large, REL_BUCKETS - 1)).astype(np.int32)


def _norm_proj_body(x_ref, g_ref, w_ref, o_ref, *, precise):
    h = _rms(x_ref[...], g_ref[...])
    o_ref[...] = _mm(h, w_ref[...], precise)


def norm_proj(x, g, w, *, precise):
    rows, d = x.shape
    cols = w.shape[1]
    tm = min(rows, 128 if precise else 512)
    return pl.pallas_call(
        functools.partial(_norm_proj_body, precise=precise),
        grid=(rows // tm,),
        in_specs=[pl.BlockSpec((tm, d), lambda i: (i, 0)),
                  pl.BlockSpec((1, d), lambda i: (0, 0)),
                  pl.BlockSpec((d, cols), lambda i: (0, 0))],
        out_specs=pl.BlockSpec((tm, cols), lambda i: (i, 0)),
        out_shape=jax.ShapeDtypeStruct((rows, cols), F32),
        compiler_params=_cparams("parallel"),
        name="norm_proj",
    )(x, g, w)


def _attn_prompt_body(tab_ref, sink_ref, bkt_ref, q_ref, kp_ref, kc_ref, vp_ref, vc_ref, o_ref, bias_ref):
    b = pl.program_id(0)
    n = pl.program_id(1)

    @pl.when((b == 0) & (n == 0))
    def _():
        bk = bkt_ref[...]
        for h in range(A_HEADS):
            acc = jnp.full(bk.shape, NEG_INF, F32)
            for t in range(REL_BUCKETS):
                acc = jnp.where(bk == t, tab_ref[t, h], acc)
            bias_ref[h] = acc

    q = q_ref[...] * (A_HEAD_DIM ** -0.5)
    kb = jnp.concatenate([kp_ref[...], kc_ref[...]], axis=0).astype(BF16)
    vb = jnp.concatenate([vp_ref[...], vc_ref[...]], axis=0).astype(BF16)
    col = lax.broadcasted_iota(I32, (WINDOW, 2 * WINDOW), 1)
    dead = col < jnp.where(n == 0, WINDOW, 0)
    outs = []
    for h in range(A_HEADS):
        kvh = h // A_GROUP
        qh = q[:, h * A_HEAD_DIM:(h + 1) * A_HEAD_DIM].astype(BF16)
        kh = kb[:, kvh * A_HEAD_DIM:(kvh + 1) * A_HEAD_DIM]
        vh = vb[:, kvh * A_HEAD_DIM:(kvh + 1) * A_HEAD_DIM]
        s = lax.dot_general(qh, kh, (((1,), (1,)), ((), ())), preferred_element_type=F32)
        s = jnp.where(dead, NEG_INF, s + bias_ref[h])
        sink = sink_ref[h]
        mx = jnp.maximum(jnp.max(s, axis=-1, keepdims=True), sink)
        p = jnp.exp(s - mx)
        den = jnp.sum(p, axis=-1, keepdims=True) + jnp.exp(sink - mx)
        outs.append(jnp.dot(p.astype(BF16), vh, preferred_element_type=F32) / den)
    o_ref[...] = jnp.concatenate(outs, axis=1).astype(o_ref.dtype)


def attn_prompt(z, rel_table, sinks, bsz, seq):
    nb = seq // WINDOW
    dist = WINDOW + np.arange(WINDOW)[:, None] - np.arange(2 * WINDOW)[None, :]
    bkt = np.where((dist >= 0) & (dist <= WINDOW), _t5_buckets(dist), -1).astype(np.int32)
    kcol, vcol = Z_KA // LANES, Z_VA // LANES
    smem = pl.BlockSpec(memory_space=pltpu.SMEM)

    def cur(c):
        return pl.BlockSpec((WINDOW, LANES), lambda b, n: (b * nb + n, c))

    def prev(c):
        return pl.BlockSpec((WINDOW, LANES), lambda b, n: (b * nb + jnp.maximum(n - 1, 0), c))

    return pl.pallas_call(
        _attn_prompt_body,
        grid=(bsz, nb),
        in_specs=[smem, smem,
                  pl.BlockSpec((WINDOW, 2 * WINDOW), lambda b, n: (0, 0)),
                  pl.BlockSpec((WINDOW, A_Q), lambda b, n: (b * nb + n, 0)),
                  prev(kcol), cur(kcol), prev(vcol), cur(vcol)],
        out_specs=pl.BlockSpec((WINDOW, A_Q), lambda b, n: (b * nb + n, 0)),
        out_shape=jax.ShapeDtypeStruct((bsz * seq, A_Q), BF16),
        scratch_shapes=[pltpu.VMEM((A_HEADS, WINDOW, 2 * WINDOW), F32)],
        compiler_params=_cparams("arbitrary", "arbitrary"),
        name="attn_prompt",
    )(rel_table, sinks, jnp.asarray(bkt), z, z, z, z, z)


def _log_sigmoid(x):
    return -(jnp.maximum(-x, 0.0) + jnp.log1p(jnp.exp(-jnp.abs(x))))


def _mlstm_prompt_body(qk_ref, v_ref, g_ref, og_ref, cw_ref, bg_ref, gn_ref,
                       ob_ref, c_out, n_out, m_out,
                       c_sc, n_sc, m_sc, hist_sc, *, L, nc):
    ci = pl.program_id(1)

    @pl.when(ci == 0)
    def _():
        c_sc[...] = jnp.zeros(c_sc.shape, F32)
        n_sc[...] = jnp.zeros(n_sc.shape, F32)
        m_sc[...] = jnp.full(m_sc.shape, NEG_INF, F32)
        hist_sc[...] = jnp.zeros(hist_sc.shape, F32)

    cur = qk_ref[...]
    ext = jnp.concatenate([hist_sc[...], cur], axis=0)
    cw = cw_ref[...]
    off = SUBLANES - (B_CONV - 1)
    conv = ext[off:off + L] * cw[0:1]
    for j in range(1, B_CONV):
        conv = conv + ext[off + j:off + j + L] * cw[j:j + 1]
    hist_sc[...] = cur[L - SUBLANES:L]
    qk = conv * jax.nn.sigmoid(conv)
    q_all = qk[:, :B_QK]
    k_all = qk[:, B_QK:] * (B_DK ** -0.5)
    v_all = v_ref[...]
    og = og_ref[...]
    gn = gn_ref[...]

    G = g_ref[...] + bg_ref[...]
    lf = _log_sigmoid(G)
    row = lax.broadcasted_iota(I32, (L, L), 0)
    colm = lax.broadcasted_iota(I32, (L, L), 1)
    causal = colm <= row
    Bc = jnp.dot(causal.astype(F32), lf, precision=HI, preferred_element_type=F32)
    BT = Bc.T
    GT = G.T

    outs = []
    for h in range(B_HEADS):
        qh = q_all[:, h * B_DK:(h + 1) * B_DK]
        kh = k_all[:, h * B_DK:(h + 1) * B_DK]
        vh = v_all[:, h * B_DV:(h + 1) * B_DV]
        b_col = Bc[:, B_HEADS + h:B_HEADS + h + 1]
        ig_col = G[:, h:h + 1]
        b_row = BT[B_HEADS + h:B_HEADS + h + 1, :]
        ig_row = GT[h:h + 1, :]
        c0 = c_sc[h]
        n0 = n_sc[h:h + 1, :]
        m0 = m_sc[h:h + 1, 0:1]
        a = b_col + m0
        d = jnp.where(causal, b_col - b_row + ig_row, NEG_INF)
        m = jnp.maximum(a, jnp.max(d, axis=-1, keepdims=True))
        dw = jnp.exp(d - m)
        aw = jnp.exp(a - m)
        qb = qh.astype(BF16)
        vb = vh.astype(BF16)
        s = lax.dot_general(qb, kh.astype(BF16), (((1,), (1,)), ((), ())), preferred_element_type=F32) * dw
        num = (jnp.dot(s.astype(BF16), vb, preferred_element_type=F32)
               + aw * jnp.dot(qb, c0.astype(BF16), preferred_element_type=F32))
        den = jnp.sum(s, axis=-1, keepdims=True) + aw * jnp.sum(qh * n0, axis=-1, keepdims=True)
        hh = num / jnp.maximum(jnp.abs(den), jnp.exp(-m))
        m_last = m[L - 1:L, :]
        wl = jnp.exp(b_col[L - 1:L, :] - b_col + ig_col - m_last)
        decay = aw[L - 1:L, :]
        kw = kh * wl
        c_sc[h] = decay * c0 + lax.dot_general(kw.astype(BF16), vb, (((0,), (0,)), ((), ())),
                                               preferred_element_type=F32)
        n_sc[h:h + 1, :] = decay * n0 + jnp.sum(kw, axis=0, keepdims=True)
        m_sc[h:h + 1, :] = jnp.broadcast_to(m_last, (1, LANES))
        hn = hh * lax.rsqrt(jnp.mean(hh * hh, axis=-1, keepdims=True) + EPS) * gn[:, h * B_DV:(h + 1) * B_DV]
        outs.append(jax.nn.sigmoid(og[:, h * B_DV:(h + 1) * B_DV]) * hn)
    ob_ref[...] = jnp.concatenate(outs, axis=1).astype(ob_ref.dtype)

    @pl.when(ci == nc - 1)
    def _():
        c_out[0] = c_sc[...]
        n_out[0] = n_sc[...]
        m_out[0] = m_sc[...]


def mlstm_prompt(z, conv_w, b_gates_pad, g_norm, bsz, seq):
    L = MLSTM_CHUNK
    nc = seq // L

    def zspec(width, colblk):
        return pl.BlockSpec((L, width), lambda b, c: (b * nc + c, colblk))

    const = lambda shape: pl.BlockSpec(shape, lambda b, c: (0,) * len(shape))
    return pl.pallas_call(
        functools.partial(_mlstm_prompt_body, L=L, nc=nc),
        grid=(bsz, nc),
        in_specs=[zspec(2 * B_QK, Z_QK // (2 * B_QK)), zspec(B_V, Z_VB // B_V), zspec(LANES, Z_GATES // LANES),
                  zspec(B_V, Z_OG // B_V), const((B_CONV, 2 * B_QK)), const((1, LANES)), const((1, B_V))],
        out_specs=[pl.BlockSpec((L, B_V), lambda b, c: (b * nc + c, 0)),
                   pl.BlockSpec((1, B_HEADS, B_DK, B_DV), lambda b, c: (b, 0, 0, 0)),
                   pl.BlockSpec((1, SUBLANES, B_DK), lambda b, c: (b, 0, 0)),
                   pl.BlockSpec((1, SUBLANES, LANES), lambda b, c: (b, 0, 0))],
        out_shape=[jax.ShapeDtypeStruct((bsz * seq, B_V), BF16),
                   jax.ShapeDtypeStruct((bsz, B_HEADS, B_DK, B_DV), F32),
                   jax.ShapeDtypeStruct((bsz, SUBLANES, B_DK), F32),
                   jax.ShapeDtypeStruct((bsz, SUBLANES, LANES), F32)],
        scratch_shapes=[pltpu.VMEM((B_HEADS, B_DK, B_DV), F32), pltpu.VMEM((SUBLANES, B_DK), F32),
                        pltpu.VMEM((SUBLANES, LANES), F32), pltpu.VMEM((SUBLANES, 2 * B_QK), F32)],
        compiler_params=_cparams("arbitrary", "arbitrary"),
        name="mlstm_prompt",
    )(z, z, z, z, conv_w, b_gates_pad, g_norm)


SAMPLE_TILE = 8


def _row_to_col(row, n):
    eye = lax.broadcasted_iota(I32, (n, n), 0) == lax.broadcasted_iota(I32, (n, n), 1)
    return jnp.sum(jnp.where(eye, jnp.broadcast_to(row, (n, n)), 0.0), axis=1, keepdims=True)


def _mix_sample_body(tab_ref, sink_ref, bkt_ref, z_ref, ck_ref, cv_ref, c0_ref, n0_ref, m0_ref, cb_ref,
                     cw_ref, bg_ref, gn_ref,
                     att_ref, ob_ref, nk_ref, nv_ref, c1_ref, n1_ref, m1_ref, ncb_ref, *, tb):
    bk = bkt_ref[...]
    cw = cw_ref[...]
    gn = gn_ref[...]
    zero_half = jnp.zeros((1, A_HEAD_DIM), F32)
    bias_cols = []
    for h in range(A_HEADS):
        bias = jnp.zeros(bk.shape, F32)
        for t in range(REL_BUCKETS):
            bias = jnp.where(bk == t, tab_ref[t, h], bias)
        bias_cols.append(bias)
    att_rows, ob_rows, m_rows = [], [], []
    for i in range(tb):
        zr = z_ref[i:i + 1, :]
        q_att = zr[:, Z_QA:Z_QA + A_Q] * (A_HEAD_DIM ** -0.5)
        ka = zr[:, Z_KA:Z_KA + A_KV]
        va = zr[:, Z_VA:Z_VA + A_KV]
        kc = ck_ref[i]
        vc = cv_ref[i]
        nk_ref[i] = jnp.concatenate([kc[1:], ka], axis=0)
        nv_ref[i] = jnp.concatenate([vc[1:], va], axis=0)
        heads = []
        for h in range(A_HEADS):
            kvh = h // A_GROUP
            qh = q_att[:, h * A_HEAD_DIM:(h + 1) * A_HEAD_DIM]
            qrow = jnp.concatenate([qh, zero_half] if kvh == 0 else [zero_half, qh], axis=1)
            lc = jnp.sum(kc * qrow, axis=-1, keepdims=True) + bias_cols[h]
            ln = jnp.sum(ka * qrow, axis=-1, keepdims=True) + tab_ref[0, h]
            sink = sink_ref[h]
            mx = jnp.maximum(jnp.maximum(jnp.max(lc, axis=0, keepdims=True), ln), sink)
            pc = jnp.exp(lc - mx)
            pn = jnp.exp(ln - mx)
            den = jnp.sum(pc, axis=0, keepdims=True) + pn + jnp.exp(sink - mx)
            o = (jnp.sum(pc * vc, axis=0, keepdims=True) + pn * va) / den
            heads.append(o[:, kvh * A_HEAD_DIM:(kvh + 1) * A_HEAD_DIM])
        att_rows.append(jnp.concatenate(heads, axis=1))

        qk_pre = zr[:, Z_QK:Z_QK + 2 * B_QK]
        hist = cb_ref[i]
        conv = qk_pre * cw[B_CONV - 1:B_CONV]
        for j in range(B_CONV - 1):
            conv = conv + hist[j:j + 1] * cw[j:j + 1]
        ncb_ref[i] = jnp.concatenate([hist[1:], qk_pre], axis=0)
        qk = conv * jax.nn.sigmoid(conv)
        G = zr[:, Z_GATES:Z_GATES + LANES] + bg_ref[...]
        lfr = _log_sigmoid(G)
        og = zr[:, Z_OG:Z_OG + B_V]
        v_pre = zr[:, Z_VB:Z_VB + B_V]
        obs, ms = [], []
        for h in range(B_HEADS):
            qh = qk[:, h * B_DK:(h + 1) * B_DK]
            kh = qk[:, B_QK + h * B_DK:B_QK + (h + 1) * B_DK] * (B_DK ** -0.5)
            vh = v_pre[:, h * B_DV:(h + 1) * B_DV]
            ig = G[:, h:h + 1]
            lf = lfr[:, B_HEADS + h:B_HEADS + h + 1]
            c0 = c0_ref[i, h]
            n0 = n0_ref[i, h:h + 1, :]
            m0 = m0_ref[i:i + 1, h:h + 1]
            a = lf + m0
            m = jnp.maximum(a, ig)
            dw = jnp.exp(ig - m)
            aw = jnp.exp(a - m)
            s = jnp.sum(qh * kh, axis=-1, keepdims=True) * dw
            q_col = _row_to_col(qh, B_DK)
            k_col = _row_to_col(kh, B_DK)
            num = s * vh + aw * jnp.sum(q_col * c0, axis=0, keepdims=True)
            den = s + aw * jnp.sum(qh * n0, axis=-1, keepdims=True)
            hh = num / jnp.maximum(jnp.abs(den), jnp.exp(-m))
            c1_ref[i, h] = aw * c0 + dw * (k_col * vh)
            n1_ref[i, h:h + 1, :] = aw * n0 + dw * kh
            ms.append(m)
            hn = hh * lax.rsqrt(jnp.mean(hh * hh, axis=-1, keepdims=True) + EPS) * gn[:, h * B_DV:(h + 1) * B_DV]
            obs.append(jax.nn.sigmoid(og[:, h * B_DV:(h + 1) * B_DV]) * hn)
        ob_rows.append(jnp.concatenate(obs, axis=1))
        lane = lax.broadcasted_iota(I32, (1, LANES), 1)
        mrow = jnp.zeros((1, LANES), F32)
        for h in range(B_HEADS):
            mrow = jnp.where(lane == h, ms[h], mrow)
        m_rows.append(mrow)
    att_ref[...] = jnp.concatenate(att_rows, axis=0)
    ob_ref[...] = jnp.concatenate(ob_rows, axis=0)
    m1_ref[...] = jnp.concatenate(m_rows, axis=0)


def mix_sample(z, rel_table, sinks, ck, cv, c0, n0, m0, conv_buf, conv_w, b_gates_pad, g_norm):
    nb = z.shape[0]
    tb = min(SAMPLE_TILE, nb)
    n_buf = ck.shape[1]
    bkt = _t5_buckets(n_buf - np.arange(n_buf))[:, None]
    smem = pl.BlockSpec(memory_space=pltpu.SMEM)
    const = lambda shape: pl.BlockSpec(shape, lambda i: (0,) * len(shape))
    lead = lambda shape: pl.BlockSpec((tb,) + shape, lambda i: (i,) + (0,) * len(shape))
    return pl.pallas_call(
        functools.partial(_mix_sample_body, tb=tb),
        grid=(nb // tb,),
        in_specs=[smem, smem, const((n_buf, 1)), lead((Z_COLS,)), lead((n_buf, A_KV)), lead((n_buf, A_KV)),
                  lead((B_HEADS, B_DK, B_DV)), lead((B_HEADS, B_DK)), lead((B_HEADS,)),
                  lead((B_CONV - 1, 2 * B_QK)), const((B_CONV, 2 * B_QK)), const((1, LANES)), const((1, B_V))],
        out_specs=[lead((A_Q,)), lead((B_V,)), lead((n_buf, A_KV)), lead((n_buf, A_KV)),
                   lead((B_HEADS, B_DK, B_DV)), lead((B_HEADS, B_DK)), lead((LANES,)),
                   lead((B_CONV - 1, 2 * B_QK))],
        out_shape=[jax.ShapeDtypeStruct((nb, A_Q), F32), jax.ShapeDtypeStruct((nb, B_V), F32),
                   jax.ShapeDtypeStruct(ck.shape, F32), jax.ShapeDtypeStruct(cv.shape, F32),
                   jax.ShapeDtypeStruct(c0.shape, F32), jax.ShapeDtypeStruct(n0.shape, F32),
                   jax.ShapeDtypeStruct((nb, LANES), F32), jax.ShapeDtypeStruct(conv_buf.shape, F32)],
        compiler_params=_cparams("parallel"),
        name="mix_sample",
    )(rel_table, sinks, jnp.asarray(bkt), z, ck, cv, c0, n0, m0, conv_buf, conv_w, b_gates_pad, g_norm)


def _store_row_tiles(ref, val, rows):
    for s in range(ROW_TILES):
        ref[pl.ds(s, rows, stride=ROW_TILES), :] = val[:, s * LANES:(s + 1) * LANES]


def _load_row_tiles(ref, rows, start=0, stride=ROW_TILES):
    return jnp.concatenate([ref[pl.ds(start + s, rows, stride=stride), :] for s in range(ROW_TILES)], axis=1)


def _route(logits):
    lane = lax.broadcasted_iota(I32, logits.shape, 1)
    big = jnp.int32(1 << 20)
    gl = jnp.where(lane < N_GROUPS, logits, NEG_INF)
    gmax = jnp.max(gl, axis=-1, keepdims=True)
    gidx = jnp.min(jnp.where(gl == gmax, lane, big), axis=-1, keepdims=True)
    g_gate = 1.0 / jnp.sum(jnp.exp(gl - gmax), axis=-1, keepdims=True)
    lo = N_GROUPS + gidx * EXPERTS_PER_GROUP
    el = jnp.where((lane >= lo) & (lane < lo + EXPERTS_PER_GROUP), logits, NEG_INF)
    v1 = jnp.max(el, axis=-1, keepdims=True)
    i1 = jnp.min(jnp.where(el == v1, lane, big), axis=-1, keepdims=True)
    el2 = jnp.where(lane == i1, NEG_INF, el)
    v2 = jnp.max(el2, axis=-1, keepdims=True)
    i2 = jnp.min(jnp.where(el2 == v2, lane, big), axis=-1, keepdims=True)
    t = jnp.exp(v2 - v1)
    w1 = g_gate / (1.0 + t)
    w2 = g_gate * t / (1.0 + t)
    eid = jnp.where(lane == 0, i1 - N_GROUPS, jnp.where(lane == 1, i2 - N_GROUPS, 0))
    gate = jnp.where(lane == 0, w1, jnp.where(lane == 1, w2, 0.0))
    return eid, gate


def _proj_router_body(*refs, n_in, has_bias, precise, tm):
    a_refs = refs[:n_in]
    w_refs = refs[n_in:2 * n_in]
    k = 2 * n_in
    bias_ref = refs[k] if has_bias else None
    k += 1 if has_bias else 0
    x_ref, g_ref, wr_ref, br_ref, x1_ref, h8_ref, eid_ref, gate_ref = refs[k:]
    acc = x_ref[...]
    if has_bias:
        acc = acc + bias_ref[...]
    for a_ref, w_ref in zip(a_refs, w_refs):
        acc = acc + _mm(a_ref[...], w_ref[...], precise)
    x1_ref[...] = acc
    h = _rms(acc, g_ref[...])
    _store_row_tiles(h8_ref, h, tm)
    logits = jnp.dot(h, wr_ref[...], precision=HI, preferred_element_type=F32) + br_ref[...]
    eid, gate = _route(logits)
    eid_ref[...] = eid
    gate_ref[...] = gate


def proj_router(a_list, w_list, bias, x, g, wr, br, *, precise):
    rows, d = x.shape
    tm = min(rows, 128 if precise else 512)
    n_in = len(a_list)
    row_spec = lambda width: pl.BlockSpec((tm, width), lambda i: (i, 0))
    const = lambda shape: pl.BlockSpec(shape, lambda i: (0,) * len(shape))
    in_specs = [row_spec(a.shape[1]) for a in a_list] + [const(w.shape) for w in w_list]
    args = list(a_list) + list(w_list)
    if bias is not None:
        in_specs.append(const((1, d)))
        args.append(bias)
    in_specs += [row_spec(d), const((1, d)), const((d, LANES)), const((1, LANES))]
    args += [x, g, wr, br]
    return pl.pallas_call(
        functools.partial(_proj_router_body, n_in=n_in, has_bias=bias is not None, precise=precise, tm=tm),
        grid=(rows // tm,),
        in_specs=in_specs,
        out_specs=[row_spec(d), pl.BlockSpec((tm * ROW_TILES, LANES), lambda i: (i, 0)),
                   row_spec(LANES), row_spec(LANES)],
        out_shape=[jax.ShapeDtypeStruct((rows, d), F32), jax.ShapeDtypeStruct((rows * ROW_TILES, LANES), F32),
                   jax.ShapeDtypeStruct((rows, LANES), I32), jax.ShapeDtypeStruct((rows, LANES), F32)],
        compiler_params=_cparams("parallel"),
        name="proj_router",
    )(*args)


def _experts_body(be_ref, src_ref, srcn_ref, dst_ref, h8_hbm, wg_ref, wu_ref, wd_ref, yt_hbm,
                  xbuf, ybuf, gsem, ssem, *wcast, nblk, precise):
    j = pl.program_id(0)
    slot = j % 2

    def tile_rows(r):
        return pl.ds(pl.multiple_of(r * ROW_TILES, ROW_TILES), ROW_TILES)

    def gather_row(idx_ref, buf_slot):
        def one(r, c):
            pltpu.make_async_copy(h8_hbm.at[tile_rows(idx_ref[0, 0, r])], xbuf.at[buf_slot, tile_rows(r)],
                                  gsem.at[buf_slot]).start()
            return c
        lax.fori_loop(0, EXPERT_BLOCK, one, 0)

    @pl.when(j == 0)
    def _():
        gather_row(src_ref, 0)

    @pl.when(j + 1 < nblk)
    def _():
        gather_row(srcn_ref, 1 - slot)

    pltpu.make_async_copy(xbuf.at[slot], xbuf.at[slot], gsem.at[slot]).wait()
    x = _load_row_tiles(xbuf.at[slot], EXPERT_BLOCK)
    if precise:
        wg, wu, wd = wg_ref[0], wu_ref[0], wd_ref[0]
    else:
        wgb, wub, wdb = wcast

        @pl.when((j == 0) | (be_ref[j] != be_ref[jnp.maximum(j - 1, 0)]))
        def _():
            wgb[...] = wg_ref[0].astype(BF16)
            wub[...] = wu_ref[0].astype(BF16)
            wdb[...] = wd_ref[0].astype(BF16)

        wg, wu, wd = wgb[...], wub[...], wdb[...]
    gt = _mm(x, wg, precise)
    up = _mm(x, wu, precise)
    hid = gt * jax.nn.sigmoid(gt) * up
    y = _mm(hid, wd, precise)

    @pl.when(j > 0)
    def _():
        pltpu.make_async_copy(ybuf, ybuf, ssem).wait()

    _store_row_tiles(ybuf, y, EXPERT_BLOCK)

    def scatter_one(r, c):
        pltpu.make_async_copy(ybuf.at[tile_rows(r)], yt_hbm.at[tile_rows(dst_ref[0, 0, r])], ssem).start()
        return c
    lax.fori_loop(0, EXPERT_BLOCK, scatter_one, 0)

    @pl.when(j == nblk - 1)
    def _():
        pltpu.make_async_copy(ybuf, ybuf, ssem).wait()


def experts(h8, block_e, slot_src, slot_dst, wg, wu, wd, layer, *, precise):
    nblk = slot_src.shape[0]
    n_slots = nblk * EXPERT_BLOCK
    d, ff = wg.shape[2], wg.shape[3]
    idx_spec = lambda f: pl.BlockSpec((1, 1, EXPERT_BLOCK), f, memory_space=pltpu.SMEM)
    gs = pltpu.PrefetchScalarGridSpec(
        num_scalar_prefetch=1,
        grid=(nblk,),
        in_specs=[idx_spec(lambda j, be: (j, 0, 0)),
                  idx_spec(lambda j, be: (jnp.minimum(j + 1, nblk - 1), 0, 0)),
                  idx_spec(lambda j, be: (j, 0, 0)),
                  pl.BlockSpec(memory_space=pl.ANY),
                  pl.BlockSpec((None, 1, d, ff), lambda j, be: (layer, be[j], 0, 0)),
                  pl.BlockSpec((None, 1, d, ff), lambda j, be: (layer, be[j], 0, 0)),
                  pl.BlockSpec((None, 1, ff, d), lambda j, be: (layer, be[j], 0, 0))],
        out_specs=pl.BlockSpec(memory_space=pl.ANY),
        scratch_shapes=[pltpu.VMEM((2, EXPERT_BLOCK * ROW_TILES, LANES), F32),
                        pltpu.VMEM((EXPERT_BLOCK * ROW_TILES, LANES), F32),
                        pltpu.SemaphoreType.DMA((2,)), pltpu.SemaphoreType.DMA(())]
        + ([] if precise else [pltpu.VMEM((d, ff), BF16), pltpu.VMEM((d, ff), BF16), pltpu.VMEM((ff, d), BF16)]),
    )
    return pl.pallas_call(
        functools.partial(_experts_body, nblk=nblk, precise=precise),
        grid_spec=gs,
        out_shape=jax.ShapeDtypeStruct((n_slots * ROW_TILES, LANES), F32),
        compiler_params=_cparams("arbitrary"),
        name="experts",
    )(block_e, slot_src, slot_src, slot_dst, h8, wg, wu, wd)


def moe_plan(eid):
    n_tok = eid.shape[0]
    n_assign = n_tok * TOP_K
    nblk = (n_assign + N_EXPERTS * (EXPERT_BLOCK - 1)) // EXPERT_BLOCK
    n_slots = nblk * EXPERT_BLOCK
    e_flat = eid.reshape(-1)
    order = jnp.argsort(e_flat).astype(I32)
    counts = jnp.sum((e_flat[:, None] == jnp.arange(N_EXPERTS, dtype=I32)[None, :]).astype(I32), axis=0)
    starts = jnp.cumsum(counts) - counts
    padded = (counts + EXPERT_BLOCK - 1) // EXPERT_BLOCK * EXPERT_BLOCK
    pends = jnp.cumsum(padded)
    pstarts = pends - padded
    blk_start = jnp.arange(nblk, dtype=I32) * EXPERT_BLOCK
    block_e = jnp.minimum(jnp.sum((pends[None, :] <= blk_start[:, None]).astype(I32), axis=1), N_EXPERTS - 1)
    slot = jnp.arange(n_slots, dtype=I32)
    e_slot = jnp.repeat(block_e, EXPERT_BLOCK)
    rank = slot - pstarts[e_slot]
    valid = rank < counts[e_slot]
    a = order[jnp.clip(starts[e_slot] + rank, 0, n_assign - 1)]
    spare = n_assign + jnp.cumsum((~valid).astype(I32)) - 1
    slot_src = jnp.where(valid, a // TOP_K, 0).astype(I32)
    slot_dst = jnp.where(valid, a, spare).astype(I32)
    return block_e, slot_src.reshape(nblk, 1, EXPERT_BLOCK), slot_dst.reshape(nblk, 1, EXPERT_BLOCK)


def _combine(x_ref, yt_ref, gate_ref, tm):
    gate = gate_ref[...]
    y0 = _load_row_tiles(yt_ref, tm, 0, TOP_K * ROW_TILES)
    y1 = _load_row_tiles(yt_ref, tm, ROW_TILES, TOP_K * ROW_TILES)
    return x_ref[...] + (y0 * gate[:, 0:1] + y1 * gate[:, 1:2])


def _combine_glu_body(x_ref, yt_ref, gate_ref, g_ref, w_ref, b_ref, x2_ref, u_ref, *, tm, precise):
    x2 = _combine(x_ref, yt_ref, gate_ref, tm)
    x2_ref[...] = x2
    zz = _mm(_rms(x2, g_ref[...]), w_ref[...], precise) + b_ref[...]
    half = zz.shape[1] // 2
    u_ref[...] = zz[:, :half] * jax.nn.sigmoid(zz[:, half:])


def _combine_final_body(x_ref, yt_ref, gate_ref, g_ref, o_ref, *, tm):
    o_ref[...] = _rms(_combine(x_ref, yt_ref, gate_ref, tm), g_ref[...])


def combine_glu(x, yt, gate, g, w, b, *, precise):
    rows, d = x.shape
    tm = min(rows, 128 if precise else 256)
    cols = w.shape[1]
    row_spec = lambda width: pl.BlockSpec((tm, width), lambda i: (i, 0))
    const = lambda shape: pl.BlockSpec(shape, lambda i: (0,) * len(shape))
    return pl.pallas_call(
        functools.partial(_combine_glu_body, tm=tm, precise=precise),
        grid=(rows // tm,),
        in_specs=[row_spec(d), pl.BlockSpec((tm * TOP_K * ROW_TILES, LANES), lambda i: (i, 0)), row_spec(LANES),
                  const((1, d)), const((d, cols)), const((1, cols))],
        out_specs=[row_spec(d), row_spec(cols // 2)],
        out_shape=[jax.ShapeDtypeStruct((rows, d), F32), jax.ShapeDtypeStruct((rows, cols // 2), F32)],
        compiler_params=_cparams("parallel"),
        name="combine_glu",
    )(x, yt, gate, g, w, b)


def combine_final(x, yt, gate, g):
    rows, d = x.shape
    tm = min(rows, 256)
    row_spec = lambda width: pl.BlockSpec((tm, width), lambda i: (i, 0))
    return pl.pallas_call(
        functools.partial(_combine_final_body, tm=tm),
        grid=(rows // tm,),
        in_specs=[row_spec(d), pl.BlockSpec((tm * TOP_K * ROW_TILES, LANES), lambda i: (i, 0)), row_spec(LANES),
                  pl.BlockSpec((1, d), lambda i: (0, 0))],
        out_specs=row_spec(d),
        out_shape=jax.ShapeDtypeStruct((rows, d), F32),
        compiler_params=_cparams("parallel"),
        name="combine_final",
    )(x, yt, gate, g)


CONV_TILE = 256
CONV_HIST = 32


def _ln_swish(y, g, b):
    yc = y - jnp.mean(y, axis=-1, keepdims=True)
    yn = yc * lax.rsqrt(jnp.mean(yc * yc, axis=-1, keepdims=True) + EPS) * g + b
    return yn * jax.nn.sigmoid(yn)


CONV_ROWS = 64


def _dwconv_prompt_body(u_ref, w_ref, bdw_ref, g_ref, b_ref, o_ref, ext, y_sc, *, tt):
    t = pl.program_id(1)
    n_lt = ext.shape[0]

    @pl.when(t == 0)
    def _():
        ext[:, 0:CONV_HIST, :] = jnp.zeros((n_lt, CONV_HIST, LANES), F32)

    @pl.when(t > 0)
    def _():
        ext[:, 0:CONV_HIST, :] = ext[:, tt:tt + CONV_HIST, :]

    for j in range(n_lt):
        ext[j, CONV_HIST:CONV_HIST + tt, :] = u_ref[:, j * LANES:(j + 1) * LANES]
    off = CONV_HIST - (C_KERNEL - 1)
    for j in range(n_lt):
        wj = w_ref[:, j * LANES:(j + 1) * LANES]
        bj = bdw_ref[:, j * LANES:(j + 1) * LANES]
        for c in range(tt // CONV_ROWS):
            acc = ext[j, pl.ds(off + c * CONV_ROWS, CONV_ROWS), :] * wj[0:1] + bj
            for k in range(1, C_KERNEL):
                acc = acc + ext[j, pl.ds(off + k + c * CONV_ROWS, CONV_ROWS), :] * wj[k:k + 1]
            y_sc[c * CONV_ROWS:(c + 1) * CONV_ROWS, j * LANES:(j + 1) * LANES] = acc
    o_ref[...] = _ln_swish(y_sc[...], g_ref[...], b_ref[...]).astype(o_ref.dtype)


def dwconv_prompt(u, w, b_dw, ln_g, ln_b, bsz, seq):
    tt = min(CONV_TILE, seq)
    nt = seq // tt
    d = u.shape[1]
    const = lambda shape: pl.BlockSpec(shape, lambda b, t: (0,) * len(shape))
    return pl.pallas_call(
        functools.partial(_dwconv_prompt_body, tt=tt),
        grid=(bsz, nt),
        in_specs=[pl.BlockSpec((tt, d), lambda b, t: (b * nt + t, 0)), const((C_KERNEL, d)), const((1, d)),
                  const((1, d)), const((1, d))],
        out_specs=pl.BlockSpec((tt, d), lambda b, t: (b * nt + t, 0)),
        out_shape=jax.ShapeDtypeStruct((bsz * seq, d), BF16),
        scratch_shapes=[pltpu.VMEM((d // LANES, CONV_HIST + tt, LANES), F32), pltpu.VMEM((tt, d), F32)],
        compiler_params=_cparams("arbitrary", "arbitrary"),
        name="dwconv_prompt",
    )(u, w, b_dw, ln_g, ln_b)


def _dwconv_sample_body(u_ref, buf_ref, w_ref, bdw_ref, g_ref, b_ref, o_ref, nbuf_ref, *, tb):
    w = w_ref[...]
    rows = []
    for i in range(tb):
        hist = buf_ref[i]
        ur = u_ref[i:i + 1, :]
        rows.append(jnp.sum(hist * w[:C_KERNEL - 1], axis=0, keepdims=True) + ur * w[C_KERNEL - 1:C_KERNEL])
        nbuf_ref[i] = jnp.concatenate([hist[1:], ur], axis=0)
    y = jnp.concatenate(rows, axis=0) + bdw_ref[...]
    o_ref[...] = _ln_swish(y, g_ref[...], b_ref[...])


def dwconv_sample(u, buf, w, b_dw, ln_g, ln_b):
    nb, d = u.shape
    tb = min(SAMPLE_TILE, nb)
    const = lambda shape: pl.BlockSpec(shape, lambda i: (0,) * len(shape))
    return pl.pallas_call(
        functools.partial(_dwconv_sample_body, tb=tb),
        grid=(nb // tb,),
        in_specs=[pl.BlockSpec((tb, d), lambda i: (i, 0)), pl.BlockSpec((tb, C_KERNEL - 1, d), lambda i: (i, 0, 0)),
                  const((C_KERNEL, d)), const((1, d)), const((1, d)), const((1, d))],
        out_specs=[pl.BlockSpec((tb, d), lambda i: (i, 0)), pl.BlockSpec((tb, C_KERNEL - 1, d), lambda i: (i, 0, 0))],
        out_shape=[jax.ShapeDtypeStruct((nb, d), F32), jax.ShapeDtypeStruct(buf.shape, F32)],
        compiler_params=_cparams("parallel"),
        name="dwconv_sample",
    )(u, buf, w, b_dw, ln_g, ln_b)


def _moe(h8, eid128, wg, wu, wd, layer, *, precise):
    block_e, slot_src, slot_dst = moe_plan(eid128[:, :TOP_K])
    return experts(h8, block_e, slot_src, slot_dst, wg, wu, wd, layer, precise=precise)


def _trunk(x, caches, p, *, prompt):
    bsz, seq, d = x.shape
    rows = bsz * seq
    precise = not prompt
    wdt = F32 if precise else BF16
    xf = x.reshape(rows, d)
    row = lambda v: v.reshape(1, -1).astype(F32)

    z = norm_proj(xf, row(p['norm_mix'][0]), p['w_in'].astype(wdt), precise=precise)
    if prompt:
        att = attn_prompt(z, p['rel_table'], p['sinks'], bsz, seq)
        out_b, c1, n1, m1 = mlstm_prompt(z, p['conv_w'], p['b_gates'], p['g_mnorm'], bsz, seq)
        z3 = z.reshape(bsz, seq, Z_COLS)
        new_k = z3[:, seq - WINDOW:, Z_KA:Z_KA + A_KV].reshape(bsz, WINDOW, A_KV_HEADS, A_HEAD_DIM)
        new_v = z3[:, seq - WINDOW:, Z_VA:Z_VA + A_KV].reshape(bsz, WINDOW, A_KV_HEADS, A_HEAD_DIM)
        new_conv = z3[:, seq - (B_CONV - 1):, Z_QK:Z_QK + 2 * B_QK]
        n1 = n1[:, :B_HEADS]
        m1 = m1[:, :B_HEADS, 0]
    else:
        ck, cv, c0, n0, m0, cbuf = caches[:6]
        n_buf = ck.shape[1]
        att, out_b, new_k, new_v, c1, n1, m1, new_conv = mix_sample(
            z, p['rel_table'], p['sinks'], ck.reshape(bsz, n_buf, A_KV), cv.reshape(bsz, n_buf, A_KV),
            c0, n0, m0, cbuf, p['conv_w'], p['b_gates'], p['g_mnorm'])
        new_k = new_k.reshape(bsz, n_buf, A_KV_HEADS, A_HEAD_DIM)
        new_v = new_v.reshape(bsz, n_buf, A_KV_HEADS, A_HEAD_DIM)
        m1 = m1[:, :B_HEADS]
    w_out = p['w_out'].astype(wdt)
    x1, h8, eid, gate = proj_router([att, out_b], [w_out[:A_Q], w_out[A_Q:]], None, xf,
                                    row(p['norm_ffn'][0]), p['w_router'][0], p['b_router'][0], precise=precise)
    yt = _moe(h8, eid, p['w_eg'], p['w_eu'], p['w_ed'], 0, precise=precise)

    x2, u = combine_glu(x1, yt, gate, row(p['norm_mix'][1]), p['w_pw1'].astype(wdt), row(p['b_pw1']),
                        precise=precise)
    if prompt:
        yc = dwconv_prompt(u, p['w_dw'], row(p['b_dw']), row(p['ln_g']), row(p['ln_b']), bsz, seq)
        new_cbuf = u.reshape(bsz, seq, d)[:, seq - (C_KERNEL - 1):]
    else:
        yc, new_cbuf = dwconv_sample(u, caches[6], p['w_dw'], row(p['b_dw']), row(p['ln_g']), row(p['ln_b']))
    x3, h8, eid, gate = proj_router([yc], [p['w_pw2'].astype(wdt)], row(p['b_pw2']), x2,
                                    row(p['norm_ffn'][1]), p['w_router'][1], p['b_router'][1], precise=precise)
    yt = _moe(h8, eid, p['w_eg'], p['w_eu'], p['w_ed'], 1, precise=precise)
    y = combine_final(x3, yt, gate, row(p['norm_final']))
    add_layer = lambda t: t[None]
    return (y.reshape(bsz, seq, d),) + tuple(add_layer(t) for t in (new_k, new_v, c1, n1, m1, new_conv, new_cbuf))


def kernel(x_prompt, x_sample, cache_win_k, cache_win_v, state_mlstm_c, state_mlstm_n, state_mlstm_m, state_mlstm_conv, state_conv, norm_mix, norm_ffn, norm_final, rel_bias_table, w_in_mix, b_mlstm_gates, w_mlstm_qk_conv, attn_sinks, g_mlstm_norm, w_out_mix, w_pw1, b_pw1, w_dw, b_dw, ln_conv_g, ln_conv_b, w_pw2, b_pw2, w_router_group, b_router_group, w_router_expert, b_router_expert, w_expert_gate, w_expert_up, w_expert_down):
    w_in = w_in_mix[0]
    s_q, s_k, s_v, s_qk, s_vb, s_g = A_Q, A_Q + A_KV, A_Q + 2 * A_KV, A_Q + 2 * A_KV + 2 * B_QK, \
        A_Q + 2 * A_KV + 2 * B_QK + B_V, A_Q + 2 * A_KV + 2 * B_QK + B_V + 2 * B_HEADS
    w_in_r = jnp.concatenate([w_in[:, :s_q], w_in[:, s_v:s_qk], w_in[:, s_qk:s_vb], w_in[:, s_g:],
                              w_in[:, s_q:s_k], w_in[:, s_k:s_v], w_in[:, s_vb:s_g],
                              jnp.zeros((D_MODEL, LANES - 2 * B_HEADS), F32)], axis=1)
    b_gates = jnp.concatenate([b_mlstm_gates[0], jnp.zeros((LANES - 2 * B_HEADS,), F32)]).reshape(1, LANES)
    depth = w_router_group.shape[0]
    w_re = jnp.transpose(w_router_expert, (0, 2, 1, 3)).reshape(depth, D_MODEL, N_EXPERTS)
    w_router = jnp.concatenate([w_router_group, w_re,
                                jnp.zeros((depth, D_MODEL, LANES - N_GROUPS - N_EXPERTS), F32)], axis=-1)
    b_router = jnp.concatenate([b_router_group, b_router_expert.reshape(depth, N_EXPERTS),
                                jnp.zeros((depth, LANES - N_GROUPS - N_EXPERTS), F32)], axis=-1)[:, None, :]
    p = dict(norm_mix=norm_mix, norm_ffn=norm_ffn, norm_final=norm_final, rel_table=rel_bias_table,
             sinks=attn_sinks[0], w_in=w_in_r, b_gates=b_gates, conv_w=w_mlstm_qk_conv[0],
             g_mnorm=g_mlstm_norm[0].reshape(1, B_V), w_out=w_out_mix[0], w_pw1=w_pw1[0], b_pw1=b_pw1[0],
             w_dw=w_dw[0], b_dw=b_dw[0], ln_g=ln_conv_g[0], ln_b=ln_conv_b[0], w_pw2=w_pw2[0], b_pw2=b_pw2[0],
             w_router=w_router, b_router=b_router, w_eg=w_expert_gate, w_eu=w_expert_up, w_ed=w_expert_down)
    caches = (cache_win_k[0], cache_win_v[0], state_mlstm_c[0], state_mlstm_n[0], state_mlstm_m[0],
              state_mlstm_conv[0], state_conv[0])
    out_p = _trunk(x_prompt, None, p, prompt=True)
    out_s = _trunk(x_sample, caches, p, prompt=False)
    return (out_p[0], out_s[0]) + out_p[1:] + out_s[1:]
```

```python
import functools
import math

import numpy as np
import jax
import jax.numpy as jnp
from jax import lax
from jax.experimental import pallas as pl
from jax.experimental.pallas import tpu as pltpu

F32 = jnp.float32
BF16 = jnp.bfloat16
I32 = jnp.int32
HI = lax.Precision.HIGHEST
NEG_INF = float("-inf")

LANES = 128
SUBLANES = 8
VMEM_LIMIT = 56 * 1024 * 1024

D_MODEL = 1024
A_HEADS = 8
A_KV_HEADS = 2
A_GROUP = A_HEADS // A_KV_HEADS
A_HEAD_DIM = 64
WINDOW = 128
REL_BUCKETS = 32
REL_MAX_DIST = 128
B_HEADS = 4
B_DK = 64
B_DV = 128
B_CONV = 4
C_KERNEL = 31
N_GROUPS = 4
EXPERTS_PER_GROUP = 8
N_EXPERTS = N_GROUPS * EXPERTS_PER_GROUP
TOP_K = 2
EXPERT_FF = D_MODEL // 2
EXPERT_BLOCK = 128
EPS = 1e-6

A_Q = A_HEADS * A_HEAD_DIM
A_KV = A_KV_HEADS * A_HEAD_DIM
B_QK = B_HEADS * B_DK
B_V = B_HEADS * B_DV
ROW_TILES = D_MODEL // LANES

Z_QA, Z_QK, Z_VB, Z_OG, Z_KA, Z_VA, Z_GATES = 0, 512, 1024, 1536, 2048, 2176, 2304
Z_COLS = 2432
MLSTM_CHUNK = 128


def _cparams(*sem):
    return pltpu.CompilerParams(dimension_semantics=sem, vmem_limit_bytes=VMEM_LIMIT)


def _rms(x, g):
    return x * lax.rsqrt(jnp.mean(x * x, axis=-1, keepdims=True) + EPS) * g


def _mm(a, w, precise):
    if precise:
        return jnp.dot(a.astype(F32), w, precision=HI, preferred_element_type=F32)
    return jnp.dot(a.astype(BF16), w, preferred_element_type=F32)


def _t5_buckets(dist):
    exact = REL_BUCKETS // 2
    d = np.maximum(dist, 0)
    large = exact + (np.log(np.maximum(d, 1).astype(np.float32) / exact)
                     / math.log(REL_MAX_DIST / exact) * (REL_BUCKETS - exact)).astype(np.int32)
    return np.where(d < exact, d, np.minimum(large, REL_BUCKETS - 1)).astype(np.int32)


def _norm_proj_body(x_ref, g_ref, w_ref, o_ref, *, precise):
    h = _rms(x_ref[...], g_ref[...])
    o_ref[...] = _mm(h, w_ref[...], precise)


def norm_proj(x, g, w, *, precise):
    rows, d = x.shape
    cols = w.shape[1]
    tm = min(rows, 128 if precise else 512)
    return pl.pallas_call(
        functools.partial(_norm_proj_body, precise=precise),
        grid=(rows // tm,),
        in_specs=[pl.BlockSpec((tm, d), lambda i: (i, 0)),
                  pl.BlockSpec((1, d), lambda i: (0, 0)),
                  pl.BlockSpec((d, cols), lambda i: (0, 0))],
        out_specs=pl.BlockSpec((tm, cols), lambda i: (i, 0)),
        out_shape=jax.ShapeDtypeStruct((rows, cols), F32),
        compiler_params=_cparams("parallel"),
        name="norm_proj",
    )(x, g, w)


def _attn_prompt_body(tab_ref, sink_ref, bkt_ref, q_ref, kp_ref, kc_ref, vp_ref, vc_ref, o_ref, bias_ref):
    b = pl.program_id(0)
    n = pl.program_id(1)

    @pl.when((b == 0) & (n == 0))
    def _():
        bk = bkt_ref[...]
        for h in range(A_HEADS):
            acc = jnp.full(bk.shape, NEG_INF, F32)
            for t in range(REL_BUCKETS):
                acc = jnp.where(bk == t, tab_ref[t, h], acc)
            bias_ref[h] = acc

    q = q_ref[...] * (A_HEAD_DIM ** -0.5)
    kb = jnp.concatenate([kp_ref[...], kc_ref[...]], axis=0).astype(BF16)
    vb = jnp.concatenate([vp_ref[...], vc_ref[...]], axis=0).astype(BF16)
    col = lax.broadcasted_iota(I32, (WINDOW, 2 * WINDOW), 1)
    dead = col < jnp.where(n == 0, WINDOW, 0)
    outs = []
    for h in range(A_HEADS):
        kvh = h // A_GROUP
        qh = q[:, h * A_HEAD_DIM:(h + 1) * A_HEAD_DIM].astype(BF16)
        kh = kb[:, kvh * A_HEAD_DIM:(kvh + 1) * A_HEAD_DIM]
        vh = vb[:, kvh * A_HEAD_DIM:(kvh + 1) * A_HEAD_DIM]
        s = lax.dot_general(qh, kh, (((1,), (1,)), ((), ())), preferred_element_type=F32)
        s = jnp.where(dead, NEG_INF, s + bias_ref[h])
        sink = sink_ref[h]
        mx = jnp.maximum(jnp.max(s, axis=-1, keepdims=True), sink)
        p = jnp.exp(s - mx)
        den = jnp.sum(p, axis=-1, keepdims=True) + jnp.exp(sink - mx)
        outs.append(jnp.dot(p.astype(BF16), vh, preferred_element_type=F32) / den)
    o_ref[...] = jnp.concatenate(outs, axis=1).astype(o_ref.dtype)


def attn_prompt(z, rel_table, sinks, bsz, seq):
    nb = seq // WINDOW
    dist = WINDOW + np.arange(WINDOW)[:, None] - np.arange(2 * WINDOW)[None, :]
    bkt = np.where((dist >= 0) & (dist <= WINDOW), _t5_buckets(dist), -1).astype(np.int32)
    kcol, vcol = Z_KA // LANES, Z_VA // LANES
    smem = pl.BlockSpec(memory_space=pltpu.SMEM)

    def cur(c):
        return pl.BlockSpec((WINDOW, LANES), lambda b, n: (b * nb + n, c))

    def prev(c):
        return pl.BlockSpec((WINDOW, LANES), lambda b, n: (b * nb + jnp.maximum(n - 1, 0), c))

    return pl.pallas_call(
        _attn_prompt_body,
        grid=(bsz, nb),
        in_specs=[smem, smem,
                  pl.BlockSpec((WINDOW, 2 * WINDOW), lambda b, n: (0, 0)),
                  pl.BlockSpec((WINDOW, A_Q), lambda b, n: (b * nb + n, 0)),
                  prev(kcol), cur(kcol), prev(vcol), cur(vcol)],
        out_specs=pl.BlockSpec((WINDOW, A_Q), lambda b, n: (b * nb + n, 0)),
        out_shape=jax.ShapeDtypeStruct((bsz * seq, A_Q), BF16),
        scratch_shapes=[pltpu.VMEM((A_HEADS, WINDOW, 2 * WINDOW), F32)],
        compiler_params=_cparams("arbitrary", "arbitrary"),
        name="attn_prompt",
    )(rel_table, sinks, jnp.asarray(bkt), z, z, z, z, z)


def _log_sigmoid(x):
    return -(jnp.maximum(-x, 0.0) + jnp.log1p(jnp.exp(-jnp.abs(x))))


def _mlstm_prompt_body(qk_ref, v_ref, g_ref, og_ref, cw_ref, bg_ref, gn_ref,
                       ob_ref, c_out, n_out, m_out,
                       c_sc, n_sc, m_sc, hist_sc, *, L, nc):
    ci = pl.program_id(1)

    @pl.when(ci == 0)
    def _():
        c_sc[...] = jnp.zeros(c_sc.shape, F32)
        n_sc[...] = jnp.zeros(n_sc.shape, F32)
        m_sc[...] = jnp.full(m_sc.shape, NEG_INF, F32)
        hist_sc[...] = jnp.zeros(hist_sc.shape, F32)

    cur = qk_ref[...]
    ext = jnp.concatenate([hist_sc[...], cur], axis=0)
    cw = cw_ref[...]
    off = SUBLANES - (B_CONV - 1)
    conv = ext[off:off + L] * cw[0:1]
    for j in range(1, B_CONV):
        conv = conv + ext[off + j:off + j + L] * cw[j:j + 1]
    hist_sc[...] = cur[L - SUBLANES:L]
    qk = conv * jax.nn.sigmoid(conv)
    q_all = qk[:, :B_QK]
    k_all = qk[:, B_QK:] * (B_DK ** -0.5)
    v_all = v_ref[...]
    og = og_ref[...]
    gn = gn_ref[...]

    G = g_ref[...] + bg_ref[...]
    lf = _log_sigmoid(G)
    row = lax.broadcasted_iota(I32, (L, L), 0)
    colm = lax.broadcasted_iota(I32, (L, L), 1)
    causal = colm <= row
    Bc = jnp.dot(causal.astype(F32), lf, precision=HI, preferred_element_type=F32)
    BT = Bc.T
    GT = G.T

    outs = []
    for h in range(B_HEADS):
        qh = q_all[:, h * B_DK:(h + 1) * B_DK]
        kh = k_all[:, h * B_DK:(h + 1) * B_DK]
        vh = v_all[:, h * B_DV:(h + 1) * B_DV]
        b_col = Bc[:, B_HEADS + h:B_HEADS + h + 1]
        ig_col = G[:, h:h + 1]
        b_row = BT[B_HEADS + h:B_HEADS + h + 1, :]
        ig_row = GT[h:h + 1, :]
        c0 = c_sc[h]
        n0 = n_sc[h:h + 1, :]
        m0 = m_sc[h:h + 1, 0:1]
        a = b_col + m0
        d = jnp.where(causal, b_col - b_row + ig_row, NEG_INF)
        m = jnp.maximum(a, jnp.max(d, axis=-1, keepdims=True))
        dw = jnp.exp(d - m)
        aw = jnp.exp(a - m)
        qb = qh.astype(BF16)
        vb = vh.astype(BF16)
        s = lax.dot_general(qb, kh.astype(BF16), (((1,), (1,)), ((), ())), preferred_element_type=F32) * dw
        num = (jnp.dot(s.astype(BF16), vb, preferred_element_type=F32)
               + aw * jnp.dot(qb, c0.astype(BF16), preferred_element_type=F32))
        den = jnp.sum(s, axis=-1, keepdims=True) + aw * jnp.sum(qh * n0, axis=-1, keepdims=True)
        hh = num / jnp.maximum(jnp.abs(den), jnp.exp(-m))
        m_last = m[L - 1:L, :]
        wl = jnp.exp(b_col[L - 1:L, :] - b_col + ig_col - m_last)
        decay = aw[L - 1:L, :]
        kw = kh * wl
        c_sc[h] = decay * c0 + lax.dot_general(kw.astype(BF16), vb, (((0,), (0,)), ((), ())),
                                               preferred_element_type=F32)
        n_sc[h:h + 1, :] = decay * n0 + jnp.sum(kw, axis=0, keepdims=True)
        m_sc[h:h + 1, :] = jnp.broadcast_to(m_last, (1, LANES))
        hn = hh * lax.rsqrt(jnp.mean(hh * hh, axis=-1, keepdims=True) + EPS) * gn[:, h * B_DV:(h + 1) * B_DV]
        outs.append(jax.nn.sigmoid(og[:, h * B_DV:(h + 1) * B_DV]) * hn)
    ob_ref[...] = jnp.concatenate(outs, axis=1).astype(ob_ref.dtype)

    @pl.when(ci == nc - 1)
    def _():
        c_out[0] = c_sc[...]
        n_out[0] = n_sc[...]
        m_out[0] = m_sc[...]


def mlstm_prompt(z, conv_w, b_gates_pad, g_norm, bsz, seq):
    L = MLSTM_CHUNK
    nc = seq // L

    def zspec(width, colblk):
        return pl.BlockSpec((L, width), lambda b, c: (b * nc + c, colblk))

    const = lambda shape: pl.BlockSpec(shape, lambda b, c: (0,) * len(shape))
    return pl.pallas_call(
        functools.partial(_mlstm_prompt_body, L=L, nc=nc),
        grid=(bsz, nc),
        in_specs=[zspec(2 * B_QK, Z_QK // (2 * B_QK)), zspec(B_V, Z_VB // B_V), zspec(LANES, Z_GATES // LANES),
                  zspec(B_V, Z_OG // B_V), const((B_CONV, 2 * B_QK)), const((1, LANES)), const((1, B_V))],
        out_specs=[pl.BlockSpec((L, B_V), lambda b, c: (b * nc + c, 0)),
                   pl.BlockSpec((1, B_HEADS, B_DK, B_DV), lambda b, c: (b, 0, 0, 0)),
                   pl.BlockSpec((1, SUBLANES, B_DK), lambda b, c: (b, 0, 0)),
                   pl.BlockSpec((1, SUBLANES, LANES), lambda b, c: (b, 0, 0))],
        out_shape=[jax.ShapeDtypeStruct((bsz * seq, B_V), BF16),
                   jax.ShapeDtypeStruct((bsz, B_HEADS, B_DK, B_DV), F32),
                   jax.ShapeDtypeStruct((bsz, SUBLANES, B_DK), F32),
                   jax.ShapeDtypeStruct((bsz, SUBLANES, LANES), F32)],
        scratch_shapes=[pltpu.VMEM((B_HEADS, B_DK, B_DV), F32), pltpu.VMEM((SUBLANES, B_DK), F32),
                        pltpu.VMEM((SUBLANES, LANES), F32), pltpu.VMEM((SUBLANES, 2 * B_QK), F32)],
        compiler_params=_cparams("arbitrary", "arbitrary"),
        name="mlstm_prompt",
    )(z, z, z, z, conv_w, b_gates_pad, g_norm)


SAMPLE_TILE = 8


def _row_to_col(row, n):
    eye = lax.broadcasted_iota(I32, (n, n), 0) == lax.broadcasted_iota(I32, (n, n), 1)
    return jnp.sum(jnp.where(eye, jnp.broadcast_to(row, (n, n)), 0.0), axis=1, keepdims=True)


def _mix_sample_body(tab_ref, sink_ref, bkt_ref, z_ref, ck_ref, cv_ref, c0_ref, n0_ref, m0_ref, cb_ref,
                     cw_ref, bg_ref, gn_ref,
                     att_ref, ob_ref, nk_ref, nv_ref, c1_ref, n1_ref, m1_ref, ncb_ref, *, tb):
    bk = bkt_ref[...]
    cw = cw_ref[...]
    gn = gn_ref[...]
    zero_half = jnp.zeros((1, A_HEAD_DIM), F32)
    bias_cols = []
    for h in range(A_HEADS):
        bias = jnp.zeros(bk.shape, F32)
        for t in range(REL_BUCKETS):
            bias = jnp.where(bk == t, tab_ref[t, h], bias)
        bias_cols.append(bias)
    att_rows, ob_rows, m_rows = [], [], []
    for i in range(tb):
        zr = z_ref[i:i + 1, :]
        q_att = zr[:, Z_QA:Z_QA + A_Q] * (A_HEAD_DIM ** -0.5)
        ka = zr[:, Z_KA:Z_KA + A_KV]
        va = zr[:, Z_VA:Z_VA + A_KV]
        kc = ck_ref[i]
        vc = cv_ref[i]
        nk_ref[i] = jnp.concatenate([kc[1:], ka], axis=0)
        nv_ref[i] = jnp.concatenate([vc[1:], va], axis=0)
        heads = []
        for h in range(A_HEADS):
            kvh = h // A_GROUP
            qh = q_att[:, h * A_HEAD_DIM:(h + 1) * A_HEAD_DIM]
            qrow = jnp.concatenate([qh, zero_half] if kvh == 0 else [zero_half, qh], axis=1)
            lc = jnp.sum(kc * qrow, axis=-1, keepdims=True) + bias_cols[h]
            ln = jnp.sum(ka * qrow, axis=-1, keepdims=True) + tab_ref[0, h]
            sink = sink_ref[h]
            mx = jnp.maximum(jnp.maximum(jnp.max(lc, axis=0, keepdims=True), ln), sink)
            pc = jnp.exp(lc - mx)
            pn = jnp.exp(ln - mx)
            den = jnp.sum(pc, axis=0, keepdims=True) + pn + jnp.exp(sink - mx)
            o = (jnp.sum(pc * vc, axis=0, keepdims=True) + pn * va) / den
            heads.append(o[:, kvh * A_HEAD_DIM:(kvh + 1) * A_HEAD_DIM])
        att_rows.append(jnp.concatenate(heads, axis=1))

        qk_pre = zr[:, Z_QK:Z_QK + 2 * B_QK]
        hist = cb_ref[i]
        conv = qk_pre * cw[B_CONV - 1:B_CONV]
        for j in range(B_CONV - 1):
            conv = conv + hist[j:j + 1] * cw[j:j + 1]
        ncb_ref[i] = jnp.concatenate([hist[1:], qk_pre], axis=0)
        qk = conv * jax.nn.sigmoid(conv)
        G = zr[:, Z_GATES:Z_GATES + LANES] + bg_ref[...]
        lfr = _log_sigmoid(G)
        og = zr[:, Z_OG:Z_OG + B_V]
        v_pre = zr[:, Z_VB:Z_VB + B_V]
        obs, ms = [], []
        for h in range(B_HEADS):
            qh = qk[:, h * B_DK:(h + 1) * B_DK]
            kh = qk[:, B_QK + h * B_DK:B_QK + (h + 1) * B_DK] * (B_DK ** -0.5)
            vh = v_pre[:, h * B_DV:(h + 1) * B_DV]
            ig = G[:, h:h + 1]
            lf = lfr[:, B_HEADS + h:B_HEADS + h + 1]
            c0 = c0_ref[i, h]
            n0 = n0_ref[i, h:h + 1, :]
            m0 = m0_ref[i:i + 1, h:h + 1]
            a = lf + m0
            m = jnp.maximum(a, ig)
            dw = jnp.exp(ig - m)
            aw = jnp.exp(a - m)
            s = jnp.sum(qh * kh, axis=-1, keepdims=True) * dw
            q_col = _row_to_col(qh, B_DK)
            k_col = _row_to_col(kh, B_DK)
            num = s * vh + aw * jnp.sum(q_col * c0, axis=0, keepdims=True)
            den = s + aw * jnp.sum(qh * n0, axis=-1, keepdims=True)
            hh = num / jnp.maximum(jnp.abs(den), jnp.exp(-m))
            c1_ref[i, h] = aw * c0 + dw * (k_col * vh)
            n1_ref[i, h:h + 1, :] = aw * n0 + dw * kh
            ms.append(m)
            hn = hh * lax.rsqrt(jnp.mean(hh * hh, axis=-1, keepdims=True) + EPS) * gn[:, h * B_DV:(h + 1) * B_DV]
            obs.append(jax.nn.sigmoid(og[:, h * B_DV:(h + 1) * B_DV]) * hn)
        ob_rows.append(jnp.concatenate(obs, axis=1))
        lane = lax.broadcasted_iota(I32, (1, LANES), 1)
        mrow = jnp.zeros((1, LANES), F32)
        for h in range(B_HEADS):
            mrow = jnp.where(lane == h, ms[h], mrow)
        m_rows.append(mrow)
    att_ref[...] = jnp.concatenate(att_rows, axis=0)
    ob_ref[...] = jnp.concatenate(ob_rows, axis=0)
    m1_ref[...] = jnp.concatenate(m_rows, axis=0)


def mix_sample(z, rel_table, sinks, ck, cv, c0, n0, m0, conv_buf, conv_w, b_gates_pad, g_norm):
    nb = z.shape[0]
    tb = min(SAMPLE_TILE, nb)
    n_buf = ck.shape[1]
    bkt = _t5_buckets(n_buf - np.arange(n_buf))[:, None]
    smem = pl.BlockSpec(memory_space=pltpu.SMEM)
    const = lambda shape: pl.BlockSpec(shape, lambda i: (0,) * len(shape))
    lead = lambda shape: pl.BlockSpec((tb,) + shape, lambda i: (i,) + (0,) * len(shape))
    return pl.pallas_call(
        functools.partial(_mix_sample_body, tb=tb),
        grid=(nb // tb,),
        in_specs=[smem, smem, const((n_buf, 1)), lead((Z_COLS,)), lead((n_buf, A_KV)), lead((n_buf, A_KV)),
                  lead((B_HEADS, B_DK, B_DV)), lead((B_HEADS, B_DK)), lead((B_HEADS,)),
                  lead((B_CONV - 1, 2 * B_QK)), const((B_CONV, 2 * B_QK)), const((1, LANES)), const((1, B_V))],
        out_specs=[lead((A_Q,)), lead((B_V,)), lead((n_buf, A_KV)), lead((n_buf, A_KV)),
                   lead((B_HEADS, B_DK, B_DV)), lead((B_HEADS, B_DK)), lead((LANES,)),
                   lead((B_CONV - 1, 2 * B_QK))],
        out_shape=[jax.ShapeDtypeStruct((nb, A_Q), F32), jax.ShapeDtypeStruct((nb, B_V), F32),
                   jax.ShapeDtypeStruct(ck.shape, F32), jax.ShapeDtypeStruct(cv.shape, F32),
                   jax.ShapeDtypeStruct(c0.shape, F32), jax.ShapeDtypeStruct(n0.shape, F32),
                   jax.ShapeDtypeStruct((nb, LANES), F32), jax.ShapeDtypeStruct(conv_buf.shape, F32)],
        compiler_params=_cparams("parallel"),
        name="mix_sample",
    )(rel_table, sinks, jnp.asarray(bkt), z, ck, cv, c0, n0, m0, conv_buf, conv_w, b_gates_pad, g_norm)


def _store_row_tiles(ref, val, rows):
    for s in range(ROW_TILES):
        ref[pl.ds(s, rows, stride=ROW_TILES), :] = val[:, s * LANES:(s + 1) * LANES]


def _load_row_tiles(ref, rows, start=0, stride=ROW_TILES):
    return jnp.concatenate([ref[pl.ds(start + s, rows, stride=stride), :] for s in range(ROW_TILES)], axis=1)


def _route(logits):
    lane = lax.broadcasted_iota(I32, logits.shape, 1)
    big = jnp.int32(1 << 20)
    gl = jnp.where(lane < N_GROUPS, logits, NEG_INF)
    gmax = jnp.max(gl, axis=-1, keepdims=True)
    gidx = jnp.min(jnp.where(gl == gmax, lane, big), axis=-1, keepdims=True)
    g_gate = 1.0 / jnp.sum(jnp.exp(gl - gmax), axis=-1, keepdims=True)
    lo = N_GROUPS + gidx * EXPERTS_PER_GROUP
    el = jnp.where((lane >= lo) & (lane < lo + EXPERTS_PER_GROUP), logits, NEG_INF)
    v1 = jnp.max(el, axis=-1, keepdims=True)
    i1 = jnp.min(jnp.where(el == v1, lane, big), axis=-1, keepdims=True)
    el2 = jnp.where(lane == i1, NEG_INF, el)
    v2 = jnp.max(el2, axis=-1, keepdims=True)
    i2 = jnp.min(jnp.where(el2 == v2, lane, big), axis=-1, keepdims=True)
    t = jnp.exp(v2 - v1)
    w1 = g_gate / (1.0 + t)
    w2 = g_gate * t / (1.0 + t)
    eid = jnp.where(lane == 0, i1 - N_GROUPS, jnp.where(lane == 1, i2 - N_GROUPS, 0))
    gate = jnp.where(lane == 0, w1, jnp.where(lane == 1, w2, 0.0))
    return eid, gate


def _proj_router_body(*refs, n_in, has_bias, precise, tm):
    a_refs = refs[:n_in]
    w_refs = refs[n_in:2 * n_in]
    k = 2 * n_in
    bias_ref = refs[k] if has_bias else None
    k += 1 if has_bias else 0
    x_ref, g_ref, wr_ref, br_ref, x1_ref, h8_ref, eid_ref, gate_ref = refs[k:]
    acc = x_ref[...]
    if has_bias:
        acc = acc + bias_ref[...]
    for a_ref, w_ref in zip(a_refs, w_refs):
        acc = acc + _mm(a_ref[...], w_ref[...], precise)
    x1_ref[...] = acc
    h = _rms(acc, g_ref[...])
    _store_row_tiles(h8_ref, h, tm)
    logits = jnp.dot(h, wr_ref[...], precision=HI, preferred_element_type=F32) + br_ref[...]
    eid, gate = _route(logits)
    eid_ref[...] = eid
    gate_ref[...] = gate


def proj_router(a_list, w_list, bias, x, g, wr, br, *, precise):
    rows, d = x.shape
    tm = min(rows, 128 if precise else 512)
    n_in = len(a_list)
    row_spec = lambda width: pl.BlockSpec((tm, width), lambda i: (i, 0))
    const = lambda shape: pl.BlockSpec(shape, lambda i: (0,) * len(shape))
    in_specs = [row_spec(a.shape[1]) for a in a_list] + [const(w.shape) for w in w_list]
    args = list(a_list) + list(w_list)
    if bias is not None:
        in_specs.append(const((1, d)))
        args.append(bias)
    in_specs += [row_spec(d), const((1, d)), const((d, LANES)), const((1, LANES))]
    args += [x, g, wr, br]
    return pl.pallas_call(
        functools.partial(_proj_router_body, n_in=n_in, has_bias=bias is not None, precise=precise, tm=tm),
        grid=(rows // tm,),
        in_specs=in_specs,
        out_specs=[row_spec(d), pl.BlockSpec((tm * ROW_TILES, LANES), lambda i: (i, 0)),
                   row_spec(LANES), row_spec(LANES)],
        out_shape=[jax.ShapeDtypeStruct((rows, d), F32), jax.ShapeDtypeStruct((rows * ROW_TILES, LANES), F32),
                   jax.ShapeDtypeStruct((rows, LANES), I32), jax.ShapeDtypeStruct((rows, LANES), F32)],
        compiler_params=_cparams("parallel"),
        name="proj_router",
    )(*args)


MOE_CHUNK = 512


def _tile_rows(r, n=1):
    return pl.ds(pl.multiple_of(r * ROW_TILES, ROW_TILES), n * ROW_TILES)


def _pow2_pieces(limit):
    p = 1
    while p * 2 <= limit:
        p *= 2
    out = []
    while p >= 1:
        out.append(p)
        p //= 2
    return out


def _for_each_piece(n, pieces, fn):
    for p in pieces:
        @pl.when((n & p) != 0)
        def _(p=p):
            fn(n & ~(2 * p - 1), p)


def _dispatch_body(rn_ref, rg_ref, rl_ref, ps_ref, pn_ref, tail_ref, lpos_ref, h8_ref, xs_hbm,
                   stage, zbuf, sem, zsem, *, chunk, nch):
    c = pl.program_id(0)
    slot = c % 2
    run_pieces = _pow2_pieces(chunk)
    pad_pieces = _pow2_pieces(EXPERT_BLOCK - 1)

    @pl.when(c == 0)
    def _():
        zbuf[...] = jnp.zeros(zbuf.shape, F32)

        def pad_dmas(e, op):
            def one(off, p):
                cp = pltpu.make_async_copy(zbuf.at[_tile_rows(0, p)], xs_hbm.at[_tile_rows(ps_ref[e] + off, p)], zsem)
                cp.start() if op == 0 else cp.wait()
            _for_each_piece(pn_ref[e], pad_pieces, one)

        def issue(e, carry):
            pad_dmas(e, 0)
            return carry

        def wait(e, carry):
            pad_dmas(e, 1)
            return carry
        lax.fori_loop(0, N_EXPERTS, issue, 0)
        lax.fori_loop(0, N_EXPERTS, wait, 0)

        half = EXPERT_BLOCK // 2

        def tail_dmas(i, op):
            cp = pltpu.make_async_copy(zbuf, xs_hbm.at[_tile_rows(tail_ref[0] + i * half, half)], zsem)
            cp.start() if op == 0 else cp.wait()

        def tail_issue(i, carry):
            tail_dmas(i, 0)
            return carry

        def tail_wait(i, carry):
            tail_dmas(i, 1)
            return carry
        lax.fori_loop(0, tail_ref[1], tail_issue, 0)
        lax.fori_loop(0, tail_ref[1], tail_wait, 0)

    def copy_tok(t, carry):
        row = h8_ref[_tile_rows(t), :]
        stage[slot, _tile_rows(lpos_ref[0, 0, TOP_K * t]), :] = row
        stage[slot, _tile_rows(lpos_ref[0, 0, TOP_K * t + 1]), :] = row
        return carry
    lax.fori_loop(0, chunk, copy_tok, 0, unroll=8)

    @pl.when(c > 0)
    def _():
        pltpu.make_async_copy(stage.at[1 - slot], stage.at[1 - slot], sem).wait()

    def send_runs(e, carry):
        k = c * N_EXPERTS + e

        def one(off, p):
            pltpu.make_async_copy(stage.at[slot, _tile_rows(rl_ref[k] + off, p)],
                                  xs_hbm.at[_tile_rows(rg_ref[k] + off, p)], sem).start()
        _for_each_piece(rn_ref[k], run_pieces, one)
        return carry
    lax.fori_loop(0, N_EXPERTS, send_runs, 0)

    @pl.when(c == nch - 1)
    def _():
        pltpu.make_async_copy(stage.at[slot], stage.at[slot], sem).wait()


def dispatch(h8, plan, n_slots):
    chunk, nch = plan['chunk'], plan['nch']
    gs = pltpu.PrefetchScalarGridSpec(
        num_scalar_prefetch=6,
        grid=(nch,),
        in_specs=[pl.BlockSpec((1, 1, TOP_K * chunk), lambda c, *_: (c, 0, 0), memory_space=pltpu.SMEM),
                  pl.BlockSpec((chunk * ROW_TILES, LANES), lambda c, *_: (c, 0))],
        out_specs=pl.BlockSpec(memory_space=pl.ANY),
        scratch_shapes=[pltpu.VMEM((2, TOP_K * chunk * ROW_TILES, LANES), F32),
                        pltpu.VMEM((EXPERT_BLOCK // 2 * ROW_TILES, LANES), F32),
                        pltpu.SemaphoreType.DMA(()), pltpu.SemaphoreType.DMA(())],
    )
    return pl.pallas_call(
        functools.partial(_dispatch_body, chunk=chunk, nch=nch),
        grid_spec=gs,
        out_shape=jax.ShapeDtypeStruct((n_slots * ROW_TILES, LANES), F32),
        compiler_params=_cparams("arbitrary"),
        name="dispatch",
    )(plan['run_n'], plan['run_g'], plan['run_l'], plan['pad_start'], plan['pad_n'], plan['tail'], plan['lpos'], h8)


def _experts_body(be_ref, nv_ref, xs_ref, wg_ref, wu_ref, wd_ref, ys_ref, *wcast, precise):
    j = pl.program_id(0)

    @pl.when(j < nv_ref[0])
    def _():
        x = _load_row_tiles(xs_ref, EXPERT_BLOCK)
        if precise:
            wg, wu, wd = wg_ref[0], wu_ref[0], wd_ref[0]
        else:
            wgb, wub, wdb = wcast

            @pl.when((j == 0) | (be_ref[j] != be_ref[jnp.maximum(j - 1, 0)]))
            def _():
                wgb[...] = wg_ref[0].astype(BF16)
                wub[...] = wu_ref[0].astype(BF16)
                wdb[...] = wd_ref[0].astype(BF16)

            wg, wu, wd = wgb[...], wub[...], wdb[...]
        gt = _mm(x, wg, precise)
        up = _mm(x, wu, precise)
        hid = gt * jax.nn.sigmoid(gt) * up
        _store_row_tiles(ys_ref, _mm(hid, wd, precise), EXPERT_BLOCK)

    @pl.when(j >= nv_ref[0])
    def _():
        ys_ref[...] = jnp.zeros(ys_ref.shape, F32)


def experts(xs, plan, wg, wu, wd, layer, *, precise):
    nblk = plan['nblk']
    d, ff = wg.shape[2], wg.shape[3]
    blk = lambda j, be, nv: (jnp.minimum(j, nv[0] - 1), 0)
    wspec = lambda shape: pl.BlockSpec((None, 1) + shape,
                                       lambda j, be, nv: (layer, be[jnp.minimum(j, nv[0] - 1)], 0, 0))
    gs = pltpu.PrefetchScalarGridSpec(
        num_scalar_prefetch=2,
        grid=(nblk,),
        in_specs=[pl.BlockSpec((EXPERT_BLOCK * ROW_TILES, LANES), blk),
                  wspec((d, ff)), wspec((d, ff)), wspec((ff, d))],
        out_specs=pl.BlockSpec((EXPERT_BLOCK * ROW_TILES, LANES), lambda j, be, nv: (j, 0)),
        scratch_shapes=[] if precise else [pltpu.VMEM((d, ff), BF16), pltpu.VMEM((d, ff), BF16),
                                           pltpu.VMEM((ff, d), BF16)],
    )
    return pl.pallas_call(
        functools.partial(_experts_body, precise=precise),
        grid_spec=gs,
        out_shape=jax.ShapeDtypeStruct(xs.shape, F32),
        compiler_params=_cparams("arbitrary"),
        name="experts",
    )(plan['block_e'], plan['n_used'], xs, wg, wu, wd)


def moe_plan(eid, gate):
    n_tok = eid.shape[0]
    chunk = min(MOE_CHUNK, n_tok)
    nch = n_tok // chunk
    n_assign = n_tok * TOP_K
    nblk = (n_assign + N_EXPERTS * (EXPERT_BLOCK - 1)) // EXPERT_BLOCK
    e3 = eid.reshape(nch, TOP_K * chunk, 1)
    onehot = (e3 == jnp.arange(N_EXPERTS, dtype=I32)[None, None, :]).astype(I32)
    within = jnp.cumsum(onehot, axis=1) - onehot
    run_n = jnp.sum(onehot, axis=1)
    counts = jnp.sum(run_n, axis=0)
    padded = (counts + EXPERT_BLOCK - 1) // EXPERT_BLOCK * EXPERT_BLOCK
    pends = jnp.cumsum(padded)
    pstarts = pends - padded
    run_g = pstarts[None, :] + jnp.cumsum(run_n, axis=0) - run_n
    run_l = jnp.cumsum(run_n, axis=1) - run_n
    lpos = jnp.sum(onehot * (within + run_l[:, None, :]), axis=2)
    blk_start = jnp.arange(nblk, dtype=I32) * EXPERT_BLOCK
    block_e = jnp.minimum(jnp.sum((pends[None, :] <= blk_start[:, None]).astype(I32), axis=1), N_EXPERTS - 1)
    return dict(chunk=chunk, nch=nch, nblk=nblk,
                run_n=run_n.reshape(-1).astype(I32), run_g=run_g.reshape(-1).astype(I32),
                run_l=run_l.reshape(-1).astype(I32), pad_start=(pstarts + counts).astype(I32),
                pad_n=(padded - counts).astype(I32), lpos=lpos.reshape(nch, 1, TOP_K * chunk).astype(I32),
                gate=gate.reshape(nch, 1, TOP_K * chunk), block_e=block_e.astype(I32),
                n_used=(pends[-1:] // EXPERT_BLOCK).astype(I32),
                tail=jnp.stack([pends[-1], 2 * (nblk - pends[-1] // EXPERT_BLOCK)]).astype(I32))


def _combine(rn_ref, rg_ref, rl_ref, lpos_ref, gate_ref, x_ref, ys_hbm, ystage, comb, sem, *, chunk, nch):
    c = pl.program_id(0)
    slot = c % 2
    pieces = _pow2_pieces(chunk)

    def fetch(cc, sl):
        def per_e(e, carry):
            k = cc * N_EXPERTS + e

            def one(off, p):
                pltpu.make_async_copy(ys_hbm.at[_tile_rows(rg_ref[k] + off, p)],
                                      ystage.at[sl, _tile_rows(rl_ref[k] + off, p)], sem.at[sl]).start()
            _for_each_piece(rn_ref[k], pieces, one)
            return carry
        lax.fori_loop(0, N_EXPERTS, per_e, 0)

    @pl.when(c == 0)
    def _():
        fetch(0, 0)

    @pl.when(c + 1 < nch)
    def _():
        fetch(c + 1, 1 - slot)

    pltpu.make_async_copy(ystage.at[slot], ystage.at[slot], sem.at[slot]).wait()

    def per_tok(t, carry):
        y0 = ystage[slot, _tile_rows(lpos_ref[0, 0, TOP_K * t]), :]
        y1 = ystage[slot, _tile_rows(lpos_ref[0, 0, TOP_K * t + 1]), :]
        comb[_tile_rows(t), :] = gate_ref[0, 0, TOP_K * t] * y0 + gate_ref[0, 0, TOP_K * t + 1] * y1
        return carry
    lax.fori_loop(0, chunk, per_tok, 0, unroll=8)
    return x_ref[...] + _load_row_tiles(comb, chunk)


def _combine_glu_body(rn_ref, rg_ref, rl_ref, lpos_ref, gate_ref, x_ref, ys_hbm, g_ref, w_ref, b_ref,
                      x2_ref, u_ref, ystage, comb, sem, *, chunk, nch, precise):
    x2 = _combine(rn_ref, rg_ref, rl_ref, lpos_ref, gate_ref, x_ref, ys_hbm, ystage, comb, sem, chunk=chunk, nch=nch)
    x2_ref[...] = x2
    zz = _mm(_rms(x2, g_ref[...]), w_ref[...], precise) + b_ref[...]
    half = zz.shape[1] // 2
    u_ref[...] = zz[:, :half] * jax.nn.sigmoid(zz[:, half:])


def _combine_final_body(rn_ref, rg_ref, rl_ref, lpos_ref, gate_ref, x_ref, ys_hbm, g_ref, o_ref,
                        ystage, comb, sem, *, chunk, nch):
    x2 = _combine(rn_ref, rg_ref, rl_ref, lpos_ref, gate_ref, x_ref, ys_hbm, ystage, comb, sem, chunk=chunk, nch=nch)
    o_ref[...] = _rms(x2, g_ref[...])


def _combine_call(body, plan, x, ys, extra, extra_specs, out_specs, out_shape, name):
    chunk, nch = plan['chunk'], plan['nch']
    d = x.shape[1]
    smem_blk = pl.BlockSpec((1, 1, TOP_K * chunk), lambda c, *_: (c, 0, 0), memory_space=pltpu.SMEM)
    gs = pltpu.PrefetchScalarGridSpec(
        num_scalar_prefetch=3,
        grid=(nch,),
        in_specs=[smem_blk, smem_blk, pl.BlockSpec((chunk, d), lambda c, *_: (c, 0)),
                  pl.BlockSpec(memory_space=pl.ANY)] + extra_specs,
        out_specs=out_specs,
        scratch_shapes=[pltpu.VMEM((2, TOP_K * chunk * ROW_TILES, LANES), F32),
                        pltpu.VMEM((chunk * ROW_TILES, LANES), F32), pltpu.SemaphoreType.DMA((2,))],
    )
    return pl.pallas_call(
        functools.partial(body, chunk=chunk, nch=nch),
        grid_spec=gs,
        out_shape=out_shape,
        compiler_params=_cparams("arbitrary"),
        name=name,
    )(plan['run_n'], plan['run_g'], plan['run_l'], plan['lpos'], plan['gate'], x, ys, *extra)


def combine_glu(x, ys, plan, g, w, b, *, precise):
    rows, d = x.shape
    chunk = plan['chunk']
    cols = w.shape[1]
    const = lambda shape: pl.BlockSpec(shape, lambda c, *_: (0,) * len(shape))
    row_spec = lambda width: pl.BlockSpec((chunk, width), lambda c, *_: (c, 0))
    return _combine_call(
        functools.partial(_combine_glu_body, precise=precise), plan, x, ys, [g, w, b],
        [const((1, d)), const((d, cols)), const((1, cols))], [row_spec(d), row_spec(cols // 2)],
        [jax.ShapeDtypeStruct((rows, d), F32), jax.ShapeDtypeStruct((rows, cols // 2), F32)], "combine_glu")


def combine_final(x, ys, plan, g):
    rows, d = x.shape
    chunk = plan['chunk']
    return _combine_call(
        _combine_final_body, plan, x, ys, [g], [pl.BlockSpec((1, d), lambda c, *_: (0, 0))],
        pl.BlockSpec((chunk, d), lambda c, *_: (c, 0)), jax.ShapeDtypeStruct((rows, d), F32), "combine_final")


CONV_TILE = 256
CONV_HIST = 32


def _ln_swish(y, g, b):
    yc = y - jnp.mean(y, axis=-1, keepdims=True)
    yn = yc * lax.rsqrt(jnp.mean(yc * yc, axis=-1, keepdims=True) + EPS) * g + b
    return yn * jax.nn.sigmoid(yn)


CONV_ROWS = 64


def _dwconv_prompt_body(u_ref, w_ref, bdw_ref, g_ref, b_ref, o_ref, ext, y_sc, *, tt):
    t = pl.program_id(1)
    n_lt = ext.shape[0]

    @pl.when(t == 0)
    def _():
        ext[:, 0:CONV_HIST, :] = jnp.zeros((n_lt, CONV_HIST, LANES), F32)

    @pl.when(t > 0)
    def _():
        ext[:, 0:CONV_HIST, :] = ext[:, tt:tt + CONV_HIST, :]

    for j in range(n_lt):
        ext[j, CONV_HIST:CONV_HIST + tt, :] = u_ref[:, j * LANES:(j + 1) * LANES]
    off = CONV_HIST - (C_KERNEL - 1)
    for j in range(n_lt):
        wj = w_ref[:, j * LANES:(j + 1) * LANES]
        bj = bdw_ref[:, j * LANES:(j + 1) * LANES]
        for c in range(tt // CONV_ROWS):
            acc = ext[j, pl.ds(off + c * CONV_ROWS, CONV_ROWS), :] * wj[0:1] + bj
            for k in range(1, C_KERNEL):
                acc = acc + ext[j, pl.ds(off + k + c * CONV_ROWS, CONV_ROWS), :] * wj[k:k + 1]
            y_sc[c * CONV_ROWS:(c + 1) * CONV_ROWS, j * LANES:(j + 1) * LANES] = acc
    o_ref[...] = _ln_swish(y_sc[...], g_ref[...], b_ref[...]).astype(o_ref.dtype)


def dwconv_prompt(u, w, b_dw, ln_g, ln_b, bsz, seq):
    tt = min(CONV_TILE, seq)
    nt = seq // tt
    d = u.shape[1]
    const = lambda shape: pl.BlockSpec(shape, lambda b, t: (0,) * len(shape))
    return pl.pallas_call(
        functools.partial(_dwconv_prompt_body, tt=tt),
        grid=(bsz, nt),
        in_specs=[pl.BlockSpec((tt, d), lambda b, t: (b * nt + t, 0)), const((C_KERNEL, d)), const((1, d)),
                  const((1, d)), const((1, d))],
        out_specs=pl.BlockSpec((tt, d), lambda b, t: (b * nt + t, 0)),
        out_shape=jax.ShapeDtypeStruct((bsz * seq, d), BF16),
        scratch_shapes=[pltpu.VMEM((d // LANES, CONV_HIST + tt, LANES), F32), pltpu.VMEM((tt, d), F32)],
        compiler_params=_cparams("arbitrary", "arbitrary"),
        name="dwconv_prompt",
    )(u, w, b_dw, ln_g, ln_b)


def _dwconv_sample_body(u_ref, buf_ref, w_ref, bdw_ref, g_ref, b_ref, o_ref, nbuf_ref, *, tb):
    w = w_ref[...]
    rows = []
    for i in range(tb):
        hist = buf_ref[i]
        ur = u_ref[i:i + 1, :]
        rows.append(jnp.sum(hist * w[:C_KERNEL - 1], axis=0, keepdims=True) + ur * w[C_KERNEL - 1:C_KERNEL])
        nbuf_ref[i] = jnp.concatenate([hist[1:], ur], axis=0)
    y = jnp.concatenate(rows, axis=0) + bdw_ref[...]
    o_ref[...] = _ln_swish(y, g_ref[...], b_ref[...])


def dwconv_sample(u, buf, w, b_dw, ln_g, ln_b):
    nb, d = u.shape
    tb = min(SAMPLE_TILE, nb)
    const = lambda shape: pl.BlockSpec(shape, lambda i: (0,) * len(shape))
    return pl.pallas_call(
        functools.partial(_dwconv_sample_body, tb=tb),
        grid=(nb // tb,),
        in_specs=[pl.BlockSpec((tb, d), lambda i: (i, 0)), pl.BlockSpec((tb, C_KERNEL - 1, d), lambda i: (i, 0, 0)),
                  const((C_KERNEL, d)), const((1, d)), const((1, d)), const((1, d))],
        out_specs=[pl.BlockSpec((tb, d), lambda i: (i, 0)), pl.BlockSpec((tb, C_KERNEL - 1, d), lambda i: (i, 0, 0))],
        out_shape=[jax.ShapeDtypeStruct((nb, d), F32), jax.ShapeDtypeStruct(buf.shape, F32)],
        compiler_params=_cparams("parallel"),
        name="dwconv_sample",
    )(u, buf, w, b_dw, ln_g, ln_b)


def _moe(h8, eid128, gate128, wg, wu, wd, layer, *, precise):
    plan = moe_plan(eid128[:, :TOP_K], gate128[:, :TOP_K])
    xs = dispatch(h8, plan, plan['nblk'] * EXPERT_BLOCK)
    return experts(xs, plan, wg, wu, wd, layer, precise=precise), plan


def _trunk(x, caches, p, *, prompt):
    bsz, seq, d = x.shape
    rows = bsz * seq
    precise = not prompt
    wdt = F32 if precise else BF16
    xf = x.reshape(rows, d)
    row = lambda v: v.reshape(1, -1).astype(F32)

    z = norm_proj(xf, row(p['norm_mix'][0]), p['w_in'].astype(wdt), precise=precise)
    if prompt:
        att = attn_prompt(z, p['rel_table'], p['sinks'], bsz, seq)
        out_b, c1, n1, m1 = mlstm_prompt(z, p['conv_w'], p['b_gates'], p['g_mnorm'], bsz, seq)
        z3 = z.reshape(bsz, seq, Z_COLS)
        new_k = z3[:, seq - WINDOW:, Z_KA:Z_KA + A_KV].reshape(bsz, WINDOW, A_KV_HEADS, A_HEAD_DIM)
        new_v = z3[:, seq - WINDOW:, Z_VA:Z_VA + A_KV].reshape(bsz, WINDOW, A_KV_HEADS, A_HEAD_DIM)
        new_conv = z3[:, seq - (B_CONV - 1):, Z_QK:Z_QK + 2 * B_QK]
        n1 = n1[:, :B_HEADS]
        m1 = m1[:, :B_HEADS, 0]
    else:
        ck, cv, c0, n0, m0, cbuf = caches[:6]
        n_buf = ck.shape[1]
        att, out_b, new_k, new_v, c1, n1, m1, new_conv = mix_sample(
            z, p['rel_table'], p['sinks'], ck.reshape(bsz, n_buf, A_KV), cv.reshape(bsz, n_buf, A_KV),
            c0, n0, m0, cbuf, p['conv_w'], p['b_gates'], p['g_mnorm'])
        new_k = new_k.reshape(bsz, n_buf, A_KV_HEADS, A_HEAD_DIM)
        new_v = new_v.reshape(bsz, n_buf, A_KV_HEADS, A_HEAD_DIM)
        m1 = m1[:, :B_HEADS]
    w_out = p['w_out'].astype(wdt)
    x1, h8, eid, gate = proj_router([att, out_b], [w_out[:A_Q], w_out[A_Q:]], None, xf,
                                    row(p['norm_ffn'][0]), p['w_router'][0], p['b_router'][0], precise=precise)
    ys, plan = _moe(h8, eid, gate, p['w_eg'], p['w_eu'], p['w_ed'], 0, precise=precise)

    x2, u = combine_glu(x1, ys, plan, row(p['norm_mix'][1]), p['w_pw1'].astype(wdt), row(p['b_pw1']),
                        precise=precise)
    if prompt:
        yc = dwconv_prompt(u, p['w_dw'], row(p['b_dw']), row(p['ln_g']), row(p['ln_b']), bsz, seq)
        new_cbuf = u.reshape(bsz, seq, d)[:, seq - (C_KERNEL - 1):]
    else:
        yc, new_cbuf = dwconv_sample(u, caches[6], p['w_dw'], row(p['b_dw']), row(p['ln_g']), row(p['ln_b']))
    x3, h8, eid, gate = proj_router([yc], [p['w_pw2'].astype(wdt)], row(p['b_pw2']), x2,
                                    row(p['norm_ffn'][1]), p['w_router'][1], p['b_router'][1], precise=precise)
    ys, plan = _moe(h8, eid, gate, p['w_eg'], p['w_eu'], p['w_ed'], 1, precise=precise)
    y = combine_final(x3, ys, plan, row(p['norm_final']))
    add_layer = lambda t: t[None]
    return (y.reshape(bsz, seq, d),) + tuple(add_layer(t) for t in (new_k, new_v, c1, n1, m1, new_conv, new_cbuf))


def kernel(x_prompt, x_sample, cache_win_k, cache_win_v, state_mlstm_c, state_mlstm_n, state_mlstm_m, state_mlstm_conv, state_conv, norm_mix, norm_ffn, norm_final, rel_bias_table, w_in_mix, b_mlstm_gates, w_mlstm_qk_conv, attn_sinks, g_mlstm_norm, w_out_mix, w_pw1, b_pw1, w_dw, b_dw, ln_conv_g, ln_conv_b, w_pw2, b_pw2, w_router_group, b_router_group, w_router_expert, b_router_expert, w_expert_gate, w_expert_up, w_expert_down):
    w_in = w_in_mix[0]
    s_q, s_k, s_v, s_qk, s_vb, s_g = A_Q, A_Q + A_KV, A_Q + 2 * A_KV, A_Q + 2 * A_KV + 2 * B_QK, \
        A_Q + 2 * A_KV + 2 * B_QK + B_V, A_Q + 2 * A_KV + 2 * B_QK + B_V + 2 * B_HEADS
    w_in_r = jnp.concatenate([w_in[:, :s_q], w_in[:, s_v:s_qk], w_in[:, s_qk:s_vb], w_in[:, s_g:],
                              w_in[:, s_q:s_k], w_in[:, s_k:s_v], w_in[:, s_vb:s_g],
                              jnp.zeros((D_MODEL, LANES - 2 * B_HEADS), F32)], axis=1)
    b_gates = jnp.concatenate([b_mlstm_gates[0], jnp.zeros((LANES - 2 * B_HEADS,), F32)]).reshape(1, LANES)
    depth = w_router_group.shape[0]
    w_re = jnp.transpose(w_router_expert, (0, 2, 1, 3)).reshape(depth, D_MODEL, N_EXPERTS)
    w_router = jnp.concatenate([w_router_group, w_re,
                                jnp.zeros((depth, D_MODEL, LANES - N_GROUPS - N_EXPERTS), F32)], axis=-1)
    b_router = jnp.concatenate([b_router_group, b_router_expert.reshape(depth, N_EXPERTS),
                                jnp.zeros((depth, LANES - N_GROUPS - N_EXPERTS), F32)], axis=-1)[:, None, :]
    p = dict(norm_mix=norm_mix, norm_ffn=norm_ffn, norm_final=norm_final, rel_table=rel_bias_table,
             sinks=attn_sinks[0], w_in=w_in_r, b_gates=b_gates, conv_w=w_mlstm_qk_conv[0],
             g_mnorm=g_mlstm_norm[0].reshape(1, B_V), w_out=w_out_mix[0], w_pw1=w_pw1[0], b_pw1=b_pw1[0],
             w_dw=w_dw[0], b_dw=b_dw[0], ln_g=ln_conv_g[0], ln_b=ln_conv_b[0], w_pw2=w_pw2[0], b_pw2=b_pw2[0],
             w_router=w_router, b_router=b_router, w_eg=w_expert_gate, w_eu=w_expert_up, w_ed=w_expert_down)
    caches = (cache_win_k[0], cache_win_v[0], state_mlstm_c[0], state_mlstm_n[0], state_mlstm_m[0],
              state_mlstm_conv[0], state_conv[0])
    out_p = _trunk(x_prompt, None, p, prompt=True)
    out_s = _trunk(x_sample, caches, p, prompt=False)
    return (out_p[0], out_s[0]) + out_p[1:] + out_s[1:]
```

```python
import functools
import math

import numpy as np
import jax
import jax.numpy as jnp
from jax import lax
from jax.experimental import pallas as pl
from jax.experimental.pallas import tpu as pltpu

F32 = jnp.float32
BF16 = jnp.bfloat16
I32 = jnp.int32
HI = lax.Precision.HIGHEST
NEG_INF = float("-inf")

LANES = 128
SUBLANES = 8
VMEM_LIMIT = 56 * 1024 * 1024

D_MODEL = 1024
A_HEADS = 8
A_KV_HEADS = 2
A_GROUP = A_HEADS // A_KV_HEADS
A_HEAD_DIM = 64
WINDOW = 128
REL_BUCKETS = 32
REL_MAX_DIST = 128
B_HEADS = 4
B_DK = 64
B_DV = 128
B_CONV = 4
C_KERNEL = 31
N_GROUPS = 4
EXPERTS_PER_GROUP = 8
N_EXPERTS = N_GROUPS * EXPERTS_PER_GROUP
TOP_K = 2
EXPERT_FF = D_MODEL // 2
EXPERT_BLOCK = 256
EXPERT_BLOCK_PRECISE = 128
EPS = 1e-6

A_Q = A_HEADS * A_HEAD_DIM
A_KV = A_KV_HEADS * A_HEAD_DIM
B_QK = B_HEADS * B_DK
B_V = B_HEADS * B_DV
ROW_TILES = D_MODEL // LANES

Z_QA, Z_QK, Z_VB, Z_OG, Z_KA, Z_VA, Z_GATES = 0, 512, 1024, 1536, 2048, 2176, 2304
Z_COLS = 2432
MLSTM_CHUNK = 128


def _cparams(*sem):
    return pltpu.CompilerParams(dimension_semantics=sem, vmem_limit_bytes=VMEM_LIMIT)


def _rms(x, g):
    return x * lax.rsqrt(jnp.mean(x * x, axis=-1, keepdims=True) + EPS) * g


def _mm(a, w, precise):
    if precise:
        return jnp.dot(a.astype(F32), w, precision=HI, preferred_element_type=F32)
    return jnp.dot(a.astype(BF16), w, preferred_element_type=F32)


def _t5_buckets(dist):
    exact = REL_BUCKETS // 2
    d = np.maximum(dist, 0)
    large = exact + (np.log(np.maximum(d, 1).astype(np.float32) / exact)
                     / math.log(REL_MAX_DIST / exact) * (REL_BUCKETS - exact)).astype(np.int32)
    return np.where(d < exact, d, np.minimum(large, REL_BUCKETS - 1)).astype(np.int32)


def _norm_proj_body(x_ref, g_ref, w_ref, o_ref, *, precise):
    h = _rms(x_ref[...], g_ref[...])
    o_ref[...] = _mm(h, w_ref[...], precise)


def norm_proj(x, g, w, *, precise):
    rows, d = x.shape
    cols = w.shape[1]
    tm = min(rows, 128 if precise else 512)
    return pl.pallas_call(
        functools.partial(_norm_proj_body, precise=precise),
        grid=(rows // tm,),
        in_specs=[pl.BlockSpec((tm, d), lambda i: (i, 0)),
                  pl.BlockSpec((1, d), lambda i: (0, 0)),
                  pl.BlockSpec((d, cols), lambda i: (0, 0))],
        out_specs=pl.BlockSpec((tm, cols), lambda i: (i, 0)),
        out_shape=jax.ShapeDtypeStruct((rows, cols), F32),
        compiler_params=_cparams("parallel"),
        name="norm_proj",
    )(x, g, w)


def _attn_prompt_body(tab_ref, sink_ref, bkt_ref, q_ref, kp_ref, kc_ref, vp_ref, vc_ref, o_ref, bias_ref):
    b = pl.program_id(0)
    n = pl.program_id(1)

    @pl.when((b == 0) & (n == 0))
    def _():
        bk = bkt_ref[...]
        for h in range(A_HEADS):
            acc = jnp.full(bk.shape, NEG_INF, F32)
            for t in range(REL_BUCKETS):
                acc = jnp.where(bk == t, tab_ref[t, h], acc)
            bias_ref[h] = acc

    q = q_ref[...] * (A_HEAD_DIM ** -0.5)
    kb = jnp.concatenate([kp_ref[...], kc_ref[...]], axis=0).astype(BF16)
    vb = jnp.concatenate([vp_ref[...], vc_ref[...]], axis=0).astype(BF16)
    col = lax.broadcasted_iota(I32, (WINDOW, 2 * WINDOW), 1)
    dead = col < jnp.where(n == 0, WINDOW, 0)
    outs = []
    for h in range(A_HEADS):
        kvh = h // A_GROUP
        qh = q[:, h * A_HEAD_DIM:(h + 1) * A_HEAD_DIM].astype(BF16)
        kh = kb[:, kvh * A_HEAD_DIM:(kvh + 1) * A_HEAD_DIM]
        vh = vb[:, kvh * A_HEAD_DIM:(kvh + 1) * A_HEAD_DIM]
        s = lax.dot_general(qh, kh, (((1,), (1,)), ((), ())), preferred_element_type=F32)
        s = jnp.where(dead, NEG_INF, s + bias_ref[h])
        sink = sink_ref[h]
        mx = jnp.maximum(jnp.max(s, axis=-1, keepdims=True), sink)
        p = jnp.exp(s - mx)
        den = jnp.sum(p, axis=-1, keepdims=True) + jnp.exp(sink - mx)
        outs.append(jnp.dot(p.astype(BF16), vh, preferred_element_type=F32) / den)
    o_ref[...] = jnp.concatenate(outs, axis=1).astype(o_ref.dtype)


def attn_prompt(z, rel_table, sinks, bsz, seq):
    nb = seq // WINDOW
    dist = WINDOW + np.arange(WINDOW)[:, None] - np.arange(2 * WINDOW)[None, :]
    bkt = np.where((dist >= 0) & (dist <= WINDOW), _t5_buckets(dist), -1).astype(np.int32)
    kcol, vcol = Z_KA // LANES, Z_VA // LANES
    smem = pl.BlockSpec(memory_space=pltpu.SMEM)

    def cur(c):
        return pl.BlockSpec((WINDOW, LANES), lambda b, n: (b * nb + n, c))

    def prev(c):
        return pl.BlockSpec((WINDOW, LANES), lambda b, n: (b * nb + jnp.maximum(n - 1, 0), c))

    return pl.pallas_call(
        _attn_prompt_body,
        grid=(bsz, nb),
        in_specs=[smem, smem,
                  pl.BlockSpec((WINDOW, 2 * WINDOW), lambda b, n: (0, 0)),
                  pl.BlockSpec((WINDOW, A_Q), lambda b, n: (b * nb + n, 0)),
                  prev(kcol), cur(kcol), prev(vcol), cur(vcol)],
        out_specs=pl.BlockSpec((WINDOW, A_Q), lambda b, n: (b * nb + n, 0)),
        out_shape=jax.ShapeDtypeStruct((bsz * seq, A_Q), BF16),
        scratch_shapes=[pltpu.VMEM((A_HEADS, WINDOW, 2 * WINDOW), F32)],
        compiler_params=_cparams("arbitrary", "arbitrary"),
        name="attn_prompt",
    )(rel_table, sinks, jnp.asarray(bkt), z, z, z, z, z)


def _log_sigmoid(x):
    return -(jnp.maximum(-x, 0.0) + jnp.log1p(jnp.exp(-jnp.abs(x))))


def _mlstm_prompt_body(qk_ref, v_ref, g_ref, og_ref, cw_ref, bg_ref, gn_ref,
                       ob_ref, c_out, n_out, m_out,
                       c_sc, n_sc, m_sc, hist_sc, *, L, nc):
    ci = pl.program_id(1)

    @pl.when(ci == 0)
    def _():
        c_sc[...] = jnp.zeros(c_sc.shape, F32)
        n_sc[...] = jnp.zeros(n_sc.shape, F32)
        m_sc[...] = jnp.full(m_sc.shape, NEG_INF, F32)
        hist_sc[...] = jnp.zeros(hist_sc.shape, F32)

    cur = qk_ref[...]
    ext = jnp.concatenate([hist_sc[...], cur], axis=0)
    cw = cw_ref[...]
    off = SUBLANES - (B_CONV - 1)
    conv = ext[off:off + L] * cw[0:1]
    for j in range(1, B_CONV):
        conv = conv + ext[off + j:off + j + L] * cw[j:j + 1]
    hist_sc[...] = cur[L - SUBLANES:L]
    qk = conv * jax.nn.sigmoid(conv)
    q_all = qk[:, :B_QK]
    k_all = qk[:, B_QK:] * (B_DK ** -0.5)
    v_all = v_ref[...]
    og = og_ref[...]
    gn = gn_ref[...]

    G = g_ref[...] + bg_ref[...]
    lf = _log_sigmoid(G)
    row = lax.broadcasted_iota(I32, (L, L), 0)
    colm = lax.broadcasted_iota(I32, (L, L), 1)
    causal = colm <= row
    Bc = jnp.dot(causal.astype(F32), lf, precision=HI, preferred_element_type=F32)
    BT = Bc.T
    GT = G.T

    outs = []
    for h in range(B_HEADS):
        qh = q_all[:, h * B_DK:(h + 1) * B_DK]
        kh = k_all[:, h * B_DK:(h + 1) * B_DK]
        vh = v_all[:, h * B_DV:(h + 1) * B_DV]
        b_col = Bc[:, B_HEADS + h:B_HEADS + h + 1]
        ig_col = G[:, h:h + 1]
        b_row = BT[B_HEADS + h:B_HEADS + h + 1, :]
        ig_row = GT[h:h + 1, :]
        c0 = c_sc[h]
        n0 = n_sc[h:h + 1, :]
        m0 = m_sc[h:h + 1, 0:1]
        a = b_col + m0
        d = jnp.where(causal, b_col - b_row + ig_row, NEG_INF)
        m = jnp.maximum(a, jnp.max(d, axis=-1, keepdims=True))
        dw = jnp.exp(d - m)
        aw = jnp.exp(a - m)
        qb = qh.astype(BF16)
        vb = vh.astype(BF16)
        s = lax.dot_general(qb, kh.astype(BF16), (((1,), (1,)), ((), ())), preferred_element_type=F32) * dw
        num = (jnp.dot(s.astype(BF16), vb, preferred_element_type=F32)
               + aw * jnp.dot(qb, c0.astype(BF16), preferred_element_type=F32))
        den = jnp.sum(s, axis=-1, keepdims=True) + aw * jnp.sum(qh * n0, axis=-1, keepdims=True)
        hh = num / jnp.maximum(jnp.abs(den), jnp.exp(-m))
        m_last = m[L - 1:L, :]
        wl = jnp.exp(b_col[L - 1:L, :] - b_col + ig_col - m_last)
        decay = aw[L - 1:L, :]
        kw = kh * wl
        c_sc[h] = decay * c0 + lax.dot_general(kw.astype(BF16), vb, (((0,), (0,)), ((), ())),
                                               preferred_element_type=F32)
        n_sc[h:h + 1, :] = decay * n0 + jnp.sum(kw, axis=0, keepdims=True)
        m_sc[h:h + 1, :] = jnp.broadcast_to(m_last, (1, LANES))
        hn = hh * lax.rsqrt(jnp.mean(hh * hh, axis=-1, keepdims=True) + EPS) * gn[:, h * B_DV:(h + 1) * B_DV]
        outs.append(jax.nn.sigmoid(og[:, h * B_DV:(h + 1) * B_DV]) * hn)
    ob_ref[...] = jnp.concatenate(outs, axis=1).astype(ob_ref.dtype)

    @pl.when(ci == nc - 1)
    def _():
        c_out[0] = c_sc[...]
        n_out[0] = n_sc[...]
        m_out[0] = m_sc[...]


def mlstm_prompt(z, conv_w, b_gates_pad, g_norm, bsz, seq):
    L = MLSTM_CHUNK
    nc = seq // L

    def zspec(width, colblk):
        return pl.BlockSpec((L, width), lambda b, c: (b * nc + c, colblk))

    const = lambda shape: pl.BlockSpec(shape, lambda b, c: (0,) * len(shape))
    return pl.pallas_call(
        functools.partial(_mlstm_prompt_body, L=L, nc=nc),
        grid=(bsz, nc),
        in_specs=[zspec(2 * B_QK, Z_QK // (2 * B_QK)), zspec(B_V, Z_VB // B_V), zspec(LANES, Z_GATES // LANES),
                  zspec(B_V, Z_OG // B_V), const((B_CONV, 2 * B_QK)), const((1, LANES)), const((1, B_V))],
        out_specs=[pl.BlockSpec((L, B_V), lambda b, c: (b * nc + c, 0)),
                   pl.BlockSpec((1, B_HEADS, B_DK, B_DV), lambda b, c: (b, 0, 0, 0)),
                   pl.BlockSpec((1, SUBLANES, B_DK), lambda b, c: (b, 0, 0)),
                   pl.BlockSpec((1, SUBLANES, LANES), lambda b, c: (b, 0, 0))],
        out_shape=[jax.ShapeDtypeStruct((bsz * seq, B_V), BF16),
                   jax.ShapeDtypeStruct((bsz, B_HEADS, B_DK, B_DV), F32),
                   jax.ShapeDtypeStruct((bsz, SUBLANES, B_DK), F32),
                   jax.ShapeDtypeStruct((bsz, SUBLANES, LANES), F32)],
        scratch_shapes=[pltpu.VMEM((B_HEADS, B_DK, B_DV), F32), pltpu.VMEM((SUBLANES, B_DK), F32),
                        pltpu.VMEM((SUBLANES, LANES), F32), pltpu.VMEM((SUBLANES, 2 * B_QK), F32)],
        compiler_params=_cparams("arbitrary", "arbitrary"),
        name="mlstm_prompt",
    )(z, z, z, z, conv_w, b_gates_pad, g_norm)


SAMPLE_TILE = 8


def _row_to_col(row, n):
    eye = lax.broadcasted_iota(I32, (n, n), 0) == lax.broadcasted_iota(I32, (n, n), 1)
    return jnp.sum(jnp.where(eye, jnp.broadcast_to(row, (n, n)), 0.0), axis=1, keepdims=True)


def _mix_sample_body(tab_ref, sink_ref, bkt_ref, z_ref, ck_ref, cv_ref, c0_ref, n0_ref, m0_ref, cb_ref,
                     cw_ref, bg_ref, gn_ref,
                     att_ref, ob_ref, nk_ref, nv_ref, c1_ref, n1_ref, m1_ref, ncb_ref, *, tb):
    bk = bkt_ref[...]
    cw = cw_ref[...]
    gn = gn_ref[...]
    zero_half = jnp.zeros((1, A_HEAD_DIM), F32)
    bias_cols = []
    for h in range(A_HEADS):
        bias = jnp.zeros(bk.shape, F32)
        for t in range(REL_BUCKETS):
            bias = jnp.where(bk == t, tab_ref[t, h], bias)
        bias_cols.append(bias)
    att_rows, ob_rows, m_rows = [], [], []
    for i in range(tb):
        zr = z_ref[i:i + 1, :]
        q_att = zr[:, Z_QA:Z_QA + A_Q] * (A_HEAD_DIM ** -0.5)
        ka = zr[:, Z_KA:Z_KA + A_KV]
        va = zr[:, Z_VA:Z_VA + A_KV]
        kc = ck_ref[i]
        vc = cv_ref[i]
        nk_ref[i] = jnp.concatenate([kc[1:], ka], axis=0)
        nv_ref[i] = jnp.concatenate([vc[1:], va], axis=0)
        heads = []
        for h in range(A_HEADS):
            kvh = h // A_GROUP
            qh = q_att[:, h * A_HEAD_DIM:(h + 1) * A_HEAD_DIM]
            qrow = jnp.concatenate([qh, zero_half] if kvh == 0 else [zero_half, qh], axis=1)
            lc = jnp.sum(kc * qrow, axis=-1, keepdims=True) + bias_cols[h]
            ln = jnp.sum(ka * qrow, axis=-1, keepdims=True) + tab_ref[0, h]
            sink = sink_ref[h]
            mx = jnp.maximum(jnp.maximum(jnp.max(lc, axis=0, keepdims=True), ln), sink)
            pc = jnp.exp(lc - mx)
            pn = jnp.exp(ln - mx)
            den = jnp.sum(pc, axis=0, keepdims=True) + pn + jnp.exp(sink - mx)
            o = (jnp.sum(pc * vc, axis=0, keepdims=True) + pn * va) / den
            heads.append(o[:, kvh * A_HEAD_DIM:(kvh + 1) * A_HEAD_DIM])
        att_rows.append(jnp.concatenate(heads, axis=1))

        qk_pre = zr[:, Z_QK:Z_QK + 2 * B_QK]
        hist = cb_ref[i]
        conv = qk_pre * cw[B_CONV - 1:B_CONV]
        for j in range(B_CONV - 1):
            conv = conv + hist[j:j + 1] * cw[j:j + 1]
        ncb_ref[i] = jnp.concatenate([hist[1:], qk_pre], axis=0)
        qk = conv * jax.nn.sigmoid(conv)
        G = zr[:, Z_GATES:Z_GATES + LANES] + bg_ref[...]
        lfr = _log_sigmoid(G)
        og = zr[:, Z_OG:Z_OG + B_V]
        v_pre = zr[:, Z_VB:Z_VB + B_V]
        obs, ms = [], []
        for h in range(B_HEADS):
            qh = qk[:, h * B_DK:(h + 1) * B_DK]
            kh = qk[:, B_QK + h * B_DK:B_QK + (h + 1) * B_DK] * (B_DK ** -0.5)
            vh = v_pre[:, h * B_DV:(h + 1) * B_DV]
            ig = G[:, h:h + 1]
            lf = lfr[:, B_HEADS + h:B_HEADS + h + 1]
            c0 = c0_ref[i, h]
            n0 = n0_ref[i, h:h + 1, :]
            m0 = m0_ref[i:i + 1, h:h + 1]
            a = lf + m0
            m = jnp.maximum(a, ig)
            dw = jnp.exp(ig - m)
            aw = jnp.exp(a - m)
            s = jnp.sum(qh * kh, axis=-1, keepdims=True) * dw
            q_col = _row_to_col(qh, B_DK)
            k_col = _row_to_col(kh, B_DK)
            num = s * vh + aw * jnp.sum(q_col * c0, axis=0, keepdims=True)
            den = s + aw * jnp.sum(qh * n0, axis=-1, keepdims=True)
            hh = num / jnp.maximum(jnp.abs(den), jnp.exp(-m))
            c1_ref[i, h] = aw * c0 + dw * (k_col * vh)
            n1_ref[i, h:h + 1, :] = aw * n0 + dw * kh
            ms.append(m)
            hn = hh * lax.rsqrt(jnp.mean(hh * hh, axis=-1, keepdims=True) + EPS) * gn[:, h * B_DV:(h + 1) * B_DV]
            obs.append(jax.nn.sigmoid(og[:, h * B_DV:(h + 1) * B_DV]) * hn)
        ob_rows.append(jnp.concatenate(obs, axis=1))
        lane = lax.broadcasted_iota(I32, (1, LANES), 1)
        mrow = jnp.zeros((1, LANES), F32)
        for h in range(B_HEADS):
            mrow = jnp.where(lane == h, ms[h], mrow)
        m_rows.append(mrow)
    att_ref[...] = jnp.concatenate(att_rows, axis=0)
    ob_ref[...] = jnp.concatenate(ob_rows, axis=0)
    m1_ref[...] = jnp.concatenate(m_rows, axis=0)


def mix_sample(z, rel_table, sinks, ck, cv, c0, n0, m0, conv_buf, conv_w, b_gates_pad, g_norm):
    nb = z.shape[0]
    tb = min(SAMPLE_TILE, nb)
    n_buf = ck.shape[1]
    bkt = _t5_buckets(n_buf - np.arange(n_buf))[:, None]
    smem = pl.BlockSpec(memory_space=pltpu.SMEM)
    const = lambda shape: pl.BlockSpec(shape, lambda i: (0,) * len(shape))
    lead = lambda shape: pl.BlockSpec((tb,) + shape, lambda i: (i,) + (0,) * len(shape))
    return pl.pallas_call(
        functools.partial(_mix_sample_body, tb=tb),
        grid=(nb // tb,),
        in_specs=[smem, smem, const((n_buf, 1)), lead((Z_COLS,)), lead((n_buf, A_KV)), lead((n_buf, A_KV)),
                  lead((B_HEADS, B_DK, B_DV)), lead((B_HEADS, B_DK)), lead((B_HEADS,)),
                  lead((B_CONV - 1, 2 * B_QK)), const((B_CONV, 2 * B_QK)), const((1, LANES)), const((1, B_V))],
        out_specs=[lead((A_Q,)), lead((B_V,)), lead((n_buf, A_KV)), lead((n_buf, A_KV)),
                   lead((B_HEADS, B_DK, B_DV)), lead((B_HEADS, B_DK)), lead((LANES,)),
                   lead((B_CONV - 1, 2 * B_QK))],
        out_shape=[jax.ShapeDtypeStruct((nb, A_Q), F32), jax.ShapeDtypeStruct((nb, B_V), F32),
                   jax.ShapeDtypeStruct(ck.shape, F32), jax.ShapeDtypeStruct(cv.shape, F32),
                   jax.ShapeDtypeStruct(c0.shape, F32), jax.ShapeDtypeStruct(n0.shape, F32),
                   jax.ShapeDtypeStruct((nb, LANES), F32), jax.ShapeDtypeStruct(conv_buf.shape, F32)],
        compiler_params=_cparams("parallel"),
        name="mix_sample",
    )(rel_table, sinks, jnp.asarray(bkt), z, ck, cv, c0, n0, m0, conv_buf, conv_w, b_gates_pad, g_norm)


def _store_row_tiles(ref, val, rows):
    for s in range(ROW_TILES):
        ref[pl.ds(s, rows, stride=ROW_TILES), :] = val[:, s * LANES:(s + 1) * LANES]


def _load_row_tiles(ref, rows, start=0, stride=ROW_TILES):
    return jnp.concatenate([ref[pl.ds(start + s, rows, stride=stride), :] for s in range(ROW_TILES)], axis=1)


def _route(logits):
    lane = lax.broadcasted_iota(I32, logits.shape, 1)
    big = jnp.int32(1 << 20)
    gl = jnp.where(lane < N_GROUPS, logits, NEG_INF)
    gmax = jnp.max(gl, axis=-1, keepdims=True)
    gidx = jnp.min(jnp.where(gl == gmax, lane, big), axis=-1, keepdims=True)
    g_gate = 1.0 / jnp.sum(jnp.exp(gl - gmax), axis=-1, keepdims=True)
    lo = N_GROUPS + gidx * EXPERTS_PER_GROUP
    el = jnp.where((lane >= lo) & (lane < lo + EXPERTS_PER_GROUP), logits, NEG_INF)
    v1 = jnp.max(el, axis=-1, keepdims=True)
    i1 = jnp.min(jnp.where(el == v1, lane, big), axis=-1, keepdims=True)
    el2 = jnp.where(lane == i1, NEG_INF, el)
    v2 = jnp.max(el2, axis=-1, keepdims=True)
    i2 = jnp.min(jnp.where(el2 == v2, lane, big), axis=-1, keepdims=True)
    t = jnp.exp(v2 - v1)
    w1 = g_gate / (1.0 + t)
    w2 = g_gate * t / (1.0 + t)
    gate = jnp.where(lane == 0, w1, jnp.where(lane == 1, w2, 0.0))
    return i1 - N_GROUPS, i2 - N_GROUPS, gate


def _local_sort(e0, e1):
    tm = e0.shape[0]
    lane = lax.broadcasted_iota(I32, (tm, LANES), 1)
    oh0 = (lane == e0).astype(BF16)
    oh1 = (lane == e1).astype(BF16)
    r = lax.broadcasted_iota(I32, (tm, tm), 0)
    c = lax.broadcasted_iota(I32, (tm, tm), 1)
    before = (c < r).astype(BF16)
    cnt0 = jnp.sum(oh0.astype(F32), axis=0, keepdims=True)
    run_n = cnt0 + jnp.sum(oh1.astype(F32), axis=0, keepdims=True)
    er = lax.broadcasted_iota(I32, (LANES, LANES), 0)
    ec = lax.broadcasted_iota(I32, (LANES, LANES), 1)
    run_l = jnp.dot(run_n, (er < ec).astype(F32), precision=HI, preferred_element_type=F32)
    w0 = jnp.dot(before, oh0, preferred_element_type=F32) + run_l
    w1 = jnp.dot(before, oh1, preferred_element_type=F32) + run_l + cnt0
    p0 = jnp.sum(jnp.where(lane == e0, w0, 0.0), axis=-1, keepdims=True)
    p1 = jnp.sum(jnp.where(lane == e1, w1, 0.0), axis=-1, keepdims=True)
    lpos = jnp.where(lane == 0, p0, jnp.where(lane == 1, p1, 0.0)).astype(I32)
    return lpos, run_n


MOE_CHUNK = 512


def _moe_chunk(rows, precise):
    return min(rows, 128 if precise else MOE_CHUNK)


def _proj_router_body(*refs, n_in, has_bias, precise, tm):
    a_refs = refs[:n_in]
    w_refs = refs[n_in:2 * n_in]
    k = 2 * n_in
    bias_ref = refs[k] if has_bias else None
    k += 1 if has_bias else 0
    x_ref, g_ref, wr_ref, br_ref, x1_ref, h8_ref, lpos_ref, gate_ref, runn_ref = refs[k:]
    acc = x_ref[...]
    if has_bias:
        acc = acc + bias_ref[...]
    for a_ref, w_ref in zip(a_refs, w_refs):
        acc = acc + _mm(a_ref[...], w_ref[...], precise)
    x1_ref[...] = acc
    h = _rms(acc, g_ref[...])
    _store_row_tiles(h8_ref, h, tm)
    logits = jnp.dot(h, wr_ref[...], precision=HI, preferred_element_type=F32) + br_ref[...]
    e0, e1, gate = _route(logits)
    lpos, run_n = _local_sort(e0, e1)
    lpos_ref[...] = lpos
    gate_ref[...] = gate
    runn_ref[...] = jnp.broadcast_to(run_n, runn_ref.shape).astype(I32)


def proj_router(a_list, w_list, bias, x, g, wr, br, *, precise):
    rows, d = x.shape
    tm = _moe_chunk(rows, precise)
    n_in = len(a_list)
    row_spec = lambda width: pl.BlockSpec((tm, width), lambda i: (i, 0))
    const = lambda shape: pl.BlockSpec(shape, lambda i: (0,) * len(shape))
    in_specs = [row_spec(a.shape[1]) for a in a_list] + [const(w.shape) for w in w_list]
    args = list(a_list) + list(w_list)
    if bias is not None:
        in_specs.append(const((1, d)))
        args.append(bias)
    in_specs += [row_spec(d), const((1, d)), const((d, LANES)), const((1, LANES))]
    args += [x, g, wr, br]
    return pl.pallas_call(
        functools.partial(_proj_router_body, n_in=n_in, has_bias=bias is not None, precise=precise, tm=tm),
        grid=(rows // tm,),
        in_specs=in_specs,
        out_specs=[row_spec(d), pl.BlockSpec((tm * ROW_TILES, LANES), lambda i: (i, 0)),
                   row_spec(LANES), row_spec(LANES), pl.BlockSpec((SUBLANES, LANES), lambda i: (i, 0))],
        out_shape=[jax.ShapeDtypeStruct((rows, d), F32), jax.ShapeDtypeStruct((rows * ROW_TILES, LANES), F32),
                   jax.ShapeDtypeStruct((rows, LANES), I32), jax.ShapeDtypeStruct((rows, LANES), F32),
                   jax.ShapeDtypeStruct((rows // tm * SUBLANES, LANES), I32)],
        compiler_params=_cparams("parallel"),
        name="proj_router",
    )(*args)


def _tile_rows(r, n=1):
    return pl.ds(pl.multiple_of(r * ROW_TILES, ROW_TILES), n * ROW_TILES)


def _pow2_pieces(limit):
    p = 1
    while p * 2 <= limit:
        p *= 2
    out = []
    while p >= 1:
        out.append(p)
        p //= 2
    return out


COMMON_PIECE = 32


def _for_each_piece(n, pieces, fn):
    def emit(ps):
        for p in ps:
            @pl.when((n & p) != 0)
            def _(p=p):
                fn(n & ~(2 * p - 1), p)

    big = [p for p in pieces if p > COMMON_PIECE]
    if big:
        @pl.when(n > 2 * COMMON_PIECE - 1)
        def _():
            emit(big)
    emit([p for p in pieces if p <= COMMON_PIECE])


def _dispatch_body(rn_ref, rg_ref, rl_ref, ps_ref, pn_ref, tail_ref, lpos_ref, h8_ref, xs_hbm,
                   stage, zbuf, sem, zsem, *, chunk, nch, blk):
    c = pl.program_id(0)
    slot = c % 2
    run_pieces = _pow2_pieces(chunk)
    pad_pieces = _pow2_pieces(blk - 1)

    @pl.when(c == 0)
    def _():
        zbuf[...] = jnp.zeros(zbuf.shape, F32)

        def pad_dmas(e, op):
            def one(off, p):
                cp = pltpu.make_async_copy(zbuf.at[_tile_rows(0, p)], xs_hbm.at[_tile_rows(ps_ref[e] + off, p)], zsem)
                cp.start() if op == 0 else cp.wait()
            _for_each_piece(pn_ref[e], pad_pieces, one)

        def issue(e, carry):
            pad_dmas(e, 0)
            return carry

        def wait(e, carry):
            pad_dmas(e, 1)
            return carry
        lax.fori_loop(0, N_EXPERTS, issue, 0)
        lax.fori_loop(0, N_EXPERTS, wait, 0)

        half = blk // 2

        def tail_dmas(i, op):
            cp = pltpu.make_async_copy(zbuf, xs_hbm.at[_tile_rows(tail_ref[0] + i * half, half)], zsem)
            cp.start() if op == 0 else cp.wait()

        def tail_issue(i, carry):
            tail_dmas(i, 0)
            return carry

        def tail_wait(i, carry):
            tail_dmas(i, 1)
            return carry
        lax.fori_loop(0, tail_ref[1], tail_issue, 0)
        lax.fori_loop(0, tail_ref[1], tail_wait, 0)

    def copy_tok(t, carry):
        row = h8_ref[_tile_rows(t), :]
        stage[slot, _tile_rows(lpos_ref[0, 0, TOP_K * t]), :] = row
        stage[slot, _tile_rows(lpos_ref[0, 0, TOP_K * t + 1]), :] = row
        return carry
    lax.fori_loop(0, chunk, copy_tok, 0, unroll=8)

    @pl.when(c > 0)
    def _():
        pltpu.make_async_copy(stage.at[1 - slot], stage.at[1 - slot], sem).wait()

    def send_runs(e, carry):
        k = c * N_EXPERTS + e

        def one(off, p):
            pltpu.make_async_copy(stage.at[slot, _tile_rows(rl_ref[k] + off, p)],
                                  xs_hbm.at[_tile_rows(rg_ref[k] + off, p)], sem).start()
        _for_each_piece(rn_ref[k], run_pieces, one)
        return carry
    lax.fori_loop(0, N_EXPERTS, send_runs, 0)

    @pl.when(c == nch - 1)
    def _():
        pltpu.make_async_copy(stage.at[slot], stage.at[slot], sem).wait()


def dispatch(h8, plan):
    chunk, nch, blk = plan['chunk'], plan['nch'], plan['blk']
    n_slots = plan['nblk'] * blk
    gs = pltpu.PrefetchScalarGridSpec(
        num_scalar_prefetch=6,
        grid=(nch,),
        in_specs=[pl.BlockSpec((1, 1, TOP_K * chunk), lambda c, *_: (c, 0, 0), memory_space=pltpu.SMEM),
                  pl.BlockSpec((chunk * ROW_TILES, LANES), lambda c, *_: (c, 0))],
        out_specs=pl.BlockSpec(memory_space=pl.ANY),
        scratch_shapes=[pltpu.VMEM((2, TOP_K * chunk * ROW_TILES, LANES), F32),
                        pltpu.VMEM((blk // 2 * ROW_TILES, LANES), F32),
                        pltpu.SemaphoreType.DMA(()), pltpu.SemaphoreType.DMA(())],
    )
    return pl.pallas_call(
        functools.partial(_dispatch_body, chunk=chunk, nch=nch, blk=blk),
        grid_spec=gs,
        out_shape=jax.ShapeDtypeStruct((n_slots * ROW_TILES, LANES), F32),
        compiler_params=_cparams("arbitrary"),
        name="dispatch",
    )(plan['run_n'], plan['run_g'], plan['run_l'], plan['pad_start'], plan['pad_n'], plan['tail'], plan['lpos'], h8)


def _experts_body(be_ref, nv_ref, xs_ref, wg_ref, wu_ref, wd_ref, ys_ref, *wcast, precise, blk):
    j = pl.program_id(0)

    @pl.when(j < nv_ref[0])
    def _():
        x = _load_row_tiles(xs_ref, blk)
        if precise:
            wg, wu, wd = wg_ref[0], wu_ref[0], wd_ref[0]
        else:
            wgb, wub, wdb = wcast

            @pl.when((j == 0) | (be_ref[j] != be_ref[jnp.maximum(j - 1, 0)]))
            def _():
                wgb[...] = wg_ref[0].astype(BF16)
                wub[...] = wu_ref[0].astype(BF16)
                wdb[...] = wd_ref[0].astype(BF16)

            wg, wu, wd = wgb[...], wub[...], wdb[...]
        gt = _mm(x, wg, precise)
        up = _mm(x, wu, precise)
        hid = gt * jax.nn.sigmoid(gt) * up
        _store_row_tiles(ys_ref, _mm(hid, wd, precise), blk)

    @pl.when(j >= nv_ref[0])
    def _():
        ys_ref[...] = jnp.zeros(ys_ref.shape, F32)


def experts(xs, plan, wg, wu, wd, layer, *, precise):
    nblk, rows = plan['nblk'], plan['blk'] * ROW_TILES
    d, ff = wg.shape[2], wg.shape[3]
    blk = lambda j, be, nv: (jnp.minimum(j, nv[0] - 1), 0)
    wspec = lambda shape: pl.BlockSpec((None, 1) + shape,
                                       lambda j, be, nv: (layer, be[jnp.minimum(j, nv[0] - 1)], 0, 0))
    gs = pltpu.PrefetchScalarGridSpec(
        num_scalar_prefetch=2,
        grid=(nblk,),
        in_specs=[pl.BlockSpec((rows, LANES), blk),
                  wspec((d, ff)), wspec((d, ff)), wspec((ff, d))],
        out_specs=pl.BlockSpec((rows, LANES), lambda j, be, nv: (j, 0)),
        scratch_shapes=[] if precise else [pltpu.VMEM((d, ff), BF16), pltpu.VMEM((d, ff), BF16),
                                           pltpu.VMEM((ff, d), BF16)],
    )
    return pl.pallas_call(
        functools.partial(_experts_body, precise=precise, blk=plan['blk']),
        grid_spec=gs,
        out_shape=jax.ShapeDtypeStruct(xs.shape, F32),
        compiler_params=_cparams("arbitrary"),
        name="experts",
    )(plan['block_e'], plan['n_used'], xs, wg, wu, wd)


def moe_plan(lpos128, gate128, runn8, blk):
    n_tok = lpos128.shape[0]
    nch = runn8.shape[0] // SUBLANES
    chunk = n_tok // nch
    n_assign = n_tok * TOP_K
    nblk = (n_assign + N_EXPERTS * (blk - 1)) // blk
    run_n = runn8.reshape(nch, SUBLANES, LANES)[:, 0, :N_EXPERTS]
    lpos = lpos128[:, :TOP_K]
    gate = gate128[:, :TOP_K]
    counts = jnp.sum(run_n, axis=0)
    padded = (counts + blk - 1) // blk * blk
    pends = jnp.cumsum(padded)
    pstarts = pends - padded
    run_g = pstarts[None, :] + jnp.cumsum(run_n, axis=0) - run_n
    run_l = jnp.cumsum(run_n, axis=1) - run_n
    blk_start = jnp.arange(nblk, dtype=I32) * blk
    block_e = jnp.minimum(jnp.sum((pends[None, :] <= blk_start[:, None]).astype(I32), axis=1), N_EXPERTS - 1)
    return dict(chunk=chunk, nch=nch, nblk=nblk, blk=blk,
                run_n=run_n.reshape(-1).astype(I32), run_g=run_g.reshape(-1).astype(I32),
                run_l=run_l.reshape(-1).astype(I32), pad_start=(pstarts + counts).astype(I32),
                pad_n=(padded - counts).astype(I32), lpos=lpos.reshape(nch, 1, TOP_K * chunk).astype(I32),
                gate=gate.reshape(nch, 1, TOP_K * chunk), block_e=block_e.astype(I32),
                n_used=(pends[-1:] // blk).astype(I32),
                tail=jnp.stack([pends[-1], 2 * (nblk - pends[-1] // blk)]).astype(I32))


def _combine(rn_ref, rg_ref, rl_ref, lpos_ref, gate_ref, x_ref, ys_hbm, ystage, comb, sem, *, chunk, nch):
    c = pl.program_id(0)
    slot = c % 2
    pieces = _pow2_pieces(chunk)

    def fetch(cc, sl):
        def per_e(e, carry):
            k = cc * N_EXPERTS + e

            def one(off, p):
                pltpu.make_async_copy(ys_hbm.at[_tile_rows(rg_ref[k] + off, p)],
                                      ystage.at[sl, _tile_rows(rl_ref[k] + off, p)], sem.at[sl]).start()
            _for_each_piece(rn_ref[k], pieces, one)
            return carry
        lax.fori_loop(0, N_EXPERTS, per_e, 0)

    @pl.when(c == 0)
    def _():
        fetch(0, 0)

    @pl.when(c + 1 < nch)
    def _():
        fetch(c + 1, 1 - slot)

    pltpu.make_async_copy(ystage.at[slot], ystage.at[slot], sem.at[slot]).wait()

    def per_tok(t, carry):
        y0 = ystage[slot, _tile_rows(lpos_ref[0, 0, TOP_K * t]), :]
        y1 = ystage[slot, _tile_rows(lpos_ref[0, 0, TOP_K * t + 1]), :]
        comb[_tile_rows(t), :] = gate_ref[0, 0, TOP_K * t] * y0 + gate_ref[0, 0, TOP_K * t + 1] * y1
        return carry
    lax.fori_loop(0, chunk, per_tok, 0, unroll=8)
    return x_ref[...] + _load_row_tiles(comb, chunk)


def _combine_glu_body(rn_ref, rg_ref, rl_ref, lpos_ref, gate_ref, x_ref, ys_hbm, g_ref, w_ref, b_ref,
                      x2_ref, u_ref, ystage, comb, sem, *, chunk, nch, precise):
    x2 = _combine(rn_ref, rg_ref, rl_ref, lpos_ref, gate_ref, x_ref, ys_hbm, ystage, comb, sem, chunk=chunk, nch=nch)
    x2_ref[...] = x2
    zz = _mm(_rms(x2, g_ref[...]), w_ref[...], precise) + b_ref[...]
    half = zz.shape[1] // 2
    u_ref[...] = zz[:, :half] * jax.nn.sigmoid(zz[:, half:])


def _combine_final_body(rn_ref, rg_ref, rl_ref, lpos_ref, gate_ref, x_ref, ys_hbm, g_ref, o_ref,
                        ystage, comb, sem, *, chunk, nch):
    x2 = _combine(rn_ref, rg_ref, rl_ref, lpos_ref, gate_ref, x_ref, ys_hbm, ystage, comb, sem, chunk=chunk, nch=nch)
    o_ref[...] = _rms(x2, g_ref[...])


def _combine_call(body, plan, x, ys, extra, extra_specs, out_specs, out_shape, name):
    chunk, nch = plan['chunk'], plan['nch']
    d = x.shape[1]
    smem_blk = pl.BlockSpec((1, 1, TOP_K * chunk), lambda c, *_: (c, 0, 0), memory_space=pltpu.SMEM)
    gs = pltpu.PrefetchScalarGridSpec(
        num_scalar_prefetch=3,
        grid=(nch,),
        in_specs=[smem_blk, smem_blk, pl.BlockSpec((chunk, d), lambda c, *_: (c, 0)),
                  pl.BlockSpec(memory_space=pl.ANY)] + extra_specs,
        out_specs=out_specs,
        scratch_shapes=[pltpu.VMEM((2, TOP_K * chunk * ROW_TILES, LANES), F32),
                        pltpu.VMEM((chunk * ROW_TILES, LANES), F32), pltpu.SemaphoreType.DMA((2,))],
    )
    return pl.pallas_call(
        functools.partial(body, chunk=chunk, nch=nch),
        grid_spec=gs,
        out_shape=out_shape,
        compiler_params=_cparams("arbitrary"),
        name=name,
    )(plan['run_n'], plan['run_g'], plan['run_l'], plan['lpos'], plan['gate'], x, ys, *extra)


def combine_glu(x, ys, plan, g, w, b, *, precise):
    rows, d = x.shape
    chunk = plan['chunk']
    cols = w.shape[1]
    const = lambda shape: pl.BlockSpec(shape, lambda c, *_: (0,) * len(shape))
    row_spec = lambda width: pl.BlockSpec((chunk, width), lambda c, *_: (c, 0))
    return _combine_call(
        functools.partial(_combine_glu_body, precise=precise), plan, x, ys, [g, w, b],
        [const((1, d)), const((d, cols)), const((1, cols))], [row_spec(d), row_spec(cols // 2)],
        [jax.ShapeDtypeStruct((rows, d), F32), jax.ShapeDtypeStruct((rows, cols // 2), F32)], "combine_glu")


def combine_final(x, ys, plan, g):
    rows, d = x.shape
    chunk = plan['chunk']
    return _combine_call(
        _combine_final_body, plan, x, ys, [g], [pl.BlockSpec((1, d), lambda c, *_: (0, 0))],
        pl.BlockSpec((chunk, d), lambda c, *_: (c, 0)), jax.ShapeDtypeStruct((rows, d), F32), "combine_final")


CONV_TILE = 256
CONV_HIST = 32


def _ln_swish(y, g, b):
    yc = y - jnp.mean(y, axis=-1, keepdims=True)
    yn = yc * lax.rsqrt(jnp.mean(yc * yc, axis=-1, keepdims=True) + EPS) * g + b
    return yn * jax.nn.sigmoid(yn)


CONV_ROWS = 64


def _dwconv_prompt_body(u_ref, w_ref, bdw_ref, g_ref, b_ref, o_ref, ext, y_sc, *, tt):
    t = pl.program_id(1)
    n_lt = ext.shape[0]

    @pl.when(t == 0)
    def _():
        ext[:, 0:CONV_HIST, :] = jnp.zeros((n_lt, CONV_HIST, LANES), F32)

    @pl.when(t > 0)
    def _():
        ext[:, 0:CONV_HIST, :] = ext[:, tt:tt + CONV_HIST, :]

    for j in range(n_lt):
        ext[j, CONV_HIST:CONV_HIST + tt, :] = u_ref[:, j * LANES:(j + 1) * LANES]
    off = CONV_HIST - (C_KERNEL - 1)
    for j in range(n_lt):
        wj = w_ref[:, j * LANES:(j + 1) * LANES]
        bj = bdw_ref[:, j * LANES:(j + 1) * LANES]
        for c in range(tt // CONV_ROWS):
            acc = ext[j, pl.ds(off + c * CONV_ROWS, CONV_ROWS), :] * wj[0:1] + bj
            for k in range(1, C_KERNEL):
                acc = acc + ext[j, pl.ds(off + k + c * CONV_ROWS, CONV_ROWS), :] * wj[k:k + 1]
            y_sc[c * CONV_ROWS:(c + 1) * CONV_ROWS, j * LANES:(j + 1) * LANES] = acc
    o_ref[...] = _ln_swish(y_sc[...], g_ref[...], b_ref[...]).astype(o_ref.dtype)


def dwconv_prompt(u, w, b_dw, ln_g, ln_b, bsz, seq):
    tt = min(CONV_TILE, seq)
    nt = seq // tt
    d = u.shape[1]
    const = lambda shape: pl.BlockSpec(shape, lambda b, t: (0,) * len(shape))
    return pl.pallas_call(
        functools.partial(_dwconv_prompt_body, tt=tt),
        grid=(bsz, nt),
        in_specs=[pl.BlockSpec((tt, d), lambda b, t: (b * nt + t, 0)), const((C_KERNEL, d)), const((1, d)),
                  const((1, d)), const((1, d))],
        out_specs=pl.BlockSpec((tt, d), lambda b, t: (b * nt + t, 0)),
        out_shape=jax.ShapeDtypeStruct((bsz * seq, d), BF16),
        scratch_shapes=[pltpu.VMEM((d // LANES, CONV_HIST + tt, LANES), F32), pltpu.VMEM((tt, d), F32)],
        compiler_params=_cparams("arbitrary", "arbitrary"),
        name="dwconv_prompt",
    )(u, w, b_dw, ln_g, ln_b)


def _dwconv_sample_body(u_ref, buf_ref, w_ref, bdw_ref, g_ref, b_ref, o_ref, nbuf_ref, *, tb):
    w = w_ref[...]
    rows = []
    for i in range(tb):
        hist = buf_ref[i]
        ur = u_ref[i:i + 1, :]
        rows.append(jnp.sum(hist * w[:C_KERNEL - 1], axis=0, keepdims=True) + ur * w[C_KERNEL - 1:C_KERNEL])
        nbuf_ref[i] = jnp.concatenate([hist[1:], ur], axis=0)
    y = jnp.concatenate(rows, axis=0) + bdw_ref[...]
    o_ref[...] = _ln_swish(y, g_ref[...], b_ref[...])


def dwconv_sample(u, buf, w, b_dw, ln_g, ln_b):
    nb, d = u.shape
    tb = min(SAMPLE_TILE, nb)
    const = lambda shape: pl.BlockSpec(shape, lambda i: (0,) * len(shape))
    return pl.pallas_call(
        functools.partial(_dwconv_sample_body, tb=tb),
        grid=(nb // tb,),
        in_specs=[pl.BlockSpec((tb, d), lambda i: (i, 0)), pl.BlockSpec((tb, C_KERNEL - 1, d), lambda i: (i, 0, 0)),
                  const((C_KERNEL, d)), const((1, d)), const((1, d)), const((1, d))],
        out_specs=[pl.BlockSpec((tb, d), lambda i: (i, 0)), pl.BlockSpec((tb, C_KERNEL - 1, d), lambda i: (i, 0, 0))],
        out_shape=[jax.ShapeDtypeStruct((nb, d), F32), jax.ShapeDtypeStruct(buf.shape, F32)],
        compiler_params=_cparams("parallel"),
        name="dwconv_sample",
    )(u, buf, w, b_dw, ln_g, ln_b)


def _moe(h8, lpos128, gate128, runn8, wg, wu, wd, layer, *, precise):
    plan = moe_plan(lpos128, gate128, runn8, EXPERT_BLOCK_PRECISE if precise else EXPERT_BLOCK)
    xs = dispatch(h8, plan)
    return experts(xs, plan, wg, wu, wd, layer, precise=precise), plan


def _trunk(x, caches, p, *, prompt):
    bsz, seq, d = x.shape
    rows = bsz * seq
    precise = not prompt
    wdt = F32 if precise else BF16
    xf = x.reshape(rows, d)
    row = lambda v: v.reshape(1, -1).astype(F32)

    z = norm_proj(xf, row(p['norm_mix'][0]), p['w_in'].astype(wdt), precise=precise)
    if prompt:
        att = attn_prompt(z, p['rel_table'], p['sinks'], bsz, seq)
        out_b, c1, n1, m1 = mlstm_prompt(z, p['conv_w'], p['b_gates'], p['g_mnorm'], bsz, seq)
        z3 = z.reshape(bsz, seq, Z_COLS)
        new_k = z3[:, seq - WINDOW:, Z_KA:Z_KA + A_KV].reshape(bsz, WINDOW, A_KV_HEADS, A_HEAD_DIM)
        new_v = z3[:, seq - WINDOW:, Z_VA:Z_VA + A_KV].reshape(bsz, WINDOW, A_KV_HEADS, A_HEAD_DIM)
        new_conv = z3[:, seq - (B_CONV - 1):, Z_QK:Z_QK + 2 * B_QK]
        n1 = n1[:, :B_HEADS]
        m1 = m1[:, :B_HEADS, 0]
    else:
        ck, cv, c0, n0, m0, cbuf = caches[:6]
        n_buf = ck.shape[1]
        att, out_b, new_k, new_v, c1, n1, m1, new_conv = mix_sample(
            z, p['rel_table'], p['sinks'], ck.reshape(bsz, n_buf, A_KV), cv.reshape(bsz, n_buf, A_KV),
            c0, n0, m0, cbuf, p['conv_w'], p['b_gates'], p['g_mnorm'])
        new_k = new_k.reshape(bsz, n_buf, A_KV_HEADS, A_HEAD_DIM)
        new_v = new_v.reshape(bsz, n_buf, A_KV_HEADS, A_HEAD_DIM)
        m1 = m1[:, :B_HEADS]
    w_out = p['w_out'].astype(wdt)
    x1, h8, lpos, gate, runn = proj_router([att, out_b], [w_out[:A_Q], w_out[A_Q:]], None, xf,
                                    row(p['norm_ffn'][0]), p['w_router'][0], p['b_router'][0], precise=precise)
    ys, plan = _moe(h8, lpos, gate, runn, p['w_eg'], p['w_eu'], p['w_ed'], 0, precise=precise)

    x2, u = combine_glu(x1, ys, plan, row(p['norm_mix'][1]), p['w_pw1'].astype(wdt), row(p['b_pw1']),
                        precise=precise)
    if prompt:
        yc = dwconv_prompt(u, p['w_dw'], row(p['b_dw']), row(p['ln_g']), row(p['ln_b']), bsz, seq)
        new_cbuf = u.reshape(bsz, seq, d)[:, seq - (C_KERNEL - 1):]
    else:
        yc, new_cbuf = dwconv_sample(u, caches[6], p['w_dw'], row(p['b_dw']), row(p['ln_g']), row(p['ln_b']))
    x3, h8, lpos, gate, runn = proj_router([yc], [p['w_pw2'].astype(wdt)], row(p['b_pw2']), x2,
                                    row(p['norm_ffn'][1]), p['w_router'][1], p['b_router'][1], precise=precise)
    ys, plan = _moe(h8, lpos, gate, runn, p['w_eg'], p['w_eu'], p['w_ed'], 1, precise=precise)
    y = combine_final(x3, ys, plan, row(p['norm_final']))
    add_layer = lambda t: t[None]
    return (y.reshape(bsz, seq, d),) + tuple(add_layer(t) for t in (new_k, new_v, c1, n1, m1, new_conv, new_cbuf))


def kernel(x_prompt, x_sample, cache_win_k, cache_win_v, state_mlstm_c, state_mlstm_n, state_mlstm_m, state_mlstm_conv, state_conv, norm_mix, norm_ffn, norm_final, rel_bias_table, w_in_mix, b_mlstm_gates, w_mlstm_qk_conv, attn_sinks, g_mlstm_norm, w_out_mix, w_pw1, b_pw1, w_dw, b_dw, ln_conv_g, ln_conv_b, w_pw2, b_pw2, w_router_group, b_router_group, w_router_expert, b_router_expert, w_expert_gate, w_expert_up, w_expert_down):
    w_in = w_in_mix[0]
    s_q, s_k, s_v, s_qk, s_vb, s_g = A_Q, A_Q + A_KV, A_Q + 2 * A_KV, A_Q + 2 * A_KV + 2 * B_QK, \
        A_Q + 2 * A_KV + 2 * B_QK + B_V, A_Q + 2 * A_KV + 2 * B_QK + B_V + 2 * B_HEADS
    w_in_r = jnp.concatenate([w_in[:, :s_q], w_in[:, s_v:s_qk], w_in[:, s_qk:s_vb], w_in[:, s_g:],
                              w_in[:, s_q:s_k], w_in[:, s_k:s_v], w_in[:, s_vb:s_g],
                              jnp.zeros((D_MODEL, LANES - 2 * B_HEADS), F32)], axis=1)
    b_gates = jnp.concatenate([b_mlstm_gates[0], jnp.zeros((LANES - 2 * B_HEADS,), F32)]).reshape(1, LANES)
    depth = w_router_group.shape[0]
    w_re = jnp.transpose(w_router_expert, (0, 2, 1, 3)).reshape(depth, D_MODEL, N_EXPERTS)
    w_router = jnp.concatenate([w_router_group, w_re,
                                jnp.zeros((depth, D_MODEL, LANES - N_GROUPS - N_EXPERTS), F32)], axis=-1)
    b_router = jnp.concatenate([b_router_group, b_router_expert.reshape(depth, N_EXPERTS),
                                jnp.zeros((depth, LANES - N_GROUPS - N_EXPERTS), F32)], axis=-1)[:, None, :]
    p = dict(norm_mix=norm_mix, norm_ffn=norm_ffn, norm_final=norm_final, rel_table=rel_bias_table,
             sinks=attn_sinks[0], w_in=w_in_r, b_gates=b_gates, conv_w=w_mlstm_qk_conv[0],
             g_mnorm=g_mlstm_norm[0].reshape(1, B_V), w_out=w_out_mix[0], w_pw1=w_pw1[0], b_pw1=b_pw1[0],
             w_dw=w_dw[0], b_dw=b_dw[0], ln_g=ln_conv_g[0], ln_b=ln_conv_b[0], w_pw2=w_pw2[0], b_pw2=b_pw2[0],
             w_router=w_router, b_router=b_router, w_eg=w_expert_gate, w_eu=w_expert_up, w_ed=w_expert_down)
    caches = (cache_win_k[0], cache_win_v[0], state_mlstm_c[0], state_mlstm_n[0], state_mlstm_m[0],
              state_mlstm_conv[0], state_conv[0])
    out_p = _trunk(x_prompt, None, p, prompt=True)
    out_s = _trunk(x_sample, caches, p, prompt=False)
    return (out_p[0], out_s[0]) + out_p[1:] + out_s[1:]
```

```python
import functools
import math

import numpy as np
import jax
import jax.numpy as jnp
from jax import lax
from jax.experimental import pallas as pl
from jax.experimental.pallas import tpu as pltpu

F32 = jnp.float32
BF16 = jnp.bfloat16
I32 = jnp.int32
HI = lax.Precision.HIGHEST
NEG_INF = float("-inf")

LANES = 128
SUBLANES = 8
VMEM_LIMIT = 56 * 1024 * 1024

D_MODEL = 1024
A_HEADS = 8
A_KV_HEADS = 2
A_GROUP = A_HEADS // A_KV_HEADS
A_HEAD_DIM = 64
WINDOW = 128
REL_BUCKETS = 32
REL_MAX_DIST = 128
B_HEADS = 4
B_DK = 64
B_DV = 128
B_CONV = 4
C_KERNEL = 31
N_GROUPS = 4
EXPERTS_PER_GROUP = 8
N_EXPERTS = N_GROUPS * EXPERTS_PER_GROUP
TOP_K = 2
EXPERT_FF = D_MODEL // 2
EXPERT_BLOCK = 256
EXPERT_BLOCK_PRECISE = 128
EPS = 1e-6

A_Q = A_HEADS * A_HEAD_DIM
A_KV = A_KV_HEADS * A_HEAD_DIM
B_QK = B_HEADS * B_DK
B_V = B_HEADS * B_DV
ROW_TILES = D_MODEL // LANES

Z_QA, Z_QK, Z_VB, Z_OG, Z_KA, Z_VA, Z_GATES = 0, 512, 1024, 1536, 2048, 2176, 2304
Z_COLS = 2432
MLSTM_CHUNK = 128


def _cparams(*sem):
    return pltpu.CompilerParams(dimension_semantics=sem, vmem_limit_bytes=VMEM_LIMIT)


def _rms(x, g):
    return x * lax.rsqrt(jnp.mean(x * x, axis=-1, keepdims=True) + EPS) * g


def _mm(a, w, precise):
    if precise:
        return jnp.dot(a.astype(F32), w, precision=HI, preferred_element_type=F32)
    return jnp.dot(a.astype(BF16), w, preferred_element_type=F32)


def _t5_buckets(dist):
    exact = REL_BUCKETS // 2
    d = np.maximum(dist, 0)
    large = exact + (np.log(np.maximum(d, 1).astype(np.float32) / exact)
                     / math.log(REL_MAX_DIST / exact) * (REL_BUCKETS - exact)).astype(np.int32)
    return np.where(d < exact, d, np.minimum(large, REL_BUCKETS - 1)).astype(np.int32)


def _norm_proj_body(x_ref, g_ref, w_ref, o_ref, *, precise):
    h = _rms(x_ref[...], g_ref[...])
    o_ref[...] = _mm(h, w_ref[...], precise)


def norm_proj(x, g, w, *, precise):
    rows, d = x.shape
    cols = w.shape[1]
    tm = min(rows, 128 if precise else 512)
    return pl.pallas_call(
        functools.partial(_norm_proj_body, precise=precise),
        grid=(rows // tm,),
        in_specs=[pl.BlockSpec((tm, d), lambda i: (i, 0)),
                  pl.BlockSpec((1, d), lambda i: (0, 0)),
                  pl.BlockSpec((d, cols), lambda i: (0, 0))],
        out_specs=pl.BlockSpec((tm, cols), lambda i: (i, 0)),
        out_shape=jax.ShapeDtypeStruct((rows, cols), F32),
        compiler_params=_cparams("parallel"),
        name="norm_proj",
    )(x, g, w)


def _attn_prompt_body(tab_ref, sink_ref, bkt_ref, q_ref, kp_ref, kc_ref, vp_ref, vc_ref, o_ref, bias_ref):
    b = pl.program_id(0)
    n = pl.program_id(1)

    @pl.when((b == 0) & (n == 0))
    def _():
        bk = bkt_ref[...]
        for h in range(A_HEADS):
            acc = jnp.full(bk.shape, NEG_INF, F32)
            for t in range(REL_BUCKETS):
                acc = jnp.where(bk == t, tab_ref[t, h], acc)
            bias_ref[h] = acc

    q = q_ref[...] * (A_HEAD_DIM ** -0.5)
    kb = jnp.concatenate([kp_ref[...], kc_ref[...]], axis=0).astype(BF16)
    vb = jnp.concatenate([vp_ref[...], vc_ref[...]], axis=0).astype(BF16)
    col = lax.broadcasted_iota(I32, (WINDOW, 2 * WINDOW), 1)
    dead = col < jnp.where(n == 0, WINDOW, 0)
    outs = []
    for h in range(A_HEADS):
        kvh = h // A_GROUP
        qh = q[:, h * A_HEAD_DIM:(h + 1) * A_HEAD_DIM].astype(BF16)
        kh = kb[:, kvh * A_HEAD_DIM:(kvh + 1) * A_HEAD_DIM]
        vh = vb[:, kvh * A_HEAD_DIM:(kvh + 1) * A_HEAD_DIM]
        s = lax.dot_general(qh, kh, (((1,), (1,)), ((), ())), preferred_element_type=F32)
        s = jnp.where(dead, NEG_INF, s + bias_ref[h])
        sink = sink_ref[h]
        mx = jnp.maximum(jnp.max(s, axis=-1, keepdims=True), sink)
        p = jnp.exp(s - mx)
        den = jnp.sum(p, axis=-1, keepdims=True) + jnp.exp(sink - mx)
        outs.append(jnp.dot(p.astype(BF16), vh, preferred_element_type=F32) / den)
    o_ref[...] = jnp.concatenate(outs, axis=1).astype(o_ref.dtype)


def attn_prompt(z, rel_table, sinks, bsz, seq):
    nb = seq // WINDOW
    dist = WINDOW + np.arange(WINDOW)[:, None] - np.arange(2 * WINDOW)[None, :]
    bkt = np.where((dist >= 0) & (dist <= WINDOW), _t5_buckets(dist), -1).astype(np.int32)
    kcol, vcol = Z_KA // LANES, Z_VA // LANES
    smem = pl.BlockSpec(memory_space=pltpu.SMEM)

    def cur(c):
        return pl.BlockSpec((WINDOW, LANES), lambda b, n: (b * nb + n, c))

    def prev(c):
        return pl.BlockSpec((WINDOW, LANES), lambda b, n: (b * nb + jnp.maximum(n - 1, 0), c))

    return pl.pallas_call(
        _attn_prompt_body,
        grid=(bsz, nb),
        in_specs=[smem, smem,
                  pl.BlockSpec((WINDOW, 2 * WINDOW), lambda b, n: (0, 0)),
                  pl.BlockSpec((WINDOW, A_Q), lambda b, n: (b * nb + n, 0)),
                  prev(kcol), cur(kcol), prev(vcol), cur(vcol)],
        out_specs=pl.BlockSpec((WINDOW, A_Q), lambda b, n: (b * nb + n, 0)),
        out_shape=jax.ShapeDtypeStruct((bsz * seq, A_Q), BF16),
        scratch_shapes=[pltpu.VMEM((A_HEADS, WINDOW, 2 * WINDOW), F32)],
        compiler_params=_cparams("arbitrary", "arbitrary"),
        name="attn_prompt",
    )(rel_table, sinks, jnp.asarray(bkt), z, z, z, z, z)


def _log_sigmoid(x):
    return -(jnp.maximum(-x, 0.0) + jnp.log1p(jnp.exp(-jnp.abs(x))))


def _mlstm_prompt_body(qk_ref, v_ref, g_ref, og_ref, cw_ref, bg_ref, gn_ref,
                       ob_ref, c_out, n_out, m_out,
                       c_sc, n_sc, m_sc, hist_sc, *, L, nc, nbat):
    ci = pl.program_id(1)

    @pl.when(ci == 0)
    def _():
        c_sc[...] = jnp.zeros(c_sc.shape, F32)
        n_sc[...] = jnp.zeros(n_sc.shape, F32)
        m_sc[...] = jnp.full(m_sc.shape, NEG_INF, F32)
        hist_sc[...] = jnp.zeros(hist_sc.shape, F32)

    row = lax.broadcasted_iota(I32, (L, L), 0)
    colm = lax.broadcasted_iota(I32, (L, L), 1)
    causal = colm <= row
    tril = causal.astype(F32)
    stores = []
    for bb in range(nbat):
        stores += _mlstm_chunk(qk_ref.at[bb], v_ref.at[bb], g_ref.at[bb], og_ref.at[bb], cw_ref, bg_ref, gn_ref,
                               ob_ref.at[bb], c_sc.at[bb], n_sc.at[bb], m_sc.at[bb], hist_sc.at[bb],
                               causal, tril, L)
    for store in stores:
        store()

    @pl.when(ci == nc - 1)
    def _():
        c_out[...] = c_sc[...]
        n_out[...] = n_sc[...]
        m_out[...] = m_sc[...]


def _mlstm_chunk(qk_ref, v_ref, g_ref, og_ref, cw_ref, bg_ref, gn_ref, ob_ref, c_sc, n_sc, m_sc, hist_sc,
                 causal, tril, L):
    cur = qk_ref[...]
    ext = jnp.concatenate([hist_sc[...], cur], axis=0)
    cw = cw_ref[...]
    off = SUBLANES - (B_CONV - 1)
    conv = ext[off:off + L] * cw[0:1]
    for j in range(1, B_CONV):
        conv = conv + ext[off + j:off + j + L] * cw[j:j + 1]
    qk = conv * jax.nn.sigmoid(conv)
    q_all = qk[:, :B_QK]
    k_all = qk[:, B_QK:] * (B_DK ** -0.5)
    v_all = v_ref[...]
    og = og_ref[...]
    gn = gn_ref[...]

    G = g_ref[...] + bg_ref[...]
    lf = _log_sigmoid(G)
    Bc = jnp.dot(tril, lf, precision=HI, preferred_element_type=F32)
    BT = Bc.T
    GT = G.T

    outs, stores = [], []
    for h in range(B_HEADS):
        qh = q_all[:, h * B_DK:(h + 1) * B_DK]
        kh = k_all[:, h * B_DK:(h + 1) * B_DK]
        vh = v_all[:, h * B_DV:(h + 1) * B_DV]
        b_col = Bc[:, B_HEADS + h:B_HEADS + h + 1]
        ig_col = G[:, h:h + 1]
        b_row = BT[B_HEADS + h:B_HEADS + h + 1, :]
        ig_row = GT[h:h + 1, :]
        c0 = c_sc[h]
        n0 = n_sc[h:h + 1, :]
        m0 = m_sc[h:h + 1, 0:1]
        a = b_col + m0
        d = jnp.where(causal, b_col - b_row + ig_row, NEG_INF)
        m = jnp.maximum(a, jnp.max(d, axis=-1, keepdims=True))
        dw = jnp.exp(d - m)
        aw = jnp.exp(a - m)
        qb = qh.astype(BF16)
        vb = vh.astype(BF16)
        s = lax.dot_general(qb, kh.astype(BF16), (((1,), (1,)), ((), ())), preferred_element_type=F32) * dw
        num = (jnp.dot(s.astype(BF16), vb, preferred_element_type=F32)
               + aw * jnp.dot(qb, c0.astype(BF16), preferred_element_type=F32))
        den = jnp.sum(s, axis=-1, keepdims=True) + aw * jnp.sum(qh * n0, axis=-1, keepdims=True)
        hh = num / jnp.maximum(jnp.abs(den), jnp.exp(-m))
        m_last = m[L - 1:L, :]
        wl = jnp.exp(b_col[L - 1:L, :] - b_col + ig_col - m_last)
        decay = aw[L - 1:L, :]
        kw = kh * wl
        c1 = decay * c0 + lax.dot_general(kw.astype(BF16), vb, (((0,), (0,)), ((), ())),
                                          preferred_element_type=F32)
        n1 = decay * n0 + jnp.sum(kw, axis=0, keepdims=True)
        stores.append(functools.partial(_store_state, c_sc, n_sc, m_sc, h, c1, n1, m_last))
        hn = hh * lax.rsqrt(jnp.mean(hh * hh, axis=-1, keepdims=True) + EPS) * gn[:, h * B_DV:(h + 1) * B_DV]
        outs.append(jax.nn.sigmoid(og[:, h * B_DV:(h + 1) * B_DV]) * hn)
    out = jnp.concatenate(outs, axis=1).astype(ob_ref.dtype)
    stores.append(functools.partial(_store_chunk, ob_ref, hist_sc, out, cur[L - SUBLANES:L]))
    return stores


def _store_state(c_sc, n_sc, m_sc, h, c1, n1, m_last):
    c_sc[h] = c1
    n_sc[h:h + 1, :] = n1
    m_sc[h:h + 1, :] = jnp.broadcast_to(m_last, (1, LANES))


def _store_chunk(ob_ref, hist_sc, out, tail):
    ob_ref[...] = out
    hist_sc[...] = tail


MLSTM_BATCH = 1


def mlstm_prompt(z, conv_w, b_gates_pad, g_norm, bsz, seq):
    L = MLSTM_CHUNK
    nc = seq // L
    nbat = MLSTM_BATCH if bsz % MLSTM_BATCH == 0 else 1
    z3 = z.reshape(bsz, seq, Z_COLS)

    def zspec(width, colblk):
        return pl.BlockSpec((nbat, L, width), lambda b, c: (b, c, colblk))

    const = lambda shape: pl.BlockSpec(shape, lambda b, c: (0,) * len(shape))
    state = lambda shape: pl.BlockSpec((nbat,) + shape, lambda b, c: (b,) + (0,) * len(shape))
    out_b, c1, n1, m1 = pl.pallas_call(
        functools.partial(_mlstm_prompt_body, L=L, nc=nc, nbat=nbat),
        grid=(bsz // nbat, nc),
        in_specs=[zspec(2 * B_QK, Z_QK // (2 * B_QK)), zspec(B_V, Z_VB // B_V), zspec(LANES, Z_GATES // LANES),
                  zspec(B_V, Z_OG // B_V), const((B_CONV, 2 * B_QK)), const((1, LANES)), const((1, B_V))],
        out_specs=[pl.BlockSpec((nbat, L, B_V), lambda b, c: (b, c, 0)),
                   state((B_HEADS, B_DK, B_DV)), state((SUBLANES, B_DK)), state((SUBLANES, LANES))],
        out_shape=[jax.ShapeDtypeStruct((bsz, seq, B_V), BF16),
                   jax.ShapeDtypeStruct((bsz, B_HEADS, B_DK, B_DV), F32),
                   jax.ShapeDtypeStruct((bsz, SUBLANES, B_DK), F32),
                   jax.ShapeDtypeStruct((bsz, SUBLANES, LANES), F32)],
        scratch_shapes=[pltpu.VMEM((nbat, B_HEADS, B_DK, B_DV), F32), pltpu.VMEM((nbat, SUBLANES, B_DK), F32),
                        pltpu.VMEM((nbat, SUBLANES, LANES), F32), pltpu.VMEM((nbat, SUBLANES, 2 * B_QK), F32)],
        compiler_params=_cparams("arbitrary", "arbitrary"),
        name="mlstm_prompt",
    )(z3, z3, z3, z3, conv_w, b_gates_pad, g_norm)
    return out_b.reshape(bsz * seq, B_V), c1, n1, m1


SAMPLE_TILE = 8


def _row_to_col(row, n):
    eye = lax.broadcasted_iota(I32, (n, n), 0) == lax.broadcasted_iota(I32, (n, n), 1)
    return jnp.sum(jnp.where(eye, jnp.broadcast_to(row, (n, n)), 0.0), axis=1, keepdims=True)


def _mix_sample_body(tab_ref, sink_ref, bkt_ref, z_ref, ck_ref, cv_ref, c0_ref, n0_ref, m0_ref, cb_ref,
                     cw_ref, bg_ref, gn_ref,
                     att_ref, ob_ref, nk_ref, nv_ref, c1_ref, n1_ref, m1_ref, ncb_ref, *, tb):
    bk = bkt_ref[...]
    cw = cw_ref[...]
    gn = gn_ref[...]
    zero_half = jnp.zeros((1, A_HEAD_DIM), F32)
    bias_rows = []
    for h in range(A_HEADS):
        bias = jnp.zeros(bk.shape, F32)
        for t in range(REL_BUCKETS):
            bias = jnp.where(bk == t, tab_ref[t, h], bias)
        bias_rows.append(bias)
    bias_c = jnp.concatenate(bias_rows, axis=0)
    head_id = lax.broadcasted_iota(I32, (A_HEADS, 1), 0)
    bias_n = jnp.zeros((A_HEADS, 1), F32)
    sinks = jnp.zeros((A_HEADS, 1), F32)
    for h in range(A_HEADS):
        bias_n = jnp.where(head_id == h, tab_ref[0, h], bias_n)
        sinks = jnp.where(head_id == h, sink_ref[h], sinks)
    att_rows, ob_rows, m_rows = [], [], []
    for i in range(tb):
        zr = z_ref[i:i + 1, :]
        q_att = zr[:, Z_QA:Z_QA + A_Q] * (A_HEAD_DIM ** -0.5)
        ka = zr[:, Z_KA:Z_KA + A_KV]
        va = zr[:, Z_VA:Z_VA + A_KV]
        kc = ck_ref[i]
        vc = cv_ref[i]
        nk_ref[i] = jnp.concatenate([kc[1:], ka], axis=0)
        nv_ref[i] = jnp.concatenate([vc[1:], va], axis=0)
        q_rows = []
        for h in range(A_HEADS):
            qh = q_att[:, h * A_HEAD_DIM:(h + 1) * A_HEAD_DIM]
            q_rows.append(jnp.concatenate([qh, zero_half] if h // A_GROUP == 0 else [zero_half, qh], axis=1))
        qm = jnp.concatenate(q_rows, axis=0)
        lc = lax.dot_general(qm, kc, (((1,), (1,)), ((), ())), precision=HI,
                             preferred_element_type=F32) + bias_c
        ln = jnp.sum(qm * ka, axis=-1, keepdims=True) + bias_n
        mx = jnp.maximum(jnp.maximum(jnp.max(lc, axis=-1, keepdims=True), ln), sinks)
        pc = jnp.exp(lc - mx)
        pn = jnp.exp(ln - mx)
        den = jnp.sum(pc, axis=-1, keepdims=True) + pn + jnp.exp(sinks - mx)
        o = (jnp.dot(pc, vc, precision=HI, preferred_element_type=F32) + pn * va) / den
        att_rows.append(jnp.concatenate(
            [o[h:h + 1, (h // A_GROUP) * A_HEAD_DIM:(h // A_GROUP + 1) * A_HEAD_DIM] for h in range(A_HEADS)],
            axis=1))

        qk_pre = zr[:, Z_QK:Z_QK + 2 * B_QK]
        hist = cb_ref[i]
        conv = qk_pre * cw[B_CONV - 1:B_CONV]
        for j in range(B_CONV - 1):
            conv = conv + hist[j:j + 1] * cw[j:j + 1]
        ncb_ref[i] = jnp.concatenate([hist[1:], qk_pre], axis=0)
        qk = conv * jax.nn.sigmoid(conv)
        G = zr[:, Z_GATES:Z_GATES + LANES] + bg_ref[...]
        lfr = _log_sigmoid(G)
        og = zr[:, Z_OG:Z_OG + B_V]
        v_pre = zr[:, Z_VB:Z_VB + B_V]
        obs, ms = [], []
        for h in range(B_HEADS):
            qh = qk[:, h * B_DK:(h + 1) * B_DK]
            kh = qk[:, B_QK + h * B_DK:B_QK + (h + 1) * B_DK] * (B_DK ** -0.5)
            vh = v_pre[:, h * B_DV:(h + 1) * B_DV]
            ig = G[:, h:h + 1]
            lf = lfr[:, B_HEADS + h:B_HEADS + h + 1]
            c0 = c0_ref[i, h]
            n0 = n0_ref[i, h:h + 1, :]
            m0 = m0_ref[i:i + 1, h:h + 1]
            a = lf + m0
            m = jnp.maximum(a, ig)
            dw = jnp.exp(ig - m)
            aw = jnp.exp(a - m)
            s = jnp.sum(qh * kh, axis=-1, keepdims=True) * dw
            q_col = _row_to_col(qh, B_DK)
            k_col = _row_to_col(kh, B_DK)
            num = s * vh + aw * jnp.sum(q_col * c0, axis=0, keepdims=True)
            den = s + aw * jnp.sum(qh * n0, axis=-1, keepdims=True)
            hh = num / jnp.maximum(jnp.abs(den), jnp.exp(-m))
            c1_ref[i, h] = aw * c0 + dw * (k_col * vh)
            n1_ref[i, h:h + 1, :] = aw * n0 + dw * kh
            ms.append(m)
            hn = hh * lax.rsqrt(jnp.mean(hh * hh, axis=-1, keepdims=True) + EPS) * gn[:, h * B_DV:(h + 1) * B_DV]
            obs.append(jax.nn.sigmoid(og[:, h * B_DV:(h + 1) * B_DV]) * hn)
        ob_rows.append(jnp.concatenate(obs, axis=1))
        lane = lax.broadcasted_iota(I32, (1, LANES), 1)
        mrow = jnp.zeros((1, LANES), F32)
        for h in range(B_HEADS):
            mrow = jnp.where(lane == h, ms[h], mrow)
        m_rows.append(mrow)
    att_ref[...] = jnp.concatenate(att_rows, axis=0)
    ob_ref[...] = jnp.concatenate(ob_rows, axis=0)
    m1_ref[...] = jnp.concatenate(m_rows, axis=0)


def mix_sample(z, rel_table, sinks, ck, cv, c0, n0, m0, conv_buf, conv_w, b_gates_pad, g_norm):
    nb = z.shape[0]
    tb = min(SAMPLE_TILE, nb)
    n_buf = ck.shape[1]
    bkt = _t5_buckets(n_buf - np.arange(n_buf))[None, :]
    smem = pl.BlockSpec(memory_space=pltpu.SMEM)
    const = lambda shape: pl.BlockSpec(shape, lambda i: (0,) * len(shape))
    lead = lambda shape: pl.BlockSpec((tb,) + shape, lambda i: (i,) + (0,) * len(shape))
    return pl.pallas_call(
        functools.partial(_mix_sample_body, tb=tb),
        grid=(nb // tb,),
        in_specs=[smem, smem, const((1, n_buf)), lead((Z_COLS,)), lead((n_buf, A_KV)), lead((n_buf, A_KV)),
                  lead((B_HEADS, B_DK, B_DV)), lead((B_HEADS, B_DK)), lead((B_HEADS,)),
                  lead((B_CONV - 1, 2 * B_QK)), const((B_CONV, 2 * B_QK)), const((1, LANES)), const((1, B_V))],
        out_specs=[lead((A_Q,)), lead((B_V,)), lead((n_buf, A_KV)), lead((n_buf, A_KV)),
                   lead((B_HEADS, B_DK, B_DV)), lead((B_HEADS, B_DK)), lead((LANES,)),
                   lead((B_CONV - 1, 2 * B_QK))],
        out_shape=[jax.ShapeDtypeStruct((nb, A_Q), F32), jax.ShapeDtypeStruct((nb, B_V), F32),
                   jax.ShapeDtypeStruct(ck.shape, F32), jax.ShapeDtypeStruct(cv.shape, F32),
                   jax.ShapeDtypeStruct(c0.shape, F32), jax.ShapeDtypeStruct(n0.shape, F32),
                   jax.ShapeDtypeStruct((nb, LANES), F32), jax.ShapeDtypeStruct(conv_buf.shape, F32)],
        compiler_params=_cparams("parallel"),
        name="mix_sample",
    )(rel_table, sinks, jnp.asarray(bkt), z, ck, cv, c0, n0, m0, conv_buf, conv_w, b_gates_pad, g_norm)


def _store_row_tiles(ref, val, rows):
    for s in range(ROW_TILES):
        ref[pl.ds(s, rows, stride=ROW_TILES), :] = val[:, s * LANES:(s + 1) * LANES]


def _load_row_tiles(ref, rows, start=0, stride=ROW_TILES):
    return jnp.concatenate([ref[pl.ds(start + s, rows, stride=stride), :] for s in range(ROW_TILES)], axis=1)


def _route(logits):
    lane = lax.broadcasted_iota(I32, logits.shape, 1)
    big = jnp.int32(1 << 20)
    gl = jnp.where(lane < N_GROUPS, logits, NEG_INF)
    gmax = jnp.max(gl, axis=-1, keepdims=True)
    gidx = jnp.min(jnp.where(gl == gmax, lane, big), axis=-1, keepdims=True)
    g_gate = 1.0 / jnp.sum(jnp.exp(gl - gmax), axis=-1, keepdims=True)
    lo = N_GROUPS + gidx * EXPERTS_PER_GROUP
    el = jnp.where((lane >= lo) & (lane < lo + EXPERTS_PER_GROUP), logits, NEG_INF)
    v1 = jnp.max(el, axis=-1, keepdims=True)
    i1 = jnp.min(jnp.where(el == v1, lane, big), axis=-1, keepdims=True)
    el2 = jnp.where(lane == i1, NEG_INF, el)
    v2 = jnp.max(el2, axis=-1, keepdims=True)
    i2 = jnp.min(jnp.where(el2 == v2, lane, big), axis=-1, keepdims=True)
    t = jnp.exp(v2 - v1)
    w1 = g_gate / (1.0 + t)
    w2 = g_gate * t / (1.0 + t)
    gate = jnp.where(lane == 0, w1, jnp.where(lane == 1, w2, 0.0))
    return i1 - N_GROUPS, i2 - N_GROUPS, gate


def _local_sort(e0, e1):
    tm = e0.shape[0]
    lane = lax.broadcasted_iota(I32, (tm, LANES), 1)
    oh0 = (lane == e0).astype(BF16)
    oh1 = (lane == e1).astype(BF16)
    r = lax.broadcasted_iota(I32, (tm, tm), 0)
    c = lax.broadcasted_iota(I32, (tm, tm), 1)
    before = (c < r).astype(BF16)
    cnt0 = jnp.sum(oh0.astype(F32), axis=0, keepdims=True)
    run_n = cnt0 + jnp.sum(oh1.astype(F32), axis=0, keepdims=True)
    er = lax.broadcasted_iota(I32, (LANES, LANES), 0)
    ec = lax.broadcasted_iota(I32, (LANES, LANES), 1)
    run_l = jnp.dot(run_n, (er < ec).astype(F32), precision=HI, preferred_element_type=F32)
    w0 = jnp.dot(before, oh0, preferred_element_type=F32) + run_l
    w1 = jnp.dot(before, oh1, preferred_element_type=F32) + run_l + cnt0
    p0 = jnp.sum(jnp.where(lane == e0, w0, 0.0), axis=-1, keepdims=True)
    p1 = jnp.sum(jnp.where(lane == e1, w1, 0.0), axis=-1, keepdims=True)
    lpos = jnp.where(lane == 0, p0, jnp.where(lane == 1, p1, 0.0)).astype(I32)
    return lpos, run_n


MOE_CHUNK = 512


def _moe_chunk(rows, precise):
    return min(rows, 128 if precise else MOE_CHUNK)


def _proj_router_body(*refs, n_in, has_bias, precise, tm):
    a_refs = refs[:n_in]
    w_refs = refs[n_in:2 * n_in]
    k = 2 * n_in
    bias_ref = refs[k] if has_bias else None
    k += 1 if has_bias else 0
    x_ref, g_ref, wr_ref, br_ref, x1_ref, h8_ref, lpos_ref, gate_ref, runn_ref = refs[k:]
    acc = x_ref[...]
    if has_bias:
        acc = acc + bias_ref[...]
    for a_ref, w_ref in zip(a_refs, w_refs):
        acc = acc + _mm(a_ref[...], w_ref[...], precise)
    x1_ref[...] = acc
    h = _rms(acc, g_ref[...])
    _store_row_tiles(h8_ref, h, tm)
    wr = wr_ref[...]
    if precise:
        logits = jnp.dot(h, wr, precision=HI, preferred_element_type=F32)
    else:
        h_hi = h.astype(BF16)
        h_lo = (h - h_hi.astype(F32)).astype(BF16)
        w_hi = wr.astype(BF16)
        w_lo = (wr - w_hi.astype(F32)).astype(BF16)
        logits = (jnp.dot(h_hi, w_hi, preferred_element_type=F32) + jnp.dot(h_lo, w_hi, preferred_element_type=F32)
                  + jnp.dot(h_hi, w_lo, preferred_element_type=F32))
    e0, e1, gate = _route(logits + br_ref[...])
    lpos, run_n = _local_sort(e0, e1)
    lpos_ref[...] = lpos
    gate_ref[...] = gate
    runn_ref[...] = jnp.broadcast_to(run_n, runn_ref.shape).astype(I32)


def proj_router(a_list, w_list, bias, x, g, wr, br, *, precise):
    rows, d = x.shape
    tm = _moe_chunk(rows, precise)
    n_in = len(a_list)
    row_spec = lambda width: pl.BlockSpec((tm, width), lambda i: (i, 0))
    const = lambda shape: pl.BlockSpec(shape, lambda i: (0,) * len(shape))
    in_specs = [row_spec(a.shape[1]) for a in a_list] + [const(w.shape) for w in w_list]
    args = list(a_list) + list(w_list)
    if bias is not None:
        in_specs.append(const((1, d)))
        args.append(bias)
    in_specs += [row_spec(d), const((1, d)), const((d, LANES)), const((1, LANES))]
    args += [x, g, wr, br]
    return pl.pallas_call(
        functools.partial(_proj_router_body, n_in=n_in, has_bias=bias is not None, precise=precise, tm=tm),
        grid=(rows // tm,),
        in_specs=in_specs,
        out_specs=[row_spec(d), pl.BlockSpec((tm * ROW_TILES, LANES), lambda i: (i, 0)),
                   row_spec(LANES), row_spec(LANES), pl.BlockSpec((SUBLANES, LANES), lambda i: (i, 0))],
        out_shape=[jax.ShapeDtypeStruct((rows, d), F32), jax.ShapeDtypeStruct((rows * ROW_TILES, LANES), F32),
                   jax.ShapeDtypeStruct((rows, LANES), I32), jax.ShapeDtypeStruct((rows, LANES), F32),
                   jax.ShapeDtypeStruct((rows // tm * SUBLANES, LANES), I32)],
        compiler_params=_cparams("parallel"),
        name="proj_router",
    )(*args)


def _tile_rows(r, n=1):
    return pl.ds(pl.multiple_of(r * ROW_TILES, ROW_TILES), n * ROW_TILES)


def _pow2_pieces(limit):
    p = 1
    while p * 2 <= limit:
        p *= 2
    out = []
    while p >= 1:
        out.append(p)
        p //= 2
    return out


COMMON_PIECE = 32


def _for_each_piece(n, pieces, fn):
    def emit(ps):
        for p in ps:
            @pl.when((n & p) != 0)
            def _(p=p):
                fn(n & ~(2 * p - 1), p)

    big = [p for p in pieces if p > COMMON_PIECE]
    if big:
        @pl.when(n > 2 * COMMON_PIECE - 1)
        def _():
            emit(big)
    emit([p for p in pieces if p <= COMMON_PIECE])


def _dispatch_body(rn_ref, rg_ref, rl_ref, ps_ref, pn_ref, tail_ref, lpos_ref, h8_ref, xs_hbm,
                   stage, zbuf, sem, zsem, *, chunk, nch, blk):
    c = pl.program_id(0)
    slot = c % 2
    run_pieces = _pow2_pieces(chunk)
    pad_pieces = _pow2_pieces(blk - 1)

    @pl.when(c == 0)
    def _():
        zbuf[...] = jnp.zeros(zbuf.shape, F32)

        def pad_dmas(e, op):
            def one(off, p):
                cp = pltpu.make_async_copy(zbuf.at[_tile_rows(0, p)], xs_hbm.at[_tile_rows(ps_ref[e] + off, p)], zsem)
                cp.start() if op == 0 else cp.wait()
            _for_each_piece(pn_ref[e], pad_pieces, one)

        def issue(e, carry):
            pad_dmas(e, 0)
            return carry

        def wait(e, carry):
            pad_dmas(e, 1)
            return carry
        lax.fori_loop(0, N_EXPERTS, issue, 0)
        lax.fori_loop(0, N_EXPERTS, wait, 0)

        half = blk // 2

        def tail_dmas(i, op):
            cp = pltpu.make_async_copy(zbuf, xs_hbm.at[_tile_rows(tail_ref[0] + i * half, half)], zsem)
            cp.start() if op == 0 else cp.wait()

        def tail_issue(i, carry):
            tail_dmas(i, 0)
            return carry

        def tail_wait(i, carry):
            tail_dmas(i, 1)
            return carry
        lax.fori_loop(0, tail_ref[1], tail_issue, 0)
        lax.fori_loop(0, tail_ref[1], tail_wait, 0)

    def copy_tok(t, carry):
        row = h8_ref[_tile_rows(t), :]
        stage[slot, _tile_rows(lpos_ref[0, 0, TOP_K * t]), :] = row
        stage[slot, _tile_rows(lpos_ref[0, 0, TOP_K * t + 1]), :] = row
        return carry
    lax.fori_loop(0, chunk, copy_tok, 0, unroll=8)

    @pl.when(c > 0)
    def _():
        pltpu.make_async_copy(stage.at[1 - slot], stage.at[1 - slot], sem).wait()

    def send_runs(e, carry):
        k = c * N_EXPERTS + e

        def one(off, p):
            pltpu.make_async_copy(stage.at[slot, _tile_rows(rl_ref[k] + off, p)],
                                  xs_hbm.at[_tile_rows(rg_ref[k] + off, p)], sem).start()
        _for_each_piece(rn_ref[k], run_pieces, one)
        return carry
    lax.fori_loop(0, N_EXPERTS, send_runs, 0)

    @pl.when(c == nch - 1)
    def _():
        pltpu.make_async_copy(stage.at[slot], stage.at[slot], sem).wait()


def dispatch(h8, plan):
    chunk, nch, blk = plan['chunk'], plan['nch'], plan['blk']
    n_slots = plan['nblk'] * blk
    gs = pltpu.PrefetchScalarGridSpec(
        num_scalar_prefetch=6,
        grid=(nch,),
        in_specs=[pl.BlockSpec((1, 1, TOP_K * chunk), lambda c, *_: (c, 0, 0), memory_space=pltpu.SMEM),
                  pl.BlockSpec((chunk * ROW_TILES, LANES), lambda c, *_: (c, 0))],
        out_specs=pl.BlockSpec(memory_space=pl.ANY),
        scratch_shapes=[pltpu.VMEM((2, TOP_K * chunk * ROW_TILES, LANES), F32),
                        pltpu.VMEM((blk // 2 * ROW_TILES, LANES), F32),
                        pltpu.SemaphoreType.DMA(()), pltpu.SemaphoreType.DMA(())],
    )
    return pl.pallas_call(
        functools.partial(_dispatch_body, chunk=chunk, nch=nch, blk=blk),
        grid_spec=gs,
        out_shape=jax.ShapeDtypeStruct((n_slots * ROW_TILES, LANES), F32),
        compiler_params=_cparams("arbitrary"),
        name="dispatch",
    )(plan['run_n'], plan['run_g'], plan['run_l'], plan['pad_start'], plan['pad_n'], plan['tail'], plan['lpos'], h8)


def _experts_body(be_ref, nv_ref, xs_ref, wg_ref, wu_ref, wd_ref, ys_ref, *wcast, precise, blk):
    j = pl.program_id(0)

    @pl.when(j < nv_ref[0])
    def _():
        x = _load_row_tiles(xs_ref, blk)
        if precise:
            wg, wu, wd = wg_ref[0], wu_ref[0], wd_ref[0]
        else:
            wgb, wub, wdb = wcast

            @pl.when((j == 0) | (be_ref[j] != be_ref[jnp.maximum(j - 1, 0)]))
            def _():
                wgb[...] = wg_ref[0].astype(BF16)
                wub[...] = wu_ref[0].astype(BF16)
                wdb[...] = wd_ref[0].astype(BF16)

            wg, wu, wd = wgb[...], wub[...], wdb[...]
        gt = _mm(x, wg, precise)
        up = _mm(x, wu, precise)
        hid = gt * jax.nn.sigmoid(gt) * up
        _store_row_tiles(ys_ref, _mm(hid, wd, precise), blk)

    @pl.when(j >= nv_ref[0])
    def _():
        ys_ref[...] = jnp.zeros(ys_ref.shape, F32)


def experts(xs, plan, wg, wu, wd, layer, *, precise):
    nblk, rows = plan['nblk'], plan['blk'] * ROW_TILES
    d, ff = wg.shape[2], wg.shape[3]
    blk = lambda j, be, nv: (jnp.minimum(j, nv[0] - 1), 0)
    wspec = lambda shape: pl.BlockSpec((None, 1) + shape,
                                       lambda j, be, nv: (layer, be[jnp.minimum(j, nv[0] - 1)], 0, 0))
    gs = pltpu.PrefetchScalarGridSpec(
        num_scalar_prefetch=2,
        grid=(nblk,),
        in_specs=[pl.BlockSpec((rows, LANES), blk),
                  wspec((d, ff)), wspec((d, ff)), wspec((ff, d))],
        out_specs=pl.BlockSpec((rows, LANES), lambda j, be, nv: (j, 0)),
        scratch_shapes=[] if precise else [pltpu.VMEM((d, ff), BF16), pltpu.VMEM((d, ff), BF16),
                                           pltpu.VMEM((ff, d), BF16)],
    )
    return pl.pallas_call(
        functools.partial(_experts_body, precise=precise, blk=plan['blk']),
        grid_spec=gs,
        out_shape=jax.ShapeDtypeStruct(xs.shape, F32),
        compiler_params=_cparams("arbitrary"),
        name="experts",
    )(plan['block_e'], plan['n_used'], xs, wg, wu, wd)


def moe_plan(lpos128, gate128, runn8, blk):
    n_tok = lpos128.shape[0]
    nch = runn8.shape[0] // SUBLANES
    chunk = n_tok // nch
    n_assign = n_tok * TOP_K
    nblk = (n_assign + N_EXPERTS * (blk - 1)) // blk
    run_n = runn8.reshape(nch, SUBLANES, LANES)[:, 0, :N_EXPERTS]
    lpos = lpos128[:, :TOP_K]
    gate = gate128[:, :TOP_K]
    counts = jnp.sum(run_n, axis=0)
    padded = (counts + blk - 1) // blk * blk
    pends = jnp.cumsum(padded)
    pstarts = pends - padded
    run_g = pstarts[None, :] + jnp.cumsum(run_n, axis=0) - run_n
    run_l = jnp.cumsum(run_n, axis=1) - run_n
    blk_start = jnp.arange(nblk, dtype=I32) * blk
    block_e = jnp.minimum(jnp.sum((pends[None, :] <= blk_start[:, None]).astype(I32), axis=1), N_EXPERTS - 1)
    return dict(chunk=chunk, nch=nch, nblk=nblk, blk=blk,
                run_n=run_n.reshape(-1).astype(I32), run_g=run_g.reshape(-1).astype(I32),
                run_l=run_l.reshape(-1).astype(I32), pad_start=(pstarts + counts).astype(I32),
                pad_n=(padded - counts).astype(I32), lpos=lpos.reshape(nch, 1, TOP_K * chunk).astype(I32),
                gate=gate.reshape(nch, 1, TOP_K * chunk), block_e=block_e.astype(I32),
                n_used=(pends[-1:] // blk).astype(I32),
                tail=jnp.stack([pends[-1], 2 * (nblk - pends[-1] // blk)]).astype(I32))


def _combine(rn_ref, rg_ref, rl_ref, lpos_ref, gate_ref, x_ref, ys_hbm, ystage, comb, sem, *, chunk, nch):
    c = pl.program_id(0)
    slot = c % 2
    pieces = _pow2_pieces(chunk)

    def fetch(cc, sl):
        def per_e(e, carry):
            k = cc * N_EXPERTS + e

            def one(off, p):
                pltpu.make_async_copy(ys_hbm.at[_tile_rows(rg_ref[k] + off, p)],
                                      ystage.at[sl, _tile_rows(rl_ref[k] + off, p)], sem.at[sl]).start()
            _for_each_piece(rn_ref[k], pieces, one)
            return carry
        lax.fori_loop(0, N_EXPERTS, per_e, 0)

    @pl.when(c == 0)
    def _():
        fetch(0, 0)

    @pl.when(c + 1 < nch)
    def _():
        fetch(c + 1, 1 - slot)

    pltpu.make_async_copy(ystage.at[slot], ystage.at[slot], sem.at[slot]).wait()

    def per_tok(t, carry):
        y0 = ystage[slot, _tile_rows(lpos_ref[0, 0, TOP_K * t]), :]
        y1 = ystage[slot, _tile_rows(lpos_ref[0, 0, TOP_K * t + 1]), :]
        comb[_tile_rows(t), :] = gate_ref[0, 0, TOP_K * t] * y0 + gate_ref[0, 0, TOP_K * t + 1] * y1
        return carry
    lax.fori_loop(0, chunk, per_tok, 0, unroll=8)
    return x_ref[...] + _load_row_tiles(comb, chunk)


def _combine_glu_body(rn_ref, rg_ref, rl_ref, lpos_ref, gate_ref, x_ref, ys_hbm, g_ref, w_ref, b_ref,
                      x2_ref, u_ref, ystage, comb, sem, *, chunk, nch, precise):
    x2 = _combine(rn_ref, rg_ref, rl_ref, lpos_ref, gate_ref, x_ref, ys_hbm, ystage, comb, sem, chunk=chunk, nch=nch)
    x2_ref[...] = x2
    zz = _mm(_rms(x2, g_ref[...]), w_ref[...], precise) + b_ref[...]
    half = zz.shape[1] // 2
    u_ref[...] = zz[:, :half] * jax.nn.sigmoid(zz[:, half:])


def _combine_final_body(rn_ref, rg_ref, rl_ref, lpos_ref, gate_ref, x_ref, ys_hbm, g_ref, o_ref,
                        ystage, comb, sem, *, chunk, nch):
    x2 = _combine(rn_ref, rg_ref, rl_ref, lpos_ref, gate_ref, x_ref, ys_hbm, ystage, comb, sem, chunk=chunk, nch=nch)
    o_ref[...] = _rms(x2, g_ref[...])


def _combine_call(body, plan, x, ys, extra, extra_specs, out_specs, out_shape, name):
    chunk, nch = plan['chunk'], plan['nch']
    d = x.shape[1]
    smem_blk = pl.BlockSpec((1, 1, TOP_K * chunk), lambda c, *_: (c, 0, 0), memory_space=pltpu.SMEM)
    gs = pltpu.PrefetchScalarGridSpec(
        num_scalar_prefetch=3,
        grid=(nch,),
        in_specs=[smem_blk, smem_blk, pl.BlockSpec((chunk, d), lambda c, *_: (c, 0)),
                  pl.BlockSpec(memory_space=pl.ANY)] + extra_specs,
        out_specs=out_specs,
        scratch_shapes=[pltpu.VMEM((2, TOP_K * chunk * ROW_TILES, LANES), F32),
                        pltpu.VMEM((chunk * ROW_TILES, LANES), F32), pltpu.SemaphoreType.DMA((2,))],
    )
    return pl.pallas_call(
        functools.partial(body, chunk=chunk, nch=nch),
        grid_spec=gs,
        out_shape=out_shape,
        compiler_params=_cparams("arbitrary"),
        name=name,
    )(plan['run_n'], plan['run_g'], plan['run_l'], plan['lpos'], plan['gate'], x, ys, *extra)


def combine_glu(x, ys, plan, g, w, b, *, precise):
    rows, d = x.shape
    chunk = plan['chunk']
    cols = w.shape[1]
    const = lambda shape: pl.BlockSpec(shape, lambda c, *_: (0,) * len(shape))
    row_spec = lambda width: pl.BlockSpec((chunk, width), lambda c, *_: (c, 0))
    return _combine_call(
        functools.partial(_combine_glu_body, precise=precise), plan, x, ys, [g, w, b],
        [const((1, d)), const((d, cols)), const((1, cols))], [row_spec(d), row_spec(cols // 2)],
        [jax.ShapeDtypeStruct((rows, d), F32), jax.ShapeDtypeStruct((rows, cols // 2), F32)], "combine_glu")


def combine_final(x, ys, plan, g):
    rows, d = x.shape
    chunk = plan['chunk']
    return _combine_call(
        _combine_final_body, plan, x, ys, [g], [pl.BlockSpec((1, d), lambda c, *_: (0, 0))],
        pl.BlockSpec((chunk, d), lambda c, *_: (c, 0)), jax.ShapeDtypeStruct((rows, d), F32), "combine_final")


CONV_TILE = 256
CONV_HIST = 32


def _ln_swish(y, g, b):
    yc = y - jnp.mean(y, axis=-1, keepdims=True)
    yn = yc * lax.rsqrt(jnp.mean(yc * yc, axis=-1, keepdims=True) + EPS) * g + b
    return yn * jax.nn.sigmoid(yn)


CONV_ROWS = 64
LN_ROWS = 16


def _dwconv_prompt_body(u_ref, w_ref, bdw_ref, g_ref, b_ref, o_ref, ext, y_sc, *, tt):
    t = pl.program_id(1)
    n_lt = ext.shape[0]

    @pl.when(t == 0)
    def _():
        ext[:, 0:CONV_HIST, :] = jnp.zeros((n_lt, CONV_HIST, LANES), F32)

    @pl.when(t > 0)
    def _():
        ext[:, 0:CONV_HIST, :] = ext[:, tt:tt + CONV_HIST, :]

    for j in range(n_lt):
        ext[j, CONV_HIST:CONV_HIST + tt, :] = u_ref[:, j * LANES:(j + 1) * LANES]
    off = CONV_HIST - (C_KERNEL - 1)
    for j in range(n_lt):
        wj = w_ref[:, j * LANES:(j + 1) * LANES]
        bj = bdw_ref[:, j * LANES:(j + 1) * LANES]
        for c in range(tt // CONV_ROWS):
            acc = ext[j, pl.ds(off + c * CONV_ROWS, CONV_ROWS), :] * wj[0:1] + bj
            for k in range(1, C_KERNEL):
                acc = acc + ext[j, pl.ds(off + k + c * CONV_ROWS, CONV_ROWS), :] * wj[k:k + 1]
            y_sc[c * CONV_ROWS:(c + 1) * CONV_ROWS, j * LANES:(j + 1) * LANES] = acc

    def ln_rows(r, carry):
        rows = pl.ds(pl.multiple_of(r * LN_ROWS, LN_ROWS), LN_ROWS)
        o_ref[rows, :] = _ln_swish(y_sc[rows, :], g_ref[...], b_ref[...]).astype(o_ref.dtype)
        return carry
    lax.fori_loop(0, tt // LN_ROWS, ln_rows, 0)


def dwconv_prompt(u, w, b_dw, ln_g, ln_b, bsz, seq):
    tt = min(CONV_TILE, seq)
    nt = seq // tt
    d = u.shape[1]
    const = lambda shape: pl.BlockSpec(shape, lambda b, t: (0,) * len(shape))
    return pl.pallas_call(
        functools.partial(_dwconv_prompt_body, tt=tt),
        grid=(bsz, nt),
        in_specs=[pl.BlockSpec((tt, d), lambda b, t: (b * nt + t, 0)), const((C_KERNEL, d)), const((1, d)),
                  const((1, d)), const((1, d))],
        out_specs=pl.BlockSpec((tt, d), lambda b, t: (b * nt + t, 0)),
        out_shape=jax.ShapeDtypeStruct((bsz * seq, d), BF16),
        scratch_shapes=[pltpu.VMEM((d // LANES, CONV_HIST + tt, LANES), F32), pltpu.VMEM((tt, d), F32)],
        compiler_params=_cparams("arbitrary", "arbitrary"),
        name="dwconv_prompt",
    )(u, w, b_dw, ln_g, ln_b)


def _dwconv_sample_body(u_ref, buf_ref, w_ref, bdw_ref, g_ref, b_ref, o_ref, nbuf_ref, *, tb):
    w = w_ref[...]
    rows = []
    for i in range(tb):
        hist = buf_ref[i]
        ur = u_ref[i:i + 1, :]
        rows.append(jnp.sum(hist * w[:C_KERNEL - 1], axis=0, keepdims=True) + ur * w[C_KERNEL - 1:C_KERNEL])
        nbuf_ref[i] = jnp.concatenate([hist[1:], ur], axis=0)
    y = jnp.concatenate(rows, axis=0) + bdw_ref[...]
    o_ref[...] = _ln_swish(y, g_ref[...], b_ref[...])


def dwconv_sample(u, buf, w, b_dw, ln_g, ln_b):
    nb, d = u.shape
    tb = min(SAMPLE_TILE, nb)
    const = lambda shape: pl.BlockSpec(shape, lambda i: (0,) * len(shape))
    return pl.pallas_call(
        functools.partial(_dwconv_sample_body, tb=tb),
        grid=(nb // tb,),
        in_specs=[pl.BlockSpec((tb, d), lambda i: (i, 0)), pl.BlockSpec((tb, C_KERNEL - 1, d), lambda i: (i, 0, 0)),
                  const((C_KERNEL, d)), const((1, d)), const((1, d)), const((1, d))],
        out_specs=[pl.BlockSpec((tb, d), lambda i: (i, 0)), pl.BlockSpec((tb, C_KERNEL - 1, d), lambda i: (i, 0, 0))],
        out_shape=[jax.ShapeDtypeStruct((nb, d), F32), jax.ShapeDtypeStruct(buf.shape, F32)],
        compiler_params=_cparams("parallel"),
        name="dwconv_sample",
    )(u, buf, w, b_dw, ln_g, ln_b)


def _moe(h8, lpos128, gate128, runn8, wg, wu, wd, layer, *, precise):
    plan = moe_plan(lpos128, gate128, runn8, EXPERT_BLOCK_PRECISE if precise else EXPERT_BLOCK)
    xs = dispatch(h8, plan)
    return experts(xs, plan, wg, wu, wd, layer, precise=precise), plan


def _trunk(x, caches, p, *, prompt):
    bsz, seq, d = x.shape
    rows = bsz * seq
    precise = not prompt
    wdt = F32 if precise else BF16
    xf = x.reshape(rows, d)
    row = lambda v: v.reshape(1, -1).astype(F32)

    z = norm_proj(xf, row(p['norm_mix'][0]), p['w_in'].astype(wdt), precise=precise)
    if prompt:
        att = attn_prompt(z, p['rel_table'], p['sinks'], bsz, seq)
        out_b, c1, n1, m1 = mlstm_prompt(z, p['conv_w'], p['b_gates'], p['g_mnorm'], bsz, seq)
        z3 = z.reshape(bsz, seq, Z_COLS)
        new_k = z3[:, seq - WINDOW:, Z_KA:Z_KA + A_KV].reshape(bsz, WINDOW, A_KV_HEADS, A_HEAD_DIM)
        new_v = z3[:, seq - WINDOW:, Z_VA:Z_VA + A_KV].reshape(bsz, WINDOW, A_KV_HEADS, A_HEAD_DIM)
        new_conv = z3[:, seq - (B_CONV - 1):, Z_QK:Z_QK + 2 * B_QK]
        n1 = n1[:, :B_HEADS]
        m1 = m1[:, :B_HEADS, 0]
    else:
        ck, cv, c0, n0, m0, cbuf = caches[:6]
        n_buf = ck.shape[1]
        att, out_b, new_k, new_v, c1, n1, m1, new_conv = mix_sample(
            z, p['rel_table'], p['sinks'], ck.reshape(bsz, n_buf, A_KV), cv.reshape(bsz, n_buf, A_KV),
            c0, n0, m0, cbuf, p['conv_w'], p['b_gates'], p['g_mnorm'])
        new_k = new_k.reshape(bsz, n_buf, A_KV_HEADS, A_HEAD_DIM)
        new_v = new_v.reshape(bsz, n_buf, A_KV_HEADS, A_HEAD_DIM)
        m1 = m1[:, :B_HEADS]
    w_out = p['w_out'].astype(wdt)
    x1, h8, lpos, gate, runn = proj_router([att, out_b], [w_out[:A_Q], w_out[A_Q:]], None, xf,
                                    row(p['norm_ffn'][0]), p['w_router'][0], p['b_router'][0], precise=precise)
    ys, plan = _moe(h8, lpos, gate, runn, p['w_eg'], p['w_eu'], p['w_ed'], 0, precise=precise)

    x2, u = combine_glu(x1, ys, plan, row(p['norm_mix'][1]), p['w_pw1'].astype(wdt), row(p['b_pw1']),
                        precise=precise)
    if prompt:
        yc = dwconv_prompt(u, p['w_dw'], row(p['b_dw']), row(p['ln_g']), row(p['ln_b']), bsz, seq)
        new_cbuf = u.reshape(bsz, seq, d)[:, seq - (C_KERNEL - 1):]
    else:
        yc, new_cbuf = dwconv_sample(u, caches[6], p['w_dw'], row(p['b_dw']), row(p['ln_g']), row(p['ln_b']))
    x3, h8, lpos, gate, runn = proj_router([yc], [p['w_pw2'].astype(wdt)], row(p['b_pw2']), x2,
                                    row(p['norm_ffn'][1]), p['w_router'][1], p['b_router'][1], precise=precise)
    ys, plan = _moe(h8, lpos, gate, runn, p['w_eg'], p['w_eu'], p['w_ed'], 1, precise=precise)
    y = combine_final(x3, ys, plan, row(p['norm_final']))
    add_layer = lambda t: t[None]
    return (y.reshape(bsz, seq, d),) + tuple(add_layer(t) for t in (new_k, new_v, c1, n1, m1, new_conv, new_cbuf))


def kernel(x_prompt, x_sample, cache_win_k, cache_win_v, state_mlstm_c, state_mlstm_n, state_mlstm_m, state_mlstm_conv, state_conv, norm_mix, norm_ffn, norm_final, rel_bias_table, w_in_mix, b_mlstm_gates, w_mlstm_qk_conv, attn_sinks, g_mlstm_norm, w_out_mix, w_pw1, b_pw1, w_dw, b_dw, ln_conv_g, ln_conv_b, w_pw2, b_pw2, w_router_group, b_router_group, w_router_expert, b_router_expert, w_expert_gate, w_expert_up, w_expert_down):
    w_in = w_in_mix[0]
    s_q, s_k, s_v, s_qk, s_vb, s_g = A_Q, A_Q + A_KV, A_Q + 2 * A_KV, A_Q + 2 * A_KV + 2 * B_QK, \
        A_Q + 2 * A_KV + 2 * B_QK + B_V, A_Q + 2 * A_KV + 2 * B_QK + B_V + 2 * B_HEADS
    w_in_r = jnp.concatenate([w_in[:, :s_q], w_in[:, s_v:s_qk], w_in[:, s_qk:s_vb], w_in[:, s_g:],
                              w_in[:, s_q:s_k], w_in[:, s_k:s_v], w_in[:, s_vb:s_g],
                              jnp.zeros((D_MODEL, LANES - 2 * B_HEADS), F32)], axis=1)
    b_gates = jnp.concatenate([b_mlstm_gates[0], jnp.zeros((LANES - 2 * B_HEADS,), F32)]).reshape(1, LANES)
    depth = w_router_group.shape[0]
    w_re = jnp.transpose(w_router_expert, (0, 2, 1, 3)).reshape(depth, D_MODEL, N_EXPERTS)
    w_router = jnp.concatenate([w_router_group, w_re,
                                jnp.zeros((depth, D_MODEL, LANES - N_GROUPS - N_EXPERTS), F32)], axis=-1)
    b_router = jnp.concatenate([b_router_group, b_router_expert.reshape(depth, N_EXPERTS),
                                jnp.zeros((depth, LANES - N_GROUPS - N_EXPERTS), F32)], axis=-1)[:, None, :]
    p = dict(norm_mix=norm_mix, norm_ffn=norm_ffn, norm_final=norm_final, rel_table=rel_bias_table,
             sinks=attn_sinks[0], w_in=w_in_r, b_gates=b_gates, conv_w=w_mlstm_qk_conv[0],
             g_mnorm=g_mlstm_norm[0].reshape(1, B_V), w_out=w_out_mix[0], w_pw1=w_pw1[0], b_pw1=b_pw1[0],
             w_dw=w_dw[0], b_dw=b_dw[0], ln_g=ln_conv_g[0], ln_b=ln_conv_b[0], w_pw2=w_pw2[0], b_pw2=b_pw2[0],
             w_router=w_router, b_router=b_router, w_eg=w_expert_gate, w_eu=w_expert_up, w_ed=w_expert_down)
    caches = (cache_win_k[0], cache_win_v[0], state_mlstm_c[0], state_mlstm_n[0], state_mlstm_m[0],
              state_mlstm_conv[0], state_conv[0])
    out_p = _trunk(x_prompt, None, p, prompt=True)
    out_s = _trunk(x_sample, caches, p, prompt=False)
    return (out_p[0], out_s[0]) + out_p[1:] + out_s[1:]
```

```python
import functools
import math

import numpy as np
import jax
import jax.numpy as jnp
from jax import lax
from jax.experimental import pallas as pl
from jax.experimental.pallas import tpu as pltpu

F32 = jnp.float32
BF16 = jnp.bfloat16
I32 = jnp.int32
HI = lax.Precision.HIGHEST
NEG_INF = float("-inf")

LANES = 128
SUBLANES = 8
VMEM_LIMIT = 56 * 1024 * 1024

D_MODEL = 1024
A_HEADS = 8
A_KV_HEADS = 2
A_GROUP = A_HEADS // A_KV_HEADS
A_HEAD_DIM = 64
WINDOW = 128
REL_BUCKETS = 32
REL_MAX_DIST = 128
B_HEADS = 4
B_DK = 64
B_DV = 128
B_CONV = 4
C_KERNEL = 31
N_GROUPS = 4
EXPERTS_PER_GROUP = 8
N_EXPERTS = N_GROUPS * EXPERTS_PER_GROUP
TOP_K = 2
EXPERT_FF = D_MODEL // 2
EXPERT_BLOCK = 256
EXPERT_BLOCK_PRECISE = 128
EPS = 1e-6

A_Q = A_HEADS * A_HEAD_DIM
A_KV = A_KV_HEADS * A_HEAD_DIM
B_QK = B_HEADS * B_DK
B_V = B_HEADS * B_DV
ROW_TILES = D_MODEL // LANES

Z_QA, Z_QK, Z_VB, Z_OG, Z_KA, Z_VA, Z_GATES = 0, 512, 1024, 1536, 2048, 2176, 2304
Z_COLS = 2432
MLSTM_CHUNK = 128


def _cparams(*sem):
    return pltpu.CompilerParams(dimension_semantics=sem, vmem_limit_bytes=VMEM_LIMIT)


def _rms(x, g):
    return x * lax.rsqrt(jnp.mean(x * x, axis=-1, keepdims=True) + EPS) * g


def _mm(a, w, precise):
    if precise:
        return jnp.dot(a.astype(F32), w, precision=HI, preferred_element_type=F32)
    return jnp.dot(a.astype(BF16), w, preferred_element_type=F32)


def _t5_buckets(dist):
    exact = REL_BUCKETS // 2
    d = np.maximum(dist, 0)
    large = exact + (np.log(np.maximum(d, 1).astype(np.float32) / exact)
                     / math.log(REL_MAX_DIST / exact) * (REL_BUCKETS - exact)).astype(np.int32)
    return np.where(d < exact, d, np.minimum(large, REL_BUCKETS - 1)).astype(np.int32)


def _norm_proj_body(x_ref, g_ref, w_ref, o_ref, *, precise):
    h = _rms(x_ref[...], g_ref[...])
    o_ref[...] = _mm(h, w_ref[...], precise)


def norm_proj(x, g, w, *, precise):
    rows, d = x.shape
    cols = w.shape[1]
    tm = min(rows, 128 if precise else 512)
    return pl.pallas_call(
        functools.partial(_norm_proj_body, precise=precise),
        grid=(rows // tm,),
        in_specs=[pl.BlockSpec((tm, d), lambda i: (i, 0)),
                  pl.BlockSpec((1, d), lambda i: (0, 0)),
                  pl.BlockSpec((d, cols), lambda i: (0, 0))],
        out_specs=pl.BlockSpec((tm, cols), lambda i: (i, 0)),
        out_shape=jax.ShapeDtypeStruct((rows, cols), F32),
        compiler_params=_cparams("parallel"),
        name="norm_proj",
    )(x, g, w)


def _attn_prompt_body(tab_ref, sink_ref, bkt_ref, q_ref, kp_ref, kc_ref, vp_ref, vc_ref, o_ref, bias_ref):
    b = pl.program_id(0)
    n = pl.program_id(1)

    @pl.when((b == 0) & (n == 0))
    def _():
        bk = bkt_ref[...]
        first = lax.broadcasted_iota(I32, bk.shape, 1) >= WINDOW
        for h in range(A_HEADS):
            acc = jnp.full(bk.shape, NEG_INF, F32)
            for t in range(REL_BUCKETS):
                acc = jnp.where(bk == t, tab_ref[t, h], acc)
            bias_ref[h] = acc
            bias_ref[A_HEADS + h] = jnp.where(first, acc, NEG_INF)

    q = q_ref[...] * (A_HEAD_DIM ** -0.5)
    kb = jnp.concatenate([kp_ref[...], kc_ref[...]], axis=0).astype(BF16)
    vb = jnp.concatenate([vp_ref[...], vc_ref[...]], axis=0).astype(BF16)
    table = jnp.where(n == 0, A_HEADS, 0)
    outs = []
    for h in range(A_HEADS):
        kvh = h // A_GROUP
        qh = q[:, h * A_HEAD_DIM:(h + 1) * A_HEAD_DIM].astype(BF16)
        kh = kb[:, kvh * A_HEAD_DIM:(kvh + 1) * A_HEAD_DIM]
        vh = vb[:, kvh * A_HEAD_DIM:(kvh + 1) * A_HEAD_DIM]
        s = lax.dot_general(qh, kh, (((1,), (1,)), ((), ())), preferred_element_type=F32)
        s = s + bias_ref[table + h]
        sink = sink_ref[h]
        mx = jnp.maximum(jnp.max(s, axis=-1, keepdims=True), sink)
        p = jnp.exp(s - mx)
        den = jnp.sum(p, axis=-1, keepdims=True) + jnp.exp(sink - mx)
        outs.append(jnp.dot(p.astype(BF16), vh, preferred_element_type=F32) / den)
    o_ref[...] = jnp.concatenate(outs, axis=1).astype(o_ref.dtype)


def attn_prompt(z, rel_table, sinks, bsz, seq):
    nb = seq // WINDOW
    dist = WINDOW + np.arange(WINDOW)[:, None] - np.arange(2 * WINDOW)[None, :]
    bkt = np.where((dist >= 0) & (dist <= WINDOW), _t5_buckets(dist), -1).astype(np.int32)
    kcol, vcol = Z_KA // LANES, Z_VA // LANES
    smem = pl.BlockSpec(memory_space=pltpu.SMEM)

    def cur(c):
        return pl.BlockSpec((WINDOW, LANES), lambda b, n: (b * nb + n, c))

    def prev(c):
        return pl.BlockSpec((WINDOW, LANES), lambda b, n: (b * nb + jnp.maximum(n - 1, 0), c))

    return pl.pallas_call(
        _attn_prompt_body,
        grid=(bsz, nb),
        in_specs=[smem, smem,
                  pl.BlockSpec((WINDOW, 2 * WINDOW), lambda b, n: (0, 0)),
                  pl.BlockSpec((WINDOW, A_Q), lambda b, n: (b * nb + n, 0)),
                  prev(kcol), cur(kcol), prev(vcol), cur(vcol)],
        out_specs=pl.BlockSpec((WINDOW, A_Q), lambda b, n: (b * nb + n, 0)),
        out_shape=jax.ShapeDtypeStruct((bsz * seq, A_Q), BF16),
        scratch_shapes=[pltpu.VMEM((2 * A_HEADS, WINDOW, 2 * WINDOW), F32)],
        compiler_params=_cparams("arbitrary", "arbitrary"),
        name="attn_prompt",
    )(rel_table, sinks, jnp.asarray(bkt), z, z, z, z, z)


def _log_sigmoid(x):
    return -(jnp.maximum(-x, 0.0) + jnp.log1p(jnp.exp(-jnp.abs(x))))


def _mlstm_prompt_body(qk_ref, v_ref, g_ref, og_ref, cw_ref, bg_ref, gn_ref,
                       ob_ref, c_out, n_out, m_out,
                       c_sc, n_sc, m_sc, hist_sc, *, L, nc, nbat):
    ci = pl.program_id(1)

    @pl.when(ci == 0)
    def _():
        c_sc[...] = jnp.zeros(c_sc.shape, F32)
        n_sc[...] = jnp.zeros(n_sc.shape, F32)
        m_sc[...] = jnp.full(m_sc.shape, NEG_INF, F32)
        hist_sc[...] = jnp.zeros(hist_sc.shape, F32)

    row = lax.broadcasted_iota(I32, (L, L), 0)
    colm = lax.broadcasted_iota(I32, (L, L), 1)
    causal = colm <= row
    tril = causal.astype(F32)
    stores = []
    for bb in range(nbat):
        stores += _mlstm_chunk(qk_ref.at[bb], v_ref.at[bb], g_ref.at[bb], og_ref.at[bb], cw_ref, bg_ref, gn_ref,
                               ob_ref.at[bb], c_sc.at[bb], n_sc.at[bb], m_sc.at[bb], hist_sc.at[bb],
                               causal, tril, L)
    for store in stores:
        store()

    @pl.when(ci == nc - 1)
    def _():
        c_out[...] = c_sc[...]
        n_out[...] = n_sc[...]
        m_out[...] = m_sc[...]


def _mlstm_chunk(qk_ref, v_ref, g_ref, og_ref, cw_ref, bg_ref, gn_ref, ob_ref, c_sc, n_sc, m_sc, hist_sc,
                 causal, tril, L):
    cur = qk_ref[...]
    ext = jnp.concatenate([hist_sc[...], cur], axis=0)
    cw = cw_ref[...]
    off = SUBLANES - (B_CONV - 1)
    conv = ext[off:off + L] * cw[0:1]
    for j in range(1, B_CONV):
        conv = conv + ext[off + j:off + j + L] * cw[j:j + 1]
    qk = conv * jax.nn.sigmoid(conv)
    q_all = qk[:, :B_QK]
    k_all = qk[:, B_QK:] * (B_DK ** -0.5)
    v_all = v_ref[...]
    og = og_ref[...]
    gn = gn_ref[...]

    G = g_ref[...] + bg_ref[...]
    lf = _log_sigmoid(G)
    Bc = jnp.dot(tril, lf, precision=HI, preferred_element_type=F32)
    BT = Bc.T
    GT = G.T

    outs, stores = [], []
    for h in range(B_HEADS):
        qh = q_all[:, h * B_DK:(h + 1) * B_DK]
        kh = k_all[:, h * B_DK:(h + 1) * B_DK]
        vh = v_all[:, h * B_DV:(h + 1) * B_DV]
        b_col = Bc[:, B_HEADS + h:B_HEADS + h + 1]
        ig_col = G[:, h:h + 1]
        b_row = BT[B_HEADS + h:B_HEADS + h + 1, :]
        ig_row = GT[h:h + 1, :]
        c0 = c_sc[h]
        n0 = n_sc[h:h + 1, :]
        m0 = m_sc[h:h + 1, 0:1]
        a = b_col + m0
        d = jnp.where(causal, b_col - b_row + ig_row, NEG_INF)
        m = jnp.maximum(a, jnp.max(d, axis=-1, keepdims=True))
        dw = jnp.exp(d - m)
        aw = jnp.exp(a - m)
        qb = qh.astype(BF16)
        vb = vh.astype(BF16)
        s = lax.dot_general(qb, kh.astype(BF16), (((1,), (1,)), ((), ())), preferred_element_type=F32) * dw
        num = (jnp.dot(s.astype(BF16), vb, preferred_element_type=F32)
               + aw * jnp.dot(qb, c0.astype(BF16), preferred_element_type=F32))
        den = jnp.sum(s, axis=-1, keepdims=True) + aw * jnp.sum(qh * n0, axis=-1, keepdims=True)
        hh = num / jnp.maximum(jnp.abs(den), jnp.exp(-m))
        m_last = m[L - 1:L, :]
        wl = jnp.exp(b_col[L - 1:L, :] - b_col + ig_col - m_last)
        decay = aw[L - 1:L, :]
        kw = kh * wl
        c1 = decay * c0 + lax.dot_general(kw.astype(BF16), vb, (((0,), (0,)), ((), ())),
                                          preferred_element_type=F32)
        n1 = decay * n0 + jnp.sum(kw, axis=0, keepdims=True)
        stores.append(functools.partial(_store_state, c_sc, n_sc, m_sc, h, c1, n1, m_last))
        hn = hh * lax.rsqrt(jnp.mean(hh * hh, axis=-1, keepdims=True) + EPS) * gn[:, h * B_DV:(h + 1) * B_DV]
        outs.append(jax.nn.sigmoid(og[:, h * B_DV:(h + 1) * B_DV]) * hn)
    out = jnp.concatenate(outs, axis=1).astype(ob_ref.dtype)
    stores.append(functools.partial(_store_chunk, ob_ref, hist_sc, out, cur[L - SUBLANES:L]))
    return stores


def _store_state(c_sc, n_sc, m_sc, h, c1, n1, m_last):
    c_sc[h] = c1
    n_sc[h:h + 1, :] = n1
    m_sc[h:h + 1, :] = jnp.broadcast_to(m_last, (1, LANES))


def _store_chunk(ob_ref, hist_sc, out, tail):
    ob_ref[...] = out
    hist_sc[...] = tail


MLSTM_BATCH = 1


def mlstm_prompt(z, conv_w, b_gates_pad, g_norm, bsz, seq):
    L = MLSTM_CHUNK
    nc = seq // L
    nbat = MLSTM_BATCH if bsz % MLSTM_BATCH == 0 else 1
    z3 = z.reshape(bsz, seq, Z_COLS)

    def zspec(width, colblk):
        return pl.BlockSpec((nbat, L, width), lambda b, c: (b, c, colblk))

    const = lambda shape: pl.BlockSpec(shape, lambda b, c: (0,) * len(shape))
    state = lambda shape: pl.BlockSpec((nbat,) + shape, lambda b, c: (b,) + (0,) * len(shape))
    out_b, c1, n1, m1 = pl.pallas_call(
        functools.partial(_mlstm_prompt_body, L=L, nc=nc, nbat=nbat),
        grid=(bsz // nbat, nc),
        in_specs=[zspec(2 * B_QK, Z_QK // (2 * B_QK)), zspec(B_V, Z_VB // B_V), zspec(LANES, Z_GATES // LANES),
                  zspec(B_V, Z_OG // B_V), const((B_CONV, 2 * B_QK)), const((1, LANES)), const((1, B_V))],
        out_specs=[pl.BlockSpec((nbat, L, B_V), lambda b, c: (b, c, 0)),
                   state((B_HEADS, B_DK, B_DV)), state((SUBLANES, B_DK)), state((SUBLANES, LANES))],
        out_shape=[jax.ShapeDtypeStruct((bsz, seq, B_V), BF16),
                   jax.ShapeDtypeStruct((bsz, B_HEADS, B_DK, B_DV), F32),
                   jax.ShapeDtypeStruct((bsz, SUBLANES, B_DK), F32),
                   jax.ShapeDtypeStruct((bsz, SUBLANES, LANES), F32)],
        scratch_shapes=[pltpu.VMEM((nbat, B_HEADS, B_DK, B_DV), F32), pltpu.VMEM((nbat, SUBLANES, B_DK), F32),
                        pltpu.VMEM((nbat, SUBLANES, LANES), F32), pltpu.VMEM((nbat, SUBLANES, 2 * B_QK), F32)],
        compiler_params=_cparams("arbitrary", "arbitrary"),
        name="mlstm_prompt",
    )(z3, z3, z3, z3, conv_w, b_gates_pad, g_norm)
    return out_b.reshape(bsz * seq, B_V), c1, n1, m1


SAMPLE_TILE = 8


def _row_to_col(row, n):
    eye = lax.broadcasted_iota(I32, (n, n), 0) == lax.broadcasted_iota(I32, (n, n), 1)
    return jnp.sum(jnp.where(eye, jnp.broadcast_to(row, (n, n)), 0.0), axis=1, keepdims=True)


def _mix_sample_body(tab_ref, sink_ref, bkt_ref, z_ref, ck_ref, cv_ref, c0_ref, n0_ref, m0_ref, cb_ref,
                     cw_ref, bg_ref, gn_ref,
                     att_ref, ob_ref, nk_ref, nv_ref, c1_ref, n1_ref, m1_ref, ncb_ref, *, tb):
    bk = bkt_ref[...]
    cw = cw_ref[...]
    gn = gn_ref[...]
    zero_half = jnp.zeros((1, A_HEAD_DIM), F32)
    bias_rows = []
    for h in range(A_HEADS):
        bias = jnp.zeros(bk.shape, F32)
        for t in range(REL_BUCKETS):
            bias = jnp.where(bk == t, tab_ref[t, h], bias)
        bias_rows.append(bias)
    bias_c = jnp.concatenate(bias_rows, axis=0)
    head_id = lax.broadcasted_iota(I32, (A_HEADS, 1), 0)
    bias_n = jnp.zeros((A_HEADS, 1), F32)
    sinks = jnp.zeros((A_HEADS, 1), F32)
    for h in range(A_HEADS):
        bias_n = jnp.where(head_id == h, tab_ref[0, h], bias_n)
        sinks = jnp.where(head_id == h, sink_ref[h], sinks)
    att_rows, ob_rows, m_rows = [], [], []
    for i in range(tb):
        zr = z_ref[i:i + 1, :]
        q_att = zr[:, Z_QA:Z_QA + A_Q] * (A_HEAD_DIM ** -0.5)
        ka = zr[:, Z_KA:Z_KA + A_KV]
        va = zr[:, Z_VA:Z_VA + A_KV]
        kc = ck_ref[i]
        vc = cv_ref[i]
        nk_ref[i] = jnp.concatenate([kc[1:], ka], axis=0)
        nv_ref[i] = jnp.concatenate([vc[1:], va], axis=0)
        q_rows = []
        for h in range(A_HEADS):
            qh = q_att[:, h * A_HEAD_DIM:(h + 1) * A_HEAD_DIM]
            q_rows.append(jnp.concatenate([qh, zero_half] if h // A_GROUP == 0 else [zero_half, qh], axis=1))
        qm = jnp.concatenate(q_rows, axis=0)
        lc = lax.dot_general(qm, kc, (((1,), (1,)), ((), ())), precision=HI,
                             preferred_element_type=F32) + bias_c
        ln = jnp.sum(qm * ka, axis=-1, keepdims=True) + bias_n
        mx = jnp.maximum(jnp.maximum(jnp.max(lc, axis=-1, keepdims=True), ln), sinks)
        pc = jnp.exp(lc - mx)
        pn = jnp.exp(ln - mx)
        den = jnp.sum(pc, axis=-1, keepdims=True) + pn + jnp.exp(sinks - mx)
        o = (jnp.dot(pc, vc, precision=HI, preferred_element_type=F32) + pn * va) / den
        att_rows.append(jnp.concatenate(
            [o[h:h + 1, (h // A_GROUP) * A_HEAD_DIM:(h // A_GROUP + 1) * A_HEAD_DIM] for h in range(A_HEADS)],
            axis=1))

        qk_pre = zr[:, Z_QK:Z_QK + 2 * B_QK]
        hist = cb_ref[i]
        conv = qk_pre * cw[B_CONV - 1:B_CONV]
        for j in range(B_CONV - 1):
            conv = conv + hist[j:j + 1] * cw[j:j + 1]
        ncb_ref[i] = jnp.concatenate([hist[1:], qk_pre], axis=0)
        qk = conv * jax.nn.sigmoid(conv)
        G = zr[:, Z_GATES:Z_GATES + LANES] + bg_ref[...]
        lfr = _log_sigmoid(G)
        og = zr[:, Z_OG:Z_OG + B_V]
        v_pre = zr[:, Z_VB:Z_VB + B_V]
        obs, ms = [], []
        for h in range(B_HEADS):
            qh = qk[:, h * B_DK:(h + 1) * B_DK]
            kh = qk[:, B_QK + h * B_DK:B_QK + (h + 1) * B_DK] * (B_DK ** -0.5)
            vh = v_pre[:, h * B_DV:(h + 1) * B_DV]
            ig = G[:, h:h + 1]
            lf = lfr[:, B_HEADS + h:B_HEADS + h + 1]
            c0 = c0_ref[i, h]
            n0 = n0_ref[i, h:h + 1, :]
            m0 = m0_ref[i:i + 1, h:h + 1]
            a = lf + m0
            m = jnp.maximum(a, ig)
            dw = jnp.exp(ig - m)
            aw = jnp.exp(a - m)
            s = jnp.sum(qh * kh, axis=-1, keepdims=True) * dw
            q_col = _row_to_col(qh, B_DK)
            k_col = _row_to_col(kh, B_DK)
            num = s * vh + aw * jnp.sum(q_col * c0, axis=0, keepdims=True)
            den = s + aw * jnp.sum(qh * n0, axis=-1, keepdims=True)
            hh = num / jnp.maximum(jnp.abs(den), jnp.exp(-m))
            c1_ref[i, h] = aw * c0 + dw * (k_col * vh)
            n1_ref[i, h:h + 1, :] = aw * n0 + dw * kh
            ms.append(m)
            hn = hh * lax.rsqrt(jnp.mean(hh * hh, axis=-1, keepdims=True) + EPS) * gn[:, h * B_DV:(h + 1) * B_DV]
            obs.append(jax.nn.sigmoid(og[:, h * B_DV:(h + 1) * B_DV]) * hn)
        ob_rows.append(jnp.concatenate(obs, axis=1))
        lane = lax.broadcasted_iota(I32, (1, LANES), 1)
        mrow = jnp.zeros((1, LANES), F32)
        for h in range(B_HEADS):
            mrow = jnp.where(lane == h, ms[h], mrow)
        m_rows.append(mrow)
    att_ref[...] = jnp.concatenate(att_rows, axis=0)
    ob_ref[...] = jnp.concatenate(ob_rows, axis=0)
    m1_ref[...] = jnp.concatenate(m_rows, axis=0)


def mix_sample(z, rel_table, sinks, ck, cv, c0, n0, m0, conv_buf, conv_w, b_gates_pad, g_norm):
    nb = z.shape[0]
    tb = min(SAMPLE_TILE, nb)
    n_buf = ck.shape[1]
    bkt = _t5_buckets(n_buf - np.arange(n_buf))[None, :]
    smem = pl.BlockSpec(memory_space=pltpu.SMEM)
    const = lambda shape: pl.BlockSpec(shape, lambda i: (0,) * len(shape))
    lead = lambda shape: pl.BlockSpec((tb,) + shape, lambda i: (i,) + (0,) * len(shape))
    return pl.pallas_call(
        functools.partial(_mix_sample_body, tb=tb),
        grid=(nb // tb,),
        in_specs=[smem, smem, const((1, n_buf)), lead((Z_COLS,)), lead((n_buf, A_KV)), lead((n_buf, A_KV)),
                  lead((B_HEADS, B_DK, B_DV)), lead((B_HEADS, B_DK)), lead((B_HEADS,)),
                  lead((B_CONV - 1, 2 * B_QK)), const((B_CONV, 2 * B_QK)), const((1, LANES)), const((1, B_V))],
        out_specs=[lead((A_Q,)), lead((B_V,)), lead((n_buf, A_KV)), lead((n_buf, A_KV)),
                   lead((B_HEADS, B_DK, B_DV)), lead((B_HEADS, B_DK)), lead((LANES,)),
                   lead((B_CONV - 1, 2 * B_QK))],
        out_shape=[jax.ShapeDtypeStruct((nb, A_Q), F32), jax.ShapeDtypeStruct((nb, B_V), F32),
                   jax.ShapeDtypeStruct(ck.shape, F32), jax.ShapeDtypeStruct(cv.shape, F32),
                   jax.ShapeDtypeStruct(c0.shape, F32), jax.ShapeDtypeStruct(n0.shape, F32),
                   jax.ShapeDtypeStruct((nb, LANES), F32), jax.ShapeDtypeStruct(conv_buf.shape, F32)],
        compiler_params=_cparams("parallel"),
        name="mix_sample",
    )(rel_table, sinks, jnp.asarray(bkt), z, ck, cv, c0, n0, m0, conv_buf, conv_w, b_gates_pad, g_norm)


def _store_row_tiles(ref, val, rows):
    for s in range(ROW_TILES):
        ref[pl.ds(s, rows, stride=ROW_TILES), :] = val[:, s * LANES:(s + 1) * LANES]


def _load_row_tiles(ref, rows, start=0, stride=ROW_TILES):
    return jnp.concatenate([ref[pl.ds(start + s, rows, stride=stride), :] for s in range(ROW_TILES)], axis=1)


def _route(logits):
    lane = lax.broadcasted_iota(I32, logits.shape, 1)
    big = jnp.int32(1 << 20)
    gl = jnp.where(lane < N_GROUPS, logits, NEG_INF)
    gmax = jnp.max(gl, axis=-1, keepdims=True)
    gidx = jnp.min(jnp.where(gl == gmax, lane, big), axis=-1, keepdims=True)
    g_gate = 1.0 / jnp.sum(jnp.exp(gl - gmax), axis=-1, keepdims=True)
    lo = N_GROUPS + gidx * EXPERTS_PER_GROUP
    el = jnp.where((lane >= lo) & (lane < lo + EXPERTS_PER_GROUP), logits, NEG_INF)
    v1 = jnp.max(el, axis=-1, keepdims=True)
    i1 = jnp.min(jnp.where(el == v1, lane, big), axis=-1, keepdims=True)
    el2 = jnp.where(lane == i1, NEG_INF, el)
    v2 = jnp.max(el2, axis=-1, keepdims=True)
    i2 = jnp.min(jnp.where(el2 == v2, lane, big), axis=-1, keepdims=True)
    t = jnp.exp(v2 - v1)
    w1 = g_gate / (1.0 + t)
    w2 = g_gate * t / (1.0 + t)
    gate = jnp.where(lane == 0, w1, jnp.where(lane == 1, w2, 0.0))
    return i1 - N_GROUPS, i2 - N_GROUPS, gate


def _local_sort(e0, e1):
    tm = e0.shape[0]
    lane = lax.broadcasted_iota(I32, (tm, LANES), 1)
    oh0 = (lane == e0).astype(BF16)
    oh1 = (lane == e1).astype(BF16)
    r = lax.broadcasted_iota(I32, (tm, tm), 0)
    c = lax.broadcasted_iota(I32, (tm, tm), 1)
    before = (c < r).astype(BF16)
    cnt0 = jnp.sum(oh0.astype(F32), axis=0, keepdims=True)
    run_n = cnt0 + jnp.sum(oh1.astype(F32), axis=0, keepdims=True)
    er = lax.broadcasted_iota(I32, (LANES, LANES), 0)
    ec = lax.broadcasted_iota(I32, (LANES, LANES), 1)
    run_l = jnp.dot(run_n, (er < ec).astype(F32), precision=HI, preferred_element_type=F32)
    w0 = jnp.dot(before, oh0, preferred_element_type=F32) + run_l
    w1 = jnp.dot(before, oh1, preferred_element_type=F32) + run_l + cnt0
    p0 = jnp.sum(jnp.where(lane == e0, w0, 0.0), axis=-1, keepdims=True)
    p1 = jnp.sum(jnp.where(lane == e1, w1, 0.0), axis=-1, keepdims=True)
    return jnp.where(lane == 0, p0, jnp.where(lane == 1, p1, 0.0)), run_n


MOE_CHUNK = 512


def _moe_chunk(rows, precise):
    return min(rows, 128 if precise else MOE_CHUNK)


def _proj_router_body(*refs, n_in, has_bias, precise, tm):
    a_refs = refs[:n_in]
    w_refs = refs[n_in:2 * n_in]
    k = 2 * n_in
    bias_ref = refs[k] if has_bias else None
    k += 1 if has_bias else 0
    x_ref, g_ref, wr_ref, br_ref, x1_ref, h8_ref, lpos_ref, gate_ref, runn_ref = refs[k:]
    acc = x_ref[...]
    if has_bias:
        acc = acc + bias_ref[...]
    for a_ref, w_ref in zip(a_refs, w_refs):
        acc = acc + _mm(a_ref[...], w_ref[...], precise)
    x1_ref[...] = acc
    h = _rms(acc, g_ref[...])
    _store_row_tiles(h8_ref, h, tm)
    wr = wr_ref[...]
    if precise:
        logits = jnp.dot(h, wr, precision=HI, preferred_element_type=F32)
    else:
        h_hi = h.astype(BF16)
        h_lo = (h - h_hi.astype(F32)).astype(BF16)
        w_hi = wr.astype(BF16)
        w_lo = (wr - w_hi.astype(F32)).astype(BF16)
        logits = (jnp.dot(h_hi, w_hi, preferred_element_type=F32) + jnp.dot(h_lo, w_hi, preferred_element_type=F32)
                  + jnp.dot(h_hi, w_lo, preferred_element_type=F32))
    e0, e1, gate = _route(logits + br_ref[...])
    lpos, run_n = _local_sort(e0, e1)
    lpos_ref[...] = lpos.T[:SUBLANES].astype(I32)
    gate_ref[...] = gate.T[:SUBLANES]
    runn_ref[...] = jnp.broadcast_to(run_n, runn_ref.shape).astype(I32)


def proj_router(a_list, w_list, bias, x, g, wr, br, *, precise):
    rows, d = x.shape
    tm = _moe_chunk(rows, precise)
    n_in = len(a_list)
    row_spec = lambda width: pl.BlockSpec((tm, width), lambda i: (i, 0))
    const = lambda shape: pl.BlockSpec(shape, lambda i: (0,) * len(shape))
    in_specs = [row_spec(a.shape[1]) for a in a_list] + [const(w.shape) for w in w_list]
    args = list(a_list) + list(w_list)
    if bias is not None:
        in_specs.append(const((1, d)))
        args.append(bias)
    in_specs += [row_spec(d), const((1, d)), const((d, LANES)), const((1, LANES))]
    args += [x, g, wr, br]
    return pl.pallas_call(
        functools.partial(_proj_router_body, n_in=n_in, has_bias=bias is not None, precise=precise, tm=tm),
        grid=(rows // tm,),
        in_specs=in_specs,
        out_specs=[row_spec(d), pl.BlockSpec((tm * ROW_TILES, LANES), lambda i: (i, 0)),
                   pl.BlockSpec((SUBLANES, tm), lambda i: (i, 0)), pl.BlockSpec((SUBLANES, tm), lambda i: (i, 0)),
                   pl.BlockSpec((SUBLANES, LANES), lambda i: (i, 0))],
        out_shape=[jax.ShapeDtypeStruct((rows, d), F32), jax.ShapeDtypeStruct((rows * ROW_TILES, LANES), F32),
                   jax.ShapeDtypeStruct((rows // tm * SUBLANES, tm), I32),
                   jax.ShapeDtypeStruct((rows // tm * SUBLANES, tm), F32),
                   jax.ShapeDtypeStruct((rows // tm * SUBLANES, LANES), I32)],
        compiler_params=_cparams("parallel"),
        name="proj_router",
    )(*args)


def _tile_rows(r, n=1):
    return pl.ds(pl.multiple_of(r * ROW_TILES, ROW_TILES), n * ROW_TILES)


def _pow2_pieces(limit):
    p = 1
    while p * 2 <= limit:
        p *= 2
    out = []
    while p >= 1:
        out.append(p)
        p //= 2
    return out


COMMON_PIECE = 32


def _for_each_piece(n, pieces, fn):
    def emit(ps):
        for p in ps:
            @pl.when((n & p) != 0)
            def _(p=p):
                fn(n & ~(2 * p - 1), p)

    big = [p for p in pieces if p > COMMON_PIECE]
    if big:
        @pl.when(n > 2 * COMMON_PIECE - 1)
        def _():
            emit(big)
    emit([p for p in pieces if p <= COMMON_PIECE])


def _dispatch_body(rn_ref, rg_ref, rl_ref, ps_ref, pn_ref, tail_ref, lpos_ref, h8_ref, xs_hbm,
                   stage, zbuf, sem, zsem, *, chunk, nch, blk):
    c = pl.program_id(0)
    slot = c % 2
    run_pieces = _pow2_pieces(chunk)
    pad_pieces = _pow2_pieces(blk - 1)

    @pl.when(c == 0)
    def _():
        zbuf[...] = jnp.zeros(zbuf.shape, F32)

        def pad_dmas(e, op):
            def one(off, p):
                cp = pltpu.make_async_copy(zbuf.at[_tile_rows(0, p)], xs_hbm.at[_tile_rows(ps_ref[e] + off, p)], zsem)
                cp.start() if op == 0 else cp.wait()
            _for_each_piece(pn_ref[e], pad_pieces, one)

        def issue(e, carry):
            pad_dmas(e, 0)
            return carry

        def wait(e, carry):
            pad_dmas(e, 1)
            return carry
        lax.fori_loop(0, N_EXPERTS, issue, 0)
        lax.fori_loop(0, N_EXPERTS, wait, 0)

        half = blk // 2

        def tail_dmas(i, op):
            cp = pltpu.make_async_copy(zbuf, xs_hbm.at[_tile_rows(tail_ref[0] + i * half, half)], zsem)
            cp.start() if op == 0 else cp.wait()

        def tail_issue(i, carry):
            tail_dmas(i, 0)
            return carry

        def tail_wait(i, carry):
            tail_dmas(i, 1)
            return carry
        lax.fori_loop(0, tail_ref[1], tail_issue, 0)
        lax.fori_loop(0, tail_ref[1], tail_wait, 0)

    def copy_tok(t, carry):
        row = h8_ref[_tile_rows(t), :]
        stage[slot, _tile_rows(lpos_ref[0, t]), :] = row
        stage[slot, _tile_rows(lpos_ref[1, t]), :] = row
        return carry
    lax.fori_loop(0, chunk, copy_tok, 0, unroll=8)

    @pl.when(c > 0)
    def _():
        pltpu.make_async_copy(stage.at[1 - slot], stage.at[1 - slot], sem).wait()

    def send_runs(e, carry):
        k = c * N_EXPERTS + e

        def one(off, p):
            pltpu.make_async_copy(stage.at[slot, _tile_rows(rl_ref[k] + off, p)],
                                  xs_hbm.at[_tile_rows(rg_ref[k] + off, p)], sem).start()
        _for_each_piece(rn_ref[k], run_pieces, one)
        return carry
    lax.fori_loop(0, N_EXPERTS, send_runs, 0)

    @pl.when(c == nch - 1)
    def _():
        pltpu.make_async_copy(stage.at[slot], stage.at[slot], sem).wait()


def dispatch(h8, plan):
    chunk, nch, blk = plan['chunk'], plan['nch'], plan['blk']
    n_slots = plan['nblk'] * blk
    gs = pltpu.PrefetchScalarGridSpec(
        num_scalar_prefetch=6,
        grid=(nch,),
        in_specs=[pl.BlockSpec((SUBLANES, chunk), lambda c, *_: (c, 0), memory_space=pltpu.SMEM),
                  pl.BlockSpec((chunk * ROW_TILES, LANES), lambda c, *_: (c, 0))],
        out_specs=pl.BlockSpec(memory_space=pl.ANY),
        scratch_shapes=[pltpu.VMEM((2, TOP_K * chunk * ROW_TILES, LANES), F32),
                        pltpu.VMEM((blk // 2 * ROW_TILES, LANES), F32),
                        pltpu.SemaphoreType.DMA(()), pltpu.SemaphoreType.DMA(())],
    )
    return pl.pallas_call(
        functools.partial(_dispatch_body, chunk=chunk, nch=nch, blk=blk),
        grid_spec=gs,
        out_shape=jax.ShapeDtypeStruct((n_slots * ROW_TILES, LANES), F32),
        compiler_params=_cparams("arbitrary"),
        name="dispatch",
    )(plan['run_n'], plan['run_g'], plan['run_l'], plan['pad_start'], plan['pad_n'], plan['tail'], plan['lpos'], h8)


FF_CHUNK = 512


def _experts_body(be_ref, nv_ref, xs_ref, wg_ref, wu_ref, wd_ref, ys_ref, *wcast, precise, blk):
    j = pl.program_id(0)

    @pl.when(j < nv_ref[0])
    def _():
        x = _load_row_tiles(xs_ref, blk)
        if precise:
            wg, wu, wd = wg_ref[0], wu_ref[0], wd_ref[0]
        else:
            wgb, wub, wdb = wcast

            @pl.when((j == 0) | (be_ref[j] != be_ref[jnp.maximum(j - 1, 0)]))
            def _():
                wgb[...] = wg_ref[0].astype(BF16)
                wub[...] = wu_ref[0].astype(BF16)
                wdb[...] = wd_ref[0].astype(BF16)

            wg, wu, wd = wgb, wub, wdb
        xm = x if precise else x.astype(BF16)
        ff = wg.shape[-1]
        y = None
        for f in range(0, ff, FF_CHUNK):
            gt = _mm(xm, wg[:, f:f + FF_CHUNK], precise)
            up = _mm(xm, wu[:, f:f + FF_CHUNK], precise)
            part = _mm(gt * jax.nn.sigmoid(gt) * up, wd[f:f + FF_CHUNK, :], precise)
            y = part if y is None else y + part
        _store_row_tiles(ys_ref, y, blk)

    @pl.when(j >= nv_ref[0])
    def _():
        ys_ref[...] = jnp.zeros(ys_ref.shape, F32)


def experts(xs, plan, wg, wu, wd, layer, *, precise):
    nblk, rows = plan['nblk'], plan['blk'] * ROW_TILES
    d, ff = wg.shape[2], wg.shape[3]
    blk = lambda j, be, nv: (jnp.minimum(j, nv[0] - 1), 0)
    wspec = lambda shape: pl.BlockSpec((None, 1) + shape,
                                       lambda j, be, nv: (layer, be[jnp.minimum(j, nv[0] - 1)], 0, 0))
    gs = pltpu.PrefetchScalarGridSpec(
        num_scalar_prefetch=2,
        grid=(nblk,),
        in_specs=[pl.BlockSpec((rows, LANES), blk),
                  wspec((d, ff)), wspec((d, ff)), wspec((ff, d))],
        out_specs=pl.BlockSpec((rows, LANES), lambda j, be, nv: (j, 0)),
        scratch_shapes=[] if precise else [pltpu.VMEM((d, ff), BF16), pltpu.VMEM((d, ff), BF16),
                                           pltpu.VMEM((ff, d), BF16)],
    )
    return pl.pallas_call(
        functools.partial(_experts_body, precise=precise, blk=plan['blk']),
        grid_spec=gs,
        out_shape=jax.ShapeDtypeStruct(xs.shape, F32),
        compiler_params=_cparams("arbitrary"),
        name="experts",
    )(plan['block_e'], plan['n_used'], xs, wg, wu, wd)


def moe_plan(lpos8, gate8, runn8, blk):
    nch = runn8.shape[0] // SUBLANES
    chunk = lpos8.shape[1]
    n_assign = nch * chunk * TOP_K
    nblk = (n_assign + N_EXPERTS * (blk - 1)) // blk
    run_n = runn8.reshape(nch, SUBLANES, LANES)[:, 0, :N_EXPERTS]
    counts = jnp.sum(run_n, axis=0)
    padded = (counts + blk - 1) // blk * blk
    pends = jnp.cumsum(padded)
    pstarts = pends - padded
    run_g = pstarts[None, :] + jnp.cumsum(run_n, axis=0) - run_n
    run_l = jnp.cumsum(run_n, axis=1) - run_n
    blk_start = jnp.arange(nblk, dtype=I32) * blk
    block_e = jnp.minimum(jnp.sum((pends[None, :] <= blk_start[:, None]).astype(I32), axis=1), N_EXPERTS - 1)
    return dict(chunk=chunk, nch=nch, nblk=nblk, blk=blk,
                run_n=run_n.reshape(-1).astype(I32), run_g=run_g.reshape(-1).astype(I32),
                run_l=run_l.reshape(-1).astype(I32), pad_start=(pstarts + counts).astype(I32),
                pad_n=(padded - counts).astype(I32), lpos=lpos8, gate=gate8, block_e=block_e.astype(I32),
                n_used=(pends[-1:] // blk).astype(I32),
                tail=jnp.stack([pends[-1], 2 * (nblk - pends[-1] // blk)]).astype(I32))


def _combine(rn_ref, rg_ref, rl_ref, lpos_ref, gate_ref, x_ref, ys_hbm, ystage, comb, sem, *, chunk, nch):
    c = pl.program_id(0)
    slot = c % 2
    pieces = _pow2_pieces(chunk)

    def fetch(cc, sl):
        def per_e(e, carry):
            k = cc * N_EXPERTS + e

            def one(off, p):
                pltpu.make_async_copy(ys_hbm.at[_tile_rows(rg_ref[k] + off, p)],
                                      ystage.at[sl, _tile_rows(rl_ref[k] + off, p)], sem.at[sl]).start()
            _for_each_piece(rn_ref[k], pieces, one)
            return carry
        lax.fori_loop(0, N_EXPERTS, per_e, 0)

    @pl.when(c == 0)
    def _():
        fetch(0, 0)

    @pl.when(c + 1 < nch)
    def _():
        fetch(c + 1, 1 - slot)

    pltpu.make_async_copy(ystage.at[slot], ystage.at[slot], sem.at[slot]).wait()

    def per_tok(t, carry):
        y0 = ystage[slot, _tile_rows(lpos_ref[0, t]), :]
        y1 = ystage[slot, _tile_rows(lpos_ref[1, t]), :]
        comb[_tile_rows(t), :] = gate_ref[0, t] * y0 + gate_ref[1, t] * y1
        return carry
    lax.fori_loop(0, chunk, per_tok, 0, unroll=8)
    return x_ref[...] + _load_row_tiles(comb, chunk)


def _combine_glu_body(rn_ref, rg_ref, rl_ref, lpos_ref, gate_ref, x_ref, ys_hbm, g_ref, w_ref, b_ref,
                      x2_ref, u_ref, ystage, comb, sem, *, chunk, nch, precise):
    x2 = _combine(rn_ref, rg_ref, rl_ref, lpos_ref, gate_ref, x_ref, ys_hbm, ystage, comb, sem, chunk=chunk, nch=nch)
    x2_ref[...] = x2
    zz = _mm(_rms(x2, g_ref[...]), w_ref[...], precise) + b_ref[...]
    half = zz.shape[1] // 2
    u_ref[...] = zz[:, :half] * jax.nn.sigmoid(zz[:, half:])


def _combine_final_body(rn_ref, rg_ref, rl_ref, lpos_ref, gate_ref, x_ref, ys_hbm, g_ref, o_ref,
                        ystage, comb, sem, *, chunk, nch):
    x2 = _combine(rn_ref, rg_ref, rl_ref, lpos_ref, gate_ref, x_ref, ys_hbm, ystage, comb, sem, chunk=chunk, nch=nch)
    o_ref[...] = _rms(x2, g_ref[...])


def _combine_call(body, plan, x, ys, extra, extra_specs, out_specs, out_shape, name):
    chunk, nch = plan['chunk'], plan['nch']
    d = x.shape[1]
    smem_blk = pl.BlockSpec((SUBLANES, chunk), lambda c, *_: (c, 0), memory_space=pltpu.SMEM)
    gs = pltpu.PrefetchScalarGridSpec(
        num_scalar_prefetch=3,
        grid=(nch,),
        in_specs=[smem_blk, smem_blk, pl.BlockSpec((chunk, d), lambda c, *_: (c, 0)),
                  pl.BlockSpec(memory_space=pl.ANY)] + extra_specs,
        out_specs=out_specs,
        scratch_shapes=[pltpu.VMEM((2, TOP_K * chunk * ROW_TILES, LANES), F32),
                        pltpu.VMEM((chunk * ROW_TILES, LANES), F32), pltpu.SemaphoreType.DMA((2,))],
    )
    return pl.pallas_call(
        functools.partial(body, chunk=chunk, nch=nch),
        grid_spec=gs,
        out_shape=out_shape,
        compiler_params=_cparams("arbitrary"),
        name=name,
    )(plan['run_n'], plan['run_g'], plan['run_l'], plan['lpos'], plan['gate'], x, ys, *extra)


def combine_glu(x, ys, plan, g, w, b, *, precise):
    rows, d = x.shape
    chunk = plan['chunk']
    cols = w.shape[1]
    const = lambda shape: pl.BlockSpec(shape, lambda c, *_: (0,) * len(shape))
    row_spec = lambda width: pl.BlockSpec((chunk, width), lambda c, *_: (c, 0))
    return _combine_call(
        functools.partial(_combine_glu_body, precise=precise), plan, x, ys, [g, w, b],
        [const((1, d)), const((d, cols)), const((1, cols))], [row_spec(d), row_spec(cols // 2)],
        [jax.ShapeDtypeStruct((rows, d), F32), jax.ShapeDtypeStruct((rows, cols // 2), F32)], "combine_glu")


def combine_final(x, ys, plan, g):
    rows, d = x.shape
    chunk = plan['chunk']
    return _combine_call(
        _combine_final_body, plan, x, ys, [g], [pl.BlockSpec((1, d), lambda c, *_: (0, 0))],
        pl.BlockSpec((chunk, d), lambda c, *_: (c, 0)), jax.ShapeDtypeStruct((rows, d), F32), "combine_final")


CONV_TILE = 256
CONV_HIST = 32


def _ln_swish(y, g, b):
    yc = y - jnp.mean(y, axis=-1, keepdims=True)
    yn = yc * lax.rsqrt(jnp.mean(yc * yc, axis=-1, keepdims=True) + EPS) * g + b
    return yn * jax.nn.sigmoid(yn)


CONV_ROWS = 64
LN_ROWS = 16
LN_UNROLL = 8


def _dwconv_prompt_body(u_ref, w_ref, bdw_ref, g_ref, b_ref, o_ref, ext, y_sc, *, tt):
    t = pl.program_id(1)
    n_lt = ext.shape[0]

    @pl.when(t == 0)
    def _():
        ext[:, 0:CONV_HIST, :] = jnp.zeros((n_lt, CONV_HIST, LANES), F32)

    @pl.when(t > 0)
    def _():
        ext[:, 0:CONV_HIST, :] = ext[:, tt:tt + CONV_HIST, :]

    for j in range(n_lt):
        ext[j, CONV_HIST:CONV_HIST + tt, :] = u_ref[:, j * LANES:(j + 1) * LANES]
    off = CONV_HIST - (C_KERNEL - 1)
    for j in range(n_lt):
        wj = w_ref[:, j * LANES:(j + 1) * LANES]
        bj = bdw_ref[:, j * LANES:(j + 1) * LANES]
        for c in range(tt // CONV_ROWS):
            acc = ext[j, pl.ds(off + c * CONV_ROWS, CONV_ROWS), :] * wj[0:1] + bj
            for k in range(1, C_KERNEL):
                acc = acc + ext[j, pl.ds(off + k + c * CONV_ROWS, CONV_ROWS), :] * wj[k:k + 1]
            y_sc[c * CONV_ROWS:(c + 1) * CONV_ROWS, j * LANES:(j + 1) * LANES] = acc

    def ln_rows(r, carry):
        rows = pl.ds(pl.multiple_of(r * LN_ROWS, LN_ROWS), LN_ROWS)
        o_ref[rows, :] = _ln_swish(y_sc[rows, :], g_ref[...], b_ref[...]).astype(o_ref.dtype)
        return carry
    lax.fori_loop(0, tt // LN_ROWS, ln_rows, 0, unroll=LN_UNROLL)


def dwconv_prompt(u, w, b_dw, ln_g, ln_b, bsz, seq):
    tt = min(CONV_TILE, seq)
    nt = seq // tt
    d = u.shape[1]
    const = lambda shape: pl.BlockSpec(shape, lambda b, t: (0,) * len(shape))
    return pl.pallas_call(
        functools.partial(_dwconv_prompt_body, tt=tt),
        grid=(bsz, nt),
        in_specs=[pl.BlockSpec((tt, d), lambda b, t: (b * nt + t, 0)), const((C_KERNEL, d)), const((1, d)),
                  const((1, d)), const((1, d))],
        out_specs=pl.BlockSpec((tt, d), lambda b, t: (b * nt + t, 0)),
        out_shape=jax.ShapeDtypeStruct((bsz * seq, d), BF16),
        scratch_shapes=[pltpu.VMEM((d // LANES, CONV_HIST + tt, LANES), F32), pltpu.VMEM((tt, d), F32)],
        compiler_params=_cparams("arbitrary", "arbitrary"),
        name="dwconv_prompt",
    )(u, w, b_dw, ln_g, ln_b)


def _dwconv_sample_body(u_ref, buf_ref, w_ref, bdw_ref, g_ref, b_ref, o_ref, nbuf_ref, *, tb):
    w = w_ref[...]
    rows = []
    for i in range(tb):
        hist = buf_ref[i]
        ur = u_ref[i:i + 1, :]
        rows.append(jnp.sum(hist * w[:C_KERNEL - 1], axis=0, keepdims=True) + ur * w[C_KERNEL - 1:C_KERNEL])
        nbuf_ref[i] = jnp.concatenate([hist[1:], ur], axis=0)
    y = jnp.concatenate(rows, axis=0) + bdw_ref[...]
    o_ref[...] = _ln_swish(y, g_ref[...], b_ref[...])


def dwconv_sample(u, buf, w, b_dw, ln_g, ln_b):
    nb, d = u.shape
    tb = min(SAMPLE_TILE, nb)
    const = lambda shape: pl.BlockSpec(shape, lambda i: (0,) * len(shape))
    return pl.pallas_call(
        functools.partial(_dwconv_sample_body, tb=tb),
        grid=(nb // tb,),
        in_specs=[pl.BlockSpec((tb, d), lambda i: (i, 0)), pl.BlockSpec((tb, C_KERNEL - 1, d), lambda i: (i, 0, 0)),
                  const((C_KERNEL, d)), const((1, d)), const((1, d)), const((1, d))],
        out_specs=[pl.BlockSpec((tb, d), lambda i: (i, 0)), pl.BlockSpec((tb, C_KERNEL - 1, d), lambda i: (i, 0, 0))],
        out_shape=[jax.ShapeDtypeStruct((nb, d), F32), jax.ShapeDtypeStruct(buf.shape, F32)],
        compiler_params=_cparams("parallel"),
        name="dwconv_sample",
    )(u, buf, w, b_dw, ln_g, ln_b)


def _moe(h8, lpos128, gate128, runn8, wg, wu, wd, layer, *, precise):
    plan = moe_plan(lpos128, gate128, runn8, EXPERT_BLOCK_PRECISE if precise else EXPERT_BLOCK)
    xs = dispatch(h8, plan)
    return experts(xs, plan, wg, wu, wd, layer, precise=precise), plan


def _trunk(x, caches, p, *, prompt):
    bsz, seq, d = x.shape
    rows = bsz * seq
    precise = not prompt
    wdt = F32 if precise else BF16
    xf = x.reshape(rows, d)
    row = lambda v: v.reshape(1, -1).astype(F32)

    z = norm_proj(xf, row(p['norm_mix'][0]), p['w_in'].astype(wdt), precise=precise)
    if prompt:
        att = attn_prompt(z, p['rel_table'], p['sinks'], bsz, seq)
        out_b, c1, n1, m1 = mlstm_prompt(z, p['conv_w'], p['b_gates'], p['g_mnorm'], bsz, seq)
        z3 = z.reshape(bsz, seq, Z_COLS)
        new_k = z3[:, seq - WINDOW:, Z_KA:Z_KA + A_KV].reshape(bsz, WINDOW, A_KV_HEADS, A_HEAD_DIM)
        new_v = z3[:, seq - WINDOW:, Z_VA:Z_VA + A_KV].reshape(bsz, WINDOW, A_KV_HEADS, A_HEAD_DIM)
        new_conv = z3[:, seq - (B_CONV - 1):, Z_QK:Z_QK + 2 * B_QK]
        n1 = n1[:, :B_HEADS]
        m1 = m1[:, :B_HEADS, 0]
    else:
        ck, cv, c0, n0, m0, cbuf = caches[:6]
        n_buf = ck.shape[1]
        att, out_b, new_k, new_v, c1, n1, m1, new_conv = mix_sample(
            z, p['rel_table'], p['sinks'], ck.reshape(bsz, n_buf, A_KV), cv.reshape(bsz, n_buf, A_KV),
            c0, n0, m0, cbuf, p['conv_w'], p['b_gates'], p['g_mnorm'])
        new_k = new_k.reshape(bsz, n_buf, A_KV_HEADS, A_HEAD_DIM)
        new_v = new_v.reshape(bsz, n_buf, A_KV_HEADS, A_HEAD_DIM)
        m1 = m1[:, :B_HEADS]
    w_out = p['w_out'].astype(wdt)
    x1, h8, lpos, gate, runn = proj_router([att, out_b], [w_out[:A_Q], w_out[A_Q:]], None, xf,
                                    row(p['norm_ffn'][0]), p['w_router'][0], p['b_router'][0], precise=precise)
    ys, plan = _moe(h8, lpos, gate, runn, p['w_eg'], p['w_eu'], p['w_ed'], 0, precise=precise)

    x2, u = combine_glu(x1, ys, plan, row(p['norm_mix'][1]), p['w_pw1'].astype(wdt), row(p['b_pw1']),
                        precise=precise)
    if prompt:
        yc = dwconv_prompt(u, p['w_dw'], row(p['b_dw']), row(p['ln_g']), row(p['ln_b']), bsz, seq)
        new_cbuf = u.reshape(bsz, seq, d)[:, seq - (C_KERNEL - 1):]
    else:
        yc, new_cbuf = dwconv_sample(u, caches[6], p['w_dw'], row(p['b_dw']), row(p['ln_g']), row(p['ln_b']))
    x3, h8, lpos, gate, runn = proj_router([yc], [p['w_pw2'].astype(wdt)], row(p['b_pw2']), x2,
                                    row(p['norm_ffn'][1]), p['w_router'][1], p['b_router'][1], precise=precise)
    ys, plan = _moe(h8, lpos, gate, runn, p['w_eg'], p['w_eu'], p['w_ed'], 1, precise=precise)
    y = combine_final(x3, ys, plan, row(p['norm_final']))
    add_layer = lambda t: t[None]
    return (y.reshape(bsz, seq, d),) + tuple(add_layer(t) for t in (new_k, new_v, c1, n1, m1, new_conv, new_cbuf))


def kernel(x_prompt, x_sample, cache_win_k, cache_win_v, state_mlstm_c, state_mlstm_n, state_mlstm_m, state_mlstm_conv, state_conv, norm_mix, norm_ffn, norm_final, rel_bias_table, w_in_mix, b_mlstm_gates, w_mlstm_qk_conv, attn_sinks, g_mlstm_norm, w_out_mix, w_pw1, b_pw1, w_dw, b_dw, ln_conv_g, ln_conv_b, w_pw2, b_pw2, w_router_group, b_router_group, w_router_expert, b_router_expert, w_expert_gate, w_expert_up, w_expert_down):
    w_in = w_in_mix[0]
    s_q, s_k, s_v, s_qk, s_vb, s_g = A_Q, A_Q + A_KV, A_Q + 2 * A_KV, A_Q + 2 * A_KV + 2 * B_QK, \
        A_Q + 2 * A_KV + 2 * B_QK + B_V, A_Q + 2 * A_KV + 2 * B_QK + B_V + 2 * B_HEADS
    w_in_r = jnp.concatenate([w_in[:, :s_q], w_in[:, s_v:s_qk], w_in[:, s_qk:s_vb], w_in[:, s_g:],
                              w_in[:, s_q:s_k], w_in[:, s_k:s_v], w_in[:, s_vb:s_g],
                              jnp.zeros((D_MODEL, LANES - 2 * B_HEADS), F32)], axis=1)
    b_gates = jnp.concatenate([b_mlstm_gates[0], jnp.zeros((LANES - 2 * B_HEADS,), F32)]).reshape(1, LANES)
    depth = w_router_group.shape[0]
    w_re = jnp.transpose(w_router_expert, (0, 2, 1, 3)).reshape(depth, D_MODEL, N_EXPERTS)
    w_router = jnp.concatenate([w_router_group, w_re,
                                jnp.zeros((depth, D_MODEL, LANES - N_GROUPS - N_EXPERTS), F32)], axis=-1)
    b_router = jnp.concatenate([b_router_group, b_router_expert.reshape(depth, N_EXPERTS),
                                jnp.zeros((depth, LANES - N_GROUPS - N_EXPERTS), F32)], axis=-1)[:, None, :]
    p = dict(norm_mix=norm_mix, norm_ffn=norm_ffn, norm_final=norm_final, rel_table=rel_bias_table,
             sinks=attn_sinks[0], w_in=w_in_r, b_gates=b_gates, conv_w=w_mlstm_qk_conv[0],
             g_mnorm=g_mlstm_norm[0].reshape(1, B_V), w_out=w_out_mix[0], w_pw1=w_pw1[0], b_pw1=b_pw1[0],
             w_dw=w_dw[0], b_dw=b_dw[0], ln_g=ln_conv_g[0], ln_b=ln_conv_b[0], w_pw2=w_pw2[0], b_pw2=b_pw2[0],
             w_router=w_router, b_router=b_router, w_eg=w_expert_gate, w_eu=w_expert_up, w_ed=w_expert_down)
    caches = (cache_win_k[0], cache_win_v[0], state_mlstm_c[0], state_mlstm_n[0], state_mlstm_m[0],
              state_mlstm_conv[0], state_conv[0])
    out_p = _trunk(x_prompt, None, p, prompt=True)
    out_s = _trunk(x_sample, caches, p, prompt=False)
    return (out_p[0], out_s[0]) + out_p[1:] + out_s[1:]
```

```python
import functools
import math

import numpy as np
import jax
import jax.numpy as jnp
from jax import lax
from jax.experimental import pallas as pl
from jax.experimental.pallas import tpu as pltpu

F32 = jnp.float32
BF16 = jnp.bfloat16
I32 = jnp.int32
HI = lax.Precision.HIGHEST
NEG_INF = float("-inf")

LANES = 128
SUBLANES = 8
VMEM_LIMIT = 56 * 1024 * 1024

D_MODEL = 1024
A_HEADS = 8
A_KV_HEADS = 2
A_GROUP = A_HEADS // A_KV_HEADS
A_HEAD_DIM = 64
WINDOW = 128
REL_BUCKETS = 32
REL_MAX_DIST = 128
B_HEADS = 4
B_DK = 64
B_DV = 128
B_CONV = 4
C_KERNEL = 31
N_GROUPS = 4
EXPERTS_PER_GROUP = 8
N_EXPERTS = N_GROUPS * EXPERTS_PER_GROUP
TOP_K = 2
EXPERT_FF = D_MODEL // 2
EXPERT_BLOCK = 256
EXPERT_BLOCK_PRECISE = 128
EPS = 1e-6

A_Q = A_HEADS * A_HEAD_DIM
A_KV = A_KV_HEADS * A_HEAD_DIM
B_QK = B_HEADS * B_DK
B_V = B_HEADS * B_DV
ROW_TILES = D_MODEL // LANES

Z_QA, Z_QK, Z_VB, Z_OG, Z_KA, Z_VA, Z_GATES = 0, 512, 1024, 1536, 2048, 2176, 2304
Z_COLS = 2432
MLSTM_CHUNK = 128


def _cparams(*sem):
    return pltpu.CompilerParams(dimension_semantics=sem, vmem_limit_bytes=VMEM_LIMIT)


def _rms(x, g):
    return x * lax.rsqrt(jnp.mean(x * x, axis=-1, keepdims=True) + EPS) * g


def _mm(a, w, precise):
    if not precise:
        return jnp.dot(a.astype(BF16), w, preferred_element_type=F32)
    a = a.astype(F32)
    a_hi = a.astype(BF16)
    a_lo = (a - a_hi.astype(F32)).astype(BF16)
    w_hi = w.astype(BF16)
    w_lo = (w - w_hi.astype(F32)).astype(BF16)
    return (jnp.dot(a_hi, w_hi, preferred_element_type=F32) + jnp.dot(a_lo, w_hi, preferred_element_type=F32)
            + jnp.dot(a_hi, w_lo, preferred_element_type=F32))


def _t5_buckets(dist):
    exact = REL_BUCKETS // 2
    d = np.maximum(dist, 0)
    large = exact + (np.log(np.maximum(d, 1).astype(np.float32) / exact)
                     / math.log(REL_MAX_DIST / exact) * (REL_BUCKETS - exact)).astype(np.int32)
    return np.where(d < exact, d, np.minimum(large, REL_BUCKETS - 1)).astype(np.int32)


def _norm_proj_body(x_ref, g_ref, w_ref, o_ref, *, precise):
    h = _rms(x_ref[...], g_ref[...])
    o_ref[...] = _mm(h, w_ref[...], precise)


def norm_proj(x, g, w, *, precise):
    rows, d = x.shape
    cols = w.shape[1]
    tm = min(rows, 128 if precise else 512)
    return pl.pallas_call(
        functools.partial(_norm_proj_body, precise=precise),
        grid=(rows // tm,),
        in_specs=[pl.BlockSpec((tm, d), lambda i: (i, 0)),
                  pl.BlockSpec((1, d), lambda i: (0, 0)),
                  pl.BlockSpec((d, cols), lambda i: (0, 0))],
        out_specs=pl.BlockSpec((tm, cols), lambda i: (i, 0)),
        out_shape=jax.ShapeDtypeStruct((rows, cols), F32),
        compiler_params=_cparams("parallel"),
        name="norm_proj",
    )(x, g, w)


def _attn_prompt_body(tab_ref, sink_ref, bkt_ref, q_ref, kp_ref, kc_ref, vp_ref, vc_ref, o_ref, bias_ref):
    b = pl.program_id(0)
    n = pl.program_id(1)

    @pl.when((b == 0) & (n == 0))
    def _():
        bk = bkt_ref[...]
        first = lax.broadcasted_iota(I32, bk.shape, 1) >= WINDOW
        for h in range(A_HEADS):
            acc = jnp.full(bk.shape, NEG_INF, F32)
            for t in range(REL_BUCKETS):
                acc = jnp.where(bk == t, tab_ref[t, h], acc)
            bias_ref[h] = acc
            bias_ref[A_HEADS + h] = jnp.where(first, acc, NEG_INF)

    q = q_ref[...] * (A_HEAD_DIM ** -0.5)
    kb = jnp.concatenate([kp_ref[...], kc_ref[...]], axis=0).astype(BF16)
    vb = jnp.concatenate([vp_ref[...], vc_ref[...]], axis=0).astype(BF16)
    table = jnp.where(n == 0, A_HEADS, 0)
    outs = []
    for h in range(A_HEADS):
        kvh = h // A_GROUP
        qh = q[:, h * A_HEAD_DIM:(h + 1) * A_HEAD_DIM].astype(BF16)
        kh = kb[:, kvh * A_HEAD_DIM:(kvh + 1) * A_HEAD_DIM]
        vh = vb[:, kvh * A_HEAD_DIM:(kvh + 1) * A_HEAD_DIM]
        s = lax.dot_general(qh, kh, (((1,), (1,)), ((), ())), preferred_element_type=F32)
        s = s + bias_ref[table + h]
        sink = sink_ref[h]
        mx = jnp.maximum(jnp.max(s, axis=-1, keepdims=True), sink)
        p = jnp.exp(s - mx)
        den = jnp.sum(p, axis=-1, keepdims=True) + jnp.exp(sink - mx)
        outs.append(jnp.dot(p.astype(BF16), vh, preferred_element_type=F32) / den)
    o_ref[...] = jnp.concatenate(outs, axis=1).astype(o_ref.dtype)


def attn_prompt(z, rel_table, sinks, bsz, seq):
    nb = seq // WINDOW
    dist = WINDOW + np.arange(WINDOW)[:, None] - np.arange(2 * WINDOW)[None, :]
    bkt = np.where((dist >= 0) & (dist <= WINDOW), _t5_buckets(dist), -1).astype(np.int32)
    kcol, vcol = Z_KA // LANES, Z_VA // LANES
    smem = pl.BlockSpec(memory_space=pltpu.SMEM)

    def cur(c):
        return pl.BlockSpec((WINDOW, LANES), lambda b, n: (b * nb + n, c))

    def prev(c):
        return pl.BlockSpec((WINDOW, LANES), lambda b, n: (b * nb + jnp.maximum(n - 1, 0), c))

    return pl.pallas_call(
        _attn_prompt_body,
        grid=(bsz, nb),
        in_specs=[smem, smem,
                  pl.BlockSpec((WINDOW, 2 * WINDOW), lambda b, n: (0, 0)),
                  pl.BlockSpec((WINDOW, A_Q), lambda b, n: (b * nb + n, 0)),
                  prev(kcol), cur(kcol), prev(vcol), cur(vcol)],
        out_specs=pl.BlockSpec((WINDOW, A_Q), lambda b, n: (b * nb + n, 0)),
        out_shape=jax.ShapeDtypeStruct((bsz * seq, A_Q), BF16),
        scratch_shapes=[pltpu.VMEM((2 * A_HEADS, WINDOW, 2 * WINDOW), F32)],
        compiler_params=_cparams("arbitrary", "arbitrary"),
        name="attn_prompt",
    )(rel_table, sinks, jnp.asarray(bkt), z, z, z, z, z)


def _log_sigmoid(x):
    return -(jnp.maximum(-x, 0.0) + jnp.log1p(jnp.exp(-jnp.abs(x))))


def _mlstm_prompt_body(qk_ref, v_ref, g_ref, og_ref, cw_ref, bg_ref, gn_ref,
                       ob_ref, c_out, n_out, m_out,
                       c_sc, n_sc, m_sc, hist_sc, *, L, nc, nbat):
    ci = pl.program_id(1)

    @pl.when(ci == 0)
    def _():
        c_sc[...] = jnp.zeros(c_sc.shape, F32)
        n_sc[...] = jnp.zeros(n_sc.shape, F32)
        m_sc[...] = jnp.full(m_sc.shape, NEG_INF, F32)
        hist_sc[...] = jnp.zeros(hist_sc.shape, F32)

    row = lax.broadcasted_iota(I32, (L, L), 0)
    colm = lax.broadcasted_iota(I32, (L, L), 1)
    causal = colm <= row
    tril = causal.astype(F32)
    stores = []
    for bb in range(nbat):
        stores += _mlstm_chunk(qk_ref.at[bb], v_ref.at[bb], g_ref.at[bb], og_ref.at[bb], cw_ref, bg_ref, gn_ref,
                               ob_ref.at[bb], c_sc.at[bb], n_sc.at[bb], m_sc.at[bb], hist_sc.at[bb],
                               causal, tril, L)
    for store in stores:
        store()

    @pl.when(ci == nc - 1)
    def _():
        c_out[...] = c_sc[...]
        n_out[...] = n_sc[...]
        m_out[...] = m_sc[...]


def _mlstm_chunk(qk_ref, v_ref, g_ref, og_ref, cw_ref, bg_ref, gn_ref, ob_ref, c_sc, n_sc, m_sc, hist_sc,
                 causal, tril, L):
    cur = qk_ref[...]
    ext = jnp.concatenate([hist_sc[...], cur], axis=0)
    cw = cw_ref[...]
    off = SUBLANES - (B_CONV - 1)
    conv = ext[off:off + L] * cw[0:1]
    for j in range(1, B_CONV):
        conv = conv + ext[off + j:off + j + L] * cw[j:j + 1]
    qk = conv * jax.nn.sigmoid(conv)
    q_all = qk[:, :B_QK]
    k_all = qk[:, B_QK:] * (B_DK ** -0.5)
    v_all = v_ref[...]
    og = og_ref[...]
    gn = gn_ref[...]

    G = g_ref[...] + bg_ref[...]
    lf = _log_sigmoid(G)
    Bc = jnp.dot(tril, lf, precision=HI, preferred_element_type=F32)
    BT = Bc.T
    GT = G.T

    outs, stores = [], []
    for h in range(B_HEADS):
        qh = q_all[:, h * B_DK:(h + 1) * B_DK]
        kh = k_all[:, h * B_DK:(h + 1) * B_DK]
        vh = v_all[:, h * B_DV:(h + 1) * B_DV]
        b_col = Bc[:, B_HEADS + h:B_HEADS + h + 1]
        ig_col = G[:, h:h + 1]
        b_row = BT[B_HEADS + h:B_HEADS + h + 1, :]
        ig_row = GT[h:h + 1, :]
        c0 = c_sc[h]
        n0 = n_sc[h:h + 1, :]
        m0 = m_sc[h:h + 1, 0:1]
        a = b_col + m0
        d = jnp.where(causal, b_col - b_row + ig_row, NEG_INF)
        m = jnp.maximum(a, jnp.max(d, axis=-1, keepdims=True))
        dw = jnp.exp(d - m)
        aw = jnp.exp(a - m)
        qb = qh.astype(BF16)
        vb = vh.astype(BF16)
        s = lax.dot_general(qb, kh.astype(BF16), (((1,), (1,)), ((), ())), preferred_element_type=F32) * dw
        num = (jnp.dot(s.astype(BF16), vb, preferred_element_type=F32)
               + aw * jnp.dot(qb, c0.astype(BF16), preferred_element_type=F32))
        den = jnp.sum(s, axis=-1, keepdims=True) + aw * jnp.sum(qh * n0, axis=-1, keepdims=True)
        hh = num / jnp.maximum(jnp.abs(den), jnp.exp(-m))
        m_last = m[L - 1:L, :]
        wl = jnp.exp(b_col[L - 1:L, :] - b_col + ig_col - m_last)
        decay = aw[L - 1:L, :]
        kw = kh * wl
        c1 = decay * c0 + lax.dot_general(kw.astype(BF16), vb, (((0,), (0,)), ((), ())),
                                          preferred_element_type=F32)
        n1 = decay * n0 + jnp.sum(kw, axis=0, keepdims=True)
        stores.append(functools.partial(_store_state, c_sc, n_sc, m_sc, h, c1, n1, m_last))
        hn = hh * lax.rsqrt(jnp.mean(hh * hh, axis=-1, keepdims=True) + EPS) * gn[:, h * B_DV:(h + 1) * B_DV]
        outs.append(jax.nn.sigmoid(og[:, h * B_DV:(h + 1) * B_DV]) * hn)
    out = jnp.concatenate(outs, axis=1).astype(ob_ref.dtype)
    stores.append(functools.partial(_store_chunk, ob_ref, hist_sc, out, cur[L - SUBLANES:L]))
    return stores


def _store_state(c_sc, n_sc, m_sc, h, c1, n1, m_last):
    c_sc[h] = c1
    n_sc[h:h + 1, :] = n1
    m_sc[h:h + 1, :] = jnp.broadcast_to(m_last, (1, LANES))


def _store_chunk(ob_ref, hist_sc, out, tail):
    ob_ref[...] = out
    hist_sc[...] = tail


MLSTM_BATCH = 1


def mlstm_prompt(z, conv_w, b_gates_pad, g_norm, bsz, seq):
    L = MLSTM_CHUNK
    nc = seq // L
    nbat = MLSTM_BATCH if bsz % MLSTM_BATCH == 0 else 1
    z3 = z.reshape(bsz, seq, Z_COLS)

    def zspec(width, colblk):
        return pl.BlockSpec((nbat, L, width), lambda b, c: (b, c, colblk))

    const = lambda shape: pl.BlockSpec(shape, lambda b, c: (0,) * len(shape))
    state = lambda shape: pl.BlockSpec((nbat,) + shape, lambda b, c: (b,) + (0,) * len(shape))
    out_b, c1, n1, m1 = pl.pallas_call(
        functools.partial(_mlstm_prompt_body, L=L, nc=nc, nbat=nbat),
        grid=(bsz // nbat, nc),
        in_specs=[zspec(2 * B_QK, Z_QK // (2 * B_QK)), zspec(B_V, Z_VB // B_V), zspec(LANES, Z_GATES // LANES),
                  zspec(B_V, Z_OG // B_V), const((B_CONV, 2 * B_QK)), const((1, LANES)), const((1, B_V))],
        out_specs=[pl.BlockSpec((nbat, L, B_V), lambda b, c: (b, c, 0)),
                   state((B_HEADS, B_DK, B_DV)), state((SUBLANES, B_DK)), state((SUBLANES, LANES))],
        out_shape=[jax.ShapeDtypeStruct((bsz, seq, B_V), BF16),
                   jax.ShapeDtypeStruct((bsz, B_HEADS, B_DK, B_DV), F32),
                   jax.ShapeDtypeStruct((bsz, SUBLANES, B_DK), F32),
                   jax.ShapeDtypeStruct((bsz, SUBLANES, LANES), F32)],
        scratch_shapes=[pltpu.VMEM((nbat, B_HEADS, B_DK, B_DV), F32), pltpu.VMEM((nbat, SUBLANES, B_DK), F32),
                        pltpu.VMEM((nbat, SUBLANES, LANES), F32), pltpu.VMEM((nbat, SUBLANES, 2 * B_QK), F32)],
        compiler_params=_cparams("arbitrary", "arbitrary"),
        name="mlstm_prompt",
    )(z3, z3, z3, z3, conv_w, b_gates_pad, g_norm)
    return out_b.reshape(bsz * seq, B_V), c1, n1, m1


SAMPLE_TILE = 8


def _row_to_col(row, n):
    eye = lax.broadcasted_iota(I32, (n, n), 0) == lax.broadcasted_iota(I32, (n, n), 1)
    return jnp.sum(jnp.where(eye, jnp.broadcast_to(row, (n, n)), 0.0), axis=1, keepdims=True)


def _mix_sample_body(tab_ref, sink_ref, bkt_ref, z_ref, ck_ref, cv_ref, c0_ref, n0_ref, m0_ref, cb_ref,
                     cw_ref, bg_ref, gn_ref,
                     att_ref, ob_ref, nk_ref, nv_ref, c1_ref, n1_ref, m1_ref, ncb_ref, *, tb):
    bk = bkt_ref[...]
    cw = cw_ref[...]
    gn = gn_ref[...]
    zero_half = jnp.zeros((1, A_HEAD_DIM), F32)
    bias_rows = []
    for h in range(A_HEADS):
        bias = jnp.zeros(bk.shape, F32)
        for t in range(REL_BUCKETS):
            bias = jnp.where(bk == t, tab_ref[t, h], bias)
        bias_rows.append(bias)
    bias_c = jnp.concatenate(bias_rows, axis=0)
    head_id = lax.broadcasted_iota(I32, (A_HEADS, 1), 0)
    bias_n = jnp.zeros((A_HEADS, 1), F32)
    sinks = jnp.zeros((A_HEADS, 1), F32)
    for h in range(A_HEADS):
        bias_n = jnp.where(head_id == h, tab_ref[0, h], bias_n)
        sinks = jnp.where(head_id == h, sink_ref[h], sinks)
    att_rows, ob_rows, m_rows = [], [], []
    stores = []
    for i in range(tb):
        zr = z_ref[i:i + 1, :]
        q_att = zr[:, Z_QA:Z_QA + A_Q] * (A_HEAD_DIM ** -0.5)
        ka = zr[:, Z_KA:Z_KA + A_KV]
        va = zr[:, Z_VA:Z_VA + A_KV]
        kc = ck_ref[i]
        vc = cv_ref[i]
        stores.append((nk_ref, i, jnp.concatenate([kc[1:], ka], axis=0)))
        stores.append((nv_ref, i, jnp.concatenate([vc[1:], va], axis=0)))
        q_rows = []
        for h in range(A_HEADS):
            qh = q_att[:, h * A_HEAD_DIM:(h + 1) * A_HEAD_DIM]
            q_rows.append(jnp.concatenate([qh, zero_half] if h // A_GROUP == 0 else [zero_half, qh], axis=1))
        qm = jnp.concatenate(q_rows, axis=0)
        lc = lax.dot_general(qm, kc, (((1,), (1,)), ((), ())), precision=HI,
                             preferred_element_type=F32) + bias_c
        ln = jnp.sum(qm * ka, axis=-1, keepdims=True) + bias_n
        mx = jnp.maximum(jnp.maximum(jnp.max(lc, axis=-1, keepdims=True), ln), sinks)
        pc = jnp.exp(lc - mx)
        pn = jnp.exp(ln - mx)
        den = jnp.sum(pc, axis=-1, keepdims=True) + pn + jnp.exp(sinks - mx)
        o = (jnp.dot(pc, vc, precision=HI, preferred_element_type=F32) + pn * va) / den
        att_rows.append(jnp.concatenate(
            [o[h:h + 1, (h // A_GROUP) * A_HEAD_DIM:(h // A_GROUP + 1) * A_HEAD_DIM] for h in range(A_HEADS)],
            axis=1))

        qk_pre = zr[:, Z_QK:Z_QK + 2 * B_QK]
        hist = cb_ref[i]
        conv = qk_pre * cw[B_CONV - 1:B_CONV]
        for j in range(B_CONV - 1):
            conv = conv + hist[j:j + 1] * cw[j:j + 1]
        stores.append((ncb_ref, i, jnp.concatenate([hist[1:], qk_pre], axis=0)))
        qk = conv * jax.nn.sigmoid(conv)
        G = zr[:, Z_GATES:Z_GATES + LANES] + bg_ref[...]
        lfr = _log_sigmoid(G)
        og = zr[:, Z_OG:Z_OG + B_V]
        v_pre = zr[:, Z_VB:Z_VB + B_V]
        obs, ms = [], []
        for h in range(B_HEADS):
            qh = qk[:, h * B_DK:(h + 1) * B_DK]
            kh = qk[:, B_QK + h * B_DK:B_QK + (h + 1) * B_DK] * (B_DK ** -0.5)
            vh = v_pre[:, h * B_DV:(h + 1) * B_DV]
            ig = G[:, h:h + 1]
            lf = lfr[:, B_HEADS + h:B_HEADS + h + 1]
            c0 = c0_ref[i, h]
            n0 = n0_ref[i, h:h + 1, :]
            m0 = m0_ref[i:i + 1, h:h + 1]
            a = lf + m0
            m = jnp.maximum(a, ig)
            dw = jnp.exp(ig - m)
            aw = jnp.exp(a - m)
            s = jnp.sum(qh * kh, axis=-1, keepdims=True) * dw
            q_col = _row_to_col(qh, B_DK)
            k_col = _row_to_col(kh, B_DK)
            num = s * vh + aw * jnp.sum(q_col * c0, axis=0, keepdims=True)
            den = s + aw * jnp.sum(qh * n0, axis=-1, keepdims=True)
            hh = num / jnp.maximum(jnp.abs(den), jnp.exp(-m))
            stores.append((c1_ref, (i, h), aw * c0 + dw * (k_col * vh)))
            stores.append((n1_ref, (i, slice(h, h + 1), slice(None)), aw * n0 + dw * kh))
            ms.append(m)
            hn = hh * lax.rsqrt(jnp.mean(hh * hh, axis=-1, keepdims=True) + EPS) * gn[:, h * B_DV:(h + 1) * B_DV]
            obs.append(jax.nn.sigmoid(og[:, h * B_DV:(h + 1) * B_DV]) * hn)
        ob_rows.append(jnp.concatenate(obs, axis=1))
        lane = lax.broadcasted_iota(I32, (1, LANES), 1)
        mrow = jnp.zeros((1, LANES), F32)
        for h in range(B_HEADS):
            mrow = jnp.where(lane == h, ms[h], mrow)
        m_rows.append(mrow)
    for ref, idx, val in stores:
        ref[idx] = val
    att_ref[...] = jnp.concatenate(att_rows, axis=0)
    ob_ref[...] = jnp.concatenate(ob_rows, axis=0)
    m1_ref[...] = jnp.concatenate(m_rows, axis=0)


def mix_sample(z, rel_table, sinks, ck, cv, c0, n0, m0, conv_buf, conv_w, b_gates_pad, g_norm):
    nb = z.shape[0]
    tb = min(SAMPLE_TILE, nb)
    n_buf = ck.shape[1]
    bkt = _t5_buckets(n_buf - np.arange(n_buf))[None, :]
    smem = pl.BlockSpec(memory_space=pltpu.SMEM)
    const = lambda shape: pl.BlockSpec(shape, lambda i: (0,) * len(shape))
    lead = lambda shape: pl.BlockSpec((tb,) + shape, lambda i: (i,) + (0,) * len(shape))
    return pl.pallas_call(
        functools.partial(_mix_sample_body, tb=tb),
        grid=(nb // tb,),
        in_specs=[smem, smem, const((1, n_buf)), lead((Z_COLS,)), lead((n_buf, A_KV)), lead((n_buf, A_KV)),
                  lead((B_HEADS, B_DK, B_DV)), lead((B_HEADS, B_DK)), lead((B_HEADS,)),
                  lead((B_CONV - 1, 2 * B_QK)), const((B_CONV, 2 * B_QK)), const((1, LANES)), const((1, B_V))],
        out_specs=[lead((A_Q,)), lead((B_V,)), lead((n_buf, A_KV)), lead((n_buf, A_KV)),
                   lead((B_HEADS, B_DK, B_DV)), lead((B_HEADS, B_DK)), lead((LANES,)),
                   lead((B_CONV - 1, 2 * B_QK))],
        out_shape=[jax.ShapeDtypeStruct((nb, A_Q), F32), jax.ShapeDtypeStruct((nb, B_V), F32),
                   jax.ShapeDtypeStruct(ck.shape, F32), jax.ShapeDtypeStruct(cv.shape, F32),
                   jax.ShapeDtypeStruct(c0.shape, F32), jax.ShapeDtypeStruct(n0.shape, F32),
                   jax.ShapeDtypeStruct((nb, LANES), F32), jax.ShapeDtypeStruct(conv_buf.shape, F32)],
        compiler_params=_cparams("parallel"),
        name="mix_sample",
    )(rel_table, sinks, jnp.asarray(bkt), z, ck, cv, c0, n0, m0, conv_buf, conv_w, b_gates_pad, g_norm)


def _store_row_tiles(ref, val, rows):
    for s in range(ROW_TILES):
        ref[pl.ds(s, rows, stride=ROW_TILES), :] = val[:, s * LANES:(s + 1) * LANES]


def _load_row_tiles(ref, rows, start=0, stride=ROW_TILES):
    return jnp.concatenate([ref[pl.ds(start + s, rows, stride=stride), :] for s in range(ROW_TILES)], axis=1)


def _route(logits):
    lane = lax.broadcasted_iota(I32, logits.shape, 1)
    big = jnp.int32(1 << 20)
    gl = jnp.where(lane < N_GROUPS, logits, NEG_INF)
    gmax = jnp.max(gl, axis=-1, keepdims=True)
    gidx = jnp.min(jnp.where(gl == gmax, lane, big), axis=-1, keepdims=True)
    g_gate = 1.0 / jnp.sum(jnp.exp(gl - gmax), axis=-1, keepdims=True)
    lo = N_GROUPS + gidx * EXPERTS_PER_GROUP
    el = jnp.where((lane >= lo) & (lane < lo + EXPERTS_PER_GROUP), logits, NEG_INF)
    v1 = jnp.max(el, axis=-1, keepdims=True)
    i1 = jnp.min(jnp.where(el == v1, lane, big), axis=-1, keepdims=True)
    el2 = jnp.where(lane == i1, NEG_INF, el)
    v2 = jnp.max(el2, axis=-1, keepdims=True)
    i2 = jnp.min(jnp.where(el2 == v2, lane, big), axis=-1, keepdims=True)
    t = jnp.exp(v2 - v1)
    w1 = g_gate / (1.0 + t)
    w2 = g_gate * t / (1.0 + t)
    gate = jnp.where(lane == 0, w1, jnp.where(lane == 1, w2, 0.0))
    return i1 - N_GROUPS, i2 - N_GROUPS, gate


def _local_sort(e0, e1):
    tm = e0.shape[0]
    lane = lax.broadcasted_iota(I32, (tm, LANES), 1)
    oh0 = (lane == e0).astype(BF16)
    oh1 = (lane == e1).astype(BF16)
    r = lax.broadcasted_iota(I32, (tm, tm), 0)
    c = lax.broadcasted_iota(I32, (tm, tm), 1)
    before = (c < r).astype(BF16)
    cnt0 = jnp.sum(oh0.astype(F32), axis=0, keepdims=True)
    run_n = cnt0 + jnp.sum(oh1.astype(F32), axis=0, keepdims=True)
    er = lax.broadcasted_iota(I32, (LANES, LANES), 0)
    ec = lax.broadcasted_iota(I32, (LANES, LANES), 1)
    run_l = jnp.dot(run_n, (er < ec).astype(F32), precision=HI, preferred_element_type=F32)
    w0 = jnp.dot(before, oh0, preferred_element_type=F32) + run_l
    w1 = jnp.dot(before, oh1, preferred_element_type=F32) + run_l + cnt0
    p0 = jnp.sum(jnp.where(lane == e0, w0, 0.0), axis=-1, keepdims=True)
    p1 = jnp.sum(jnp.where(lane == e1, w1, 0.0), axis=-1, keepdims=True)
    return jnp.where(lane == 0, p0, jnp.where(lane == 1, p1, 0.0)), run_n


MOE_CHUNK = 512


def _moe_chunk(rows, precise):
    return min(rows, 128 if precise else MOE_CHUNK)


def _proj_router_body(*refs, n_in, has_bias, precise, tm):
    a_refs = refs[:n_in]
    w_refs = refs[n_in:2 * n_in]
    k = 2 * n_in
    bias_ref = refs[k] if has_bias else None
    k += 1 if has_bias else 0
    x_ref, g_ref, wr_ref, br_ref, x1_ref, h8_ref, lpos_ref, gate_ref, runn_ref = refs[k:]
    acc = x_ref[...]
    if has_bias:
        acc = acc + bias_ref[...]
    for a_ref, w_ref in zip(a_refs, w_refs):
        acc = acc + _mm(a_ref[...], w_ref[...], precise)
    x1_ref[...] = acc
    h = _rms(acc, g_ref[...])
    _store_row_tiles(h8_ref, h, tm)
    wr = wr_ref[...]
    if precise:
        logits = jnp.dot(h, wr, precision=HI, preferred_element_type=F32)
    else:
        h_hi = h.astype(BF16)
        h_lo = (h - h_hi.astype(F32)).astype(BF16)
        w_hi = wr.astype(BF16)
        w_lo = (wr - w_hi.astype(F32)).astype(BF16)
        logits = (jnp.dot(h_hi, w_hi, preferred_element_type=F32) + jnp.dot(h_lo, w_hi, preferred_element_type=F32)
                  + jnp.dot(h_hi, w_lo, preferred_element_type=F32))
    e0, e1, gate = _route(logits + br_ref[...])
    lpos, run_n = _local_sort(e0, e1)
    lpos_ref[...] = (lpos.T[:SUBLANES] * ROW_TILES).astype(I32)
    gate_ref[...] = gate.T[:SUBLANES]
    runn_ref[...] = jnp.broadcast_to(run_n, runn_ref.shape).astype(I32)


def proj_router(a_list, w_list, bias, x, g, wr, br, *, precise):
    rows, d = x.shape
    tm = _moe_chunk(rows, precise)
    n_in = len(a_list)
    row_spec = lambda width: pl.BlockSpec((tm, width), lambda i: (i, 0))
    const = lambda shape: pl.BlockSpec(shape, lambda i: (0,) * len(shape))
    in_specs = [row_spec(a.shape[1]) for a in a_list] + [const(w.shape) for w in w_list]
    args = list(a_list) + list(w_list)
    if bias is not None:
        in_specs.append(const((1, d)))
        args.append(bias)
    in_specs += [row_spec(d), const((1, d)), const((d, LANES)), const((1, LANES))]
    args += [x, g, wr, br]
    return pl.pallas_call(
        functools.partial(_proj_router_body, n_in=n_in, has_bias=bias is not None, precise=precise, tm=tm),
        grid=(rows // tm,),
        in_specs=in_specs,
        out_specs=[row_spec(d), pl.BlockSpec((tm * ROW_TILES, LANES), lambda i: (i, 0)),
                   pl.BlockSpec((SUBLANES, tm), lambda i: (i, 0)), pl.BlockSpec((SUBLANES, tm), lambda i: (i, 0)),
                   pl.BlockSpec((SUBLANES, LANES), lambda i: (i, 0))],
        out_shape=[jax.ShapeDtypeStruct((rows, d), F32), jax.ShapeDtypeStruct((rows * ROW_TILES, LANES), F32),
                   jax.ShapeDtypeStruct((rows // tm * SUBLANES, tm), I32),
                   jax.ShapeDtypeStruct((rows // tm * SUBLANES, tm), F32),
                   jax.ShapeDtypeStruct((rows // tm * SUBLANES, LANES), I32)],
        compiler_params=_cparams("parallel"),
        name="proj_router",
    )(*args)


def _rows_at(offset):
    return pl.ds(pl.multiple_of(offset, ROW_TILES), ROW_TILES)


def _tile_rows(r, n=1):
    return pl.ds(pl.multiple_of(r * ROW_TILES, ROW_TILES), n * ROW_TILES)


def _pow2_pieces(limit):
    p = 1
    while p * 2 <= limit:
        p *= 2
    out = []
    while p >= 1:
        out.append(p)
        p //= 2
    return out


COMMON_PIECE = 32


def _for_each_piece(n, pieces, fn):
    def emit(ps):
        for p in ps:
            @pl.when((n & p) != 0)
            def _(p=p):
                fn(n & ~(2 * p - 1), p)

    big = [p for p in pieces if p > COMMON_PIECE]
    if big:
        @pl.when(n > 2 * COMMON_PIECE - 1)
        def _():
            emit(big)
    emit([p for p in pieces if p <= COMMON_PIECE])


def _dispatch_body(rn_ref, rg_ref, rl_ref, ps_ref, pn_ref, tail_ref, lpos_ref, h8_ref, xs_hbm,
                   stage, zbuf, sem, zsem, *, chunk, nch, blk):
    c = pl.program_id(0)
    slot = c % 2
    run_pieces = _pow2_pieces(chunk)
    pad_pieces = _pow2_pieces(blk - 1)

    @pl.when(c == 0)
    def _():
        zbuf[...] = jnp.zeros(zbuf.shape, F32)

        def pad_dmas(e, op):
            def one(off, p):
                cp = pltpu.make_async_copy(zbuf.at[_tile_rows(0, p)], xs_hbm.at[_tile_rows(ps_ref[e] + off, p)], zsem)
                cp.start() if op == 0 else cp.wait()
            _for_each_piece(pn_ref[e], pad_pieces, one)

        def issue(e, carry):
            pad_dmas(e, 0)
            return carry

        def wait(e, carry):
            pad_dmas(e, 1)
            return carry
        lax.fori_loop(0, N_EXPERTS, issue, 0)
        lax.fori_loop(0, N_EXPERTS, wait, 0)

        half = blk // 2

        def tail_dmas(i, op):
            cp = pltpu.make_async_copy(zbuf, xs_hbm.at[_tile_rows(tail_ref[0] + i * half, half)], zsem)
            cp.start() if op == 0 else cp.wait()

        def tail_issue(i, carry):
            tail_dmas(i, 0)
            return carry

        def tail_wait(i, carry):
            tail_dmas(i, 1)
            return carry
        lax.fori_loop(0, tail_ref[1], tail_issue, 0)
        lax.fori_loop(0, tail_ref[1], tail_wait, 0)

    def copy_tok(t, carry):
        row = h8_ref[_tile_rows(t), :]
        stage[slot, _rows_at(lpos_ref[0, 0, t]), :] = row
        stage[slot, _rows_at(lpos_ref[0, 0, chunk + t]), :] = row
        return carry
    lax.fori_loop(0, chunk, copy_tok, 0, unroll=8)

    @pl.when(c > 0)
    def _():
        pltpu.make_async_copy(stage.at[1 - slot], stage.at[1 - slot], sem).wait()

    def send_runs(e, carry):
        k = c * N_EXPERTS + e

        def one(off, p):
            pltpu.make_async_copy(stage.at[slot, _tile_rows(rl_ref[k] + off, p)],
                                  xs_hbm.at[_tile_rows(rg_ref[k] + off, p)], sem).start()
        _for_each_piece(rn_ref[k], run_pieces, one)
        return carry
    lax.fori_loop(0, N_EXPERTS, send_runs, 0)

    @pl.when(c == nch - 1)
    def _():
        pltpu.make_async_copy(stage.at[slot], stage.at[slot], sem).wait()


def dispatch(h8, plan):
    chunk, nch, blk = plan['chunk'], plan['nch'], plan['blk']
    n_slots = plan['nblk'] * blk
    gs = pltpu.PrefetchScalarGridSpec(
        num_scalar_prefetch=6,
        grid=(nch,),
        in_specs=[pl.BlockSpec((1, 1, TOP_K * chunk), lambda c, *_: (c, 0, 0), memory_space=pltpu.SMEM),
                  pl.BlockSpec((chunk * ROW_TILES, LANES), lambda c, *_: (c, 0))],
        out_specs=pl.BlockSpec(memory_space=pl.ANY),
        scratch_shapes=[pltpu.VMEM((2, TOP_K * chunk * ROW_TILES, LANES), F32),
                        pltpu.VMEM((blk // 2 * ROW_TILES, LANES), F32),
                        pltpu.SemaphoreType.DMA(()), pltpu.SemaphoreType.DMA(())],
    )
    return pl.pallas_call(
        functools.partial(_dispatch_body, chunk=chunk, nch=nch, blk=blk),
        grid_spec=gs,
        out_shape=jax.ShapeDtypeStruct((n_slots * ROW_TILES, LANES), F32),
        compiler_params=_cparams("arbitrary"),
        name="dispatch",
    )(plan['run_n'], plan['run_g'], plan['run_l'], plan['pad_start'], plan['pad_n'], plan['tail'], plan['lpos'], h8)


FF_CHUNK = 512


def _experts_body(be_ref, nv_ref, xs_ref, wg_ref, wu_ref, wd_ref, ys_ref, *wcast, precise, blk):
    j = pl.program_id(0)

    @pl.when(j < nv_ref[0])
    def _():
        x = _load_row_tiles(xs_ref, blk)
        if precise:
            wg, wu, wd = wg_ref[0], wu_ref[0], wd_ref[0]
        else:
            wgb, wub, wdb = wcast

            @pl.when((j == 0) | (be_ref[j] != be_ref[jnp.maximum(j - 1, 0)]))
            def _():
                wgb[...] = wg_ref[0].astype(BF16)
                wub[...] = wu_ref[0].astype(BF16)
                wdb[...] = wd_ref[0].astype(BF16)

            wg, wu, wd = wgb, wub, wdb
        xm = x if precise else x.astype(BF16)
        ff = wg.shape[-1]
        y = None
        for f in range(0, ff, FF_CHUNK):
            gt = _mm(xm, wg[:, f:f + FF_CHUNK], precise)
            up = _mm(xm, wu[:, f:f + FF_CHUNK], precise)
            part = _mm(gt * jax.nn.sigmoid(gt) * up, wd[f:f + FF_CHUNK, :], precise)
            y = part if y is None else y + part
        _store_row_tiles(ys_ref, y, blk)

    @pl.when(j >= nv_ref[0])
    def _():
        ys_ref[...] = jnp.zeros(ys_ref.shape, F32)


def experts(xs, plan, wg, wu, wd, layer, *, precise):
    nblk, rows = plan['nblk'], plan['blk'] * ROW_TILES
    d, ff = wg.shape[2], wg.shape[3]
    blk = lambda j, be, nv: (jnp.minimum(j, nv[0] - 1), 0)
    wspec = lambda shape: pl.BlockSpec((None, 1) + shape,
                                       lambda j, be, nv: (layer, be[jnp.minimum(j, nv[0] - 1)], 0, 0))
    gs = pltpu.PrefetchScalarGridSpec(
        num_scalar_prefetch=2,
        grid=(nblk,),
        in_specs=[pl.BlockSpec((rows, LANES), blk),
                  wspec((d, ff)), wspec((d, ff)), wspec((ff, d))],
        out_specs=pl.BlockSpec((rows, LANES), lambda j, be, nv: (j, 0)),
        scratch_shapes=[] if precise else [pltpu.VMEM((d, ff), BF16), pltpu.VMEM((d, ff), BF16),
                                           pltpu.VMEM((ff, d), BF16)],
    )
    return pl.pallas_call(
        functools.partial(_experts_body, precise=precise, blk=plan['blk']),
        grid_spec=gs,
        out_shape=jax.ShapeDtypeStruct(xs.shape, F32),
        compiler_params=_cparams("arbitrary"),
        name="experts",
    )(plan['block_e'], plan['n_used'], xs, wg, wu, wd)


def moe_plan(lpos8, gate8, runn8, blk):
    nch = runn8.shape[0] // SUBLANES
    chunk = lpos8.shape[1]
    n_assign = nch * chunk * TOP_K
    nblk = (n_assign + N_EXPERTS * (blk - 1)) // blk
    run_n = runn8.reshape(nch, SUBLANES, LANES)[:, 0, :N_EXPERTS]
    per_chunk = lambda a: a.reshape(nch, 1, SUBLANES * chunk)[:, :, :TOP_K * chunk]
    counts = jnp.sum(run_n, axis=0)
    padded = (counts + blk - 1) // blk * blk
    pends = jnp.cumsum(padded)
    pstarts = pends - padded
    run_g = pstarts[None, :] + jnp.cumsum(run_n, axis=0) - run_n
    run_l = jnp.cumsum(run_n, axis=1) - run_n
    blk_start = jnp.arange(nblk, dtype=I32) * blk
    block_e = jnp.minimum(jnp.sum((pends[None, :] <= blk_start[:, None]).astype(I32), axis=1), N_EXPERTS - 1)
    return dict(chunk=chunk, nch=nch, nblk=nblk, blk=blk,
                run_n=run_n.reshape(-1).astype(I32), run_g=run_g.reshape(-1).astype(I32),
                run_l=run_l.reshape(-1).astype(I32), pad_start=(pstarts + counts).astype(I32),
                pad_n=(padded - counts).astype(I32), lpos=per_chunk(lpos8), gate=per_chunk(gate8), block_e=block_e.astype(I32),
                n_used=(pends[-1:] // blk).astype(I32),
                tail=jnp.stack([pends[-1], 2 * (nblk - pends[-1] // blk)]).astype(I32))


def _combine(rn_ref, rg_ref, rl_ref, lpos_ref, gate_ref, x_ref, ys_hbm, ystage, comb, sem, *, chunk, nch):
    c = pl.program_id(0)
    slot = c % 2
    pieces = _pow2_pieces(chunk)

    def fetch(cc, sl):
        def per_e(e, carry):
            k = cc * N_EXPERTS + e

            def one(off, p):
                pltpu.make_async_copy(ys_hbm.at[_tile_rows(rg_ref[k] + off, p)],
                                      ystage.at[sl, _tile_rows(rl_ref[k] + off, p)], sem.at[sl]).start()
            _for_each_piece(rn_ref[k], pieces, one)
            return carry
        lax.fori_loop(0, N_EXPERTS, per_e, 0)

    @pl.when(c == 0)
    def _():
        fetch(0, 0)

    @pl.when(c + 1 < nch)
    def _():
        fetch(c + 1, 1 - slot)

    pltpu.make_async_copy(ystage.at[slot], ystage.at[slot], sem.at[slot]).wait()

    def per_tok(t, carry):
        y0 = ystage[slot, _rows_at(lpos_ref[0, 0, t]), :]
        y1 = ystage[slot, _rows_at(lpos_ref[0, 0, chunk + t]), :]
        comb[_tile_rows(t), :] = gate_ref[0, 0, t] * y0 + gate_ref[0, 0, chunk + t] * y1
        return carry
    lax.fori_loop(0, chunk, per_tok, 0, unroll=8)
    return x_ref[...] + _load_row_tiles(comb, chunk)


def _combine_glu_body(rn_ref, rg_ref, rl_ref, lpos_ref, gate_ref, x_ref, ys_hbm, g_ref, w_ref, b_ref,
                      x2_ref, u_ref, ystage, comb, sem, *, chunk, nch, precise):
    x2 = _combine(rn_ref, rg_ref, rl_ref, lpos_ref, gate_ref, x_ref, ys_hbm, ystage, comb, sem, chunk=chunk, nch=nch)
    x2_ref[...] = x2
    zz = _mm(_rms(x2, g_ref[...]), w_ref[...], precise) + b_ref[...]
    half = zz.shape[1] // 2
    u_ref[...] = zz[:, :half] * jax.nn.sigmoid(zz[:, half:])


def _combine_final_body(rn_ref, rg_ref, rl_ref, lpos_ref, gate_ref, x_ref, ys_hbm, g_ref, o_ref,
                        ystage, comb, sem, *, chunk, nch):
    x2 = _combine(rn_ref, rg_ref, rl_ref, lpos_ref, gate_ref, x_ref, ys_hbm, ystage, comb, sem, chunk=chunk, nch=nch)
    o_ref[...] = _rms(x2, g_ref[...])


def _combine_call(body, plan, x, ys, extra, extra_specs, out_specs, out_shape, name):
    chunk, nch = plan['chunk'], plan['nch']
    d = x.shape[1]
    smem_blk = pl.BlockSpec((1, 1, TOP_K * chunk), lambda c, *_: (c, 0, 0), memory_space=pltpu.SMEM)
    gs = pltpu.PrefetchScalarGridSpec(
        num_scalar_prefetch=3,
        grid=(nch,),
        in_specs=[smem_blk, smem_blk, pl.BlockSpec((chunk, d), lambda c, *_: (c, 0)),
                  pl.BlockSpec(memory_space=pl.ANY)] + extra_specs,
        out_specs=out_specs,
        scratch_shapes=[pltpu.VMEM((2, TOP_K * chunk * ROW_TILES, LANES), F32),
                        pltpu.VMEM((chunk * ROW_TILES, LANES), F32), pltpu.SemaphoreType.DMA((2,))],
    )
    return pl.pallas_call(
        functools.partial(body, chunk=chunk, nch=nch),
        grid_spec=gs,
        out_shape=out_shape,
        compiler_params=_cparams("arbitrary"),
        name=name,
    )(plan['run_n'], plan['run_g'], plan['run_l'], plan['lpos'], plan['gate'], x, ys, *extra)


def combine_glu(x, ys, plan, g, w, b, *, precise):
    rows, d = x.shape
    chunk = plan['chunk']
    cols = w.shape[1]
    const = lambda shape: pl.BlockSpec(shape, lambda c, *_: (0,) * len(shape))
    row_spec = lambda width: pl.BlockSpec((chunk, width), lambda c, *_: (c, 0))
    return _combine_call(
        functools.partial(_combine_glu_body, precise=precise), plan, x, ys, [g, w, b],
        [const((1, d)), const((d, cols)), const((1, cols))], [row_spec(d), row_spec(cols // 2)],
        [jax.ShapeDtypeStruct((rows, d), F32), jax.ShapeDtypeStruct((rows, cols // 2), F32)], "combine_glu")


def combine_final(x, ys, plan, g):
    rows, d = x.shape
    chunk = plan['chunk']
    return _combine_call(
        _combine_final_body, plan, x, ys, [g], [pl.BlockSpec((1, d), lambda c, *_: (0, 0))],
        pl.BlockSpec((chunk, d), lambda c, *_: (c, 0)), jax.ShapeDtypeStruct((rows, d), F32), "combine_final")


CONV_TILE = 256
CONV_HIST = 32


def _ln_swish(y, g, b):
    yc = y - jnp.mean(y, axis=-1, keepdims=True)
    yn = yc * lax.rsqrt(jnp.mean(yc * yc, axis=-1, keepdims=True) + EPS) * g + b
    return yn * jax.nn.sigmoid(yn)


CONV_ROWS = 64
LN_ROWS = 16
LN_UNROLL = 8


def _dwconv_prompt_body(u_ref, w_ref, bdw_ref, g_ref, b_ref, o_ref, ext, y_sc, *, tt):
    t = pl.program_id(1)
    n_lt = ext.shape[0]

    @pl.when(t == 0)
    def _():
        ext[:, 0:CONV_HIST, :] = jnp.zeros((n_lt, CONV_HIST, LANES), F32)

    @pl.when(t > 0)
    def _():
        ext[:, 0:CONV_HIST, :] = ext[:, tt:tt + CONV_HIST, :]

    for j in range(n_lt):
        ext[j, CONV_HIST:CONV_HIST + tt, :] = u_ref[:, j * LANES:(j + 1) * LANES]
    off = CONV_HIST - (C_KERNEL - 1)
    for j in range(n_lt):
        wj = w_ref[:, j * LANES:(j + 1) * LANES]
        bj = bdw_ref[:, j * LANES:(j + 1) * LANES]
        for c in range(tt // CONV_ROWS):
            acc = ext[j, pl.ds(off + c * CONV_ROWS, CONV_ROWS), :] * wj[0:1] + bj
            for k in range(1, C_KERNEL):
                acc = acc + ext[j, pl.ds(off + k + c * CONV_ROWS, CONV_ROWS), :] * wj[k:k + 1]
            y_sc[c * CONV_ROWS:(c + 1) * CONV_ROWS, j * LANES:(j + 1) * LANES] = acc

    def ln_rows(r, carry):
        rows = pl.ds(pl.multiple_of(r * LN_ROWS, LN_ROWS), LN_ROWS)
        o_ref[rows, :] = _ln_swish(y_sc[rows, :], g_ref[...], b_ref[...]).astype(o_ref.dtype)
        return carry
    lax.fori_loop(0, tt // LN_ROWS, ln_rows, 0, unroll=LN_UNROLL)


def dwconv_prompt(u, w, b_dw, ln_g, ln_b, bsz, seq):
    tt = min(CONV_TILE, seq)
    nt = seq // tt
    d = u.shape[1]
    const = lambda shape: pl.BlockSpec(shape, lambda b, t: (0,) * len(shape))
    return pl.pallas_call(
        functools.partial(_dwconv_prompt_body, tt=tt),
        grid=(bsz, nt),
        in_specs=[pl.BlockSpec((tt, d), lambda b, t: (b * nt + t, 0)), const((C_KERNEL, d)), const((1, d)),
                  const((1, d)), const((1, d))],
        out_specs=pl.BlockSpec((tt, d), lambda b, t: (b * nt + t, 0)),
        out_shape=jax.ShapeDtypeStruct((bsz * seq, d), BF16),
        scratch_shapes=[pltpu.VMEM((d // LANES, CONV_HIST + tt, LANES), F32), pltpu.VMEM((tt, d), F32)],
        compiler_params=_cparams("arbitrary", "arbitrary"),
        name="dwconv_prompt",
    )(u, w, b_dw, ln_g, ln_b)


def _dwconv_sample_body(u_ref, buf_ref, w_ref, bdw_ref, g_ref, b_ref, o_ref, nbuf_ref, *, tb):
    w = w_ref[...]
    rows = []
    for i in range(tb):
        hist = buf_ref[i]
        ur = u_ref[i:i + 1, :]
        rows.append(jnp.sum(hist * w[:C_KERNEL - 1], axis=0, keepdims=True) + ur * w[C_KERNEL - 1:C_KERNEL])
        nbuf_ref[i] = jnp.concatenate([hist[1:], ur], axis=0)
    y = jnp.concatenate(rows, axis=0) + bdw_ref[...]
    o_ref[...] = _ln_swish(y, g_ref[...], b_ref[...])


def dwconv_sample(u, buf, w, b_dw, ln_g, ln_b):
    nb, d = u.shape
    tb = min(SAMPLE_TILE, nb)
    const = lambda shape: pl.BlockSpec(shape, lambda i: (0,) * len(shape))
    return pl.pallas_call(
        functools.partial(_dwconv_sample_body, tb=tb),
        grid=(nb // tb,),
        in_specs=[pl.BlockSpec((tb, d), lambda i: (i, 0)), pl.BlockSpec((tb, C_KERNEL - 1, d), lambda i: (i, 0, 0)),
                  const((C_KERNEL, d)), const((1, d)), const((1, d)), const((1, d))],
        out_specs=[pl.BlockSpec((tb, d), lambda i: (i, 0)), pl.BlockSpec((tb, C_KERNEL - 1, d), lambda i: (i, 0, 0))],
        out_shape=[jax.ShapeDtypeStruct((nb, d), F32), jax.ShapeDtypeStruct(buf.shape, F32)],
        compiler_params=_cparams("parallel"),
        name="dwconv_sample",
    )(u, buf, w, b_dw, ln_g, ln_b)


def _moe(h8, lpos128, gate128, runn8, wg, wu, wd, layer, *, precise):
    plan = moe_plan(lpos128, gate128, runn8, EXPERT_BLOCK_PRECISE if precise else EXPERT_BLOCK)
    xs = dispatch(h8, plan)
    return experts(xs, plan, wg, wu, wd, layer, precise=precise), plan


def _trunk(x, caches, p, *, prompt):
    bsz, seq, d = x.shape
    rows = bsz * seq
    precise = not prompt
    wdt = F32 if precise else BF16
    xf = x.reshape(rows, d)
    row = lambda v: v.reshape(1, -1).astype(F32)

    z = norm_proj(xf, row(p['norm_mix'][0]), p['w_in'].astype(wdt), precise=precise)
    if prompt:
        att = attn_prompt(z, p['rel_table'], p['sinks'], bsz, seq)
        out_b, c1, n1, m1 = mlstm_prompt(z, p['conv_w'], p['b_gates'], p['g_mnorm'], bsz, seq)
        z3 = z.reshape(bsz, seq, Z_COLS)
        new_k = z3[:, seq - WINDOW:, Z_KA:Z_KA + A_KV].reshape(bsz, WINDOW, A_KV_HEADS, A_HEAD_DIM)
        new_v = z3[:, seq - WINDOW:, Z_VA:Z_VA + A_KV].reshape(bsz, WINDOW, A_KV_HEADS, A_HEAD_DIM)
        new_conv = z3[:, seq - (B_CONV - 1):, Z_QK:Z_QK + 2 * B_QK]
        n1 = n1[:, :B_HEADS]
        m1 = m1[:, :B_HEADS, 0]
    else:
        ck, cv, c0, n0, m0, cbuf = caches[:6]
        n_buf = ck.shape[1]
        att, out_b, new_k, new_v, c1, n1, m1, new_conv = mix_sample(
            z, p['rel_table'], p['sinks'], ck.reshape(bsz, n_buf, A_KV), cv.reshape(bsz, n_buf, A_KV),
            c0, n0, m0, cbuf, p['conv_w'], p['b_gates'], p['g_mnorm'])
        new_k = new_k.reshape(bsz, n_buf, A_KV_HEADS, A_HEAD_DIM)
        new_v = new_v.reshape(bsz, n_buf, A_KV_HEADS, A_HEAD_DIM)
        m1 = m1[:, :B_HEADS]
    w_out = p['w_out'].astype(wdt)
    x1, h8, lpos, gate, runn = proj_router([att, out_b], [w_out[:A_Q], w_out[A_Q:]], None, xf,
                                    row(p['norm_ffn'][0]), p['w_router'][0], p['b_router'][0], precise=precise)
    ys, plan = _moe(h8, lpos, gate, runn, p['w_eg'], p['w_eu'], p['w_ed'], 0, precise=precise)

    x2, u = combine_glu(x1, ys, plan, row(p['norm_mix'][1]), p['w_pw1'].astype(wdt), row(p['b_pw1']),
                        precise=precise)
    if prompt:
        yc = dwconv_prompt(u, p['w_dw'], row(p['b_dw']), row(p['ln_g']), row(p['ln_b']), bsz, seq)
        new_cbuf = u.reshape(bsz, seq, d)[:, seq - (C_KERNEL - 1):]
    else:
        yc, new_cbuf = dwconv_sample(u, caches[6], p['w_dw'], row(p['b_dw']), row(p['ln_g']), row(p['ln_b']))
    x3, h8, lpos, gate, runn = proj_router([yc], [p['w_pw2'].astype(wdt)], row(p['b_pw2']), x2,
                                    row(p['norm_ffn'][1]), p['w_router'][1], p['b_router'][1], precise=precise)
    ys, plan = _moe(h8, lpos, gate, runn, p['w_eg'], p['w_eu'], p['w_ed'], 1, precise=precise)
    y = combine_final(x3, ys, plan, row(p['norm_final']))
    add_layer = lambda t: t[None]
    return (y.reshape(bsz, seq, d),) + tuple(add_layer(t) for t in (new_k, new_v, c1, n1, m1, new_conv, new_cbuf))


def kernel(x_prompt, x_sample, cache_win_k, cache_win_v, state_mlstm_c, state_mlstm_n, state_mlstm_m, state_mlstm_conv, state_conv, norm_mix, norm_ffn, norm_final, rel_bias_table, w_in_mix, b_mlstm_gates, w_mlstm_qk_conv, attn_sinks, g_mlstm_norm, w_out_mix, w_pw1, b_pw1, w_dw, b_dw, ln_conv_g, ln_conv_b, w_pw2, b_pw2, w_router_group, b_router_group, w_router_expert, b_router_expert, w_expert_gate, w_expert_up, w_expert_down):
    w_in = w_in_mix[0]
    s_q, s_k, s_v, s_qk, s_vb, s_g = A_Q, A_Q + A_KV, A_Q + 2 * A_KV, A_Q + 2 * A_KV + 2 * B_QK, \
        A_Q + 2 * A_KV + 2 * B_QK + B_V, A_Q + 2 * A_KV + 2 * B_QK + B_V + 2 * B_HEADS
    w_in_r = jnp.concatenate([w_in[:, :s_q], w_in[:, s_v:s_qk], w_in[:, s_qk:s_vb], w_in[:, s_g:],
                              w_in[:, s_q:s_k], w_in[:, s_k:s_v], w_in[:, s_vb:s_g],
                              jnp.zeros((D_MODEL, LANES - 2 * B_HEADS), F32)], axis=1)
    b_gates = jnp.concatenate([b_mlstm_gates[0], jnp.zeros((LANES - 2 * B_HEADS,), F32)]).reshape(1, LANES)
    depth = w_router_group.shape[0]
    w_re = jnp.transpose(w_router_expert, (0, 2, 1, 3)).reshape(depth, D_MODEL, N_EXPERTS)
    w_router = jnp.concatenate([w_router_group, w_re,
                                jnp.zeros((depth, D_MODEL, LANES - N_GROUPS - N_EXPERTS), F32)], axis=-1)
    b_router = jnp.concatenate([b_router_group, b_router_expert.reshape(depth, N_EXPERTS),
                                jnp.zeros((depth, LANES - N_GROUPS - N_EXPERTS), F32)], axis=-1)[:, None, :]
    p = dict(norm_mix=norm_mix, norm_ffn=norm_ffn, norm_final=norm_final, rel_table=rel_bias_table,
             sinks=attn_sinks[0], w_in=w_in_r, b_gates=b_gates, conv_w=w_mlstm_qk_conv[0],
             g_mnorm=g_mlstm_norm[0].reshape(1, B_V), w_out=w_out_mix[0], w_pw1=w_pw1[0], b_pw1=b_pw1[0],
             w_dw=w_dw[0], b_dw=b_dw[0], ln_g=ln_conv_g[0], ln_b=ln_conv_b[0], w_pw2=w_pw2[0], b_pw2=b_pw2[0],
             w_router=w_router, b_router=b_router, w_eg=w_expert_gate, w_eu=w_expert_up, w_ed=w_expert_down)
    caches = (cache_win_k[0], cache_win_v[0], state_mlstm_c[0], state_mlstm_n[0], state_mlstm_m[0],
              state_mlstm_conv[0], state_conv[0])
    out_p = _trunk(x_prompt, None, p, prompt=True)
    out_s = _trunk(x_sample, caches, p, prompt=False)
    return (out_p[0], out_s[0]) + out_p[1:] + out_s[1:]
```

```python
import functools
import math

import numpy as np
import jax
import jax.numpy as jnp
from jax import lax
from jax.experimental import pallas as pl
from jax.experimental.pallas import tpu as pltpu

F32 = jnp.float32
BF16 = jnp.bfloat16
I32 = jnp.int32
HI = lax.Precision.HIGHEST
NEG_INF = float("-inf")

LANES = 128
SUBLANES = 8
VMEM_LIMIT = 56 * 1024 * 1024

D_MODEL = 1024
A_HEADS = 8
A_KV_HEADS = 2
A_GROUP = A_HEADS // A_KV_HEADS
A_HEAD_DIM = 64
WINDOW = 128
REL_BUCKETS = 32
REL_MAX_DIST = 128
B_HEADS = 4
B_DK = 64
B_DV = 128
B_CONV = 4
C_KERNEL = 31
N_GROUPS = 4
EXPERTS_PER_GROUP = 8
N_EXPERTS = N_GROUPS * EXPERTS_PER_GROUP
TOP_K = 2
EXPERT_FF = D_MODEL // 2
EXPERT_BLOCK = 256
EXPERT_BLOCK_PRECISE = 128
EPS = 1e-6

A_Q = A_HEADS * A_HEAD_DIM
A_KV = A_KV_HEADS * A_HEAD_DIM
B_QK = B_HEADS * B_DK
B_V = B_HEADS * B_DV
ROW_TILES = D_MODEL // LANES

Z_QA, Z_QK, Z_VB, Z_OG, Z_KA, Z_VA, Z_GATES = 0, 512, 1024, 1536, 2048, 2176, 2304
Z_COLS = 2432
MLSTM_CHUNK = 128


def _cparams(*sem):
    return pltpu.CompilerParams(dimension_semantics=sem, vmem_limit_bytes=VMEM_LIMIT)


def _rms(x, g):
    return x * lax.rsqrt(jnp.mean(x * x, axis=-1, keepdims=True) + EPS) * g


def _mm(a, w, precise):
    if not precise:
        return jnp.dot(a.astype(BF16), w, preferred_element_type=F32)
    a = a.astype(F32)
    a_hi = a.astype(BF16)
    a_lo = (a - a_hi.astype(F32)).astype(BF16)
    w_hi = w.astype(BF16)
    w_lo = (w - w_hi.astype(F32)).astype(BF16)
    return (jnp.dot(a_hi, w_hi, preferred_element_type=F32) + jnp.dot(a_lo, w_hi, preferred_element_type=F32)
            + jnp.dot(a_hi, w_lo, preferred_element_type=F32))


def _t5_buckets(dist):
    exact = REL_BUCKETS // 2
    d = np.maximum(dist, 0)
    large = exact + (np.log(np.maximum(d, 1).astype(np.float32) / exact)
                     / math.log(REL_MAX_DIST / exact) * (REL_BUCKETS - exact)).astype(np.int32)
    return np.where(d < exact, d, np.minimum(large, REL_BUCKETS - 1)).astype(np.int32)


def _norm_proj_body(x_ref, g_ref, w_ref, o_ref, *, precise):
    h = _rms(x_ref[...], g_ref[...])
    o_ref[...] = _mm(h, w_ref[...], precise)


def norm_proj(x, g, w, *, precise):
    rows, d = x.shape
    cols = w.shape[1]
    tm = min(rows, 128 if precise else 512)
    return pl.pallas_call(
        functools.partial(_norm_proj_body, precise=precise),
        grid=(rows // tm,),
        in_specs=[pl.BlockSpec((tm, d), lambda i: (i, 0)),
                  pl.BlockSpec((1, d), lambda i: (0, 0)),
                  pl.BlockSpec((d, cols), lambda i: (0, 0))],
        out_specs=pl.BlockSpec((tm, cols), lambda i: (i, 0)),
        out_shape=jax.ShapeDtypeStruct((rows, cols), F32),
        compiler_params=_cparams("parallel"),
        name="norm_proj",
    )(x, g, w)


def _attn_prompt_body(tab_ref, sink_ref, bkt_ref, q_ref, kp_ref, kc_ref, vp_ref, vc_ref, o_ref, bias_ref):
    b = pl.program_id(0)
    n = pl.program_id(1)

    @pl.when((b == 0) & (n == 0))
    def _():
        bk = bkt_ref[...]
        first = lax.broadcasted_iota(I32, bk.shape, 1) >= WINDOW
        for h in range(A_HEADS):
            acc = jnp.full(bk.shape, NEG_INF, F32)
            for t in range(REL_BUCKETS):
                acc = jnp.where(bk == t, tab_ref[t, h], acc)
            bias_ref[h] = acc
            bias_ref[A_HEADS + h] = jnp.where(first, acc, NEG_INF)

    q = q_ref[...] * (A_HEAD_DIM ** -0.5)
    kb = jnp.concatenate([kp_ref[...], kc_ref[...]], axis=0).astype(BF16)
    vb = jnp.concatenate([vp_ref[...], vc_ref[...]], axis=0).astype(BF16)
    table = jnp.where(n == 0, A_HEADS, 0)
    outs = []
    for h in range(A_HEADS):
        kvh = h // A_GROUP
        qh = q[:, h * A_HEAD_DIM:(h + 1) * A_HEAD_DIM].astype(BF16)
        kh = kb[:, kvh * A_HEAD_DIM:(kvh + 1) * A_HEAD_DIM]
        vh = vb[:, kvh * A_HEAD_DIM:(kvh + 1) * A_HEAD_DIM]
        s = lax.dot_general(qh, kh, (((1,), (1,)), ((), ())), preferred_element_type=F32)
        s = s + bias_ref[table + h]
        sink = sink_ref[h]
        mx = jnp.maximum(jnp.max(s, axis=-1, keepdims=True), sink)
        p = jnp.exp(s - mx)
        den = jnp.sum(p, axis=-1, keepdims=True) + jnp.exp(sink - mx)
        outs.append(jnp.dot(p.astype(BF16), vh, preferred_element_type=F32) / den)
    o_ref[...] = jnp.concatenate(outs, axis=1).astype(o_ref.dtype)


def attn_prompt(z, rel_table, sinks, bsz, seq):
    nb = seq // WINDOW
    dist = WINDOW + np.arange(WINDOW)[:, None] - np.arange(2 * WINDOW)[None, :]
    bkt = np.where((dist >= 0) & (dist <= WINDOW), _t5_buckets(dist), -1).astype(np.int32)
    kcol, vcol = Z_KA // LANES, Z_VA // LANES
    smem = pl.BlockSpec(memory_space=pltpu.SMEM)

    def cur(c):
        return pl.BlockSpec((WINDOW, LANES), lambda b, n: (b * nb + n, c))

    def prev(c):
        return pl.BlockSpec((WINDOW, LANES), lambda b, n: (b * nb + jnp.maximum(n - 1, 0), c))

    return pl.pallas_call(
        _attn_prompt_body,
        grid=(bsz, nb),
        in_specs=[smem, smem,
                  pl.BlockSpec((WINDOW, 2 * WINDOW), lambda b, n: (0, 0)),
                  pl.BlockSpec((WINDOW, A_Q), lambda b, n: (b * nb + n, 0)),
                  prev(kcol), cur(kcol), prev(vcol), cur(vcol)],
        out_specs=pl.BlockSpec((WINDOW, A_Q), lambda b, n: (b * nb + n, 0)),
        out_shape=jax.ShapeDtypeStruct((bsz * seq, A_Q), BF16),
        scratch_shapes=[pltpu.VMEM((2 * A_HEADS, WINDOW, 2 * WINDOW), F32)],
        compiler_params=_cparams("arbitrary", "arbitrary"),
        name="attn_prompt",
    )(rel_table, sinks, jnp.asarray(bkt), z, z, z, z, z)


def _log_sigmoid(x):
    return -(jnp.maximum(-x, 0.0) + jnp.log1p(jnp.exp(-jnp.abs(x))))


def _mlstm_prompt_body(qk_ref, v_ref, g_ref, og_ref, cw_ref, bg_ref, gn_ref,
                       ob_ref, c_out, n_out, m_out,
                       c_sc, n_sc, m_sc, hist_sc, *, L, nc, nbat):
    ci = pl.program_id(1)

    @pl.when(ci == 0)
    def _():
        c_sc[...] = jnp.zeros(c_sc.shape, F32)
        n_sc[...] = jnp.zeros(n_sc.shape, F32)
        m_sc[...] = jnp.full(m_sc.shape, NEG_INF, F32)
        hist_sc[...] = jnp.zeros(hist_sc.shape, F32)

    row = lax.broadcasted_iota(I32, (L, L), 0)
    colm = lax.broadcasted_iota(I32, (L, L), 1)
    causal = colm <= row
    tril = causal.astype(F32)
    stores = []
    for bb in range(nbat):
        stores += _mlstm_chunk(qk_ref.at[bb], v_ref.at[bb], g_ref.at[bb], og_ref.at[bb], cw_ref, bg_ref, gn_ref,
                               ob_ref.at[bb], c_sc.at[bb], n_sc.at[bb], m_sc.at[bb], hist_sc.at[bb],
                               causal, tril, L)
    for store in stores:
        store()

    @pl.when(ci == nc - 1)
    def _():
        c_out[...] = c_sc[...]
        n_out[...] = n_sc[...]
        m_out[...] = m_sc[...]


def _mlstm_chunk(qk_ref, v_ref, g_ref, og_ref, cw_ref, bg_ref, gn_ref, ob_ref, c_sc, n_sc, m_sc, hist_sc,
                 causal, tril, L):
    cur = qk_ref[...]
    ext = jnp.concatenate([hist_sc[...], cur], axis=0)
    cw = cw_ref[...]
    off = SUBLANES - (B_CONV - 1)
    conv = ext[off:off + L] * cw[0:1]
    for j in range(1, B_CONV):
        conv = conv + ext[off + j:off + j + L] * cw[j:j + 1]
    qk = conv * jax.nn.sigmoid(conv)
    q_all = qk[:, :B_QK]
    k_all = qk[:, B_QK:] * (B_DK ** -0.5)
    v_all = v_ref[...]
    og = og_ref[...]
    gn = gn_ref[...]

    G = g_ref[...] + bg_ref[...]
    lf = _log_sigmoid(G)
    Bc = jnp.dot(tril, lf, precision=HI, preferred_element_type=F32)
    BT = Bc.T
    GT = G.T

    outs, stores = [], []
    for h in range(B_HEADS):
        qh = q_all[:, h * B_DK:(h + 1) * B_DK]
        kh = k_all[:, h * B_DK:(h + 1) * B_DK]
        vh = v_all[:, h * B_DV:(h + 1) * B_DV]
        b_col = Bc[:, B_HEADS + h:B_HEADS + h + 1]
        ig_col = G[:, h:h + 1]
        b_row = BT[B_HEADS + h:B_HEADS + h + 1, :]
        ig_row = GT[h:h + 1, :]
        c0 = c_sc[h]
        n0 = n_sc[h:h + 1, :]
        m0 = m_sc[h:h + 1, 0:1]
        a = b_col + m0
        d = jnp.where(causal, b_col - b_row + ig_row, NEG_INF)
        m = jnp.maximum(a, jnp.max(d, axis=-1, keepdims=True))
        dw = jnp.exp(d - m)
        aw = jnp.exp(a - m)
        qb = qh.astype(BF16)
        vb = vh.astype(BF16)
        s = lax.dot_general(qb, kh.astype(BF16), (((1,), (1,)), ((), ())), preferred_element_type=F32) * dw
        num = (jnp.dot(s.astype(BF16), vb, preferred_element_type=F32)
               + aw * jnp.dot(qb, c0.astype(BF16), preferred_element_type=F32))
        den = jnp.sum(s, axis=-1, keepdims=True) + aw * jnp.sum(qh * n0, axis=-1, keepdims=True)
        hh = num / jnp.maximum(jnp.abs(den), jnp.exp(-m))
        m_last = m[L - 1:L, :]
        wl = jnp.exp(b_col[L - 1:L, :] - b_col + ig_col - m_last)
        decay = aw[L - 1:L, :]
        kw = kh * wl
        c1 = decay * c0 + lax.dot_general(kw.astype(BF16), vb, (((0,), (0,)), ((), ())),
                                          preferred_element_type=F32)
        n1 = decay * n0 + jnp.sum(kw, axis=0, keepdims=True)
        stores.append(functools.partial(_store_state, c_sc, n_sc, m_sc, h, c1, n1, m_last))
        hn = hh * lax.rsqrt(jnp.mean(hh * hh, axis=-1, keepdims=True) + EPS) * gn[:, h * B_DV:(h + 1) * B_DV]
        outs.append(jax.nn.sigmoid(og[:, h * B_DV:(h + 1) * B_DV]) * hn)
    out = jnp.concatenate(outs, axis=1).astype(ob_ref.dtype)
    stores.append(functools.partial(_store_chunk, ob_ref, hist_sc, out, cur[L - SUBLANES:L]))
    return stores


def _store_state(c_sc, n_sc, m_sc, h, c1, n1, m_last):
    c_sc[h] = c1
    n_sc[h:h + 1, :] = n1
    m_sc[h:h + 1, :] = jnp.broadcast_to(m_last, (1, LANES))


def _store_chunk(ob_ref, hist_sc, out, tail):
    ob_ref[...] = out
    hist_sc[...] = tail


MLSTM_BATCH = 1


def mlstm_prompt(z, conv_w, b_gates_pad, g_norm, bsz, seq):
    L = MLSTM_CHUNK
    nc = seq // L
    nbat = MLSTM_BATCH if bsz % MLSTM_BATCH == 0 else 1
    z3 = z.reshape(bsz, seq, Z_COLS)

    def zspec(width, colblk):
        return pl.BlockSpec((nbat, L, width), lambda b, c: (b, c, colblk))

    const = lambda shape: pl.BlockSpec(shape, lambda b, c: (0,) * len(shape))
    state = lambda shape: pl.BlockSpec((nbat,) + shape, lambda b, c: (b,) + (0,) * len(shape))
    out_b, c1, n1, m1 = pl.pallas_call(
        functools.partial(_mlstm_prompt_body, L=L, nc=nc, nbat=nbat),
        grid=(bsz // nbat, nc),
        in_specs=[zspec(2 * B_QK, Z_QK // (2 * B_QK)), zspec(B_V, Z_VB // B_V), zspec(LANES, Z_GATES // LANES),
                  zspec(B_V, Z_OG // B_V), const((B_CONV, 2 * B_QK)), const((1, LANES)), const((1, B_V))],
        out_specs=[pl.BlockSpec((nbat, L, B_V), lambda b, c: (b, c, 0)),
                   state((B_HEADS, B_DK, B_DV)), state((SUBLANES, B_DK)), state((SUBLANES, LANES))],
        out_shape=[jax.ShapeDtypeStruct((bsz, seq, B_V), BF16),
                   jax.ShapeDtypeStruct((bsz, B_HEADS, B_DK, B_DV), F32),
                   jax.ShapeDtypeStruct((bsz, SUBLANES, B_DK), F32),
                   jax.ShapeDtypeStruct((bsz, SUBLANES, LANES), F32)],
        scratch_shapes=[pltpu.VMEM((nbat, B_HEADS, B_DK, B_DV), F32), pltpu.VMEM((nbat, SUBLANES, B_DK), F32),
                        pltpu.VMEM((nbat, SUBLANES, LANES), F32), pltpu.VMEM((nbat, SUBLANES, 2 * B_QK), F32)],
        compiler_params=_cparams("arbitrary", "arbitrary"),
        name="mlstm_prompt",
    )(z3, z3, z3, z3, conv_w, b_gates_pad, g_norm)
    return out_b.reshape(bsz * seq, B_V), c1, n1, m1


SAMPLE_TILE = 8


def _row_to_col(row, n):
    eye = lax.broadcasted_iota(I32, (n, n), 0) == lax.broadcasted_iota(I32, (n, n), 1)
    return jnp.sum(jnp.where(eye, jnp.broadcast_to(row, (n, n)), 0.0), axis=1, keepdims=True)


def _mix_sample_body(tab_ref, sink_ref, bkt_ref, z_ref, ck_ref, cv_ref, c0_ref, n0_ref, m0_ref, cb_ref,
                     cw_ref, bg_ref, gn_ref,
                     att_ref, ob_ref, nk_ref, nv_ref, c1_ref, n1_ref, m1_ref, ncb_ref, *, tb):
    bk = bkt_ref[...]
    cw = cw_ref[...]
    gn = gn_ref[...]
    zero_half = jnp.zeros((1, A_HEAD_DIM), F32)
    bias_rows = []
    for h in range(A_HEADS):
        bias = jnp.zeros(bk.shape, F32)
        for t in range(REL_BUCKETS):
            bias = jnp.where(bk == t, tab_ref[t, h], bias)
        bias_rows.append(bias)
    bias_c = jnp.concatenate(bias_rows, axis=0)
    head_id = lax.broadcasted_iota(I32, (A_HEADS, 1), 0)
    bias_n = jnp.zeros((A_HEADS, 1), F32)
    sinks = jnp.zeros((A_HEADS, 1), F32)
    for h in range(A_HEADS):
        bias_n = jnp.where(head_id == h, tab_ref[0, h], bias_n)
        sinks = jnp.where(head_id == h, sink_ref[h], sinks)
    att_rows, ob_rows, m_rows = [], [], []
    stores = []
    for i in range(tb):
        zr = z_ref[i:i + 1, :]
        q_att = zr[:, Z_QA:Z_QA + A_Q] * (A_HEAD_DIM ** -0.5)
        ka = zr[:, Z_KA:Z_KA + A_KV]
        va = zr[:, Z_VA:Z_VA + A_KV]
        kc = ck_ref[i]
        vc = cv_ref[i]
        stores.append((nk_ref, i, jnp.concatenate([kc[1:], ka], axis=0)))
        stores.append((nv_ref, i, jnp.concatenate([vc[1:], va], axis=0)))
        q_rows = []
        for h in range(A_HEADS):
            qh = q_att[:, h * A_HEAD_DIM:(h + 1) * A_HEAD_DIM]
            q_rows.append(jnp.concatenate([qh, zero_half] if h // A_GROUP == 0 else [zero_half, qh], axis=1))
        qm = jnp.concatenate(q_rows, axis=0)
        lc = lax.dot_general(qm, kc, (((1,), (1,)), ((), ())), precision=HI,
                             preferred_element_type=F32) + bias_c
        ln = jnp.sum(qm * ka, axis=-1, keepdims=True) + bias_n
        mx = jnp.maximum(jnp.maximum(jnp.max(lc, axis=-1, keepdims=True), ln), sinks)
        pc = jnp.exp(lc - mx)
        pn = jnp.exp(ln - mx)
        den = jnp.sum(pc, axis=-1, keepdims=True) + pn + jnp.exp(sinks - mx)
        o = (jnp.dot(pc, vc, precision=HI, preferred_element_type=F32) + pn * va) / den
        att_rows.append(jnp.concatenate(
            [o[h:h + 1, (h // A_GROUP) * A_HEAD_DIM:(h // A_GROUP + 1) * A_HEAD_DIM] for h in range(A_HEADS)],
            axis=1))

        qk_pre = zr[:, Z_QK:Z_QK + 2 * B_QK]
        hist = cb_ref[i]
        conv = qk_pre * cw[B_CONV - 1:B_CONV]
        for j in range(B_CONV - 1):
            conv = conv + hist[j:j + 1] * cw[j:j + 1]
        stores.append((ncb_ref, i, jnp.concatenate([hist[1:], qk_pre], axis=0)))
        qk = conv * jax.nn.sigmoid(conv)
        G = zr[:, Z_GATES:Z_GATES + LANES] + bg_ref[...]
        lfr = _log_sigmoid(G)
        og = zr[:, Z_OG:Z_OG + B_V]
        v_pre = zr[:, Z_VB:Z_VB + B_V]
        obs, ms = [], []
        for h in range(B_HEADS):
            qh = qk[:, h * B_DK:(h + 1) * B_DK]
            kh = qk[:, B_QK + h * B_DK:B_QK + (h + 1) * B_DK] * (B_DK ** -0.5)
            vh = v_pre[:, h * B_DV:(h + 1) * B_DV]
            ig = G[:, h:h + 1]
            lf = lfr[:, B_HEADS + h:B_HEADS + h + 1]
            c0 = c0_ref[i, h]
            n0 = n0_ref[i, h:h + 1, :]
            m0 = m0_ref[i:i + 1, h:h + 1]
            a = lf + m0
            m = jnp.maximum(a, ig)
            dw = jnp.exp(ig - m)
            aw = jnp.exp(a - m)
            s = jnp.sum(qh * kh, axis=-1, keepdims=True) * dw
            q_col = _row_to_col(qh, B_DK)
            k_col = _row_to_col(kh, B_DK)
            num = s * vh + aw * jnp.sum(q_col * c0, axis=0, keepdims=True)
            den = s + aw * jnp.sum(qh * n0, axis=-1, keepdims=True)
            hh = num / jnp.maximum(jnp.abs(den), jnp.exp(-m))
            stores.append((c1_ref, (i, h), aw * c0 + dw * (k_col * vh)))
            stores.append((n1_ref, (i, slice(h, h + 1), slice(None)), aw * n0 + dw * kh))
            ms.append(m)
            hn = hh * lax.rsqrt(jnp.mean(hh * hh, axis=-1, keepdims=True) + EPS) * gn[:, h * B_DV:(h + 1) * B_DV]
            obs.append(jax.nn.sigmoid(og[:, h * B_DV:(h + 1) * B_DV]) * hn)
        ob_rows.append(jnp.concatenate(obs, axis=1))
        lane = lax.broadcasted_iota(I32, (1, LANES), 1)
        mrow = jnp.zeros((1, LANES), F32)
        for h in range(B_HEADS):
            mrow = jnp.where(lane == h, ms[h], mrow)
        m_rows.append(mrow)
    for ref, idx, val in stores:
        ref[idx] = val
    att_ref[...] = jnp.concatenate(att_rows, axis=0)
    ob_ref[...] = jnp.concatenate(ob_rows, axis=0)
    m1_ref[...] = jnp.concatenate(m_rows, axis=0)


def mix_sample(z, rel_table, sinks, ck, cv, c0, n0, m0, conv_buf, conv_w, b_gates_pad, g_norm):
    nb = z.shape[0]
    tb = min(SAMPLE_TILE, nb)
    n_buf = ck.shape[1]
    bkt = _t5_buckets(n_buf - np.arange(n_buf))[None, :]
    smem = pl.BlockSpec(memory_space=pltpu.SMEM)
    const = lambda shape: pl.BlockSpec(shape, lambda i: (0,) * len(shape))
    lead = lambda shape: pl.BlockSpec((tb,) + shape, lambda i: (i,) + (0,) * len(shape))
    return pl.pallas_call(
        functools.partial(_mix_sample_body, tb=tb),
        grid=(nb // tb,),
        in_specs=[smem, smem, const((1, n_buf)), lead((Z_COLS,)), lead((n_buf, A_KV)), lead((n_buf, A_KV)),
                  lead((B_HEADS, B_DK, B_DV)), lead((B_HEADS, B_DK)), lead((B_HEADS,)),
                  lead((B_CONV - 1, 2 * B_QK)), const((B_CONV, 2 * B_QK)), const((1, LANES)), const((1, B_V))],
        out_specs=[lead((A_Q,)), lead((B_V,)), lead((n_buf, A_KV)), lead((n_buf, A_KV)),
                   lead((B_HEADS, B_DK, B_DV)), lead((B_HEADS, B_DK)), lead((LANES,)),
                   lead((B_CONV - 1, 2 * B_QK))],
        out_shape=[jax.ShapeDtypeStruct((nb, A_Q), F32), jax.ShapeDtypeStruct((nb, B_V), F32),
                   jax.ShapeDtypeStruct(ck.shape, F32), jax.ShapeDtypeStruct(cv.shape, F32),
                   jax.ShapeDtypeStruct(c0.shape, F32), jax.ShapeDtypeStruct(n0.shape, F32),
                   jax.ShapeDtypeStruct((nb, LANES), F32), jax.ShapeDtypeStruct(conv_buf.shape, F32)],
        compiler_params=_cparams("parallel"),
        name="mix_sample",
    )(rel_table, sinks, jnp.asarray(bkt), z, ck, cv, c0, n0, m0, conv_buf, conv_w, b_gates_pad, g_norm)


def _store_row_tiles(ref, val, rows):
    for s in range(ROW_TILES):
        ref[pl.ds(s, rows, stride=ROW_TILES), :] = val[:, s * LANES:(s + 1) * LANES]


def _load_row_tiles(ref, rows, start=0, stride=ROW_TILES):
    return jnp.concatenate([ref[pl.ds(start + s, rows, stride=stride), :] for s in range(ROW_TILES)], axis=1)


def _route(logits):
    lane = lax.broadcasted_iota(I32, logits.shape, 1)
    big = jnp.int32(1 << 20)
    gl = jnp.where(lane < N_GROUPS, logits, NEG_INF)
    gmax = jnp.max(gl, axis=-1, keepdims=True)
    gidx = jnp.min(jnp.where(gl == gmax, lane, big), axis=-1, keepdims=True)
    g_gate = 1.0 / jnp.sum(jnp.exp(gl - gmax), axis=-1, keepdims=True)
    lo = N_GROUPS + gidx * EXPERTS_PER_GROUP
    el = jnp.where((lane >= lo) & (lane < lo + EXPERTS_PER_GROUP), logits, NEG_INF)
    v1 = jnp.max(el, axis=-1, keepdims=True)
    i1 = jnp.min(jnp.where(el == v1, lane, big), axis=-1, keepdims=True)
    el2 = jnp.where(lane == i1, NEG_INF, el)
    v2 = jnp.max(el2, axis=-1, keepdims=True)
    i2 = jnp.min(jnp.where(el2 == v2, lane, big), axis=-1, keepdims=True)
    t = jnp.exp(v2 - v1)
    w1 = g_gate / (1.0 + t)
    w2 = g_gate * t / (1.0 + t)
    gate = jnp.where(lane == 0, w1, jnp.where(lane == 1, w2, 0.0))
    return i1 - N_GROUPS, i2 - N_GROUPS, gate


def _local_sort(e0, e1):
    tm = e0.shape[0]
    lane = lax.broadcasted_iota(I32, (tm, LANES), 1)
    oh0 = (lane == e0).astype(BF16)
    oh1 = (lane == e1).astype(BF16)
    r = lax.broadcasted_iota(I32, (tm, tm), 0)
    c = lax.broadcasted_iota(I32, (tm, tm), 1)
    before = (c < r).astype(BF16)
    cnt0 = jnp.sum(oh0.astype(F32), axis=0, keepdims=True)
    run_n = cnt0 + jnp.sum(oh1.astype(F32), axis=0, keepdims=True)
    er = lax.broadcasted_iota(I32, (LANES, LANES), 0)
    ec = lax.broadcasted_iota(I32, (LANES, LANES), 1)
    run_l = jnp.dot(run_n, (er < ec).astype(F32), precision=HI, preferred_element_type=F32)
    w0 = jnp.dot(before, oh0, preferred_element_type=F32) + run_l
    w1 = jnp.dot(before, oh1, preferred_element_type=F32) + run_l + cnt0
    p0 = jnp.sum(jnp.where(lane == e0, w0, 0.0), axis=-1, keepdims=True)
    p1 = jnp.sum(jnp.where(lane == e1, w1, 0.0), axis=-1, keepdims=True)
    return jnp.where(lane == 0, p0, jnp.where(lane == 1, p1, 0.0)), run_n


MOE_CHUNK = 512


def _moe_chunk(rows, precise):
    return min(rows, 128 if precise else MOE_CHUNK)


def _proj_router_body(*refs, n_in, has_bias, precise, tm):
    a_refs = refs[:n_in]
    w_refs = refs[n_in:2 * n_in]
    k = 2 * n_in
    bias_ref = refs[k] if has_bias else None
    k += 1 if has_bias else 0
    x_ref, g_ref, wr_ref, br_ref, x1_ref, h8_ref, lpos_ref, gate_ref, runn_ref = refs[k:]
    acc = x_ref[...]
    if has_bias:
        acc = acc + bias_ref[...]
    for a_ref, w_ref in zip(a_refs, w_refs):
        acc = acc + _mm(a_ref[...], w_ref[...], precise)
    x1_ref[...] = acc
    h = _rms(acc, g_ref[...])
    _store_row_tiles(h8_ref, h, tm)
    wr = wr_ref[...]
    if precise:
        logits = jnp.dot(h, wr, precision=HI, preferred_element_type=F32)
    else:
        h_hi = h.astype(BF16)
        h_lo = (h - h_hi.astype(F32)).astype(BF16)
        w_hi = wr.astype(BF16)
        w_lo = (wr - w_hi.astype(F32)).astype(BF16)
        logits = (jnp.dot(h_hi, w_hi, preferred_element_type=F32) + jnp.dot(h_lo, w_hi, preferred_element_type=F32)
                  + jnp.dot(h_hi, w_lo, preferred_element_type=F32))
    e0, e1, gate = _route(logits + br_ref[...])
    lpos, run_n = _local_sort(e0, e1)
    lpos_ref[...] = (lpos.T[:SUBLANES] * ROW_TILES).astype(I32)
    gate_ref[...] = gate.T[:SUBLANES]
    runn_ref[...] = jnp.broadcast_to(run_n, runn_ref.shape).astype(I32)


def proj_router(a_list, w_list, bias, x, g, wr, br, *, precise):
    rows, d = x.shape
    tm = _moe_chunk(rows, precise)
    n_in = len(a_list)
    row_spec = lambda width: pl.BlockSpec((tm, width), lambda i: (i, 0))
    const = lambda shape: pl.BlockSpec(shape, lambda i: (0,) * len(shape))
    in_specs = [row_spec(a.shape[1]) for a in a_list] + [const(w.shape) for w in w_list]
    args = list(a_list) + list(w_list)
    if bias is not None:
        in_specs.append(const((1, d)))
        args.append(bias)
    in_specs += [row_spec(d), const((1, d)), const((d, LANES)), const((1, LANES))]
    args += [x, g, wr, br]
    return pl.pallas_call(
        functools.partial(_proj_router_body, n_in=n_in, has_bias=bias is not None, precise=precise, tm=tm),
        grid=(rows // tm,),
        in_specs=in_specs,
        out_specs=[row_spec(d), pl.BlockSpec((tm * ROW_TILES, LANES), lambda i: (i, 0)),
                   pl.BlockSpec((SUBLANES, tm), lambda i: (i, 0)), pl.BlockSpec((SUBLANES, tm), lambda i: (i, 0)),
                   pl.BlockSpec((SUBLANES, LANES), lambda i: (i, 0))],
        out_shape=[jax.ShapeDtypeStruct((rows, d), F32), jax.ShapeDtypeStruct((rows * ROW_TILES, LANES), F32),
                   jax.ShapeDtypeStruct((rows // tm * SUBLANES, tm), I32),
                   jax.ShapeDtypeStruct((rows // tm * SUBLANES, tm), F32),
                   jax.ShapeDtypeStruct((rows // tm * SUBLANES, LANES), I32)],
        compiler_params=_cparams("parallel"),
        name="proj_router",
    )(*args)


def _rows_at(offset):
    return pl.ds(pl.multiple_of(offset, ROW_TILES), ROW_TILES)


def _tile_rows(r, n=1):
    return pl.ds(pl.multiple_of(r * ROW_TILES, ROW_TILES), n * ROW_TILES)


def _pow2_pieces(limit):
    p = 1
    while p * 2 <= limit:
        p *= 2
    out = []
    while p >= 1:
        out.append(p)
        p //= 2
    return out


COMMON_PIECE = 32


def _for_each_piece(n, pieces, fn):
    def emit(ps):
        for p in ps:
            @pl.when((n & p) != 0)
            def _(p=p):
                fn(n & ~(2 * p - 1), p)

    big = [p for p in pieces if p > COMMON_PIECE]
    if big:
        @pl.when(n > 2 * COMMON_PIECE - 1)
        def _():
            emit(big)
    emit([p for p in pieces if p <= COMMON_PIECE])


def _dispatch_body(rn_ref, rg_ref, rl_ref, ps_ref, pn_ref, tail_ref, lpos_ref, h8_ref, xs_hbm,
                   stage, zbuf, sem, zsem, *, chunk, nch, blk):
    c = pl.program_id(0)
    slot = c % 2
    run_pieces = _pow2_pieces(chunk)
    pad_pieces = _pow2_pieces(blk - 1)

    @pl.when(c == 0)
    def _():
        zbuf[...] = jnp.zeros(zbuf.shape, F32)

        def pad_dmas(e, op):
            def one(off, p):
                cp = pltpu.make_async_copy(zbuf.at[_tile_rows(0, p)], xs_hbm.at[_tile_rows(ps_ref[e] + off, p)], zsem)
                cp.start() if op == 0 else cp.wait()
            _for_each_piece(pn_ref[e], pad_pieces, one)

        def issue(e, carry):
            pad_dmas(e, 0)
            return carry

        def wait(e, carry):
            pad_dmas(e, 1)
            return carry
        lax.fori_loop(0, N_EXPERTS, issue, 0)
        lax.fori_loop(0, N_EXPERTS, wait, 0)

        half = blk // 2

        def tail_dmas(i, op):
            cp = pltpu.make_async_copy(zbuf, xs_hbm.at[_tile_rows(tail_ref[0] + i * half, half)], zsem)
            cp.start() if op == 0 else cp.wait()

        def tail_issue(i, carry):
            tail_dmas(i, 0)
            return carry

        def tail_wait(i, carry):
            tail_dmas(i, 1)
            return carry
        lax.fori_loop(0, tail_ref[1], tail_issue, 0)
        lax.fori_loop(0, tail_ref[1], tail_wait, 0)

    def copy_tok(t, carry):
        row = h8_ref[_tile_rows(t), :]
        stage[slot, _rows_at(lpos_ref[0, 0, t]), :] = row
        stage[slot, _rows_at(lpos_ref[0, 0, chunk + t]), :] = row
        return carry
    lax.fori_loop(0, chunk, copy_tok, 0, unroll=8)

    @pl.when(c > 0)
    def _():
        pltpu.make_async_copy(stage.at[1 - slot], stage.at[1 - slot], sem).wait()

    def send_runs(e, carry):
        k = c * N_EXPERTS + e

        def one(off, p):
            pltpu.make_async_copy(stage.at[slot, _tile_rows(rl_ref[k] + off, p)],
                                  xs_hbm.at[_tile_rows(rg_ref[k] + off, p)], sem).start()
        _for_each_piece(rn_ref[k], run_pieces, one)
        return carry
    lax.fori_loop(0, N_EXPERTS, send_runs, 0)

    @pl.when(c == nch - 1)
    def _():
        pltpu.make_async_copy(stage.at[slot], stage.at[slot], sem).wait()


def dispatch(h8, plan):
    chunk, nch, blk = plan['chunk'], plan['nch'], plan['blk']
    n_slots = plan['nblk'] * blk
    gs = pltpu.PrefetchScalarGridSpec(
        num_scalar_prefetch=6,
        grid=(nch,),
        in_specs=[pl.BlockSpec((1, 1, TOP_K * chunk), lambda c, *_: (c, 0, 0), memory_space=pltpu.SMEM),
                  pl.BlockSpec((chunk * ROW_TILES, LANES), lambda c, *_: (c, 0))],
        out_specs=pl.BlockSpec(memory_space=pl.ANY),
        scratch_shapes=[pltpu.VMEM((2, TOP_K * chunk * ROW_TILES, LANES), F32),
                        pltpu.VMEM((blk // 2 * ROW_TILES, LANES), F32),
                        pltpu.SemaphoreType.DMA(()), pltpu.SemaphoreType.DMA(())],
    )
    return pl.pallas_call(
        functools.partial(_dispatch_body, chunk=chunk, nch=nch, blk=blk),
        grid_spec=gs,
        out_shape=jax.ShapeDtypeStruct((n_slots * ROW_TILES, LANES), F32),
        compiler_params=_cparams("arbitrary"),
        name="dispatch",
    )(plan['run_n'], plan['run_g'], plan['run_l'], plan['pad_start'], plan['pad_n'], plan['tail'], plan['lpos'], h8)


FF_CHUNK = 512


def _experts_body(be_ref, nv_ref, xs_ref, wg_ref, wu_ref, wd_ref, ys_ref, xb, *wcast, precise, blk):
    j = pl.program_id(0)
    nv = nv_ref[0]
    slot = j % 2
    k = j - 1

    @pl.when(j == 0)
    def _():
        xb[1] = jnp.zeros(xb.shape[1:], xb.dtype)

    if not precise:
        wgb, wub, wdb = wcast

        @pl.when((k == 0) | ((k > 0) & (k < nv) & (be_ref[jnp.maximum(k, 0)] != be_ref[jnp.maximum(k - 1, 0)])))
        def _():
            wgb[...] = wg_ref[0].astype(BF16)
            wub[...] = wu_ref[0].astype(BF16)
            wdb[...] = wd_ref[0].astype(BF16)

    @pl.when(j <= nv)
    def _():
        xb[slot] = _load_row_tiles(xs_ref, blk).astype(xb.dtype)
        if precise:
            wg, wu, wd = wg_ref[0], wu_ref[0], wd_ref[0]
        else:
            wg, wu, wd = wgb, wub, wdb
        xm = xb[1 - slot]
        gt = _mm(xm, wg[...], precise)
        up = _mm(xm, wu[...], precise)
        _store_row_tiles(ys_ref, _mm(gt * jax.nn.sigmoid(gt) * up, wd[...], precise), blk)

    @pl.when(j > nv)
    def _():
        ys_ref[...] = jnp.zeros(ys_ref.shape, F32)


def experts(xs, plan, wg, wu, wd, layer, *, precise):
    nblk, rows = plan['nblk'], plan['blk'] * ROW_TILES
    d, ff = wg.shape[2], wg.shape[3]
    blk = lambda j, be, nv: (jnp.minimum(j, nv[0] - 1), 0)
    wspec = lambda shape: pl.BlockSpec((None, 1) + shape,
                                       lambda j, be, nv: (layer, be[jnp.clip(j - 1, 0, nv[0] - 1)], 0, 0))
    gs = pltpu.PrefetchScalarGridSpec(
        num_scalar_prefetch=2,
        grid=(nblk + 1,),
        in_specs=[pl.BlockSpec((rows, LANES), blk),
                  wspec((d, ff)), wspec((d, ff)), wspec((ff, d))],
        out_specs=pl.BlockSpec((rows, LANES), lambda j, be, nv: (jnp.maximum(j - 1, 0), 0)),
        scratch_shapes=[pltpu.VMEM((2, plan['blk'], d), F32 if precise else BF16)]
        + ([] if precise else [pltpu.VMEM((d, ff), BF16), pltpu.VMEM((d, ff), BF16), pltpu.VMEM((ff, d), BF16)]),
    )
    return pl.pallas_call(
        functools.partial(_experts_body, precise=precise, blk=plan['blk']),
        grid_spec=gs,
        out_shape=jax.ShapeDtypeStruct(xs.shape, F32),
        compiler_params=_cparams("arbitrary"),
        name="experts",
    )(plan['block_e'], plan['n_used'], xs, wg, wu, wd)


def moe_plan(lpos8, gate8, runn8, blk):
    nch = runn8.shape[0] // SUBLANES
    chunk = lpos8.shape[1]
    n_assign = nch * chunk * TOP_K
    nblk = (n_assign + N_EXPERTS * (blk - 1)) // blk
    run_n = runn8.reshape(nch, SUBLANES, LANES)[:, 0, :N_EXPERTS]
    per_chunk = lambda a: a.reshape(nch, 1, SUBLANES * chunk)[:, :, :TOP_K * chunk]
    counts = jnp.sum(run_n, axis=0)
    padded = (counts + blk - 1) // blk * blk
    pends = jnp.cumsum(padded)
    pstarts = pends - padded
    run_g = pstarts[None, :] + jnp.cumsum(run_n, axis=0) - run_n
    run_l = jnp.cumsum(run_n, axis=1) - run_n
    blk_start = jnp.arange(nblk, dtype=I32) * blk
    block_e = jnp.minimum(jnp.sum((pends[None, :] <= blk_start[:, None]).astype(I32), axis=1), N_EXPERTS - 1)
    return dict(chunk=chunk, nch=nch, nblk=nblk, blk=blk,
                run_n=run_n.reshape(-1).astype(I32), run_g=run_g.reshape(-1).astype(I32),
                run_l=run_l.reshape(-1).astype(I32), pad_start=(pstarts + counts).astype(I32),
                pad_n=(padded - counts).astype(I32), lpos=per_chunk(lpos8), gate=per_chunk(gate8), block_e=block_e.astype(I32),
                n_used=(pends[-1:] // blk).astype(I32),
                tail=jnp.stack([pends[-1], 2 * (nblk - pends[-1] // blk)]).astype(I32))


def _combine(rn_ref, rg_ref, rl_ref, lpos_ref, gate_ref, x_ref, ys_hbm, ystage, comb, sem, *, chunk, nch):
    c = pl.program_id(0)
    slot = c % 2
    pieces = _pow2_pieces(chunk)

    def fetch(cc, sl):
        def per_e(e, carry):
            k = cc * N_EXPERTS + e

            def one(off, p):
                pltpu.make_async_copy(ys_hbm.at[_tile_rows(rg_ref[k] + off, p)],
                                      ystage.at[sl, _tile_rows(rl_ref[k] + off, p)], sem.at[sl]).start()
            _for_each_piece(rn_ref[k], pieces, one)
            return carry
        lax.fori_loop(0, N_EXPERTS, per_e, 0)

    @pl.when(c == 0)
    def _():
        fetch(0, 0)

    @pl.when(c + 1 < nch)
    def _():
        fetch(c + 1, 1 - slot)

    pltpu.make_async_copy(ystage.at[slot], ystage.at[slot], sem.at[slot]).wait()

    def per_tok(t, carry):
        y0 = ystage[slot, _rows_at(lpos_ref[0, 0, t]), :]
        y1 = ystage[slot, _rows_at(lpos_ref[0, 0, chunk + t]), :]
        comb[_tile_rows(t), :] = gate_ref[0, 0, t] * y0 + gate_ref[0, 0, chunk + t] * y1
        return carry
    lax.fori_loop(0, chunk, per_tok, 0, unroll=8)
    return x_ref[...] + _load_row_tiles(comb, chunk)


def _combine_glu_body(rn_ref, rg_ref, rl_ref, lpos_ref, gate_ref, x_ref, ys_hbm, g_ref, w_ref, b_ref,
                      x2_ref, u_ref, ystage, comb, sem, *, chunk, nch, precise):
    x2 = _combine(rn_ref, rg_ref, rl_ref, lpos_ref, gate_ref, x_ref, ys_hbm, ystage, comb, sem, chunk=chunk, nch=nch)
    x2_ref[...] = x2
    zz = _mm(_rms(x2, g_ref[...]), w_ref[...], precise) + b_ref[...]
    half = zz.shape[1] // 2
    u_ref[...] = zz[:, :half] * jax.nn.sigmoid(zz[:, half:])


def _combine_final_body(rn_ref, rg_ref, rl_ref, lpos_ref, gate_ref, x_ref, ys_hbm, g_ref, o_ref,
                        ystage, comb, sem, *, chunk, nch):
    x2 = _combine(rn_ref, rg_ref, rl_ref, lpos_ref, gate_ref, x_ref, ys_hbm, ystage, comb, sem, chunk=chunk, nch=nch)
    o_ref[...] = _rms(x2, g_ref[...])


def _combine_call(body, plan, x, ys, extra, extra_specs, out_specs, out_shape, name):
    chunk, nch = plan['chunk'], plan['nch']
    d = x.shape[1]
    smem_blk = pl.BlockSpec((1, 1, TOP_K * chunk), lambda c, *_: (c, 0, 0), memory_space=pltpu.SMEM)
    gs = pltpu.PrefetchScalarGridSpec(
        num_scalar_prefetch=3,
        grid=(nch,),
        in_specs=[smem_blk, smem_blk, pl.BlockSpec((chunk, d), lambda c, *_: (c, 0)),
                  pl.BlockSpec(memory_space=pl.ANY)] + extra_specs,
        out_specs=out_specs,
        scratch_shapes=[pltpu.VMEM((2, TOP_K * chunk * ROW_TILES, LANES), F32),
                        pltpu.VMEM((chunk * ROW_TILES, LANES), F32), pltpu.SemaphoreType.DMA((2,))],
    )
    return pl.pallas_call(
        functools.partial(body, chunk=chunk, nch=nch),
        grid_spec=gs,
        out_shape=out_shape,
        compiler_params=_cparams("arbitrary"),
        name=name,
    )(plan['run_n'], plan['run_g'], plan['run_l'], plan['lpos'], plan['gate'], x, ys, *extra)


def combine_glu(x, ys, plan, g, w, b, *, precise):
    rows, d = x.shape
    chunk = plan['chunk']
    cols = w.shape[1]
    const = lambda shape: pl.BlockSpec(shape, lambda c, *_: (0,) * len(shape))
    row_spec = lambda width: pl.BlockSpec((chunk, width), lambda c, *_: (c, 0))
    return _combine_call(
        functools.partial(_combine_glu_body, precise=precise), plan, x, ys, [g, w, b],
        [const((1, d)), const((d, cols)), const((1, cols))], [row_spec(d), row_spec(cols // 2)],
        [jax.ShapeDtypeStruct((rows, d), F32), jax.ShapeDtypeStruct((rows, cols // 2), F32)], "combine_glu")


def combine_final(x, ys, plan, g):
    rows, d = x.shape
    chunk = plan['chunk']
    return _combine_call(
        _combine_final_body, plan, x, ys, [g], [pl.BlockSpec((1, d), lambda c, *_: (0, 0))],
        pl.BlockSpec((chunk, d), lambda c, *_: (c, 0)), jax.ShapeDtypeStruct((rows, d), F32), "combine_final")


CONV_TILE = 256
CONV_HIST = 32


def _ln_swish(y, g, b):
    yc = y - jnp.mean(y, axis=-1, keepdims=True)
    yn = yc * lax.rsqrt(jnp.mean(yc * yc, axis=-1, keepdims=True) + EPS) * g + b
    return yn * jax.nn.sigmoid(yn)


CONV_ROWS = 64
LN_ROWS = 16
LN_UNROLL = 8


def _dwconv_prompt_body(u_ref, w_ref, bdw_ref, g_ref, b_ref, o_ref, ext, y_sc, *, tt):
    t = pl.program_id(1)
    n_lt = ext.shape[0]

    @pl.when(t == 0)
    def _():
        ext[:, 0:CONV_HIST, :] = jnp.zeros((n_lt, CONV_HIST, LANES), F32)

    @pl.when(t > 0)
    def _():
        ext[:, 0:CONV_HIST, :] = ext[:, tt:tt + CONV_HIST, :]

    for j in range(n_lt):
        ext[j, CONV_HIST:CONV_HIST + tt, :] = u_ref[:, j * LANES:(j + 1) * LANES]
    off = CONV_HIST - (C_KERNEL - 1)
    for j in range(n_lt):
        wj = w_ref[:, j * LANES:(j + 1) * LANES]
        bj = bdw_ref[:, j * LANES:(j + 1) * LANES]
        for c in range(tt // CONV_ROWS):
            acc = ext[j, pl.ds(off + c * CONV_ROWS, CONV_ROWS), :] * wj[0:1] + bj
            for k in range(1, C_KERNEL):
                acc = acc + ext[j, pl.ds(off + k + c * CONV_ROWS, CONV_ROWS), :] * wj[k:k + 1]
            y_sc[c * CONV_ROWS:(c + 1) * CONV_ROWS, j * LANES:(j + 1) * LANES] = acc

    def ln_rows(r, carry):
        rows = pl.ds(pl.multiple_of(r * LN_ROWS, LN_ROWS), LN_ROWS)
        o_ref[rows, :] = _ln_swish(y_sc[rows, :], g_ref[...], b_ref[...]).astype(o_ref.dtype)
        return carry
    lax.fori_loop(0, tt // LN_ROWS, ln_rows, 0, unroll=LN_UNROLL)


def dwconv_prompt(u, w, b_dw, ln_g, ln_b, bsz, seq):
    tt = min(CONV_TILE, seq)
    nt = seq // tt
    d = u.shape[1]
    const = lambda shape: pl.BlockSpec(shape, lambda b, t: (0,) * len(shape))
    return pl.pallas_call(
        functools.partial(_dwconv_prompt_body, tt=tt),
        grid=(bsz, nt),
        in_specs=[pl.BlockSpec((tt, d), lambda b, t: (b * nt + t, 0)), const((C_KERNEL, d)), const((1, d)),
                  const((1, d)), const((1, d))],
        out_specs=pl.BlockSpec((tt, d), lambda b, t: (b * nt + t, 0)),
        out_shape=jax.ShapeDtypeStruct((bsz * seq, d), BF16),
        scratch_shapes=[pltpu.VMEM((d // LANES, CONV_HIST + tt, LANES), F32), pltpu.VMEM((tt, d), F32)],
        compiler_params=_cparams("arbitrary", "arbitrary"),
        name="dwconv_prompt",
    )(u, w, b_dw, ln_g, ln_b)


def _dwconv_sample_body(u_ref, buf_ref, w_ref, bdw_ref, g_ref, b_ref, o_ref, nbuf_ref, *, tb):
    w = w_ref[...]
    rows = []
    for i in range(tb):
        hist = buf_ref[i]
        ur = u_ref[i:i + 1, :]
        rows.append(jnp.sum(hist * w[:C_KERNEL - 1], axis=0, keepdims=True) + ur * w[C_KERNEL - 1:C_KERNEL])
        nbuf_ref[i] = jnp.concatenate([hist[1:], ur], axis=0)
    y = jnp.concatenate(rows, axis=0) + bdw_ref[...]
    o_ref[...] = _ln_swish(y, g_ref[...], b_ref[...])


def dwconv_sample(u, buf, w, b_dw, ln_g, ln_b):
    nb, d = u.shape
    tb = min(SAMPLE_TILE, nb)
    const = lambda shape: pl.BlockSpec(shape, lambda i: (0,) * len(shape))
    return pl.pallas_call(
        functools.partial(_dwconv_sample_body, tb=tb),
        grid=(nb // tb,),
        in_specs=[pl.BlockSpec((tb, d), lambda i: (i, 0)), pl.BlockSpec((tb, C_KERNEL - 1, d), lambda i: (i, 0, 0)),
                  const((C_KERNEL, d)), const((1, d)), const((1, d)), const((1, d))],
        out_specs=[pl.BlockSpec((tb, d), lambda i: (i, 0)), pl.BlockSpec((tb, C_KERNEL - 1, d), lambda i: (i, 0, 0))],
        out_shape=[jax.ShapeDtypeStruct((nb, d), F32), jax.ShapeDtypeStruct(buf.shape, F32)],
        compiler_params=_cparams("parallel"),
        name="dwconv_sample",
    )(u, buf, w, b_dw, ln_g, ln_b)


def _moe(h8, lpos128, gate128, runn8, wg, wu, wd, layer, *, precise):
    plan = moe_plan(lpos128, gate128, runn8, EXPERT_BLOCK_PRECISE if precise else EXPERT_BLOCK)
    xs = dispatch(h8, plan)
    return experts(xs, plan, wg, wu, wd, layer, precise=precise), plan


def _trunk(x, caches, p, *, prompt):
    bsz, seq, d = x.shape
    rows = bsz * seq
    precise = not prompt
    wdt = F32 if precise else BF16
    xf = x.reshape(rows, d)
    row = lambda v: v.reshape(1, -1).astype(F32)

    z = norm_proj(xf, row(p['norm_mix'][0]), p['w_in'].astype(wdt), precise=precise)
    if prompt:
        att = attn_prompt(z, p['rel_table'], p['sinks'], bsz, seq)
        out_b, c1, n1, m1 = mlstm_prompt(z, p['conv_w'], p['b_gates'], p['g_mnorm'], bsz, seq)
        z3 = z.reshape(bsz, seq, Z_COLS)
        new_k = z3[:, seq - WINDOW:, Z_KA:Z_KA + A_KV].reshape(bsz, WINDOW, A_KV_HEADS, A_HEAD_DIM)
        new_v = z3[:, seq - WINDOW:, Z_VA:Z_VA + A_KV].reshape(bsz, WINDOW, A_KV_HEADS, A_HEAD_DIM)
        new_conv = z3[:, seq - (B_CONV - 1):, Z_QK:Z_QK + 2 * B_QK]
        n1 = n1[:, :B_HEADS]
        m1 = m1[:, :B_HEADS, 0]
    else:
        ck, cv, c0, n0, m0, cbuf = caches[:6]
        n_buf = ck.shape[1]
        att, out_b, new_k, new_v, c1, n1, m1, new_conv = mix_sample(
            z, p['rel_table'], p['sinks'], ck.reshape(bsz, n_buf, A_KV), cv.reshape(bsz, n_buf, A_KV),
            c0, n0, m0, cbuf, p['conv_w'], p['b_gates'], p['g_mnorm'])
        new_k = new_k.reshape(bsz, n_buf, A_KV_HEADS, A_HEAD_DIM)
        new_v = new_v.reshape(bsz, n_buf, A_KV_HEADS, A_HEAD_DIM)
        m1 = m1[:, :B_HEADS]
    w_out = p['w_out'].astype(wdt)
    x1, h8, lpos, gate, runn = proj_router([att, out_b], [w_out[:A_Q], w_out[A_Q:]], None, xf,
                                    row(p['norm_ffn'][0]), p['w_router'][0], p['b_router'][0], precise=precise)
    ys, plan = _moe(h8, lpos, gate, runn, p['w_eg'], p['w_eu'], p['w_ed'], 0, precise=precise)

    x2, u = combine_glu(x1, ys, plan, row(p['norm_mix'][1]), p['w_pw1'].astype(wdt), row(p['b_pw1']),
                        precise=precise)
    if prompt:
        yc = dwconv_prompt(u, p['w_dw'], row(p['b_dw']), row(p['ln_g']), row(p['ln_b']), bsz, seq)
        new_cbuf = u.reshape(bsz, seq, d)[:, seq - (C_KERNEL - 1):]
    else:
        yc, new_cbuf = dwconv_sample(u, caches[6], p['w_dw'], row(p['b_dw']), row(p['ln_g']), row(p['ln_b']))
    x3, h8, lpos, gate, runn = proj_router([yc], [p['w_pw2'].astype(wdt)], row(p['b_pw2']), x2,
                                    row(p['norm_ffn'][1]), p['w_router'][1], p['b_router'][1], precise=precise)
    ys, plan = _moe(h8, lpos, gate, runn, p['w_eg'], p['w_eu'], p['w_ed'], 1, precise=precise)
    y = combine_final(x3, ys, plan, row(p['norm_final']))
    add_layer = lambda t: t[None]
    return (y.reshape(bsz, seq, d),) + tuple(add_layer(t) for t in (new_k, new_v, c1, n1, m1, new_conv, new_cbuf))


def kernel(x_prompt, x_sample, cache_win_k, cache_win_v, state_mlstm_c, state_mlstm_n, state_mlstm_m, state_mlstm_conv, state_conv, norm_mix, norm_ffn, norm_final, rel_bias_table, w_in_mix, b_mlstm_gates, w_mlstm_qk_conv, attn_sinks, g_mlstm_norm, w_out_mix, w_pw1, b_pw1, w_dw, b_dw, ln_conv_g, ln_conv_b, w_pw2, b_pw2, w_router_group, b_router_group, w_router_expert, b_router_expert, w_expert_gate, w_expert_up, w_expert_down):
    w_in = w_in_mix[0]
    s_q, s_k, s_v, s_qk, s_vb, s_g = A_Q, A_Q + A_KV, A_Q + 2 * A_KV, A_Q + 2 * A_KV + 2 * B_QK, \
        A_Q + 2 * A_KV + 2 * B_QK + B_V, A_Q + 2 * A_KV + 2 * B_QK + B_V + 2 * B_HEADS
    w_in_r = jnp.concatenate([w_in[:, :s_q], w_in[:, s_v:s_qk], w_in[:, s_qk:s_vb], w_in[:, s_g:],
                              w_in[:, s_q:s_k], w_in[:, s_k:s_v], w_in[:, s_vb:s_g],
                              jnp.zeros((D_MODEL, LANES - 2 * B_HEADS), F32)], axis=1)
    b_gates = jnp.concatenate([b_mlstm_gates[0], jnp.zeros((LANES - 2 * B_HEADS,), F32)]).reshape(1, LANES)
    depth = w_router_group.shape[0]
    w_re = jnp.transpose(w_router_expert, (0, 2, 1, 3)).reshape(depth, D_MODEL, N_EXPERTS)
    w_router = jnp.concatenate([w_router_group, w_re,
                                jnp.zeros((depth, D_MODEL, LANES - N_GROUPS - N_EXPERTS), F32)], axis=-1)
    b_router = jnp.concatenate([b_router_group, b_router_expert.reshape(depth, N_EXPERTS),
                                jnp.zeros((depth, LANES - N_GROUPS - N_EXPERTS), F32)], axis=-1)[:, None, :]
    p = dict(norm_mix=norm_mix, norm_ffn=norm_ffn, norm_final=norm_final, rel_table=rel_bias_table,
             sinks=attn_sinks[0], w_in=w_in_r, b_gates=b_gates, conv_w=w_mlstm_qk_conv[0],
             g_mnorm=g_mlstm_norm[0].reshape(1, B_V), w_out=w_out_mix[0], w_pw1=w_pw1[0], b_pw1=b_pw1[0],
             w_dw=w_dw[0], b_dw=b_dw[0], ln_g=ln_conv_g[0], ln_b=ln_conv_b[0], w_pw2=w_pw2[0], b_pw2=b_pw2[0],
             w_router=w_router, b_router=b_router, w_eg=w_expert_gate, w_eu=w_expert_up, w_ed=w_expert_down)
    caches = (cache_win_k[0], cache_win_v[0], state_mlstm_c[0], state_mlstm_n[0], state_mlstm_m[0],
              state_mlstm_conv[0], state_conv[0])
    out_p = _trunk(x_prompt, None, p, prompt=True)
    out_s = _trunk(x_sample, caches, p, prompt=False)
    return (out_p[0], out_s[0]) + out_p[1:] + out_s[1:]
```

```python
import functools
import math

import numpy as np
import jax
import jax.numpy as jnp
from jax import lax
from jax.experimental import pallas as pl
from jax.experimental.pallas import tpu as pltpu

F32 = jnp.float32
BF16 = jnp.bfloat16
I32 = jnp.int32
HI = lax.Precision.HIGHEST
NEG_INF = float("-inf")

LANES = 128
SUBLANES = 8
VMEM_LIMIT = 56 * 1024 * 1024

D_MODEL = 1024
A_HEADS = 8
A_KV_HEADS = 2
A_GROUP = A_HEADS // A_KV_HEADS
A_HEAD_DIM = 64
WINDOW = 128
REL_BUCKETS = 32
REL_MAX_DIST = 128
B_HEADS = 4
B_DK = 64
B_DV = 128
B_CONV = 4
C_KERNEL = 31
N_GROUPS = 4
EXPERTS_PER_GROUP = 8
N_EXPERTS = N_GROUPS * EXPERTS_PER_GROUP
TOP_K = 2
EXPERT_FF = D_MODEL // 2
EXPERT_BLOCK = 256
EXPERT_BLOCK_PRECISE = 128
EPS = 1e-6

A_Q = A_HEADS * A_HEAD_DIM
A_KV = A_KV_HEADS * A_HEAD_DIM
B_QK = B_HEADS * B_DK
B_V = B_HEADS * B_DV
ROW_TILES = D_MODEL // LANES

Z_QA, Z_QK, Z_VB, Z_OG, Z_KA, Z_VA, Z_GATES = 0, 512, 1024, 1536, 2048, 2176, 2304
Z_COLS = 2432
MLSTM_CHUNK = 128


def _cparams(*sem):
    return pltpu.CompilerParams(dimension_semantics=sem, vmem_limit_bytes=VMEM_LIMIT)


def _rms(x, g):
    return x * lax.rsqrt(jnp.mean(x * x, axis=-1, keepdims=True) + EPS) * g


def _mm(a, w, precise):
    if not precise:
        return jnp.dot(a.astype(BF16), w, preferred_element_type=F32)
    a = a.astype(F32)
    a_hi = a.astype(BF16)
    a_lo = (a - a_hi.astype(F32)).astype(BF16)
    w_hi = w.astype(BF16)
    w_lo = (w - w_hi.astype(F32)).astype(BF16)
    return (jnp.dot(a_hi, w_hi, preferred_element_type=F32) + jnp.dot(a_lo, w_hi, preferred_element_type=F32)
            + jnp.dot(a_hi, w_lo, preferred_element_type=F32))


def _t5_buckets(dist):
    exact = REL_BUCKETS // 2
    d = np.maximum(dist, 0)
    large = exact + (np.log(np.maximum(d, 1).astype(np.float32) / exact)
                     / math.log(REL_MAX_DIST / exact) * (REL_BUCKETS - exact)).astype(np.int32)
    return np.where(d < exact, d, np.minimum(large, REL_BUCKETS - 1)).astype(np.int32)


P_QK, P_OG, P_KA, P_VA, P_GATES = 0, 512, 1024, 1152, 1280
P_COLS = 1408
PB_QA, PB_VB = 0, 512
PB_COLS = 1024


def _norm_proj_body(x_ref, g_ref, w_ref, *o_refs, precise):
    h = _rms(x_ref[...], g_ref[...])
    z = _mm(h, w_ref[...], precise)
    if len(o_refs) == 1:
        o_refs[0][...] = z
        return
    z32_ref, zb_ref = o_refs
    z32_ref[...] = jnp.concatenate([z[:, Z_QK:Z_QK + 2 * B_QK], z[:, Z_OG:Z_OG + B_V], z[:, Z_KA:Z_GATES + LANES]],
                                   axis=1)
    zb_ref[...] = jnp.concatenate([z[:, Z_QA:Z_QA + A_Q], z[:, Z_VB:Z_VB + B_V]], axis=1).astype(BF16)


def norm_proj(x, g, w, *, precise):
    rows, d = x.shape
    cols = w.shape[1]
    tm = min(rows, 128 if precise else 512)
    row_spec = lambda width: pl.BlockSpec((tm, width), lambda i: (i, 0))
    if precise:
        out_specs, out_shape = row_spec(cols), jax.ShapeDtypeStruct((rows, cols), F32)
    else:
        out_specs = [row_spec(P_COLS), row_spec(PB_COLS)]
        out_shape = [jax.ShapeDtypeStruct((rows, P_COLS), F32), jax.ShapeDtypeStruct((rows, PB_COLS), BF16)]
    return pl.pallas_call(
        functools.partial(_norm_proj_body, precise=precise),
        grid=(rows // tm,),
        in_specs=[row_spec(d), pl.BlockSpec((1, d), lambda i: (0, 0)), pl.BlockSpec((d, cols), lambda i: (0, 0))],
        out_specs=out_specs,
        out_shape=out_shape,
        compiler_params=_cparams("parallel"),
        name="norm_proj",
    )(x, g, w)


def _attn_prompt_body(tab_ref, sink_ref, bkt_ref, q_ref, kp_ref, kc_ref, vp_ref, vc_ref, o_ref, bias_ref):
    b = pl.program_id(0)
    n = pl.program_id(1)

    @pl.when((b == 0) & (n == 0))
    def _():
        bk = bkt_ref[...]
        first = lax.broadcasted_iota(I32, bk.shape, 1) >= WINDOW
        for h in range(A_HEADS):
            acc = jnp.full(bk.shape, NEG_INF, F32)
            for t in range(REL_BUCKETS):
                acc = jnp.where(bk == t, tab_ref[t, h], acc)
            bias_ref[h] = acc
            bias_ref[A_HEADS + h] = jnp.where(first, acc, NEG_INF)

    q = q_ref[...] * (A_HEAD_DIM ** -0.5)
    kb = jnp.concatenate([kp_ref[...], kc_ref[...]], axis=0).astype(BF16)
    vb = jnp.concatenate([vp_ref[...], vc_ref[...]], axis=0).astype(BF16)
    table = jnp.where(n == 0, A_HEADS, 0)
    outs = []
    for h in range(A_HEADS):
        kvh = h // A_GROUP
        qh = q[:, h * A_HEAD_DIM:(h + 1) * A_HEAD_DIM].astype(BF16)
        kh = kb[:, kvh * A_HEAD_DIM:(kvh + 1) * A_HEAD_DIM]
        vh = vb[:, kvh * A_HEAD_DIM:(kvh + 1) * A_HEAD_DIM]
        s = lax.dot_general(qh, kh, (((1,), (1,)), ((), ())), preferred_element_type=F32)
        s = s + bias_ref[table + h]
        sink = sink_ref[h]
        mx = jnp.maximum(jnp.max(s, axis=-1, keepdims=True), sink)
        p = jnp.exp(s - mx)
        den = jnp.sum(p, axis=-1, keepdims=True) + jnp.exp(sink - mx)
        outs.append(jnp.dot(p.astype(BF16), vh, preferred_element_type=F32) / den)
    o_ref[...] = jnp.concatenate(outs, axis=1).astype(o_ref.dtype)


def attn_prompt(z, zb, rel_table, sinks, bsz, seq):
    nb = seq // WINDOW
    dist = WINDOW + np.arange(WINDOW)[:, None] - np.arange(2 * WINDOW)[None, :]
    bkt = np.where((dist >= 0) & (dist <= WINDOW), _t5_buckets(dist), -1).astype(np.int32)
    kcol, vcol = P_KA // LANES, P_VA // LANES
    smem = pl.BlockSpec(memory_space=pltpu.SMEM)

    def cur(c):
        return pl.BlockSpec((WINDOW, LANES), lambda b, n: (b * nb + n, c))

    def prev(c):
        return pl.BlockSpec((WINDOW, LANES), lambda b, n: (b * nb + jnp.maximum(n - 1, 0), c))

    return pl.pallas_call(
        _attn_prompt_body,
        grid=(bsz, nb),
        in_specs=[smem, smem,
                  pl.BlockSpec((WINDOW, 2 * WINDOW), lambda b, n: (0, 0)),
                  pl.BlockSpec((WINDOW, A_Q), lambda b, n: (b * nb + n, PB_QA // A_Q)),
                  prev(kcol), cur(kcol), prev(vcol), cur(vcol)],
        out_specs=pl.BlockSpec((WINDOW, A_Q), lambda b, n: (b * nb + n, 0)),
        out_shape=jax.ShapeDtypeStruct((bsz * seq, A_Q), BF16),
        scratch_shapes=[pltpu.VMEM((2 * A_HEADS, WINDOW, 2 * WINDOW), F32)],
        compiler_params=_cparams("arbitrary", "arbitrary"),
        name="attn_prompt",
    )(rel_table, sinks, jnp.asarray(bkt), zb, z, z, z, z)


def _log_sigmoid(x):
    return -(jnp.maximum(-x, 0.0) + jnp.log1p(jnp.exp(-jnp.abs(x))))


def _mlstm_prompt_body(qk_ref, v_ref, g_ref, og_ref, cw_ref, bg_ref, gn_ref,
                       ob_ref, c_out, n_out, m_out,
                       c_sc, n_sc, m_sc, hist_sc, *, L, nc, nbat):
    ci = pl.program_id(1)

    @pl.when(ci == 0)
    def _():
        c_sc[...] = jnp.zeros(c_sc.shape, F32)
        n_sc[...] = jnp.zeros(n_sc.shape, F32)
        m_sc[...] = jnp.full(m_sc.shape, NEG_INF, F32)
        hist_sc[...] = jnp.zeros(hist_sc.shape, F32)

    row = lax.broadcasted_iota(I32, (L, L), 0)
    colm = lax.broadcasted_iota(I32, (L, L), 1)
    causal = colm <= row
    tril = causal.astype(F32)
    stores = []
    for bb in range(nbat):
        stores += _mlstm_chunk(qk_ref.at[bb], v_ref.at[bb], g_ref.at[bb], og_ref.at[bb], cw_ref, bg_ref, gn_ref,
                               ob_ref.at[bb], c_sc.at[bb], n_sc.at[bb], m_sc.at[bb], hist_sc.at[bb],
                               causal, tril, L)
    for store in stores:
        store()

    @pl.when(ci == nc - 1)
    def _():
        c_out[...] = c_sc[...]
        n_out[...] = n_sc[...]
        m_out[...] = m_sc[...]


def _mlstm_chunk(qk_ref, v_ref, g_ref, og_ref, cw_ref, bg_ref, gn_ref, ob_ref, c_sc, n_sc, m_sc, hist_sc,
                 causal, tril, L):
    cur = qk_ref[...]
    ext = jnp.concatenate([hist_sc[...], cur], axis=0)
    cw = cw_ref[...]
    off = SUBLANES - (B_CONV - 1)
    conv = ext[off:off + L] * cw[0:1]
    for j in range(1, B_CONV):
        conv = conv + ext[off + j:off + j + L] * cw[j:j + 1]
    qk = conv * jax.nn.sigmoid(conv)
    q_all = qk[:, :B_QK]
    k_all = qk[:, B_QK:] * (B_DK ** -0.5)
    v_all = v_ref[...]
    og = og_ref[...]
    gn = gn_ref[...]

    G = g_ref[...] + bg_ref[...]
    lf = _log_sigmoid(G)
    Bc = jnp.dot(tril, lf, precision=HI, preferred_element_type=F32)
    BT = Bc.T
    GT = G.T

    outs, stores = [], []
    for h in range(B_HEADS):
        qh = q_all[:, h * B_DK:(h + 1) * B_DK]
        kh = k_all[:, h * B_DK:(h + 1) * B_DK]
        vh = v_all[:, h * B_DV:(h + 1) * B_DV]
        b_col = Bc[:, B_HEADS + h:B_HEADS + h + 1]
        ig_col = G[:, h:h + 1]
        b_row = BT[B_HEADS + h:B_HEADS + h + 1, :]
        ig_row = GT[h:h + 1, :]
        c0 = c_sc[h]
        n0 = n_sc[h:h + 1, :]
        m0 = m_sc[h:h + 1, 0:1]
        a = b_col + m0
        d = jnp.where(causal, b_col - b_row + ig_row, NEG_INF)
        m = jnp.maximum(a, jnp.max(d, axis=-1, keepdims=True))
        dw = jnp.exp(d - m)
        aw = jnp.exp(a - m)
        qb = qh.astype(BF16)
        vb = vh.astype(BF16)
        s = lax.dot_general(qb, kh.astype(BF16), (((1,), (1,)), ((), ())), preferred_element_type=F32) * dw
        num = (jnp.dot(s.astype(BF16), vb, preferred_element_type=F32)
               + aw * jnp.dot(qb, c0.astype(BF16), preferred_element_type=F32))
        den = jnp.sum(s, axis=-1, keepdims=True) + aw * jnp.sum(qh * n0, axis=-1, keepdims=True)
        hh = num / jnp.maximum(jnp.abs(den), jnp.exp(-m))
        m_last = m[L - 1:L, :]
        wl = jnp.exp(b_col[L - 1:L, :] - b_col + ig_col - m_last)
        decay = aw[L - 1:L, :]
        kw = kh * wl
        c1 = decay * c0 + lax.dot_general(kw.astype(BF16), vb, (((0,), (0,)), ((), ())),
                                          preferred_element_type=F32)
        n1 = decay * n0 + jnp.sum(kw, axis=0, keepdims=True)
        stores.append(functools.partial(_store_state, c_sc, n_sc, m_sc, h, c1, n1, m_last))
        hn = hh * lax.rsqrt(jnp.mean(hh * hh, axis=-1, keepdims=True) + EPS) * gn[:, h * B_DV:(h + 1) * B_DV]
        outs.append(jax.nn.sigmoid(og[:, h * B_DV:(h + 1) * B_DV]) * hn)
    out = jnp.concatenate(outs, axis=1).astype(ob_ref.dtype)
    stores.append(functools.partial(_store_chunk, ob_ref, hist_sc, out, cur[L - SUBLANES:L]))
    return stores


def _store_state(c_sc, n_sc, m_sc, h, c1, n1, m_last):
    c_sc[h] = c1
    n_sc[h:h + 1, :] = n1
    m_sc[h:h + 1, :] = jnp.broadcast_to(m_last, (1, LANES))


def _store_chunk(ob_ref, hist_sc, out, tail):
    ob_ref[...] = out
    hist_sc[...] = tail


MLSTM_BATCH = 1


def mlstm_prompt(z, zb, conv_w, b_gates_pad, g_norm, bsz, seq):
    L = MLSTM_CHUNK
    nc = seq // L
    nbat = MLSTM_BATCH if bsz % MLSTM_BATCH == 0 else 1
    z3 = z.reshape(bsz, seq, P_COLS)
    zb3 = zb.reshape(bsz, seq, PB_COLS)

    def zspec(width, colblk):
        return pl.BlockSpec((nbat, L, width), lambda b, c: (b, c, colblk))

    const = lambda shape: pl.BlockSpec(shape, lambda b, c: (0,) * len(shape))
    state = lambda shape: pl.BlockSpec((nbat,) + shape, lambda b, c: (b,) + (0,) * len(shape))
    out_b, c1, n1, m1 = pl.pallas_call(
        functools.partial(_mlstm_prompt_body, L=L, nc=nc, nbat=nbat),
        grid=(bsz // nbat, nc),
        in_specs=[zspec(2 * B_QK, P_QK // (2 * B_QK)), zspec(B_V, PB_VB // B_V), zspec(LANES, P_GATES // LANES),
                  zspec(B_V, P_OG // B_V), const((B_CONV, 2 * B_QK)), const((1, LANES)), const((1, B_V))],
        out_specs=[pl.BlockSpec((nbat, L, B_V), lambda b, c: (b, c, 0)),
                   state((B_HEADS, B_DK, B_DV)), state((SUBLANES, B_DK)), state((SUBLANES, LANES))],
        out_shape=[jax.ShapeDtypeStruct((bsz, seq, B_V), BF16),
                   jax.ShapeDtypeStruct((bsz, B_HEADS, B_DK, B_DV), F32),
                   jax.ShapeDtypeStruct((bsz, SUBLANES, B_DK), F32),
                   jax.ShapeDtypeStruct((bsz, SUBLANES, LANES), F32)],
        scratch_shapes=[pltpu.VMEM((nbat, B_HEADS, B_DK, B_DV), F32), pltpu.VMEM((nbat, SUBLANES, B_DK), F32),
                        pltpu.VMEM((nbat, SUBLANES, LANES), F32), pltpu.VMEM((nbat, SUBLANES, 2 * B_QK), F32)],
        compiler_params=_cparams("arbitrary", "arbitrary"),
        name="mlstm_prompt",
    )(z3, zb3, z3, z3, conv_w, b_gates_pad, g_norm)
    return out_b.reshape(bsz * seq, B_V), c1, n1, m1


SAMPLE_TILE = 8


def _row_to_col(row, n):
    eye = lax.broadcasted_iota(I32, (n, n), 0) == lax.broadcasted_iota(I32, (n, n), 1)
    return jnp.sum(jnp.where(eye, jnp.broadcast_to(row, (n, n)), 0.0), axis=1, keepdims=True)


def _mix_sample_body(tab_ref, sink_ref, bkt_ref, z_ref, ck_ref, cv_ref, c0_ref, n0_ref, m0_ref, cb_ref,
                     cw_ref, bg_ref, gn_ref,
                     att_ref, ob_ref, nk_ref, nv_ref, c1_ref, n1_ref, m1_ref, ncb_ref, *, tb):
    bk = bkt_ref[...]
    cw = cw_ref[...]
    gn = gn_ref[...]
    zero_half = jnp.zeros((1, A_HEAD_DIM), F32)
    bias_rows = []
    for h in range(A_HEADS):
        bias = jnp.zeros(bk.shape, F32)
        for t in range(REL_BUCKETS):
            bias = jnp.where(bk == t, tab_ref[t, h], bias)
        bias_rows.append(bias)
    bias_c = jnp.concatenate(bias_rows, axis=0)
    head_id = lax.broadcasted_iota(I32, (A_HEADS, 1), 0)
    bias_n = jnp.zeros((A_HEADS, 1), F32)
    sinks = jnp.zeros((A_HEADS, 1), F32)
    for h in range(A_HEADS):
        bias_n = jnp.where(head_id == h, tab_ref[0, h], bias_n)
        sinks = jnp.where(head_id == h, sink_ref[h], sinks)
    att_rows, ob_rows, m_rows = [], [], []
    stores = []
    for i in range(tb):
        zr = z_ref[i:i + 1, :]
        q_att = zr[:, Z_QA:Z_QA + A_Q] * (A_HEAD_DIM ** -0.5)
        ka = zr[:, Z_KA:Z_KA + A_KV]
        va = zr[:, Z_VA:Z_VA + A_KV]
        kc = ck_ref[i]
        vc = cv_ref[i]
        stores.append((nk_ref, i, jnp.concatenate([kc[1:], ka], axis=0)))
        stores.append((nv_ref, i, jnp.concatenate([vc[1:], va], axis=0)))
        q_rows = []
        for h in range(A_HEADS):
            qh = q_att[:, h * A_HEAD_DIM:(h + 1) * A_HEAD_DIM]
            q_rows.append(jnp.concatenate([qh, zero_half] if h // A_GROUP == 0 else [zero_half, qh], axis=1))
        qm = jnp.concatenate(q_rows, axis=0)
        lc = lax.dot_general(qm, kc, (((1,), (1,)), ((), ())), precision=HI,
                             preferred_element_type=F32) + bias_c
        ln = jnp.sum(qm * ka, axis=-1, keepdims=True) + bias_n
        mx = jnp.maximum(jnp.maximum(jnp.max(lc, axis=-1, keepdims=True), ln), sinks)
        pc = jnp.exp(lc - mx)
        pn = jnp.exp(ln - mx)
        den = jnp.sum(pc, axis=-1, keepdims=True) + pn + jnp.exp(sinks - mx)
        o = (jnp.dot(pc, vc, precision=HI, preferred_element_type=F32) + pn * va) / den
        att_rows.append(jnp.concatenate(
            [o[h:h + 1, (h // A_GROUP) * A_HEAD_DIM:(h // A_GROUP + 1) * A_HEAD_DIM] for h in range(A_HEADS)],
            axis=1))

        qk_pre = zr[:, Z_QK:Z_QK + 2 * B_QK]
        hist = cb_ref[i]
        conv = qk_pre * cw[B_CONV - 1:B_CONV]
        for j in range(B_CONV - 1):
            conv = conv + hist[j:j + 1] * cw[j:j + 1]
        stores.append((ncb_ref, i, jnp.concatenate([hist[1:], qk_pre], axis=0)))
        qk = conv * jax.nn.sigmoid(conv)
        G = zr[:, Z_GATES:Z_GATES + LANES] + bg_ref[...]
        lfr = _log_sigmoid(G)
        og = zr[:, Z_OG:Z_OG + B_V]
        v_pre = zr[:, Z_VB:Z_VB + B_V]
        obs, ms = [], []
        for h in range(B_HEADS):
            qh = qk[:, h * B_DK:(h + 1) * B_DK]
            kh = qk[:, B_QK + h * B_DK:B_QK + (h + 1) * B_DK] * (B_DK ** -0.5)
            vh = v_pre[:, h * B_DV:(h + 1) * B_DV]
            ig = G[:, h:h + 1]
            lf = lfr[:, B_HEADS + h:B_HEADS + h + 1]
            c0 = c0_ref[i, h]
            n0 = n0_ref[i, h:h + 1, :]
            m0 = m0_ref[i:i + 1, h:h + 1]
            a = lf + m0
            m = jnp.maximum(a, ig)
            dw = jnp.exp(ig - m)
            aw = jnp.exp(a - m)
            s = jnp.sum(qh * kh, axis=-1, keepdims=True) * dw
            q_col = _row_to_col(qh, B_DK)
            k_col = _row_to_col(kh, B_DK)
            num = s * vh + aw * jnp.sum(q_col * c0, axis=0, keepdims=True)
            den = s + aw * jnp.sum(qh * n0, axis=-1, keepdims=True)
            hh = num / jnp.maximum(jnp.abs(den), jnp.exp(-m))
            stores.append((c1_ref, (i, h), aw * c0 + dw * (k_col * vh)))
            stores.append((n1_ref, (i, slice(h, h + 1), slice(None)), aw * n0 + dw * kh))
            ms.append(m)
            hn = hh * lax.rsqrt(jnp.mean(hh * hh, axis=-1, keepdims=True) + EPS) * gn[:, h * B_DV:(h + 1) * B_DV]
            obs.append(jax.nn.sigmoid(og[:, h * B_DV:(h + 1) * B_DV]) * hn)
        ob_rows.append(jnp.concatenate(obs, axis=1))
        lane = lax.broadcasted_iota(I32, (1, LANES), 1)
        mrow = jnp.zeros((1, LANES), F32)
        for h in range(B_HEADS):
            mrow = jnp.where(lane == h, ms[h], mrow)
        m_rows.append(mrow)
    for ref, idx, val in stores:
        ref[idx] = val
    att_ref[...] = jnp.concatenate(att_rows, axis=0)
    ob_ref[...] = jnp.concatenate(ob_rows, axis=0)
    m1_ref[...] = jnp.concatenate(m_rows, axis=0)


def mix_sample(z, rel_table, sinks, ck, cv, c0, n0, m0, conv_buf, conv_w, b_gates_pad, g_norm):
    nb = z.shape[0]
    tb = min(SAMPLE_TILE, nb)
    n_buf = ck.shape[1]
    bkt = _t5_buckets(n_buf - np.arange(n_buf))[None, :]
    smem = pl.BlockSpec(memory_space=pltpu.SMEM)
    const = lambda shape: pl.BlockSpec(shape, lambda i: (0,) * len(shape))
    lead = lambda shape: pl.BlockSpec((tb,) + shape, lambda i: (i,) + (0,) * len(shape))
    return pl.pallas_call(
        functools.partial(_mix_sample_body, tb=tb),
        grid=(nb // tb,),
        in_specs=[smem, smem, const((1, n_buf)), lead((Z_COLS,)), lead((n_buf, A_KV)), lead((n_buf, A_KV)),
                  lead((B_HEADS, B_DK, B_DV)), lead((B_HEADS, B_DK)), lead((B_HEADS,)),
                  lead((B_CONV - 1, 2 * B_QK)), const((B_CONV, 2 * B_QK)), const((1, LANES)), const((1, B_V))],
        out_specs=[lead((A_Q,)), lead((B_V,)), lead((n_buf, A_KV)), lead((n_buf, A_KV)),
                   lead((B_HEADS, B_DK, B_DV)), lead((B_HEADS, B_DK)), lead((LANES,)),
                   lead((B_CONV - 1, 2 * B_QK))],
        out_shape=[jax.ShapeDtypeStruct((nb, A_Q), F32), jax.ShapeDtypeStruct((nb, B_V), F32),
                   jax.ShapeDtypeStruct(ck.shape, F32), jax.ShapeDtypeStruct(cv.shape, F32),
                   jax.ShapeDtypeStruct(c0.shape, F32), jax.ShapeDtypeStruct(n0.shape, F32),
                   jax.ShapeDtypeStruct((nb, LANES), F32), jax.ShapeDtypeStruct(conv_buf.shape, F32)],
        compiler_params=_cparams("parallel"),
        name="mix_sample",
    )(rel_table, sinks, jnp.asarray(bkt), z, ck, cv, c0, n0, m0, conv_buf, conv_w, b_gates_pad, g_norm)


def _store_row_tiles(ref, val, rows):
    for s in range(ROW_TILES):
        ref[pl.ds(s, rows, stride=ROW_TILES), :] = val[:, s * LANES:(s + 1) * LANES]


def _load_row_tiles(ref, rows, start=0, stride=ROW_TILES):
    return jnp.concatenate([ref[pl.ds(start + s, rows, stride=stride), :] for s in range(ROW_TILES)], axis=1)


def _route(logits):
    lane = lax.broadcasted_iota(I32, logits.shape, 1)
    big = jnp.int32(1 << 20)
    gl = jnp.where(lane < N_GROUPS, logits, NEG_INF)
    gmax = jnp.max(gl, axis=-1, keepdims=True)
    gidx = jnp.min(jnp.where(gl == gmax, lane, big), axis=-1, keepdims=True)
    g_gate = 1.0 / jnp.sum(jnp.exp(gl - gmax), axis=-1, keepdims=True)
    lo = N_GROUPS + gidx * EXPERTS_PER_GROUP
    el = jnp.where((lane >= lo) & (lane < lo + EXPERTS_PER_GROUP), logits, NEG_INF)
    v1 = jnp.max(el, axis=-1, keepdims=True)
    i1 = jnp.min(jnp.where(el == v1, lane, big), axis=-1, keepdims=True)
    el2 = jnp.where(lane == i1, NEG_INF, el)
    v2 = jnp.max(el2, axis=-1, keepdims=True)
    i2 = jnp.min(jnp.where(el2 == v2, lane, big), axis=-1, keepdims=True)
    t = jnp.exp(v2 - v1)
    w1 = g_gate / (1.0 + t)
    w2 = g_gate * t / (1.0 + t)
    gate = jnp.where(lane == 0, w1, jnp.where(lane == 1, w2, 0.0))
    return i1 - N_GROUPS, i2 - N_GROUPS, gate


def _local_sort(e0, e1):
    tm = e0.shape[0]
    lane = lax.broadcasted_iota(I32, (tm, LANES), 1)
    oh0 = (lane == e0).astype(BF16)
    oh1 = (lane == e1).astype(BF16)
    r = lax.broadcasted_iota(I32, (tm, tm), 0)
    c = lax.broadcasted_iota(I32, (tm, tm), 1)
    before = (c < r).astype(BF16)
    cnt0 = jnp.sum(oh0.astype(F32), axis=0, keepdims=True)
    run_n = cnt0 + jnp.sum(oh1.astype(F32), axis=0, keepdims=True)
    er = lax.broadcasted_iota(I32, (LANES, LANES), 0)
    ec = lax.broadcasted_iota(I32, (LANES, LANES), 1)
    run_l = jnp.dot(run_n, (er < ec).astype(F32), precision=HI, preferred_element_type=F32)
    w0 = jnp.dot(before, oh0, preferred_element_type=F32) + run_l
    w1 = jnp.dot(before, oh1, preferred_element_type=F32) + run_l + cnt0
    p0 = jnp.sum(jnp.where(lane == e0, w0, 0.0), axis=-1, keepdims=True)
    p1 = jnp.sum(jnp.where(lane == e1, w1, 0.0), axis=-1, keepdims=True)
    return jnp.where(lane == 0, p0, jnp.where(lane == 1, p1, 0.0)), run_n


MOE_CHUNK = 512


def _moe_chunk(rows, precise):
    return min(rows, 128 if precise else MOE_CHUNK)


def _proj_router_body(*refs, n_in, has_bias, precise, tm):
    a_refs = refs[:n_in]
    w_refs = refs[n_in:2 * n_in]
    k = 2 * n_in
    bias_ref = refs[k] if has_bias else None
    k += 1 if has_bias else 0
    x_ref, g_ref, wr_ref, br_ref, x1_ref, lpos_ref, gate_ref, runn_ref = refs[k:]
    acc = x_ref[...]
    if has_bias:
        acc = acc + bias_ref[...]
    for a_ref, w_ref in zip(a_refs, w_refs):
        acc = acc + _mm(a_ref[...], w_ref[...], precise)
    x1_ref[...] = acc
    h = _rms(acc, g_ref[...])
    wr = wr_ref[...]
    if precise:
        logits = jnp.dot(h, wr, precision=HI, preferred_element_type=F32)
    else:
        h_hi = h.astype(BF16)
        h_lo = (h - h_hi.astype(F32)).astype(BF16)
        w_hi = wr.astype(BF16)
        w_lo = (wr - w_hi.astype(F32)).astype(BF16)
        logits = (jnp.dot(h_hi, w_hi, preferred_element_type=F32) + jnp.dot(h_lo, w_hi, preferred_element_type=F32)
                  + jnp.dot(h_hi, w_lo, preferred_element_type=F32))
    e0, e1, gate = _route(logits + br_ref[...])
    lpos, run_n = _local_sort(e0, e1)
    lpos_ref[...] = (lpos.T[:SUBLANES] * ROW_TILES).astype(I32)
    gate_ref[...] = gate.T[:SUBLANES]
    runn_ref[...] = jnp.broadcast_to(run_n, runn_ref.shape).astype(I32)


def proj_router(a_list, w_list, bias, x, g, wr, br, *, precise):
    rows, d = x.shape
    tm = _moe_chunk(rows, precise)
    n_in = len(a_list)
    row_spec = lambda width: pl.BlockSpec((tm, width), lambda i: (i, 0))
    const = lambda shape: pl.BlockSpec(shape, lambda i: (0,) * len(shape))
    in_specs = [row_spec(a.shape[1]) for a in a_list] + [const(w.shape) for w in w_list]
    args = list(a_list) + list(w_list)
    if bias is not None:
        in_specs.append(const((1, d)))
        args.append(bias)
    in_specs += [row_spec(d), const((1, d)), const((d, LANES)), const((1, LANES))]
    args += [x, g, wr, br]
    return pl.pallas_call(
        functools.partial(_proj_router_body, n_in=n_in, has_bias=bias is not None, precise=precise, tm=tm),
        grid=(rows // tm,),
        in_specs=in_specs,
        out_specs=[row_spec(d),
                   pl.BlockSpec((SUBLANES, tm), lambda i: (i, 0)), pl.BlockSpec((SUBLANES, tm), lambda i: (i, 0)),
                   pl.BlockSpec((SUBLANES, LANES), lambda i: (i, 0))],
        out_shape=[jax.ShapeDtypeStruct((rows, d), F32),
                   jax.ShapeDtypeStruct((rows // tm * SUBLANES, tm), I32),
                   jax.ShapeDtypeStruct((rows // tm * SUBLANES, tm), F32),
                   jax.ShapeDtypeStruct((rows // tm * SUBLANES, LANES), I32)],
        compiler_params=_cparams("parallel"),
        name="proj_router",
    )(*args)


def _rows_at(offset):
    return pl.ds(pl.multiple_of(offset, ROW_TILES), ROW_TILES)


def _tile_rows(r, n=1):
    return pl.ds(pl.multiple_of(r * ROW_TILES, ROW_TILES), n * ROW_TILES)


def _pow2_pieces(limit):
    p = 1
    while p * 2 <= limit:
        p *= 2
    out = []
    while p >= 1:
        out.append(p)
        p //= 2
    return out


COMMON_PIECE = 32


def _for_each_piece(n, pieces, fn):
    def emit(ps):
        for p in ps:
            @pl.when((n & p) != 0)
            def _(p=p):
                fn(n & ~(2 * p - 1), p)

    big = [p for p in pieces if p > COMMON_PIECE]
    if big:
        @pl.when(n > 2 * COMMON_PIECE - 1)
        def _():
            emit(big)
    emit([p for p in pieces if p <= COMMON_PIECE])


def _dispatch_body(rn_ref, rg_ref, rl_ref, ps_ref, pn_ref, tail_ref, lpos_ref, x_ref, g_ref, xs_hbm,
                   hrow, stage, zbuf, sem, zsem, *, chunk, nch, blk):
    c = pl.program_id(0)
    slot = c % 2
    run_pieces = _pow2_pieces(chunk)
    pad_pieces = _pow2_pieces(blk - 1)

    @pl.when(c == 0)
    def _():
        zbuf[...] = jnp.zeros(zbuf.shape, F32)

        def pad_dmas(e, op):
            def one(off, p):
                cp = pltpu.make_async_copy(zbuf.at[_tile_rows(0, p)], xs_hbm.at[_tile_rows(ps_ref[e] + off, p)], zsem)
                cp.start() if op == 0 else cp.wait()
            _for_each_piece(pn_ref[e], pad_pieces, one)

        def issue(e, carry):
            pad_dmas(e, 0)
            return carry

        def wait(e, carry):
            pad_dmas(e, 1)
            return carry
        lax.fori_loop(0, N_EXPERTS, issue, 0)
        lax.fori_loop(0, N_EXPERTS, wait, 0)

        half = blk // 2

        def tail_dmas(i, op):
            cp = pltpu.make_async_copy(zbuf, xs_hbm.at[_tile_rows(tail_ref[0] + i * half, half)], zsem)
            cp.start() if op == 0 else cp.wait()

        def tail_issue(i, carry):
            tail_dmas(i, 0)
            return carry

        def tail_wait(i, carry):
            tail_dmas(i, 1)
            return carry
        lax.fori_loop(0, tail_ref[1], tail_issue, 0)
        lax.fori_loop(0, tail_ref[1], tail_wait, 0)

    _store_row_tiles(hrow, _rms(x_ref[...], g_ref[...]), chunk)

    def copy_tok(t, carry):
        row = hrow[_tile_rows(t), :]
        stage[slot, _rows_at(lpos_ref[0, 0, t]), :] = row
        stage[slot, _rows_at(lpos_ref[0, 0, chunk + t]), :] = row
        return carry
    lax.fori_loop(0, chunk, copy_tok, 0, unroll=8)

    @pl.when(c > 0)
    def _():
        pltpu.make_async_copy(stage.at[1 - slot], stage.at[1 - slot], sem).wait()

    def send_runs(e, carry):
        k = c * N_EXPERTS + e

        def one(off, p):
            pltpu.make_async_copy(stage.at[slot, _tile_rows(rl_ref[k] + off, p)],
                                  xs_hbm.at[_tile_rows(rg_ref[k] + off, p)], sem).start()
        _for_each_piece(rn_ref[k], run_pieces, one)
        return carry
    lax.fori_loop(0, N_EXPERTS, send_runs, 0)

    @pl.when(c == nch - 1)
    def _():
        pltpu.make_async_copy(stage.at[slot], stage.at[slot], sem).wait()


def dispatch(x, g, plan):
    chunk, nch, blk = plan['chunk'], plan['nch'], plan['blk']
    n_slots = plan['nblk'] * blk
    d = x.shape[1]
    gs = pltpu.PrefetchScalarGridSpec(
        num_scalar_prefetch=6,
        grid=(nch,),
        in_specs=[pl.BlockSpec((1, 1, TOP_K * chunk), lambda c, *_: (c, 0, 0), memory_space=pltpu.SMEM),
                  pl.BlockSpec((chunk, d), lambda c, *_: (c, 0)), pl.BlockSpec((1, d), lambda c, *_: (0, 0))],
        out_specs=pl.BlockSpec(memory_space=pl.ANY),
        scratch_shapes=[pltpu.VMEM((chunk * ROW_TILES, LANES), F32),
                        pltpu.VMEM((2, TOP_K * chunk * ROW_TILES, LANES), F32),
                        pltpu.VMEM((blk // 2 * ROW_TILES, LANES), F32),
                        pltpu.SemaphoreType.DMA(()), pltpu.SemaphoreType.DMA(())],
    )
    return pl.pallas_call(
        functools.partial(_dispatch_body, chunk=chunk, nch=nch, blk=blk),
        grid_spec=gs,
        out_shape=jax.ShapeDtypeStruct((n_slots * ROW_TILES, LANES), F32),
        compiler_params=_cparams("arbitrary"),
        name="dispatch",
    )(plan['run_n'], plan['run_g'], plan['run_l'], plan['pad_start'], plan['pad_n'], plan['tail'], plan['lpos'], x, g)


FF_CHUNK = 512


def _experts_body(be_ref, nv_ref, xs_ref, wg_ref, wu_ref, wd_ref, ys_ref, xb, *wcast, precise, blk):
    j = pl.program_id(0)
    nv = nv_ref[0]
    slot = j % 2
    k = j - 1

    @pl.when(j == 0)
    def _():
        xb[1] = jnp.zeros(xb.shape[1:], xb.dtype)

    if not precise:
        wgb, wub, wdb = wcast

        @pl.when((k == 0) | ((k > 0) & (k < nv) & (be_ref[jnp.maximum(k, 0)] != be_ref[jnp.maximum(k - 1, 0)])))
        def _():
            wgb[...] = wg_ref[0].astype(BF16)
            wub[...] = wu_ref[0].astype(BF16)
            wdb[...] = wd_ref[0].astype(BF16)

    @pl.when(j <= nv)
    def _():
        xb[slot] = _load_row_tiles(xs_ref, blk).astype(xb.dtype)
        if precise:
            wg, wu, wd = wg_ref[0], wu_ref[0], wd_ref[0]
        else:
            wg, wu, wd = wgb, wub, wdb
        xm = xb[1 - slot]
        gt = _mm(xm, wg[...], precise)
        up = _mm(xm, wu[...], precise)
        _store_row_tiles(ys_ref, _mm(gt * jax.nn.sigmoid(gt) * up, wd[...], precise), blk)

    @pl.when(j > nv)
    def _():
        ys_ref[...] = jnp.zeros(ys_ref.shape, F32)


def experts(xs, plan, wg, wu, wd, layer, *, precise):
    nblk, rows = plan['nblk'], plan['blk'] * ROW_TILES
    d, ff = wg.shape[2], wg.shape[3]
    blk = lambda j, be, nv: (jnp.minimum(j, nv[0] - 1), 0)
    wspec = lambda shape: pl.BlockSpec((None, 1) + shape,
                                       lambda j, be, nv: (layer, be[jnp.clip(j - 1, 0, nv[0] - 1)], 0, 0))
    gs = pltpu.PrefetchScalarGridSpec(
        num_scalar_prefetch=2,
        grid=(nblk + 1,),
        in_specs=[pl.BlockSpec((rows, LANES), blk),
                  wspec((d, ff)), wspec((d, ff)), wspec((ff, d))],
        out_specs=pl.BlockSpec((rows, LANES), lambda j, be, nv: (jnp.maximum(j - 1, 0), 0)),
        scratch_shapes=[pltpu.VMEM((2, plan['blk'], d), F32 if precise else BF16)]
        + ([] if precise else [pltpu.VMEM((d, ff), BF16), pltpu.VMEM((d, ff), BF16), pltpu.VMEM((ff, d), BF16)]),
    )
    return pl.pallas_call(
        functools.partial(_experts_body, precise=precise, blk=plan['blk']),
        grid_spec=gs,
        out_shape=jax.ShapeDtypeStruct(xs.shape, F32),
        compiler_params=_cparams("arbitrary"),
        name="experts",
    )(plan['block_e'], plan['n_used'], xs, wg, wu, wd)


def moe_plan(lpos8, gate8, runn8, blk):
    nch = runn8.shape[0] // SUBLANES
    chunk = lpos8.shape[1]
    n_assign = nch * chunk * TOP_K
    nblk = (n_assign + N_EXPERTS * (blk - 1)) // blk
    run_n = runn8.reshape(nch, SUBLANES, LANES)[:, 0, :N_EXPERTS]
    per_chunk = lambda a: a.reshape(nch, 1, SUBLANES * chunk)[:, :, :TOP_K * chunk]
    counts = jnp.sum(run_n, axis=0)
    padded = (counts + blk - 1) // blk * blk
    pends = jnp.cumsum(padded)
    pstarts = pends - padded
    run_g = pstarts[None, :] + jnp.cumsum(run_n, axis=0) - run_n
    run_l = jnp.cumsum(run_n, axis=1) - run_n
    blk_start = jnp.arange(nblk, dtype=I32) * blk
    block_e = jnp.minimum(jnp.sum((pends[None, :] <= blk_start[:, None]).astype(I32), axis=1), N_EXPERTS - 1)
    return dict(chunk=chunk, nch=nch, nblk=nblk, blk=blk,
                run_n=run_n.reshape(-1).astype(I32), run_g=run_g.reshape(-1).astype(I32),
                run_l=run_l.reshape(-1).astype(I32), pad_start=(pstarts + counts).astype(I32),
                pad_n=(padded - counts).astype(I32), lpos=per_chunk(lpos8), gate=per_chunk(gate8), block_e=block_e.astype(I32),
                n_used=(pends[-1:] // blk).astype(I32),
                tail=jnp.stack([pends[-1], 2 * (nblk - pends[-1] // blk)]).astype(I32))


def _combine(rn_ref, rg_ref, rl_ref, lpos_ref, gate_ref, x_ref, ys_hbm, ystage, comb, sem, *, chunk, nch):
    c = pl.program_id(0)
    slot = c % 2
    pieces = _pow2_pieces(chunk)

    def fetch(cc, sl):
        def per_e(e, carry):
            k = cc * N_EXPERTS + e

            def one(off, p):
                pltpu.make_async_copy(ys_hbm.at[_tile_rows(rg_ref[k] + off, p)],
                                      ystage.at[sl, _tile_rows(rl_ref[k] + off, p)], sem.at[sl]).start()
            _for_each_piece(rn_ref[k], pieces, one)
            return carry
        lax.fori_loop(0, N_EXPERTS, per_e, 0)

    @pl.when(c == 0)
    def _():
        fetch(0, 0)

    @pl.when(c + 1 < nch)
    def _():
        fetch(c + 1, 1 - slot)

    pltpu.make_async_copy(ystage.at[slot], ystage.at[slot], sem.at[slot]).wait()

    def per_tok(t, carry):
        y0 = ystage[slot, _rows_at(lpos_ref[0, 0, t]), :]
        y1 = ystage[slot, _rows_at(lpos_ref[0, 0, chunk + t]), :]
        comb[_tile_rows(t), :] = gate_ref[0, 0, t] * y0 + gate_ref[0, 0, chunk + t] * y1
        return carry
    lax.fori_loop(0, chunk, per_tok, 0, unroll=8)
    return x_ref[...] + _load_row_tiles(comb, chunk)


def _combine_glu_body(rn_ref, rg_ref, rl_ref, lpos_ref, gate_ref, x_ref, ys_hbm, g_ref, w_ref, b_ref,
                      x2_ref, u_ref, ystage, comb, sem, *, chunk, nch, precise):
    x2 = _combine(rn_ref, rg_ref, rl_ref, lpos_ref, gate_ref, x_ref, ys_hbm, ystage, comb, sem, chunk=chunk, nch=nch)
    x2_ref[...] = x2
    zz = _mm(_rms(x2, g_ref[...]), w_ref[...], precise) + b_ref[...]
    half = zz.shape[1] // 2
    u_ref[...] = zz[:, :half] * jax.nn.sigmoid(zz[:, half:])


def _combine_final_body(rn_ref, rg_ref, rl_ref, lpos_ref, gate_ref, x_ref, ys_hbm, g_ref, o_ref,
                        ystage, comb, sem, *, chunk, nch):
    x2 = _combine(rn_ref, rg_ref, rl_ref, lpos_ref, gate_ref, x_ref, ys_hbm, ystage, comb, sem, chunk=chunk, nch=nch)
    o_ref[...] = _rms(x2, g_ref[...])


def _combine_call(body, plan, x, ys, extra, extra_specs, out_specs, out_shape, name):
    chunk, nch = plan['chunk'], plan['nch']
    d = x.shape[1]
    smem_blk = pl.BlockSpec((1, 1, TOP_K * chunk), lambda c, *_: (c, 0, 0), memory_space=pltpu.SMEM)
    gs = pltpu.PrefetchScalarGridSpec(
        num_scalar_prefetch=3,
        grid=(nch,),
        in_specs=[smem_blk, smem_blk, pl.BlockSpec((chunk, d), lambda c, *_: (c, 0)),
                  pl.BlockSpec(memory_space=pl.ANY)] + extra_specs,
        out_specs=out_specs,
        scratch_shapes=[pltpu.VMEM((2, TOP_K * chunk * ROW_TILES, LANES), F32),
                        pltpu.VMEM((chunk * ROW_TILES, LANES), F32), pltpu.SemaphoreType.DMA((2,))],
    )
    return pl.pallas_call(
        functools.partial(body, chunk=chunk, nch=nch),
        grid_spec=gs,
        out_shape=out_shape,
        compiler_params=_cparams("arbitrary"),
        name=name,
    )(plan['run_n'], plan['run_g'], plan['run_l'], plan['lpos'], plan['gate'], x, ys, *extra)


def combine_glu(x, ys, plan, g, w, b, *, precise):
    rows, d = x.shape
    chunk = plan['chunk']
    cols = w.shape[1]
    const = lambda shape: pl.BlockSpec(shape, lambda c, *_: (0,) * len(shape))
    row_spec = lambda width: pl.BlockSpec((chunk, width), lambda c, *_: (c, 0))
    return _combine_call(
        functools.partial(_combine_glu_body, precise=precise), plan, x, ys, [g, w, b],
        [const((1, d)), const((d, cols)), const((1, cols))], [row_spec(d), row_spec(cols // 2)],
        [jax.ShapeDtypeStruct((rows, d), F32), jax.ShapeDtypeStruct((rows, cols // 2), F32)], "combine_glu")


def combine_final(x, ys, plan, g):
    rows, d = x.shape
    chunk = plan['chunk']
    return _combine_call(
        _combine_final_body, plan, x, ys, [g], [pl.BlockSpec((1, d), lambda c, *_: (0, 0))],
        pl.BlockSpec((chunk, d), lambda c, *_: (c, 0)), jax.ShapeDtypeStruct((rows, d), F32), "combine_final")


CONV_TILE = 256
CONV_HIST = 32


def _ln_swish(y, g, b):
    yc = y - jnp.mean(y, axis=-1, keepdims=True)
    yn = yc * lax.rsqrt(jnp.mean(yc * yc, axis=-1, keepdims=True) + EPS) * g + b
    return yn * jax.nn.sigmoid(yn)


CONV_ROWS = 64
LN_ROWS = 16
LN_UNROLL = 8


def _dwconv_prompt_body(u_ref, w_ref, bdw_ref, g_ref, b_ref, o_ref, ext, y_sc, *, tt):
    t = pl.program_id(1)
    n_lt = ext.shape[0]

    @pl.when(t == 0)
    def _():
        ext[:, 0:CONV_HIST, :] = jnp.zeros((n_lt, CONV_HIST, LANES), F32)

    @pl.when(t > 0)
    def _():
        ext[:, 0:CONV_HIST, :] = ext[:, tt:tt + CONV_HIST, :]

    for j in range(n_lt):
        ext[j, CONV_HIST:CONV_HIST + tt, :] = u_ref[:, j * LANES:(j + 1) * LANES]
    off = CONV_HIST - (C_KERNEL - 1)
    for j in range(n_lt):
        wj = w_ref[:, j * LANES:(j + 1) * LANES]
        bj = bdw_ref[:, j * LANES:(j + 1) * LANES]
        for c in range(tt // CONV_ROWS):
            acc = ext[j, pl.ds(off + c * CONV_ROWS, CONV_ROWS), :] * wj[0:1] + bj
            for k in range(1, C_KERNEL):
                acc = acc + ext[j, pl.ds(off + k + c * CONV_ROWS, CONV_ROWS), :] * wj[k:k + 1]
            y_sc[c * CONV_ROWS:(c + 1) * CONV_ROWS, j * LANES:(j + 1) * LANES] = acc

    def ln_rows(r, carry):
        rows = pl.ds(pl.multiple_of(r * LN_ROWS, LN_ROWS), LN_ROWS)
        o_ref[rows, :] = _ln_swish(y_sc[rows, :], g_ref[...], b_ref[...]).astype(o_ref.dtype)
        return carry
    lax.fori_loop(0, tt // LN_ROWS, ln_rows, 0, unroll=LN_UNROLL)


def dwconv_prompt(u, w, b_dw, ln_g, ln_b, bsz, seq):
    tt = min(CONV_TILE, seq)
    nt = seq // tt
    d = u.shape[1]
    const = lambda shape: pl.BlockSpec(shape, lambda b, t: (0,) * len(shape))
    return pl.pallas_call(
        functools.partial(_dwconv_prompt_body, tt=tt),
        grid=(bsz, nt),
        in_specs=[pl.BlockSpec((tt, d), lambda b, t: (b * nt + t, 0)), const((C_KERNEL, d)), const((1, d)),
                  const((1, d)), const((1, d))],
        out_specs=pl.BlockSpec((tt, d), lambda b, t: (b * nt + t, 0)),
        out_shape=jax.ShapeDtypeStruct((bsz * seq, d), BF16),
        scratch_shapes=[pltpu.VMEM((d // LANES, CONV_HIST + tt, LANES), F32), pltpu.VMEM((tt, d), F32)],
        compiler_params=_cparams("arbitrary", "arbitrary"),
        name="dwconv_prompt",
    )(u, w, b_dw, ln_g, ln_b)


def _dwconv_sample_body(u_ref, buf_ref, w_ref, bdw_ref, g_ref, b_ref, o_ref, nbuf_ref, *, tb):
    w = w_ref[...]
    rows = []
    for i in range(tb):
        hist = buf_ref[i]
        ur = u_ref[i:i + 1, :]
        rows.append(jnp.sum(hist * w[:C_KERNEL - 1], axis=0, keepdims=True) + ur * w[C_KERNEL - 1:C_KERNEL])
        nbuf_ref[i] = jnp.concatenate([hist[1:], ur], axis=0)
    y = jnp.concatenate(rows, axis=0) + bdw_ref[...]
    o_ref[...] = _ln_swish(y, g_ref[...], b_ref[...])


def dwconv_sample(u, buf, w, b_dw, ln_g, ln_b):
    nb, d = u.shape
    tb = min(SAMPLE_TILE, nb)
    const = lambda shape: pl.BlockSpec(shape, lambda i: (0,) * len(shape))
    return pl.pallas_call(
        functools.partial(_dwconv_sample_body, tb=tb),
        grid=(nb // tb,),
        in_specs=[pl.BlockSpec((tb, d), lambda i: (i, 0)), pl.BlockSpec((tb, C_KERNEL - 1, d), lambda i: (i, 0, 0)),
                  const((C_KERNEL, d)), const((1, d)), const((1, d)), const((1, d))],
        out_specs=[pl.BlockSpec((tb, d), lambda i: (i, 0)), pl.BlockSpec((tb, C_KERNEL - 1, d), lambda i: (i, 0, 0))],
        out_shape=[jax.ShapeDtypeStruct((nb, d), F32), jax.ShapeDtypeStruct(buf.shape, F32)],
        compiler_params=_cparams("parallel"),
        name="dwconv_sample",
    )(u, buf, w, b_dw, ln_g, ln_b)


def _moe(x, g, lpos8, gate8, runn8, wg, wu, wd, layer, *, precise):
    plan = moe_plan(lpos8, gate8, runn8, EXPERT_BLOCK_PRECISE if precise else EXPERT_BLOCK)
    xs = dispatch(x, g, plan)
    return experts(xs, plan, wg, wu, wd, layer, precise=precise), plan


def _trunk(x, caches, p, *, prompt):
    bsz, seq, d = x.shape
    rows = bsz * seq
    precise = not prompt
    wdt = F32 if precise else BF16
    xf = x.reshape(rows, d)
    row = lambda v: v.reshape(1, -1).astype(F32)

    z = norm_proj(xf, row(p['norm_mix'][0]), p['w_in'].astype(wdt), precise=precise)
    if prompt:
        z, zb = z
        att = attn_prompt(z, zb, p['rel_table'], p['sinks'], bsz, seq)
        out_b, c1, n1, m1 = mlstm_prompt(z, zb, p['conv_w'], p['b_gates'], p['g_mnorm'], bsz, seq)
        z3 = z.reshape(bsz, seq, P_COLS)
        new_k = z3[:, seq - WINDOW:, P_KA:P_KA + A_KV].reshape(bsz, WINDOW, A_KV_HEADS, A_HEAD_DIM)
        new_v = z3[:, seq - WINDOW:, P_VA:P_VA + A_KV].reshape(bsz, WINDOW, A_KV_HEADS, A_HEAD_DIM)
        new_conv = z3[:, seq - (B_CONV - 1):, P_QK:P_QK + 2 * B_QK]
        n1 = n1[:, :B_HEADS]
        m1 = m1[:, :B_HEADS, 0]
    else:
        ck, cv, c0, n0, m0, cbuf = caches[:6]
        n_buf = ck.shape[1]
        att, out_b, new_k, new_v, c1, n1, m1, new_conv = mix_sample(
            z, p['rel_table'], p['sinks'], ck.reshape(bsz, n_buf, A_KV), cv.reshape(bsz, n_buf, A_KV),
            c0, n0, m0, cbuf, p['conv_w'], p['b_gates'], p['g_mnorm'])
        new_k = new_k.reshape(bsz, n_buf, A_KV_HEADS, A_HEAD_DIM)
        new_v = new_v.reshape(bsz, n_buf, A_KV_HEADS, A_HEAD_DIM)
        m1 = m1[:, :B_HEADS]
    w_out = p['w_out'].astype(wdt)
    x1, lpos, gate, runn = proj_router([att, out_b], [w_out[:A_Q], w_out[A_Q:]], None, xf,
                                       row(p['norm_ffn'][0]), p['w_router'][0], p['b_router'][0], precise=precise)
    ys, plan = _moe(x1, row(p['norm_ffn'][0]), lpos, gate, runn, p['w_eg'], p['w_eu'], p['w_ed'], 0,
                    precise=precise)

    x2, u = combine_glu(x1, ys, plan, row(p['norm_mix'][1]), p['w_pw1'].astype(wdt), row(p['b_pw1']),
                        precise=precise)
    if prompt:
        yc = dwconv_prompt(u, p['w_dw'], row(p['b_dw']), row(p['ln_g']), row(p['ln_b']), bsz, seq)
        new_cbuf = u.reshape(bsz, seq, d)[:, seq - (C_KERNEL - 1):]
    else:
        yc, new_cbuf = dwconv_sample(u, caches[6], p['w_dw'], row(p['b_dw']), row(p['ln_g']), row(p['ln_b']))
    x3, lpos, gate, runn = proj_router([yc], [p['w_pw2'].astype(wdt)], row(p['b_pw2']), x2,
                                       row(p['norm_ffn'][1]), p['w_router'][1], p['b_router'][1], precise=precise)
    ys, plan = _moe(x3, row(p['norm_ffn'][1]), lpos, gate, runn, p['w_eg'], p['w_eu'], p['w_ed'], 1,
                    precise=precise)
    y = combine_final(x3, ys, plan, row(p['norm_final']))
    add_layer = lambda t: t[None]
    return (y.reshape(bsz, seq, d),) + tuple(add_layer(t) for t in (new_k, new_v, c1, n1, m1, new_conv, new_cbuf))


def kernel(x_prompt, x_sample, cache_win_k, cache_win_v, state_mlstm_c, state_mlstm_n, state_mlstm_m, state_mlstm_conv, state_conv, norm_mix, norm_ffn, norm_final, rel_bias_table, w_in_mix, b_mlstm_gates, w_mlstm_qk_conv, attn_sinks, g_mlstm_norm, w_out_mix, w_pw1, b_pw1, w_dw, b_dw, ln_conv_g, ln_conv_b, w_pw2, b_pw2, w_router_group, b_router_group, w_router_expert, b_router_expert, w_expert_gate, w_expert_up, w_expert_down):
    w_in = w_in_mix[0]
    s_q, s_k, s_v, s_qk, s_vb, s_g = A_Q, A_Q + A_KV, A_Q + 2 * A_KV, A_Q + 2 * A_KV + 2 * B_QK, \
        A_Q + 2 * A_KV + 2 * B_QK + B_V, A_Q + 2 * A_KV + 2 * B_QK + B_V + 2 * B_HEADS
    w_in_r = jnp.concatenate([w_in[:, :s_q], w_in[:, s_v:s_qk], w_in[:, s_qk:s_vb], w_in[:, s_g:],
                              w_in[:, s_q:s_k], w_in[:, s_k:s_v], w_in[:, s_vb:s_g],
                              jnp.zeros((D_MODEL, LANES - 2 * B_HEADS), F32)], axis=1)
    b_gates = jnp.concatenate([b_mlstm_gates[0], jnp.zeros((LANES - 2 * B_HEADS,), F32)]).reshape(1, LANES)
    depth = w_router_group.shape[0]
    w_re = jnp.transpose(w_router_expert, (0, 2, 1, 3)).reshape(depth, D_MODEL, N_EXPERTS)
    w_router = jnp.concatenate([w_router_group, w_re,
                                jnp.zeros((depth, D_MODEL, LANES - N_GROUPS - N_EXPERTS), F32)], axis=-1)
    b_router = jnp.concatenate([b_router_group, b_router_expert.reshape(depth, N_EXPERTS),
                                jnp.zeros((depth, LANES - N_GROUPS - N_EXPERTS), F32)], axis=-1)[:, None, :]
    p = dict(norm_mix=norm_mix, norm_ffn=norm_ffn, norm_final=norm_final, rel_table=rel_bias_table,
             sinks=attn_sinks[0], w_in=w_in_r, b_gates=b_gates, conv_w=w_mlstm_qk_conv[0],
             g_mnorm=g_mlstm_norm[0].reshape(1, B_V), w_out=w_out_mix[0], w_pw1=w_pw1[0], b_pw1=b_pw1[0],
             w_dw=w_dw[0], b_dw=b_dw[0], ln_g=ln_conv_g[0], ln_b=ln_conv_b[0], w_pw2=w_pw2[0], b_pw2=b_pw2[0],
             w_router=w_router, b_router=b_router, w_eg=w_expert_gate, w_eu=w_expert_up, w_ed=w_expert_down)
    caches = (cache_win_k[0], cache_win_v[0], state_mlstm_c[0], state_mlstm_n[0], state_mlstm_m[0],
              state_mlstm_conv[0], state_conv[0])
    out_p = _trunk(x_prompt, None, p, prompt=True)
    out_s = _trunk(x_sample, caches, p, prompt=False)
    return (out_p[0], out_s[0]) + out_p[1:] + out_s[1:]
```

```python
import functools
import math

import numpy as np
import jax
import jax.numpy as jnp
from jax import lax
from jax.experimental import pallas as pl
from jax.experimental.pallas import tpu as pltpu

F32 = jnp.float32
BF16 = jnp.bfloat16
I32 = jnp.int32
HI = lax.Precision.HIGHEST
NEG_INF = float("-inf")

LANES = 128
SUBLANES = 8
VMEM_LIMIT = 56 * 1024 * 1024

D_MODEL = 1024
A_HEADS = 8
A_KV_HEADS = 2
A_GROUP = A_HEADS // A_KV_HEADS
A_HEAD_DIM = 64
WINDOW = 128
REL_BUCKETS = 32
REL_MAX_DIST = 128
B_HEADS = 4
B_DK = 64
B_DV = 128
B_CONV = 4
C_KERNEL = 31
N_GROUPS = 4
EXPERTS_PER_GROUP = 8
N_EXPERTS = N_GROUPS * EXPERTS_PER_GROUP
TOP_K = 2
EXPERT_FF = D_MODEL // 2
EXPERT_BLOCK = 256
EXPERT_BLOCK_PRECISE = 128
EPS = 1e-6

A_Q = A_HEADS * A_HEAD_DIM
A_KV = A_KV_HEADS * A_HEAD_DIM
B_QK = B_HEADS * B_DK
B_V = B_HEADS * B_DV
ROW_TILES = D_MODEL // LANES

Z_QA, Z_QK, Z_VB, Z_OG, Z_KA, Z_VA, Z_GATES = 0, 512, 1024, 1536, 2048, 2176, 2304
Z_COLS = 2432
MLSTM_CHUNK = 128


def _cparams(*sem):
    return pltpu.CompilerParams(dimension_semantics=sem, vmem_limit_bytes=VMEM_LIMIT)


def _rms(x, g):
    return x * lax.rsqrt(jnp.mean(x * x, axis=-1, keepdims=True) + EPS) * g


def _mm(a, w, precise):
    if not precise:
        return jnp.dot(a.astype(BF16), w, preferred_element_type=F32)
    a = a.astype(F32)
    a_hi = a.astype(BF16)
    a_lo = (a - a_hi.astype(F32)).astype(BF16)
    w_hi = w.astype(BF16)
    w_lo = (w - w_hi.astype(F32)).astype(BF16)
    return (jnp.dot(a_hi, w_hi, preferred_element_type=F32) + jnp.dot(a_lo, w_hi, preferred_element_type=F32)
            + jnp.dot(a_hi, w_lo, preferred_element_type=F32))


def _t5_buckets(dist):
    exact = REL_BUCKETS // 2
    d = np.maximum(dist, 0)
    large = exact + (np.log(np.maximum(d, 1).astype(np.float32) / exact)
                     / math.log(REL_MAX_DIST / exact) * (REL_BUCKETS - exact)).astype(np.int32)
    return np.where(d < exact, d, np.minimum(large, REL_BUCKETS - 1)).astype(np.int32)


P_QK, P_OG, P_KA, P_VA, P_GATES = 0, 512, 1024, 1152, 1280
P_COLS = 1408
PB_QA, PB_VB = 0, 512
PB_COLS = 1024


def _norm_proj_body(x_ref, g_ref, w_ref, *o_refs, precise):
    h = _rms(x_ref[...], g_ref[...])
    z = _mm(h, w_ref[...], precise)
    if len(o_refs) == 1:
        o_refs[0][...] = z
        return
    z32_ref, zb_ref = o_refs
    z32_ref[...] = jnp.concatenate([z[:, Z_QK:Z_QK + 2 * B_QK], z[:, Z_OG:Z_OG + B_V], z[:, Z_KA:Z_GATES + LANES]],
                                   axis=1)
    zb_ref[...] = jnp.concatenate([z[:, Z_QA:Z_QA + A_Q], z[:, Z_VB:Z_VB + B_V]], axis=1).astype(BF16)


def norm_proj(x, g, w, *, precise):
    rows, d = x.shape
    cols = w.shape[1]
    tm = min(rows, 128 if precise else 512)
    row_spec = lambda width: pl.BlockSpec((tm, width), lambda i: (i, 0))
    if precise:
        out_specs, out_shape = row_spec(cols), jax.ShapeDtypeStruct((rows, cols), F32)
    else:
        out_specs = [row_spec(P_COLS), row_spec(PB_COLS)]
        out_shape = [jax.ShapeDtypeStruct((rows, P_COLS), F32), jax.ShapeDtypeStruct((rows, PB_COLS), BF16)]
    return pl.pallas_call(
        functools.partial(_norm_proj_body, precise=precise),
        grid=(rows // tm,),
        in_specs=[row_spec(d), pl.BlockSpec((1, d), lambda i: (0, 0)), pl.BlockSpec((d, cols), lambda i: (0, 0))],
        out_specs=out_specs,
        out_shape=out_shape,
        compiler_params=_cparams("parallel"),
        name="norm_proj",
    )(x, g, w)


def _attn_prompt_body(tab_ref, sink_ref, bkt_ref, q_ref, kp_ref, kc_ref, vp_ref, vc_ref, o_ref, bias_ref, *, nq):
    b = pl.program_id(0)
    n = pl.program_id(1)

    @pl.when((b == 0) & (n == 0))
    def _():
        bk = bkt_ref[...]
        first = lax.broadcasted_iota(I32, bk.shape, 1) >= WINDOW
        for h in range(A_HEADS):
            acc = jnp.full(bk.shape, NEG_INF, F32)
            for t in range(REL_BUCKETS):
                acc = jnp.where(bk == t, tab_ref[t, h], acc)
            bias_ref[h] = acc
            bias_ref[A_HEADS + h] = jnp.where(first, acc, NEG_INF)

    k_all = jnp.concatenate([kp_ref[...], kc_ref[...]], axis=0).astype(BF16)
    v_all = jnp.concatenate([vp_ref[...], vc_ref[...]], axis=0).astype(BF16)
    for sub in range(nq):
        rows = slice(sub * WINDOW, (sub + 1) * WINDOW)
        q = q_ref[rows, :] * (A_HEAD_DIM ** -0.5)
        kb = k_all[sub * WINDOW:(sub + 2) * WINDOW]
        vb = v_all[sub * WINDOW:(sub + 2) * WINDOW]
        table = jnp.where(n == 0, A_HEADS, 0) if sub == 0 else 0
        outs = []
        for h in range(A_HEADS):
            kvh = h // A_GROUP
            qh = q[:, h * A_HEAD_DIM:(h + 1) * A_HEAD_DIM].astype(BF16)
            kh = kb[:, kvh * A_HEAD_DIM:(kvh + 1) * A_HEAD_DIM]
            vh = vb[:, kvh * A_HEAD_DIM:(kvh + 1) * A_HEAD_DIM]
            s = lax.dot_general(qh, kh, (((1,), (1,)), ((), ())), preferred_element_type=F32)
            s = s + bias_ref[table + h]
            sink = sink_ref[h]
            mx = jnp.maximum(jnp.max(s, axis=-1, keepdims=True), sink)
            p = jnp.exp(s - mx)
            den = jnp.sum(p, axis=-1, keepdims=True) + jnp.exp(sink - mx)
            outs.append(jnp.dot(p.astype(BF16), vh, preferred_element_type=F32) / den)
        o_ref[rows, :] = jnp.concatenate(outs, axis=1).astype(o_ref.dtype)


ATTN_BLOCKS = 2


def attn_prompt(z, zb, rel_table, sinks, bsz, seq):
    nb = seq // WINDOW
    dist = WINDOW + np.arange(WINDOW)[:, None] - np.arange(2 * WINDOW)[None, :]
    bkt = np.where((dist >= 0) & (dist <= WINDOW), _t5_buckets(dist), -1).astype(np.int32)
    kcol, vcol = P_KA // LANES, P_VA // LANES
    smem = pl.BlockSpec(memory_space=pltpu.SMEM)
    nq = ATTN_BLOCKS if nb % ATTN_BLOCKS == 0 else 1
    ns = nb // nq

    def cur(c):
        return pl.BlockSpec((nq * WINDOW, LANES), lambda b, n: (b * ns + n, c))

    def prev(c):
        return pl.BlockSpec((WINDOW, LANES), lambda b, n: (b * nb + jnp.maximum(n * nq - 1, 0), c))

    return pl.pallas_call(
        functools.partial(_attn_prompt_body, nq=nq),
        grid=(bsz, ns),
        in_specs=[smem, smem,
                  pl.BlockSpec((WINDOW, 2 * WINDOW), lambda b, n: (0, 0)),
                  pl.BlockSpec((nq * WINDOW, A_Q), lambda b, n: (b * ns + n, PB_QA // A_Q)),
                  prev(kcol), cur(kcol), prev(vcol), cur(vcol)],
        out_specs=pl.BlockSpec((nq * WINDOW, A_Q), lambda b, n: (b * ns + n, 0)),
        out_shape=jax.ShapeDtypeStruct((bsz * seq, A_Q), BF16),
        scratch_shapes=[pltpu.VMEM((2 * A_HEADS, WINDOW, 2 * WINDOW), F32)],
        compiler_params=_cparams("arbitrary", "arbitrary"),
        name="attn_prompt",
    )(rel_table, sinks, jnp.asarray(bkt), zb, z, z, z, z)


def _log_sigmoid(x):
    return -(jnp.maximum(-x, 0.0) + jnp.log1p(jnp.exp(-jnp.abs(x))))


def _mlstm_prompt_body(qk_ref, v_ref, g_ref, og_ref, cw_ref, bg_ref, gn_ref,
                       ob_ref, c_out, n_out, m_out,
                       c_sc, n_sc, m_sc, hist_sc, *, L, nc, nbat):
    ci = pl.program_id(1)

    @pl.when(ci == 0)
    def _():
        c_sc[...] = jnp.zeros(c_sc.shape, F32)
        n_sc[...] = jnp.zeros(n_sc.shape, F32)
        m_sc[...] = jnp.full(m_sc.shape, NEG_INF, F32)
        hist_sc[...] = jnp.zeros(hist_sc.shape, F32)

    row = lax.broadcasted_iota(I32, (L, L), 0)
    colm = lax.broadcasted_iota(I32, (L, L), 1)
    causal = colm <= row
    tril = causal.astype(F32)
    stores = []
    for bb in range(nbat):
        stores += _mlstm_chunk(qk_ref.at[bb], v_ref.at[bb], g_ref.at[bb], og_ref.at[bb], cw_ref, bg_ref, gn_ref,
                               ob_ref.at[bb], c_sc.at[bb], n_sc.at[bb], m_sc.at[bb], hist_sc.at[bb],
                               causal, tril, L)
    for store in stores:
        store()

    @pl.when(ci == nc - 1)
    def _():
        c_out[...] = c_sc[...]
        n_out[...] = n_sc[...]
        m_out[...] = m_sc[...]


def _mlstm_chunk(qk_ref, v_ref, g_ref, og_ref, cw_ref, bg_ref, gn_ref, ob_ref, c_sc, n_sc, m_sc, hist_sc,
                 causal, tril, L):
    cur = qk_ref[...]
    ext = jnp.concatenate([hist_sc[...], cur], axis=0)
    cw = cw_ref[...]
    off = SUBLANES - (B_CONV - 1)
    conv = ext[off:off + L] * cw[0:1]
    for j in range(1, B_CONV):
        conv = conv + ext[off + j:off + j + L] * cw[j:j + 1]
    qk = conv * jax.nn.sigmoid(conv)
    q_all = qk[:, :B_QK]
    k_all = qk[:, B_QK:] * (B_DK ** -0.5)
    k_t = k_all.T
    v_all = v_ref[...]
    og = og_ref[...]
    gn = gn_ref[...]

    G = g_ref[...] + bg_ref[...]
    lf = _log_sigmoid(G)
    Bc = jnp.dot(tril, lf, precision=HI, preferred_element_type=F32)
    BT = Bc.T
    GT = G.T

    outs, stores = [], []
    for h in range(B_HEADS):
        qh = q_all[:, h * B_DK:(h + 1) * B_DK]
        kh = k_all[:, h * B_DK:(h + 1) * B_DK]
        vh = v_all[:, h * B_DV:(h + 1) * B_DV]
        b_col = Bc[:, B_HEADS + h:B_HEADS + h + 1]
        ig_col = G[:, h:h + 1]
        b_row = BT[B_HEADS + h:B_HEADS + h + 1, :]
        ig_row = GT[h:h + 1, :]
        c0 = c_sc[h]
        n0 = n_sc[h:h + 1, :]
        m0 = m_sc[h:h + 1, 0:1]
        a = b_col + m0
        d = jnp.where(causal, b_col - b_row + ig_row, NEG_INF)
        m = jnp.maximum(a, jnp.max(d, axis=-1, keepdims=True))
        dw = jnp.exp(d - m)
        aw = jnp.exp(a - m)
        qb = qh.astype(BF16)
        vb = vh.astype(BF16)
        s = lax.dot_general(qb, kh.astype(BF16), (((1,), (1,)), ((), ())), preferred_element_type=F32) * dw
        num = (jnp.dot(s.astype(BF16), vb, preferred_element_type=F32)
               + aw * jnp.dot(qb, c0.astype(BF16), preferred_element_type=F32))
        den = jnp.sum(s, axis=-1, keepdims=True) + aw * jnp.sum(qh * n0, axis=-1, keepdims=True)
        hh = num / jnp.maximum(jnp.abs(den), jnp.exp(-m))
        m_last = m[L - 1:L, :]
        wl = jnp.exp(b_col[L - 1:L, :] - b_col + ig_col - m_last)
        decay = aw[L - 1:L, :]
        kw_t = k_t[h * B_DK:(h + 1) * B_DK, :] * dw[L - 1:L, :]
        c1 = decay * c0 + jnp.dot(kw_t.astype(BF16), vb, preferred_element_type=F32)
        n1 = decay * n0 + jnp.sum(kh * wl, axis=0, keepdims=True)
        stores.append(functools.partial(_store_state, c_sc, n_sc, m_sc, h, c1, n1, m_last))
        hn = hh * lax.rsqrt(jnp.mean(hh * hh, axis=-1, keepdims=True) + EPS) * gn[:, h * B_DV:(h + 1) * B_DV]
        outs.append(jax.nn.sigmoid(og[:, h * B_DV:(h + 1) * B_DV]) * hn)
    out = jnp.concatenate(outs, axis=1).astype(ob_ref.dtype)
    stores.append(functools.partial(_store_chunk, ob_ref, hist_sc, out, cur[L - SUBLANES:L]))
    return stores


def _store_state(c_sc, n_sc, m_sc, h, c1, n1, m_last):
    c_sc[h] = c1
    n_sc[h:h + 1, :] = n1
    m_sc[h:h + 1, :] = jnp.broadcast_to(m_last, (1, LANES))


def _store_chunk(ob_ref, hist_sc, out, tail):
    ob_ref[...] = out
    hist_sc[...] = tail


MLSTM_BATCH = 1


def mlstm_prompt(z, zb, conv_w, b_gates_pad, g_norm, bsz, seq):
    L = MLSTM_CHUNK
    nc = seq // L
    nbat = MLSTM_BATCH if bsz % MLSTM_BATCH == 0 else 1
    z3 = z.reshape(bsz, seq, P_COLS)
    zb3 = zb.reshape(bsz, seq, PB_COLS)

    def zspec(width, colblk):
        return pl.BlockSpec((nbat, L, width), lambda b, c: (b, c, colblk))

    const = lambda shape: pl.BlockSpec(shape, lambda b, c: (0,) * len(shape))
    state = lambda shape: pl.BlockSpec((nbat,) + shape, lambda b, c: (b,) + (0,) * len(shape))
    out_b, c1, n1, m1 = pl.pallas_call(
        functools.partial(_mlstm_prompt_body, L=L, nc=nc, nbat=nbat),
        grid=(bsz // nbat, nc),
        in_specs=[zspec(2 * B_QK, P_QK // (2 * B_QK)), zspec(B_V, PB_VB // B_V), zspec(LANES, P_GATES // LANES),
                  zspec(B_V, P_OG // B_V), const((B_CONV, 2 * B_QK)), const((1, LANES)), const((1, B_V))],
        out_specs=[pl.BlockSpec((nbat, L, B_V), lambda b, c: (b, c, 0)),
                   state((B_HEADS, B_DK, B_DV)), state((SUBLANES, B_DK)), state((SUBLANES, LANES))],
        out_shape=[jax.ShapeDtypeStruct((bsz, seq, B_V), BF16),
                   jax.ShapeDtypeStruct((bsz, B_HEADS, B_DK, B_DV), F32),
                   jax.ShapeDtypeStruct((bsz, SUBLANES, B_DK), F32),
                   jax.ShapeDtypeStruct((bsz, SUBLANES, LANES), F32)],
        scratch_shapes=[pltpu.VMEM((nbat, B_HEADS, B_DK, B_DV), F32), pltpu.VMEM((nbat, SUBLANES, B_DK), F32),
                        pltpu.VMEM((nbat, SUBLANES, LANES), F32), pltpu.VMEM((nbat, SUBLANES, 2 * B_QK), F32)],
        compiler_params=_cparams("arbitrary", "arbitrary"),
        name="mlstm_prompt",
    )(z3, zb3, z3, z3, conv_w, b_gates_pad, g_norm)
    return out_b.reshape(bsz * seq, B_V), c1, n1, m1


SAMPLE_TILE = 8


def _row_to_col(row, n):
    eye = lax.broadcasted_iota(I32, (n, n), 0) == lax.broadcasted_iota(I32, (n, n), 1)
    return jnp.sum(jnp.where(eye, jnp.broadcast_to(row, (n, n)), 0.0), axis=1, keepdims=True)


def _mix_sample_body(tab_ref, sink_ref, bkt_ref, z_ref, ck_ref, cv_ref, c0_ref, n0_ref, m0_ref, cb_ref,
                     cw_ref, bg_ref, gn_ref,
                     att_ref, ob_ref, nk_ref, nv_ref, c1_ref, n1_ref, m1_ref, ncb_ref, *, tb):
    bk = bkt_ref[...]
    cw = cw_ref[...]
    gn = gn_ref[...]
    zero_half = jnp.zeros((1, A_HEAD_DIM), F32)
    bias_rows = []
    for h in range(A_HEADS):
        bias = jnp.zeros(bk.shape, F32)
        for t in range(REL_BUCKETS):
            bias = jnp.where(bk == t, tab_ref[t, h], bias)
        bias_rows.append(bias)
    bias_c = jnp.concatenate(bias_rows, axis=0)
    head_id = lax.broadcasted_iota(I32, (A_HEADS, 1), 0)
    bias_n = jnp.zeros((A_HEADS, 1), F32)
    sinks = jnp.zeros((A_HEADS, 1), F32)
    for h in range(A_HEADS):
        bias_n = jnp.where(head_id == h, tab_ref[0, h], bias_n)
        sinks = jnp.where(head_id == h, sink_ref[h], sinks)
    att_rows, ob_rows, m_rows = [], [], []
    stores = []
    for i in range(tb):
        zr = z_ref[i:i + 1, :]
        q_att = zr[:, Z_QA:Z_QA + A_Q] * (A_HEAD_DIM ** -0.5)
        ka = zr[:, Z_KA:Z_KA + A_KV]
        va = zr[:, Z_VA:Z_VA + A_KV]
        kc = ck_ref[i]
        vc = cv_ref[i]
        stores.append((nk_ref, i, jnp.concatenate([kc[1:], ka], axis=0)))
        stores.append((nv_ref, i, jnp.concatenate([vc[1:], va], axis=0)))
        q_rows = []
        for h in range(A_HEADS):
            qh = q_att[:, h * A_HEAD_DIM:(h + 1) * A_HEAD_DIM]
            q_rows.append(jnp.concatenate([qh, zero_half] if h // A_GROUP == 0 else [zero_half, qh], axis=1))
        qm = jnp.concatenate(q_rows, axis=0)
        lc = lax.dot_general(qm, kc, (((1,), (1,)), ((), ())), precision=HI,
                             preferred_element_type=F32) + bias_c
        ln = jnp.sum(qm * ka, axis=-1, keepdims=True) + bias_n
        mx = jnp.maximum(jnp.maximum(jnp.max(lc, axis=-1, keepdims=True), ln), sinks)
        pc = jnp.exp(lc - mx)
        pn = jnp.exp(ln - mx)
        den = jnp.sum(pc, axis=-1, keepdims=True) + pn + jnp.exp(sinks - mx)
        o = (jnp.dot(pc, vc, precision=HI, preferred_element_type=F32) + pn * va) / den
        att_rows.append(jnp.concatenate(
            [o[h:h + 1, (h // A_GROUP) * A_HEAD_DIM:(h // A_GROUP + 1) * A_HEAD_DIM] for h in range(A_HEADS)],
            axis=1))

        qk_pre = zr[:, Z_QK:Z_QK + 2 * B_QK]
        hist = cb_ref[i]
        conv = qk_pre * cw[B_CONV - 1:B_CONV]
        for j in range(B_CONV - 1):
            conv = conv + hist[j:j + 1] * cw[j:j + 1]
        stores.append((ncb_ref, i, jnp.concatenate([hist[1:], qk_pre], axis=0)))
        qk = conv * jax.nn.sigmoid(conv)
        G = zr[:, Z_GATES:Z_GATES + LANES] + bg_ref[...]
        lfr = _log_sigmoid(G)
        og = zr[:, Z_OG:Z_OG + B_V]
        v_pre = zr[:, Z_VB:Z_VB + B_V]
        obs, ms = [], []
        for h in range(B_HEADS):
            qh = qk[:, h * B_DK:(h + 1) * B_DK]
            kh = qk[:, B_QK + h * B_DK:B_QK + (h + 1) * B_DK] * (B_DK ** -0.5)
            vh = v_pre[:, h * B_DV:(h + 1) * B_DV]
            ig = G[:, h:h + 1]
            lf = lfr[:, B_HEADS + h:B_HEADS + h + 1]
            c0 = c0_ref[i, h]
            n0 = n0_ref[i, h:h + 1, :]
            m0 = m0_ref[i:i + 1, h:h + 1]
            a = lf + m0
            m = jnp.maximum(a, ig)
            dw = jnp.exp(ig - m)
            aw = jnp.exp(a - m)
            s = jnp.sum(qh * kh, axis=-1, keepdims=True) * dw
            q_col = _row_to_col(qh, B_DK)
            k_col = _row_to_col(kh, B_DK)
            num = s * vh + aw * jnp.sum(q_col * c0, axis=0, keepdims=True)
            den = s + aw * jnp.sum(qh * n0, axis=-1, keepdims=True)
            hh = num / jnp.maximum(jnp.abs(den), jnp.exp(-m))
            stores.append((c1_ref, (i, h), aw * c0 + dw * (k_col * vh)))
            stores.append((n1_ref, (i, slice(h, h + 1), slice(None)), aw * n0 + dw * kh))
            ms.append(m)
            hn = hh * lax.rsqrt(jnp.mean(hh * hh, axis=-1, keepdims=True) + EPS) * gn[:, h * B_DV:(h + 1) * B_DV]
            obs.append(jax.nn.sigmoid(og[:, h * B_DV:(h + 1) * B_DV]) * hn)
        ob_rows.append(jnp.concatenate(obs, axis=1))
        lane = lax.broadcasted_iota(I32, (1, LANES), 1)
        mrow = jnp.zeros((1, LANES), F32)
        for h in range(B_HEADS):
            mrow = jnp.where(lane == h, ms[h], mrow)
        m_rows.append(mrow)
    for ref, idx, val in stores:
        ref[idx] = val
    att_ref[...] = jnp.concatenate(att_rows, axis=0)
    ob_ref[...] = jnp.concatenate(ob_rows, axis=0)
    m1_ref[...] = jnp.concatenate(m_rows, axis=0)


def mix_sample(z, rel_table, sinks, ck, cv, c0, n0, m0, conv_buf, conv_w, b_gates_pad, g_norm):
    nb = z.shape[0]
    tb = min(SAMPLE_TILE, nb)
    n_buf = ck.shape[1]
    bkt = _t5_buckets(n_buf - np.arange(n_buf))[None, :]
    smem = pl.BlockSpec(memory_space=pltpu.SMEM)
    const = lambda shape: pl.BlockSpec(shape, lambda i: (0,) * len(shape))
    lead = lambda shape: pl.BlockSpec((tb,) + shape, lambda i: (i,) + (0,) * len(shape))
    return pl.pallas_call(
        functools.partial(_mix_sample_body, tb=tb),
        grid=(nb // tb,),
        in_specs=[smem, smem, const((1, n_buf)), lead((Z_COLS,)), lead((n_buf, A_KV)), lead((n_buf, A_KV)),
                  lead((B_HEADS, B_DK, B_DV)), lead((B_HEADS, B_DK)), lead((B_HEADS,)),
                  lead((B_CONV - 1, 2 * B_QK)), const((B_CONV, 2 * B_QK)), const((1, LANES)), const((1, B_V))],
        out_specs=[lead((A_Q,)), lead((B_V,)), lead((n_buf, A_KV)), lead((n_buf, A_KV)),
                   lead((B_HEADS, B_DK, B_DV)), lead((B_HEADS, B_DK)), lead((LANES,)),
                   lead((B_CONV - 1, 2 * B_QK))],
        out_shape=[jax.ShapeDtypeStruct((nb, A_Q), F32), jax.ShapeDtypeStruct((nb, B_V), F32),
                   jax.ShapeDtypeStruct(ck.shape, F32), jax.ShapeDtypeStruct(cv.shape, F32),
                   jax.ShapeDtypeStruct(c0.shape, F32), jax.ShapeDtypeStruct(n0.shape, F32),
                   jax.ShapeDtypeStruct((nb, LANES), F32), jax.ShapeDtypeStruct(conv_buf.shape, F32)],
        compiler_params=_cparams("parallel"),
        name="mix_sample",
    )(rel_table, sinks, jnp.asarray(bkt), z, ck, cv, c0, n0, m0, conv_buf, conv_w, b_gates_pad, g_norm)


def _store_row_tiles(ref, val, rows):
    for s in range(ROW_TILES):
        ref[pl.ds(s, rows, stride=ROW_TILES), :] = val[:, s * LANES:(s + 1) * LANES]


def _load_row_tiles(ref, rows, start=0, stride=ROW_TILES):
    return jnp.concatenate([ref[pl.ds(start + s, rows, stride=stride), :] for s in range(ROW_TILES)], axis=1)


def _route(logits):
    lane = lax.broadcasted_iota(I32, logits.shape, 1)
    big = jnp.int32(1 << 20)
    gl = jnp.where(lane < N_GROUPS, logits, NEG_INF)
    gmax = jnp.max(gl, axis=-1, keepdims=True)
    gidx = jnp.min(jnp.where(gl == gmax, lane, big), axis=-1, keepdims=True)
    g_gate = 1.0 / jnp.sum(jnp.exp(gl - gmax), axis=-1, keepdims=True)
    lo = N_GROUPS + gidx * EXPERTS_PER_GROUP
    el = jnp.where((lane >= lo) & (lane < lo + EXPERTS_PER_GROUP), logits, NEG_INF)
    v1 = jnp.max(el, axis=-1, keepdims=True)
    i1 = jnp.min(jnp.where(el == v1, lane, big), axis=-1, keepdims=True)
    el2 = jnp.where(lane == i1, NEG_INF, el)
    v2 = jnp.max(el2, axis=-1, keepdims=True)
    i2 = jnp.min(jnp.where(el2 == v2, lane, big), axis=-1, keepdims=True)
    t = jnp.exp(v2 - v1)
    w1 = g_gate / (1.0 + t)
    w2 = g_gate * t / (1.0 + t)
    gate = jnp.where(lane == 0, w1, jnp.where(lane == 1, w2, 0.0))
    return i1 - N_GROUPS, i2 - N_GROUPS, gate


def _local_sort(e0, e1):
    tm = e0.shape[0]
    lane = lax.broadcasted_iota(I32, (tm, LANES), 1)
    oh0 = (lane == e0).astype(BF16)
    oh1 = (lane == e1).astype(BF16)
    r = lax.broadcasted_iota(I32, (tm, tm), 0)
    c = lax.broadcasted_iota(I32, (tm, tm), 1)
    before = (c < r).astype(BF16)
    cnt0 = jnp.sum(oh0.astype(F32), axis=0, keepdims=True)
    run_n = cnt0 + jnp.sum(oh1.astype(F32), axis=0, keepdims=True)
    er = lax.broadcasted_iota(I32, (LANES, LANES), 0)
    ec = lax.broadcasted_iota(I32, (LANES, LANES), 1)
    run_l = jnp.dot(run_n, (er < ec).astype(F32), precision=HI, preferred_element_type=F32)
    w0 = jnp.dot(before, oh0, preferred_element_type=F32) + run_l
    w1 = jnp.dot(before, oh1, preferred_element_type=F32) + run_l + cnt0
    p0 = jnp.sum(jnp.where(lane == e0, w0, 0.0), axis=-1, keepdims=True)
    p1 = jnp.sum(jnp.where(lane == e1, w1, 0.0), axis=-1, keepdims=True)
    return jnp.where(lane == 0, p0, jnp.where(lane == 1, p1, 0.0)), run_n


MOE_CHUNK = 512


def _moe_chunk(rows, precise):
    return min(rows, 128 if precise else MOE_CHUNK)


def _proj_router_body(*refs, n_in, has_bias, precise, tm):
    a_refs = refs[:n_in]
    w_refs = refs[n_in:2 * n_in]
    k = 2 * n_in
    bias_ref = refs[k] if has_bias else None
    k += 1 if has_bias else 0
    x_ref, g_ref, wr_ref, br_ref, x1_ref, h8_ref, lpos_ref, gate_ref, runn_ref = refs[k:]
    acc = x_ref[...]
    if has_bias:
        acc = acc + bias_ref[...]
    for a_ref, w_ref in zip(a_refs, w_refs):
        acc = acc + _mm(a_ref[...], w_ref[...], precise)
    x1_ref[...] = acc
    h = _rms(acc, g_ref[...])
    _store_row_tiles(h8_ref, h, tm)
    wr = wr_ref[...]
    if precise:
        logits = jnp.dot(h, wr, precision=HI, preferred_element_type=F32)
    else:
        h_hi = h.astype(BF16)
        h_lo = (h - h_hi.astype(F32)).astype(BF16)
        w_hi = wr.astype(BF16)
        w_lo = (wr - w_hi.astype(F32)).astype(BF16)
        logits = (jnp.dot(h_hi, w_hi, preferred_element_type=F32) + jnp.dot(h_lo, w_hi, preferred_element_type=F32)
                  + jnp.dot(h_hi, w_lo, preferred_element_type=F32))
    e0, e1, gate = _route(logits + br_ref[...])
    lpos, run_n = _local_sort(e0, e1)
    lpos_ref[...] = (lpos.T[:SUBLANES] * ROW_TILES).astype(I32)
    gate_ref[...] = gate.T[:SUBLANES]
    runn_ref[...] = jnp.broadcast_to(run_n, runn_ref.shape).astype(I32)


def proj_router(a_list, w_list, bias, x, g, wr, br, *, precise):
    rows, d = x.shape
    tm = _moe_chunk(rows, precise)
    n_in = len(a_list)
    row_spec = lambda width: pl.BlockSpec((tm, width), lambda i: (i, 0))
    const = lambda shape: pl.BlockSpec(shape, lambda i: (0,) * len(shape))
    in_specs = [row_spec(a.shape[1]) for a in a_list] + [const(w.shape) for w in w_list]
    args = list(a_list) + list(w_list)
    if bias is not None:
        in_specs.append(const((1, d)))
        args.append(bias)
    in_specs += [row_spec(d), const((1, d)), const((d, LANES)), const((1, LANES))]
    args += [x, g, wr, br]
    return pl.pallas_call(
        functools.partial(_proj_router_body, n_in=n_in, has_bias=bias is not None, precise=precise, tm=tm),
        grid=(rows // tm,),
        in_specs=in_specs,
        out_specs=[row_spec(d), pl.BlockSpec((tm * ROW_TILES, LANES), lambda i: (i, 0)),
                   pl.BlockSpec((SUBLANES, tm), lambda i: (i, 0)), pl.BlockSpec((SUBLANES, tm), lambda i: (i, 0)),
                   pl.BlockSpec((SUBLANES, LANES), lambda i: (i, 0))],
        out_shape=[jax.ShapeDtypeStruct((rows, d), F32), jax.ShapeDtypeStruct((rows * ROW_TILES, LANES), F32),
                   jax.ShapeDtypeStruct((rows // tm * SUBLANES, tm), I32),
                   jax.ShapeDtypeStruct((rows // tm * SUBLANES, tm), F32),
                   jax.ShapeDtypeStruct((rows // tm * SUBLANES, LANES), I32)],
        compiler_params=_cparams("parallel"),
        name="proj_router",
    )(*args)


def _rows_at(offset):
    return pl.ds(pl.multiple_of(offset, ROW_TILES), ROW_TILES)


def _tile_rows(r, n=1):
    return pl.ds(pl.multiple_of(r * ROW_TILES, ROW_TILES), n * ROW_TILES)


def _pow2_pieces(limit):
    p = 1
    while p * 2 <= limit:
        p *= 2
    out = []
    while p >= 1:
        out.append(p)
        p //= 2
    return out


COMMON_PIECE = 32


def _for_each_piece(n, pieces, fn):
    def emit(ps):
        for p in ps:
            @pl.when((n & p) != 0)
            def _(p=p):
                fn(n & ~(2 * p - 1), p)

    big = [p for p in pieces if p > COMMON_PIECE]
    if big:
        @pl.when(n > 2 * COMMON_PIECE - 1)
        def _():
            emit(big)
    emit([p for p in pieces if p <= COMMON_PIECE])


def _dispatch_body(rn_ref, rg_ref, rl_ref, ps_ref, pn_ref, tail_ref, lpos_ref, h8_ref, xs_hbm,
                   stage, zbuf, sem, zsem, *, chunk, nch, blk):
    c = pl.program_id(0)
    slot = c % 2
    run_pieces = _pow2_pieces(chunk)
    pad_pieces = _pow2_pieces(blk - 1)

    @pl.when(c == 0)
    def _():
        zbuf[...] = jnp.zeros(zbuf.shape, F32)

        def pad_dmas(e, op):
            def one(off, p):
                cp = pltpu.make_async_copy(zbuf.at[_tile_rows(0, p)], xs_hbm.at[_tile_rows(ps_ref[e] + off, p)], zsem)
                cp.start() if op == 0 else cp.wait()
            _for_each_piece(pn_ref[e], pad_pieces, one)

        def issue(e, carry):
            pad_dmas(e, 0)
            return carry

        def wait(e, carry):
            pad_dmas(e, 1)
            return carry
        lax.fori_loop(0, N_EXPERTS, issue, 0)
        lax.fori_loop(0, N_EXPERTS, wait, 0)

        half = blk // 2

        def tail_dmas(i, op):
            cp = pltpu.make_async_copy(zbuf, xs_hbm.at[_tile_rows(tail_ref[0] + i * half, half)], zsem)
            cp.start() if op == 0 else cp.wait()

        def tail_issue(i, carry):
            tail_dmas(i, 0)
            return carry

        def tail_wait(i, carry):
            tail_dmas(i, 1)
            return carry
        lax.fori_loop(0, tail_ref[1], tail_issue, 0)
        lax.fori_loop(0, tail_ref[1], tail_wait, 0)

    def copy_tok(t, carry):
        row = h8_ref[_tile_rows(t), :]
        stage[slot, _rows_at(lpos_ref[0, 0, t]), :] = row
        stage[slot, _rows_at(lpos_ref[0, 0, chunk + t]), :] = row
        return carry
    lax.fori_loop(0, chunk, copy_tok, 0, unroll=8)

    @pl.when(c > 0)
    def _():
        pltpu.make_async_copy(stage.at[1 - slot], stage.at[1 - slot], sem).wait()

    def send_runs(e, carry):
        k = c * N_EXPERTS + e

        def one(off, p):
            pltpu.make_async_copy(stage.at[slot, _tile_rows(rl_ref[k] + off, p)],
                                  xs_hbm.at[_tile_rows(rg_ref[k] + off, p)], sem).start()
        _for_each_piece(rn_ref[k], run_pieces, one)
        return carry
    lax.fori_loop(0, N_EXPERTS, send_runs, 0)

    @pl.when(c == nch - 1)
    def _():
        pltpu.make_async_copy(stage.at[slot], stage.at[slot], sem).wait()


def dispatch(h8, plan):
    chunk, nch, blk = plan['chunk'], plan['nch'], plan['blk']
    n_slots = plan['nblk'] * blk
    gs = pltpu.PrefetchScalarGridSpec(
        num_scalar_prefetch=6,
        grid=(nch,),
        in_specs=[pl.BlockSpec((1, 1, TOP_K * chunk), lambda c, *_: (c, 0, 0), memory_space=pltpu.SMEM),
                  pl.BlockSpec((chunk * ROW_TILES, LANES), lambda c, *_: (c, 0))],
        out_specs=pl.BlockSpec(memory_space=pl.ANY),
        scratch_shapes=[pltpu.VMEM((2, TOP_K * chunk * ROW_TILES, LANES), F32),
                        pltpu.VMEM((blk // 2 * ROW_TILES, LANES), F32),
                        pltpu.SemaphoreType.DMA(()), pltpu.SemaphoreType.DMA(())],
    )
    return pl.pallas_call(
        functools.partial(_dispatch_body, chunk=chunk, nch=nch, blk=blk),
        grid_spec=gs,
        out_shape=jax.ShapeDtypeStruct((n_slots * ROW_TILES, LANES), F32),
        compiler_params=_cparams("arbitrary"),
        name="dispatch",
    )(plan['run_n'], plan['run_g'], plan['run_l'], plan['pad_start'], plan['pad_n'], plan['tail'], plan['lpos'], h8)


FF_CHUNK = 512


def _experts_body(be_ref, nv_ref, xs_ref, wg_ref, wu_ref, wd_ref, ys_ref, xb, *wcast, precise, blk):
    j = pl.program_id(0)
    nv = nv_ref[0]
    slot = j % 2
    k = j - 1

    @pl.when(j == 0)
    def _():
        xb[1] = jnp.zeros(xb.shape[1:], xb.dtype)

    if not precise:
        wgb, wub, wdb = wcast

        @pl.when((k == 0) | ((k > 0) & (k < nv) & (be_ref[jnp.maximum(k, 0)] != be_ref[jnp.maximum(k - 1, 0)])))
        def _():
            wgb[...] = wg_ref[0].astype(BF16)
            wub[...] = wu_ref[0].astype(BF16)
            wdb[...] = wd_ref[0].astype(BF16)

    @pl.when(j <= nv)
    def _():
        xb[slot] = _load_row_tiles(xs_ref, blk).astype(xb.dtype)
        if precise:
            wg, wu, wd = wg_ref[0], wu_ref[0], wd_ref[0]
        else:
            wg, wu, wd = wgb, wub, wdb
        xm = xb[1 - slot]
        gt = _mm(xm, wg[...], precise)
        up = _mm(xm, wu[...], precise)
        _store_row_tiles(ys_ref, _mm(gt * jax.nn.sigmoid(gt) * up, wd[...], precise), blk)

    @pl.when(j > nv)
    def _():
        ys_ref[...] = jnp.zeros(ys_ref.shape, F32)


def experts(xs, plan, wg, wu, wd, layer, *, precise):
    nblk, rows = plan['nblk'], plan['blk'] * ROW_TILES
    d, ff = wg.shape[2], wg.shape[3]
    blk = lambda j, be, nv: (jnp.minimum(j, nv[0] - 1), 0)
    wspec = lambda shape: pl.BlockSpec((None, 1) + shape,
                                       lambda j, be, nv: (layer, be[jnp.clip(j - 1, 0, nv[0] - 1)], 0, 0))
    gs = pltpu.PrefetchScalarGridSpec(
        num_scalar_prefetch=2,
        grid=(nblk + 1,),
        in_specs=[pl.BlockSpec((rows, LANES), blk),
                  wspec((d, ff)), wspec((d, ff)), wspec((ff, d))],
        out_specs=pl.BlockSpec((rows, LANES), lambda j, be, nv: (jnp.maximum(j - 1, 0), 0)),
        scratch_shapes=[pltpu.VMEM((2, plan['blk'], d), F32 if precise else BF16)]
        + ([] if precise else [pltpu.VMEM((d, ff), BF16), pltpu.VMEM((d, ff), BF16), pltpu.VMEM((ff, d), BF16)]),
    )
    return pl.pallas_call(
        functools.partial(_experts_body, precise=precise, blk=plan['blk']),
        grid_spec=gs,
        out_shape=jax.ShapeDtypeStruct(xs.shape, F32),
        compiler_params=_cparams("arbitrary"),
        name="experts",
    )(plan['block_e'], plan['n_used'], xs, wg, wu, wd)


def moe_plan(lpos8, gate8, runn8, blk):
    nch = runn8.shape[0] // SUBLANES
    chunk = lpos8.shape[1]
    n_assign = nch * chunk * TOP_K
    nblk = (n_assign + N_EXPERTS * (blk - 1)) // blk
    run_n = runn8.reshape(nch, SUBLANES, LANES)[:, 0, :N_EXPERTS]
    per_chunk = lambda a: a.reshape(nch, 1, SUBLANES * chunk)[:, :, :TOP_K * chunk]
    counts = jnp.sum(run_n, axis=0)
    padded = (counts + blk - 1) // blk * blk
    pends = jnp.cumsum(padded)
    pstarts = pends - padded
    run_g = pstarts[None, :] + jnp.cumsum(run_n, axis=0) - run_n
    run_l = jnp.cumsum(run_n, axis=1) - run_n
    blk_start = jnp.arange(nblk, dtype=I32) * blk
    block_e = jnp.minimum(jnp.sum((pends[None, :] <= blk_start[:, None]).astype(I32), axis=1), N_EXPERTS - 1)
    return dict(chunk=chunk, nch=nch, nblk=nblk, blk=blk,
                run_n=run_n.reshape(-1).astype(I32), run_g=run_g.reshape(-1).astype(I32),
                run_l=run_l.reshape(-1).astype(I32), pad_start=(pstarts + counts).astype(I32),
                pad_n=(padded - counts).astype(I32), lpos=per_chunk(lpos8), gate=per_chunk(gate8), block_e=block_e.astype(I32),
                n_used=(pends[-1:] // blk).astype(I32),
                tail=jnp.stack([pends[-1], 2 * (nblk - pends[-1] // blk)]).astype(I32))


def _combine(rn_ref, rg_ref, rl_ref, lpos_ref, gate_ref, x_ref, ys_hbm, ystage, comb, sem, *, chunk, nch):
    c = pl.program_id(0)
    slot = c % 2
    pieces = _pow2_pieces(chunk)

    def fetch(cc, sl):
        def per_e(e, carry):
            k = cc * N_EXPERTS + e

            def one(off, p):
                pltpu.make_async_copy(ys_hbm.at[_tile_rows(rg_ref[k] + off, p)],
                                      ystage.at[sl, _tile_rows(rl_ref[k] + off, p)], sem.at[sl]).start()
            _for_each_piece(rn_ref[k], pieces, one)
            return carry
        lax.fori_loop(0, N_EXPERTS, per_e, 0)

    @pl.when(c == 0)
    def _():
        fetch(0, 0)

    @pl.when(c + 1 < nch)
    def _():
        fetch(c + 1, 1 - slot)

    pltpu.make_async_copy(ystage.at[slot], ystage.at[slot], sem.at[slot]).wait()

    def per_tok(t, carry):
        y0 = ystage[slot, _rows_at(lpos_ref[0, 0, t]), :]
        y1 = ystage[slot, _rows_at(lpos_ref[0, 0, chunk + t]), :]
        comb[_tile_rows(t), :] = gate_ref[0, 0, t] * y0 + gate_ref[0, 0, chunk + t] * y1
        return carry
    lax.fori_loop(0, chunk, per_tok, 0, unroll=8)
    return x_ref[...] + _load_row_tiles(comb, chunk)


def _combine_glu_body(rn_ref, rg_ref, rl_ref, lpos_ref, gate_ref, x_ref, ys_hbm, g_ref, w_ref, b_ref,
                      x2_ref, u_ref, ystage, comb, sem, *, chunk, nch, precise):
    x2 = _combine(rn_ref, rg_ref, rl_ref, lpos_ref, gate_ref, x_ref, ys_hbm, ystage, comb, sem, chunk=chunk, nch=nch)
    x2_ref[...] = x2
    zz = _mm(_rms(x2, g_ref[...]), w_ref[...], precise) + b_ref[...]
    half = zz.shape[1] // 2
    u_ref[...] = zz[:, :half] * jax.nn.sigmoid(zz[:, half:])


def _combine_final_body(rn_ref, rg_ref, rl_ref, lpos_ref, gate_ref, x_ref, ys_hbm, g_ref, o_ref,
                        ystage, comb, sem, *, chunk, nch):
    x2 = _combine(rn_ref, rg_ref, rl_ref, lpos_ref, gate_ref, x_ref, ys_hbm, ystage, comb, sem, chunk=chunk, nch=nch)
    o_ref[...] = _rms(x2, g_ref[...])


def _combine_call(body, plan, x, ys, extra, extra_specs, out_specs, out_shape, name):
    chunk, nch = plan['chunk'], plan['nch']
    d = x.shape[1]
    smem_blk = pl.BlockSpec((1, 1, TOP_K * chunk), lambda c, *_: (c, 0, 0), memory_space=pltpu.SMEM)
    gs = pltpu.PrefetchScalarGridSpec(
        num_scalar_prefetch=3,
        grid=(nch,),
        in_specs=[smem_blk, smem_blk, pl.BlockSpec((chunk, d), lambda c, *_: (c, 0)),
                  pl.BlockSpec(memory_space=pl.ANY)] + extra_specs,
        out_specs=out_specs,
        scratch_shapes=[pltpu.VMEM((2, TOP_K * chunk * ROW_TILES, LANES), F32),
                        pltpu.VMEM((chunk * ROW_TILES, LANES), F32), pltpu.SemaphoreType.DMA((2,))],
    )
    return pl.pallas_call(
        functools.partial(body, chunk=chunk, nch=nch),
        grid_spec=gs,
        out_shape=out_shape,
        compiler_params=_cparams("arbitrary"),
        name=name,
    )(plan['run_n'], plan['run_g'], plan['run_l'], plan['lpos'], plan['gate'], x, ys, *extra)


def combine_glu(x, ys, plan, g, w, b, *, precise):
    rows, d = x.shape
    chunk = plan['chunk']
    cols = w.shape[1]
    const = lambda shape: pl.BlockSpec(shape, lambda c, *_: (0,) * len(shape))
    row_spec = lambda width: pl.BlockSpec((chunk, width), lambda c, *_: (c, 0))
    return _combine_call(
        functools.partial(_combine_glu_body, precise=precise), plan, x, ys, [g, w, b],
        [const((1, d)), const((d, cols)), const((1, cols))], [row_spec(d), row_spec(cols // 2)],
        [jax.ShapeDtypeStruct((rows, d), F32), jax.ShapeDtypeStruct((rows, cols // 2), F32)], "combine_glu")


def combine_final(x, ys, plan, g):
    rows, d = x.shape
    chunk = plan['chunk']
    return _combine_call(
        _combine_final_body, plan, x, ys, [g], [pl.BlockSpec((1, d), lambda c, *_: (0, 0))],
        pl.BlockSpec((chunk, d), lambda c, *_: (c, 0)), jax.ShapeDtypeStruct((rows, d), F32), "combine_final")


CONV_TILE = 512
CONV_HIST = 32


def _ln_swish(y, g, b):
    yc = y - jnp.mean(y, axis=-1, keepdims=True)
    yn = yc * lax.rsqrt(jnp.mean(yc * yc, axis=-1, keepdims=True) + EPS) * g + b
    return yn * jax.nn.sigmoid(yn)


CONV_ROWS = 64
LN_ROWS = 16
LN_UNROLL = 8


def _dwconv_prompt_body(u_ref, w_ref, bdw_ref, g_ref, b_ref, o_ref, ext, y_sc, *, tt):
    t = pl.program_id(1)
    n_lt = ext.shape[0]

    @pl.when(t == 0)
    def _():
        ext[:, 0:CONV_HIST, :] = jnp.zeros((n_lt, CONV_HIST, LANES), F32)

    @pl.when(t > 0)
    def _():
        ext[:, 0:CONV_HIST, :] = ext[:, tt:tt + CONV_HIST, :]

    for j in range(n_lt):
        ext[j, CONV_HIST:CONV_HIST + tt, :] = u_ref[:, j * LANES:(j + 1) * LANES]
    off = CONV_HIST - (C_KERNEL - 1)
    for j in range(n_lt):
        wj = w_ref[:, j * LANES:(j + 1) * LANES]
        bj = bdw_ref[:, j * LANES:(j + 1) * LANES]
        for c in range(tt // CONV_ROWS):
            acc = ext[j, pl.ds(off + c * CONV_ROWS, CONV_ROWS), :] * wj[0:1] + bj
            for k in range(1, C_KERNEL):
                acc = acc + ext[j, pl.ds(off + k + c * CONV_ROWS, CONV_ROWS), :] * wj[k:k + 1]
            y_sc[c * CONV_ROWS:(c + 1) * CONV_ROWS, j * LANES:(j + 1) * LANES] = acc

    def ln_rows(r, carry):
        rows = pl.ds(pl.multiple_of(r * LN_ROWS, LN_ROWS), LN_ROWS)
        o_ref[rows, :] = _ln_swish(y_sc[rows, :], g_ref[...], b_ref[...]).astype(o_ref.dtype)
        return carry
    lax.fori_loop(0, tt // LN_ROWS, ln_rows, 0, unroll=LN_UNROLL)


def dwconv_prompt(u, w, b_dw, ln_g, ln_b, bsz, seq):
    tt = min(CONV_TILE, seq)
    nt = seq // tt
    d = u.shape[1]
    const = lambda shape: pl.BlockSpec(shape, lambda b, t: (0,) * len(shape))
    return pl.pallas_call(
        functools.partial(_dwconv_prompt_body, tt=tt),
        grid=(bsz, nt),
        in_specs=[pl.BlockSpec((tt, d), lambda b, t: (b * nt + t, 0)), const((C_KERNEL, d)), const((1, d)),
                  const((1, d)), const((1, d))],
        out_specs=pl.BlockSpec((tt, d), lambda b, t: (b * nt + t, 0)),
        out_shape=jax.ShapeDtypeStruct((bsz * seq, d), BF16),
        scratch_shapes=[pltpu.VMEM((d // LANES, CONV_HIST + tt, LANES), F32), pltpu.VMEM((tt, d), F32)],
        compiler_params=_cparams("arbitrary", "arbitrary"),
        name="dwconv_prompt",
    )(u, w, b_dw, ln_g, ln_b)


def _dwconv_sample_body(u_ref, buf_ref, w_ref, bdw_ref, g_ref, b_ref, o_ref, nbuf_ref, *, tb):
    w = w_ref[...]
    rows = []
    for i in range(tb):
        hist = buf_ref[i]
        ur = u_ref[i:i + 1, :]
        rows.append(jnp.sum(hist * w[:C_KERNEL - 1], axis=0, keepdims=True) + ur * w[C_KERNEL - 1:C_KERNEL])
        nbuf_ref[i] = jnp.concatenate([hist[1:], ur], axis=0)
    y = jnp.concatenate(rows, axis=0) + bdw_ref[...]
    o_ref[...] = _ln_swish(y, g_ref[...], b_ref[...])


def dwconv_sample(u, buf, w, b_dw, ln_g, ln_b):
    nb, d = u.shape
    tb = min(SAMPLE_TILE, nb)
    const = lambda shape: pl.BlockSpec(shape, lambda i: (0,) * len(shape))
    return pl.pallas_call(
        functools.partial(_dwconv_sample_body, tb=tb),
        grid=(nb // tb,),
        in_specs=[pl.BlockSpec((tb, d), lambda i: (i, 0)), pl.BlockSpec((tb, C_KERNEL - 1, d), lambda i: (i, 0, 0)),
                  const((C_KERNEL, d)), const((1, d)), const((1, d)), const((1, d))],
        out_specs=[pl.BlockSpec((tb, d), lambda i: (i, 0)), pl.BlockSpec((tb, C_KERNEL - 1, d), lambda i: (i, 0, 0))],
        out_shape=[jax.ShapeDtypeStruct((nb, d), F32), jax.ShapeDtypeStruct(buf.shape, F32)],
        compiler_params=_cparams("parallel"),
        name="dwconv_sample",
    )(u, buf, w, b_dw, ln_g, ln_b)


def _moe(h8, lpos8, gate8, runn8, wg, wu, wd, layer, *, precise):
    plan = moe_plan(lpos8, gate8, runn8, EXPERT_BLOCK_PRECISE if precise else EXPERT_BLOCK)
    xs = dispatch(h8, plan)
    return experts(xs, plan, wg, wu, wd, layer, precise=precise), plan


def _trunk(x, caches, p, *, prompt):
    bsz, seq, d = x.shape
    rows = bsz * seq
    precise = not prompt
    wdt = F32 if precise else BF16
    xf = x.reshape(rows, d)
    row = lambda v: v.reshape(1, -1).astype(F32)

    z = norm_proj(xf, row(p['norm_mix'][0]), p['w_in'].astype(wdt), precise=precise)
    if prompt:
        z, zb = z
        att = attn_prompt(z, zb, p['rel_table'], p['sinks'], bsz, seq)
        out_b, c1, n1, m1 = mlstm_prompt(z, zb, p['conv_w'], p['b_gates'], p['g_mnorm'], bsz, seq)
        z3 = z.reshape(bsz, seq, P_COLS)
        new_k = z3[:, seq - WINDOW:, P_KA:P_KA + A_KV].reshape(bsz, WINDOW, A_KV_HEADS, A_HEAD_DIM)
        new_v = z3[:, seq - WINDOW:, P_VA:P_VA + A_KV].reshape(bsz, WINDOW, A_KV_HEADS, A_HEAD_DIM)
        new_conv = z3[:, seq - (B_CONV - 1):, P_QK:P_QK + 2 * B_QK]
        n1 = n1[:, :B_HEADS]
        m1 = m1[:, :B_HEADS, 0]
    else:
        ck, cv, c0, n0, m0, cbuf = caches[:6]
        n_buf = ck.shape[1]
        att, out_b, new_k, new_v, c1, n1, m1, new_conv = mix_sample(
            z, p['rel_table'], p['sinks'], ck.reshape(bsz, n_buf, A_KV), cv.reshape(bsz, n_buf, A_KV),
            c0, n0, m0, cbuf, p['conv_w'], p['b_gates'], p['g_mnorm'])
        new_k = new_k.reshape(bsz, n_buf, A_KV_HEADS, A_HEAD_DIM)
        new_v = new_v.reshape(bsz, n_buf, A_KV_HEADS, A_HEAD_DIM)
        m1 = m1[:, :B_HEADS]
    w_out = p['w_out'].astype(wdt)
    x1, h8, lpos, gate, runn = proj_router([att, out_b], [w_out[:A_Q], w_out[A_Q:]], None, xf, row(p['norm_ffn'][0]),
                                           p['w_router'][0], p['b_router'][0], precise=precise)
    ys, plan = _moe(h8, lpos, gate, runn, p['w_eg'], p['w_eu'], p['w_ed'], 0, precise=precise)

    x2, u = combine_glu(x1, ys, plan, row(p['norm_mix'][1]), p['w_pw1'].astype(wdt), row(p['b_pw1']),
                        precise=precise)
    if prompt:
        yc = dwconv_prompt(u, p['w_dw'], row(p['b_dw']), row(p['ln_g']), row(p['ln_b']), bsz, seq)
        new_cbuf = u.reshape(bsz, seq, d)[:, seq - (C_KERNEL - 1):]
    else:
        yc, new_cbuf = dwconv_sample(u, caches[6], p['w_dw'], row(p['b_dw']), row(p['ln_g']), row(p['ln_b']))
    x3, h8, lpos, gate, runn = proj_router([yc], [p['w_pw2'].astype(wdt)], row(p['b_pw2']), x2, row(p['norm_ffn'][1]),
                                           p['w_router'][1], p['b_router'][1], precise=precise)
    ys, plan = _moe(h8, lpos, gate, runn, p['w_eg'], p['w_eu'], p['w_ed'], 1, precise=precise)
    y = combine_final(x3, ys, plan, row(p['norm_final']))
    add_layer = lambda t: t[None]
    return (y.reshape(bsz, seq, d),) + tuple(add_layer(t) for t in (new_k, new_v, c1, n1, m1, new_conv, new_cbuf))


def kernel(x_prompt, x_sample, cache_win_k, cache_win_v, state_mlstm_c, state_mlstm_n, state_mlstm_m, state_mlstm_conv, state_conv, norm_mix, norm_ffn, norm_final, rel_bias_table, w_in_mix, b_mlstm_gates, w_mlstm_qk_conv, attn_sinks, g_mlstm_norm, w_out_mix, w_pw1, b_pw1, w_dw, b_dw, ln_conv_g, ln_conv_b, w_pw2, b_pw2, w_router_group, b_router_group, w_router_expert, b_router_expert, w_expert_gate, w_expert_up, w_expert_down):
    w_in = w_in_mix[0]
    s_q, s_k, s_v, s_qk, s_vb, s_g = A_Q, A_Q + A_KV, A_Q + 2 * A_KV, A_Q + 2 * A_KV + 2 * B_QK, \
        A_Q + 2 * A_KV + 2 * B_QK + B_V, A_Q + 2 * A_KV + 2 * B_QK + B_V + 2 * B_HEADS
    w_in_r = jnp.concatenate([w_in[:, :s_q], w_in[:, s_v:s_qk], w_in[:, s_qk:s_vb], w_in[:, s_g:],
                              w_in[:, s_q:s_k], w_in[:, s_k:s_v], w_in[:, s_vb:s_g],
                              jnp.zeros((D_MODEL, LANES - 2 * B_HEADS), F32)], axis=1)
    b_gates = jnp.concatenate([b_mlstm_gates[0], jnp.zeros((LANES - 2 * B_HEADS,), F32)]).reshape(1, LANES)
    depth = w_router_group.shape[0]
    w_re = jnp.transpose(w_router_expert, (0, 2, 1, 3)).reshape(depth, D_MODEL, N_EXPERTS)
    w_router = jnp.concatenate([w_router_group, w_re,
                                jnp.zeros((depth, D_MODEL, LANES - N_GROUPS - N_EXPERTS), F32)], axis=-1)
    b_router = jnp.concatenate([b_router_group, b_router_expert.reshape(depth, N_EXPERTS),
                                jnp.zeros((depth, LANES - N_GROUPS - N_EXPERTS), F32)], axis=-1)[:, None, :]
    p = dict(norm_mix=norm_mix, norm_ffn=norm_ffn, norm_final=norm_final, rel_table=rel_bias_table,
             sinks=attn_sinks[0], w_in=w_in_r, b_gates=b_gates, conv_w=w_mlstm_qk_conv[0],
             g_mnorm=g_mlstm_norm[0].reshape(1, B_V), w_out=w_out_mix[0], w_pw1=w_pw1[0], b_pw1=b_pw1[0],
             w_dw=w_dw[0], b_dw=b_dw[0], ln_g=ln_conv_g[0], ln_b=ln_conv_b[0], w_pw2=w_pw2[0], b_pw2=b_pw2[0],
             w_router=w_router, b_router=b_router, w_eg=w_expert_gate, w_eu=w_expert_up, w_ed=w_expert_down)
    caches = (cache_win_k[0], cache_win_v[0], state_mlstm_c[0], state_mlstm_n[0], state_mlstm_m[0],
              state_mlstm_conv[0], state_conv[0])
    out_p = _trunk(x_prompt, None, p, prompt=True)
    out_s = _trunk(x_sample, caches, p, prompt=False)
    return (out_p[0], out_s[0]) + out_p[1:] + out_s[1:]
```

```python
import functools
import math

import numpy as np
import jax
import jax.numpy as jnp
from jax import lax
from jax.experimental import pallas as pl
from jax.experimental.pallas import tpu as pltpu

F32 = jnp.float32
BF16 = jnp.bfloat16
I32 = jnp.int32
HI = lax.Precision.HIGHEST
NEG_INF = float("-inf")

LANES = 128
SUBLANES = 8
VMEM_LIMIT = 56 * 1024 * 1024

D_MODEL = 1024
A_HEADS = 8
A_KV_HEADS = 2
A_GROUP = A_HEADS // A_KV_HEADS
A_HEAD_DIM = 64
WINDOW = 128
REL_BUCKETS = 32
REL_MAX_DIST = 128
B_HEADS = 4
B_DK = 64
B_DV = 128
B_CONV = 4
C_KERNEL = 31
N_GROUPS = 4
EXPERTS_PER_GROUP = 8
N_EXPERTS = N_GROUPS * EXPERTS_PER_GROUP
TOP_K = 2
EXPERT_FF = D_MODEL // 2
EXPERT_BLOCK = 256
EXPERT_BLOCK_PRECISE = 128
EPS = 1e-6

A_Q = A_HEADS * A_HEAD_DIM
A_KV = A_KV_HEADS * A_HEAD_DIM
B_QK = B_HEADS * B_DK
B_V = B_HEADS * B_DV
ROW_TILES = D_MODEL // LANES

Z_QA, Z_QK, Z_VB, Z_OG, Z_KA, Z_VA, Z_GATES = 0, 512, 1024, 1536, 2048, 2176, 2304
Z_COLS = 2432
MLSTM_CHUNK = 128


def _cparams(*sem):
    return pltpu.CompilerParams(dimension_semantics=sem, vmem_limit_bytes=VMEM_LIMIT)


def _rms(x, g):
    return x * lax.rsqrt(jnp.mean(x * x, axis=-1, keepdims=True) + EPS) * g


def _mm(a, w, precise):
    if not precise:
        return jnp.dot(a.astype(BF16), w, preferred_element_type=F32)
    a = a.astype(F32)
    a_hi = a.astype(BF16)
    a_lo = (a - a_hi.astype(F32)).astype(BF16)
    w_hi = w.astype(BF16)
    w_lo = (w - w_hi.astype(F32)).astype(BF16)
    return (jnp.dot(a_hi, w_hi, preferred_element_type=F32) + jnp.dot(a_lo, w_hi, preferred_element_type=F32)
            + jnp.dot(a_hi, w_lo, preferred_element_type=F32))


def _t5_buckets(dist):
    exact = REL_BUCKETS // 2
    d = np.maximum(dist, 0)
    large = exact + (np.log(np.maximum(d, 1).astype(np.float32) / exact)
                     / math.log(REL_MAX_DIST / exact) * (REL_BUCKETS - exact)).astype(np.int32)
    return np.where(d < exact, d, np.minimum(large, REL_BUCKETS - 1)).astype(np.int32)


P_QK, P_OG, P_KA, P_VA, P_GATES = 0, 512, 1024, 1152, 1280
P_COLS = 1408
PB_QA, PB_VB = 0, 512
PB_COLS = 1024


def _norm_proj_body(x_ref, g_ref, w_ref, *o_refs, precise):
    h = _rms(x_ref[...], g_ref[...])
    z = _mm(h, w_ref[...], precise)
    if len(o_refs) == 1:
        o_refs[0][...] = z
        return
    z32_ref, zb_ref = o_refs
    z32_ref[...] = jnp.concatenate([z[:, Z_QK:Z_QK + 2 * B_QK], z[:, Z_OG:Z_OG + B_V], z[:, Z_KA:Z_GATES + LANES]],
                                   axis=1)
    zb_ref[...] = jnp.concatenate([z[:, Z_QA:Z_QA + A_Q], z[:, Z_VB:Z_VB + B_V]], axis=1).astype(BF16)


def norm_proj(x, g, w, *, precise):
    rows, d = x.shape
    cols = w.shape[1]
    tm = min(rows, 128 if precise else 512)
    row_spec = lambda width: pl.BlockSpec((tm, width), lambda i: (i, 0))
    if precise:
        out_specs, out_shape = row_spec(cols), jax.ShapeDtypeStruct((rows, cols), F32)
    else:
        out_specs = [row_spec(P_COLS), row_spec(PB_COLS)]
        out_shape = [jax.ShapeDtypeStruct((rows, P_COLS), F32), jax.ShapeDtypeStruct((rows, PB_COLS), BF16)]
    return pl.pallas_call(
        functools.partial(_norm_proj_body, precise=precise),
        grid=(rows // tm,),
        in_specs=[row_spec(d), pl.BlockSpec((1, d), lambda i: (0, 0)), pl.BlockSpec((d, cols), lambda i: (0, 0))],
        out_specs=out_specs,
        out_shape=out_shape,
        compiler_params=_cparams("parallel"),
        name="norm_proj",
    )(x, g, w)


def _attn_prompt_body(tab_ref, sink_ref, bkt_ref, q_ref, kp_ref, kc_ref, vp_ref, vc_ref, o_ref, bias_ref, *, nq):
    b = pl.program_id(0)
    n = pl.program_id(1)

    @pl.when((b == 0) & (n == 0))
    def _():
        bk = bkt_ref[...]
        first = lax.broadcasted_iota(I32, bk.shape, 1) >= WINDOW
        for h in range(A_HEADS):
            acc = jnp.full(bk.shape, NEG_INF, F32)
            for t in range(REL_BUCKETS):
                acc = jnp.where(bk == t, tab_ref[t, h], acc)
            bias_ref[h] = acc
            bias_ref[A_HEADS + h] = jnp.where(first, acc, NEG_INF)

    k_all = jnp.concatenate([kp_ref[...], kc_ref[...]], axis=0).astype(BF16)
    v_all = jnp.concatenate([vp_ref[...], vc_ref[...]], axis=0).astype(BF16)
    for sub in range(nq):
        rows = slice(sub * WINDOW, (sub + 1) * WINDOW)
        q = q_ref[rows, :] * (A_HEAD_DIM ** -0.5)
        kb = k_all[sub * WINDOW:(sub + 2) * WINDOW]
        vb = v_all[sub * WINDOW:(sub + 2) * WINDOW]
        table = jnp.where(n == 0, A_HEADS, 0) if sub == 0 else 0
        outs = []
        for h in range(A_HEADS):
            kvh = h // A_GROUP
            qh = q[:, h * A_HEAD_DIM:(h + 1) * A_HEAD_DIM].astype(BF16)
            kh = kb[:, kvh * A_HEAD_DIM:(kvh + 1) * A_HEAD_DIM]
            vh = vb[:, kvh * A_HEAD_DIM:(kvh + 1) * A_HEAD_DIM]
            s = lax.dot_general(qh, kh, (((1,), (1,)), ((), ())), preferred_element_type=F32)
            s = s + bias_ref[table + h]
            sink = sink_ref[h]
            mx = jnp.maximum(jnp.max(s, axis=-1, keepdims=True), sink)
            p = jnp.exp(s - mx)
            den = jnp.sum(p, axis=-1, keepdims=True) + jnp.exp(sink - mx)
            outs.append(jnp.dot(p.astype(BF16), vh, preferred_element_type=F32) / den)
        o_ref[rows, :] = jnp.concatenate(outs, axis=1).astype(o_ref.dtype)


ATTN_BLOCKS = 2


def attn_prompt(z, zb, rel_table, sinks, bsz, seq):
    nb = seq // WINDOW
    dist = WINDOW + np.arange(WINDOW)[:, None] - np.arange(2 * WINDOW)[None, :]
    bkt = np.where((dist >= 0) & (dist <= WINDOW), _t5_buckets(dist), -1).astype(np.int32)
    kcol, vcol = P_KA // LANES, P_VA // LANES
    smem = pl.BlockSpec(memory_space=pltpu.SMEM)
    nq = ATTN_BLOCKS if nb % ATTN_BLOCKS == 0 else 1
    ns = nb // nq

    def cur(c):
        return pl.BlockSpec((nq * WINDOW, LANES), lambda b, n: (b * ns + n, c))

    def prev(c):
        return pl.BlockSpec((WINDOW, LANES), lambda b, n: (b * nb + jnp.maximum(n * nq - 1, 0), c))

    return pl.pallas_call(
        functools.partial(_attn_prompt_body, nq=nq),
        grid=(bsz, ns),
        in_specs=[smem, smem,
                  pl.BlockSpec((WINDOW, 2 * WINDOW), lambda b, n: (0, 0)),
                  pl.BlockSpec((nq * WINDOW, A_Q), lambda b, n: (b * ns + n, PB_QA // A_Q)),
                  prev(kcol), cur(kcol), prev(vcol), cur(vcol)],
        out_specs=pl.BlockSpec((nq * WINDOW, A_Q), lambda b, n: (b * ns + n, 0)),
        out_shape=jax.ShapeDtypeStruct((bsz * seq, A_Q), BF16),
        scratch_shapes=[pltpu.VMEM((2 * A_HEADS, WINDOW, 2 * WINDOW), F32)],
        compiler_params=_cparams("arbitrary", "arbitrary"),
        name="attn_prompt",
    )(rel_table, sinks, jnp.asarray(bkt), zb, z, z, z, z)


def _log_sigmoid(x):
    return -(jnp.maximum(-x, 0.0) + jnp.log1p(jnp.exp(-jnp.abs(x))))


def _mlstm_prompt_body(qk_ref, v_ref, g_ref, og_ref, cw_ref, bg_ref, gn_ref,
                       ob_ref, c_out, n_out, m_out,
                       c_sc, n_sc, m_sc, hist_sc, *, L, nc, nbat):
    ci = pl.program_id(1)

    @pl.when(ci == 0)
    def _():
        c_sc[...] = jnp.zeros(c_sc.shape, F32)
        n_sc[...] = jnp.zeros(n_sc.shape, F32)
        m_sc[...] = jnp.full(m_sc.shape, NEG_INF, F32)
        hist_sc[...] = jnp.zeros(hist_sc.shape, F32)

    row = lax.broadcasted_iota(I32, (L, L), 0)
    colm = lax.broadcasted_iota(I32, (L, L), 1)
    causal = colm <= row
    tril = causal.astype(F32)
    stores = []
    for bb in range(nbat):
        stores += _mlstm_chunk(qk_ref.at[bb], v_ref.at[bb], g_ref.at[bb], og_ref.at[bb], cw_ref, bg_ref, gn_ref,
                               ob_ref.at[bb], c_sc.at[bb], n_sc.at[bb], m_sc.at[bb], hist_sc.at[bb],
                               causal, tril, L)
    for store in stores:
        store()

    @pl.when(ci == nc - 1)
    def _():
        c_out[...] = c_sc[...]
        n_out[...] = n_sc[...]
        m_out[...] = m_sc[...]


def _mlstm_chunk(qk_ref, v_ref, g_ref, og_ref, cw_ref, bg_ref, gn_ref, ob_ref, c_sc, n_sc, m_sc, hist_sc,
                 causal, tril, L):
    cur = qk_ref[...]
    ext = jnp.concatenate([hist_sc[...], cur], axis=0)
    cw = cw_ref[...]
    off = SUBLANES - (B_CONV - 1)
    conv = ext[off:off + L] * cw[0:1]
    for j in range(1, B_CONV):
        conv = conv + ext[off + j:off + j + L] * cw[j:j + 1]
    qk = conv * jax.nn.sigmoid(conv)
    q_all = qk[:, :B_QK]
    k_all = qk[:, B_QK:] * (B_DK ** -0.5)
    k_t = k_all.T
    v_all = v_ref[...]
    og = og_ref[...]
    gn = gn_ref[...]

    G = g_ref[...] + bg_ref[...]
    lf = _log_sigmoid(G)
    Bc = jnp.dot(tril, lf, precision=HI, preferred_element_type=F32)
    BT = Bc.T
    GT = G.T

    outs, stores = [], []
    for h in range(B_HEADS):
        qh = q_all[:, h * B_DK:(h + 1) * B_DK]
        kh = k_all[:, h * B_DK:(h + 1) * B_DK]
        vh = v_all[:, h * B_DV:(h + 1) * B_DV]
        b_col = Bc[:, B_HEADS + h:B_HEADS + h + 1]
        ig_col = G[:, h:h + 1]
        b_row = BT[B_HEADS + h:B_HEADS + h + 1, :]
        ig_row = GT[h:h + 1, :]
        c0 = c_sc[h]
        n0 = n_sc[h:h + 1, :]
        m0 = m_sc[h:h + 1, 0:1]
        a = b_col + m0
        d = jnp.where(causal, b_col - b_row + ig_row, NEG_INF)
        m = jnp.maximum(a, jnp.max(d, axis=-1, keepdims=True))
        dw = jnp.exp(d - m)
        aw = jnp.exp(a - m)
        qb = qh.astype(BF16)
        vb = vh.astype(BF16)
        s = lax.dot_general(qb, kh.astype(BF16), (((1,), (1,)), ((), ())), preferred_element_type=F32) * dw
        num = (jnp.dot(s.astype(BF16), vb, preferred_element_type=F32)
               + aw * jnp.dot(qb, c0.astype(BF16), preferred_element_type=F32))
        den = jnp.sum(s, axis=-1, keepdims=True) + aw * jnp.sum(qh * n0, axis=-1, keepdims=True)
        hh = num / jnp.maximum(jnp.abs(den), jnp.exp(-m))
        m_last = m[L - 1:L, :]
        wl = jnp.exp(b_col[L - 1:L, :] - b_col + ig_col - m_last)
        decay = aw[L - 1:L, :]
        kw_t = k_t[h * B_DK:(h + 1) * B_DK, :] * dw[L - 1:L, :]
        c1 = decay * c0 + jnp.dot(kw_t.astype(BF16), vb, preferred_element_type=F32)
        n1 = decay * n0 + jnp.sum(kh * wl, axis=0, keepdims=True)
        stores.append(functools.partial(_store_state, c_sc, n_sc, m_sc, h, c1, n1, m_last))
        hn = hh * lax.rsqrt(jnp.mean(hh * hh, axis=-1, keepdims=True) + EPS) * gn[:, h * B_DV:(h + 1) * B_DV]
        outs.append(jax.nn.sigmoid(og[:, h * B_DV:(h + 1) * B_DV]) * hn)
    out = jnp.concatenate(outs, axis=1).astype(ob_ref.dtype)
    stores.append(functools.partial(_store_chunk, ob_ref, hist_sc, out, cur[L - SUBLANES:L]))
    return stores


def _store_state(c_sc, n_sc, m_sc, h, c1, n1, m_last):
    c_sc[h] = c1
    n_sc[h:h + 1, :] = n1
    m_sc[h:h + 1, :] = jnp.broadcast_to(m_last, (1, LANES))


def _store_chunk(ob_ref, hist_sc, out, tail):
    ob_ref[...] = out
    hist_sc[...] = tail


MLSTM_BATCH = 1


def mlstm_prompt(z, zb, conv_w, b_gates_pad, g_norm, bsz, seq):
    L = MLSTM_CHUNK
    nc = seq // L
    nbat = MLSTM_BATCH if bsz % MLSTM_BATCH == 0 else 1
    z3 = z.reshape(bsz, seq, P_COLS)
    zb3 = zb.reshape(bsz, seq, PB_COLS)

    def zspec(width, colblk):
        return pl.BlockSpec((nbat, L, width), lambda b, c: (b, c, colblk))

    const = lambda shape: pl.BlockSpec(shape, lambda b, c: (0,) * len(shape))
    state = lambda shape: pl.BlockSpec((nbat,) + shape, lambda b, c: (b,) + (0,) * len(shape))
    out_b, c1, n1, m1 = pl.pallas_call(
        functools.partial(_mlstm_prompt_body, L=L, nc=nc, nbat=nbat),
        grid=(bsz // nbat, nc),
        in_specs=[zspec(2 * B_QK, P_QK // (2 * B_QK)), zspec(B_V, PB_VB // B_V), zspec(LANES, P_GATES // LANES),
                  zspec(B_V, P_OG // B_V), const((B_CONV, 2 * B_QK)), const((1, LANES)), const((1, B_V))],
        out_specs=[pl.BlockSpec((nbat, L, B_V), lambda b, c: (b, c, 0)),
                   state((B_HEADS, B_DK, B_DV)), state((SUBLANES, B_DK)), state((SUBLANES, LANES))],
        out_shape=[jax.ShapeDtypeStruct((bsz, seq, B_V), BF16),
                   jax.ShapeDtypeStruct((bsz, B_HEADS, B_DK, B_DV), F32),
                   jax.ShapeDtypeStruct((bsz, SUBLANES, B_DK), F32),
                   jax.ShapeDtypeStruct((bsz, SUBLANES, LANES), F32)],
        scratch_shapes=[pltpu.VMEM((nbat, B_HEADS, B_DK, B_DV), F32), pltpu.VMEM((nbat, SUBLANES, B_DK), F32),
                        pltpu.VMEM((nbat, SUBLANES, LANES), F32), pltpu.VMEM((nbat, SUBLANES, 2 * B_QK), F32)],
        compiler_params=_cparams("arbitrary", "arbitrary"),
        name="mlstm_prompt",
    )(z3, zb3, z3, z3, conv_w, b_gates_pad, g_norm)
    return out_b.reshape(bsz * seq, B_V), c1, n1, m1


SAMPLE_TILE = 8


def _row_to_col(row, n):
    eye = lax.broadcasted_iota(I32, (n, n), 0) == lax.broadcasted_iota(I32, (n, n), 1)
    return jnp.sum(jnp.where(eye, jnp.broadcast_to(row, (n, n)), 0.0), axis=1, keepdims=True)


def _mix_sample_body(tab_ref, sink_ref, bkt_ref, z_ref, ck_ref, cv_ref, c0_ref, n0_ref, m0_ref, cb_ref,
                     cw_ref, bg_ref, gn_ref,
                     att_ref, ob_ref, nk_ref, nv_ref, c1_ref, n1_ref, m1_ref, ncb_ref, *, tb):
    bk = bkt_ref[...]
    cw = cw_ref[...]
    gn = gn_ref[...]
    zero_half = jnp.zeros((1, A_HEAD_DIM), F32)
    bias_rows = []
    for h in range(A_HEADS):
        bias = jnp.zeros(bk.shape, F32)
        for t in range(REL_BUCKETS):
            bias = jnp.where(bk == t, tab_ref[t, h], bias)
        bias_rows.append(bias)
    bias_c = jnp.concatenate(bias_rows, axis=0)
    head_id = lax.broadcasted_iota(I32, (A_HEADS, 1), 0)
    bias_n = jnp.zeros((A_HEADS, 1), F32)
    sinks = jnp.zeros((A_HEADS, 1), F32)
    for h in range(A_HEADS):
        bias_n = jnp.where(head_id == h, tab_ref[0, h], bias_n)
        sinks = jnp.where(head_id == h, sink_ref[h], sinks)
    att_rows, ob_rows, m_rows = [], [], []
    stores = []
    for i in range(tb):
        zr = z_ref[i:i + 1, :]
        q_att = zr[:, Z_QA:Z_QA + A_Q] * (A_HEAD_DIM ** -0.5)
        ka = zr[:, Z_KA:Z_KA + A_KV]
        va = zr[:, Z_VA:Z_VA + A_KV]
        kc = ck_ref[i]
        vc = cv_ref[i]
        stores.append((nk_ref, i, jnp.concatenate([kc[1:], ka], axis=0)))
        stores.append((nv_ref, i, jnp.concatenate([vc[1:], va], axis=0)))
        q_rows = []
        for h in range(A_HEADS):
            qh = q_att[:, h * A_HEAD_DIM:(h + 1) * A_HEAD_DIM]
            q_rows.append(jnp.concatenate([qh, zero_half] if h // A_GROUP == 0 else [zero_half, qh], axis=1))
        qm = jnp.concatenate(q_rows, axis=0)
        lc = lax.dot_general(qm, kc, (((1,), (1,)), ((), ())), precision=HI,
                             preferred_element_type=F32) + bias_c
        ln = jnp.sum(qm * ka, axis=-1, keepdims=True) + bias_n
        mx = jnp.maximum(jnp.maximum(jnp.max(lc, axis=-1, keepdims=True), ln), sinks)
        pc = jnp.exp(lc - mx)
        pn = jnp.exp(ln - mx)
        den = jnp.sum(pc, axis=-1, keepdims=True) + pn + jnp.exp(sinks - mx)
        o = (jnp.dot(pc, vc, precision=HI, preferred_element_type=F32) + pn * va) / den
        att_rows.append(jnp.concatenate(
            [o[h:h + 1, (h // A_GROUP) * A_HEAD_DIM:(h // A_GROUP + 1) * A_HEAD_DIM] for h in range(A_HEADS)],
            axis=1))

        qk_pre = zr[:, Z_QK:Z_QK + 2 * B_QK]
        hist = cb_ref[i]
        conv = qk_pre * cw[B_CONV - 1:B_CONV]
        for j in range(B_CONV - 1):
            conv = conv + hist[j:j + 1] * cw[j:j + 1]
        stores.append((ncb_ref, i, jnp.concatenate([hist[1:], qk_pre], axis=0)))
        qk = conv * jax.nn.sigmoid(conv)
        G = zr[:, Z_GATES:Z_GATES + LANES] + bg_ref[...]
        lfr = _log_sigmoid(G)
        og = zr[:, Z_OG:Z_OG + B_V]
        v_pre = zr[:, Z_VB:Z_VB + B_V]
        obs, ms = [], []
        for h in range(B_HEADS):
            qh = qk[:, h * B_DK:(h + 1) * B_DK]
            kh = qk[:, B_QK + h * B_DK:B_QK + (h + 1) * B_DK] * (B_DK ** -0.5)
            vh = v_pre[:, h * B_DV:(h + 1) * B_DV]
            ig = G[:, h:h + 1]
            lf = lfr[:, B_HEADS + h:B_HEADS + h + 1]
            c0 = c0_ref[i, h]
            n0 = n0_ref[i, h:h + 1, :]
            m0 = m0_ref[i:i + 1, h:h + 1]
            a = lf + m0
            m = jnp.maximum(a, ig)
            dw = jnp.exp(ig - m)
            aw = jnp.exp(a - m)
            s = jnp.sum(qh * kh, axis=-1, keepdims=True) * dw
            q_col = _row_to_col(qh, B_DK)
            k_col = _row_to_col(kh, B_DK)
            num = s * vh + aw * jnp.sum(q_col * c0, axis=0, keepdims=True)
            den = s + aw * jnp.sum(qh * n0, axis=-1, keepdims=True)
            hh = num / jnp.maximum(jnp.abs(den), jnp.exp(-m))
            stores.append((c1_ref, (i, h), aw * c0 + dw * (k_col * vh)))
            stores.append((n1_ref, (i, slice(h, h + 1), slice(None)), aw * n0 + dw * kh))
            ms.append(m)
            hn = hh * lax.rsqrt(jnp.mean(hh * hh, axis=-1, keepdims=True) + EPS) * gn[:, h * B_DV:(h + 1) * B_DV]
            obs.append(jax.nn.sigmoid(og[:, h * B_DV:(h + 1) * B_DV]) * hn)
        ob_rows.append(jnp.concatenate(obs, axis=1))
        lane = lax.broadcasted_iota(I32, (1, LANES), 1)
        mrow = jnp.zeros((1, LANES), F32)
        for h in range(B_HEADS):
            mrow = jnp.where(lane == h, ms[h], mrow)
        m_rows.append(mrow)
    for ref, idx, val in stores:
        ref[idx] = val
    att_ref[...] = jnp.concatenate(att_rows, axis=0)
    ob_ref[...] = jnp.concatenate(ob_rows, axis=0)
    m1_ref[...] = jnp.concatenate(m_rows, axis=0)


def mix_sample(z, rel_table, sinks, ck, cv, c0, n0, m0, conv_buf, conv_w, b_gates_pad, g_norm):
    nb = z.shape[0]
    tb = min(SAMPLE_TILE, nb)
    n_buf = ck.shape[1]
    bkt = _t5_buckets(n_buf - np.arange(n_buf))[None, :]
    smem = pl.BlockSpec(memory_space=pltpu.SMEM)
    const = lambda shape: pl.BlockSpec(shape, lambda i: (0,) * len(shape))
    lead = lambda shape: pl.BlockSpec((tb,) + shape, lambda i: (i,) + (0,) * len(shape))
    return pl.pallas_call(
        functools.partial(_mix_sample_body, tb=tb),
        grid=(nb // tb,),
        in_specs=[smem, smem, const((1, n_buf)), lead((Z_COLS,)), lead((n_buf, A_KV)), lead((n_buf, A_KV)),
                  lead((B_HEADS, B_DK, B_DV)), lead((B_HEADS, B_DK)), lead((B_HEADS,)),
                  lead((B_CONV - 1, 2 * B_QK)), const((B_CONV, 2 * B_QK)), const((1, LANES)), const((1, B_V))],
        out_specs=[lead((A_Q,)), lead((B_V,)), lead((n_buf, A_KV)), lead((n_buf, A_KV)),
                   lead((B_HEADS, B_DK, B_DV)), lead((B_HEADS, B_DK)), lead((LANES,)),
                   lead((B_CONV - 1, 2 * B_QK))],
        out_shape=[jax.ShapeDtypeStruct((nb, A_Q), F32), jax.ShapeDtypeStruct((nb, B_V), F32),
                   jax.ShapeDtypeStruct(ck.shape, F32), jax.ShapeDtypeStruct(cv.shape, F32),
                   jax.ShapeDtypeStruct(c0.shape, F32), jax.ShapeDtypeStruct(n0.shape, F32),
                   jax.ShapeDtypeStruct((nb, LANES), F32), jax.ShapeDtypeStruct(conv_buf.shape, F32)],
        compiler_params=_cparams("parallel"),
        name="mix_sample",
    )(rel_table, sinks, jnp.asarray(bkt), z, ck, cv, c0, n0, m0, conv_buf, conv_w, b_gates_pad, g_norm)


def _store_row_tiles(ref, val, rows):
    for s in range(ROW_TILES):
        ref[pl.ds(s, rows, stride=ROW_TILES), :] = val[:, s * LANES:(s + 1) * LANES]


def _load_row_tiles(ref, rows, start=0, stride=ROW_TILES):
    return jnp.concatenate([ref[pl.ds(start + s, rows, stride=stride), :] for s in range(ROW_TILES)], axis=1)


def _route(logits):
    lane = lax.broadcasted_iota(I32, logits.shape, 1)
    big = jnp.int32(1 << 20)
    gl = jnp.where(lane < N_GROUPS, logits, NEG_INF)
    gmax = jnp.max(gl, axis=-1, keepdims=True)
    gidx = jnp.min(jnp.where(gl == gmax, lane, big), axis=-1, keepdims=True)
    g_gate = 1.0 / jnp.sum(jnp.exp(gl - gmax), axis=-1, keepdims=True)
    lo = N_GROUPS + gidx * EXPERTS_PER_GROUP
    el = jnp.where((lane >= lo) & (lane < lo + EXPERTS_PER_GROUP), logits, NEG_INF)
    v1 = jnp.max(el, axis=-1, keepdims=True)
    i1 = jnp.min(jnp.where(el == v1, lane, big), axis=-1, keepdims=True)
    el2 = jnp.where(lane == i1, NEG_INF, el)
    v2 = jnp.max(el2, axis=-1, keepdims=True)
    i2 = jnp.min(jnp.where(el2 == v2, lane, big), axis=-1, keepdims=True)
    t = jnp.exp(v2 - v1)
    w1 = g_gate / (1.0 + t)
    w2 = g_gate * t / (1.0 + t)
    gate = jnp.where(lane == 0, w1, jnp.where(lane == 1, w2, 0.0))
    return i1 - N_GROUPS, i2 - N_GROUPS, gate


def _local_sort(e0, e1):
    tm = e0.shape[0]
    lane = lax.broadcasted_iota(I32, (tm, LANES), 1)
    oh0 = (lane == e0).astype(BF16)
    oh1 = (lane == e1).astype(BF16)
    r = lax.broadcasted_iota(I32, (tm, tm), 0)
    c = lax.broadcasted_iota(I32, (tm, tm), 1)
    before = (c < r).astype(BF16)
    cnt0 = jnp.sum(oh0.astype(F32), axis=0, keepdims=True)
    run_n = cnt0 + jnp.sum(oh1.astype(F32), axis=0, keepdims=True)
    er = lax.broadcasted_iota(I32, (LANES, LANES), 0)
    ec = lax.broadcasted_iota(I32, (LANES, LANES), 1)
    slot_n = jnp.floor((run_n + (RUN_SLOT - 1)) * (1.0 / RUN_SLOT)) * RUN_SLOT
    run_l = jnp.dot(slot_n, (er < ec).astype(F32), precision=HI, preferred_element_type=F32)
    w0 = jnp.dot(before, oh0, preferred_element_type=F32) + run_l
    w1 = jnp.dot(before, oh1, preferred_element_type=F32) + run_l + cnt0
    p0 = jnp.sum(jnp.where(lane == e0, w0, 0.0), axis=-1, keepdims=True)
    p1 = jnp.sum(jnp.where(lane == e1, w1, 0.0), axis=-1, keepdims=True)
    return jnp.where(lane == 0, p0, jnp.where(lane == 1, p1, 0.0)), run_n


MOE_CHUNK = 512
RUN_SLOT = 32


def _moe_chunk(rows, precise):
    return min(rows, 128 if precise else MOE_CHUNK)


def _proj_router_body(*refs, n_in, has_bias, precise, tm):
    a_refs = refs[:n_in]
    w_refs = refs[n_in:2 * n_in]
    k = 2 * n_in
    bias_ref = refs[k] if has_bias else None
    k += 1 if has_bias else 0
    x_ref, g_ref, wr_ref, br_ref, x1_ref, h8_ref, lpos_ref, gate_ref, runn_ref = refs[k:]
    acc = x_ref[...]
    if has_bias:
        acc = acc + bias_ref[...]
    for a_ref, w_ref in zip(a_refs, w_refs):
        acc = acc + _mm(a_ref[...], w_ref[...], precise)
    x1_ref[...] = acc
    h = _rms(acc, g_ref[...])
    _store_row_tiles(h8_ref, h, tm)
    wr = wr_ref[...]
    if precise:
        logits = jnp.dot(h, wr, precision=HI, preferred_element_type=F32)
    else:
        h_hi = h.astype(BF16)
        h_lo = (h - h_hi.astype(F32)).astype(BF16)
        w_hi = wr.astype(BF16)
        w_lo = (wr - w_hi.astype(F32)).astype(BF16)
        logits = (jnp.dot(h_hi, w_hi, preferred_element_type=F32) + jnp.dot(h_lo, w_hi, preferred_element_type=F32)
                  + jnp.dot(h_hi, w_lo, preferred_element_type=F32))
    e0, e1, gate = _route(logits + br_ref[...])
    lpos, run_n = _local_sort(e0, e1)
    lpos_ref[...] = (lpos.T[:SUBLANES] * ROW_TILES).astype(I32)
    gate_ref[...] = gate.T[:SUBLANES]
    runn_ref[...] = jnp.broadcast_to(run_n, runn_ref.shape).astype(I32)


def proj_router(a_list, w_list, bias, x, g, wr, br, *, precise):
    rows, d = x.shape
    tm = _moe_chunk(rows, precise)
    n_in = len(a_list)
    row_spec = lambda width: pl.BlockSpec((tm, width), lambda i: (i, 0))
    const = lambda shape: pl.BlockSpec(shape, lambda i: (0,) * len(shape))
    in_specs = [row_spec(a.shape[1]) for a in a_list] + [const(w.shape) for w in w_list]
    args = list(a_list) + list(w_list)
    if bias is not None:
        in_specs.append(const((1, d)))
        args.append(bias)
    in_specs += [row_spec(d), const((1, d)), const((d, LANES)), const((1, LANES))]
    args += [x, g, wr, br]
    return pl.pallas_call(
        functools.partial(_proj_router_body, n_in=n_in, has_bias=bias is not None, precise=precise, tm=tm),
        grid=(rows // tm,),
        in_specs=in_specs,
        out_specs=[row_spec(d), pl.BlockSpec((tm * ROW_TILES, LANES), lambda i: (i, 0)),
                   pl.BlockSpec((SUBLANES, tm), lambda i: (i, 0)), pl.BlockSpec((SUBLANES, tm), lambda i: (i, 0)),
                   pl.BlockSpec((SUBLANES, LANES), lambda i: (i, 0))],
        out_shape=[jax.ShapeDtypeStruct((rows, d), F32), jax.ShapeDtypeStruct((rows * ROW_TILES, LANES), F32),
                   jax.ShapeDtypeStruct((rows // tm * SUBLANES, tm), I32),
                   jax.ShapeDtypeStruct((rows // tm * SUBLANES, tm), F32),
                   jax.ShapeDtypeStruct((rows // tm * SUBLANES, LANES), I32)],
        compiler_params=_cparams("parallel"),
        name="proj_router",
    )(*args)


def _rows_at(offset):
    return pl.ds(pl.multiple_of(offset, ROW_TILES), ROW_TILES)


def _tile_rows(r, n=1):
    return pl.ds(pl.multiple_of(r * ROW_TILES, ROW_TILES), n * ROW_TILES)


def _pow2_pieces(limit):
    p = 1
    while p * 2 <= limit:
        p *= 2
    out = []
    while p >= 1:
        out.append(p)
        p //= 2
    return out


COMMON_PIECE = 32


def _for_each_piece(n, pieces, fn):
    def emit(ps):
        for p in ps:
            @pl.when((n & p) != 0)
            def _(p=p):
                fn(n & ~(2 * p - 1), p)

    big = [p for p in pieces if p > COMMON_PIECE]
    if big:
        @pl.when(n > 2 * COMMON_PIECE - 1)
        def _():
            emit(big)
    emit([p for p in pieces if p <= COMMON_PIECE])


def _piece_rows(first_row, i):
    return _tile_rows(first_row + i * RUN_SLOT, RUN_SLOT)


def _dispatch_body(rg_ref, sl_ref, np_ref, npc_ref, ps_ref, pn_ref, tail_ref, lpos_ref, h8_ref, xs_hbm,
                   stage, zbuf, sem, zsem, *, chunk, nch, blk):
    c = pl.program_id(0)
    slot = c % 2
    pad_pieces = _pow2_pieces(blk + RUN_SLOT - 1)

    def wait_pieces(count, buf):
        def one(i, carry):
            pltpu.make_async_copy(stage.at[buf, _tile_rows(0, RUN_SLOT)], stage.at[buf, _tile_rows(0, RUN_SLOT)],
                                  sem).wait()
            return carry
        lax.fori_loop(0, count, one, 0)

    @pl.when(c == 0)
    def _():
        stage[...] = jnp.zeros(stage.shape, F32)

    def copy_tok(t, carry):
        row = h8_ref[_tile_rows(t), :]
        stage[slot, _rows_at(lpos_ref[0, 0, t]), :] = row
        stage[slot, _rows_at(lpos_ref[0, 0, chunk + t]), :] = row
        return carry
    lax.fori_loop(0, chunk, copy_tok, 0, unroll=8)

    @pl.when(c > 0)
    def _():
        wait_pieces(npc_ref[jnp.maximum(c - 1, 0)], 1 - slot)

    def send_runs(e, carry):
        k = c * N_EXPERTS + e

        def one(i, carry2):
            pltpu.make_async_copy(stage.at[slot, _piece_rows(sl_ref[k], i)], xs_hbm.at[_piece_rows(rg_ref[k], i)],
                                  sem).start()
            return carry2
        lax.fori_loop(0, np_ref[k], one, 0)
        return carry
    lax.fori_loop(0, N_EXPERTS, send_runs, 0)

    @pl.when(c == nch - 1)
    def _():
        wait_pieces(npc_ref[c], slot)
        zbuf[...] = jnp.zeros(zbuf.shape, F32)

        def pad_dmas(e, op):
            def one(off, p):
                cp = pltpu.make_async_copy(zbuf.at[_tile_rows(0, p)], xs_hbm.at[_tile_rows(ps_ref[e] + off, p)], zsem)
                cp.start() if op == 0 else cp.wait()
            _for_each_piece(pn_ref[e], pad_pieces, one)

        def issue(e, carry):
            pad_dmas(e, 0)
            return carry

        def wait(e, carry):
            pad_dmas(e, 1)
            return carry
        lax.fori_loop(0, N_EXPERTS, issue, 0)
        lax.fori_loop(0, N_EXPERTS, wait, 0)

        half = blk // 2

        def tail_dmas(i, op):
            cp = pltpu.make_async_copy(zbuf.at[_tile_rows(0, half)],
                                       xs_hbm.at[_tile_rows(tail_ref[0] + i * half, half)], zsem)
            cp.start() if op == 0 else cp.wait()

        def tail_issue(i, carry):
            tail_dmas(i, 0)
            return carry

        def tail_wait(i, carry):
            tail_dmas(i, 1)
            return carry
        lax.fori_loop(0, tail_ref[1], tail_issue, 0)
        lax.fori_loop(0, tail_ref[1], tail_wait, 0)


def _stage_rows(chunk):
    return TOP_K * chunk + N_EXPERTS * (RUN_SLOT - 1) // RUN_SLOT * RUN_SLOT + RUN_SLOT


def dispatch(h8, plan):
    chunk, nch, blk = plan['chunk'], plan['nch'], plan['blk']
    n_slots = plan['nblk'] * blk
    gs = pltpu.PrefetchScalarGridSpec(
        num_scalar_prefetch=7,
        grid=(nch,),
        in_specs=[pl.BlockSpec((1, 1, TOP_K * chunk), lambda c, *_: (c, 0, 0), memory_space=pltpu.SMEM),
                  pl.BlockSpec((chunk * ROW_TILES, LANES), lambda c, *_: (c, 0))],
        out_specs=pl.BlockSpec(memory_space=pl.ANY),
        scratch_shapes=[pltpu.VMEM((2, _stage_rows(chunk) * ROW_TILES, LANES), F32),
                        pltpu.VMEM((blk * ROW_TILES, LANES), F32),
                        pltpu.SemaphoreType.DMA(()), pltpu.SemaphoreType.DMA(())],
    )
    return pl.pallas_call(
        functools.partial(_dispatch_body, chunk=chunk, nch=nch, blk=blk),
        grid_spec=gs,
        out_shape=jax.ShapeDtypeStruct((n_slots * ROW_TILES, LANES), F32),
        compiler_params=_cparams("arbitrary"),
        name="dispatch",
    )(plan['run_g'], plan['slot_l'], plan['n_piece'], plan['n_piece_chunk'], plan['pad_start'], plan['pad_n'],
      plan['tail'], plan['lpos'], h8)


FF_CHUNK = 512


def _experts_body(be_ref, nv_ref, xs_ref, wg_ref, wu_ref, wd_ref, ys_ref, xb, *wcast, precise, blk):
    j = pl.program_id(0)
    nv = nv_ref[0]
    slot = j % 2
    k = j - 1

    @pl.when(j == 0)
    def _():
        xb[1] = jnp.zeros(xb.shape[1:], xb.dtype)

    if not precise:
        wgb, wub, wdb = wcast

        @pl.when((k == 0) | ((k > 0) & (k < nv) & (be_ref[jnp.maximum(k, 0)] != be_ref[jnp.maximum(k - 1, 0)])))
        def _():
            wgb[...] = wg_ref[0].astype(BF16)
            wub[...] = wu_ref[0].astype(BF16)
            wdb[...] = wd_ref[0].astype(BF16)

    @pl.when(j <= nv)
    def _():
        xb[slot] = _load_row_tiles(xs_ref, blk).astype(xb.dtype)
        if precise:
            wg, wu, wd = wg_ref[0], wu_ref[0], wd_ref[0]
        else:
            wg, wu, wd = wgb, wub, wdb
        xm = xb[1 - slot]
        gt = _mm(xm, wg[...], precise)
        up = _mm(xm, wu[...], precise)
        _store_row_tiles(ys_ref, _mm(gt * jax.nn.sigmoid(gt) * up, wd[...], precise), blk)

    @pl.when(j > nv)
    def _():
        ys_ref[...] = jnp.zeros(ys_ref.shape, F32)


def experts(xs, plan, wg, wu, wd, layer, *, precise):
    nblk, rows = plan['nblk'], plan['blk'] * ROW_TILES
    d, ff = wg.shape[2], wg.shape[3]
    blk = lambda j, be, nv: (jnp.minimum(j, nv[0] - 1), 0)
    wspec = lambda shape: pl.BlockSpec((None, 1) + shape,
                                       lambda j, be, nv: (layer, be[jnp.clip(j - 1, 0, nv[0] - 1)], 0, 0))
    gs = pltpu.PrefetchScalarGridSpec(
        num_scalar_prefetch=2,
        grid=(nblk + 1,),
        in_specs=[pl.BlockSpec((rows, LANES), blk),
                  wspec((d, ff)), wspec((d, ff)), wspec((ff, d))],
        out_specs=pl.BlockSpec((rows, LANES), lambda j, be, nv: (jnp.maximum(j - 1, 0), 0)),
        scratch_shapes=[pltpu.VMEM((2, plan['blk'], d), F32 if precise else BF16)]
        + ([] if precise else [pltpu.VMEM((d, ff), BF16), pltpu.VMEM((d, ff), BF16), pltpu.VMEM((ff, d), BF16)]),
    )
    return pl.pallas_call(
        functools.partial(_experts_body, precise=precise, blk=plan['blk']),
        grid_spec=gs,
        out_shape=jax.ShapeDtypeStruct(xs.shape, F32),
        compiler_params=_cparams("arbitrary"),
        name="experts",
    )(plan['block_e'], plan['n_used'], xs, wg, wu, wd)


def moe_plan(lpos8, gate8, runn8, blk):
    nch = runn8.shape[0] // SUBLANES
    chunk = lpos8.shape[1]
    n_assign = nch * chunk * TOP_K
    nblk = (n_assign + N_EXPERTS * (blk - 1 + RUN_SLOT) + blk - 1) // blk
    run_n = runn8.reshape(nch, SUBLANES, LANES)[:, 0, :N_EXPERTS]
    per_chunk = lambda a: a.reshape(nch, 1, SUBLANES * chunk)[:, :, :TOP_K * chunk]
    counts = jnp.sum(run_n, axis=0)
    padded = (counts + RUN_SLOT + blk - 1) // blk * blk
    pends = jnp.cumsum(padded)
    pstarts = pends - padded
    run_g = pstarts[None, :] + jnp.cumsum(run_n, axis=0) - run_n
    n_piece = (run_n + RUN_SLOT - 1) // RUN_SLOT
    slot_l = (jnp.cumsum(n_piece, axis=1) - n_piece) * RUN_SLOT
    blk_start = jnp.arange(nblk, dtype=I32) * blk
    block_e = jnp.minimum(jnp.sum((pends[None, :] <= blk_start[:, None]).astype(I32), axis=1), N_EXPERTS - 1)
    return dict(chunk=chunk, nch=nch, nblk=nblk, blk=blk,
                run_g=run_g.reshape(-1).astype(I32), slot_l=slot_l.reshape(-1).astype(I32),
                n_piece=n_piece.reshape(-1).astype(I32), n_piece_chunk=jnp.sum(n_piece, axis=1).astype(I32),
                pad_start=(pstarts + counts).astype(I32), pad_n=(padded - counts).astype(I32),
                lpos=per_chunk(lpos8), gate=per_chunk(gate8), block_e=block_e.astype(I32),
                n_used=(pends[-1:] // blk).astype(I32),
                tail=jnp.stack([pends[-1], 2 * (nblk - pends[-1] // blk)]).astype(I32))


def _combine(rg_ref, sl_ref, np_ref, npc_ref, lpos_ref, gate_ref, x_ref, ys_hbm, ystage, comb, sem, *, chunk, nch):
    c = pl.program_id(0)
    slot = c % 2

    def fetch(cc, sl):
        def per_e(e, carry):
            k = cc * N_EXPERTS + e

            def one(i, carry2):
                pltpu.make_async_copy(ys_hbm.at[_piece_rows(rg_ref[k], i)], ystage.at[sl, _piece_rows(sl_ref[k], i)],
                                      sem.at[sl]).start()
                return carry2
            lax.fori_loop(0, np_ref[k], one, 0)
            return carry
        lax.fori_loop(0, N_EXPERTS, per_e, 0)

    @pl.when(c == 0)
    def _():
        fetch(0, 0)

    @pl.when(c + 1 < nch)
    def _():
        fetch(c + 1, 1 - slot)

    def wait_piece(i, carry):
        pltpu.make_async_copy(ystage.at[slot, _tile_rows(0, RUN_SLOT)], ystage.at[slot, _tile_rows(0, RUN_SLOT)],
                              sem.at[slot]).wait()
        return carry
    lax.fori_loop(0, npc_ref[c], wait_piece, 0)

    def per_tok(t, carry):
        y0 = ystage[slot, _rows_at(lpos_ref[0, 0, t]), :]
        y1 = ystage[slot, _rows_at(lpos_ref[0, 0, chunk + t]), :]
        comb[_tile_rows(t), :] = gate_ref[0, 0, t] * y0 + gate_ref[0, 0, chunk + t] * y1
        return carry
    lax.fori_loop(0, chunk, per_tok, 0, unroll=8)
    return x_ref[...] + _load_row_tiles(comb, chunk)


def _combine_glu_body(rg_ref, sl_ref, np_ref, npc_ref, lpos_ref, gate_ref, x_ref, ys_hbm, g_ref, w_ref, b_ref,
                      x2_ref, u_ref, ystage, comb, sem, *, chunk, nch, precise):
    x2 = _combine(rg_ref, sl_ref, np_ref, npc_ref, lpos_ref, gate_ref, x_ref, ys_hbm, ystage, comb, sem,
                  chunk=chunk, nch=nch)
    x2_ref[...] = x2
    zz = _mm(_rms(x2, g_ref[...]), w_ref[...], precise) + b_ref[...]
    half = zz.shape[1] // 2
    u_ref[...] = zz[:, :half] * jax.nn.sigmoid(zz[:, half:])


def _combine_final_body(rg_ref, sl_ref, np_ref, npc_ref, lpos_ref, gate_ref, x_ref, ys_hbm, g_ref, o_ref,
                        ystage, comb, sem, *, chunk, nch):
    x2 = _combine(rg_ref, sl_ref, np_ref, npc_ref, lpos_ref, gate_ref, x_ref, ys_hbm, ystage, comb, sem,
                  chunk=chunk, nch=nch)
    o_ref[...] = _rms(x2, g_ref[...])


def _combine_call(body, plan, x, ys, extra, extra_specs, out_specs, out_shape, name):
    chunk, nch = plan['chunk'], plan['nch']
    d = x.shape[1]
    smem_blk = pl.BlockSpec((1, 1, TOP_K * chunk), lambda c, *_: (c, 0, 0), memory_space=pltpu.SMEM)
    gs = pltpu.PrefetchScalarGridSpec(
        num_scalar_prefetch=4,
        grid=(nch,),
        in_specs=[smem_blk, smem_blk, pl.BlockSpec((chunk, d), lambda c, *_: (c, 0)),
                  pl.BlockSpec(memory_space=pl.ANY)] + extra_specs,
        out_specs=out_specs,
        scratch_shapes=[pltpu.VMEM((2, _stage_rows(chunk) * ROW_TILES, LANES), F32),
                        pltpu.VMEM((chunk * ROW_TILES, LANES), F32), pltpu.SemaphoreType.DMA((2,))],
    )
    return pl.pallas_call(
        functools.partial(body, chunk=chunk, nch=nch),
        grid_spec=gs,
        out_shape=out_shape,
        compiler_params=_cparams("arbitrary"),
        name=name,
    )(plan['run_g'], plan['slot_l'], plan['n_piece'], plan['n_piece_chunk'], plan['lpos'], plan['gate'], x, ys,
      *extra)


def combine_glu(x, ys, plan, g, w, b, *, precise):
    rows, d = x.shape
    chunk = plan['chunk']
    cols = w.shape[1]
    const = lambda shape: pl.BlockSpec(shape, lambda c, *_: (0,) * len(shape))
    row_spec = lambda width: pl.BlockSpec((chunk, width), lambda c, *_: (c, 0))
    return _combine_call(
        functools.partial(_combine_glu_body, precise=precise), plan, x, ys, [g, w, b],
        [const((1, d)), const((d, cols)), const((1, cols))], [row_spec(d), row_spec(cols // 2)],
        [jax.ShapeDtypeStruct((rows, d), F32), jax.ShapeDtypeStruct((rows, cols // 2), F32)], "combine_glu")


def combine_final(x, ys, plan, g):
    rows, d = x.shape
    chunk = plan['chunk']
    return _combine_call(
        _combine_final_body, plan, x, ys, [g], [pl.BlockSpec((1, d), lambda c, *_: (0, 0))],
        pl.BlockSpec((chunk, d), lambda c, *_: (c, 0)), jax.ShapeDtypeStruct((rows, d), F32), "combine_final")


CONV_TILE = 512
CONV_HIST = 32


def _ln_swish(y, g, b):
    yc = y - jnp.mean(y, axis=-1, keepdims=True)
    yn = yc * lax.rsqrt(jnp.mean(yc * yc, axis=-1, keepdims=True) + EPS) * g + b
    return yn * jax.nn.sigmoid(yn)


CONV_ROWS = 64
LN_ROWS = 16
LN_UNROLL = 8


def _dwconv_prompt_body(u_ref, w_ref, bdw_ref, g_ref, b_ref, o_ref, ext, y_sc, *, tt):
    t = pl.program_id(1)
    n_lt = ext.shape[0]

    @pl.when(t == 0)
    def _():
        ext[:, 0:CONV_HIST, :] = jnp.zeros((n_lt, CONV_HIST, LANES), F32)

    @pl.when(t > 0)
    def _():
        ext[:, 0:CONV_HIST, :] = ext[:, tt:tt + CONV_HIST, :]

    for j in range(n_lt):
        ext[j, CONV_HIST:CONV_HIST + tt, :] = u_ref[:, j * LANES:(j + 1) * LANES]
    off = CONV_HIST - (C_KERNEL - 1)
    for j in range(n_lt):
        wj = w_ref[:, j * LANES:(j + 1) * LANES]
        bj = bdw_ref[:, j * LANES:(j + 1) * LANES]
        for c in range(tt // CONV_ROWS):
            acc = ext[j, pl.ds(off + c * CONV_ROWS, CONV_ROWS), :] * wj[0:1] + bj
            for k in range(1, C_KERNEL):
                acc = acc + ext[j, pl.ds(off + k + c * CONV_ROWS, CONV_ROWS), :] * wj[k:k + 1]
            y_sc[c * CONV_ROWS:(c + 1) * CONV_ROWS, j * LANES:(j + 1) * LANES] = acc

    def ln_rows(r, carry):
        rows = pl.ds(pl.multiple_of(r * LN_ROWS, LN_ROWS), LN_ROWS)
        o_ref[rows, :] = _ln_swish(y_sc[rows, :], g_ref[...], b_ref[...]).astype(o_ref.dtype)
        return carry
    lax.fori_loop(0, tt // LN_ROWS, ln_rows, 0, unroll=LN_UNROLL)


def dwconv_prompt(u, w, b_dw, ln_g, ln_b, bsz, seq):
    tt = min(CONV_TILE, seq)
    nt = seq // tt
    d = u.shape[1]
    const = lambda shape: pl.BlockSpec(shape, lambda b, t: (0,) * len(shape))
    return pl.pallas_call(
        functools.partial(_dwconv_prompt_body, tt=tt),
        grid=(bsz, nt),
        in_specs=[pl.BlockSpec((tt, d), lambda b, t: (b * nt + t, 0)), const((C_KERNEL, d)), const((1, d)),
                  const((1, d)), const((1, d))],
        out_specs=pl.BlockSpec((tt, d), lambda b, t: (b * nt + t, 0)),
        out_shape=jax.ShapeDtypeStruct((bsz * seq, d), BF16),
        scratch_shapes=[pltpu.VMEM((d // LANES, CONV_HIST + tt, LANES), F32), pltpu.VMEM((tt, d), F32)],
        compiler_params=_cparams("arbitrary", "arbitrary"),
        name="dwconv_prompt",
    )(u, w, b_dw, ln_g, ln_b)


def _dwconv_sample_body(u_ref, buf_ref, w_ref, bdw_ref, g_ref, b_ref, o_ref, nbuf_ref, *, tb):
    w = w_ref[...]
    rows = []
    for i in range(tb):
        hist = buf_ref[i]
        ur = u_ref[i:i + 1, :]
        rows.append(jnp.sum(hist * w[:C_KERNEL - 1], axis=0, keepdims=True) + ur * w[C_KERNEL - 1:C_KERNEL])
        nbuf_ref[i] = jnp.concatenate([hist[1:], ur], axis=0)
    y = jnp.concatenate(rows, axis=0) + bdw_ref[...]
    o_ref[...] = _ln_swish(y, g_ref[...], b_ref[...])


def dwconv_sample(u, buf, w, b_dw, ln_g, ln_b):
    nb, d = u.shape
    tb = min(SAMPLE_TILE, nb)
    const = lambda shape: pl.BlockSpec(shape, lambda i: (0,) * len(shape))
    return pl.pallas_call(
        functools.partial(_dwconv_sample_body, tb=tb),
        grid=(nb // tb,),
        in_specs=[pl.BlockSpec((tb, d), lambda i: (i, 0)), pl.BlockSpec((tb, C_KERNEL - 1, d), lambda i: (i, 0, 0)),
                  const((C_KERNEL, d)), const((1, d)), const((1, d)), const((1, d))],
        out_specs=[pl.BlockSpec((tb, d), lambda i: (i, 0)), pl.BlockSpec((tb, C_KERNEL - 1, d), lambda i: (i, 0, 0))],
        out_shape=[jax.ShapeDtypeStruct((nb, d), F32), jax.ShapeDtypeStruct(buf.shape, F32)],
        compiler_params=_cparams("parallel"),
        name="dwconv_sample",
    )(u, buf, w, b_dw, ln_g, ln_b)


def _moe(h8, lpos8, gate8, runn8, wg, wu, wd, layer, *, precise):
    plan = moe_plan(lpos8, gate8, runn8, EXPERT_BLOCK_PRECISE if precise else EXPERT_BLOCK)
    xs = dispatch(h8, plan)
    return experts(xs, plan, wg, wu, wd, layer, precise=precise), plan


def _trunk(x, caches, p, *, prompt):
    bsz, seq, d = x.shape
    rows = bsz * seq
    precise = not prompt
    wdt = F32 if precise else BF16
    xf = x.reshape(rows, d)
    row = lambda v: v.reshape(1, -1).astype(F32)

    z = norm_proj(xf, row(p['norm_mix'][0]), p['w_in'].astype(wdt), precise=precise)
    if prompt:
        z, zb = z
        att = attn_prompt(z, zb, p['rel_table'], p['sinks'], bsz, seq)
        out_b, c1, n1, m1 = mlstm_prompt(z, zb, p['conv_w'], p['b_gates'], p['g_mnorm'], bsz, seq)
        z3 = z.reshape(bsz, seq, P_COLS)
        new_k = z3[:, seq - WINDOW:, P_KA:P_KA + A_KV].reshape(bsz, WINDOW, A_KV_HEADS, A_HEAD_DIM)
        new_v = z3[:, seq - WINDOW:, P_VA:P_VA + A_KV].reshape(bsz, WINDOW, A_KV_HEADS, A_HEAD_DIM)
        new_conv = z3[:, seq - (B_CONV - 1):, P_QK:P_QK + 2 * B_QK]
        n1 = n1[:, :B_HEADS]
        m1 = m1[:, :B_HEADS, 0]
    else:
        ck, cv, c0, n0, m0, cbuf = caches[:6]
        n_buf = ck.shape[1]
        att, out_b, new_k, new_v, c1, n1, m1, new_conv = mix_sample(
            z, p['rel_table'], p['sinks'], ck.reshape(bsz, n_buf, A_KV), cv.reshape(bsz, n_buf, A_KV),
            c0, n0, m0, cbuf, p['conv_w'], p['b_gates'], p['g_mnorm'])
        new_k = new_k.reshape(bsz, n_buf, A_KV_HEADS, A_HEAD_DIM)
        new_v = new_v.reshape(bsz, n_buf, A_KV_HEADS, A_HEAD_DIM)
        m1 = m1[:, :B_HEADS]
    w_out = p['w_out'].astype(wdt)
    x1, h8, lpos, gate, runn = proj_router([att, out_b], [w_out[:A_Q], w_out[A_Q:]], None, xf, row(p['norm_ffn'][0]),
                                           p['w_router'][0], p['b_router'][0], precise=precise)
    ys, plan = _moe(h8, lpos, gate, runn, p['w_eg'], p['w_eu'], p['w_ed'], 0, precise=precise)

    x2, u = combine_glu(x1, ys, plan, row(p['norm_mix'][1]), p['w_pw1'].astype(wdt), row(p['b_pw1']),
                        precise=precise)
    if prompt:
        yc = dwconv_prompt(u, p['w_dw'], row(p['b_dw']), row(p['ln_g']), row(p['ln_b']), bsz, seq)
        new_cbuf = u.reshape(bsz, seq, d)[:, seq - (C_KERNEL - 1):]
    else:
        yc, new_cbuf = dwconv_sample(u, caches[6], p['w_dw'], row(p['b_dw']), row(p['ln_g']), row(p['ln_b']))
    x3, h8, lpos, gate, runn = proj_router([yc], [p['w_pw2'].astype(wdt)], row(p['b_pw2']), x2, row(p['norm_ffn'][1]),
                                           p['w_router'][1], p['b_router'][1], precise=precise)
    ys, plan = _moe(h8, lpos, gate, runn, p['w_eg'], p['w_eu'], p['w_ed'], 1, precise=precise)
    y = combine_final(x3, ys, plan, row(p['norm_final']))
    add_layer = lambda t: t[None]
    return (y.reshape(bsz, seq, d),) + tuple(add_layer(t) for t in (new_k, new_v, c1, n1, m1, new_conv, new_cbuf))


def kernel(x_prompt, x_sample, cache_win_k, cache_win_v, state_mlstm_c, state_mlstm_n, state_mlstm_m, state_mlstm_conv, state_conv, norm_mix, norm_ffn, norm_final, rel_bias_table, w_in_mix, b_mlstm_gates, w_mlstm_qk_conv, attn_sinks, g_mlstm_norm, w_out_mix, w_pw1, b_pw1, w_dw, b_dw, ln_conv_g, ln_conv_b, w_pw2, b_pw2, w_router_group, b_router_group, w_router_expert, b_router_expert, w_expert_gate, w_expert_up, w_expert_down):
    w_in = w_in_mix[0]
    s_q, s_k, s_v, s_qk, s_vb, s_g = A_Q, A_Q + A_KV, A_Q + 2 * A_KV, A_Q + 2 * A_KV + 2 * B_QK, \
        A_Q + 2 * A_KV + 2 * B_QK + B_V, A_Q + 2 * A_KV + 2 * B_QK + B_V + 2 * B_HEADS
    w_in_r = jnp.concatenate([w_in[:, :s_q], w_in[:, s_v:s_qk], w_in[:, s_qk:s_vb], w_in[:, s_g:],
                              w_in[:, s_q:s_k], w_in[:, s_k:s_v], w_in[:, s_vb:s_g],
                              jnp.zeros((D_MODEL, LANES - 2 * B_HEADS), F32)], axis=1)
    b_gates = jnp.concatenate([b_mlstm_gates[0], jnp.zeros((LANES - 2 * B_HEADS,), F32)]).reshape(1, LANES)
    depth = w_router_group.shape[0]
    w_re = jnp.transpose(w_router_expert, (0, 2, 1, 3)).reshape(depth, D_MODEL, N_EXPERTS)
    w_router = jnp.concatenate([w_router_group, w_re,
                                jnp.zeros((depth, D_MODEL, LANES - N_GROUPS - N_EXPERTS), F32)], axis=-1)
    b_router = jnp.concatenate([b_router_group, b_router_expert.reshape(depth, N_EXPERTS),
                                jnp.zeros((depth, LANES - N_GROUPS - N_EXPERTS), F32)], axis=-1)[:, None, :]
    p = dict(norm_mix=norm_mix, norm_ffn=norm_ffn, norm_final=norm_final, rel_table=rel_bias_table,
             sinks=attn_sinks[0], w_in=w_in_r, b_gates=b_gates, conv_w=w_mlstm_qk_conv[0],
             g_mnorm=g_mlstm_norm[0].reshape(1, B_V), w_out=w_out_mix[0], w_pw1=w_pw1[0], b_pw1=b_pw1[0],
             w_dw=w_dw[0], b_dw=b_dw[0], ln_g=ln_conv_g[0], ln_b=ln_conv_b[0], w_pw2=w_pw2[0], b_pw2=b_pw2[0],
             w_router=w_router, b_router=b_router, w_eg=w_expert_gate, w_eu=w_expert_up, w_ed=w_expert_down)
    caches = (cache_win_k[0], cache_win_v[0], state_mlstm_c[0], state_mlstm_n[0], state_mlstm_m[0],
              state_mlstm_conv[0], state_conv[0])
    out_p = _trunk(x_prompt, None, p, prompt=True)
    out_s = _trunk(x_sample, caches, p, prompt=False)
    return (out_p[0], out_s[0]) + out_p[1:] + out_s[1:]
```

```python
import functools
import math

import numpy as np
import jax
import jax.numpy as jnp
from jax import lax
from jax.experimental import pallas as pl
from jax.experimental.pallas import tpu as pltpu

F32 = jnp.float32
BF16 = jnp.bfloat16
I32 = jnp.int32
HI = lax.Precision.HIGHEST
NEG_INF = float("-inf")

LANES = 128
SUBLANES = 8
VMEM_LIMIT = 56 * 1024 * 1024

D_MODEL = 1024
A_HEADS = 8
A_KV_HEADS = 2
A_GROUP = A_HEADS // A_KV_HEADS
A_HEAD_DIM = 64
WINDOW = 128
REL_BUCKETS = 32
REL_MAX_DIST = 128
B_HEADS = 4
B_DK = 64
B_DV = 128
B_CONV = 4
C_KERNEL = 31
N_GROUPS = 4
EXPERTS_PER_GROUP = 8
N_EXPERTS = N_GROUPS * EXPERTS_PER_GROUP
TOP_K = 2
EXPERT_FF = D_MODEL // 2
EXPERT_BLOCK = 256
EXPERT_BLOCK_PRECISE = 128
EPS = 1e-6

A_Q = A_HEADS * A_HEAD_DIM
A_KV = A_KV_HEADS * A_HEAD_DIM
B_QK = B_HEADS * B_DK
B_V = B_HEADS * B_DV
ROW_TILES = D_MODEL // LANES

Z_QA, Z_QK, Z_VB, Z_OG, Z_KA, Z_VA, Z_GATES = 0, 512, 1024, 1536, 2048, 2176, 2304
Z_COLS = 2432
MLSTM_CHUNK = 128


def _cparams(*sem):
    return pltpu.CompilerParams(dimension_semantics=sem, vmem_limit_bytes=VMEM_LIMIT)


def _rms(x, g):
    return x * lax.rsqrt(jnp.mean(x * x, axis=-1, keepdims=True) + EPS) * g


def _mm(a, w, precise):
    if not precise:
        return jnp.dot(a.astype(BF16), w, preferred_element_type=F32)
    a = a.astype(F32)
    a_hi = a.astype(BF16)
    a_lo = (a - a_hi.astype(F32)).astype(BF16)
    w_hi = w.astype(BF16)
    w_lo = (w - w_hi.astype(F32)).astype(BF16)
    return (jnp.dot(a_hi, w_hi, preferred_element_type=F32) + jnp.dot(a_lo, w_hi, preferred_element_type=F32)
            + jnp.dot(a_hi, w_lo, preferred_element_type=F32))


def _t5_buckets(dist):
    exact = REL_BUCKETS // 2
    d = np.maximum(dist, 0)
    large = exact + (np.log(np.maximum(d, 1).astype(np.float32) / exact)
                     / math.log(REL_MAX_DIST / exact) * (REL_BUCKETS - exact)).astype(np.int32)
    return np.where(d < exact, d, np.minimum(large, REL_BUCKETS - 1)).astype(np.int32)


P_QK, P_OG, P_KA, P_VA, P_GATES = 0, 512, 1024, 1152, 1280
P_COLS = 1408
PB_QA, PB_VB = 0, 512
PB_COLS = 1024


def _norm_proj_body(x_ref, g_ref, w_ref, *o_refs, precise):
    h = _rms(x_ref[...], g_ref[...])
    z = _mm(h, w_ref[...], precise)
    if len(o_refs) == 1:
        o_refs[0][...] = z
        return
    z32_ref, zb_ref = o_refs
    z32_ref[...] = jnp.concatenate([z[:, Z_QK:Z_QK + 2 * B_QK], z[:, Z_OG:Z_OG + B_V], z[:, Z_KA:Z_GATES + LANES]],
                                   axis=1)
    zb_ref[...] = jnp.concatenate([z[:, Z_QA:Z_QA + A_Q], z[:, Z_VB:Z_VB + B_V]], axis=1).astype(BF16)


def norm_proj(x, g, w, *, precise):
    rows, d = x.shape
    cols = w.shape[1]
    tm = min(rows, 128 if precise else 512)
    row_spec = lambda width: pl.BlockSpec((tm, width), lambda i: (i, 0))
    if precise:
        out_specs, out_shape = row_spec(cols), jax.ShapeDtypeStruct((rows, cols), F32)
    else:
        out_specs = [row_spec(P_COLS), row_spec(PB_COLS)]
        out_shape = [jax.ShapeDtypeStruct((rows, P_COLS), F32), jax.ShapeDtypeStruct((rows, PB_COLS), BF16)]
    return pl.pallas_call(
        functools.partial(_norm_proj_body, precise=precise),
        grid=(rows // tm,),
        in_specs=[row_spec(d), pl.BlockSpec((1, d), lambda i: (0, 0)), pl.BlockSpec((d, cols), lambda i: (0, 0))],
        out_specs=out_specs,
        out_shape=out_shape,
        compiler_params=_cparams("parallel"),
        name="norm_proj",
    )(x, g, w)


def _attn_prompt_body(tab_ref, sink_ref, bkt_ref, q_ref, kp_ref, kc_ref, vp_ref, vc_ref, o_ref, bias_ref, *, nq):
    b = pl.program_id(0)
    n = pl.program_id(1)

    @pl.when((b == 0) & (n == 0))
    def _():
        bk = bkt_ref[...]
        first = lax.broadcasted_iota(I32, bk.shape, 1) >= WINDOW
        for h in range(A_HEADS):
            acc = jnp.full(bk.shape, NEG_INF, F32)
            for t in range(REL_BUCKETS):
                acc = jnp.where(bk == t, tab_ref[t, h], acc)
            bias_ref[h] = acc
            bias_ref[A_HEADS + h] = jnp.where(first, acc, NEG_INF)

    k_all = jnp.concatenate([kp_ref[...], kc_ref[...]], axis=0).astype(BF16)
    v_all = jnp.concatenate([vp_ref[...], vc_ref[...]], axis=0).astype(BF16)
    for sub in range(nq):
        rows = slice(sub * WINDOW, (sub + 1) * WINDOW)
        q = q_ref[rows, :] * (A_HEAD_DIM ** -0.5)
        kb = k_all[sub * WINDOW:(sub + 2) * WINDOW]
        vb = v_all[sub * WINDOW:(sub + 2) * WINDOW]
        table = jnp.where(n == 0, A_HEADS, 0) if sub == 0 else 0
        outs = []
        for h in range(A_HEADS):
            kvh = h // A_GROUP
            qh = q[:, h * A_HEAD_DIM:(h + 1) * A_HEAD_DIM].astype(BF16)
            kh = kb[:, kvh * A_HEAD_DIM:(kvh + 1) * A_HEAD_DIM]
            vh = vb[:, kvh * A_HEAD_DIM:(kvh + 1) * A_HEAD_DIM]
            s = lax.dot_general(qh, kh, (((1,), (1,)), ((), ())), preferred_element_type=F32)
            s = s + bias_ref[table + h]
            sink = sink_ref[h]
            mx = jnp.maximum(jnp.max(s, axis=-1, keepdims=True), sink)
            p = jnp.exp(s - mx)
            den = jnp.sum(p, axis=-1, keepdims=True) + jnp.exp(sink - mx)
            outs.append(jnp.dot(p.astype(BF16), vh, preferred_element_type=F32) / den)
        o_ref[rows, :] = jnp.concatenate(outs, axis=1).astype(o_ref.dtype)


ATTN_BLOCKS = 2


def attn_prompt(z, zb, rel_table, sinks, bsz, seq):
    nb = seq // WINDOW
    dist = WINDOW + np.arange(WINDOW)[:, None] - np.arange(2 * WINDOW)[None, :]
    bkt = np.where((dist >= 0) & (dist <= WINDOW), _t5_buckets(dist), -1).astype(np.int32)
    kcol, vcol = P_KA // LANES, P_VA // LANES
    smem = pl.BlockSpec(memory_space=pltpu.SMEM)
    nq = ATTN_BLOCKS if nb % ATTN_BLOCKS == 0 else 1
    ns = nb // nq

    def cur(c):
        return pl.BlockSpec((nq * WINDOW, LANES), lambda b, n: (b * ns + n, c))

    def prev(c):
        return pl.BlockSpec((WINDOW, LANES), lambda b, n: (b * nb + jnp.maximum(n * nq - 1, 0), c))

    return pl.pallas_call(
        functools.partial(_attn_prompt_body, nq=nq),
        grid=(bsz, ns),
        in_specs=[smem, smem,
                  pl.BlockSpec((WINDOW, 2 * WINDOW), lambda b, n: (0, 0)),
                  pl.BlockSpec((nq * WINDOW, A_Q), lambda b, n: (b * ns + n, PB_QA // A_Q)),
                  prev(kcol), cur(kcol), prev(vcol), cur(vcol)],
        out_specs=pl.BlockSpec((nq * WINDOW, A_Q), lambda b, n: (b * ns + n, 0)),
        out_shape=jax.ShapeDtypeStruct((bsz * seq, A_Q), BF16),
        scratch_shapes=[pltpu.VMEM((2 * A_HEADS, WINDOW, 2 * WINDOW), F32)],
        compiler_params=_cparams("arbitrary", "arbitrary"),
        name="attn_prompt",
    )(rel_table, sinks, jnp.asarray(bkt), zb, z, z, z, z)


def _log_sigmoid(x):
    return -(jnp.maximum(-x, 0.0) + jnp.log1p(jnp.exp(-jnp.abs(x))))


def _mlstm_prompt_body(qk_ref, v_ref, g_ref, og_ref, cw_ref, bg_ref, gn_ref,
                       ob_ref, c_out, n_out, m_out,
                       c_sc, n_sc, m_sc, hist_sc, *, L, nc, nbat):
    ci = pl.program_id(1)

    @pl.when(ci == 0)
    def _():
        c_sc[...] = jnp.zeros(c_sc.shape, F32)
        n_sc[...] = jnp.zeros(n_sc.shape, F32)
        m_sc[...] = jnp.full(m_sc.shape, NEG_INF, F32)
        hist_sc[...] = jnp.zeros(hist_sc.shape, F32)

    row = lax.broadcasted_iota(I32, (L, L), 0)
    colm = lax.broadcasted_iota(I32, (L, L), 1)
    causal = colm <= row
    tril = causal.astype(F32)
    stores = []
    for bb in range(nbat):
        stores += _mlstm_chunk(qk_ref.at[bb], v_ref.at[bb], g_ref.at[bb], og_ref.at[bb], cw_ref, bg_ref, gn_ref,
                               ob_ref.at[bb], c_sc.at[bb], n_sc.at[bb], m_sc.at[bb], hist_sc.at[bb],
                               causal, tril, L)
    for store in stores:
        store()

    @pl.when(ci == nc - 1)
    def _():
        c_out[...] = c_sc[...]
        n_out[...] = n_sc[...]
        m_out[...] = m_sc[...]


def _mlstm_chunk(qk_ref, v_ref, g_ref, og_ref, cw_ref, bg_ref, gn_ref, ob_ref, c_sc, n_sc, m_sc, hist_sc,
                 causal, tril, L):
    cur = qk_ref[...]
    ext = jnp.concatenate([hist_sc[...], cur], axis=0)
    cw = cw_ref[...]
    off = SUBLANES - (B_CONV - 1)
    conv = ext[off:off + L] * cw[0:1]
    for j in range(1, B_CONV):
        conv = conv + ext[off + j:off + j + L] * cw[j:j + 1]
    qk = conv * jax.nn.sigmoid(conv)
    q_all = qk[:, :B_QK]
    k_all = qk[:, B_QK:] * (B_DK ** -0.5)
    k_t = k_all.T
    v_all = v_ref[...]
    og = og_ref[...]
    gn = gn_ref[...]

    G = g_ref[...] + bg_ref[...]
    lf = _log_sigmoid(G)
    Bc = jnp.dot(tril, lf, precision=HI, preferred_element_type=F32)
    BT = Bc.T
    GT = G.T

    outs, stores = [], []
    for h in range(B_HEADS):
        qh = q_all[:, h * B_DK:(h + 1) * B_DK]
        kh = k_all[:, h * B_DK:(h + 1) * B_DK]
        vh = v_all[:, h * B_DV:(h + 1) * B_DV]
        b_col = Bc[:, B_HEADS + h:B_HEADS + h + 1]
        ig_col = G[:, h:h + 1]
        b_row = BT[B_HEADS + h:B_HEADS + h + 1, :]
        ig_row = GT[h:h + 1, :]
        c0 = c_sc[h]
        n0 = n_sc[h:h + 1, :]
        m0 = m_sc[h:h + 1, 0:1]
        a = b_col + m0
        d = jnp.where(causal, b_col - b_row + ig_row, NEG_INF)
        m = jnp.maximum(a, jnp.max(d, axis=-1, keepdims=True))
        dw = jnp.exp(d - m)
        aw = jnp.exp(a - m)
        qb = qh.astype(BF16)
        vb = vh.astype(BF16)
        s = lax.dot_general(qb, kh.astype(BF16), (((1,), (1,)), ((), ())), preferred_element_type=F32) * dw
        num = (jnp.dot(s.astype(BF16), vb, preferred_element_type=F32)
               + aw * jnp.dot(qb, c0.astype(BF16), preferred_element_type=F32))
        den = jnp.sum(s, axis=-1, keepdims=True) + aw * jnp.sum(qh * n0, axis=-1, keepdims=True)
        hh = num / jnp.maximum(jnp.abs(den), jnp.exp(-m))
        m_last = m[L - 1:L, :]
        wl = jnp.exp(b_col[L - 1:L, :] - b_col + ig_col - m_last)
        decay = aw[L - 1:L, :]
        kw_t = k_t[h * B_DK:(h + 1) * B_DK, :] * dw[L - 1:L, :]
        c1 = decay * c0 + jnp.dot(kw_t.astype(BF16), vb, preferred_element_type=F32)
        n1 = decay * n0 + jnp.sum(kh * wl, axis=0, keepdims=True)
        stores.append(functools.partial(_store_state, c_sc, n_sc, m_sc, h, c1, n1, m_last))
        hn = hh * lax.rsqrt(jnp.mean(hh * hh, axis=-1, keepdims=True) + EPS) * gn[:, h * B_DV:(h + 1) * B_DV]
        outs.append(jax.nn.sigmoid(og[:, h * B_DV:(h + 1) * B_DV]) * hn)
    out = jnp.concatenate(outs, axis=1).astype(ob_ref.dtype)
    stores.append(functools.partial(_store_chunk, ob_ref, hist_sc, out, cur[L - SUBLANES:L]))
    return stores


def _store_state(c_sc, n_sc, m_sc, h, c1, n1, m_last):
    c_sc[h] = c1
    n_sc[h:h + 1, :] = n1
    m_sc[h:h + 1, :] = jnp.broadcast_to(m_last, (1, LANES))


def _store_chunk(ob_ref, hist_sc, out, tail):
    ob_ref[...] = out
    hist_sc[...] = tail


MLSTM_BATCH = 1


def mlstm_prompt(z, zb, conv_w, b_gates_pad, g_norm, bsz, seq):
    L = MLSTM_CHUNK
    nc = seq // L
    nbat = MLSTM_BATCH if bsz % MLSTM_BATCH == 0 else 1
    z3 = z.reshape(bsz, seq, P_COLS)
    zb3 = zb.reshape(bsz, seq, PB_COLS)

    def zspec(width, colblk):
        return pl.BlockSpec((nbat, L, width), lambda b, c: (b, c, colblk))

    const = lambda shape: pl.BlockSpec(shape, lambda b, c: (0,) * len(shape))
    state = lambda shape: pl.BlockSpec((nbat,) + shape, lambda b, c: (b,) + (0,) * len(shape))
    out_b, c1, n1, m1 = pl.pallas_call(
        functools.partial(_mlstm_prompt_body, L=L, nc=nc, nbat=nbat),
        grid=(bsz // nbat, nc),
        in_specs=[zspec(2 * B_QK, P_QK // (2 * B_QK)), zspec(B_V, PB_VB // B_V), zspec(LANES, P_GATES // LANES),
                  zspec(B_V, P_OG // B_V), const((B_CONV, 2 * B_QK)), const((1, LANES)), const((1, B_V))],
        out_specs=[pl.BlockSpec((nbat, L, B_V), lambda b, c: (b, c, 0)),
                   state((B_HEADS, B_DK, B_DV)), state((SUBLANES, B_DK)), state((SUBLANES, LANES))],
        out_shape=[jax.ShapeDtypeStruct((bsz, seq, B_V), BF16),
                   jax.ShapeDtypeStruct((bsz, B_HEADS, B_DK, B_DV), F32),
                   jax.ShapeDtypeStruct((bsz, SUBLANES, B_DK), F32),
                   jax.ShapeDtypeStruct((bsz, SUBLANES, LANES), F32)],
        scratch_shapes=[pltpu.VMEM((nbat, B_HEADS, B_DK, B_DV), F32), pltpu.VMEM((nbat, SUBLANES, B_DK), F32),
                        pltpu.VMEM((nbat, SUBLANES, LANES), F32), pltpu.VMEM((nbat, SUBLANES, 2 * B_QK), F32)],
        compiler_params=_cparams("arbitrary", "arbitrary"),
        name="mlstm_prompt",
    )(z3, zb3, z3, z3, conv_w, b_gates_pad, g_norm)
    return out_b.reshape(bsz * seq, B_V), c1, n1, m1


SAMPLE_TILE = 8


def _mix_sample_body(tab_ref, sink_ref, bkt_ref, z_ref, ck_ref, cv_ref, c0_ref, n0_ref, m0_ref, cb_ref,
                     cw_ref, bg_ref, gn_ref,
                     att_ref, ob_ref, nk_ref, nv_ref, c1_ref, n1_ref, m1_ref, ncb_ref, *, tb):
    bk = bkt_ref[...]
    cw = cw_ref[...]
    gn = gn_ref[...]
    bias_rows = []
    for h in range(A_HEADS):
        bias = jnp.zeros(bk.shape, F32)
        for t in range(REL_BUCKETS):
            bias = jnp.where(bk == t, tab_ref[t, h], bias)
        bias_rows.append(bias)
    n_keys = bk.shape[1]
    rows, cols = A_HEADS * tb, tb * n_keys
    z_all = z_ref[...]
    q_all = z_all[:, Z_QA:Z_QA + A_Q] * (A_HEAD_DIM ** -0.5)
    ka_all = z_all[:, Z_KA:Z_KA + A_KV]
    va_all = z_all[:, Z_VA:Z_VA + A_KV]
    zero_half = jnp.zeros((tb, A_HEAD_DIM), F32)
    q_rows = []
    for h in range(A_HEADS):
        qh = q_all[:, h * A_HEAD_DIM:(h + 1) * A_HEAD_DIM]
        q_rows.append(jnp.concatenate([qh, zero_half] if h // A_GROUP == 0 else [zero_half, qh], axis=1))
    qm = jnp.concatenate(q_rows, axis=0)
    ka_rep = jnp.concatenate([ka_all] * A_HEADS, axis=0)
    va_rep = jnp.concatenate([va_all] * A_HEADS, axis=0)
    k_cat = jnp.concatenate([ck_ref[i] for i in range(tb)], axis=0)
    v_cat = jnp.concatenate([cv_ref[i] for i in range(tb)], axis=0)
    bias_c = jnp.concatenate([jnp.broadcast_to(b, (tb, n_keys)) for b in bias_rows], axis=0)
    bias_c = jnp.concatenate([bias_c] * tb, axis=1)
    r_id = lax.broadcasted_iota(I32, (rows, cols), 0)
    c_id = lax.broadcasted_iota(I32, (rows, cols), 1)
    own = (r_id & (tb - 1)) == (c_id >> (n_keys.bit_length() - 1))
    head_id = lax.broadcasted_iota(I32, (rows, 1), 0) >> (tb.bit_length() - 1)
    bias_n = jnp.zeros((rows, 1), F32)
    sinks = jnp.zeros((rows, 1), F32)
    for h in range(A_HEADS):
        bias_n = jnp.where(head_id == h, tab_ref[0, h], bias_n)
        sinks = jnp.where(head_id == h, sink_ref[h], sinks)
    lc = lax.dot_general(qm, k_cat, (((1,), (1,)), ((), ())), precision=HI, preferred_element_type=F32)
    lc = jnp.where(own, lc + bias_c, NEG_INF)
    ln = jnp.sum(qm * ka_rep, axis=-1, keepdims=True) + bias_n
    mx = jnp.maximum(jnp.maximum(jnp.max(lc, axis=-1, keepdims=True), ln), sinks)
    pc = jnp.exp(lc - mx)
    pn = jnp.exp(ln - mx)
    den = jnp.sum(pc, axis=-1, keepdims=True) + pn + jnp.exp(sinks - mx)
    o = (jnp.dot(pc, v_cat, precision=HI, preferred_element_type=F32) + pn * va_rep) / den
    att_ref[...] = jnp.concatenate(
        [o[h * tb:(h + 1) * tb, (h // A_GROUP) * A_HEAD_DIM:(h // A_GROUP + 1) * A_HEAD_DIM]
         for h in range(A_HEADS)], axis=1)

    for i in range(tb):
        nk_ref[i] = jnp.concatenate([ck_ref[i][1:], ka_all[i:i + 1]], axis=0)
        nv_ref[i] = jnp.concatenate([cv_ref[i][1:], va_all[i:i + 1]], axis=0)

    qk_pre = z_all[:, Z_QK:Z_QK + 2 * B_QK]
    hist = [cb_ref[i] for i in range(tb)]
    conv = qk_pre * cw[B_CONV - 1:B_CONV]
    for j in range(B_CONV - 1):
        conv = conv + jnp.concatenate([hs[j:j + 1] for hs in hist], axis=0) * cw[j:j + 1]
    for i in range(tb):
        ncb_ref[i] = jnp.concatenate([hist[i][1:], qk_pre[i:i + 1]], axis=0)
    qk = conv * jax.nn.sigmoid(conv)
    q_m = qk[:, :B_QK]
    k_m = qk[:, B_QK:] * (B_DK ** -0.5)
    v_m = z_all[:, Z_VB:Z_VB + B_V]
    og = z_all[:, Z_OG:Z_OG + B_V]
    G = z_all[:, Z_GATES:Z_GATES + LANES] + bg_ref[...]
    ig = G[:, :B_HEADS]
    a = _log_sigmoid(G)[:, B_HEADS:2 * B_HEADS] + m0_ref[...]
    m = jnp.maximum(a, ig)
    dw = jnp.exp(ig - m)
    aw = jnp.exp(a - m)
    li = lax.broadcasted_iota(I32, (B_QK, LANES), 0) >> (B_DK.bit_length() - 1)
    ind = (li == lax.broadcasted_iota(I32, (B_QK, LANES), 1)).astype(F32)
    n0_m = jnp.concatenate([jnp.concatenate([n0_ref[i, h:h + 1, :] for h in range(B_HEADS)], axis=1)
                            for i in range(tb)], axis=0)
    qk_dot = jnp.dot(q_m * k_m, ind, precision=HI, preferred_element_type=F32)[:, :B_HEADS]
    qn_dot = jnp.dot(q_m * n0_m, ind, precision=HI, preferred_element_type=F32)[:, :B_HEADS]
    s = qk_dot * dw
    inv = 1.0 / jnp.maximum(jnp.abs(s + aw * qn_dot), jnp.exp(-m))
    m1_ref[...] = jnp.concatenate([m, jnp.zeros((tb, LANES - B_HEADS), F32)], axis=1)
    stack = jnp.concatenate([q_m[:, h * B_DK:(h + 1) * B_DK] for h in range(B_HEADS)]
                            + [k_m[:, h * B_DK:(h + 1) * B_DK] for h in range(B_HEADS)], axis=0)
    cols = jnp.concatenate([stack, jnp.zeros_like(stack)], axis=1).T[:B_DK]
    hh_rows = [[] for _ in range(B_HEADS)]
    for i in range(tb):
        for h in range(B_HEADS):
            jq = h * tb + i
            jk = (B_HEADS + h) * tb + i
            c0 = c0_ref[i, h]
            vh = v_m[i:i + 1, h * B_DV:(h + 1) * B_DV]
            aw_p, dw_p = aw[i:i + 1, h:h + 1], dw[i:i + 1, h:h + 1]
            num = s[i:i + 1, h:h + 1] * vh + aw_p * jnp.sum(cols[:, jq:jq + 1] * c0, axis=0, keepdims=True)
            hh_rows[h].append(num * inv[i:i + 1, h:h + 1])
            c1_ref[i, h] = aw_p * c0 + dw_p * (cols[:, jk:jk + 1] * vh)
            n1_ref[i, h:h + 1, :] = aw_p * n0_ref[i, h:h + 1, :] + dw_p * k_m[i:i + 1, h * B_DK:(h + 1) * B_DK]
    obs = []
    for h in range(B_HEADS):
        hh = jnp.concatenate(hh_rows[h], axis=0)
        hn = hh * lax.rsqrt(jnp.mean(hh * hh, axis=-1, keepdims=True) + EPS) * gn[:, h * B_DV:(h + 1) * B_DV]
        obs.append(jax.nn.sigmoid(og[:, h * B_DV:(h + 1) * B_DV]) * hn)
    ob_ref[...] = jnp.concatenate(obs, axis=1)


def mix_sample(z, rel_table, sinks, ck, cv, c0, n0, m0, conv_buf, conv_w, b_gates_pad, g_norm):
    nb = z.shape[0]
    tb = min(SAMPLE_TILE, nb)
    n_buf = ck.shape[1]
    bkt = _t5_buckets(n_buf - np.arange(n_buf))[None, :]
    smem = pl.BlockSpec(memory_space=pltpu.SMEM)
    const = lambda shape: pl.BlockSpec(shape, lambda i: (0,) * len(shape))
    lead = lambda shape: pl.BlockSpec((tb,) + shape, lambda i: (i,) + (0,) * len(shape))
    return pl.pallas_call(
        functools.partial(_mix_sample_body, tb=tb),
        grid=(nb // tb,),
        in_specs=[smem, smem, const((1, n_buf)), lead((Z_COLS,)), lead((n_buf, A_KV)), lead((n_buf, A_KV)),
                  lead((B_HEADS, B_DK, B_DV)), lead((B_HEADS, B_DK)), lead((B_HEADS,)),
                  lead((B_CONV - 1, 2 * B_QK)), const((B_CONV, 2 * B_QK)), const((1, LANES)), const((1, B_V))],
        out_specs=[lead((A_Q,)), lead((B_V,)), lead((n_buf, A_KV)), lead((n_buf, A_KV)),
                   lead((B_HEADS, B_DK, B_DV)), lead((B_HEADS, B_DK)), lead((LANES,)),
                   lead((B_CONV - 1, 2 * B_QK))],
        out_shape=[jax.ShapeDtypeStruct((nb, A_Q), F32), jax.ShapeDtypeStruct((nb, B_V), F32),
                   jax.ShapeDtypeStruct(ck.shape, F32), jax.ShapeDtypeStruct(cv.shape, F32),
                   jax.ShapeDtypeStruct(c0.shape, F32), jax.ShapeDtypeStruct(n0.shape, F32),
                   jax.ShapeDtypeStruct((nb, LANES), F32), jax.ShapeDtypeStruct(conv_buf.shape, F32)],
        compiler_params=_cparams("parallel"),
        name="mix_sample",
    )(rel_table, sinks, jnp.asarray(bkt), z, ck, cv, c0, n0, m0, conv_buf, conv_w, b_gates_pad, g_norm)


def _store_row_tiles(ref, val, rows):
    for s in range(ROW_TILES):
        ref[pl.ds(s, rows, stride=ROW_TILES), :] = val[:, s * LANES:(s + 1) * LANES]


def _load_row_tiles(ref, rows, start=0, stride=ROW_TILES):
    return jnp.concatenate([ref[pl.ds(start + s, rows, stride=stride), :] for s in range(ROW_TILES)], axis=1)


def _route(logits):
    lane = lax.broadcasted_iota(I32, logits.shape, 1)
    big = jnp.int32(1 << 20)
    gl = jnp.where(lane < N_GROUPS, logits, NEG_INF)
    gmax = jnp.max(gl, axis=-1, keepdims=True)
    gidx = jnp.min(jnp.where(gl == gmax, lane, big), axis=-1, keepdims=True)
    g_gate = 1.0 / jnp.sum(jnp.exp(gl - gmax), axis=-1, keepdims=True)
    lo = N_GROUPS + gidx * EXPERTS_PER_GROUP
    el = jnp.where((lane >= lo) & (lane < lo + EXPERTS_PER_GROUP), logits, NEG_INF)
    v1 = jnp.max(el, axis=-1, keepdims=True)
    i1 = jnp.min(jnp.where(el == v1, lane, big), axis=-1, keepdims=True)
    el2 = jnp.where(lane == i1, NEG_INF, el)
    v2 = jnp.max(el2, axis=-1, keepdims=True)
    i2 = jnp.min(jnp.where(el2 == v2, lane, big), axis=-1, keepdims=True)
    t = jnp.exp(v2 - v1)
    w1 = g_gate / (1.0 + t)
    w2 = g_gate * t / (1.0 + t)
    gate = jnp.where(lane == 0, w1, jnp.where(lane == 1, w2, 0.0))
    return i1 - N_GROUPS, i2 - N_GROUPS, gate


def _local_sort(e0, e1):
    tm = e0.shape[0]
    lane = lax.broadcasted_iota(I32, (tm, LANES), 1)
    oh0 = (lane == e0).astype(BF16)
    oh1 = (lane == e1).astype(BF16)
    r = lax.broadcasted_iota(I32, (tm, tm), 0)
    c = lax.broadcasted_iota(I32, (tm, tm), 1)
    before = (c < r).astype(BF16)
    cnt0 = jnp.sum(oh0.astype(F32), axis=0, keepdims=True)
    run_n = cnt0 + jnp.sum(oh1.astype(F32), axis=0, keepdims=True)
    er = lax.broadcasted_iota(I32, (LANES, LANES), 0)
    ec = lax.broadcasted_iota(I32, (LANES, LANES), 1)
    slot_n = jnp.floor((run_n + (RUN_SLOT - 1)) * (1.0 / RUN_SLOT)) * RUN_SLOT
    run_l = jnp.dot(slot_n, (er < ec).astype(F32), precision=HI, preferred_element_type=F32)
    w0 = jnp.dot(before, oh0, preferred_element_type=F32) + run_l
    w1 = jnp.dot(before, oh1, preferred_element_type=F32) + run_l + cnt0
    p0 = jnp.sum(jnp.where(lane == e0, w0, 0.0), axis=-1, keepdims=True)
    p1 = jnp.sum(jnp.where(lane == e1, w1, 0.0), axis=-1, keepdims=True)
    return jnp.where(lane == 0, p0, jnp.where(lane == 1, p1, 0.0)), run_n


MOE_CHUNK = 512
RUN_SLOT = 32


def _moe_chunk(rows, precise):
    return min(rows, 128 if precise else MOE_CHUNK)


def _proj_router_body(*refs, n_in, has_bias, precise, tm):
    a_refs = refs[:n_in]
    w_refs = refs[n_in:2 * n_in]
    k = 2 * n_in
    bias_ref = refs[k] if has_bias else None
    k += 1 if has_bias else 0
    x_ref, g_ref, wr_ref, br_ref, x1_ref, h8_ref, lpos_ref, gate_ref, runn_ref = refs[k:]
    acc = x_ref[...]
    if has_bias:
        acc = acc + bias_ref[...]
    for a_ref, w_ref in zip(a_refs, w_refs):
        acc = acc + _mm(a_ref[...], w_ref[...], precise)
    x1_ref[...] = acc
    h = _rms(acc, g_ref[...])
    _store_row_tiles(h8_ref, h, tm)
    wr = wr_ref[...]
    if precise:
        logits = jnp.dot(h, wr, precision=HI, preferred_element_type=F32)
    else:
        h_hi = h.astype(BF16)
        h_lo = (h - h_hi.astype(F32)).astype(BF16)
        w_hi = wr.astype(BF16)
        w_lo = (wr - w_hi.astype(F32)).astype(BF16)
        logits = (jnp.dot(h_hi, w_hi, preferred_element_type=F32) + jnp.dot(h_lo, w_hi, preferred_element_type=F32)
                  + jnp.dot(h_hi, w_lo, preferred_element_type=F32))
    e0, e1, gate = _route(logits + br_ref[...])
    lpos, run_n = _local_sort(e0, e1)
    lpos_ref[...] = (lpos.T[:SUBLANES] * ROW_TILES).astype(I32)
    gate_ref[...] = gate.T[:SUBLANES]
    runn_ref[...] = jnp.broadcast_to(run_n, runn_ref.shape).astype(I32)


def proj_router(a_list, w_list, bias, x, g, wr, br, *, precise):
    rows, d = x.shape
    tm = _moe_chunk(rows, precise)
    n_in = len(a_list)
    row_spec = lambda width: pl.BlockSpec((tm, width), lambda i: (i, 0))
    const = lambda shape: pl.BlockSpec(shape, lambda i: (0,) * len(shape))
    in_specs = [row_spec(a.shape[1]) for a in a_list] + [const(w.shape) for w in w_list]
    args = list(a_list) + list(w_list)
    if bias is not None:
        in_specs.append(const((1, d)))
        args.append(bias)
    in_specs += [row_spec(d), const((1, d)), const((d, LANES)), const((1, LANES))]
    args += [x, g, wr, br]
    return pl.pallas_call(
        functools.partial(_proj_router_body, n_in=n_in, has_bias=bias is not None, precise=precise, tm=tm),
        grid=(rows // tm,),
        in_specs=in_specs,
        out_specs=[row_spec(d), pl.BlockSpec((tm * ROW_TILES, LANES), lambda i: (i, 0)),
                   pl.BlockSpec((SUBLANES, tm), lambda i: (i, 0)), pl.BlockSpec((SUBLANES, tm), lambda i: (i, 0)),
                   pl.BlockSpec((SUBLANES, LANES), lambda i: (i, 0))],
        out_shape=[jax.ShapeDtypeStruct((rows, d), F32), jax.ShapeDtypeStruct((rows * ROW_TILES, LANES), F32),
                   jax.ShapeDtypeStruct((rows // tm * SUBLANES, tm), I32),
                   jax.ShapeDtypeStruct((rows // tm * SUBLANES, tm), F32),
                   jax.ShapeDtypeStruct((rows // tm * SUBLANES, LANES), I32)],
        compiler_params=_cparams("parallel"),
        name="proj_router",
    )(*args)


def _rows_at(offset):
    return pl.ds(pl.multiple_of(offset, ROW_TILES), ROW_TILES)


def _tile_rows(r, n=1):
    return pl.ds(pl.multiple_of(r * ROW_TILES, ROW_TILES), n * ROW_TILES)


def _pow2_pieces(limit):
    p = 1
    while p * 2 <= limit:
        p *= 2
    out = []
    while p >= 1:
        out.append(p)
        p //= 2
    return out


COMMON_PIECE = 32


def _for_each_piece(n, pieces, fn):
    def emit(ps):
        for p in ps:
            @pl.when((n & p) != 0)
            def _(p=p):
                fn(n & ~(2 * p - 1), p)

    big = [p for p in pieces if p > COMMON_PIECE]
    if big:
        @pl.when(n > 2 * COMMON_PIECE - 1)
        def _():
            emit(big)
    emit([p for p in pieces if p <= COMMON_PIECE])


def _piece_rows(first_row, i):
    return _tile_rows(first_row + i * RUN_SLOT, RUN_SLOT)


def _dispatch_body(rg_ref, sl_ref, np_ref, npc_ref, ps_ref, pn_ref, tail_ref, lpos_ref, h8_ref, xs_hbm,
                   stage, zbuf, sem, zsem, *, chunk, nch, blk):
    c = pl.program_id(0)
    slot = c % 2
    pad_pieces = _pow2_pieces(blk + RUN_SLOT - 1)

    def wait_pieces(count, buf):
        def one(i, carry):
            pltpu.make_async_copy(stage.at[buf, _tile_rows(0, RUN_SLOT)], stage.at[buf, _tile_rows(0, RUN_SLOT)],
                                  sem).wait()
            return carry
        lax.fori_loop(0, count, one, 0)

    @pl.when(c == 0)
    def _():
        stage[...] = jnp.zeros(stage.shape, F32)

    def copy_tok(t, carry):
        row = h8_ref[_tile_rows(t), :]
        stage[slot, _rows_at(lpos_ref[0, 0, t]), :] = row
        stage[slot, _rows_at(lpos_ref[0, 0, chunk + t]), :] = row
        return carry
    lax.fori_loop(0, chunk, copy_tok, 0, unroll=8)

    @pl.when(c > 0)
    def _():
        wait_pieces(npc_ref[jnp.maximum(c - 1, 0)], 1 - slot)

    def send_runs(e, carry):
        k = c * N_EXPERTS + e

        def one(i, carry2):
            pltpu.make_async_copy(stage.at[slot, _piece_rows(sl_ref[k], i)], xs_hbm.at[_piece_rows(rg_ref[k], i)],
                                  sem).start()
            return carry2
        lax.fori_loop(0, np_ref[k], one, 0)
        return carry
    lax.fori_loop(0, N_EXPERTS, send_runs, 0)

    @pl.when(c == nch - 1)
    def _():
        wait_pieces(npc_ref[c], slot)
        zbuf[...] = jnp.zeros(zbuf.shape, F32)

        def pad_dmas(e, op):
            def one(off, p):
                cp = pltpu.make_async_copy(zbuf.at[_tile_rows(0, p)], xs_hbm.at[_tile_rows(ps_ref[e] + off, p)], zsem)
                cp.start() if op == 0 else cp.wait()
            _for_each_piece(pn_ref[e], pad_pieces, one)

        def issue(e, carry):
            pad_dmas(e, 0)
            return carry

        def wait(e, carry):
            pad_dmas(e, 1)
            return carry
        lax.fori_loop(0, N_EXPERTS, issue, 0)
        lax.fori_loop(0, N_EXPERTS, wait, 0)

        half = blk // 2

        def tail_dmas(i, op):
            cp = pltpu.make_async_copy(zbuf.at[_tile_rows(0, half)],
                                       xs_hbm.at[_tile_rows(tail_ref[0] + i * half, half)], zsem)
            cp.start() if op == 0 else cp.wait()

        def tail_issue(i, carry):
            tail_dmas(i, 0)
            return carry

        def tail_wait(i, carry):
            tail_dmas(i, 1)
            return carry
        lax.fori_loop(0, tail_ref[1], tail_issue, 0)
        lax.fori_loop(0, tail_ref[1], tail_wait, 0)


def _stage_rows(chunk):
    return TOP_K * chunk + N_EXPERTS * (RUN_SLOT - 1) // RUN_SLOT * RUN_SLOT + RUN_SLOT


def dispatch(h8, plan):
    chunk, nch, blk = plan['chunk'], plan['nch'], plan['blk']
    n_slots = plan['nblk'] * blk
    gs = pltpu.PrefetchScalarGridSpec(
        num_scalar_prefetch=7,
        grid=(nch,),
        in_specs=[pl.BlockSpec((1, 1, TOP_K * chunk), lambda c, *_: (c, 0, 0), memory_space=pltpu.SMEM),
                  pl.BlockSpec((chunk * ROW_TILES, LANES), lambda c, *_: (c, 0))],
        out_specs=pl.BlockSpec(memory_space=pl.ANY),
        scratch_shapes=[pltpu.VMEM((2, _stage_rows(chunk) * ROW_TILES, LANES), F32),
                        pltpu.VMEM((blk * ROW_TILES, LANES), F32),
                        pltpu.SemaphoreType.DMA(()), pltpu.SemaphoreType.DMA(())],
    )
    return pl.pallas_call(
        functools.partial(_dispatch_body, chunk=chunk, nch=nch, blk=blk),
        grid_spec=gs,
        out_shape=jax.ShapeDtypeStruct((n_slots * ROW_TILES, LANES), F32),
        compiler_params=_cparams("arbitrary"),
        name="dispatch",
    )(plan['run_g'], plan['slot_l'], plan['n_piece'], plan['n_piece_chunk'], plan['pad_start'], plan['pad_n'],
      plan['tail'], plan['lpos'], h8)


FF_CHUNK = 512


def _experts_body(be_ref, nv_ref, xs_ref, wg_ref, wu_ref, wd_ref, ys_ref, xb, *wcast, precise, blk):
    j = pl.program_id(0)
    nv = nv_ref[0]
    slot = j % 2
    k = j - 1

    @pl.when(j == 0)
    def _():
        xb[1] = jnp.zeros(xb.shape[1:], xb.dtype)

    if not precise:
        wgb, wub, wdb = wcast

        @pl.when((k == 0) | ((k > 0) & (k < nv) & (be_ref[jnp.maximum(k, 0)] != be_ref[jnp.maximum(k - 1, 0)])))
        def _():
            wgb[...] = wg_ref[0].astype(BF16)
            wub[...] = wu_ref[0].astype(BF16)
            wdb[...] = wd_ref[0].astype(BF16)

    @pl.when(j <= nv)
    def _():
        xb[slot] = _load_row_tiles(xs_ref, blk).astype(xb.dtype)
        if precise:
            wg, wu, wd = wg_ref[0], wu_ref[0], wd_ref[0]
        else:
            wg, wu, wd = wgb, wub, wdb
        xm = xb[1 - slot]
        gt = _mm(xm, wg[...], precise)
        up = _mm(xm, wu[...], precise)
        _store_row_tiles(ys_ref, _mm(gt * jax.nn.sigmoid(gt) * up, wd[...], precise), blk)

    @pl.when(j > nv)
    def _():
        ys_ref[...] = jnp.zeros(ys_ref.shape, F32)


def experts(xs, plan, wg, wu, wd, layer, *, precise):
    nblk, rows = plan['nblk'], plan['blk'] * ROW_TILES
    d, ff = wg.shape[2], wg.shape[3]
    blk = lambda j, be, nv: (jnp.minimum(j, nv[0] - 1), 0)
    wspec = lambda shape: pl.BlockSpec((None, 1) + shape,
                                       lambda j, be, nv: (layer, be[jnp.clip(j - 1, 0, nv[0] - 1)], 0, 0))
    gs = pltpu.PrefetchScalarGridSpec(
        num_scalar_prefetch=2,
        grid=(nblk + 1,),
        in_specs=[pl.BlockSpec((rows, LANES), blk),
                  wspec((d, ff)), wspec((d, ff)), wspec((ff, d))],
        out_specs=pl.BlockSpec((rows, LANES), lambda j, be, nv: (jnp.maximum(j - 1, 0), 0)),
        scratch_shapes=[pltpu.VMEM((2, plan['blk'], d), F32 if precise else BF16)]
        + ([] if precise else [pltpu.VMEM((d, ff), BF16), pltpu.VMEM((d, ff), BF16), pltpu.VMEM((ff, d), BF16)]),
    )
    return pl.pallas_call(
        functools.partial(_experts_body, precise=precise, blk=plan['blk']),
        grid_spec=gs,
        out_shape=jax.ShapeDtypeStruct(xs.shape, F32),
        compiler_params=_cparams("arbitrary"),
        name="experts",
    )(plan['block_e'], plan['n_used'], xs, wg, wu, wd)


def moe_plan(lpos8, gate8, runn8, blk):
    nch = runn8.shape[0] // SUBLANES
    chunk = lpos8.shape[1]
    n_assign = nch * chunk * TOP_K
    nblk = (n_assign + N_EXPERTS * (blk - 1 + RUN_SLOT) + blk - 1) // blk
    run_n = runn8.reshape(nch, SUBLANES, LANES)[:, 0, :N_EXPERTS]
    per_chunk = lambda a: a.reshape(nch, 1, SUBLANES * chunk)[:, :, :TOP_K * chunk]
    counts = jnp.sum(run_n, axis=0)
    padded = (counts + RUN_SLOT + blk - 1) // blk * blk
    pends = jnp.cumsum(padded)
    pstarts = pends - padded
    run_g = pstarts[None, :] + jnp.cumsum(run_n, axis=0) - run_n
    n_piece = (run_n + RUN_SLOT - 1) // RUN_SLOT
    slot_l = (jnp.cumsum(n_piece, axis=1) - n_piece) * RUN_SLOT
    blk_start = jnp.arange(nblk, dtype=I32) * blk
    block_e = jnp.minimum(jnp.sum((pends[None, :] <= blk_start[:, None]).astype(I32), axis=1), N_EXPERTS - 1)
    return dict(chunk=chunk, nch=nch, nblk=nblk, blk=blk,
                run_g=run_g.reshape(-1).astype(I32), slot_l=slot_l.reshape(-1).astype(I32),
                n_piece=n_piece.reshape(-1).astype(I32), n_piece_chunk=jnp.sum(n_piece, axis=1).astype(I32),
                pad_start=(pstarts + counts).astype(I32), pad_n=(padded - counts).astype(I32),
                lpos=per_chunk(lpos8), gate=per_chunk(gate8), block_e=block_e.astype(I32),
                n_used=(pends[-1:] // blk).astype(I32),
                tail=jnp.stack([pends[-1], 2 * (nblk - pends[-1] // blk)]).astype(I32))


def _combine(rg_ref, sl_ref, np_ref, npc_ref, lpos_ref, gate_ref, x_ref, ys_hbm, ystage, comb, sem, *, chunk, nch):
    c = pl.program_id(0)
    slot = c % 2

    def fetch(cc, sl):
        def per_e(e, carry):
            k = cc * N_EXPERTS + e

            def one(i, carry2):
                pltpu.make_async_copy(ys_hbm.at[_piece_rows(rg_ref[k], i)], ystage.at[sl, _piece_rows(sl_ref[k], i)],
                                      sem.at[sl]).start()
                return carry2
            lax.fori_loop(0, np_ref[k], one, 0)
            return carry
        lax.fori_loop(0, N_EXPERTS, per_e, 0)

    @pl.when(c == 0)
    def _():
        fetch(0, 0)

    @pl.when(c + 1 < nch)
    def _():
        fetch(c + 1, 1 - slot)

    def wait_piece(i, carry):
        pltpu.make_async_copy(ystage.at[slot, _tile_rows(0, RUN_SLOT)], ystage.at[slot, _tile_rows(0, RUN_SLOT)],
                              sem.at[slot]).wait()
        return carry
    lax.fori_loop(0, npc_ref[c], wait_piece, 0)

    def per_tok(t, carry):
        y0 = ystage[slot, _rows_at(lpos_ref[0, 0, t]), :]
        y1 = ystage[slot, _rows_at(lpos_ref[0, 0, chunk + t]), :]
        comb[_tile_rows(t), :] = gate_ref[0, 0, t] * y0 + gate_ref[0, 0, chunk + t] * y1
        return carry
    lax.fori_loop(0, chunk, per_tok, 0, unroll=8)
    return x_ref[...] + _load_row_tiles(comb, chunk)


def _combine_glu_body(rg_ref, sl_ref, np_ref, npc_ref, lpos_ref, gate_ref, x_ref, ys_hbm, g_ref, w_ref, b_ref,
                      x2_ref, u_ref, ystage, comb, sem, *, chunk, nch, precise):
    x2 = _combine(rg_ref, sl_ref, np_ref, npc_ref, lpos_ref, gate_ref, x_ref, ys_hbm, ystage, comb, sem,
                  chunk=chunk, nch=nch)
    x2_ref[...] = x2
    zz = _mm(_rms(x2, g_ref[...]), w_ref[...], precise) + b_ref[...]
    half = zz.shape[1] // 2
    u_ref[...] = zz[:, :half] * jax.nn.sigmoid(zz[:, half:])


def _combine_final_body(rg_ref, sl_ref, np_ref, npc_ref, lpos_ref, gate_ref, x_ref, ys_hbm, g_ref, o_ref,
                        ystage, comb, sem, *, chunk, nch):
    x2 = _combine(rg_ref, sl_ref, np_ref, npc_ref, lpos_ref, gate_ref, x_ref, ys_hbm, ystage, comb, sem,
                  chunk=chunk, nch=nch)
    o_ref[...] = _rms(x2, g_ref[...])


def _combine_call(body, plan, x, ys, extra, extra_specs, out_specs, out_shape, name):
    chunk, nch = plan['chunk'], plan['nch']
    d = x.shape[1]
    smem_blk = pl.BlockSpec((1, 1, TOP_K * chunk), lambda c, *_: (c, 0, 0), memory_space=pltpu.SMEM)
    gs = pltpu.PrefetchScalarGridSpec(
        num_scalar_prefetch=4,
        grid=(nch,),
        in_specs=[smem_blk, smem_blk, pl.BlockSpec((chunk, d), lambda c, *_: (c, 0)),
                  pl.BlockSpec(memory_space=pl.ANY)] + extra_specs,
        out_specs=out_specs,
        scratch_shapes=[pltpu.VMEM((2, _stage_rows(chunk) * ROW_TILES, LANES), F32),
                        pltpu.VMEM((chunk * ROW_TILES, LANES), F32), pltpu.SemaphoreType.DMA((2,))],
    )
    return pl.pallas_call(
        functools.partial(body, chunk=chunk, nch=nch),
        grid_spec=gs,
        out_shape=out_shape,
        compiler_params=_cparams("arbitrary"),
        name=name,
    )(plan['run_g'], plan['slot_l'], plan['n_piece'], plan['n_piece_chunk'], plan['lpos'], plan['gate'], x, ys,
      *extra)


def combine_glu(x, ys, plan, g, w, b, *, precise):
    rows, d = x.shape
    chunk = plan['chunk']
    cols = w.shape[1]
    const = lambda shape: pl.BlockSpec(shape, lambda c, *_: (0,) * len(shape))
    row_spec = lambda width: pl.BlockSpec((chunk, width), lambda c, *_: (c, 0))
    return _combine_call(
        functools.partial(_combine_glu_body, precise=precise), plan, x, ys, [g, w, b],
        [const((1, d)), const((d, cols)), const((1, cols))], [row_spec(d), row_spec(cols // 2)],
        [jax.ShapeDtypeStruct((rows, d), F32), jax.ShapeDtypeStruct((rows, cols // 2), F32)], "combine_glu")


def combine_final(x, ys, plan, g):
    rows, d = x.shape
    chunk = plan['chunk']
    return _combine_call(
        _combine_final_body, plan, x, ys, [g], [pl.BlockSpec((1, d), lambda c, *_: (0, 0))],
        pl.BlockSpec((chunk, d), lambda c, *_: (c, 0)), jax.ShapeDtypeStruct((rows, d), F32), "combine_final")


CONV_TILE = 512
CONV_HIST = 32


def _ln_swish(y, g, b):
    yc = y - jnp.mean(y, axis=-1, keepdims=True)
    yn = yc * lax.rsqrt(jnp.mean(yc * yc, axis=-1, keepdims=True) + EPS) * g + b
    return yn * jax.nn.sigmoid(yn)


CONV_ROWS = 64
LN_ROWS = 16
LN_UNROLL = 8


def _dwconv_prompt_body(u_ref, w_ref, bdw_ref, g_ref, b_ref, o_ref, ext, y_sc, *, tt):
    t = pl.program_id(1)
    n_lt = ext.shape[0]

    @pl.when(t == 0)
    def _():
        ext[:, 0:CONV_HIST, :] = jnp.zeros((n_lt, CONV_HIST, LANES), F32)

    @pl.when(t > 0)
    def _():
        ext[:, 0:CONV_HIST, :] = ext[:, tt:tt + CONV_HIST, :]

    for j in range(n_lt):
        ext[j, CONV_HIST:CONV_HIST + tt, :] = u_ref[:, j * LANES:(j + 1) * LANES]
    off = CONV_HIST - (C_KERNEL - 1)
    for j in range(n_lt):
        wj = w_ref[:, j * LANES:(j + 1) * LANES]
        bj = bdw_ref[:, j * LANES:(j + 1) * LANES]
        for c in range(tt // CONV_ROWS):
            acc = ext[j, pl.ds(off + c * CONV_ROWS, CONV_ROWS), :] * wj[0:1] + bj
            for k in range(1, C_KERNEL):
                acc = acc + ext[j, pl.ds(off + k + c * CONV_ROWS, CONV_ROWS), :] * wj[k:k + 1]
            y_sc[c * CONV_ROWS:(c + 1) * CONV_ROWS, j * LANES:(j + 1) * LANES] = acc

    def ln_rows(r, carry):
        rows = pl.ds(pl.multiple_of(r * LN_ROWS, LN_ROWS), LN_ROWS)
        o_ref[rows, :] = _ln_swish(y_sc[rows, :], g_ref[...], b_ref[...]).astype(o_ref.dtype)
        return carry
    lax.fori_loop(0, tt // LN_ROWS, ln_rows, 0, unroll=LN_UNROLL)


def dwconv_prompt(u, w, b_dw, ln_g, ln_b, bsz, seq):
    tt = min(CONV_TILE, seq)
    nt = seq // tt
    d = u.shape[1]
    const = lambda shape: pl.BlockSpec(shape, lambda b, t: (0,) * len(shape))
    return pl.pallas_call(
        functools.partial(_dwconv_prompt_body, tt=tt),
        grid=(bsz, nt),
        in_specs=[pl.BlockSpec((tt, d), lambda b, t: (b * nt + t, 0)), const((C_KERNEL, d)), const((1, d)),
                  const((1, d)), const((1, d))],
        out_specs=pl.BlockSpec((tt, d), lambda b, t: (b * nt + t, 0)),
        out_shape=jax.ShapeDtypeStruct((bsz * seq, d), BF16),
        scratch_shapes=[pltpu.VMEM((d // LANES, CONV_HIST + tt, LANES), F32), pltpu.VMEM((tt, d), F32)],
        compiler_params=_cparams("arbitrary", "arbitrary"),
        name="dwconv_prompt",
    )(u, w, b_dw, ln_g, ln_b)


def _dwconv_sample_body(u_ref, buf_ref, w_ref, bdw_ref, g_ref, b_ref, o_ref, nbuf_ref, *, tb):
    w = w_ref[...]
    rows = []
    for i in range(tb):
        hist = buf_ref[i]
        ur = u_ref[i:i + 1, :]
        rows.append(jnp.sum(hist * w[:C_KERNEL - 1], axis=0, keepdims=True) + ur * w[C_KERNEL - 1:C_KERNEL])
        nbuf_ref[i] = jnp.concatenate([hist[1:], ur], axis=0)
    y = jnp.concatenate(rows, axis=0) + bdw_ref[...]
    o_ref[...] = _ln_swish(y, g_ref[...], b_ref[...])


def dwconv_sample(u, buf, w, b_dw, ln_g, ln_b):
    nb, d = u.shape
    tb = min(SAMPLE_TILE, nb)
    const = lambda shape: pl.BlockSpec(shape, lambda i: (0,) * len(shape))
    return pl.pallas_call(
        functools.partial(_dwconv_sample_body, tb=tb),
        grid=(nb // tb,),
        in_specs=[pl.BlockSpec((tb, d), lambda i: (i, 0)), pl.BlockSpec((tb, C_KERNEL - 1, d), lambda i: (i, 0, 0)),
                  const((C_KERNEL, d)), const((1, d)), const((1, d)), const((1, d))],
        out_specs=[pl.BlockSpec((tb, d), lambda i: (i, 0)), pl.BlockSpec((tb, C_KERNEL - 1, d), lambda i: (i, 0, 0))],
        out_shape=[jax.ShapeDtypeStruct((nb, d), F32), jax.ShapeDtypeStruct(buf.shape, F32)],
        compiler_params=_cparams("parallel"),
        name="dwconv_sample",
    )(u, buf, w, b_dw, ln_g, ln_b)


def _moe(h8, lpos8, gate8, runn8, wg, wu, wd, layer, *, blk, precise):
    plan = moe_plan(lpos8, gate8, runn8, blk)
    xs = dispatch(h8, plan)
    return experts(xs, plan, wg, wu, wd, layer, precise=precise), plan


def _trunk(x, caches, p, *, prompt):
    bsz, seq, d = x.shape
    rows = bsz * seq
    precise = not prompt
    wdt = F32 if precise else BF16
    xf = x.reshape(rows, d)
    row = lambda v: v.reshape(1, -1).astype(F32)

    z = norm_proj(xf, row(p['norm_mix'][0]), p['w_in'].astype(wdt), precise=precise)
    if prompt:
        z, zb = z
        att = attn_prompt(z, zb, p['rel_table'], p['sinks'], bsz, seq)
        out_b, c1, n1, m1 = mlstm_prompt(z, zb, p['conv_w'], p['b_gates'], p['g_mnorm'], bsz, seq)
        z3 = z.reshape(bsz, seq, P_COLS)
        new_k = z3[:, seq - WINDOW:, P_KA:P_KA + A_KV].reshape(bsz, WINDOW, A_KV_HEADS, A_HEAD_DIM)
        new_v = z3[:, seq - WINDOW:, P_VA:P_VA + A_KV].reshape(bsz, WINDOW, A_KV_HEADS, A_HEAD_DIM)
        new_conv = z3[:, seq - (B_CONV - 1):, P_QK:P_QK + 2 * B_QK]
        n1 = n1[:, :B_HEADS]
        m1 = m1[:, :B_HEADS, 0]
    else:
        ck, cv, c0, n0, m0, cbuf = caches[:6]
        n_buf = ck.shape[1]
        att, out_b, new_k, new_v, c1, n1, m1, new_conv = mix_sample(
            z, p['rel_table'], p['sinks'], ck.reshape(bsz, n_buf, A_KV), cv.reshape(bsz, n_buf, A_KV),
            c0, n0, m0, cbuf, p['conv_w'], p['b_gates'], p['g_mnorm'])
        new_k = new_k.reshape(bsz, n_buf, A_KV_HEADS, A_HEAD_DIM)
        new_v = new_v.reshape(bsz, n_buf, A_KV_HEADS, A_HEAD_DIM)
        m1 = m1[:, :B_HEADS]
    w_out = p['w_out'].astype(wdt)
    x1, h8, lpos, gate, runn = proj_router([att, out_b], [w_out[:A_Q], w_out[A_Q:]], None, xf, row(p['norm_ffn'][0]),
                                           p['w_router'][0], p['b_router'][0], precise=precise)
    blk = EXPERT_BLOCK_PRECISE if precise else EXPERT_BLOCK
    ys, plan = _moe(h8, lpos, gate, runn, p['w_eg'], p['w_eu'], p['w_ed'], 0, blk=blk, precise=precise)

    x2, u = combine_glu(x1, ys, plan, row(p['norm_mix'][1]), p['w_pw1'].astype(wdt), row(p['b_pw1']),
                        precise=precise)
    if prompt:
        yc = dwconv_prompt(u, p['w_dw'], row(p['b_dw']), row(p['ln_g']), row(p['ln_b']), bsz, seq)
        new_cbuf = u.reshape(bsz, seq, d)[:, seq - (C_KERNEL - 1):]
    else:
        yc, new_cbuf = dwconv_sample(u, caches[6], p['w_dw'], row(p['b_dw']), row(p['ln_g']), row(p['ln_b']))
    x3, h8, lpos, gate, runn = proj_router([yc], [p['w_pw2'].astype(wdt)], row(p['b_pw2']), x2, row(p['norm_ffn'][1]),
                                           p['w_router'][1], p['b_router'][1], precise=precise)
    ys, plan = _moe(h8, lpos, gate, runn, p['w_eg'], p['w_eu'], p['w_ed'], 1, blk=blk, precise=False)
    y = combine_final(x3, ys, plan, row(p['norm_final']))
    add_layer = lambda t: t[None]
    return (y.reshape(bsz, seq, d),) + tuple(add_layer(t) for t in (new_k, new_v, c1, n1, m1, new_conv, new_cbuf))


def kernel(x_prompt, x_sample, cache_win_k, cache_win_v, state_mlstm_c, state_mlstm_n, state_mlstm_m, state_mlstm_conv, state_conv, norm_mix, norm_ffn, norm_final, rel_bias_table, w_in_mix, b_mlstm_gates, w_mlstm_qk_conv, attn_sinks, g_mlstm_norm, w_out_mix, w_pw1, b_pw1, w_dw, b_dw, ln_conv_g, ln_conv_b, w_pw2, b_pw2, w_router_group, b_router_group, w_router_expert, b_router_expert, w_expert_gate, w_expert_up, w_expert_down):
    w_in = w_in_mix[0]
    s_q, s_k, s_v, s_qk, s_vb, s_g = A_Q, A_Q + A_KV, A_Q + 2 * A_KV, A_Q + 2 * A_KV + 2 * B_QK, \
        A_Q + 2 * A_KV + 2 * B_QK + B_V, A_Q + 2 * A_KV + 2 * B_QK + B_V + 2 * B_HEADS
    w_in_r = jnp.concatenate([w_in[:, :s_q], w_in[:, s_v:s_qk], w_in[:, s_qk:s_vb], w_in[:, s_g:],
                              w_in[:, s_q:s_k], w_in[:, s_k:s_v], w_in[:, s_vb:s_g],
                              jnp.zeros((D_MODEL, LANES - 2 * B_HEADS), F32)], axis=1)
    b_gates = jnp.concatenate([b_mlstm_gates[0], jnp.zeros((LANES - 2 * B_HEADS,), F32)]).reshape(1, LANES)
    depth = w_router_group.shape[0]
    w_re = jnp.transpose(w_router_expert, (0, 2, 1, 3)).reshape(depth, D_MODEL, N_EXPERTS)
    w_router = jnp.concatenate([w_router_group, w_re,
                                jnp.zeros((depth, D_MODEL, LANES - N_GROUPS - N_EXPERTS), F32)], axis=-1)
    b_router = jnp.concatenate([b_router_group, b_router_expert.reshape(depth, N_EXPERTS),
                                jnp.zeros((depth, LANES - N_GROUPS - N_EXPERTS), F32)], axis=-1)[:, None, :]
    p = dict(norm_mix=norm_mix, norm_ffn=norm_ffn, norm_final=norm_final, rel_table=rel_bias_table,
             sinks=attn_sinks[0], w_in=w_in_r, b_gates=b_gates, conv_w=w_mlstm_qk_conv[0],
             g_mnorm=g_mlstm_norm[0].reshape(1, B_V), w_out=w_out_mix[0], w_pw1=w_pw1[0], b_pw1=b_pw1[0],
             w_dw=w_dw[0], b_dw=b_dw[0], ln_g=ln_conv_g[0], ln_b=ln_conv_b[0], w_pw2=w_pw2[0], b_pw2=b_pw2[0],
             w_router=w_router, b_router=b_router, w_eg=w_expert_gate, w_eu=w_expert_up, w_ed=w_expert_down)
    caches = (cache_win_k[0], cache_win_v[0], state_mlstm_c[0], state_mlstm_n[0], state_mlstm_m[0],
              state_mlstm_conv[0], state_conv[0])
    out_p = _trunk(x_prompt, None, p, prompt=True)
    out_s = _trunk(x_sample, caches, p, prompt=False)
    return (out_p[0], out_s[0]) + out_p[1:] + out_s[1:]
```

```python
import functools
import math

import numpy as np
import jax
import jax.numpy as jnp
from jax import lax
from jax.experimental import pallas as pl
from jax.experimental.pallas import tpu as pltpu

F32 = jnp.float32
BF16 = jnp.bfloat16
I32 = jnp.int32
HI = lax.Precision.HIGHEST
NEG_INF = float("-inf")

LANES = 128
SUBLANES = 8
VMEM_LIMIT = 56 * 1024 * 1024

D_MODEL = 1024
A_HEADS = 8
A_KV_HEADS = 2
A_GROUP = A_HEADS // A_KV_HEADS
A_HEAD_DIM = 64
WINDOW = 128
REL_BUCKETS = 32
REL_MAX_DIST = 128
B_HEADS = 4
B_DK = 64
B_DV = 128
B_CONV = 4
C_KERNEL = 31
N_GROUPS = 4
EXPERTS_PER_GROUP = 8
N_EXPERTS = N_GROUPS * EXPERTS_PER_GROUP
TOP_K = 2
EXPERT_FF = D_MODEL // 2
EXPERT_BLOCK = 512
EXPERT_BLOCK_PRECISE = 128
EPS = 1e-6

A_Q = A_HEADS * A_HEAD_DIM
A_KV = A_KV_HEADS * A_HEAD_DIM
B_QK = B_HEADS * B_DK
B_V = B_HEADS * B_DV
ROW_TILES = D_MODEL // LANES

Z_QA, Z_QK, Z_VB, Z_OG, Z_KA, Z_VA, Z_GATES = 0, 512, 1024, 1536, 2048, 2176, 2304
Z_COLS = 2432
MLSTM_CHUNK = 128


def _cparams(*sem):
    return pltpu.CompilerParams(dimension_semantics=sem, vmem_limit_bytes=VMEM_LIMIT)


def _rms(x, g):
    return x * lax.rsqrt(jnp.mean(x * x, axis=-1, keepdims=True) + EPS) * g


def _mm(a, w, precise):
    if not precise:
        return jnp.dot(a.astype(BF16), w, preferred_element_type=F32)
    a = a.astype(F32)
    a_hi = a.astype(BF16)
    a_lo = (a - a_hi.astype(F32)).astype(BF16)
    w_hi = w.astype(BF16)
    w_lo = (w - w_hi.astype(F32)).astype(BF16)
    return (jnp.dot(a_hi, w_hi, preferred_element_type=F32) + jnp.dot(a_lo, w_hi, preferred_element_type=F32)
            + jnp.dot(a_hi, w_lo, preferred_element_type=F32))


def _t5_buckets(dist):
    exact = REL_BUCKETS // 2
    d = np.maximum(dist, 0)
    large = exact + (np.log(np.maximum(d, 1).astype(np.float32) / exact)
                     / math.log(REL_MAX_DIST / exact) * (REL_BUCKETS - exact)).astype(np.int32)
    return np.where(d < exact, d, np.minimum(large, REL_BUCKETS - 1)).astype(np.int32)


P_QK, P_OG, P_KA, P_VA, P_GATES = 0, 512, 1024, 1152, 1280
P_COLS = 1408
PB_QA, PB_VB = 0, 512
PB_COLS = 1024


def _norm_proj_body(x_ref, g_ref, w_ref, *o_refs, precise):
    h = _rms(x_ref[...], g_ref[...])
    z = _mm(h, w_ref[...], precise)
    if len(o_refs) == 1:
        o_refs[0][...] = z
        return
    z32_ref, zb_ref = o_refs
    z32_ref[...] = jnp.concatenate([z[:, Z_QK:Z_QK + 2 * B_QK], z[:, Z_OG:Z_OG + B_V], z[:, Z_KA:Z_GATES + LANES]],
                                   axis=1)
    zb_ref[...] = jnp.concatenate([z[:, Z_QA:Z_QA + A_Q], z[:, Z_VB:Z_VB + B_V]], axis=1).astype(BF16)


def norm_proj(x, g, w, *, precise):
    rows, d = x.shape
    cols = w.shape[1]
    tm = min(rows, 128 if precise else 512)
    row_spec = lambda width: pl.BlockSpec((tm, width), lambda i: (i, 0))
    if precise:
        out_specs, out_shape = row_spec(cols), jax.ShapeDtypeStruct((rows, cols), F32)
    else:
        out_specs = [row_spec(P_COLS), row_spec(PB_COLS)]
        out_shape = [jax.ShapeDtypeStruct((rows, P_COLS), F32), jax.ShapeDtypeStruct((rows, PB_COLS), BF16)]
    return pl.pallas_call(
        functools.partial(_norm_proj_body, precise=precise),
        grid=(rows // tm,),
        in_specs=[row_spec(d), pl.BlockSpec((1, d), lambda i: (0, 0)), pl.BlockSpec((d, cols), lambda i: (0, 0))],
        out_specs=out_specs,
        out_shape=out_shape,
        compiler_params=_cparams("parallel"),
        name="norm_proj",
    )(x, g, w)


def _attn_prompt_body(tab_ref, sink_ref, bkt_ref, q_ref, kp_ref, kc_ref, vp_ref, vc_ref, o_ref, bias_ref, *, nq):
    b = pl.program_id(0)
    n = pl.program_id(1)

    @pl.when((b == 0) & (n == 0))
    def _():
        bk = bkt_ref[...]
        first = lax.broadcasted_iota(I32, bk.shape, 1) >= WINDOW
        for h in range(A_HEADS):
            acc = jnp.full(bk.shape, NEG_INF, F32)
            for t in range(REL_BUCKETS):
                acc = jnp.where(bk == t, tab_ref[t, h], acc)
            bias_ref[h] = acc
            bias_ref[A_HEADS + h] = jnp.where(first, acc, NEG_INF)

    k_all = jnp.concatenate([kp_ref[...], kc_ref[...]], axis=0).astype(BF16)
    v_all = jnp.concatenate([vp_ref[...], vc_ref[...]], axis=0).astype(BF16)
    for sub in range(nq):
        rows = slice(sub * WINDOW, (sub + 1) * WINDOW)
        q = q_ref[rows, :] * (A_HEAD_DIM ** -0.5)
        kb = k_all[sub * WINDOW:(sub + 2) * WINDOW]
        vb = v_all[sub * WINDOW:(sub + 2) * WINDOW]
        table = jnp.where(n == 0, A_HEADS, 0) if sub == 0 else 0
        outs = []
        for h in range(A_HEADS):
            kvh = h // A_GROUP
            qh = q[:, h * A_HEAD_DIM:(h + 1) * A_HEAD_DIM].astype(BF16)
            kh = kb[:, kvh * A_HEAD_DIM:(kvh + 1) * A_HEAD_DIM]
            vh = vb[:, kvh * A_HEAD_DIM:(kvh + 1) * A_HEAD_DIM]
            s = lax.dot_general(qh, kh, (((1,), (1,)), ((), ())), preferred_element_type=F32)
            s = s + bias_ref[table + h]
            sink = sink_ref[h]
            mx = jnp.maximum(jnp.max(s, axis=-1, keepdims=True), sink)
            p = jnp.exp(s - mx)
            den = jnp.sum(p, axis=-1, keepdims=True) + jnp.exp(sink - mx)
            outs.append(jnp.dot(p.astype(BF16), vh, preferred_element_type=F32) / den)
        o_ref[rows, :] = jnp.concatenate(outs, axis=1).astype(o_ref.dtype)


ATTN_BLOCKS = 2


def attn_prompt(z, zb, rel_table, sinks, bsz, seq):
    nb = seq // WINDOW
    dist = WINDOW + np.arange(WINDOW)[:, None] - np.arange(2 * WINDOW)[None, :]
    bkt = np.where((dist >= 0) & (dist <= WINDOW), _t5_buckets(dist), -1).astype(np.int32)
    kcol, vcol = P_KA // LANES, P_VA // LANES
    smem = pl.BlockSpec(memory_space=pltpu.SMEM)
    nq = ATTN_BLOCKS if nb % ATTN_BLOCKS == 0 else 1
    ns = nb // nq

    def cur(c):
        return pl.BlockSpec((nq * WINDOW, LANES), lambda b, n: (b * ns + n, c))

    def prev(c):
        return pl.BlockSpec((WINDOW, LANES), lambda b, n: (b * nb + jnp.maximum(n * nq - 1, 0), c))

    return pl.pallas_call(
        functools.partial(_attn_prompt_body, nq=nq),
        grid=(bsz, ns),
        in_specs=[smem, smem,
                  pl.BlockSpec((WINDOW, 2 * WINDOW), lambda b, n: (0, 0)),
                  pl.BlockSpec((nq * WINDOW, A_Q), lambda b, n: (b * ns + n, PB_QA // A_Q)),
                  prev(kcol), cur(kcol), prev(vcol), cur(vcol)],
        out_specs=pl.BlockSpec((nq * WINDOW, A_Q), lambda b, n: (b * ns + n, 0)),
        out_shape=jax.ShapeDtypeStruct((bsz * seq, A_Q), BF16),
        scratch_shapes=[pltpu.VMEM((2 * A_HEADS, WINDOW, 2 * WINDOW), F32)],
        compiler_params=_cparams("arbitrary", "arbitrary"),
        name="attn_prompt",
    )(rel_table, sinks, jnp.asarray(bkt), zb, z, z, z, z)


def _log_sigmoid(x):
    return -(jnp.maximum(-x, 0.0) + jnp.log1p(jnp.exp(-jnp.abs(x))))


def _mlstm_prompt_body(qk_ref, v_ref, g_ref, og_ref, cw_ref, bg_ref, gn_ref,
                       ob_ref, c_out, n_out, m_out,
                       c_sc, n_sc, m_sc, hist_sc, *, L, nc, nbat, nsub):
    ci = pl.program_id(1)

    @pl.when(ci == 0)
    def _():
        c_sc[...] = jnp.zeros(c_sc.shape, F32)
        n_sc[...] = jnp.zeros(n_sc.shape, F32)
        m_sc[...] = jnp.full(m_sc.shape, NEG_INF, F32)
        hist_sc[...] = jnp.zeros(hist_sc.shape, F32)

    row = lax.broadcasted_iota(I32, (L, L), 0)
    colm = lax.broadcasted_iota(I32, (L, L), 1)
    causal = colm <= row
    tril = causal.astype(F32)
    for sub in range(nsub):
        rows = pl.ds(sub * L, L)
        stores = []
        for bb in range(nbat):
            stores += _mlstm_chunk(qk_ref.at[bb, rows], v_ref.at[bb, rows], g_ref.at[bb, rows], og_ref.at[bb, rows],
                                   cw_ref, bg_ref, gn_ref, ob_ref.at[bb, rows], c_sc.at[bb], n_sc.at[bb],
                                   m_sc.at[bb], hist_sc.at[bb], causal, tril, L)
        for store in stores:
            store()

    @pl.when(ci == nc - 1)
    def _():
        c_out[...] = c_sc[...]
        n_out[...] = n_sc[...]
        m_out[...] = m_sc[...]


def _mlstm_chunk(qk_ref, v_ref, g_ref, og_ref, cw_ref, bg_ref, gn_ref, ob_ref, c_sc, n_sc, m_sc, hist_sc,
                 causal, tril, L):
    cur = qk_ref[...]
    ext = jnp.concatenate([hist_sc[...], cur], axis=0)
    cw = cw_ref[...]
    off = SUBLANES - (B_CONV - 1)
    conv = ext[off:off + L] * cw[0:1]
    for j in range(1, B_CONV):
        conv = conv + ext[off + j:off + j + L] * cw[j:j + 1]
    qk = conv * jax.nn.sigmoid(conv)
    q_all = qk[:, :B_QK]
    k_all = qk[:, B_QK:] * (B_DK ** -0.5)
    k_t = k_all.T
    v_all = v_ref[...]
    og = og_ref[...]
    gn = gn_ref[...]

    G = g_ref[...] + bg_ref[...]
    lf = _log_sigmoid(G)
    Bc = jnp.dot(tril, lf, precision=HI, preferred_element_type=F32)
    BT = Bc.T
    GT = G.T

    outs, stores = [], []
    for h in range(B_HEADS):
        qh = q_all[:, h * B_DK:(h + 1) * B_DK]
        kh = k_all[:, h * B_DK:(h + 1) * B_DK]
        vh = v_all[:, h * B_DV:(h + 1) * B_DV]
        b_col = Bc[:, B_HEADS + h:B_HEADS + h + 1]
        ig_col = G[:, h:h + 1]
        b_row = BT[B_HEADS + h:B_HEADS + h + 1, :]
        ig_row = GT[h:h + 1, :]
        c0 = c_sc[h]
        n0 = n_sc[h:h + 1, :]
        m0 = m_sc[h:h + 1, 0:1]
        a = b_col + m0
        d = jnp.where(causal, b_col - b_row + ig_row, NEG_INF)
        m = jnp.maximum(a, jnp.max(d, axis=-1, keepdims=True))
        dw = jnp.exp(d - m)
        aw = jnp.exp(a - m)
        qb = qh.astype(BF16)
        vb = vh.astype(BF16)
        s = lax.dot_general(qb, kh.astype(BF16), (((1,), (1,)), ((), ())), preferred_element_type=F32) * dw
        num = (jnp.dot(s.astype(BF16), vb, preferred_element_type=F32)
               + aw * jnp.dot(qb, c0.astype(BF16), preferred_element_type=F32))
        den = jnp.sum(s, axis=-1, keepdims=True) + aw * jnp.sum(qh * n0, axis=-1, keepdims=True)
        hh = num / jnp.maximum(jnp.abs(den), jnp.exp(-m))
        m_last = m[L - 1:L, :]
        wl = jnp.exp(b_col[L - 1:L, :] - b_col + ig_col - m_last)
        decay = aw[L - 1:L, :]
        kw_t = k_t[h * B_DK:(h + 1) * B_DK, :] * dw[L - 1:L, :]
        c1 = decay * c0 + jnp.dot(kw_t.astype(BF16), vb, preferred_element_type=F32)
        n1 = decay * n0 + jnp.sum(kh * wl, axis=0, keepdims=True)
        stores.append(functools.partial(_store_state, c_sc, n_sc, m_sc, h, c1, n1, m_last))
        hn = hh * lax.rsqrt(jnp.mean(hh * hh, axis=-1, keepdims=True) + EPS) * gn[:, h * B_DV:(h + 1) * B_DV]
        outs.append(jax.nn.sigmoid(og[:, h * B_DV:(h + 1) * B_DV]) * hn)
    out = jnp.concatenate(outs, axis=1).astype(ob_ref.dtype)
    stores.append(functools.partial(_store_chunk, ob_ref, hist_sc, out, cur[L - SUBLANES:L]))
    return stores


def _store_state(c_sc, n_sc, m_sc, h, c1, n1, m_last):
    c_sc[h] = c1
    n_sc[h:h + 1, :] = n1
    m_sc[h:h + 1, :] = jnp.broadcast_to(m_last, (1, LANES))


def _store_chunk(ob_ref, hist_sc, out, tail):
    ob_ref[...] = out
    hist_sc[...] = tail


MLSTM_BATCH = 1
MLSTM_SUBCHUNKS = 1


def mlstm_prompt(z, zb, conv_w, b_gates_pad, g_norm, bsz, seq):
    L = MLSTM_CHUNK
    nsub = MLSTM_SUBCHUNKS if (seq // L) % MLSTM_SUBCHUNKS == 0 else 1
    nc = seq // (L * nsub)
    nbat = MLSTM_BATCH if bsz % MLSTM_BATCH == 0 else 1
    z3 = z.reshape(bsz, seq, P_COLS)
    zb3 = zb.reshape(bsz, seq, PB_COLS)

    def zspec(width, colblk):
        return pl.BlockSpec((nbat, nsub * L, width), lambda b, c: (b, c, colblk))

    const = lambda shape: pl.BlockSpec(shape, lambda b, c: (0,) * len(shape))
    state = lambda shape: pl.BlockSpec((nbat,) + shape, lambda b, c: (b,) + (0,) * len(shape))
    out_b, c1, n1, m1 = pl.pallas_call(
        functools.partial(_mlstm_prompt_body, L=L, nc=nc, nbat=nbat, nsub=nsub),
        grid=(bsz // nbat, nc),
        in_specs=[zspec(2 * B_QK, P_QK // (2 * B_QK)), zspec(B_V, PB_VB // B_V), zspec(LANES, P_GATES // LANES),
                  zspec(B_V, P_OG // B_V), const((B_CONV, 2 * B_QK)), const((1, LANES)), const((1, B_V))],
        out_specs=[pl.BlockSpec((nbat, nsub * L, B_V), lambda b, c: (b, c, 0)),
                   state((B_HEADS, B_DK, B_DV)), state((SUBLANES, B_DK)), state((SUBLANES, LANES))],
        out_shape=[jax.ShapeDtypeStruct((bsz, seq, B_V), BF16),
                   jax.ShapeDtypeStruct((bsz, B_HEADS, B_DK, B_DV), F32),
                   jax.ShapeDtypeStruct((bsz, SUBLANES, B_DK), F32),
                   jax.ShapeDtypeStruct((bsz, SUBLANES, LANES), F32)],
        scratch_shapes=[pltpu.VMEM((nbat, B_HEADS, B_DK, B_DV), F32), pltpu.VMEM((nbat, SUBLANES, B_DK), F32),
                        pltpu.VMEM((nbat, SUBLANES, LANES), F32), pltpu.VMEM((nbat, SUBLANES, 2 * B_QK), F32)],
        compiler_params=_cparams("arbitrary", "arbitrary"),
        name="mlstm_prompt",
    )(z3, zb3, z3, z3, conv_w, b_gates_pad, g_norm)
    return out_b.reshape(bsz * seq, B_V), c1, n1, m1


SAMPLE_TILE = 8


def _mix_sample_body(tab_ref, sink_ref, bkt_ref, z_ref, ck_ref, cv_ref, c0_ref, n0_ref, m0_ref, cb_ref,
                     cw_ref, bg_ref, gn_ref,
                     att_ref, ob_ref, nk_ref, nv_ref, c1_ref, n1_ref, m1_ref, ncb_ref, *, tb):
    bk = bkt_ref[...]
    cw = cw_ref[...]
    gn = gn_ref[...]
    bias_rows = []
    for h in range(A_HEADS):
        bias = jnp.zeros(bk.shape, F32)
        for t in range(REL_BUCKETS):
            bias = jnp.where(bk == t, tab_ref[t, h], bias)
        bias_rows.append(bias)
    n_keys = bk.shape[1]
    rows, cols = A_HEADS * tb, tb * n_keys
    z_all = z_ref[...]
    q_all = z_all[:, Z_QA:Z_QA + A_Q] * (A_HEAD_DIM ** -0.5)
    ka_all = z_all[:, Z_KA:Z_KA + A_KV]
    va_all = z_all[:, Z_VA:Z_VA + A_KV]
    zero_half = jnp.zeros((tb, A_HEAD_DIM), F32)
    q_rows = []
    for h in range(A_HEADS):
        qh = q_all[:, h * A_HEAD_DIM:(h + 1) * A_HEAD_DIM]
        q_rows.append(jnp.concatenate([qh, zero_half] if h // A_GROUP == 0 else [zero_half, qh], axis=1))
    qm = jnp.concatenate(q_rows, axis=0)
    ka_rep = jnp.concatenate([ka_all] * A_HEADS, axis=0)
    va_rep = jnp.concatenate([va_all] * A_HEADS, axis=0)
    k_cat = jnp.concatenate([ck_ref[i] for i in range(tb)], axis=0)
    v_cat = jnp.concatenate([cv_ref[i] for i in range(tb)], axis=0)
    bias_c = jnp.concatenate([jnp.broadcast_to(b, (tb, n_keys)) for b in bias_rows], axis=0)
    bias_c = jnp.concatenate([bias_c] * tb, axis=1)
    r_id = lax.broadcasted_iota(I32, (rows, cols), 0)
    c_id = lax.broadcasted_iota(I32, (rows, cols), 1)
    own = (r_id & (tb - 1)) == (c_id >> (n_keys.bit_length() - 1))
    head_id = lax.broadcasted_iota(I32, (rows, 1), 0) >> (tb.bit_length() - 1)
    bias_n = jnp.zeros((rows, 1), F32)
    sinks = jnp.zeros((rows, 1), F32)
    for h in range(A_HEADS):
        bias_n = jnp.where(head_id == h, tab_ref[0, h], bias_n)
        sinks = jnp.where(head_id == h, sink_ref[h], sinks)
    lc = lax.dot_general(qm, k_cat, (((1,), (1,)), ((), ())), precision=HI, preferred_element_type=F32)
    lc = jnp.where(own, lc + bias_c, NEG_INF)
    ln = jnp.sum(qm * ka_rep, axis=-1, keepdims=True) + bias_n
    mx = jnp.maximum(jnp.maximum(jnp.max(lc, axis=-1, keepdims=True), ln), sinks)
    pc = jnp.exp(lc - mx)
    pn = jnp.exp(ln - mx)
    den = jnp.sum(pc, axis=-1, keepdims=True) + pn + jnp.exp(sinks - mx)
    o = (jnp.dot(pc, v_cat, precision=HI, preferred_element_type=F32) + pn * va_rep) / den
    att_ref[...] = jnp.concatenate(
        [o[h * tb:(h + 1) * tb, (h // A_GROUP) * A_HEAD_DIM:(h // A_GROUP + 1) * A_HEAD_DIM]
         for h in range(A_HEADS)], axis=1)

    for i in range(tb):
        nk_ref[i] = jnp.concatenate([ck_ref[i][1:], ka_all[i:i + 1]], axis=0)
        nv_ref[i] = jnp.concatenate([cv_ref[i][1:], va_all[i:i + 1]], axis=0)

    qk_pre = z_all[:, Z_QK:Z_QK + 2 * B_QK]
    hist = [cb_ref[i] for i in range(tb)]
    conv = qk_pre * cw[B_CONV - 1:B_CONV]
    for j in range(B_CONV - 1):
        conv = conv + jnp.concatenate([hs[j:j + 1] for hs in hist], axis=0) * cw[j:j + 1]
    for i in range(tb):
        ncb_ref[i] = jnp.concatenate([hist[i][1:], qk_pre[i:i + 1]], axis=0)
    qk = conv * jax.nn.sigmoid(conv)
    q_m = qk[:, :B_QK]
    k_m = qk[:, B_QK:] * (B_DK ** -0.5)
    v_m = z_all[:, Z_VB:Z_VB + B_V]
    og = z_all[:, Z_OG:Z_OG + B_V]
    G = z_all[:, Z_GATES:Z_GATES + LANES] + bg_ref[...]
    ig = G[:, :B_HEADS]
    a = _log_sigmoid(G)[:, B_HEADS:2 * B_HEADS] + m0_ref[...]
    m = jnp.maximum(a, ig)
    dw = jnp.exp(ig - m)
    aw = jnp.exp(a - m)
    li = lax.broadcasted_iota(I32, (B_QK, LANES), 0) >> (B_DK.bit_length() - 1)
    ind = (li == lax.broadcasted_iota(I32, (B_QK, LANES), 1)).astype(F32)
    n0_m = jnp.concatenate([jnp.concatenate([n0_ref[i, h:h + 1, :] for h in range(B_HEADS)], axis=1)
                            for i in range(tb)], axis=0)
    qk_dot = jnp.dot(q_m * k_m, ind, precision=HI, preferred_element_type=F32)[:, :B_HEADS]
    qn_dot = jnp.dot(q_m * n0_m, ind, precision=HI, preferred_element_type=F32)[:, :B_HEADS]
    s = qk_dot * dw
    inv = 1.0 / jnp.maximum(jnp.abs(s + aw * qn_dot), jnp.exp(-m))
    m1_ref[...] = jnp.concatenate([m, jnp.zeros((tb, LANES - B_HEADS), F32)], axis=1)
    stack = jnp.concatenate([q_m[:, h * B_DK:(h + 1) * B_DK] for h in range(B_HEADS)]
                            + [k_m[:, h * B_DK:(h + 1) * B_DK] for h in range(B_HEADS)], axis=0)
    cols = jnp.concatenate([stack, jnp.zeros_like(stack)], axis=1).T[:B_DK]
    hh_rows = [[] for _ in range(B_HEADS)]
    for i in range(tb):
        for h in range(B_HEADS):
            jq = h * tb + i
            jk = (B_HEADS + h) * tb + i
            c0 = c0_ref[i, h]
            vh = v_m[i:i + 1, h * B_DV:(h + 1) * B_DV]
            aw_p, dw_p = aw[i:i + 1, h:h + 1], dw[i:i + 1, h:h + 1]
            num = s[i:i + 1, h:h + 1] * vh + aw_p * jnp.sum(cols[:, jq:jq + 1] * c0, axis=0, keepdims=True)
            hh_rows[h].append(num * inv[i:i + 1, h:h + 1])
            c1_ref[i, h] = aw_p * c0 + dw_p * (cols[:, jk:jk + 1] * vh)
            n1_ref[i, h:h + 1, :] = aw_p * n0_ref[i, h:h + 1, :] + dw_p * k_m[i:i + 1, h * B_DK:(h + 1) * B_DK]
    obs = []
    for h in range(B_HEADS):
        hh = jnp.concatenate(hh_rows[h], axis=0)
        hn = hh * lax.rsqrt(jnp.mean(hh * hh, axis=-1, keepdims=True) + EPS) * gn[:, h * B_DV:(h + 1) * B_DV]
        obs.append(jax.nn.sigmoid(og[:, h * B_DV:(h + 1) * B_DV]) * hn)
    ob_ref[...] = jnp.concatenate(obs, axis=1)


def mix_sample(z, rel_table, sinks, ck, cv, c0, n0, m0, conv_buf, conv_w, b_gates_pad, g_norm):
    nb = z.shape[0]
    tb = min(SAMPLE_TILE, nb)
    n_buf = ck.shape[1]
    bkt = _t5_buckets(n_buf - np.arange(n_buf))[None, :]
    smem = pl.BlockSpec(memory_space=pltpu.SMEM)
    const = lambda shape: pl.BlockSpec(shape, lambda i: (0,) * len(shape))
    lead = lambda shape: pl.BlockSpec((tb,) + shape, lambda i: (i,) + (0,) * len(shape))
    return pl.pallas_call(
        functools.partial(_mix_sample_body, tb=tb),
        grid=(nb // tb,),
        in_specs=[smem, smem, const((1, n_buf)), lead((Z_COLS,)), lead((n_buf, A_KV)), lead((n_buf, A_KV)),
                  lead((B_HEADS, B_DK, B_DV)), lead((B_HEADS, B_DK)), lead((B_HEADS,)),
                  lead((B_CONV - 1, 2 * B_QK)), const((B_CONV, 2 * B_QK)), const((1, LANES)), const((1, B_V))],
        out_specs=[lead((A_Q,)), lead((B_V,)), lead((n_buf, A_KV)), lead((n_buf, A_KV)),
                   lead((B_HEADS, B_DK, B_DV)), lead((B_HEADS, B_DK)), lead((LANES,)),
                   lead((B_CONV - 1, 2 * B_QK))],
        out_shape=[jax.ShapeDtypeStruct((nb, A_Q), F32), jax.ShapeDtypeStruct((nb, B_V), F32),
                   jax.ShapeDtypeStruct(ck.shape, F32), jax.ShapeDtypeStruct(cv.shape, F32),
                   jax.ShapeDtypeStruct(c0.shape, F32), jax.ShapeDtypeStruct(n0.shape, F32),
                   jax.ShapeDtypeStruct((nb, LANES), F32), jax.ShapeDtypeStruct(conv_buf.shape, F32)],
        compiler_params=_cparams("parallel"),
        name="mix_sample",
    )(rel_table, sinks, jnp.asarray(bkt), z, ck, cv, c0, n0, m0, conv_buf, conv_w, b_gates_pad, g_norm)


def _store_row_tiles(ref, val, rows):
    for s in range(ROW_TILES):
        ref[pl.ds(s, rows, stride=ROW_TILES), :] = val[:, s * LANES:(s + 1) * LANES]


def _load_row_tiles(ref, rows, start=0, stride=ROW_TILES):
    return jnp.concatenate([ref[pl.ds(start + s, rows, stride=stride), :] for s in range(ROW_TILES)], axis=1)


def _route(logits):
    lane = lax.broadcasted_iota(I32, logits.shape, 1)
    big = jnp.int32(1 << 20)
    gl = jnp.where(lane < N_GROUPS, logits, NEG_INF)
    gmax = jnp.max(gl, axis=-1, keepdims=True)
    gidx = jnp.min(jnp.where(gl == gmax, lane, big), axis=-1, keepdims=True)
    g_gate = 1.0 / jnp.sum(jnp.exp(gl - gmax), axis=-1, keepdims=True)
    lo = N_GROUPS + gidx * EXPERTS_PER_GROUP
    el = jnp.where((lane >= lo) & (lane < lo + EXPERTS_PER_GROUP), logits, NEG_INF)
    v1 = jnp.max(el, axis=-1, keepdims=True)
    i1 = jnp.min(jnp.where(el == v1, lane, big), axis=-1, keepdims=True)
    el2 = jnp.where(lane == i1, NEG_INF, el)
    v2 = jnp.max(el2, axis=-1, keepdims=True)
    i2 = jnp.min(jnp.where(el2 == v2, lane, big), axis=-1, keepdims=True)
    t = jnp.exp(v2 - v1)
    w1 = g_gate / (1.0 + t)
    w2 = g_gate * t / (1.0 + t)
    gate = jnp.where(lane == 0, w1, jnp.where(lane == 1, w2, 0.0))
    return i1 - N_GROUPS, i2 - N_GROUPS, gate


def _local_sort(e0, e1):
    tm = e0.shape[0]
    lane = lax.broadcasted_iota(I32, (tm, LANES), 1)
    oh0 = (lane == e0).astype(BF16)
    oh1 = (lane == e1).astype(BF16)
    r = lax.broadcasted_iota(I32, (tm, tm), 0)
    c = lax.broadcasted_iota(I32, (tm, tm), 1)
    before = (c < r).astype(BF16)
    cnt0 = jnp.sum(oh0.astype(F32), axis=0, keepdims=True)
    run_n = cnt0 + jnp.sum(oh1.astype(F32), axis=0, keepdims=True)
    er = lax.broadcasted_iota(I32, (LANES, LANES), 0)
    ec = lax.broadcasted_iota(I32, (LANES, LANES), 1)
    slot_n = jnp.floor((run_n + (RUN_SLOT - 1)) * (1.0 / RUN_SLOT)) * RUN_SLOT
    run_l = jnp.dot(slot_n, (er < ec).astype(F32), precision=HI, preferred_element_type=F32)
    w0 = jnp.dot(before, oh0, preferred_element_type=F32) + run_l
    w1 = jnp.dot(before, oh1, preferred_element_type=F32) + run_l + cnt0
    p0 = jnp.sum(jnp.where(lane == e0, w0, 0.0), axis=-1, keepdims=True)
    p1 = jnp.sum(jnp.where(lane == e1, w1, 0.0), axis=-1, keepdims=True)
    return jnp.where(lane == 0, p0, jnp.where(lane == 1, p1, 0.0)), run_n


MOE_CHUNK = 512
RUN_SLOT = 32


def _moe_chunk(rows, precise):
    return min(rows, 128 if precise else MOE_CHUNK)


def _proj_router_body(*refs, n_in, has_bias, precise, tm):
    a_refs = refs[:n_in]
    w_refs = refs[n_in:2 * n_in]
    k = 2 * n_in
    bias_ref = refs[k] if has_bias else None
    k += 1 if has_bias else 0
    x_ref, g_ref, wr_ref, br_ref, x1_ref, h8_ref, lpos_ref, gate_ref, runn_ref = refs[k:]
    acc = x_ref[...]
    if has_bias:
        acc = acc + bias_ref[...]
    for a_ref, w_ref in zip(a_refs, w_refs):
        acc = acc + _mm(a_ref[...], w_ref[...], precise)
    x1_ref[...] = acc
    h = _rms(acc, g_ref[...])
    _store_row_tiles(h8_ref, h, tm)
    wr = wr_ref[...]
    if precise:
        logits = jnp.dot(h, wr, precision=HI, preferred_element_type=F32)
    else:
        h_hi = h.astype(BF16)
        h_lo = (h - h_hi.astype(F32)).astype(BF16)
        w_hi = wr.astype(BF16)
        w_lo = (wr - w_hi.astype(F32)).astype(BF16)
        logits = (jnp.dot(h_hi, w_hi, preferred_element_type=F32) + jnp.dot(h_lo, w_hi, preferred_element_type=F32)
                  + jnp.dot(h_hi, w_lo, preferred_element_type=F32))
    e0, e1, gate = _route(logits + br_ref[...])
    lpos, run_n = _local_sort(e0, e1)
    lpos_ref[...] = (lpos.T[:SUBLANES] * ROW_TILES).astype(I32)
    gate_ref[...] = gate.T[:SUBLANES]
    runn_ref[...] = jnp.broadcast_to(run_n, runn_ref.shape).astype(I32)


def proj_router(a_list, w_list, bias, x, g, wr, br, *, precise):
    rows, d = x.shape
    tm = _moe_chunk(rows, precise)
    n_in = len(a_list)
    row_spec = lambda width: pl.BlockSpec((tm, width), lambda i: (i, 0))
    const = lambda shape: pl.BlockSpec(shape, lambda i: (0,) * len(shape))
    in_specs = [row_spec(a.shape[1]) for a in a_list] + [const(w.shape) for w in w_list]
    args = list(a_list) + list(w_list)
    if bias is not None:
        in_specs.append(const((1, d)))
        args.append(bias)
    in_specs += [row_spec(d), const((1, d)), const((d, LANES)), const((1, LANES))]
    args += [x, g, wr, br]
    return pl.pallas_call(
        functools.partial(_proj_router_body, n_in=n_in, has_bias=bias is not None, precise=precise, tm=tm),
        grid=(rows // tm,),
        in_specs=in_specs,
        out_specs=[row_spec(d), pl.BlockSpec((tm * ROW_TILES, LANES), lambda i: (i, 0)),
                   pl.BlockSpec((SUBLANES, tm), lambda i: (i, 0)), pl.BlockSpec((SUBLANES, tm), lambda i: (i, 0)),
                   pl.BlockSpec((SUBLANES, LANES), lambda i: (i, 0))],
        out_shape=[jax.ShapeDtypeStruct((rows, d), F32), jax.ShapeDtypeStruct((rows * ROW_TILES, LANES), F32),
                   jax.ShapeDtypeStruct((rows // tm * SUBLANES, tm), I32),
                   jax.ShapeDtypeStruct((rows // tm * SUBLANES, tm), F32),
                   jax.ShapeDtypeStruct((rows // tm * SUBLANES, LANES), I32)],
        compiler_params=_cparams("parallel"),
        name="proj_router",
    )(*args)


def _rows_at(offset):
    return pl.ds(pl.multiple_of(offset, ROW_TILES), ROW_TILES)


def _tile_rows(r, n=1):
    return pl.ds(pl.multiple_of(r * ROW_TILES, ROW_TILES), n * ROW_TILES)


def _pow2_pieces(limit):
    p = 1
    while p * 2 <= limit:
        p *= 2
    out = []
    while p >= 1:
        out.append(p)
        p //= 2
    return out


COMMON_PIECE = 32


def _for_each_piece(n, pieces, fn):
    def emit(ps):
        for p in ps:
            @pl.when((n & p) != 0)
            def _(p=p):
                fn(n & ~(2 * p - 1), p)

    big = [p for p in pieces if p > COMMON_PIECE]
    if big:
        @pl.when(n > 2 * COMMON_PIECE - 1)
        def _():
            emit(big)
    emit([p for p in pieces if p <= COMMON_PIECE])


def _piece_rows(first_row, i):
    return _tile_rows(first_row + i * RUN_SLOT, RUN_SLOT)


def _dispatch_body(rg_ref, sl_ref, np_ref, npc_ref, ps_ref, pn_ref, tail_ref, lpos_ref, h8_ref, xs_hbm,
                   stage, zbuf, sem, zsem, *, chunk, nch, blk):
    c = pl.program_id(0)
    slot = c % 2
    pad_pieces = _pow2_pieces(blk + RUN_SLOT - 1)

    def wait_pieces(count, buf):
        def one(i, carry):
            pltpu.make_async_copy(stage.at[buf, _tile_rows(0, RUN_SLOT)], stage.at[buf, _tile_rows(0, RUN_SLOT)],
                                  sem).wait()
            return carry
        lax.fori_loop(0, count, one, 0)

    @pl.when(c == 0)
    def _():
        stage[...] = jnp.zeros(stage.shape, F32)

    def copy_tok(t, carry):
        row = h8_ref[_tile_rows(t), :]
        stage[slot, _rows_at(lpos_ref[0, 0, t]), :] = row
        stage[slot, _rows_at(lpos_ref[0, 0, chunk + t]), :] = row
        return carry
    lax.fori_loop(0, chunk, copy_tok, 0, unroll=8)

    @pl.when(c > 0)
    def _():
        wait_pieces(npc_ref[jnp.maximum(c - 1, 0)], 1 - slot)

    def send_runs(e, carry):
        k = c * N_EXPERTS + e

        def one(i, carry2):
            pltpu.make_async_copy(stage.at[slot, _piece_rows(sl_ref[k], i)], xs_hbm.at[_piece_rows(rg_ref[k], i)],
                                  sem).start()
            return carry2
        lax.fori_loop(0, np_ref[k], one, 0)
        return carry
    lax.fori_loop(0, N_EXPERTS, send_runs, 0)

    @pl.when(c == nch - 1)
    def _():
        wait_pieces(npc_ref[c], slot)
        zbuf[...] = jnp.zeros(zbuf.shape, F32)

        def pad_dmas(e, op):
            def one(off, p):
                cp = pltpu.make_async_copy(zbuf.at[_tile_rows(0, p)], xs_hbm.at[_tile_rows(ps_ref[e] + off, p)], zsem)
                cp.start() if op == 0 else cp.wait()
            _for_each_piece(pn_ref[e], pad_pieces, one)

        def issue(e, carry):
            pad_dmas(e, 0)
            return carry

        def wait(e, carry):
            pad_dmas(e, 1)
            return carry
        lax.fori_loop(0, N_EXPERTS, issue, 0)
        lax.fori_loop(0, N_EXPERTS, wait, 0)

        half = blk // 2

        def tail_dmas(i, op):
            cp = pltpu.make_async_copy(zbuf.at[_tile_rows(0, half)],
                                       xs_hbm.at[_tile_rows(tail_ref[0] + i * half, half)], zsem)
            cp.start() if op == 0 else cp.wait()

        def tail_issue(i, carry):
            tail_dmas(i, 0)
            return carry

        def tail_wait(i, carry):
            tail_dmas(i, 1)
            return carry
        lax.fori_loop(0, tail_ref[1], tail_issue, 0)
        lax.fori_loop(0, tail_ref[1], tail_wait, 0)


def _stage_rows(chunk):
    return TOP_K * chunk + N_EXPERTS * (RUN_SLOT - 1) // RUN_SLOT * RUN_SLOT + RUN_SLOT


def dispatch(h8, plan):
    chunk, nch, blk = plan['chunk'], plan['nch'], plan['blk']
    n_slots = plan['nblk'] * blk
    gs = pltpu.PrefetchScalarGridSpec(
        num_scalar_prefetch=7,
        grid=(nch,),
        in_specs=[pl.BlockSpec((1, 1, TOP_K * chunk), lambda c, *_: (c, 0, 0), memory_space=pltpu.SMEM),
                  pl.BlockSpec((chunk * ROW_TILES, LANES), lambda c, *_: (c, 0))],
        out_specs=pl.BlockSpec(memory_space=pl.ANY),
        scratch_shapes=[pltpu.VMEM((2, _stage_rows(chunk) * ROW_TILES, LANES), F32),
                        pltpu.VMEM((blk * ROW_TILES, LANES), F32),
                        pltpu.SemaphoreType.DMA(()), pltpu.SemaphoreType.DMA(())],
    )
    return pl.pallas_call(
        functools.partial(_dispatch_body, chunk=chunk, nch=nch, blk=blk),
        grid_spec=gs,
        out_shape=jax.ShapeDtypeStruct((n_slots * ROW_TILES, LANES), F32),
        compiler_params=_cparams("arbitrary"),
        name="dispatch",
    )(plan['run_g'], plan['slot_l'], plan['n_piece'], plan['n_piece_chunk'], plan['pad_start'], plan['pad_n'],
      plan['tail'], plan['lpos'], h8)


FF_CHUNK = 512


def _experts_body(be_ref, nv_ref, pe_ref, xs_ref, wg_ref, wu_ref, wd_ref, ys_ref, xb, *wscr, precise, blk, layer):
    j = pl.program_id(0)
    nv = nv_ref[0]
    slot = j % 2
    k = j - 1

    @pl.when(j == 0)
    def _():
        xb[1] = jnp.zeros(xb.shape[1:], xb.dtype)

    if not precise:
        wgb, wub, wdb, wgf, wuf, wdf, wsem = wscr

        def weight_copies(e, ws):
            return [pltpu.make_async_copy(src.at[layer, e], dst.at[ws], wsem.at[ws])
                    for src, dst in ((wg_ref, wgf), (wu_ref, wuf), (wd_ref, wdf))]

        @pl.when(j == 0)
        def _():
            for cp in weight_copies(be_ref[0], 0):
                cp.start()

        kc = jnp.clip(k, 0, nv - 1)
        e = be_ref[kc]
        ws = pe_ref[N_EXPERTS + e]

        @pl.when((k >= 0) & (k < nv) & ((k == 0) | (e != be_ref[jnp.maximum(kc - 1, 0)])))
        def _():
            for cp in weight_copies(e, ws):
                cp.wait()
            wgb[...] = wgf[ws].astype(BF16)
            wub[...] = wuf[ws].astype(BF16)
            wdb[...] = wdf[ws].astype(BF16)
            nxt = pe_ref[e]

            @pl.when(nxt < nv)
            def _():
                for cp in weight_copies(be_ref[jnp.minimum(nxt, nv - 1)], 1 - ws):
                    cp.start()

    @pl.when(j <= nv)
    def _():
        xb[slot] = _load_row_tiles(xs_ref, blk).astype(xb.dtype)
        if precise:
            wg, wu, wd = wg_ref[0], wu_ref[0], wd_ref[0]
        else:
            wg, wu, wd = wgb, wub, wdb
        xm = xb[1 - slot]
        gt = _mm(xm, wg[...], precise)
        up = _mm(xm, wu[...], precise)
        _store_row_tiles(ys_ref, _mm(gt * jax.nn.sigmoid(gt) * up, wd[...], precise), blk)

    @pl.when(j > nv)
    def _():
        ys_ref[...] = jnp.zeros(ys_ref.shape, F32)


def experts(xs, plan, wg, wu, wd, layer, *, precise):
    nblk, rows = plan['nblk'], plan['blk'] * ROW_TILES
    d, ff = wg.shape[2], wg.shape[3]
    blk = lambda j, be, nv, pe: (jnp.minimum(j, nv[0] - 1), 0)
    if precise:
        wspec = lambda shape: pl.BlockSpec((None, 1) + shape,
                                           lambda j, be, nv, pe: (layer, be[jnp.clip(j - 1, 0, nv[0] - 1)], 0, 0))
        wspecs = [wspec((d, ff)), wspec((d, ff)), wspec((ff, d))]
        wscratch = []
    else:
        wspecs = [pl.BlockSpec(memory_space=pl.ANY)] * 3
        wscratch = [pltpu.VMEM((d, ff), BF16), pltpu.VMEM((d, ff), BF16), pltpu.VMEM((ff, d), BF16),
                    pltpu.VMEM((2, d, ff), F32), pltpu.VMEM((2, d, ff), F32), pltpu.VMEM((2, ff, d), F32),
                    pltpu.SemaphoreType.DMA((2,))]
    gs = pltpu.PrefetchScalarGridSpec(
        num_scalar_prefetch=3,
        grid=(nblk + 1,),
        in_specs=[pl.BlockSpec((rows, LANES), blk)] + wspecs,
        out_specs=pl.BlockSpec((rows, LANES), lambda j, be, nv, pe: (jnp.maximum(j - 1, 0), 0)),
        scratch_shapes=[pltpu.VMEM((2, plan['blk'], d), F32 if precise else BF16)] + wscratch,
    )
    return pl.pallas_call(
        functools.partial(_experts_body, precise=precise, blk=plan['blk'], layer=layer),
        grid_spec=gs,
        out_shape=jax.ShapeDtypeStruct(xs.shape, F32),
        compiler_params=_cparams("arbitrary"),
        name="experts",
    )(plan['block_e'], plan['n_used'], plan['expert_tab'], xs, wg, wu, wd)


def moe_plan(lpos8, gate8, runn8, blk):
    nch = runn8.shape[0] // SUBLANES
    chunk = lpos8.shape[1]
    n_assign = nch * chunk * TOP_K
    nblk = (n_assign + N_EXPERTS * (blk - 1 + RUN_SLOT) + blk - 1) // blk
    run_n = runn8.reshape(nch, SUBLANES, LANES)[:, 0, :N_EXPERTS]
    per_chunk = lambda a: a.reshape(nch, 1, SUBLANES * chunk)[:, :, :TOP_K * chunk]
    counts = jnp.sum(run_n, axis=0)
    padded = (counts + RUN_SLOT + blk - 1) // blk * blk
    pends = jnp.cumsum(padded)
    pstarts = pends - padded
    run_g = pstarts[None, :] + jnp.cumsum(run_n, axis=0) - run_n
    n_piece = (run_n + RUN_SLOT - 1) // RUN_SLOT
    slot_l = (jnp.cumsum(n_piece, axis=1) - n_piece) * RUN_SLOT
    blk_start = jnp.arange(nblk, dtype=I32) * blk
    block_e = jnp.minimum(jnp.sum((pends[None, :] <= blk_start[:, None]).astype(I32), axis=1), N_EXPERTS - 1)
    return dict(chunk=chunk, nch=nch, nblk=nblk, blk=blk,
                run_g=run_g.reshape(-1).astype(I32), slot_l=slot_l.reshape(-1).astype(I32),
                n_piece=n_piece.reshape(-1).astype(I32), n_piece_chunk=jnp.sum(n_piece, axis=1).astype(I32),
                pad_start=(pstarts + counts).astype(I32), pad_n=(padded - counts).astype(I32),
                lpos=per_chunk(lpos8), gate=per_chunk(gate8), block_e=block_e.astype(I32),
                expert_tab=jnp.concatenate([pends // blk, jnp.arange(N_EXPERTS, dtype=I32) % 2]).astype(I32),
                n_used=(pends[-1:] // blk).astype(I32),
                tail=jnp.stack([pends[-1], 2 * (nblk - pends[-1] // blk)]).astype(I32))


def _combine(rg_ref, sl_ref, np_ref, npc_ref, lpos_ref, gate_ref, x_ref, ys_hbm, ystage, comb, sem, *, chunk, nch):
    c = pl.program_id(0)
    slot = c % 2

    def fetch(cc, sl):
        def per_e(e, carry):
            k = cc * N_EXPERTS + e

            def one(i, carry2):
                pltpu.make_async_copy(ys_hbm.at[_piece_rows(rg_ref[k], i)], ystage.at[sl, _piece_rows(sl_ref[k], i)],
                                      sem.at[sl]).start()
                return carry2
            lax.fori_loop(0, np_ref[k], one, 0)
            return carry
        lax.fori_loop(0, N_EXPERTS, per_e, 0)

    @pl.when(c == 0)
    def _():
        fetch(0, 0)

    @pl.when(c + 1 < nch)
    def _():
        fetch(c + 1, 1 - slot)

    def wait_piece(i, carry):
        pltpu.make_async_copy(ystage.at[slot, _tile_rows(0, RUN_SLOT)], ystage.at[slot, _tile_rows(0, RUN_SLOT)],
                              sem.at[slot]).wait()
        return carry
    lax.fori_loop(0, npc_ref[c], wait_piece, 0)

    def per_tok(t, carry):
        y0 = ystage[slot, _rows_at(lpos_ref[0, 0, t]), :]
        y1 = ystage[slot, _rows_at(lpos_ref[0, 0, chunk + t]), :]
        comb[_tile_rows(t), :] = gate_ref[0, 0, t] * y0 + gate_ref[0, 0, chunk + t] * y1
        return carry
    lax.fori_loop(0, chunk, per_tok, 0, unroll=8)
    return x_ref[...] + _load_row_tiles(comb, chunk)


def _combine_glu_body(rg_ref, sl_ref, np_ref, npc_ref, lpos_ref, gate_ref, x_ref, ys_hbm, g_ref, w_ref, b_ref,
                      x2_ref, u_ref, ystage, comb, sem, *, chunk, nch, precise):
    x2 = _combine(rg_ref, sl_ref, np_ref, npc_ref, lpos_ref, gate_ref, x_ref, ys_hbm, ystage, comb, sem,
                  chunk=chunk, nch=nch)
    x2_ref[...] = x2
    zz = _mm(_rms(x2, g_ref[...]), w_ref[...], precise) + b_ref[...]
    half = zz.shape[1] // 2
    u_ref[...] = zz[:, :half] * jax.nn.sigmoid(zz[:, half:])


def _combine_final_body(rg_ref, sl_ref, np_ref, npc_ref, lpos_ref, gate_ref, x_ref, ys_hbm, g_ref, o_ref,
                        ystage, comb, sem, *, chunk, nch):
    x2 = _combine(rg_ref, sl_ref, np_ref, npc_ref, lpos_ref, gate_ref, x_ref, ys_hbm, ystage, comb, sem,
                  chunk=chunk, nch=nch)
    o_ref[...] = _rms(x2, g_ref[...])


def _combine_call(body, plan, x, ys, extra, extra_specs, out_specs, out_shape, name):
    chunk, nch = plan['chunk'], plan['nch']
    d = x.shape[1]
    smem_blk = pl.BlockSpec((1, 1, TOP_K * chunk), lambda c, *_: (c, 0, 0), memory_space=pltpu.SMEM)
    gs = pltpu.PrefetchScalarGridSpec(
        num_scalar_prefetch=4,
        grid=(nch,),
        in_specs=[smem_blk, smem_blk, pl.BlockSpec((chunk, d), lambda c, *_: (c, 0)),
                  pl.BlockSpec(memory_space=pl.ANY)] + extra_specs,
        out_specs=out_specs,
        scratch_shapes=[pltpu.VMEM((2, _stage_rows(chunk) * ROW_TILES, LANES), F32),
                        pltpu.VMEM((chunk * ROW_TILES, LANES), F32), pltpu.SemaphoreType.DMA((2,))],
    )
    return pl.pallas_call(
        functools.partial(body, chunk=chunk, nch=nch),
        grid_spec=gs,
        out_shape=out_shape,
        compiler_params=_cparams("arbitrary"),
        name=name,
    )(plan['run_g'], plan['slot_l'], plan['n_piece'], plan['n_piece_chunk'], plan['lpos'], plan['gate'], x, ys,
      *extra)


def combine_glu(x, ys, plan, g, w, b, *, precise):
    rows, d = x.shape
    chunk = plan['chunk']
    cols = w.shape[1]
    const = lambda shape: pl.BlockSpec(shape, lambda c, *_: (0,) * len(shape))
    row_spec = lambda width: pl.BlockSpec((chunk, width), lambda c, *_: (c, 0))
    return _combine_call(
        functools.partial(_combine_glu_body, precise=precise), plan, x, ys, [g, w, b],
        [const((1, d)), const((d, cols)), const((1, cols))], [row_spec(d), row_spec(cols // 2)],
        [jax.ShapeDtypeStruct((rows, d), F32), jax.ShapeDtypeStruct((rows, cols // 2), F32)], "combine_glu")


def combine_final(x, ys, plan, g):
    rows, d = x.shape
    chunk = plan['chunk']
    return _combine_call(
        _combine_final_body, plan, x, ys, [g], [pl.BlockSpec((1, d), lambda c, *_: (0, 0))],
        pl.BlockSpec((chunk, d), lambda c, *_: (c, 0)), jax.ShapeDtypeStruct((rows, d), F32), "combine_final")


CONV_TILE = 512
CONV_HIST = 32


def _ln_swish(y, g, b):
    yc = y - jnp.mean(y, axis=-1, keepdims=True)
    yn = yc * lax.rsqrt(jnp.mean(yc * yc, axis=-1, keepdims=True) + EPS) * g + b
    return yn * jax.nn.sigmoid(yn)


CONV_ROWS = 64
LN_ROWS = 16
LN_UNROLL = 8


def _dwconv_prompt_body(u_ref, w_ref, bdw_ref, g_ref, b_ref, o_ref, ext, y_sc, *, tt):
    t = pl.program_id(1)
    n_lt = ext.shape[0]

    @pl.when(t == 0)
    def _():
        ext[:, 0:CONV_HIST, :] = jnp.zeros((n_lt, CONV_HIST, LANES), F32)

    @pl.when(t > 0)
    def _():
        ext[:, 0:CONV_HIST, :] = ext[:, tt:tt + CONV_HIST, :]

    for j in range(n_lt):
        ext[j, CONV_HIST:CONV_HIST + tt, :] = u_ref[:, j * LANES:(j + 1) * LANES]
    off = CONV_HIST - (C_KERNEL - 1)
    for j in range(n_lt):
        wj = w_ref[:, j * LANES:(j + 1) * LANES]
        bj = bdw_ref[:, j * LANES:(j + 1) * LANES]
        for c in range(tt // CONV_ROWS):
            acc = ext[j, pl.ds(off + c * CONV_ROWS, CONV_ROWS), :] * wj[0:1] + bj
            for k in range(1, C_KERNEL):
                acc = acc + ext[j, pl.ds(off + k + c * CONV_ROWS, CONV_ROWS), :] * wj[k:k + 1]
            y_sc[c * CONV_ROWS:(c + 1) * CONV_ROWS, j * LANES:(j + 1) * LANES] = acc

    def ln_rows(r, carry):
        rows = pl.ds(pl.multiple_of(r * LN_ROWS, LN_ROWS), LN_ROWS)
        o_ref[rows, :] = _ln_swish(y_sc[rows, :], g_ref[...], b_ref[...]).astype(o_ref.dtype)
        return carry
    lax.fori_loop(0, tt // LN_ROWS, ln_rows, 0, unroll=LN_UNROLL)


def dwconv_prompt(u, w, b_dw, ln_g, ln_b, bsz, seq):
    tt = min(CONV_TILE, seq)
    nt = seq // tt
    d = u.shape[1]
    const = lambda shape: pl.BlockSpec(shape, lambda b, t: (0,) * len(shape))
    return pl.pallas_call(
        functools.partial(_dwconv_prompt_body, tt=tt),
        grid=(bsz, nt),
        in_specs=[pl.BlockSpec((tt, d), lambda b, t: (b * nt + t, 0)), const((C_KERNEL, d)), const((1, d)),
                  const((1, d)), const((1, d))],
        out_specs=pl.BlockSpec((tt, d), lambda b, t: (b * nt + t, 0)),
        out_shape=jax.ShapeDtypeStruct((bsz * seq, d), BF16),
        scratch_shapes=[pltpu.VMEM((d // LANES, CONV_HIST + tt, LANES), F32), pltpu.VMEM((tt, d), F32)],
        compiler_params=_cparams("arbitrary", "arbitrary"),
        name="dwconv_prompt",
    )(u, w, b_dw, ln_g, ln_b)


def _dwconv_sample_body(u_ref, buf_ref, w_ref, bdw_ref, g_ref, b_ref, o_ref, nbuf_ref, *, tb):
    w = w_ref[...]
    rows = []
    for i in range(tb):
        hist = buf_ref[i]
        ur = u_ref[i:i + 1, :]
        rows.append(jnp.sum(hist * w[:C_KERNEL - 1], axis=0, keepdims=True) + ur * w[C_KERNEL - 1:C_KERNEL])
        nbuf_ref[i] = jnp.concatenate([hist[1:], ur], axis=0)
    y = jnp.concatenate(rows, axis=0) + bdw_ref[...]
    o_ref[...] = _ln_swish(y, g_ref[...], b_ref[...])


def dwconv_sample(u, buf, w, b_dw, ln_g, ln_b):
    nb, d = u.shape
    tb = min(SAMPLE_TILE, nb)
    const = lambda shape: pl.BlockSpec(shape, lambda i: (0,) * len(shape))
    return pl.pallas_call(
        functools.partial(_dwconv_sample_body, tb=tb),
        grid=(nb // tb,),
        in_specs=[pl.BlockSpec((tb, d), lambda i: (i, 0)), pl.BlockSpec((tb, C_KERNEL - 1, d), lambda i: (i, 0, 0)),
                  const((C_KERNEL, d)), const((1, d)), const((1, d)), const((1, d))],
        out_specs=[pl.BlockSpec((tb, d), lambda i: (i, 0)), pl.BlockSpec((tb, C_KERNEL - 1, d), lambda i: (i, 0, 0))],
        out_shape=[jax.ShapeDtypeStruct((nb, d), F32), jax.ShapeDtypeStruct(buf.shape, F32)],
        compiler_params=_cparams("parallel"),
        name="dwconv_sample",
    )(u, buf, w, b_dw, ln_g, ln_b)


def _moe(h8, lpos8, gate8, runn8, wg, wu, wd, layer, *, blk, precise):
    plan = moe_plan(lpos8, gate8, runn8, blk)
    xs = dispatch(h8, plan)
    return experts(xs, plan, wg, wu, wd, layer, precise=precise), plan


def _trunk(x, caches, p, *, prompt):
    bsz, seq, d = x.shape
    rows = bsz * seq
    precise = not prompt
    wdt = F32 if precise else BF16
    xf = x.reshape(rows, d)
    row = lambda v: v.reshape(1, -1).astype(F32)

    z = norm_proj(xf, row(p['norm_mix'][0]), p['w_in'].astype(wdt), precise=precise)
    if prompt:
        z, zb = z
        att = attn_prompt(z, zb, p['rel_table'], p['sinks'], bsz, seq)
        out_b, c1, n1, m1 = mlstm_prompt(z, zb, p['conv_w'], p['b_gates'], p['g_mnorm'], bsz, seq)
        z3 = z.reshape(bsz, seq, P_COLS)
        new_k = z3[:, seq - WINDOW:, P_KA:P_KA + A_KV].reshape(bsz, WINDOW, A_KV_HEADS, A_HEAD_DIM)
        new_v = z3[:, seq - WINDOW:, P_VA:P_VA + A_KV].reshape(bsz, WINDOW, A_KV_HEADS, A_HEAD_DIM)
        new_conv = z3[:, seq - (B_CONV - 1):, P_QK:P_QK + 2 * B_QK]
        n1 = n1[:, :B_HEADS]
        m1 = m1[:, :B_HEADS, 0]
    else:
        ck, cv, c0, n0, m0, cbuf = caches[:6]
        n_buf = ck.shape[1]
        att, out_b, new_k, new_v, c1, n1, m1, new_conv = mix_sample(
            z, p['rel_table'], p['sinks'], ck.reshape(bsz, n_buf, A_KV), cv.reshape(bsz, n_buf, A_KV),
            c0, n0, m0, cbuf, p['conv_w'], p['b_gates'], p['g_mnorm'])
        new_k = new_k.reshape(bsz, n_buf, A_KV_HEADS, A_HEAD_DIM)
        new_v = new_v.reshape(bsz, n_buf, A_KV_HEADS, A_HEAD_DIM)
        m1 = m1[:, :B_HEADS]
    w_out = p['w_out'].astype(wdt)
    x1, h8, lpos, gate, runn = proj_router([att, out_b], [w_out[:A_Q], w_out[A_Q:]], None, xf, row(p['norm_ffn'][0]),
                                           p['w_router'][0], p['b_router'][0], precise=precise)
    blk = EXPERT_BLOCK_PRECISE if precise else EXPERT_BLOCK
    ys, plan = _moe(h8, lpos, gate, runn, p['w_eg'], p['w_eu'], p['w_ed'], 0, blk=blk, precise=precise)

    x2, u = combine_glu(x1, ys, plan, row(p['norm_mix'][1]), p['w_pw1'].astype(wdt), row(p['b_pw1']),
                        precise=precise)
    if prompt:
        yc = dwconv_prompt(u, p['w_dw'], row(p['b_dw']), row(p['ln_g']), row(p['ln_b']), bsz, seq)
        new_cbuf = u.reshape(bsz, seq, d)[:, seq - (C_KERNEL - 1):]
    else:
        yc, new_cbuf = dwconv_sample(u, caches[6], p['w_dw'], row(p['b_dw']), row(p['ln_g']), row(p['ln_b']))
    x3, h8, lpos, gate, runn = proj_router([yc], [p['w_pw2'].astype(wdt)], row(p['b_pw2']), x2, row(p['norm_ffn'][1]),
                                           p['w_router'][1], p['b_router'][1], precise=precise)
    ys, plan = _moe(h8, lpos, gate, runn, p['w_eg'], p['w_eu'], p['w_ed'], 1, blk=blk, precise=False)
    y = combine_final(x3, ys, plan, row(p['norm_final']))
    add_layer = lambda t: t[None]
    return (y.reshape(bsz, seq, d),) + tuple(add_layer(t) for t in (new_k, new_v, c1, n1, m1, new_conv, new_cbuf))


def kernel(x_prompt, x_sample, cache_win_k, cache_win_v, state_mlstm_c, state_mlstm_n, state_mlstm_m, state_mlstm_conv, state_conv, norm_mix, norm_ffn, norm_final, rel_bias_table, w_in_mix, b_mlstm_gates, w_mlstm_qk_conv, attn_sinks, g_mlstm_norm, w_out_mix, w_pw1, b_pw1, w_dw, b_dw, ln_conv_g, ln_conv_b, w_pw2, b_pw2, w_router_group, b_router_group, w_router_expert, b_router_expert, w_expert_gate, w_expert_up, w_expert_down):
    w_in = w_in_mix[0]
    s_q, s_k, s_v, s_qk, s_vb, s_g = A_Q, A_Q + A_KV, A_Q + 2 * A_KV, A_Q + 2 * A_KV + 2 * B_QK, \
        A_Q + 2 * A_KV + 2 * B_QK + B_V, A_Q + 2 * A_KV + 2 * B_QK + B_V + 2 * B_HEADS
    w_in_r = jnp.concatenate([w_in[:, :s_q], w_in[:, s_v:s_qk], w_in[:, s_qk:s_vb], w_in[:, s_g:],
                              w_in[:, s_q:s_k], w_in[:, s_k:s_v], w_in[:, s_vb:s_g],
                              jnp.zeros((D_MODEL, LANES - 2 * B_HEADS), F32)], axis=1)
    b_gates = jnp.concatenate([b_mlstm_gates[0], jnp.zeros((LANES - 2 * B_HEADS,), F32)]).reshape(1, LANES)
    depth = w_router_group.shape[0]
    w_re = jnp.transpose(w_router_expert, (0, 2, 1, 3)).reshape(depth, D_MODEL, N_EXPERTS)
    w_router = jnp.concatenate([w_router_group, w_re,
                                jnp.zeros((depth, D_MODEL, LANES - N_GROUPS - N_EXPERTS), F32)], axis=-1)
    b_router = jnp.concatenate([b_router_group, b_router_expert.reshape(depth, N_EXPERTS),
                                jnp.zeros((depth, LANES - N_GROUPS - N_EXPERTS), F32)], axis=-1)[:, None, :]
    p = dict(norm_mix=norm_mix, norm_ffn=norm_ffn, norm_final=norm_final, rel_table=rel_bias_table,
             sinks=attn_sinks[0], w_in=w_in_r, b_gates=b_gates, conv_w=w_mlstm_qk_conv[0],
             g_mnorm=g_mlstm_norm[0].reshape(1, B_V), w_out=w_out_mix[0], w_pw1=w_pw1[0], b_pw1=b_pw1[0],
             w_dw=w_dw[0], b_dw=b_dw[0], ln_g=ln_conv_g[0], ln_b=ln_conv_b[0], w_pw2=w_pw2[0], b_pw2=b_pw2[0],
             w_router=w_router, b_router=b_router, w_eg=w_expert_gate, w_eu=w_expert_up, w_ed=w_expert_down)
    caches = (cache_win_k[0], cache_win_v[0], state_mlstm_c[0], state_mlstm_n[0], state_mlstm_m[0],
              state_mlstm_conv[0], state_conv[0])
    out_p = _trunk(x_prompt, None, p, prompt=True)
    out_s = _trunk(x_sample, caches, p, prompt=False)
    return (out_p[0], out_s[0]) + out_p[1:] + out_s[1:]
```

```python
import functools
import math

import numpy as np
import jax
import jax.numpy as jnp
from jax import lax
from jax.experimental import pallas as pl
from jax.experimental.pallas import tpu as pltpu

F32 = jnp.float32
BF16 = jnp.bfloat16
I32 = jnp.int32
HI = lax.Precision.HIGHEST
NEG_INF = float("-inf")

LANES = 128
SUBLANES = 8
VMEM_LIMIT = 56 * 1024 * 1024

D_MODEL = 1024
A_HEADS = 8
A_KV_HEADS = 2
A_GROUP = A_HEADS // A_KV_HEADS
A_HEAD_DIM = 64
WINDOW = 128
REL_BUCKETS = 32
REL_MAX_DIST = 128
B_HEADS = 4
B_DK = 64
B_DV = 128
B_CONV = 4
C_KERNEL = 31
N_GROUPS = 4
EXPERTS_PER_GROUP = 8
N_EXPERTS = N_GROUPS * EXPERTS_PER_GROUP
TOP_K = 2
EXPERT_FF = D_MODEL // 2
EXPERT_BLOCK = 512
EXPERT_BLOCK_PRECISE = 128
EPS = 1e-6

A_Q = A_HEADS * A_HEAD_DIM
A_KV = A_KV_HEADS * A_HEAD_DIM
B_QK = B_HEADS * B_DK
B_V = B_HEADS * B_DV
ROW_TILES = D_MODEL // LANES

Z_QA, Z_QK, Z_VB, Z_OG, Z_KA, Z_VA, Z_GATES = 0, 512, 1024, 1536, 2048, 2176, 2304
Z_COLS = 2432
MLSTM_CHUNK = 128


def _cparams(*sem):
    return pltpu.CompilerParams(dimension_semantics=sem, vmem_limit_bytes=VMEM_LIMIT)


def _rms(x, g):
    return x * lax.rsqrt(jnp.mean(x * x, axis=-1, keepdims=True) + EPS) * g


def _mm(a, w, precise):
    if not precise:
        return jnp.dot(a.astype(BF16), w, preferred_element_type=F32)
    a = a.astype(F32)
    a_hi = a.astype(BF16)
    a_lo = (a - a_hi.astype(F32)).astype(BF16)
    w_hi = w.astype(BF16)
    w_lo = (w - w_hi.astype(F32)).astype(BF16)
    return (jnp.dot(a_hi, w_hi, preferred_element_type=F32) + jnp.dot(a_lo, w_hi, preferred_element_type=F32)
            + jnp.dot(a_hi, w_lo, preferred_element_type=F32))


def _t5_buckets(dist):
    exact = REL_BUCKETS // 2
    d = np.maximum(dist, 0)
    large = exact + (np.log(np.maximum(d, 1).astype(np.float32) / exact)
                     / math.log(REL_MAX_DIST / exact) * (REL_BUCKETS - exact)).astype(np.int32)
    return np.where(d < exact, d, np.minimum(large, REL_BUCKETS - 1)).astype(np.int32)


P_QK, P_OG, P_KA, P_VA, P_GATES = 0, 512, 1024, 1152, 1280
P_COLS = 1408
PB_QA, PB_VB = 0, 512
PB_COLS = 1024


def _norm_proj_body(x_ref, g_ref, w_ref, *o_refs, precise):
    h = _rms(x_ref[...], g_ref[...])
    z = _mm(h, w_ref[...], precise)
    if len(o_refs) == 1:
        o_refs[0][...] = z
        return
    z32_ref, zb_ref = o_refs
    z32_ref[...] = jnp.concatenate([z[:, Z_QK:Z_QK + 2 * B_QK], z[:, Z_OG:Z_OG + B_V], z[:, Z_KA:Z_GATES + LANES]],
                                   axis=1)
    zb_ref[...] = jnp.concatenate([z[:, Z_QA:Z_QA + A_Q], z[:, Z_VB:Z_VB + B_V]], axis=1).astype(BF16)


def norm_proj(x, g, w, *, precise):
    rows, d = x.shape
    cols = w.shape[1]
    tm = min(rows, 128 if precise else 512)
    row_spec = lambda width: pl.BlockSpec((tm, width), lambda i: (i, 0))
    if precise:
        out_specs, out_shape = row_spec(cols), jax.ShapeDtypeStruct((rows, cols), F32)
    else:
        out_specs = [row_spec(P_COLS), row_spec(PB_COLS)]
        out_shape = [jax.ShapeDtypeStruct((rows, P_COLS), F32), jax.ShapeDtypeStruct((rows, PB_COLS), BF16)]
    return pl.pallas_call(
        functools.partial(_norm_proj_body, precise=precise),
        grid=(rows // tm,),
        in_specs=[row_spec(d), pl.BlockSpec((1, d), lambda i: (0, 0)), pl.BlockSpec((d, cols), lambda i: (0, 0))],
        out_specs=out_specs,
        out_shape=out_shape,
        compiler_params=_cparams("parallel"),
        name="norm_proj",
    )(x, g, w)


def _attn_prompt_body(tab_ref, sink_ref, bkt_ref, q_ref, kp_ref, kc_ref, vp_ref, vc_ref, o_ref, bias_ref, *, nq):
    b = pl.program_id(0)
    n = pl.program_id(1)

    @pl.when((b == 0) & (n == 0))
    def _():
        bk = bkt_ref[...]
        first = lax.broadcasted_iota(I32, bk.shape, 1) >= WINDOW
        for h in range(A_HEADS):
            acc = jnp.full(bk.shape, NEG_INF, F32)
            for t in range(REL_BUCKETS):
                acc = jnp.where(bk == t, tab_ref[t, h], acc)
            bias_ref[h] = acc
            bias_ref[A_HEADS + h] = jnp.where(first, acc, NEG_INF)

    k_all = jnp.concatenate([kp_ref[...], kc_ref[...]], axis=0).astype(BF16)
    v_all = jnp.concatenate([vp_ref[...], vc_ref[...]], axis=0).astype(BF16)
    for sub in range(nq):
        rows = slice(sub * WINDOW, (sub + 1) * WINDOW)
        q = q_ref[rows, :] * (A_HEAD_DIM ** -0.5)
        kb = k_all[sub * WINDOW:(sub + 2) * WINDOW]
        vb = v_all[sub * WINDOW:(sub + 2) * WINDOW]
        table = jnp.where(n == 0, A_HEADS, 0) if sub == 0 else 0
        outs = []
        for h in range(A_HEADS):
            kvh = h // A_GROUP
            qh = q[:, h * A_HEAD_DIM:(h + 1) * A_HEAD_DIM].astype(BF16)
            kh = kb[:, kvh * A_HEAD_DIM:(kvh + 1) * A_HEAD_DIM]
            vh = vb[:, kvh * A_HEAD_DIM:(kvh + 1) * A_HEAD_DIM]
            s = lax.dot_general(qh, kh, (((1,), (1,)), ((), ())), preferred_element_type=F32)
            s = s + bias_ref[table + h]
            sink = sink_ref[h]
            mx = jnp.maximum(jnp.max(s, axis=-1, keepdims=True), sink)
            p = jnp.exp(s - mx)
            den = jnp.sum(p, axis=-1, keepdims=True) + jnp.exp(sink - mx)
            outs.append(jnp.dot(p.astype(BF16), vh, preferred_element_type=F32) / den)
        o_ref[rows, :] = jnp.concatenate(outs, axis=1).astype(o_ref.dtype)


ATTN_BLOCKS = 2


def attn_prompt(z, zb, rel_table, sinks, bsz, seq):
    nb = seq // WINDOW
    dist = WINDOW + np.arange(WINDOW)[:, None] - np.arange(2 * WINDOW)[None, :]
    bkt = np.where((dist >= 0) & (dist <= WINDOW), _t5_buckets(dist), -1).astype(np.int32)
    kcol, vcol = P_KA // LANES, P_VA // LANES
    smem = pl.BlockSpec(memory_space=pltpu.SMEM)
    nq = ATTN_BLOCKS if nb % ATTN_BLOCKS == 0 else 1
    ns = nb // nq

    def cur(c):
        return pl.BlockSpec((nq * WINDOW, LANES), lambda b, n: (b * ns + n, c))

    def prev(c):
        return pl.BlockSpec((WINDOW, LANES), lambda b, n: (b * nb + jnp.maximum(n * nq - 1, 0), c))

    return pl.pallas_call(
        functools.partial(_attn_prompt_body, nq=nq),
        grid=(bsz, ns),
        in_specs=[smem, smem,
                  pl.BlockSpec((WINDOW, 2 * WINDOW), lambda b, n: (0, 0)),
                  pl.BlockSpec((nq * WINDOW, A_Q), lambda b, n: (b * ns + n, PB_QA // A_Q)),
                  prev(kcol), cur(kcol), prev(vcol), cur(vcol)],
        out_specs=pl.BlockSpec((nq * WINDOW, A_Q), lambda b, n: (b * ns + n, 0)),
        out_shape=jax.ShapeDtypeStruct((bsz * seq, A_Q), BF16),
        scratch_shapes=[pltpu.VMEM((2 * A_HEADS, WINDOW, 2 * WINDOW), F32)],
        compiler_params=_cparams("arbitrary", "arbitrary"),
        name="attn_prompt",
    )(rel_table, sinks, jnp.asarray(bkt), zb, z, z, z, z)


def _log_sigmoid(x):
    return -(jnp.maximum(-x, 0.0) + jnp.log1p(jnp.exp(-jnp.abs(x))))


def _mlstm_prompt_body(qk_ref, v_ref, g_ref, og_ref, cw_ref, bg_ref, gn_ref,
                       ob_ref, c_out, n_out, m_out,
                       c_sc, n_sc, m_sc, hist_sc, *, L, nc, nbat, nsub):
    ci = pl.program_id(1)

    @pl.when(ci == 0)
    def _():
        c_sc[...] = jnp.zeros(c_sc.shape, F32)
        n_sc[...] = jnp.zeros(n_sc.shape, F32)
        m_sc[...] = jnp.full(m_sc.shape, NEG_INF, F32)
        hist_sc[...] = jnp.zeros(hist_sc.shape, F32)

    row = lax.broadcasted_iota(I32, (L, L), 0)
    colm = lax.broadcasted_iota(I32, (L, L), 1)
    causal = colm <= row
    tril = causal.astype(F32)
    for sub in range(nsub):
        rows = pl.ds(sub * L, L)
        stores = []
        for bb in range(nbat):
            stores += _mlstm_chunk(qk_ref.at[bb, rows], v_ref.at[bb, rows], g_ref.at[bb, rows], og_ref.at[bb, rows],
                                   cw_ref, bg_ref, gn_ref, ob_ref.at[bb, rows], c_sc.at[bb], n_sc.at[bb],
                                   m_sc.at[bb], hist_sc.at[bb], causal, tril, L)
        for store in stores:
            store()

    @pl.when(ci == nc - 1)
    def _():
        c_out[...] = c_sc[...]
        n_out[...] = n_sc[...]
        m_out[...] = m_sc[...]


def _mlstm_chunk(qk_ref, v_ref, g_ref, og_ref, cw_ref, bg_ref, gn_ref, ob_ref, c_sc, n_sc, m_sc, hist_sc,
                 causal, tril, L):
    cur = qk_ref[...]
    ext = jnp.concatenate([hist_sc[...], cur], axis=0)
    cw = cw_ref[...]
    off = SUBLANES - (B_CONV - 1)
    conv = ext[off:off + L] * cw[0:1]
    for j in range(1, B_CONV):
        conv = conv + ext[off + j:off + j + L] * cw[j:j + 1]
    qk = conv * jax.nn.sigmoid(conv)
    q_all = qk[:, :B_QK]
    k_all = qk[:, B_QK:] * (B_DK ** -0.5)
    k_t = k_all.T
    v_all = v_ref[...]
    og = og_ref[...]
    gn = gn_ref[...]

    G = g_ref[...] + bg_ref[...]
    lf = _log_sigmoid(G)
    Bc = jnp.dot(tril, lf, precision=HI, preferred_element_type=F32)
    BT = Bc.T
    GT = G.T

    H = range(B_HEADS)
    nt = (((1,), (1,)), ((), ()))
    qh = [q_all[:, h * B_DK:(h + 1) * B_DK] for h in H]
    kh = [k_all[:, h * B_DK:(h + 1) * B_DK] for h in H]
    qb = [q.astype(BF16) for q in qh]
    vb = [v_all[:, h * B_DV:(h + 1) * B_DV].astype(BF16) for h in H]
    b_col = [Bc[:, B_HEADS + h:B_HEADS + h + 1] for h in H]
    c0 = [c_sc[h] for h in H]
    n0 = [n_sc[h:h + 1, :] for h in H]
    a = [b_col[h] + m_sc[h:h + 1, 0:1] for h in H]
    d = [jnp.where(causal, b_col[h] - BT[B_HEADS + h:B_HEADS + h + 1, :] + GT[h:h + 1, :], NEG_INF) for h in H]
    qk_raw = [lax.dot_general(qb[h], kh[h].astype(BF16), nt, preferred_element_type=F32) for h in H]
    qc = [jnp.dot(qb[h], c0[h].astype(BF16), preferred_element_type=F32) for h in H]
    m = [jnp.maximum(a[h], jnp.max(d[h], axis=-1, keepdims=True)) for h in H]
    dw = [jnp.exp(d[h] - m[h]) for h in H]
    aw = [jnp.exp(a[h] - m[h]) for h in H]
    s = [qk_raw[h] * dw[h] for h in H]
    sv = [jnp.dot(s[h].astype(BF16), vb[h], preferred_element_type=F32) for h in H]
    li = lax.broadcasted_iota(I32, (B_QK, LANES), 0) >> (B_DK.bit_length() - 1)
    ind = (li == lax.broadcasted_iota(I32, (B_QK, LANES), 1)).astype(F32)
    qn = jnp.dot(q_all * jnp.concatenate(n0, axis=1), ind, precision=HI, preferred_element_type=F32)
    den = [jnp.sum(s[h], axis=-1, keepdims=True) + aw[h] * qn[:, h:h + 1] for h in H]
    hh = [(sv[h] + aw[h] * qc[h]) / jnp.maximum(jnp.abs(den[h]), jnp.exp(-m[h])) for h in H]
    m_last = [m[h][L - 1:L, :] for h in H]
    wl = [jnp.exp(b_col[h][L - 1:L, :] - b_col[h] + G[:, h:h + 1] - m_last[h]) for h in H]
    decay = [aw[h][L - 1:L, :] for h in H]
    kw_t = [(k_t[h * B_DK:(h + 1) * B_DK, :] * dw[h][L - 1:L, :]).astype(BF16) for h in H]
    c1 = [decay[h] * c0[h] + jnp.dot(kw_t[h], vb[h], preferred_element_type=F32) for h in H]
    n1 = [decay[h] * n0[h] + jnp.sum(kh[h] * wl[h], axis=0, keepdims=True) for h in H]
    stores = [functools.partial(_store_state, c_sc, n_sc, m_sc, h, c1[h], n1[h], m_last[h]) for h in H]
    hn = [hh[h] * lax.rsqrt(jnp.mean(hh[h] * hh[h], axis=-1, keepdims=True) + EPS) * gn[:, h * B_DV:(h + 1) * B_DV]
          for h in H]
    outs = [jax.nn.sigmoid(og[:, h * B_DV:(h + 1) * B_DV]) * hn[h] for h in H]
    out = jnp.concatenate(outs, axis=1).astype(ob_ref.dtype)
    stores.append(functools.partial(_store_chunk, ob_ref, hist_sc, out, cur[L - SUBLANES:L]))
    return stores


def _store_state(c_sc, n_sc, m_sc, h, c1, n1, m_last):
    c_sc[h] = c1
    n_sc[h:h + 1, :] = n1
    m_sc[h:h + 1, :] = jnp.broadcast_to(m_last, (1, LANES))


def _store_chunk(ob_ref, hist_sc, out, tail):
    ob_ref[...] = out
    hist_sc[...] = tail


MLSTM_BATCH = 1
MLSTM_SUBCHUNKS = 1


def mlstm_prompt(z, zb, conv_w, b_gates_pad, g_norm, bsz, seq):
    L = MLSTM_CHUNK
    nsub = MLSTM_SUBCHUNKS if (seq // L) % MLSTM_SUBCHUNKS == 0 else 1
    nc = seq // (L * nsub)
    nbat = MLSTM_BATCH if bsz % MLSTM_BATCH == 0 else 1
    z3 = z.reshape(bsz, seq, P_COLS)
    zb3 = zb.reshape(bsz, seq, PB_COLS)

    def zspec(width, colblk):
        return pl.BlockSpec((nbat, nsub * L, width), lambda b, c: (b, c, colblk))

    const = lambda shape: pl.BlockSpec(shape, lambda b, c: (0,) * len(shape))
    state = lambda shape: pl.BlockSpec((nbat,) + shape, lambda b, c: (b,) + (0,) * len(shape))
    out_b, c1, n1, m1 = pl.pallas_call(
        functools.partial(_mlstm_prompt_body, L=L, nc=nc, nbat=nbat, nsub=nsub),
        grid=(bsz // nbat, nc),
        in_specs=[zspec(2 * B_QK, P_QK // (2 * B_QK)), zspec(B_V, PB_VB // B_V), zspec(LANES, P_GATES // LANES),
                  zspec(B_V, P_OG // B_V), const((B_CONV, 2 * B_QK)), const((1, LANES)), const((1, B_V))],
        out_specs=[pl.BlockSpec((nbat, nsub * L, B_V), lambda b, c: (b, c, 0)),
                   state((B_HEADS, B_DK, B_DV)), state((SUBLANES, B_DK)), state((SUBLANES, LANES))],
        out_shape=[jax.ShapeDtypeStruct((bsz, seq, B_V), BF16),
                   jax.ShapeDtypeStruct((bsz, B_HEADS, B_DK, B_DV), F32),
                   jax.ShapeDtypeStruct((bsz, SUBLANES, B_DK), F32),
                   jax.ShapeDtypeStruct((bsz, SUBLANES, LANES), F32)],
        scratch_shapes=[pltpu.VMEM((nbat, B_HEADS, B_DK, B_DV), F32), pltpu.VMEM((nbat, SUBLANES, B_DK), F32),
                        pltpu.VMEM((nbat, SUBLANES, LANES), F32), pltpu.VMEM((nbat, SUBLANES, 2 * B_QK), F32)],
        compiler_params=_cparams("arbitrary", "arbitrary"),
        name="mlstm_prompt",
    )(z3, zb3, z3, z3, conv_w, b_gates_pad, g_norm)
    return out_b.reshape(bsz * seq, B_V), c1, n1, m1


SAMPLE_TILE = 8


def _mix_sample_body(tab_ref, sink_ref, bkt_ref, z_ref, ck_ref, cv_ref, c0_ref, n0_ref, m0_ref, cb_ref,
                     cw_ref, bg_ref, gn_ref,
                     att_ref, ob_ref, nk_ref, nv_ref, c1_ref, n1_ref, m1_ref, ncb_ref, *, tb):
    bk = bkt_ref[...]
    cw = cw_ref[...]
    gn = gn_ref[...]
    bias_rows = []
    for h in range(A_HEADS):
        bias = jnp.zeros(bk.shape, F32)
        for t in range(REL_BUCKETS):
            bias = jnp.where(bk == t, tab_ref[t, h], bias)
        bias_rows.append(bias)
    n_keys = bk.shape[1]
    rows, cols = A_HEADS * tb, tb * n_keys
    z_all = z_ref[...]
    q_all = z_all[:, Z_QA:Z_QA + A_Q] * (A_HEAD_DIM ** -0.5)
    ka_all = z_all[:, Z_KA:Z_KA + A_KV]
    va_all = z_all[:, Z_VA:Z_VA + A_KV]
    zero_half = jnp.zeros((tb, A_HEAD_DIM), F32)
    q_rows = []
    for h in range(A_HEADS):
        qh = q_all[:, h * A_HEAD_DIM:(h + 1) * A_HEAD_DIM]
        q_rows.append(jnp.concatenate([qh, zero_half] if h // A_GROUP == 0 else [zero_half, qh], axis=1))
    qm = jnp.concatenate(q_rows, axis=0)
    ka_rep = jnp.concatenate([ka_all] * A_HEADS, axis=0)
    va_rep = jnp.concatenate([va_all] * A_HEADS, axis=0)
    k_cat = jnp.concatenate([ck_ref[i] for i in range(tb)], axis=0)
    v_cat = jnp.concatenate([cv_ref[i] for i in range(tb)], axis=0)
    bias_c = jnp.concatenate([jnp.broadcast_to(b, (tb, n_keys)) for b in bias_rows], axis=0)
    bias_c = jnp.concatenate([bias_c] * tb, axis=1)
    r_id = lax.broadcasted_iota(I32, (rows, cols), 0)
    c_id = lax.broadcasted_iota(I32, (rows, cols), 1)
    own = (r_id & (tb - 1)) == (c_id >> (n_keys.bit_length() - 1))
    head_id = lax.broadcasted_iota(I32, (rows, 1), 0) >> (tb.bit_length() - 1)
    bias_n = jnp.zeros((rows, 1), F32)
    sinks = jnp.zeros((rows, 1), F32)
    for h in range(A_HEADS):
        bias_n = jnp.where(head_id == h, tab_ref[0, h], bias_n)
        sinks = jnp.where(head_id == h, sink_ref[h], sinks)
    lc = lax.dot_general(qm, k_cat, (((1,), (1,)), ((), ())), precision=HI, preferred_element_type=F32)
    lc = jnp.where(own, lc + bias_c, NEG_INF)
    ln = jnp.sum(qm * ka_rep, axis=-1, keepdims=True) + bias_n
    mx = jnp.maximum(jnp.maximum(jnp.max(lc, axis=-1, keepdims=True), ln), sinks)
    pc = jnp.exp(lc - mx)
    pn = jnp.exp(ln - mx)
    den = jnp.sum(pc, axis=-1, keepdims=True) + pn + jnp.exp(sinks - mx)
    o = (jnp.dot(pc, v_cat, precision=HI, preferred_element_type=F32) + pn * va_rep) / den
    att_ref[...] = jnp.concatenate(
        [o[h * tb:(h + 1) * tb, (h // A_GROUP) * A_HEAD_DIM:(h // A_GROUP + 1) * A_HEAD_DIM]
         for h in range(A_HEADS)], axis=1)

    for i in range(tb):
        nk_ref[i] = jnp.concatenate([ck_ref[i][1:], ka_all[i:i + 1]], axis=0)
        nv_ref[i] = jnp.concatenate([cv_ref[i][1:], va_all[i:i + 1]], axis=0)

    qk_pre = z_all[:, Z_QK:Z_QK + 2 * B_QK]
    hist = [cb_ref[i] for i in range(tb)]
    conv = qk_pre * cw[B_CONV - 1:B_CONV]
    for j in range(B_CONV - 1):
        conv = conv + jnp.concatenate([hs[j:j + 1] for hs in hist], axis=0) * cw[j:j + 1]
    for i in range(tb):
        ncb_ref[i] = jnp.concatenate([hist[i][1:], qk_pre[i:i + 1]], axis=0)
    qk = conv * jax.nn.sigmoid(conv)
    q_m = qk[:, :B_QK]
    k_m = qk[:, B_QK:] * (B_DK ** -0.5)
    v_m = z_all[:, Z_VB:Z_VB + B_V]
    og = z_all[:, Z_OG:Z_OG + B_V]
    G = z_all[:, Z_GATES:Z_GATES + LANES] + bg_ref[...]
    ig = G[:, :B_HEADS]
    a = _log_sigmoid(G)[:, B_HEADS:2 * B_HEADS] + m0_ref[...]
    m = jnp.maximum(a, ig)
    dw = jnp.exp(ig - m)
    aw = jnp.exp(a - m)
    li = lax.broadcasted_iota(I32, (B_QK, LANES), 0) >> (B_DK.bit_length() - 1)
    ind = (li == lax.broadcasted_iota(I32, (B_QK, LANES), 1)).astype(F32)
    n0_m = jnp.concatenate([jnp.concatenate([n0_ref[i, h:h + 1, :] for h in range(B_HEADS)], axis=1)
                            for i in range(tb)], axis=0)
    qk_dot = jnp.dot(q_m * k_m, ind, precision=HI, preferred_element_type=F32)[:, :B_HEADS]
    qn_dot = jnp.dot(q_m * n0_m, ind, precision=HI, preferred_element_type=F32)[:, :B_HEADS]
    s = qk_dot * dw
    inv = 1.0 / jnp.maximum(jnp.abs(s + aw * qn_dot), jnp.exp(-m))
    m1_ref[...] = jnp.concatenate([m, jnp.zeros((tb, LANES - B_HEADS), F32)], axis=1)
    stack = jnp.concatenate([q_m[:, h * B_DK:(h + 1) * B_DK] for h in range(B_HEADS)]
                            + [k_m[:, h * B_DK:(h + 1) * B_DK] for h in range(B_HEADS)], axis=0)
    cols = jnp.concatenate([stack, jnp.zeros_like(stack)], axis=1).T[:B_DK]
    hh_rows = [[] for _ in range(B_HEADS)]
    for i in range(tb):
        for h in range(B_HEADS):
            jq = h * tb + i
            jk = (B_HEADS + h) * tb + i
            c0 = c0_ref[i, h]
            vh = v_m[i:i + 1, h * B_DV:(h + 1) * B_DV]
            aw_p, dw_p = aw[i:i + 1, h:h + 1], dw[i:i + 1, h:h + 1]
            num = s[i:i + 1, h:h + 1] * vh + aw_p * jnp.sum(cols[:, jq:jq + 1] * c0, axis=0, keepdims=True)
            hh_rows[h].append(num * inv[i:i + 1, h:h + 1])
            c1_ref[i, h] = aw_p * c0 + dw_p * (cols[:, jk:jk + 1] * vh)
            n1_ref[i, h:h + 1, :] = aw_p * n0_ref[i, h:h + 1, :] + dw_p * k_m[i:i + 1, h * B_DK:(h + 1) * B_DK]
    obs = []
    for h in range(B_HEADS):
        hh = jnp.concatenate(hh_rows[h], axis=0)
        hn = hh * lax.rsqrt(jnp.mean(hh * hh, axis=-1, keepdims=True) + EPS) * gn[:, h * B_DV:(h + 1) * B_DV]
        obs.append(jax.nn.sigmoid(og[:, h * B_DV:(h + 1) * B_DV]) * hn)
    ob_ref[...] = jnp.concatenate(obs, axis=1)


def mix_sample(z, rel_table, sinks, ck, cv, c0, n0, m0, conv_buf, conv_w, b_gates_pad, g_norm):
    nb = z.shape[0]
    tb = min(SAMPLE_TILE, nb)
    n_buf = ck.shape[1]
    bkt = _t5_buckets(n_buf - np.arange(n_buf))[None, :]
    smem = pl.BlockSpec(memory_space=pltpu.SMEM)
    const = lambda shape: pl.BlockSpec(shape, lambda i: (0,) * len(shape))
    lead = lambda shape: pl.BlockSpec((tb,) + shape, lambda i: (i,) + (0,) * len(shape))
    return pl.pallas_call(
        functools.partial(_mix_sample_body, tb=tb),
        grid=(nb // tb,),
        in_specs=[smem, smem, const((1, n_buf)), lead((Z_COLS,)), lead((n_buf, A_KV)), lead((n_buf, A_KV)),
                  lead((B_HEADS, B_DK, B_DV)), lead((B_HEADS, B_DK)), lead((B_HEADS,)),
                  lead((B_CONV - 1, 2 * B_QK)), const((B_CONV, 2 * B_QK)), const((1, LANES)), const((1, B_V))],
        out_specs=[lead((A_Q,)), lead((B_V,)), lead((n_buf, A_KV)), lead((n_buf, A_KV)),
                   lead((B_HEADS, B_DK, B_DV)), lead((B_HEADS, B_DK)), lead((LANES,)),
                   lead((B_CONV - 1, 2 * B_QK))],
        out_shape=[jax.ShapeDtypeStruct((nb, A_Q), F32), jax.ShapeDtypeStruct((nb, B_V), F32),
                   jax.ShapeDtypeStruct(ck.shape, F32), jax.ShapeDtypeStruct(cv.shape, F32),
                   jax.ShapeDtypeStruct(c0.shape, F32), jax.ShapeDtypeStruct(n0.shape, F32),
                   jax.ShapeDtypeStruct((nb, LANES), F32), jax.ShapeDtypeStruct(conv_buf.shape, F32)],
        compiler_params=_cparams("parallel"),
        name="mix_sample",
    )(rel_table, sinks, jnp.asarray(bkt), z, ck, cv, c0, n0, m0, conv_buf, conv_w, b_gates_pad, g_norm)


def _store_row_tiles(ref, val, rows):
    for s in range(ROW_TILES):
        ref[pl.ds(s, rows, stride=ROW_TILES), :] = val[:, s * LANES:(s + 1) * LANES]


def _load_row_tiles(ref, rows, start=0, stride=ROW_TILES):
    return jnp.concatenate([ref[pl.ds(start + s, rows, stride=stride), :] for s in range(ROW_TILES)], axis=1)


def _route(logits):
    lane = lax.broadcasted_iota(I32, logits.shape, 1)
    big = jnp.int32(1 << 20)
    gl = jnp.where(lane < N_GROUPS, logits, NEG_INF)
    gmax = jnp.max(gl, axis=-1, keepdims=True)
    gidx = jnp.min(jnp.where(gl == gmax, lane, big), axis=-1, keepdims=True)
    g_gate = 1.0 / jnp.sum(jnp.exp(gl - gmax), axis=-1, keepdims=True)
    lo = N_GROUPS + gidx * EXPERTS_PER_GROUP
    el = jnp.where((lane >= lo) & (lane < lo + EXPERTS_PER_GROUP), logits, NEG_INF)
    v1 = jnp.max(el, axis=-1, keepdims=True)
    i1 = jnp.min(jnp.where(el == v1, lane, big), axis=-1, keepdims=True)
    el2 = jnp.where(lane == i1, NEG_INF, el)
    v2 = jnp.max(el2, axis=-1, keepdims=True)
    i2 = jnp.min(jnp.where(el2 == v2, lane, big), axis=-1, keepdims=True)
    t = jnp.exp(v2 - v1)
    w1 = g_gate / (1.0 + t)
    w2 = g_gate * t / (1.0 + t)
    gate = jnp.where(lane == 0, w1, jnp.where(lane == 1, w2, 0.0))
    return i1 - N_GROUPS, i2 - N_GROUPS, gate


def _local_sort(e0, e1):
    tm = e0.shape[0]
    lane = lax.broadcasted_iota(I32, (tm, LANES), 1)
    oh0 = (lane == e0).astype(BF16)
    oh1 = (lane == e1).astype(BF16)
    r = lax.broadcasted_iota(I32, (tm, tm), 0)
    c = lax.broadcasted_iota(I32, (tm, tm), 1)
    before = (c < r).astype(BF16)
    cnt0 = jnp.sum(oh0.astype(F32), axis=0, keepdims=True)
    run_n = cnt0 + jnp.sum(oh1.astype(F32), axis=0, keepdims=True)
    er = lax.broadcasted_iota(I32, (LANES, LANES), 0)
    ec = lax.broadcasted_iota(I32, (LANES, LANES), 1)
    slot_n = jnp.floor((run_n + (RUN_SLOT - 1)) * (1.0 / RUN_SLOT)) * RUN_SLOT
    run_l = jnp.dot(slot_n, (er < ec).astype(F32), precision=HI, preferred_element_type=F32)
    w0 = jnp.dot(before, oh0, preferred_element_type=F32) + run_l
    w1 = jnp.dot(before, oh1, preferred_element_type=F32) + run_l + cnt0
    p0 = jnp.sum(jnp.where(lane == e0, w0, 0.0), axis=-1, keepdims=True)
    p1 = jnp.sum(jnp.where(lane == e1, w1, 0.0), axis=-1, keepdims=True)
    return jnp.where(lane == 0, p0, jnp.where(lane == 1, p1, 0.0)), run_n


MOE_CHUNK = 512
RUN_SLOT = 32


def _moe_chunk(rows, precise):
    return min(rows, 128 if precise else MOE_CHUNK)


def _proj_router_body(*refs, n_in, has_bias, precise, tm):
    a_refs = refs[:n_in]
    w_refs = refs[n_in:2 * n_in]
    k = 2 * n_in
    bias_ref = refs[k] if has_bias else None
    k += 1 if has_bias else 0
    x_ref, g_ref, wr_ref, br_ref, x1_ref, h8_ref, lpos_ref, gate_ref, runn_ref = refs[k:]
    acc = x_ref[...]
    if has_bias:
        acc = acc + bias_ref[...]
    for a_ref, w_ref in zip(a_refs, w_refs):
        acc = acc + _mm(a_ref[...], w_ref[...], precise)
    x1_ref[...] = acc
    h = _rms(acc, g_ref[...])
    _store_row_tiles(h8_ref, h, tm)
    wr = wr_ref[...]
    if precise:
        logits = jnp.dot(h, wr, precision=HI, preferred_element_type=F32)
    else:
        h_hi = h.astype(BF16)
        h_lo = (h - h_hi.astype(F32)).astype(BF16)
        w_hi = wr.astype(BF16)
        w_lo = (wr - w_hi.astype(F32)).astype(BF16)
        logits = (jnp.dot(h_hi, w_hi, preferred_element_type=F32) + jnp.dot(h_lo, w_hi, preferred_element_type=F32)
                  + jnp.dot(h_hi, w_lo, preferred_element_type=F32))
    e0, e1, gate = _route(logits + br_ref[...])
    lpos, run_n = _local_sort(e0, e1)
    lpos_ref[...] = (lpos.T[:SUBLANES] * ROW_TILES).astype(I32)
    gate_ref[...] = gate.T[:SUBLANES]
    runn_ref[...] = jnp.broadcast_to(run_n, runn_ref.shape).astype(I32)


def proj_router(a_list, w_list, bias, x, g, wr, br, *, precise):
    rows, d = x.shape
    tm = _moe_chunk(rows, precise)
    n_in = len(a_list)
    row_spec = lambda width: pl.BlockSpec((tm, width), lambda i: (i, 0))
    const = lambda shape: pl.BlockSpec(shape, lambda i: (0,) * len(shape))
    in_specs = [row_spec(a.shape[1]) for a in a_list] + [const(w.shape) for w in w_list]
    args = list(a_list) + list(w_list)
    if bias is not None:
        in_specs.append(const((1, d)))
        args.append(bias)
    in_specs += [row_spec(d), const((1, d)), const((d, LANES)), const((1, LANES))]
    args += [x, g, wr, br]
    return pl.pallas_call(
        functools.partial(_proj_router_body, n_in=n_in, has_bias=bias is not None, precise=precise, tm=tm),
        grid=(rows // tm,),
        in_specs=in_specs,
        out_specs=[row_spec(d), pl.BlockSpec((tm * ROW_TILES, LANES), lambda i: (i, 0)),
                   pl.BlockSpec((SUBLANES, tm), lambda i: (i, 0)), pl.BlockSpec((SUBLANES, tm), lambda i: (i, 0)),
                   pl.BlockSpec((SUBLANES, LANES), lambda i: (i, 0))],
        out_shape=[jax.ShapeDtypeStruct((rows, d), F32), jax.ShapeDtypeStruct((rows * ROW_TILES, LANES), F32),
                   jax.ShapeDtypeStruct((rows // tm * SUBLANES, tm), I32),
                   jax.ShapeDtypeStruct((rows // tm * SUBLANES, tm), F32),
                   jax.ShapeDtypeStruct((rows // tm * SUBLANES, LANES), I32)],
        compiler_params=_cparams("parallel"),
        name="proj_router",
    )(*args)


def _rows_at(offset):
    return pl.ds(pl.multiple_of(offset, ROW_TILES), ROW_TILES)


def _tile_rows(r, n=1):
    return pl.ds(pl.multiple_of(r * ROW_TILES, ROW_TILES), n * ROW_TILES)


def _pow2_pieces(limit):
    p = 1
    while p * 2 <= limit:
        p *= 2
    out = []
    while p >= 1:
        out.append(p)
        p //= 2
    return out


COMMON_PIECE = 32


def _for_each_piece(n, pieces, fn):
    def emit(ps):
        for p in ps:
            @pl.when((n & p) != 0)
            def _(p=p):
                fn(n & ~(2 * p - 1), p)

    big = [p for p in pieces if p > COMMON_PIECE]
    if big:
        @pl.when(n > 2 * COMMON_PIECE - 1)
        def _():
            emit(big)
    emit([p for p in pieces if p <= COMMON_PIECE])


def _piece_rows(first_row, i):
    return _tile_rows(first_row + i * RUN_SLOT, RUN_SLOT)


def _dispatch_body(rg_ref, sl_ref, np_ref, npc_ref, ps_ref, pn_ref, tail_ref, lpos_ref, h8_ref, xs_hbm,
                   stage, zbuf, sem, zsem, *, chunk, nch, blk):
    c = pl.program_id(0)
    slot = c % 2
    pad_pieces = _pow2_pieces(blk + RUN_SLOT - 1)

    def wait_pieces(count, buf):
        def one(i, carry):
            pltpu.make_async_copy(stage.at[buf, _tile_rows(0, RUN_SLOT)], stage.at[buf, _tile_rows(0, RUN_SLOT)],
                                  sem).wait()
            return carry
        lax.fori_loop(0, count, one, 0)

    @pl.when(c == 0)
    def _():
        stage[...] = jnp.zeros(stage.shape, F32)

    def copy_tok(t, carry):
        row = h8_ref[_tile_rows(t), :]
        stage[slot, _rows_at(lpos_ref[0, 0, t]), :] = row
        stage[slot, _rows_at(lpos_ref[0, 0, chunk + t]), :] = row
        return carry
    lax.fori_loop(0, chunk, copy_tok, 0, unroll=8)

    @pl.when(c > 0)
    def _():
        wait_pieces(npc_ref[jnp.maximum(c - 1, 0)], 1 - slot)

    def send_runs(e, carry):
        k = c * N_EXPERTS + e

        def one(i, carry2):
            pltpu.make_async_copy(stage.at[slot, _piece_rows(sl_ref[k], i)], xs_hbm.at[_piece_rows(rg_ref[k], i)],
                                  sem).start()
            return carry2
        lax.fori_loop(0, np_ref[k], one, 0)
        return carry
    lax.fori_loop(0, N_EXPERTS, send_runs, 0)

    @pl.when(c == nch - 1)
    def _():
        wait_pieces(npc_ref[c], slot)
        zbuf[...] = jnp.zeros(zbuf.shape, F32)

        def pad_dmas(e, op):
            def one(off, p):
                cp = pltpu.make_async_copy(zbuf.at[_tile_rows(0, p)], xs_hbm.at[_tile_rows(ps_ref[e] + off, p)], zsem)
                cp.start() if op == 0 else cp.wait()
            _for_each_piece(pn_ref[e], pad_pieces, one)

        def issue(e, carry):
            pad_dmas(e, 0)
            return carry

        def wait(e, carry):
            pad_dmas(e, 1)
            return carry
        lax.fori_loop(0, N_EXPERTS, issue, 0)
        lax.fori_loop(0, N_EXPERTS, wait, 0)

        half = blk // 2

        def tail_dmas(i, op):
            cp = pltpu.make_async_copy(zbuf.at[_tile_rows(0, half)],
                                       xs_hbm.at[_tile_rows(tail_ref[0] + i * half, half)], zsem)
            cp.start() if op == 0 else cp.wait()

        def tail_issue(i, carry):
            tail_dmas(i, 0)
            return carry

        def tail_wait(i, carry):
            tail_dmas(i, 1)
            return carry
        lax.fori_loop(0, tail_ref[1], tail_issue, 0)
        lax.fori_loop(0, tail_ref[1], tail_wait, 0)


def _stage_rows(chunk):
    return TOP_K * chunk + N_EXPERTS * (RUN_SLOT - 1) // RUN_SLOT * RUN_SLOT + RUN_SLOT


def dispatch(h8, plan):
    chunk, nch, blk = plan['chunk'], plan['nch'], plan['blk']
    n_slots = plan['nblk'] * blk
    gs = pltpu.PrefetchScalarGridSpec(
        num_scalar_prefetch=7,
        grid=(nch,),
        in_specs=[pl.BlockSpec((1, 1, TOP_K * chunk), lambda c, *_: (c, 0, 0), memory_space=pltpu.SMEM),
                  pl.BlockSpec((chunk * ROW_TILES, LANES), lambda c, *_: (c, 0))],
        out_specs=pl.BlockSpec(memory_space=pl.ANY),
        scratch_shapes=[pltpu.VMEM((2, _stage_rows(chunk) * ROW_TILES, LANES), F32),
                        pltpu.VMEM((blk * ROW_TILES, LANES), F32),
                        pltpu.SemaphoreType.DMA(()), pltpu.SemaphoreType.DMA(())],
    )
    return pl.pallas_call(
        functools.partial(_dispatch_body, chunk=chunk, nch=nch, blk=blk),
        grid_spec=gs,
        out_shape=jax.ShapeDtypeStruct((n_slots * ROW_TILES, LANES), F32),
        compiler_params=_cparams("arbitrary"),
        name="dispatch",
    )(plan['run_g'], plan['slot_l'], plan['n_piece'], plan['n_piece_chunk'], plan['pad_start'], plan['pad_n'],
      plan['tail'], plan['lpos'], h8)


FF_CHUNK = 512


def _experts_body(be_ref, nv_ref, pe_ref, xs_ref, wg_ref, wu_ref, wd_ref, ys_ref, xb, *wscr, precise, blk, layer):
    j = pl.program_id(0)
    nv = nv_ref[0]
    slot = j % 2
    k = j - 1

    @pl.when(j == 0)
    def _():
        xb[1] = jnp.zeros(xb.shape[1:], xb.dtype)

    if not precise:
        wgb, wub, wdb, wgf, wuf, wdf, wsem = wscr

        def weight_copies(e, ws):
            return [pltpu.make_async_copy(src.at[layer, e], dst.at[ws], wsem.at[ws])
                    for src, dst in ((wg_ref, wgf), (wu_ref, wuf), (wd_ref, wdf))]

        @pl.when(j == 0)
        def _():
            for cp in weight_copies(be_ref[0], 0):
                cp.start()

        kc = jnp.clip(k, 0, nv - 1)
        e = be_ref[kc]
        ws = pe_ref[N_EXPERTS + e]

        @pl.when((k >= 0) & (k < nv) & ((k == 0) | (e != be_ref[jnp.maximum(kc - 1, 0)])))
        def _():
            for cp in weight_copies(e, ws):
                cp.wait()
            wgb[...] = wgf[ws].astype(BF16)
            wub[...] = wuf[ws].astype(BF16)
            wdb[...] = wdf[ws].astype(BF16)
            nxt = pe_ref[e]

            @pl.when(nxt < nv)
            def _():
                for cp in weight_copies(be_ref[jnp.minimum(nxt, nv - 1)], 1 - ws):
                    cp.start()

    @pl.when(j <= nv)
    def _():
        xb[slot] = _load_row_tiles(xs_ref, blk).astype(xb.dtype)
        if precise:
            wg, wu, wd = wg_ref[0], wu_ref[0], wd_ref[0]
        else:
            wg, wu, wd = wgb, wub, wdb
        xm = xb[1 - slot]
        gt = _mm(xm, wg[...], precise)
        up = _mm(xm, wu[...], precise)
        _store_row_tiles(ys_ref, _mm(gt * jax.nn.sigmoid(gt) * up, wd[...], precise), blk)

    @pl.when(j > nv)
    def _():
        ys_ref[...] = jnp.zeros(ys_ref.shape, F32)


def experts(xs, plan, wg, wu, wd, layer, *, precise):
    nblk, rows = plan['nblk'], plan['blk'] * ROW_TILES
    d, ff = wg.shape[2], wg.shape[3]
    blk = lambda j, be, nv, pe: (jnp.minimum(j, nv[0] - 1), 0)
    if precise:
        wspec = lambda shape: pl.BlockSpec((None, 1) + shape,
                                           lambda j, be, nv, pe: (layer, be[jnp.clip(j - 1, 0, nv[0] - 1)], 0, 0))
        wspecs = [wspec((d, ff)), wspec((d, ff)), wspec((ff, d))]
        wscratch = []
    else:
        wspecs = [pl.BlockSpec(memory_space=pl.ANY)] * 3
        wscratch = [pltpu.VMEM((d, ff), BF16), pltpu.VMEM((d, ff), BF16), pltpu.VMEM((ff, d), BF16),
                    pltpu.VMEM((2, d, ff), F32), pltpu.VMEM((2, d, ff), F32), pltpu.VMEM((2, ff, d), F32),
                    pltpu.SemaphoreType.DMA((2,))]
    gs = pltpu.PrefetchScalarGridSpec(
        num_scalar_prefetch=3,
        grid=(nblk + 1,),
        in_specs=[pl.BlockSpec((rows, LANES), blk)] + wspecs,
        out_specs=pl.BlockSpec((rows, LANES), lambda j, be, nv, pe: (jnp.maximum(j - 1, 0), 0)),
        scratch_shapes=[pltpu.VMEM((2, plan['blk'], d), F32 if precise else BF16)] + wscratch,
    )
    return pl.pallas_call(
        functools.partial(_experts_body, precise=precise, blk=plan['blk'], layer=layer),
        grid_spec=gs,
        out_shape=jax.ShapeDtypeStruct(xs.shape, F32),
        compiler_params=_cparams("arbitrary"),
        name="experts",
    )(plan['block_e'], plan['n_used'], plan['expert_tab'], xs, wg, wu, wd)


def moe_plan(lpos8, gate8, runn8, blk):
    nch = runn8.shape[0] // SUBLANES
    chunk = lpos8.shape[1]
    n_assign = nch * chunk * TOP_K
    nblk = (n_assign + N_EXPERTS * (blk - 1 + RUN_SLOT) + blk - 1) // blk
    run_n = runn8.reshape(nch, SUBLANES, LANES)[:, 0, :N_EXPERTS]
    per_chunk = lambda a: a.reshape(nch, 1, SUBLANES * chunk)[:, :, :TOP_K * chunk]
    counts = jnp.sum(run_n, axis=0)
    padded = (counts + RUN_SLOT + blk - 1) // blk * blk
    pends = jnp.cumsum(padded)
    pstarts = pends - padded
    run_g = pstarts[None, :] + jnp.cumsum(run_n, axis=0) - run_n
    n_piece = (run_n + RUN_SLOT - 1) // RUN_SLOT
    slot_l = (jnp.cumsum(n_piece, axis=1) - n_piece) * RUN_SLOT
    blk_start = jnp.arange(nblk, dtype=I32) * blk
    block_e = jnp.minimum(jnp.sum((pends[None, :] <= blk_start[:, None]).astype(I32), axis=1), N_EXPERTS - 1)
    return dict(chunk=chunk, nch=nch, nblk=nblk, blk=blk,
                run_g=run_g.reshape(-1).astype(I32), slot_l=slot_l.reshape(-1).astype(I32),
                n_piece=n_piece.reshape(-1).astype(I32), n_piece_chunk=jnp.sum(n_piece, axis=1).astype(I32),
                pad_start=(pstarts + counts).astype(I32), pad_n=(padded - counts).astype(I32),
                lpos=per_chunk(lpos8), gate=per_chunk(gate8), block_e=block_e.astype(I32),
                expert_tab=jnp.concatenate([pends // blk, jnp.arange(N_EXPERTS, dtype=I32) % 2]).astype(I32),
                n_used=(pends[-1:] // blk).astype(I32),
                tail=jnp.stack([pends[-1], 2 * (nblk - pends[-1] // blk)]).astype(I32))


def _combine(rg_ref, sl_ref, np_ref, npc_ref, lpos_ref, gate_ref, x_ref, ys_hbm, ystage, comb, sem, *, chunk, nch):
    c = pl.program_id(0)
    slot = c % 2

    def fetch(cc, sl):
        def per_e(e, carry):
            k = cc * N_EXPERTS + e

            def one(i, carry2):
                pltpu.make_async_copy(ys_hbm.at[_piece_rows(rg_ref[k], i)], ystage.at[sl, _piece_rows(sl_ref[k], i)],
                                      sem.at[sl]).start()
                return carry2
            lax.fori_loop(0, np_ref[k], one, 0)
            return carry
        lax.fori_loop(0, N_EXPERTS, per_e, 0)

    @pl.when(c == 0)
    def _():
        fetch(0, 0)

    @pl.when(c + 1 < nch)
    def _():
        fetch(c + 1, 1 - slot)

    def wait_piece(i, carry):
        pltpu.make_async_copy(ystage.at[slot, _tile_rows(0, RUN_SLOT)], ystage.at[slot, _tile_rows(0, RUN_SLOT)],
                              sem.at[slot]).wait()
        return carry
    lax.fori_loop(0, npc_ref[c], wait_piece, 0)

    def per_tok(t, carry):
        y0 = ystage[slot, _rows_at(lpos_ref[0, 0, t]), :]
        y1 = ystage[slot, _rows_at(lpos_ref[0, 0, chunk + t]), :]
        comb[_tile_rows(t), :] = gate_ref[0, 0, t] * y0 + gate_ref[0, 0, chunk + t] * y1
        return carry
    lax.fori_loop(0, chunk, per_tok, 0, unroll=8)
    return x_ref[...] + _load_row_tiles(comb, chunk)


def _combine_glu_body(rg_ref, sl_ref, np_ref, npc_ref, lpos_ref, gate_ref, x_ref, ys_hbm, g_ref, w_ref, b_ref,
                      x2_ref, u_ref, ystage, comb, sem, *, chunk, nch, precise):
    x2 = _combine(rg_ref, sl_ref, np_ref, npc_ref, lpos_ref, gate_ref, x_ref, ys_hbm, ystage, comb, sem,
                  chunk=chunk, nch=nch)
    x2_ref[...] = x2
    zz = _mm(_rms(x2, g_ref[...]), w_ref[...], precise) + b_ref[...]
    half = zz.shape[1] // 2
    u_ref[...] = zz[:, :half] * jax.nn.sigmoid(zz[:, half:])


def _combine_final_body(rg_ref, sl_ref, np_ref, npc_ref, lpos_ref, gate_ref, x_ref, ys_hbm, g_ref, o_ref,
                        ystage, comb, sem, *, chunk, nch):
    x2 = _combine(rg_ref, sl_ref, np_ref, npc_ref, lpos_ref, gate_ref, x_ref, ys_hbm, ystage, comb, sem,
                  chunk=chunk, nch=nch)
    o_ref[...] = _rms(x2, g_ref[...])


def _combine_call(body, plan, x, ys, extra, extra_specs, out_specs, out_shape, name):
    chunk, nch = plan['chunk'], plan['nch']
    d = x.shape[1]
    smem_blk = pl.BlockSpec((1, 1, TOP_K * chunk), lambda c, *_: (c, 0, 0), memory_space=pltpu.SMEM)
    gs = pltpu.PrefetchScalarGridSpec(
        num_scalar_prefetch=4,
        grid=(nch,),
        in_specs=[smem_blk, smem_blk, pl.BlockSpec((chunk, d), lambda c, *_: (c, 0)),
                  pl.BlockSpec(memory_space=pl.ANY)] + extra_specs,
        out_specs=out_specs,
        scratch_shapes=[pltpu.VMEM((2, _stage_rows(chunk) * ROW_TILES, LANES), F32),
                        pltpu.VMEM((chunk * ROW_TILES, LANES), F32), pltpu.SemaphoreType.DMA((2,))],
    )
    return pl.pallas_call(
        functools.partial(body, chunk=chunk, nch=nch),
        grid_spec=gs,
        out_shape=out_shape,
        compiler_params=_cparams("arbitrary"),
        name=name,
    )(plan['run_g'], plan['slot_l'], plan['n_piece'], plan['n_piece_chunk'], plan['lpos'], plan['gate'], x, ys,
      *extra)


def combine_glu(x, ys, plan, g, w, b, *, precise):
    rows, d = x.shape
    chunk = plan['chunk']
    cols = w.shape[1]
    const = lambda shape: pl.BlockSpec(shape, lambda c, *_: (0,) * len(shape))
    row_spec = lambda width: pl.BlockSpec((chunk, width), lambda c, *_: (c, 0))
    return _combine_call(
        functools.partial(_combine_glu_body, precise=precise), plan, x, ys, [g, w, b],
        [const((1, d)), const((d, cols)), const((1, cols))], [row_spec(d), row_spec(cols // 2)],
        [jax.ShapeDtypeStruct((rows, d), F32), jax.ShapeDtypeStruct((rows, cols // 2), F32)], "combine_glu")


def combine_final(x, ys, plan, g):
    rows, d = x.shape
    chunk = plan['chunk']
    return _combine_call(
        _combine_final_body, plan, x, ys, [g], [pl.BlockSpec((1, d), lambda c, *_: (0, 0))],
        pl.BlockSpec((chunk, d), lambda c, *_: (c, 0)), jax.ShapeDtypeStruct((rows, d), F32), "combine_final")


CONV_TILE = 512
CONV_HIST = 32


def _ln_swish(y, g, b):
    yc = y - jnp.mean(y, axis=-1, keepdims=True)
    yn = yc * lax.rsqrt(jnp.mean(yc * yc, axis=-1, keepdims=True) + EPS) * g + b
    return yn * jax.nn.sigmoid(yn)


CONV_ROWS = 64
LN_ROWS = 16
LN_UNROLL = 8


def _dwconv_prompt_body(u_ref, w_ref, bdw_ref, g_ref, b_ref, o_ref, ext, y_sc, *, tt):
    t = pl.program_id(1)
    n_lt = ext.shape[0]

    @pl.when(t == 0)
    def _():
        ext[:, 0:CONV_HIST, :] = jnp.zeros((n_lt, CONV_HIST, LANES), F32)

    @pl.when(t > 0)
    def _():
        ext[:, 0:CONV_HIST, :] = ext[:, tt:tt + CONV_HIST, :]

    for j in range(n_lt):
        ext[j, CONV_HIST:CONV_HIST + tt, :] = u_ref[:, j * LANES:(j + 1) * LANES]
    off = CONV_HIST - (C_KERNEL - 1)
    for j in range(n_lt):
        wj = w_ref[:, j * LANES:(j + 1) * LANES]
        bj = bdw_ref[:, j * LANES:(j + 1) * LANES]
        for c in range(tt // CONV_ROWS):
            acc = ext[j, pl.ds(off + c * CONV_ROWS, CONV_ROWS), :] * wj[0:1] + bj
            for k in range(1, C_KERNEL):
                acc = acc + ext[j, pl.ds(off + k + c * CONV_ROWS, CONV_ROWS), :] * wj[k:k + 1]
            y_sc[c * CONV_ROWS:(c + 1) * CONV_ROWS, j * LANES:(j + 1) * LANES] = acc

    def ln_rows(r, carry):
        rows = pl.ds(pl.multiple_of(r * LN_ROWS, LN_ROWS), LN_ROWS)
        o_ref[rows, :] = _ln_swish(y_sc[rows, :], g_ref[...], b_ref[...]).astype(o_ref.dtype)
        return carry
    lax.fori_loop(0, tt // LN_ROWS, ln_rows, 0, unroll=LN_UNROLL)


def dwconv_prompt(u, w, b_dw, ln_g, ln_b, bsz, seq):
    tt = min(CONV_TILE, seq)
    nt = seq // tt
    d = u.shape[1]
    const = lambda shape: pl.BlockSpec(shape, lambda b, t: (0,) * len(shape))
    return pl.pallas_call(
        functools.partial(_dwconv_prompt_body, tt=tt),
        grid=(bsz, nt),
        in_specs=[pl.BlockSpec((tt, d), lambda b, t: (b * nt + t, 0)), const((C_KERNEL, d)), const((1, d)),
                  const((1, d)), const((1, d))],
        out_specs=pl.BlockSpec((tt, d), lambda b, t: (b * nt + t, 0)),
        out_shape=jax.ShapeDtypeStruct((bsz * seq, d), BF16),
        scratch_shapes=[pltpu.VMEM((d // LANES, CONV_HIST + tt, LANES), F32), pltpu.VMEM((tt, d), F32)],
        compiler_params=_cparams("arbitrary", "arbitrary"),
        name="dwconv_prompt",
    )(u, w, b_dw, ln_g, ln_b)


def _dwconv_sample_body(u_ref, buf_ref, w_ref, bdw_ref, g_ref, b_ref, o_ref, nbuf_ref, *, tb):
    w = w_ref[...]
    rows = []
    for i in range(tb):
        hist = buf_ref[i]
        ur = u_ref[i:i + 1, :]
        rows.append(jnp.sum(hist * w[:C_KERNEL - 1], axis=0, keepdims=True) + ur * w[C_KERNEL - 1:C_KERNEL])
        nbuf_ref[i] = jnp.concatenate([hist[1:], ur], axis=0)
    y = jnp.concatenate(rows, axis=0) + bdw_ref[...]
    o_ref[...] = _ln_swish(y, g_ref[...], b_ref[...])


def dwconv_sample(u, buf, w, b_dw, ln_g, ln_b):
    nb, d = u.shape
    tb = min(SAMPLE_TILE, nb)
    const = lambda shape: pl.BlockSpec(shape, lambda i: (0,) * len(shape))
    return pl.pallas_call(
        functools.partial(_dwconv_sample_body, tb=tb),
        grid=(nb // tb,),
        in_specs=[pl.BlockSpec((tb, d), lambda i: (i, 0)), pl.BlockSpec((tb, C_KERNEL - 1, d), lambda i: (i, 0, 0)),
                  const((C_KERNEL, d)), const((1, d)), const((1, d)), const((1, d))],
        out_specs=[pl.BlockSpec((tb, d), lambda i: (i, 0)), pl.BlockSpec((tb, C_KERNEL - 1, d), lambda i: (i, 0, 0))],
        out_shape=[jax.ShapeDtypeStruct((nb, d), F32), jax.ShapeDtypeStruct(buf.shape, F32)],
        compiler_params=_cparams("parallel"),
        name="dwconv_sample",
    )(u, buf, w, b_dw, ln_g, ln_b)


def _moe(h8, lpos8, gate8, runn8, wg, wu, wd, layer, *, blk, precise):
    plan = moe_plan(lpos8, gate8, runn8, blk)
    xs = dispatch(h8, plan)
    return experts(xs, plan, wg, wu, wd, layer, precise=precise), plan


def _trunk(x, caches, p, *, prompt):
    bsz, seq, d = x.shape
    rows = bsz * seq
    precise = not prompt
    wdt = F32 if precise else BF16
    xf = x.reshape(rows, d)
    row = lambda v: v.reshape(1, -1).astype(F32)

    z = norm_proj(xf, row(p['norm_mix'][0]), p['w_in'].astype(wdt), precise=precise)
    if prompt:
        z, zb = z
        att = attn_prompt(z, zb, p['rel_table'], p['sinks'], bsz, seq)
        out_b, c1, n1, m1 = mlstm_prompt(z, zb, p['conv_w'], p['b_gates'], p['g_mnorm'], bsz, seq)
        z3 = z.reshape(bsz, seq, P_COLS)
        new_k = z3[:, seq - WINDOW:, P_KA:P_KA + A_KV].reshape(bsz, WINDOW, A_KV_HEADS, A_HEAD_DIM)
        new_v = z3[:, seq - WINDOW:, P_VA:P_VA + A_KV].reshape(bsz, WINDOW, A_KV_HEADS, A_HEAD_DIM)
        new_conv = z3[:, seq - (B_CONV - 1):, P_QK:P_QK + 2 * B_QK]
        n1 = n1[:, :B_HEADS]
        m1 = m1[:, :B_HEADS, 0]
    else:
        ck, cv, c0, n0, m0, cbuf = caches[:6]
        n_buf = ck.shape[1]
        att, out_b, new_k, new_v, c1, n1, m1, new_conv = mix_sample(
            z, p['rel_table'], p['sinks'], ck.reshape(bsz, n_buf, A_KV), cv.reshape(bsz, n_buf, A_KV),
            c0, n0, m0, cbuf, p['conv_w'], p['b_gates'], p['g_mnorm'])
        new_k = new_k.reshape(bsz, n_buf, A_KV_HEADS, A_HEAD_DIM)
        new_v = new_v.reshape(bsz, n_buf, A_KV_HEADS, A_HEAD_DIM)
        m1 = m1[:, :B_HEADS]
    w_out = p['w_out'].astype(wdt)
    x1, h8, lpos, gate, runn = proj_router([att, out_b], [w_out[:A_Q], w_out[A_Q:]], None, xf, row(p['norm_ffn'][0]),
                                           p['w_router'][0], p['b_router'][0], precise=precise)
    blk = EXPERT_BLOCK_PRECISE if precise else EXPERT_BLOCK
    ys, plan = _moe(h8, lpos, gate, runn, p['w_eg'], p['w_eu'], p['w_ed'], 0, blk=blk, precise=precise)

    x2, u = combine_glu(x1, ys, plan, row(p['norm_mix'][1]), p['w_pw1'].astype(wdt), row(p['b_pw1']),
                        precise=precise)
    if prompt:
        yc = dwconv_prompt(u, p['w_dw'], row(p['b_dw']), row(p['ln_g']), row(p['ln_b']), bsz, seq)
        new_cbuf = u.reshape(bsz, seq, d)[:, seq - (C_KERNEL - 1):]
    else:
        yc, new_cbuf = dwconv_sample(u, caches[6], p['w_dw'], row(p['b_dw']), row(p['ln_g']), row(p['ln_b']))
    x3, h8, lpos, gate, runn = proj_router([yc], [p['w_pw2'].astype(wdt)], row(p['b_pw2']), x2, row(p['norm_ffn'][1]),
                                           p['w_router'][1], p['b_router'][1], precise=precise)
    ys, plan = _moe(h8, lpos, gate, runn, p['w_eg'], p['w_eu'], p['w_ed'], 1, blk=blk, precise=precise)
    y = combine_final(x3, ys, plan, row(p['norm_final']))
    add_layer = lambda t: t[None]
    return (y.reshape(bsz, seq, d),) + tuple(add_layer(t) for t in (new_k, new_v, c1, n1, m1, new_conv, new_cbuf))


def kernel(x_prompt, x_sample, cache_win_k, cache_win_v, state_mlstm_c, state_mlstm_n, state_mlstm_m, state_mlstm_conv, state_conv, norm_mix, norm_ffn, norm_final, rel_bias_table, w_in_mix, b_mlstm_gates, w_mlstm_qk_conv, attn_sinks, g_mlstm_norm, w_out_mix, w_pw1, b_pw1, w_dw, b_dw, ln_conv_g, ln_conv_b, w_pw2, b_pw2, w_router_group, b_router_group, w_router_expert, b_router_expert, w_expert_gate, w_expert_up, w_expert_down):
    w_in = w_in_mix[0]
    s_q, s_k, s_v, s_qk, s_vb, s_g = A_Q, A_Q + A_KV, A_Q + 2 * A_KV, A_Q + 2 * A_KV + 2 * B_QK, \
        A_Q + 2 * A_KV + 2 * B_QK + B_V, A_Q + 2 * A_KV + 2 * B_QK + B_V + 2 * B_HEADS
    w_in_r = jnp.concatenate([w_in[:, :s_q], w_in[:, s_v:s_qk], w_in[:, s_qk:s_vb], w_in[:, s_g:],
                              w_in[:, s_q:s_k], w_in[:, s_k:s_v], w_in[:, s_vb:s_g],
                              jnp.zeros((D_MODEL, LANES - 2 * B_HEADS), F32)], axis=1)
    b_gates = jnp.concatenate([b_mlstm_gates[0], jnp.zeros((LANES - 2 * B_HEADS,), F32)]).reshape(1, LANES)
    depth = w_router_group.shape[0]
    w_re = jnp.transpose(w_router_expert, (0, 2, 1, 3)).reshape(depth, D_MODEL, N_EXPERTS)
    w_router = jnp.concatenate([w_router_group, w_re,
                                jnp.zeros((depth, D_MODEL, LANES - N_GROUPS - N_EXPERTS), F32)], axis=-1)
    b_router = jnp.concatenate([b_router_group, b_router_expert.reshape(depth, N_EXPERTS),
                                jnp.zeros((depth, LANES - N_GROUPS - N_EXPERTS), F32)], axis=-1)[:, None, :]
    p = dict(norm_mix=norm_mix, norm_ffn=norm_ffn, norm_final=norm_final, rel_table=rel_bias_table,
             sinks=attn_sinks[0], w_in=w_in_r, b_gates=b_gates, conv_w=w_mlstm_qk_conv[0],
             g_mnorm=g_mlstm_norm[0].reshape(1, B_V), w_out=w_out_mix[0], w_pw1=w_pw1[0], b_pw1=b_pw1[0],
             w_dw=w_dw[0], b_dw=b_dw[0], ln_g=ln_conv_g[0], ln_b=ln_conv_b[0], w_pw2=w_pw2[0], b_pw2=b_pw2[0],
             w_router=w_router, b_router=b_router, w_eg=w_expert_gate, w_eu=w_expert_up, w_ed=w_expert_down)
    caches = (cache_win_k[0], cache_win_v[0], state_mlstm_c[0], state_mlstm_n[0], state_mlstm_m[0],
              state_mlstm_conv[0], state_conv[0])
    out_p = _trunk(x_prompt, None, p, prompt=True)
    out_s = _trunk(x_sample, caches, p, prompt=False)
    return (out_p[0], out_s[0]) + out_p[1:] + out_s[1:]
```

```python
import functools
import math

import numpy as np
import jax
import jax.numpy as jnp
from jax import lax
from jax.experimental import pallas as pl
from jax.experimental.pallas import tpu as pltpu

F32 = jnp.float32
BF16 = jnp.bfloat16
I32 = jnp.int32
HI = lax.Precision.HIGHEST
NEG_INF = float("-inf")

LANES = 128
SUBLANES = 8
VMEM_LIMIT = 56 * 1024 * 1024

D_MODEL = 1024
A_HEADS = 8
A_KV_HEADS = 2
A_GROUP = A_HEADS // A_KV_HEADS
A_HEAD_DIM = 64
WINDOW = 128
REL_BUCKETS = 32
REL_MAX_DIST = 128
B_HEADS = 4
B_DK = 64
B_DV = 128
B_CONV = 4
C_KERNEL = 31
N_GROUPS = 4
EXPERTS_PER_GROUP = 8
N_EXPERTS = N_GROUPS * EXPERTS_PER_GROUP
TOP_K = 2
EXPERT_FF = D_MODEL // 2
EXPERT_BLOCK = 512
EXPERT_BLOCK_PRECISE = 128
EPS = 1e-6

A_Q = A_HEADS * A_HEAD_DIM
A_KV = A_KV_HEADS * A_HEAD_DIM
B_QK = B_HEADS * B_DK
B_V = B_HEADS * B_DV
ROW_TILES = D_MODEL // LANES

Z_QA, Z_QK, Z_VB, Z_OG, Z_KA, Z_VA, Z_GATES = 0, 512, 1024, 1536, 2048, 2176, 2304
Z_COLS = 2432
MLSTM_CHUNK = 128


def _cparams(*sem):
    return pltpu.CompilerParams(dimension_semantics=sem, vmem_limit_bytes=VMEM_LIMIT)


def _rms(x, g):
    return x * lax.rsqrt(jnp.mean(x * x, axis=-1, keepdims=True) + EPS) * g


def _mm(a, w, precise):
    if not precise:
        return jnp.dot(a.astype(BF16), w, preferred_element_type=F32)
    a = a.astype(F32)
    a_hi = a.astype(BF16)
    a_lo = (a - a_hi.astype(F32)).astype(BF16)
    w_hi = w.astype(BF16)
    w_lo = (w - w_hi.astype(F32)).astype(BF16)
    return (jnp.dot(a_hi, w_hi, preferred_element_type=F32) + jnp.dot(a_lo, w_hi, preferred_element_type=F32)
            + jnp.dot(a_hi, w_lo, preferred_element_type=F32))


def _t5_buckets(dist):
    exact = REL_BUCKETS // 2
    d = np.maximum(dist, 0)
    large = exact + (np.log(np.maximum(d, 1).astype(np.float32) / exact)
                     / math.log(REL_MAX_DIST / exact) * (REL_BUCKETS - exact)).astype(np.int32)
    return np.where(d < exact, d, np.minimum(large, REL_BUCKETS - 1)).astype(np.int32)


P_QK, P_OG, P_KA, P_VA, P_GATES = 0, 512, 1024, 1152, 1280
P_COLS = 1408
PB_QA, PB_VB = 0, 512
PB_COLS = 1024


def _norm_proj_body(x_ref, g_ref, w_ref, *o_refs, precise):
    h = _rms(x_ref[...], g_ref[...])
    z = _mm(h, w_ref[...], precise)
    if len(o_refs) == 1:
        o_refs[0][...] = z
        return
    z32_ref, zb_ref = o_refs
    z32_ref[...] = jnp.concatenate([z[:, Z_QK:Z_QK + 2 * B_QK], z[:, Z_OG:Z_OG + B_V], z[:, Z_KA:Z_GATES + LANES]],
                                   axis=1)
    zb_ref[...] = jnp.concatenate([z[:, Z_QA:Z_QA + A_Q], z[:, Z_VB:Z_VB + B_V]], axis=1).astype(BF16)


def norm_proj(x, g, w, *, precise):
    rows, d = x.shape
    cols = w.shape[1]
    tm = min(rows, 128 if precise else 512)
    row_spec = lambda width: pl.BlockSpec((tm, width), lambda i: (i, 0))
    if precise:
        out_specs, out_shape = row_spec(cols), jax.ShapeDtypeStruct((rows, cols), F32)
    else:
        out_specs = [row_spec(P_COLS), row_spec(PB_COLS)]
        out_shape = [jax.ShapeDtypeStruct((rows, P_COLS), F32), jax.ShapeDtypeStruct((rows, PB_COLS), BF16)]
    return pl.pallas_call(
        functools.partial(_norm_proj_body, precise=precise),
        grid=(rows // tm,),
        in_specs=[row_spec(d), pl.BlockSpec((1, d), lambda i: (0, 0)), pl.BlockSpec((d, cols), lambda i: (0, 0))],
        out_specs=out_specs,
        out_shape=out_shape,
        compiler_params=_cparams("parallel"),
        name="norm_proj",
    )(x, g, w)


def _attn_prompt_body(tab_ref, sink_ref, bkt_ref, q_ref, kp_ref, kc_ref, vp_ref, vc_ref, o_ref, bias_ref, *, nq):
    b = pl.program_id(0)
    n = pl.program_id(1)

    @pl.when((b == 0) & (n == 0))
    def _():
        bk = bkt_ref[...]
        first = lax.broadcasted_iota(I32, bk.shape, 1) >= WINDOW
        for h in range(A_HEADS):
            acc = jnp.full(bk.shape, NEG_INF, F32)
            for t in range(REL_BUCKETS):
                acc = jnp.where(bk == t, tab_ref[t, h], acc)
            bias_ref[h] = acc
            bias_ref[A_HEADS + h] = jnp.where(first, acc, NEG_INF)

    k_all = jnp.concatenate([kp_ref[...], kc_ref[...]], axis=0).astype(BF16)
    v_all = jnp.concatenate([vp_ref[...], vc_ref[...]], axis=0).astype(BF16)
    for sub in range(nq):
        rows = slice(sub * WINDOW, (sub + 1) * WINDOW)
        q = q_ref[rows, :] * (A_HEAD_DIM ** -0.5)
        kb = k_all[sub * WINDOW:(sub + 2) * WINDOW]
        vb = v_all[sub * WINDOW:(sub + 2) * WINDOW]
        table = jnp.where(n == 0, A_HEADS, 0) if sub == 0 else 0
        outs = []
        for h in range(A_HEADS):
            kvh = h // A_GROUP
            qh = q[:, h * A_HEAD_DIM:(h + 1) * A_HEAD_DIM].astype(BF16)
            kh = kb[:, kvh * A_HEAD_DIM:(kvh + 1) * A_HEAD_DIM]
            vh = vb[:, kvh * A_HEAD_DIM:(kvh + 1) * A_HEAD_DIM]
            s = lax.dot_general(qh, kh, (((1,), (1,)), ((), ())), preferred_element_type=F32)
            s = s + bias_ref[table + h]
            sink = sink_ref[h]
            mx = jnp.maximum(jnp.max(s, axis=-1, keepdims=True), sink)
            p = jnp.exp(s - mx)
            den = jnp.sum(p, axis=-1, keepdims=True) + jnp.exp(sink - mx)
            outs.append(jnp.dot(p.astype(BF16), vh, preferred_element_type=F32) / den)
        o_ref[rows, :] = jnp.concatenate(outs, axis=1).astype(o_ref.dtype)


ATTN_BLOCKS = 2


def attn_prompt(z, zb, rel_table, sinks, bsz, seq):
    nb = seq // WINDOW
    dist = WINDOW + np.arange(WINDOW)[:, None] - np.arange(2 * WINDOW)[None, :]
    bkt = np.where((dist >= 0) & (dist <= WINDOW), _t5_buckets(dist), -1).astype(np.int32)
    kcol, vcol = P_KA // LANES, P_VA // LANES
    smem = pl.BlockSpec(memory_space=pltpu.SMEM)
    nq = ATTN_BLOCKS if nb % ATTN_BLOCKS == 0 else 1
    ns = nb // nq

    def cur(c):
        return pl.BlockSpec((nq * WINDOW, LANES), lambda b, n: (b * ns + n, c))

    def prev(c):
        return pl.BlockSpec((WINDOW, LANES), lambda b, n: (b * nb + jnp.maximum(n * nq - 1, 0), c))

    return pl.pallas_call(
        functools.partial(_attn_prompt_body, nq=nq),
        grid=(bsz, ns),
        in_specs=[smem, smem,
                  pl.BlockSpec((WINDOW, 2 * WINDOW), lambda b, n: (0, 0)),
                  pl.BlockSpec((nq * WINDOW, A_Q), lambda b, n: (b * ns + n, PB_QA // A_Q)),
                  prev(kcol), cur(kcol), prev(vcol), cur(vcol)],
        out_specs=pl.BlockSpec((nq * WINDOW, A_Q), lambda b, n: (b * ns + n, 0)),
        out_shape=jax.ShapeDtypeStruct((bsz * seq, A_Q), BF16),
        scratch_shapes=[pltpu.VMEM((2 * A_HEADS, WINDOW, 2 * WINDOW), F32)],
        compiler_params=_cparams("arbitrary", "arbitrary"),
        name="attn_prompt",
    )(rel_table, sinks, jnp.asarray(bkt), zb, z, z, z, z)


def _log_sigmoid(x):
    return -(jnp.maximum(-x, 0.0) + jnp.log1p(jnp.exp(-jnp.abs(x))))


def _mlstm_prompt_body(qk_ref, v_ref, g_ref, og_ref, cw_ref, bg_ref, gn_ref,
                       ob_ref, c_out, n_out, m_out,
                       c_sc, n_sc, m_sc, hist_sc, *, L, nc, nbat, nsub):
    ci = pl.program_id(1)

    @pl.when(ci == 0)
    def _():
        c_sc[...] = jnp.zeros(c_sc.shape, F32)
        n_sc[...] = jnp.zeros(n_sc.shape, F32)
        m_sc[...] = jnp.full(m_sc.shape, NEG_INF, F32)
        hist_sc[...] = jnp.zeros(hist_sc.shape, F32)

    row = lax.broadcasted_iota(I32, (L, L), 0)
    colm = lax.broadcasted_iota(I32, (L, L), 1)
    causal = colm <= row
    tril = causal.astype(F32)
    for sub in range(nsub):
        rows = pl.ds(sub * L, L)
        stores = []
        for bb in range(nbat):
            stores += _mlstm_chunk(qk_ref.at[bb, rows], v_ref.at[bb, rows], g_ref.at[bb, rows], og_ref.at[bb, rows],
                                   cw_ref, bg_ref, gn_ref, ob_ref.at[bb, rows], c_sc.at[bb], n_sc.at[bb],
                                   m_sc.at[bb], hist_sc.at[bb], causal, tril, L)
        for store in stores:
            store()

    @pl.when(ci == nc - 1)
    def _():
        c_out[...] = c_sc[...]
        n_out[...] = n_sc[...]
        m_out[...] = m_sc[...]


def _mlstm_chunk(qk_ref, v_ref, g_ref, og_ref, cw_ref, bg_ref, gn_ref, ob_ref, c_sc, n_sc, m_sc, hist_sc,
                 causal, tril, L):
    cur = qk_ref[...]
    ext = jnp.concatenate([hist_sc[...], cur], axis=0)
    cw = cw_ref[...]
    off = SUBLANES - (B_CONV - 1)
    conv = ext[off:off + L] * cw[0:1]
    for j in range(1, B_CONV):
        conv = conv + ext[off + j:off + j + L] * cw[j:j + 1]
    qk = conv * jax.nn.sigmoid(conv)
    q_all = qk[:, :B_QK]
    k_all = qk[:, B_QK:] * (B_DK ** -0.5)
    k_t = k_all.T
    v_all = v_ref[...]
    og = og_ref[...]
    gn = gn_ref[...]

    G = g_ref[...] + bg_ref[...]
    lf = _log_sigmoid(G)
    Bc = jnp.dot(tril, lf, precision=HI, preferred_element_type=F32)
    BT = Bc.T
    GT = G.T

    H = range(B_HEADS)
    nt = (((1,), (1,)), ((), ()))
    qh = [q_all[:, h * B_DK:(h + 1) * B_DK] for h in H]
    kh = [k_all[:, h * B_DK:(h + 1) * B_DK] for h in H]
    qb = [q.astype(BF16) for q in qh]
    vb = [v_all[:, h * B_DV:(h + 1) * B_DV].astype(BF16) for h in H]
    b_col = [Bc[:, B_HEADS + h:B_HEADS + h + 1] for h in H]
    c0 = [c_sc[h] for h in H]
    n0 = [n_sc[h:h + 1, :] for h in H]
    a = [b_col[h] + m_sc[h:h + 1, 0:1] for h in H]
    d = [jnp.where(causal, b_col[h] - BT[B_HEADS + h:B_HEADS + h + 1, :] + GT[h:h + 1, :], NEG_INF) for h in H]
    qk_raw = [lax.dot_general(qb[h], kh[h].astype(BF16), nt, preferred_element_type=F32) for h in H]
    qc = [jnp.dot(qb[h], c0[h].astype(BF16), preferred_element_type=F32) for h in H]
    m = [jnp.maximum(a[h], jnp.max(d[h], axis=-1, keepdims=True)) for h in H]
    dw = [jnp.exp(d[h] - m[h]) for h in H]
    aw = [jnp.exp(a[h] - m[h]) for h in H]
    s = [qk_raw[h] * dw[h] for h in H]
    sv = [jnp.dot(s[h].astype(BF16), vb[h], preferred_element_type=F32) for h in H]
    li = lax.broadcasted_iota(I32, (B_QK, LANES), 0) >> (B_DK.bit_length() - 1)
    ind = (li == lax.broadcasted_iota(I32, (B_QK, LANES), 1)).astype(F32)
    qn = jnp.dot(q_all * jnp.concatenate(n0, axis=1), ind, precision=HI, preferred_element_type=F32)
    den = [jnp.sum(s[h], axis=-1, keepdims=True) + aw[h] * qn[:, h:h + 1] for h in H]
    hh = [(sv[h] + aw[h] * qc[h]) / jnp.maximum(jnp.abs(den[h]), jnp.exp(-m[h])) for h in H]
    m_last = [m[h][L - 1:L, :] for h in H]
    wl = [jnp.exp(b_col[h][L - 1:L, :] - b_col[h] + G[:, h:h + 1] - m_last[h]) for h in H]
    decay = [aw[h][L - 1:L, :] for h in H]
    kw_t = [(k_t[h * B_DK:(h + 1) * B_DK, :] * dw[h][L - 1:L, :]).astype(BF16) for h in H]
    c1 = [decay[h] * c0[h] + jnp.dot(kw_t[h], vb[h], preferred_element_type=F32) for h in H]
    n1 = [decay[h] * n0[h] + jnp.sum(kh[h] * wl[h], axis=0, keepdims=True) for h in H]
    stores = [functools.partial(_store_state, c_sc, n_sc, m_sc, h, c1[h], n1[h], m_last[h]) for h in H]
    hn = [hh[h] * lax.rsqrt(jnp.mean(hh[h] * hh[h], axis=-1, keepdims=True) + EPS) * gn[:, h * B_DV:(h + 1) * B_DV]
          for h in H]
    outs = [jax.nn.sigmoid(og[:, h * B_DV:(h + 1) * B_DV]) * hn[h] for h in H]
    out = jnp.concatenate(outs, axis=1).astype(ob_ref.dtype)
    stores.append(functools.partial(_store_chunk, ob_ref, hist_sc, out, cur[L - SUBLANES:L]))
    return stores


def _store_state(c_sc, n_sc, m_sc, h, c1, n1, m_last):
    c_sc[h] = c1
    n_sc[h:h + 1, :] = n1
    m_sc[h:h + 1, :] = jnp.broadcast_to(m_last, (1, LANES))


def _store_chunk(ob_ref, hist_sc, out, tail):
    ob_ref[...] = out
    hist_sc[...] = tail


MLSTM_BATCH = 1
MLSTM_SUBCHUNKS = 1


def mlstm_prompt(z, zb, conv_w, b_gates_pad, g_norm, bsz, seq):
    L = MLSTM_CHUNK
    nsub = MLSTM_SUBCHUNKS if (seq // L) % MLSTM_SUBCHUNKS == 0 else 1
    nc = seq // (L * nsub)
    nbat = MLSTM_BATCH if bsz % MLSTM_BATCH == 0 else 1
    z3 = z.reshape(bsz, seq, P_COLS)
    zb3 = zb.reshape(bsz, seq, PB_COLS)

    def zspec(width, colblk):
        return pl.BlockSpec((nbat, nsub * L, width), lambda b, c: (b, c, colblk))

    const = lambda shape: pl.BlockSpec(shape, lambda b, c: (0,) * len(shape))
    state = lambda shape: pl.BlockSpec((nbat,) + shape, lambda b, c: (b,) + (0,) * len(shape))
    out_b, c1, n1, m1 = pl.pallas_call(
        functools.partial(_mlstm_prompt_body, L=L, nc=nc, nbat=nbat, nsub=nsub),
        grid=(bsz // nbat, nc),
        in_specs=[zspec(2 * B_QK, P_QK // (2 * B_QK)), zspec(B_V, PB_VB // B_V), zspec(LANES, P_GATES // LANES),
                  zspec(B_V, P_OG // B_V), const((B_CONV, 2 * B_QK)), const((1, LANES)), const((1, B_V))],
        out_specs=[pl.BlockSpec((nbat, nsub * L, B_V), lambda b, c: (b, c, 0)),
                   state((B_HEADS, B_DK, B_DV)), state((SUBLANES, B_DK)), state((SUBLANES, LANES))],
        out_shape=[jax.ShapeDtypeStruct((bsz, seq, B_V), BF16),
                   jax.ShapeDtypeStruct((bsz, B_HEADS, B_DK, B_DV), F32),
                   jax.ShapeDtypeStruct((bsz, SUBLANES, B_DK), F32),
                   jax.ShapeDtypeStruct((bsz, SUBLANES, LANES), F32)],
        scratch_shapes=[pltpu.VMEM((nbat, B_HEADS, B_DK, B_DV), F32), pltpu.VMEM((nbat, SUBLANES, B_DK), F32),
                        pltpu.VMEM((nbat, SUBLANES, LANES), F32), pltpu.VMEM((nbat, SUBLANES, 2 * B_QK), F32)],
        compiler_params=_cparams("arbitrary", "arbitrary"),
        name="mlstm_prompt",
    )(z3, zb3, z3, z3, conv_w, b_gates_pad, g_norm)
    return out_b.reshape(bsz * seq, B_V), c1, n1, m1


SAMPLE_TILE = 8


def _mix_sample_body(tab_ref, sink_ref, bkt_ref, z_ref, ck_ref, cv_ref, c0_ref, n0_ref, m0_ref, cb_ref,
                     cw_ref, bg_ref, gn_ref,
                     att_ref, ob_ref, nk_ref, nv_ref, c1_ref, n1_ref, m1_ref, ncb_ref, *, tb):
    bk = bkt_ref[...]
    cw = cw_ref[...]
    gn = gn_ref[...]
    bias_rows = []
    for h in range(A_HEADS):
        bias = jnp.zeros(bk.shape, F32)
        for t in range(REL_BUCKETS):
            bias = jnp.where(bk == t, tab_ref[t, h], bias)
        bias_rows.append(bias)
    n_keys = bk.shape[1]
    rows, cols = A_HEADS * tb, tb * n_keys
    z_all = z_ref[...]
    q_all = z_all[:, Z_QA:Z_QA + A_Q] * (A_HEAD_DIM ** -0.5)
    ka_all = z_all[:, Z_KA:Z_KA + A_KV]
    va_all = z_all[:, Z_VA:Z_VA + A_KV]
    zero_half = jnp.zeros((tb, A_HEAD_DIM), F32)
    q_rows = []
    for h in range(A_HEADS):
        qh = q_all[:, h * A_HEAD_DIM:(h + 1) * A_HEAD_DIM]
        q_rows.append(jnp.concatenate([qh, zero_half] if h // A_GROUP == 0 else [zero_half, qh], axis=1))
    qm = jnp.concatenate(q_rows, axis=0)
    ka_rep = jnp.concatenate([ka_all] * A_HEADS, axis=0)
    va_rep = jnp.concatenate([va_all] * A_HEADS, axis=0)
    k_cat = jnp.concatenate([ck_ref[i] for i in range(tb)], axis=0)
    v_cat = jnp.concatenate([cv_ref[i] for i in range(tb)], axis=0)
    bias_c = jnp.concatenate([jnp.broadcast_to(b, (tb, n_keys)) for b in bias_rows], axis=0)
    bias_c = jnp.concatenate([bias_c] * tb, axis=1)
    r_id = lax.broadcasted_iota(I32, (rows, cols), 0)
    c_id = lax.broadcasted_iota(I32, (rows, cols), 1)
    own = (r_id & (tb - 1)) == (c_id >> (n_keys.bit_length() - 1))
    head_id = lax.broadcasted_iota(I32, (rows, 1), 0) >> (tb.bit_length() - 1)
    bias_n = jnp.zeros((rows, 1), F32)
    sinks = jnp.zeros((rows, 1), F32)
    for h in range(A_HEADS):
        bias_n = jnp.where(head_id == h, tab_ref[0, h], bias_n)
        sinks = jnp.where(head_id == h, sink_ref[h], sinks)
    lc = lax.dot_general(qm, k_cat, (((1,), (1,)), ((), ())), precision=HI, preferred_element_type=F32)
    lc = jnp.where(own, lc + bias_c, NEG_INF)
    ln = jnp.sum(qm * ka_rep, axis=-1, keepdims=True) + bias_n
    mx = jnp.maximum(jnp.maximum(jnp.max(lc, axis=-1, keepdims=True), ln), sinks)
    pc = jnp.exp(lc - mx)
    pn = jnp.exp(ln - mx)
    den = jnp.sum(pc, axis=-1, keepdims=True) + pn + jnp.exp(sinks - mx)
    o = (jnp.dot(pc, v_cat, precision=HI, preferred_element_type=F32) + pn * va_rep) / den
    att_ref[...] = jnp.concatenate(
        [o[h * tb:(h + 1) * tb, (h // A_GROUP) * A_HEAD_DIM:(h // A_GROUP + 1) * A_HEAD_DIM]
         for h in range(A_HEADS)], axis=1)

    for i in range(tb):
        nk_ref[i] = jnp.concatenate([ck_ref[i][1:], ka_all[i:i + 1]], axis=0)
        nv_ref[i] = jnp.concatenate([cv_ref[i][1:], va_all[i:i + 1]], axis=0)

    qk_pre = z_all[:, Z_QK:Z_QK + 2 * B_QK]
    hist = [cb_ref[i] for i in range(tb)]
    conv = qk_pre * cw[B_CONV - 1:B_CONV]
    for j in range(B_CONV - 1):
        conv = conv + jnp.concatenate([hs[j:j + 1] for hs in hist], axis=0) * cw[j:j + 1]
    for i in range(tb):
        ncb_ref[i] = jnp.concatenate([hist[i][1:], qk_pre[i:i + 1]], axis=0)
    qk = conv * jax.nn.sigmoid(conv)
    q_m = qk[:, :B_QK]
    k_m = qk[:, B_QK:] * (B_DK ** -0.5)
    v_m = z_all[:, Z_VB:Z_VB + B_V]
    og = z_all[:, Z_OG:Z_OG + B_V]
    G = z_all[:, Z_GATES:Z_GATES + LANES] + bg_ref[...]
    ig = G[:, :B_HEADS]
    a = _log_sigmoid(G)[:, B_HEADS:2 * B_HEADS] + m0_ref[...]
    m = jnp.maximum(a, ig)
    dw = jnp.exp(ig - m)
    aw = jnp.exp(a - m)
    li = lax.broadcasted_iota(I32, (B_QK, LANES), 0) >> (B_DK.bit_length() - 1)
    ind = (li == lax.broadcasted_iota(I32, (B_QK, LANES), 1)).astype(F32)
    n0_m = jnp.concatenate([jnp.concatenate([n0_ref[i, h:h + 1, :] for h in range(B_HEADS)], axis=1)
                            for i in range(tb)], axis=0)
    qk_dot = jnp.dot(q_m * k_m, ind, precision=HI, preferred_element_type=F32)[:, :B_HEADS]
    qn_dot = jnp.dot(q_m * n0_m, ind, precision=HI, preferred_element_type=F32)[:, :B_HEADS]
    s = qk_dot * dw
    inv = 1.0 / jnp.maximum(jnp.abs(s + aw * qn_dot), jnp.exp(-m))
    m1_ref[...] = jnp.concatenate([m, jnp.zeros((tb, LANES - B_HEADS), F32)], axis=1)
    stack = jnp.concatenate([q_m[:, h * B_DK:(h + 1) * B_DK] for h in range(B_HEADS)]
                            + [k_m[:, h * B_DK:(h + 1) * B_DK] for h in range(B_HEADS)], axis=0)
    cols = jnp.concatenate([stack, jnp.zeros_like(stack)], axis=1).T[:B_DK]
    hh_rows = [[] for _ in range(B_HEADS)]
    for i in range(tb):
        for h in range(B_HEADS):
            jq = h * tb + i
            jk = (B_HEADS + h) * tb + i
            c0 = c0_ref[i, h]
            vh = v_m[i:i + 1, h * B_DV:(h + 1) * B_DV]
            aw_p, dw_p = aw[i:i + 1, h:h + 1], dw[i:i + 1, h:h + 1]
            num = s[i:i + 1, h:h + 1] * vh + aw_p * jnp.sum(cols[:, jq:jq + 1] * c0, axis=0, keepdims=True)
            hh_rows[h].append(num * inv[i:i + 1, h:h + 1])
            c1_ref[i, h] = aw_p * c0 + dw_p * (cols[:, jk:jk + 1] * vh)
            n1_ref[i, h:h + 1, :] = aw_p * n0_ref[i, h:h + 1, :] + dw_p * k_m[i:i + 1, h * B_DK:(h + 1) * B_DK]
    obs = []
    for h in range(B_HEADS):
        hh = jnp.concatenate(hh_rows[h], axis=0)
        hn = hh * lax.rsqrt(jnp.mean(hh * hh, axis=-1, keepdims=True) + EPS) * gn[:, h * B_DV:(h + 1) * B_DV]
        obs.append(jax.nn.sigmoid(og[:, h * B_DV:(h + 1) * B_DV]) * hn)
    ob_ref[...] = jnp.concatenate(obs, axis=1)


def mix_sample(z, rel_table, sinks, ck, cv, c0, n0, m0, conv_buf, conv_w, b_gates_pad, g_norm):
    nb = z.shape[0]
    tb = min(SAMPLE_TILE, nb)
    n_buf = ck.shape[1]
    bkt = _t5_buckets(n_buf - np.arange(n_buf))[None, :]
    smem = pl.BlockSpec(memory_space=pltpu.SMEM)
    const = lambda shape: pl.BlockSpec(shape, lambda i: (0,) * len(shape))
    lead = lambda shape: pl.BlockSpec((tb,) + shape, lambda i: (i,) + (0,) * len(shape))
    return pl.pallas_call(
        functools.partial(_mix_sample_body, tb=tb),
        grid=(nb // tb,),
        in_specs=[smem, smem, const((1, n_buf)), lead((Z_COLS,)), lead((n_buf, A_KV)), lead((n_buf, A_KV)),
                  lead((B_HEADS, B_DK, B_DV)), lead((B_HEADS, B_DK)), lead((B_HEADS,)),
                  lead((B_CONV - 1, 2 * B_QK)), const((B_CONV, 2 * B_QK)), const((1, LANES)), const((1, B_V))],
        out_specs=[lead((A_Q,)), lead((B_V,)), lead((n_buf, A_KV)), lead((n_buf, A_KV)),
                   lead((B_HEADS, B_DK, B_DV)), lead((B_HEADS, B_DK)), lead((LANES,)),
                   lead((B_CONV - 1, 2 * B_QK))],
        out_shape=[jax.ShapeDtypeStruct((nb, A_Q), F32), jax.ShapeDtypeStruct((nb, B_V), F32),
                   jax.ShapeDtypeStruct(ck.shape, F32), jax.ShapeDtypeStruct(cv.shape, F32),
                   jax.ShapeDtypeStruct(c0.shape, F32), jax.ShapeDtypeStruct(n0.shape, F32),
                   jax.ShapeDtypeStruct((nb, LANES), F32), jax.ShapeDtypeStruct(conv_buf.shape, F32)],
        compiler_params=_cparams("parallel"),
        name="mix_sample",
    )(rel_table, sinks, jnp.asarray(bkt), z, ck, cv, c0, n0, m0, conv_buf, conv_w, b_gates_pad, g_norm)


def _store_row_tiles(ref, val, rows):
    for s in range(ROW_TILES):
        ref[pl.ds(s, rows, stride=ROW_TILES), :] = val[:, s * LANES:(s + 1) * LANES]


def _load_row_tiles(ref, rows, start=0, stride=ROW_TILES):
    return jnp.concatenate([ref[pl.ds(start + s, rows, stride=stride), :] for s in range(ROW_TILES)], axis=1)


def _route_sort(logits):
    tm = logits.shape[0]
    lt = logits.T
    big = jnp.int32(1 << 20)
    n_rows = SUBLANES * ((N_GROUPS + N_EXPERTS + SUBLANES - 1) // SUBLANES)
    row = lax.broadcasted_iota(I32, (n_rows, tm), 0)
    x = lt[:n_rows]
    gl = jnp.where(row < N_GROUPS, x, NEG_INF)
    gmax = jnp.max(gl, axis=0, keepdims=True)
    gidx = jnp.min(jnp.where(gl == gmax, row, big), axis=0, keepdims=True)
    g_gate = 1.0 / jnp.sum(jnp.exp(gl - gmax), axis=0, keepdims=True)
    lo = N_GROUPS + gidx * EXPERTS_PER_GROUP
    el = jnp.where((row >= lo) & (row < lo + EXPERTS_PER_GROUP), x, NEG_INF)
    v1 = jnp.max(el, axis=0, keepdims=True)
    i1 = jnp.min(jnp.where(el == v1, row, big), axis=0, keepdims=True)
    el2 = jnp.where(row == i1, NEG_INF, el)
    v2 = jnp.max(el2, axis=0, keepdims=True)
    i2 = jnp.min(jnp.where(el2 == v2, row, big), axis=0, keepdims=True)
    t = jnp.exp(v2 - v1)
    w1 = g_gate / (1.0 + t)
    w2 = g_gate * t / (1.0 + t)
    e0 = i1 - N_GROUPS
    e1 = i2 - N_GROUPS

    erow = lax.broadcasted_iota(I32, (N_EXPERTS, tm), 0)
    oh0 = erow == e0
    oh1 = erow == e1
    r = lax.broadcasted_iota(I32, (tm, tm), 0)
    c = lax.broadcasted_iota(I32, (tm, tm), 1)
    earlier = (r < c).astype(BF16)
    rank0 = jnp.dot(oh0.astype(BF16), earlier, preferred_element_type=F32)
    rank1 = jnp.dot(oh1.astype(BF16), earlier, preferred_element_type=F32)
    cnt0 = jnp.sum(oh0.astype(F32), axis=1, keepdims=True)
    run_n = cnt0 + jnp.sum(oh1.astype(F32), axis=1, keepdims=True)
    slot_n = jnp.floor((run_n + (RUN_SLOT - 1)) * (1.0 / RUN_SLOT)) * RUN_SLOT
    er = lax.broadcasted_iota(I32, (N_EXPERTS, N_EXPERTS), 0)
    ec = lax.broadcasted_iota(I32, (N_EXPERTS, N_EXPERTS), 1)
    run_l = jnp.dot((ec < er).astype(F32), jnp.broadcast_to(slot_n, (N_EXPERTS, LANES)), precision=HI,
                    preferred_element_type=F32)[:, 0:1]
    p0 = jnp.sum(jnp.where(oh0, rank0 + run_l, 0.0), axis=0, keepdims=True)
    p1 = jnp.sum(jnp.where(oh1, rank1 + run_l + cnt0, 0.0), axis=0, keepdims=True)
    pad = jnp.zeros((SUBLANES - TOP_K, tm), F32)
    lane_e = lax.broadcasted_iota(I32, (N_EXPERTS, LANES), 1)
    run_row = jnp.sum(jnp.where(lane_e == lax.broadcasted_iota(I32, (N_EXPERTS, LANES), 0),
                                jnp.broadcast_to(run_n, (N_EXPERTS, LANES)), 0.0), axis=0, keepdims=True)
    return jnp.concatenate([p0, p1, pad], axis=0), jnp.concatenate([w1, w2, pad], axis=0), run_row


MOE_CHUNK = 512
RUN_SLOT = 32


def _moe_chunk(rows, precise):
    return min(rows, 128 if precise else MOE_CHUNK)


def _proj_router_body(*refs, n_in, has_bias, precise, tm):
    a_refs = refs[:n_in]
    w_refs = refs[n_in:2 * n_in]
    k = 2 * n_in
    bias_ref = refs[k] if has_bias else None
    k += 1 if has_bias else 0
    x_ref, g_ref, wr_ref, br_ref, x1_ref, h8_ref, lpos_ref, gate_ref, runn_ref = refs[k:]
    acc = x_ref[...]
    if has_bias:
        acc = acc + bias_ref[...]
    for a_ref, w_ref in zip(a_refs, w_refs):
        acc = acc + _mm(a_ref[...], w_ref[...], precise)
    x1_ref[...] = acc
    h = _rms(acc, g_ref[...])
    _store_row_tiles(h8_ref, h, tm)
    wr = wr_ref[...]
    if precise:
        logits = jnp.dot(h, wr, precision=HI, preferred_element_type=F32)
    else:
        h_hi = h.astype(BF16)
        h_lo = (h - h_hi.astype(F32)).astype(BF16)
        w_hi = wr.astype(BF16)
        w_lo = (wr - w_hi.astype(F32)).astype(BF16)
        logits = (jnp.dot(h_hi, w_hi, preferred_element_type=F32) + jnp.dot(h_lo, w_hi, preferred_element_type=F32)
                  + jnp.dot(h_hi, w_lo, preferred_element_type=F32))
    lpos, gate, run_n = _route_sort(logits + br_ref[...])
    lpos_ref[...] = (lpos * ROW_TILES).astype(I32)
    gate_ref[...] = gate
    runn_ref[...] = jnp.broadcast_to(run_n, runn_ref.shape).astype(I32)


def proj_router(a_list, w_list, bias, x, g, wr, br, *, precise):
    rows, d = x.shape
    tm = _moe_chunk(rows, precise)
    n_in = len(a_list)
    row_spec = lambda width: pl.BlockSpec((tm, width), lambda i: (i, 0))
    const = lambda shape: pl.BlockSpec(shape, lambda i: (0,) * len(shape))
    in_specs = [row_spec(a.shape[1]) for a in a_list] + [const(w.shape) for w in w_list]
    args = list(a_list) + list(w_list)
    if bias is not None:
        in_specs.append(const((1, d)))
        args.append(bias)
    in_specs += [row_spec(d), const((1, d)), const((d, LANES)), const((1, LANES))]
    args += [x, g, wr, br]
    return pl.pallas_call(
        functools.partial(_proj_router_body, n_in=n_in, has_bias=bias is not None, precise=precise, tm=tm),
        grid=(rows // tm,),
        in_specs=in_specs,
        out_specs=[row_spec(d), pl.BlockSpec((tm * ROW_TILES, LANES), lambda i: (i, 0)),
                   pl.BlockSpec((SUBLANES, tm), lambda i: (i, 0)), pl.BlockSpec((SUBLANES, tm), lambda i: (i, 0)),
                   pl.BlockSpec((SUBLANES, LANES), lambda i: (i, 0))],
        out_shape=[jax.ShapeDtypeStruct((rows, d), F32), jax.ShapeDtypeStruct((rows * ROW_TILES, LANES), F32),
                   jax.ShapeDtypeStruct((rows // tm * SUBLANES, tm), I32),
                   jax.ShapeDtypeStruct((rows // tm * SUBLANES, tm), F32),
                   jax.ShapeDtypeStruct((rows // tm * SUBLANES, LANES), I32)],
        compiler_params=_cparams("parallel"),
        name="proj_router",
    )(*args)


def _rows_at(offset):
    return pl.ds(pl.multiple_of(offset, ROW_TILES), ROW_TILES)


def _tile_rows(r, n=1):
    return pl.ds(pl.multiple_of(r * ROW_TILES, ROW_TILES), n * ROW_TILES)


def _pow2_pieces(limit):
    p = 1
    while p * 2 <= limit:
        p *= 2
    out = []
    while p >= 1:
        out.append(p)
        p //= 2
    return out


COMMON_PIECE = 32


def _for_each_piece(n, pieces, fn):
    def emit(ps):
        for p in ps:
            @pl.when((n & p) != 0)
            def _(p=p):
                fn(n & ~(2 * p - 1), p)

    big = [p for p in pieces if p > COMMON_PIECE]
    if big:
        @pl.when(n > 2 * COMMON_PIECE - 1)
        def _():
            emit(big)
    emit([p for p in pieces if p <= COMMON_PIECE])


def _piece_rows(first_row, i):
    return _tile_rows(first_row + i * RUN_SLOT, RUN_SLOT)


def _dispatch_body(rg_ref, sl_ref, np_ref, npc_ref, ps_ref, pn_ref, tail_ref, lpos_ref, h8_ref, xs_hbm,
                   stage, zbuf, sem, zsem, *, chunk, nch, blk):
    c = pl.program_id(0)
    slot = c % 2
    pad_pieces = _pow2_pieces(blk + RUN_SLOT - 1)

    def wait_pieces(count, buf):
        def one(i, carry):
            pltpu.make_async_copy(stage.at[buf, _tile_rows(0, RUN_SLOT)], stage.at[buf, _tile_rows(0, RUN_SLOT)],
                                  sem).wait()
            return carry
        lax.fori_loop(0, count, one, 0)

    @pl.when(c == 0)
    def _():
        stage[...] = jnp.zeros(stage.shape, F32)

    def copy_tok(t, carry):
        row = h8_ref[_tile_rows(t), :]
        stage[slot, _rows_at(lpos_ref[0, 0, t]), :] = row
        stage[slot, _rows_at(lpos_ref[0, 0, chunk + t]), :] = row
        return carry
    lax.fori_loop(0, chunk, copy_tok, 0, unroll=8)

    @pl.when(c > 0)
    def _():
        wait_pieces(npc_ref[jnp.maximum(c - 1, 0)], 1 - slot)

    def send_runs(e, carry):
        k = c * N_EXPERTS + e

        def one(i, carry2):
            pltpu.make_async_copy(stage.at[slot, _piece_rows(sl_ref[k], i)], xs_hbm.at[_piece_rows(rg_ref[k], i)],
                                  sem).start()
            return carry2
        lax.fori_loop(0, np_ref[k], one, 0)
        return carry
    lax.fori_loop(0, N_EXPERTS, send_runs, 0)

    @pl.when(c == nch - 1)
    def _():
        wait_pieces(npc_ref[c], slot)
        zbuf[...] = jnp.zeros(zbuf.shape, F32)

        def pad_dmas(e, op):
            def one(off, p):
                cp = pltpu.make_async_copy(zbuf.at[_tile_rows(0, p)], xs_hbm.at[_tile_rows(ps_ref[e] + off, p)], zsem)
                cp.start() if op == 0 else cp.wait()
            _for_each_piece(pn_ref[e], pad_pieces, one)

        def issue(e, carry):
            pad_dmas(e, 0)
            return carry

        def wait(e, carry):
            pad_dmas(e, 1)
            return carry
        lax.fori_loop(0, N_EXPERTS, issue, 0)
        lax.fori_loop(0, N_EXPERTS, wait, 0)

        half = blk // 2

        def tail_dmas(i, op):
            cp = pltpu.make_async_copy(zbuf.at[_tile_rows(0, half)],
                                       xs_hbm.at[_tile_rows(tail_ref[0] + i * half, half)], zsem)
            cp.start() if op == 0 else cp.wait()

        def tail_issue(i, carry):
            tail_dmas(i, 0)
            return carry

        def tail_wait(i, carry):
            tail_dmas(i, 1)
            return carry
        lax.fori_loop(0, tail_ref[1], tail_issue, 0)
        lax.fori_loop(0, tail_ref[1], tail_wait, 0)


def _stage_rows(chunk):
    return TOP_K * chunk + N_EXPERTS * (RUN_SLOT - 1) // RUN_SLOT * RUN_SLOT + RUN_SLOT


def dispatch(h8, plan):
    chunk, nch, blk = plan['chunk'], plan['nch'], plan['blk']
    n_slots = plan['nblk'] * blk
    gs = pltpu.PrefetchScalarGridSpec(
        num_scalar_prefetch=7,
        grid=(nch,),
        in_specs=[pl.BlockSpec((1, 1, TOP_K * chunk), lambda c, *_: (c, 0, 0), memory_space=pltpu.SMEM),
                  pl.BlockSpec((chunk * ROW_TILES, LANES), lambda c, *_: (c, 0))],
        out_specs=pl.BlockSpec(memory_space=pl.ANY),
        scratch_shapes=[pltpu.VMEM((2, _stage_rows(chunk) * ROW_TILES, LANES), F32),
                        pltpu.VMEM((blk * ROW_TILES, LANES), F32),
                        pltpu.SemaphoreType.DMA(()), pltpu.SemaphoreType.DMA(())],
    )
    return pl.pallas_call(
        functools.partial(_dispatch_body, chunk=chunk, nch=nch, blk=blk),
        grid_spec=gs,
        out_shape=jax.ShapeDtypeStruct((n_slots * ROW_TILES, LANES), F32),
        compiler_params=_cparams("arbitrary"),
        name="dispatch",
    )(plan['run_g'], plan['slot_l'], plan['n_piece'], plan['n_piece_chunk'], plan['pad_start'], plan['pad_n'],
      plan['tail'], plan['lpos'], h8)


FF_CHUNK = 512


def _experts_body(be_ref, nv_ref, pe_ref, xs_ref, wg_ref, wu_ref, wd_ref, ys_ref, xb, *wscr, precise, blk, layer):
    j = pl.program_id(0)
    nv = nv_ref[0]
    slot = j % 2
    k = j - 1

    @pl.when(j == 0)
    def _():
        xb[1] = jnp.zeros(xb.shape[1:], xb.dtype)

    if not precise:
        wgb, wub, wdb, wgf, wuf, wdf, wsem = wscr

        def weight_copies(e, ws):
            return [pltpu.make_async_copy(src.at[layer, e], dst.at[ws], wsem.at[ws])
                    for src, dst in ((wg_ref, wgf), (wu_ref, wuf), (wd_ref, wdf))]

        @pl.when(j == 0)
        def _():
            for cp in weight_copies(be_ref[0], 0):
                cp.start()

        kc = jnp.clip(k, 0, nv - 1)
        e = be_ref[kc]
        ws = pe_ref[N_EXPERTS + e]

        @pl.when((k >= 0) & (k < nv) & ((k == 0) | (e != be_ref[jnp.maximum(kc - 1, 0)])))
        def _():
            for cp in weight_copies(e, ws):
                cp.wait()
            wgb[...] = wgf[ws].astype(BF16)
            wub[...] = wuf[ws].astype(BF16)
            wdb[...] = wdf[ws].astype(BF16)
            nxt = pe_ref[e]

            @pl.when(nxt < nv)
            def _():
                for cp in weight_copies(be_ref[jnp.minimum(nxt, nv - 1)], 1 - ws):
                    cp.start()

    @pl.when(j <= nv)
    def _():
        xb[slot] = _load_row_tiles(xs_ref, blk).astype(xb.dtype)
        if precise:
            wg, wu, wd = wg_ref[0], wu_ref[0], wd_ref[0]
        else:
            wg, wu, wd = wgb, wub, wdb
        xm = xb[1 - slot]
        gt = _mm(xm, wg[...], precise)
        up = _mm(xm, wu[...], precise)
        _store_row_tiles(ys_ref, _mm(gt * jax.nn.sigmoid(gt) * up, wd[...], precise), blk)

    @pl.when(j > nv)
    def _():
        ys_ref[...] = jnp.zeros(ys_ref.shape, F32)


def experts(xs, plan, wg, wu, wd, layer, *, precise):
    nblk, rows = plan['nblk'], plan['blk'] * ROW_TILES
    d, ff = wg.shape[2], wg.shape[3]
    blk = lambda j, be, nv, pe: (jnp.minimum(j, nv[0] - 1), 0)
    if precise:
        wspec = lambda shape: pl.BlockSpec((None, 1) + shape,
                                           lambda j, be, nv, pe: (layer, be[jnp.clip(j - 1, 0, nv[0] - 1)], 0, 0))
        wspecs = [wspec((d, ff)), wspec((d, ff)), wspec((ff, d))]
        wscratch = []
    else:
        wspecs = [pl.BlockSpec(memory_space=pl.ANY)] * 3
        wscratch = [pltpu.VMEM((d, ff), BF16), pltpu.VMEM((d, ff), BF16), pltpu.VMEM((ff, d), BF16),
                    pltpu.VMEM((2, d, ff), F32), pltpu.VMEM((2, d, ff), F32), pltpu.VMEM((2, ff, d), F32),
                    pltpu.SemaphoreType.DMA((2,))]
    gs = pltpu.PrefetchScalarGridSpec(
        num_scalar_prefetch=3,
        grid=(nblk + 1,),
        in_specs=[pl.BlockSpec((rows, LANES), blk)] + wspecs,
        out_specs=pl.BlockSpec((rows, LANES), lambda j, be, nv, pe: (jnp.maximum(j - 1, 0), 0)),
        scratch_shapes=[pltpu.VMEM((2, plan['blk'], d), F32 if precise else BF16)] + wscratch,
    )
    return pl.pallas_call(
        functools.partial(_experts_body, precise=precise, blk=plan['blk'], layer=layer),
        grid_spec=gs,
        out_shape=jax.ShapeDtypeStruct(xs.shape, F32),
        compiler_params=_cparams("arbitrary"),
        name="experts",
    )(plan['block_e'], plan['n_used'], plan['expert_tab'], xs, wg, wu, wd)


def moe_plan(lpos8, gate8, runn8, blk):
    nch = runn8.shape[0] // SUBLANES
    chunk = lpos8.shape[1]
    n_assign = nch * chunk * TOP_K
    nblk = (n_assign + N_EXPERTS * (blk - 1 + RUN_SLOT) + blk - 1) // blk
    run_n = runn8.reshape(nch, SUBLANES, LANES)[:, 0, :N_EXPERTS]
    per_chunk = lambda a: a.reshape(nch, 1, SUBLANES * chunk)[:, :, :TOP_K * chunk]
    counts = jnp.sum(run_n, axis=0)
    padded = (counts + RUN_SLOT + blk - 1) // blk * blk
    pends = jnp.cumsum(padded)
    pstarts = pends - padded
    run_g = pstarts[None, :] + jnp.cumsum(run_n, axis=0) - run_n
    n_piece = (run_n + RUN_SLOT - 1) // RUN_SLOT
    slot_l = (jnp.cumsum(n_piece, axis=1) - n_piece) * RUN_SLOT
    blk_start = jnp.arange(nblk, dtype=I32) * blk
    block_e = jnp.minimum(jnp.sum((pends[None, :] <= blk_start[:, None]).astype(I32), axis=1), N_EXPERTS - 1)
    return dict(chunk=chunk, nch=nch, nblk=nblk, blk=blk,
                run_g=run_g.reshape(-1).astype(I32), slot_l=slot_l.reshape(-1).astype(I32),
                n_piece=n_piece.reshape(-1).astype(I32), n_piece_chunk=jnp.sum(n_piece, axis=1).astype(I32),
                pad_start=(pstarts + counts).astype(I32), pad_n=(padded - counts).astype(I32),
                lpos=per_chunk(lpos8), gate=per_chunk(gate8), block_e=block_e.astype(I32),
                expert_tab=jnp.concatenate([pends // blk, jnp.arange(N_EXPERTS, dtype=I32) % 2]).astype(I32),
                n_used=(pends[-1:] // blk).astype(I32),
                tail=jnp.stack([pends[-1], 2 * (nblk - pends[-1] // blk)]).astype(I32))


def _combine(rg_ref, sl_ref, np_ref, npc_ref, lpos_ref, gate_ref, x_ref, ys_hbm, ystage, comb, sem, *, chunk, nch):
    c = pl.program_id(0)
    slot = c % 2

    def fetch(cc, sl):
        def per_e(e, carry):
            k = cc * N_EXPERTS + e

            def one(i, carry2):
                pltpu.make_async_copy(ys_hbm.at[_piece_rows(rg_ref[k], i)], ystage.at[sl, _piece_rows(sl_ref[k], i)],
                                      sem.at[sl]).start()
                return carry2
            lax.fori_loop(0, np_ref[k], one, 0)
            return carry
        lax.fori_loop(0, N_EXPERTS, per_e, 0)

    @pl.when(c == 0)
    def _():
        fetch(0, 0)

    @pl.when(c + 1 < nch)
    def _():
        fetch(c + 1, 1 - slot)

    def wait_piece(i, carry):
        pltpu.make_async_copy(ystage.at[slot, _tile_rows(0, RUN_SLOT)], ystage.at[slot, _tile_rows(0, RUN_SLOT)],
                              sem.at[slot]).wait()
        return carry
    lax.fori_loop(0, npc_ref[c], wait_piece, 0)

    def per_tok(t, carry):
        y0 = ystage[slot, _rows_at(lpos_ref[0, 0, t]), :]
        y1 = ystage[slot, _rows_at(lpos_ref[0, 0, chunk + t]), :]
        comb[_tile_rows(t), :] = gate_ref[0, 0, t] * y0 + gate_ref[0, 0, chunk + t] * y1
        return carry
    lax.fori_loop(0, chunk, per_tok, 0, unroll=8)
    return x_ref[...] + _load_row_tiles(comb, chunk)


def _combine_glu_body(rg_ref, sl_ref, np_ref, npc_ref, lpos_ref, gate_ref, x_ref, ys_hbm, g_ref, w_ref, b_ref,
                      x2_ref, u_ref, ystage, comb, sem, *, chunk, nch, precise):
    x2 = _combine(rg_ref, sl_ref, np_ref, npc_ref, lpos_ref, gate_ref, x_ref, ys_hbm, ystage, comb, sem,
                  chunk=chunk, nch=nch)
    x2_ref[...] = x2
    zz = _mm(_rms(x2, g_ref[...]), w_ref[...], precise) + b_ref[...]
    half = zz.shape[1] // 2
    u_ref[...] = zz[:, :half] * jax.nn.sigmoid(zz[:, half:])


def _combine_final_body(rg_ref, sl_ref, np_ref, npc_ref, lpos_ref, gate_ref, x_ref, ys_hbm, g_ref, o_ref,
                        ystage, comb, sem, *, chunk, nch):
    x2 = _combine(rg_ref, sl_ref, np_ref, npc_ref, lpos_ref, gate_ref, x_ref, ys_hbm, ystage, comb, sem,
                  chunk=chunk, nch=nch)
    o_ref[...] = _rms(x2, g_ref[...])


def _combine_call(body, plan, x, ys, extra, extra_specs, out_specs, out_shape, name):
    chunk, nch = plan['chunk'], plan['nch']
    d = x.shape[1]
    smem_blk = pl.BlockSpec((1, 1, TOP_K * chunk), lambda c, *_: (c, 0, 0), memory_space=pltpu.SMEM)
    gs = pltpu.PrefetchScalarGridSpec(
        num_scalar_prefetch=4,
        grid=(nch,),
        in_specs=[smem_blk, smem_blk, pl.BlockSpec((chunk, d), lambda c, *_: (c, 0)),
                  pl.BlockSpec(memory_space=pl.ANY)] + extra_specs,
        out_specs=out_specs,
        scratch_shapes=[pltpu.VMEM((2, _stage_rows(chunk) * ROW_TILES, LANES), F32),
                        pltpu.VMEM((chunk * ROW_TILES, LANES), F32), pltpu.SemaphoreType.DMA((2,))],
    )
    return pl.pallas_call(
        functools.partial(body, chunk=chunk, nch=nch),
        grid_spec=gs,
        out_shape=out_shape,
        compiler_params=_cparams("arbitrary"),
        name=name,
    )(plan['run_g'], plan['slot_l'], plan['n_piece'], plan['n_piece_chunk'], plan['lpos'], plan['gate'], x, ys,
      *extra)


def combine_glu(x, ys, plan, g, w, b, *, precise):
    rows, d = x.shape
    chunk = plan['chunk']
    cols = w.shape[1]
    const = lambda shape: pl.BlockSpec(shape, lambda c, *_: (0,) * len(shape))
    row_spec = lambda width: pl.BlockSpec((chunk, width), lambda c, *_: (c, 0))
    return _combine_call(
        functools.partial(_combine_glu_body, precise=precise), plan, x, ys, [g, w, b],
        [const((1, d)), const((d, cols)), const((1, cols))], [row_spec(d), row_spec(cols // 2)],
        [jax.ShapeDtypeStruct((rows, d), F32), jax.ShapeDtypeStruct((rows, cols // 2), F32)], "combine_glu")


def combine_final(x, ys, plan, g):
    rows, d = x.shape
    chunk = plan['chunk']
    return _combine_call(
        _combine_final_body, plan, x, ys, [g], [pl.BlockSpec((1, d), lambda c, *_: (0, 0))],
        pl.BlockSpec((chunk, d), lambda c, *_: (c, 0)), jax.ShapeDtypeStruct((rows, d), F32), "combine_final")


CONV_TILE = 512
CONV_HIST = 32


def _ln_swish(y, g, b):
    yc = y - jnp.mean(y, axis=-1, keepdims=True)
    yn = yc * lax.rsqrt(jnp.mean(yc * yc, axis=-1, keepdims=True) + EPS) * g + b
    return yn * jax.nn.sigmoid(yn)


CONV_ROWS = 64
LN_ROWS = 16
LN_UNROLL = 8


def _dwconv_prompt_body(u_ref, w_ref, bdw_ref, g_ref, b_ref, o_ref, ext, y_sc, *, tt):
    t = pl.program_id(1)
    n_lt = ext.shape[0]

    @pl.when(t == 0)
    def _():
        ext[:, 0:CONV_HIST, :] = jnp.zeros((n_lt, CONV_HIST, LANES), F32)

    @pl.when(t > 0)
    def _():
        ext[:, 0:CONV_HIST, :] = ext[:, tt:tt + CONV_HIST, :]

    for j in range(n_lt):
        ext[j, CONV_HIST:CONV_HIST + tt, :] = u_ref[:, j * LANES:(j + 1) * LANES]
    off = CONV_HIST - (C_KERNEL - 1)
    for j in range(n_lt):
        wj = w_ref[:, j * LANES:(j + 1) * LANES]
        bj = bdw_ref[:, j * LANES:(j + 1) * LANES]
        for c in range(tt // CONV_ROWS):
            acc = ext[j, pl.ds(off + c * CONV_ROWS, CONV_ROWS), :] * wj[0:1] + bj
            for k in range(1, C_KERNEL):
                acc = acc + ext[j, pl.ds(off + k + c * CONV_ROWS, CONV_ROWS), :] * wj[k:k + 1]
            y_sc[c * CONV_ROWS:(c + 1) * CONV_ROWS, j * LANES:(j + 1) * LANES] = acc

    def ln_rows(r, carry):
        rows = pl.ds(pl.multiple_of(r * LN_ROWS, LN_ROWS), LN_ROWS)
        o_ref[rows, :] = _ln_swish(y_sc[rows, :], g_ref[...], b_ref[...]).astype(o_ref.dtype)
        return carry
    lax.fori_loop(0, tt // LN_ROWS, ln_rows, 0, unroll=LN_UNROLL)


def dwconv_prompt(u, w, b_dw, ln_g, ln_b, bsz, seq):
    tt = min(CONV_TILE, seq)
    nt = seq // tt
    d = u.shape[1]
    const = lambda shape: pl.BlockSpec(shape, lambda b, t: (0,) * len(shape))
    return pl.pallas_call(
        functools.partial(_dwconv_prompt_body, tt=tt),
        grid=(bsz, nt),
        in_specs=[pl.BlockSpec((tt, d), lambda b, t: (b * nt + t, 0)), const((C_KERNEL, d)), const((1, d)),
                  const((1, d)), const((1, d))],
        out_specs=pl.BlockSpec((tt, d), lambda b, t: (b * nt + t, 0)),
        out_shape=jax.ShapeDtypeStruct((bsz * seq, d), BF16),
        scratch_shapes=[pltpu.VMEM((d // LANES, CONV_HIST + tt, LANES), F32), pltpu.VMEM((tt, d), F32)],
        compiler_params=_cparams("arbitrary", "arbitrary"),
        name="dwconv_prompt",
    )(u, w, b_dw, ln_g, ln_b)


def _dwconv_sample_body(u_ref, buf_ref, w_ref, bdw_ref, g_ref, b_ref, o_ref, nbuf_ref, *, tb):
    w = w_ref[...]
    rows = []
    for i in range(tb):
        hist = buf_ref[i]
        ur = u_ref[i:i + 1, :]
        rows.append(jnp.sum(hist * w[:C_KERNEL - 1], axis=0, keepdims=True) + ur * w[C_KERNEL - 1:C_KERNEL])
        nbuf_ref[i] = jnp.concatenate([hist[1:], ur], axis=0)
    y = jnp.concatenate(rows, axis=0) + bdw_ref[...]
    o_ref[...] = _ln_swish(y, g_ref[...], b_ref[...])


def dwconv_sample(u, buf, w, b_dw, ln_g, ln_b):
    nb, d = u.shape
    tb = min(SAMPLE_TILE, nb)
    const = lambda shape: pl.BlockSpec(shape, lambda i: (0,) * len(shape))
    return pl.pallas_call(
        functools.partial(_dwconv_sample_body, tb=tb),
        grid=(nb // tb,),
        in_specs=[pl.BlockSpec((tb, d), lambda i: (i, 0)), pl.BlockSpec((tb, C_KERNEL - 1, d), lambda i: (i, 0, 0)),
                  const((C_KERNEL, d)), const((1, d)), const((1, d)), const((1, d))],
        out_specs=[pl.BlockSpec((tb, d), lambda i: (i, 0)), pl.BlockSpec((tb, C_KERNEL - 1, d), lambda i: (i, 0, 0))],
        out_shape=[jax.ShapeDtypeStruct((nb, d), F32), jax.ShapeDtypeStruct(buf.shape, F32)],
        compiler_params=_cparams("parallel"),
        name="dwconv_sample",
    )(u, buf, w, b_dw, ln_g, ln_b)


def _moe(h8, lpos8, gate8, runn8, wg, wu, wd, layer, *, blk, precise):
    plan = moe_plan(lpos8, gate8, runn8, blk)
    xs = dispatch(h8, plan)
    return experts(xs, plan, wg, wu, wd, layer, precise=precise), plan


def _trunk(x, caches, p, *, prompt):
    bsz, seq, d = x.shape
    rows = bsz * seq
    precise = not prompt
    wdt = F32 if precise else BF16
    xf = x.reshape(rows, d)
    row = lambda v: v.reshape(1, -1).astype(F32)

    z = norm_proj(xf, row(p['norm_mix'][0]), p['w_in'].astype(wdt), precise=precise)
    if prompt:
        z, zb = z
        att = attn_prompt(z, zb, p['rel_table'], p['sinks'], bsz, seq)
        out_b, c1, n1, m1 = mlstm_prompt(z, zb, p['conv_w'], p['b_gates'], p['g_mnorm'], bsz, seq)
        z3 = z.reshape(bsz, seq, P_COLS)
        new_k = z3[:, seq - WINDOW:, P_KA:P_KA + A_KV].reshape(bsz, WINDOW, A_KV_HEADS, A_HEAD_DIM)
        new_v = z3[:, seq - WINDOW:, P_VA:P_VA + A_KV].reshape(bsz, WINDOW, A_KV_HEADS, A_HEAD_DIM)
        new_conv = z3[:, seq - (B_CONV - 1):, P_QK:P_QK + 2 * B_QK]
        n1 = n1[:, :B_HEADS]
        m1 = m1[:, :B_HEADS, 0]
    else:
        ck, cv, c0, n0, m0, cbuf = caches[:6]
        n_buf = ck.shape[1]
        att, out_b, new_k, new_v, c1, n1, m1, new_conv = mix_sample(
            z, p['rel_table'], p['sinks'], ck.reshape(bsz, n_buf, A_KV), cv.reshape(bsz, n_buf, A_KV),
            c0, n0, m0, cbuf, p['conv_w'], p['b_gates'], p['g_mnorm'])
        new_k = new_k.reshape(bsz, n_buf, A_KV_HEADS, A_HEAD_DIM)
        new_v = new_v.reshape(bsz, n_buf, A_KV_HEADS, A_HEAD_DIM)
        m1 = m1[:, :B_HEADS]
    w_out = p['w_out'].astype(wdt)
    x1, h8, lpos, gate, runn = proj_router([att, out_b], [w_out[:A_Q], w_out[A_Q:]], None, xf, row(p['norm_ffn'][0]),
                                           p['w_router'][0], p['b_router'][0], precise=precise)
    blk = EXPERT_BLOCK_PRECISE if precise else EXPERT_BLOCK
    ys, plan = _moe(h8, lpos, gate, runn, p['w_eg'], p['w_eu'], p['w_ed'], 0, blk=blk, precise=precise)

    x2, u = combine_glu(x1, ys, plan, row(p['norm_mix'][1]), p['w_pw1'].astype(wdt), row(p['b_pw1']),
                        precise=precise)
    if prompt:
        yc = dwconv_prompt(u, p['w_dw'], row(p['b_dw']), row(p['ln_g']), row(p['ln_b']), bsz, seq)
        new_cbuf = u.reshape(bsz, seq, d)[:, seq - (C_KERNEL - 1):]
    else:
        yc, new_cbuf = dwconv_sample(u, caches[6], p['w_dw'], row(p['b_dw']), row(p['ln_g']), row(p['ln_b']))
    x3, h8, lpos, gate, runn = proj_router([yc], [p['w_pw2'].astype(wdt)], row(p['b_pw2']), x2, row(p['norm_ffn'][1]),
                                           p['w_router'][1], p['b_router'][1], precise=precise)
    ys, plan = _moe(h8, lpos, gate, runn, p['w_eg'], p['w_eu'], p['w_ed'], 1, blk=blk, precise=precise)
    y = combine_final(x3, ys, plan, row(p['norm_final']))
    add_layer = lambda t: t[None]
    return (y.reshape(bsz, seq, d),) + tuple(add_layer(t) for t in (new_k, new_v, c1, n1, m1, new_conv, new_cbuf))


def kernel(x_prompt, x_sample, cache_win_k, cache_win_v, state_mlstm_c, state_mlstm_n, state_mlstm_m, state_mlstm_conv, state_conv, norm_mix, norm_ffn, norm_final, rel_bias_table, w_in_mix, b_mlstm_gates, w_mlstm_qk_conv, attn_sinks, g_mlstm_norm, w_out_mix, w_pw1, b_pw1, w_dw, b_dw, ln_conv_g, ln_conv_b, w_pw2, b_pw2, w_router_group, b_router_group, w_router_expert, b_router_expert, w_expert_gate, w_expert_up, w_expert_down):
    w_in = w_in_mix[0]
    s_q, s_k, s_v, s_qk, s_vb, s_g = A_Q, A_Q + A_KV, A_Q + 2 * A_KV, A_Q + 2 * A_KV + 2 * B_QK, \
        A_Q + 2 * A_KV + 2 * B_QK + B_V, A_Q + 2 * A_KV + 2 * B_QK + B_V + 2 * B_HEADS
    w_in_r = jnp.concatenate([w_in[:, :s_q], w_in[:, s_v:s_qk], w_in[:, s_qk:s_vb], w_in[:, s_g:],
                              w_in[:, s_q:s_k], w_in[:, s_k:s_v], w_in[:, s_vb:s_g],
                              jnp.zeros((D_MODEL, LANES - 2 * B_HEADS), F32)], axis=1)
    b_gates = jnp.concatenate([b_mlstm_gates[0], jnp.zeros((LANES - 2 * B_HEADS,), F32)]).reshape(1, LANES)
    depth = w_router_group.shape[0]
    w_re = jnp.transpose(w_router_expert, (0, 2, 1, 3)).reshape(depth, D_MODEL, N_EXPERTS)
    w_router = jnp.concatenate([w_router_group, w_re,
                                jnp.zeros((depth, D_MODEL, LANES - N_GROUPS - N_EXPERTS), F32)], axis=-1)
    b_router = jnp.concatenate([b_router_group, b_router_expert.reshape(depth, N_EXPERTS),
                                jnp.zeros((depth, LANES - N_GROUPS - N_EXPERTS), F32)], axis=-1)[:, None, :]
    p = dict(norm_mix=norm_mix, norm_ffn=norm_ffn, norm_final=norm_final, rel_table=rel_bias_table,
             sinks=attn_sinks[0], w_in=w_in_r, b_gates=b_gates, conv_w=w_mlstm_qk_conv[0],
             g_mnorm=g_mlstm_norm[0].reshape(1, B_V), w_out=w_out_mix[0], w_pw1=w_pw1[0], b_pw1=b_pw1[0],
             w_dw=w_dw[0], b_dw=b_dw[0], ln_g=ln_conv_g[0], ln_b=ln_conv_b[0], w_pw2=w_pw2[0], b_pw2=b_pw2[0],
             w_router=w_router, b_router=b_router, w_eg=w_expert_gate, w_eu=w_expert_up, w_ed=w_expert_down)
    caches = (cache_win_k[0], cache_win_v[0], state_mlstm_c[0], state_mlstm_n[0], state_mlstm_m[0],
              state_mlstm_conv[0], state_conv[0])
    out_p = _trunk(x_prompt, None, p, prompt=True)
    out_s = _trunk(x_sample, caches, p, prompt=False)
    return (out_p[0], out_s[0]) + out_p[1:] + out_s[1:]
```

```python
import functools
import math

import numpy as np
import jax
import jax.numpy as jnp
from jax import lax
from jax.experimental import pallas as pl
from jax.experimental.pallas import tpu as pltpu

F32 = jnp.float32
BF16 = jnp.bfloat16
I32 = jnp.int32
HI = lax.Precision.HIGHEST
NEG_INF = float("-inf")

LANES = 128
SUBLANES = 8
VMEM_LIMIT = 56 * 1024 * 1024

D_MODEL = 1024
A_HEADS = 8
A_KV_HEADS = 2
A_GROUP = A_HEADS // A_KV_HEADS
A_HEAD_DIM = 64
WINDOW = 128
REL_BUCKETS = 32
REL_MAX_DIST = 128
B_HEADS = 4
B_DK = 64
B_DV = 128
B_CONV = 4
C_KERNEL = 31
N_GROUPS = 4
EXPERTS_PER_GROUP = 8
N_EXPERTS = N_GROUPS * EXPERTS_PER_GROUP
TOP_K = 2
EXPERT_FF = D_MODEL // 2
EXPERT_BLOCK = 512
EXPERT_BLOCK_PRECISE = 128
EPS = 1e-6

A_Q = A_HEADS * A_HEAD_DIM
A_KV = A_KV_HEADS * A_HEAD_DIM
B_QK = B_HEADS * B_DK
B_V = B_HEADS * B_DV
ROW_TILES = D_MODEL // LANES

Z_QA, Z_QK, Z_VB, Z_OG, Z_KA, Z_VA, Z_GATES = 0, 512, 1024, 1536, 2048, 2176, 2304
Z_COLS = 2432
MLSTM_CHUNK = 128


def _cparams(*sem):
    return pltpu.CompilerParams(dimension_semantics=sem, vmem_limit_bytes=VMEM_LIMIT)


def _rms(x, g):
    return x * lax.rsqrt(jnp.mean(x * x, axis=-1, keepdims=True) + EPS) * g


def _mm(a, w, precise):
    if not precise:
        return jnp.dot(a.astype(BF16), w, preferred_element_type=F32)
    a = a.astype(F32)
    a_hi = a.astype(BF16)
    a_lo = (a - a_hi.astype(F32)).astype(BF16)
    w_hi = w.astype(BF16)
    w_lo = (w - w_hi.astype(F32)).astype(BF16)
    return (jnp.dot(a_hi, w_hi, preferred_element_type=F32) + jnp.dot(a_lo, w_hi, preferred_element_type=F32)
            + jnp.dot(a_hi, w_lo, preferred_element_type=F32))


def _t5_buckets(dist):
    exact = REL_BUCKETS // 2
    d = np.maximum(dist, 0)
    large = exact + (np.log(np.maximum(d, 1).astype(np.float32) / exact)
                     / math.log(REL_MAX_DIST / exact) * (REL_BUCKETS - exact)).astype(np.int32)
    return np.where(d < exact, d, np.minimum(large, REL_BUCKETS - 1)).astype(np.int32)


P_QK, P_OG, P_KA, P_VA, P_GATES = 0, 512, 1024, 1152, 1280
P_COLS = 1408
PB_QA, PB_VB = 0, 512
PB_COLS = 1024


def _norm_proj_body(x_ref, g_ref, w_ref, *o_refs, precise):
    h = _rms(x_ref[...], g_ref[...])
    z = _mm(h, w_ref[...], precise)
    if len(o_refs) == 1:
        o_refs[0][...] = z
        return
    z32_ref, zb_ref = o_refs
    z32_ref[...] = jnp.concatenate([z[:, Z_QK:Z_QK + 2 * B_QK], z[:, Z_OG:Z_OG + B_V], z[:, Z_KA:Z_GATES + LANES]],
                                   axis=1)
    zb_ref[...] = jnp.concatenate([z[:, Z_QA:Z_QA + A_Q], z[:, Z_VB:Z_VB + B_V]], axis=1).astype(BF16)


def norm_proj(x, g, w, *, precise):
    rows, d = x.shape
    cols = w.shape[1]
    tm = min(rows, 128 if precise else 512)
    row_spec = lambda width: pl.BlockSpec((tm, width), lambda i: (i, 0))
    if precise:
        out_specs, out_shape = row_spec(cols), jax.ShapeDtypeStruct((rows, cols), F32)
    else:
        out_specs = [row_spec(P_COLS), row_spec(PB_COLS)]
        out_shape = [jax.ShapeDtypeStruct((rows, P_COLS), F32), jax.ShapeDtypeStruct((rows, PB_COLS), BF16)]
    return pl.pallas_call(
        functools.partial(_norm_proj_body, precise=precise),
        grid=(rows // tm,),
        in_specs=[row_spec(d), pl.BlockSpec((1, d), lambda i: (0, 0)), pl.BlockSpec((d, cols), lambda i: (0, 0))],
        out_specs=out_specs,
        out_shape=out_shape,
        compiler_params=_cparams("parallel"),
        name="norm_proj",
    )(x, g, w)


def _attn_prompt_body(tab_ref, sink_ref, bkt_ref, q_ref, kp_ref, kc_ref, vp_ref, vc_ref, o_ref, bias_ref, *, nq):
    b = pl.program_id(0)
    n = pl.program_id(1)

    @pl.when((b == 0) & (n == 0))
    def _():
        bk = bkt_ref[...]
        first = lax.broadcasted_iota(I32, bk.shape, 1) >= WINDOW
        for h in range(A_HEADS):
            acc = jnp.full(bk.shape, NEG_INF, F32)
            for t in range(REL_BUCKETS):
                acc = jnp.where(bk == t, tab_ref[t, h], acc)
            bias_ref[h] = acc
            bias_ref[A_HEADS + h] = jnp.where(first, acc, NEG_INF)

    k_all = jnp.concatenate([kp_ref[...], kc_ref[...]], axis=0).astype(BF16)
    v_all = jnp.concatenate([vp_ref[...], vc_ref[...]], axis=0).astype(BF16)
    for sub in range(nq):
        rows = slice(sub * WINDOW, (sub + 1) * WINDOW)
        q = q_ref[rows, :] * (A_HEAD_DIM ** -0.5)
        kb = k_all[sub * WINDOW:(sub + 2) * WINDOW]
        vb = v_all[sub * WINDOW:(sub + 2) * WINDOW]
        table = jnp.where(n == 0, A_HEADS, 0) if sub == 0 else 0
        H = range(A_HEADS)
        nt = (((1,), (1,)), ((), ()))
        kv = [slice((h // A_GROUP) * A_HEAD_DIM, (h // A_GROUP + 1) * A_HEAD_DIM) for h in H]
        s = [lax.dot_general(q[:, h * A_HEAD_DIM:(h + 1) * A_HEAD_DIM].astype(BF16), kb[:, kv[h]], nt,
                             preferred_element_type=F32) + bias_ref[table + h] for h in H]
        mx = [jnp.maximum(jnp.max(s[h], axis=-1, keepdims=True), sink_ref[h]) for h in H]
        p = [jnp.exp(s[h] - mx[h]) for h in H]
        den = [jnp.sum(p[h], axis=-1, keepdims=True) + jnp.exp(sink_ref[h] - mx[h]) for h in H]
        outs = [jnp.dot(p[h].astype(BF16), vb[:, kv[h]], preferred_element_type=F32) / den[h] for h in H]
        o_ref[rows, :] = jnp.concatenate(outs, axis=1).astype(o_ref.dtype)


ATTN_BLOCKS = 2


def attn_prompt(z, zb, rel_table, sinks, bsz, seq):
    nb = seq // WINDOW
    dist = WINDOW + np.arange(WINDOW)[:, None] - np.arange(2 * WINDOW)[None, :]
    bkt = np.where((dist >= 0) & (dist <= WINDOW), _t5_buckets(dist), -1).astype(np.int32)
    kcol, vcol = P_KA // LANES, P_VA // LANES
    smem = pl.BlockSpec(memory_space=pltpu.SMEM)
    nq = ATTN_BLOCKS if nb % ATTN_BLOCKS == 0 else 1
    ns = nb // nq

    def cur(c):
        return pl.BlockSpec((nq * WINDOW, LANES), lambda b, n: (b * ns + n, c))

    def prev(c):
        return pl.BlockSpec((WINDOW, LANES), lambda b, n: (b * nb + jnp.maximum(n * nq - 1, 0), c))

    return pl.pallas_call(
        functools.partial(_attn_prompt_body, nq=nq),
        grid=(bsz, ns),
        in_specs=[smem, smem,
                  pl.BlockSpec((WINDOW, 2 * WINDOW), lambda b, n: (0, 0)),
                  pl.BlockSpec((nq * WINDOW, A_Q), lambda b, n: (b * ns + n, PB_QA // A_Q)),
                  prev(kcol), cur(kcol), prev(vcol), cur(vcol)],
        out_specs=pl.BlockSpec((nq * WINDOW, A_Q), lambda b, n: (b * ns + n, 0)),
        out_shape=jax.ShapeDtypeStruct((bsz * seq, A_Q), BF16),
        scratch_shapes=[pltpu.VMEM((2 * A_HEADS, WINDOW, 2 * WINDOW), F32)],
        compiler_params=_cparams("arbitrary", "arbitrary"),
        name="attn_prompt",
    )(rel_table, sinks, jnp.asarray(bkt), zb, z, z, z, z)


def _log_sigmoid(x):
    return -(jnp.maximum(-x, 0.0) + jnp.log1p(jnp.exp(-jnp.abs(x))))


def _mlstm_prompt_body(qk_ref, v_ref, g_ref, og_ref, cw_ref, bg_ref, gn_ref,
                       ob_ref, c_out, n_out, m_out,
                       c_sc, n_sc, m_sc, hist_sc, *, L, nc, nbat, nsub):
    ci = pl.program_id(1)

    @pl.when(ci == 0)
    def _():
        c_sc[...] = jnp.zeros(c_sc.shape, F32)
        n_sc[...] = jnp.zeros(n_sc.shape, F32)
        m_sc[...] = jnp.full(m_sc.shape, NEG_INF, F32)
        hist_sc[...] = jnp.zeros(hist_sc.shape, F32)

    row = lax.broadcasted_iota(I32, (L, L), 0)
    colm = lax.broadcasted_iota(I32, (L, L), 1)
    causal = colm <= row
    tril = causal.astype(F32)
    for sub in range(nsub):
        rows = pl.ds(sub * L, L)
        stores = []
        for bb in range(nbat):
            stores += _mlstm_chunk(qk_ref.at[bb, rows], v_ref.at[bb, rows], g_ref.at[bb, rows], og_ref.at[bb, rows],
                                   cw_ref, bg_ref, gn_ref, ob_ref.at[bb, rows], c_sc.at[bb], n_sc.at[bb],
                                   m_sc.at[bb], hist_sc.at[bb], causal, tril, L)
        for store in stores:
            store()

    @pl.when(ci == nc - 1)
    def _():
        c_out[...] = c_sc[...]
        n_out[...] = n_sc[...]
        m_out[...] = m_sc[...]


def _mlstm_chunk(qk_ref, v_ref, g_ref, og_ref, cw_ref, bg_ref, gn_ref, ob_ref, c_sc, n_sc, m_sc, hist_sc,
                 causal, tril, L):
    cur = qk_ref[...]
    ext = jnp.concatenate([hist_sc[...], cur], axis=0)
    cw = cw_ref[...]
    off = SUBLANES - (B_CONV - 1)
    conv = ext[off:off + L] * cw[0:1]
    for j in range(1, B_CONV):
        conv = conv + ext[off + j:off + j + L] * cw[j:j + 1]
    qk = conv * jax.nn.sigmoid(conv)
    q_all = qk[:, :B_QK]
    k_all = qk[:, B_QK:] * (B_DK ** -0.5)
    k_t = k_all.T
    v_all = v_ref[...]
    og = og_ref[...]
    gn = gn_ref[...]

    G = g_ref[...] + bg_ref[...]
    lf = _log_sigmoid(G)
    Bc = jnp.dot(tril, lf, precision=HI, preferred_element_type=F32)
    BT = Bc.T
    GT = G.T

    H = range(B_HEADS)
    nt = (((1,), (1,)), ((), ()))
    qh = [q_all[:, h * B_DK:(h + 1) * B_DK] for h in H]
    kh = [k_all[:, h * B_DK:(h + 1) * B_DK] for h in H]
    qb = [q.astype(BF16) for q in qh]
    vb = [v_all[:, h * B_DV:(h + 1) * B_DV].astype(BF16) for h in H]
    b_col = [Bc[:, B_HEADS + h:B_HEADS + h + 1] for h in H]
    c0 = [c_sc[h] for h in H]
    n0 = [n_sc[h:h + 1, :] for h in H]
    a = [b_col[h] + m_sc[h:h + 1, 0:1] for h in H]
    d = [jnp.where(causal, b_col[h] - BT[B_HEADS + h:B_HEADS + h + 1, :] + GT[h:h + 1, :], NEG_INF) for h in H]
    qk_raw = [lax.dot_general(qb[h], kh[h].astype(BF16), nt, preferred_element_type=F32) for h in H]
    qc = [jnp.dot(qb[h], c0[h].astype(BF16), preferred_element_type=F32) for h in H]
    m = [jnp.maximum(a[h], jnp.max(d[h], axis=-1, keepdims=True)) for h in H]
    dw = [jnp.exp(d[h] - m[h]) for h in H]
    aw = [jnp.exp(a[h] - m[h]) for h in H]
    s = [qk_raw[h] * dw[h] for h in H]
    sv = [jnp.dot(s[h].astype(BF16), vb[h], preferred_element_type=F32) for h in H]
    li = lax.broadcasted_iota(I32, (B_QK, LANES), 0) >> (B_DK.bit_length() - 1)
    ind = (li == lax.broadcasted_iota(I32, (B_QK, LANES), 1)).astype(F32)
    qn = jnp.dot(q_all * jnp.concatenate(n0, axis=1), ind, precision=HI, preferred_element_type=F32)
    den = [jnp.sum(s[h], axis=-1, keepdims=True) + aw[h] * qn[:, h:h + 1] for h in H]
    hh = [(sv[h] + aw[h] * qc[h]) / jnp.maximum(jnp.abs(den[h]), jnp.exp(-m[h])) for h in H]
    m_last = [m[h][L - 1:L, :] for h in H]
    wl = [jnp.exp(b_col[h][L - 1:L, :] - b_col[h] + G[:, h:h + 1] - m_last[h]) for h in H]
    decay = [aw[h][L - 1:L, :] for h in H]
    kw_t = [(k_t[h * B_DK:(h + 1) * B_DK, :] * dw[h][L - 1:L, :]).astype(BF16) for h in H]
    c1 = [decay[h] * c0[h] + jnp.dot(kw_t[h], vb[h], preferred_element_type=F32) for h in H]
    n1 = [decay[h] * n0[h] + jnp.sum(kh[h] * wl[h], axis=0, keepdims=True) for h in H]
    stores = [functools.partial(_store_state, c_sc, n_sc, m_sc, h, c1[h], n1[h], m_last[h]) for h in H]
    hn = [hh[h] * lax.rsqrt(jnp.mean(hh[h] * hh[h], axis=-1, keepdims=True) + EPS) * gn[:, h * B_DV:(h + 1) * B_DV]
          for h in H]
    outs = [jax.nn.sigmoid(og[:, h * B_DV:(h + 1) * B_DV]) * hn[h] for h in H]
    out = jnp.concatenate(outs, axis=1).astype(ob_ref.dtype)
    stores.append(functools.partial(_store_chunk, ob_ref, hist_sc, out, cur[L - SUBLANES:L]))
    return stores


def _store_state(c_sc, n_sc, m_sc, h, c1, n1, m_last):
    c_sc[h] = c1
    n_sc[h:h + 1, :] = n1
    m_sc[h:h + 1, :] = jnp.broadcast_to(m_last, (1, LANES))


def _store_chunk(ob_ref, hist_sc, out, tail):
    ob_ref[...] = out
    hist_sc[...] = tail


MLSTM_BATCH = 1
MLSTM_SUBCHUNKS = 1


def mlstm_prompt(z, zb, conv_w, b_gates_pad, g_norm, bsz, seq):
    L = MLSTM_CHUNK
    nsub = MLSTM_SUBCHUNKS if (seq // L) % MLSTM_SUBCHUNKS == 0 else 1
    nc = seq // (L * nsub)
    nbat = MLSTM_BATCH if bsz % MLSTM_BATCH == 0 else 1
    z3 = z.reshape(bsz, seq, P_COLS)
    zb3 = zb.reshape(bsz, seq, PB_COLS)

    def zspec(width, colblk):
        return pl.BlockSpec((nbat, nsub * L, width), lambda b, c: (b, c, colblk))

    const = lambda shape: pl.BlockSpec(shape, lambda b, c: (0,) * len(shape))
    state = lambda shape: pl.BlockSpec((nbat,) + shape, lambda b, c: (b,) + (0,) * len(shape))
    out_b, c1, n1, m1 = pl.pallas_call(
        functools.partial(_mlstm_prompt_body, L=L, nc=nc, nbat=nbat, nsub=nsub),
        grid=(bsz // nbat, nc),
        in_specs=[zspec(2 * B_QK, P_QK // (2 * B_QK)), zspec(B_V, PB_VB // B_V), zspec(LANES, P_GATES // LANES),
                  zspec(B_V, P_OG // B_V), const((B_CONV, 2 * B_QK)), const((1, LANES)), const((1, B_V))],
        out_specs=[pl.BlockSpec((nbat, nsub * L, B_V), lambda b, c: (b, c, 0)),
                   state((B_HEADS, B_DK, B_DV)), state((SUBLANES, B_DK)), state((SUBLANES, LANES))],
        out_shape=[jax.ShapeDtypeStruct((bsz, seq, B_V), BF16),
                   jax.ShapeDtypeStruct((bsz, B_HEADS, B_DK, B_DV), F32),
                   jax.ShapeDtypeStruct((bsz, SUBLANES, B_DK), F32),
                   jax.ShapeDtypeStruct((bsz, SUBLANES, LANES), F32)],
        scratch_shapes=[pltpu.VMEM((nbat, B_HEADS, B_DK, B_DV), F32), pltpu.VMEM((nbat, SUBLANES, B_DK), F32),
                        pltpu.VMEM((nbat, SUBLANES, LANES), F32), pltpu.VMEM((nbat, SUBLANES, 2 * B_QK), F32)],
        compiler_params=_cparams("arbitrary", "arbitrary"),
        name="mlstm_prompt",
    )(z3, zb3, z3, z3, conv_w, b_gates_pad, g_norm)
    return out_b.reshape(bsz * seq, B_V), c1, n1, m1


SAMPLE_TILE = 8


def _mix_sample_body(tab_ref, sink_ref, bkt_ref, z_ref, ck_ref, cv_ref, c0_ref, n0_ref, m0_ref, cb_ref,
                     cw_ref, bg_ref, gn_ref,
                     att_ref, ob_ref, nk_ref, nv_ref, c1_ref, n1_ref, m1_ref, ncb_ref, *, tb):
    bk = bkt_ref[...]
    cw = cw_ref[...]
    gn = gn_ref[...]
    bias_rows = []
    for h in range(A_HEADS):
        bias = jnp.zeros(bk.shape, F32)
        for t in range(REL_BUCKETS):
            bias = jnp.where(bk == t, tab_ref[t, h], bias)
        bias_rows.append(bias)
    n_keys = bk.shape[1]
    rows, cols = A_HEADS * tb, tb * n_keys
    z_all = z_ref[...]
    q_all = z_all[:, Z_QA:Z_QA + A_Q] * (A_HEAD_DIM ** -0.5)
    ka_all = z_all[:, Z_KA:Z_KA + A_KV]
    va_all = z_all[:, Z_VA:Z_VA + A_KV]
    zero_half = jnp.zeros((tb, A_HEAD_DIM), F32)
    q_rows = []
    for h in range(A_HEADS):
        qh = q_all[:, h * A_HEAD_DIM:(h + 1) * A_HEAD_DIM]
        q_rows.append(jnp.concatenate([qh, zero_half] if h // A_GROUP == 0 else [zero_half, qh], axis=1))
    qm = jnp.concatenate(q_rows, axis=0)
    ka_rep = jnp.concatenate([ka_all] * A_HEADS, axis=0)
    va_rep = jnp.concatenate([va_all] * A_HEADS, axis=0)
    k_cat = jnp.concatenate([ck_ref[i] for i in range(tb)], axis=0)
    v_cat = jnp.concatenate([cv_ref[i] for i in range(tb)], axis=0)
    bias_c = jnp.concatenate([jnp.broadcast_to(b, (tb, n_keys)) for b in bias_rows], axis=0)
    bias_c = jnp.concatenate([bias_c] * tb, axis=1)
    r_id = lax.broadcasted_iota(I32, (rows, cols), 0)
    c_id = lax.broadcasted_iota(I32, (rows, cols), 1)
    own = (r_id & (tb - 1)) == (c_id >> (n_keys.bit_length() - 1))
    head_id = lax.broadcasted_iota(I32, (rows, 1), 0) >> (tb.bit_length() - 1)
    bias_n = jnp.zeros((rows, 1), F32)
    sinks = jnp.zeros((rows, 1), F32)
    for h in range(A_HEADS):
        bias_n = jnp.where(head_id == h, tab_ref[0, h], bias_n)
        sinks = jnp.where(head_id == h, sink_ref[h], sinks)
    lc = lax.dot_general(qm, k_cat, (((1,), (1,)), ((), ())), precision=HI, preferred_element_type=F32)
    lc = jnp.where(own, lc + bias_c, NEG_INF)
    ln = jnp.sum(qm * ka_rep, axis=-1, keepdims=True) + bias_n
    mx = jnp.maximum(jnp.maximum(jnp.max(lc, axis=-1, keepdims=True), ln), sinks)
    pc = jnp.exp(lc - mx)
    pn = jnp.exp(ln - mx)
    den = jnp.sum(pc, axis=-1, keepdims=True) + pn + jnp.exp(sinks - mx)
    o = (jnp.dot(pc, v_cat, precision=HI, preferred_element_type=F32) + pn * va_rep) / den
    att_ref[...] = jnp.concatenate(
        [o[h * tb:(h + 1) * tb, (h // A_GROUP) * A_HEAD_DIM:(h // A_GROUP + 1) * A_HEAD_DIM]
         for h in range(A_HEADS)], axis=1)

    for i in range(tb):
        nk_ref[i] = jnp.concatenate([ck_ref[i][1:], ka_all[i:i + 1]], axis=0)
        nv_ref[i] = jnp.concatenate([cv_ref[i][1:], va_all[i:i + 1]], axis=0)

    qk_pre = z_all[:, Z_QK:Z_QK + 2 * B_QK]
    hist = [cb_ref[i] for i in range(tb)]
    conv = qk_pre * cw[B_CONV - 1:B_CONV]
    for j in range(B_CONV - 1):
        conv = conv + jnp.concatenate([hs[j:j + 1] for hs in hist], axis=0) * cw[j:j + 1]
    for i in range(tb):
        ncb_ref[i] = jnp.concatenate([hist[i][1:], qk_pre[i:i + 1]], axis=0)
    qk = conv * jax.nn.sigmoid(conv)
    q_m = qk[:, :B_QK]
    k_m = qk[:, B_QK:] * (B_DK ** -0.5)
    v_m = z_all[:, Z_VB:Z_VB + B_V]
    og = z_all[:, Z_OG:Z_OG + B_V]
    G = z_all[:, Z_GATES:Z_GATES + LANES] + bg_ref[...]
    ig = G[:, :B_HEADS]
    a = _log_sigmoid(G)[:, B_HEADS:2 * B_HEADS] + m0_ref[...]
    m = jnp.maximum(a, ig)
    dw = jnp.exp(ig - m)
    aw = jnp.exp(a - m)
    li = lax.broadcasted_iota(I32, (B_QK, LANES), 0) >> (B_DK.bit_length() - 1)
    ind = (li == lax.broadcasted_iota(I32, (B_QK, LANES), 1)).astype(F32)
    n0_m = jnp.concatenate([jnp.concatenate([n0_ref[i, h:h + 1, :] for h in range(B_HEADS)], axis=1)
                            for i in range(tb)], axis=0)
    qk_dot = jnp.dot(q_m * k_m, ind, precision=HI, preferred_element_type=F32)[:, :B_HEADS]
    qn_dot = jnp.dot(q_m * n0_m, ind, precision=HI, preferred_element_type=F32)[:, :B_HEADS]
    s = qk_dot * dw
    inv = 1.0 / jnp.maximum(jnp.abs(s + aw * qn_dot), jnp.exp(-m))
    m1_ref[...] = jnp.concatenate([m, jnp.zeros((tb, LANES - B_HEADS), F32)], axis=1)
    stack = jnp.concatenate([q_m[:, h * B_DK:(h + 1) * B_DK] for h in range(B_HEADS)]
                            + [k_m[:, h * B_DK:(h + 1) * B_DK] for h in range(B_HEADS)], axis=0)
    cols = jnp.concatenate([stack, jnp.zeros_like(stack)], axis=1).T[:B_DK]
    hh_rows = [[] for _ in range(B_HEADS)]
    for i in range(tb):
        for h in range(B_HEADS):
            jq = h * tb + i
            jk = (B_HEADS + h) * tb + i
            c0 = c0_ref[i, h]
            vh = v_m[i:i + 1, h * B_DV:(h + 1) * B_DV]
            aw_p, dw_p = aw[i:i + 1, h:h + 1], dw[i:i + 1, h:h + 1]
            num = s[i:i + 1, h:h + 1] * vh + aw_p * jnp.sum(cols[:, jq:jq + 1] * c0, axis=0, keepdims=True)
            hh_rows[h].append(num * inv[i:i + 1, h:h + 1])
            c1_ref[i, h] = aw_p * c0 + dw_p * (cols[:, jk:jk + 1] * vh)
            n1_ref[i, h:h + 1, :] = aw_p * n0_ref[i, h:h + 1, :] + dw_p * k_m[i:i + 1, h * B_DK:(h + 1) * B_DK]
    obs = []
    for h in range(B_HEADS):
        hh = jnp.concatenate(hh_rows[h], axis=0)
        hn = hh * lax.rsqrt(jnp.mean(hh * hh, axis=-1, keepdims=True) + EPS) * gn[:, h * B_DV:(h + 1) * B_DV]
        obs.append(jax.nn.sigmoid(og[:, h * B_DV:(h + 1) * B_DV]) * hn)
    ob_ref[...] = jnp.concatenate(obs, axis=1)


def mix_sample(z, rel_table, sinks, ck, cv, c0, n0, m0, conv_buf, conv_w, b_gates_pad, g_norm):
    nb = z.shape[0]
    tb = min(SAMPLE_TILE, nb)
    n_buf = ck.shape[1]
    bkt = _t5_buckets(n_buf - np.arange(n_buf))[None, :]
    smem = pl.BlockSpec(memory_space=pltpu.SMEM)
    const = lambda shape: pl.BlockSpec(shape, lambda i: (0,) * len(shape))
    lead = lambda shape: pl.BlockSpec((tb,) + shape, lambda i: (i,) + (0,) * len(shape))
    return pl.pallas_call(
        functools.partial(_mix_sample_body, tb=tb),
        grid=(nb // tb,),
        in_specs=[smem, smem, const((1, n_buf)), lead((Z_COLS,)), lead((n_buf, A_KV)), lead((n_buf, A_KV)),
                  lead((B_HEADS, B_DK, B_DV)), lead((B_HEADS, B_DK)), lead((B_HEADS,)),
                  lead((B_CONV - 1, 2 * B_QK)), const((B_CONV, 2 * B_QK)), const((1, LANES)), const((1, B_V))],
        out_specs=[lead((A_Q,)), lead((B_V,)), lead((n_buf, A_KV)), lead((n_buf, A_KV)),
                   lead((B_HEADS, B_DK, B_DV)), lead((B_HEADS, B_DK)), lead((LANES,)),
                   lead((B_CONV - 1, 2 * B_QK))],
        out_shape=[jax.ShapeDtypeStruct((nb, A_Q), F32), jax.ShapeDtypeStruct((nb, B_V), F32),
                   jax.ShapeDtypeStruct(ck.shape, F32), jax.ShapeDtypeStruct(cv.shape, F32),
                   jax.ShapeDtypeStruct(c0.shape, F32), jax.ShapeDtypeStruct(n0.shape, F32),
                   jax.ShapeDtypeStruct((nb, LANES), F32), jax.ShapeDtypeStruct(conv_buf.shape, F32)],
        compiler_params=_cparams("parallel"),
        name="mix_sample",
    )(rel_table, sinks, jnp.asarray(bkt), z, ck, cv, c0, n0, m0, conv_buf, conv_w, b_gates_pad, g_norm)


def _store_row_tiles(ref, val, rows):
    for s in range(ROW_TILES):
        ref[pl.ds(s, rows, stride=ROW_TILES), :] = val[:, s * LANES:(s + 1) * LANES]


def _load_row_tiles(ref, rows, start=0, stride=ROW_TILES):
    return jnp.concatenate([ref[pl.ds(start + s, rows, stride=stride), :] for s in range(ROW_TILES)], axis=1)


def _route_sort(logits):
    tm = logits.shape[0]
    lt = logits.T
    big = jnp.int32(1 << 20)
    n_rows = SUBLANES * ((N_GROUPS + N_EXPERTS + SUBLANES - 1) // SUBLANES)
    row = lax.broadcasted_iota(I32, (n_rows, tm), 0)
    x = lt[:n_rows]
    gl = jnp.where(row < N_GROUPS, x, NEG_INF)
    gmax = jnp.max(gl, axis=0, keepdims=True)
    gidx = jnp.min(jnp.where(gl == gmax, row, big), axis=0, keepdims=True)
    g_gate = 1.0 / jnp.sum(jnp.exp(gl - gmax), axis=0, keepdims=True)
    lo = N_GROUPS + gidx * EXPERTS_PER_GROUP
    el = jnp.where((row >= lo) & (row < lo + EXPERTS_PER_GROUP), x, NEG_INF)
    v1 = jnp.max(el, axis=0, keepdims=True)
    i1 = jnp.min(jnp.where(el == v1, row, big), axis=0, keepdims=True)
    el2 = jnp.where(row == i1, NEG_INF, el)
    v2 = jnp.max(el2, axis=0, keepdims=True)
    i2 = jnp.min(jnp.where(el2 == v2, row, big), axis=0, keepdims=True)
    t = jnp.exp(v2 - v1)
    w1 = g_gate / (1.0 + t)
    w2 = g_gate * t / (1.0 + t)
    e0 = i1 - N_GROUPS
    e1 = i2 - N_GROUPS

    erow = lax.broadcasted_iota(I32, (N_EXPERTS, tm), 0)
    oh0 = erow == e0
    oh1 = erow == e1
    r = lax.broadcasted_iota(I32, (tm, tm), 0)
    c = lax.broadcasted_iota(I32, (tm, tm), 1)
    earlier = (r < c).astype(BF16)
    rank0 = jnp.dot(oh0.astype(BF16), earlier, preferred_element_type=F32)
    rank1 = jnp.dot(oh1.astype(BF16), earlier, preferred_element_type=F32)
    cnt0 = jnp.sum(oh0.astype(F32), axis=1, keepdims=True)
    run_n = cnt0 + jnp.sum(oh1.astype(F32), axis=1, keepdims=True)
    slot_n = jnp.floor((run_n + (RUN_SLOT - 1)) * (1.0 / RUN_SLOT)) * RUN_SLOT
    er = lax.broadcasted_iota(I32, (N_EXPERTS, N_EXPERTS), 0)
    ec = lax.broadcasted_iota(I32, (N_EXPERTS, N_EXPERTS), 1)
    run_l = jnp.dot((ec < er).astype(F32), jnp.broadcast_to(slot_n, (N_EXPERTS, LANES)), precision=HI,
                    preferred_element_type=F32)[:, 0:1]
    p0 = jnp.sum(jnp.where(oh0, rank0 + run_l, 0.0), axis=0, keepdims=True)
    p1 = jnp.sum(jnp.where(oh1, rank1 + run_l + cnt0, 0.0), axis=0, keepdims=True)
    pad = jnp.zeros((SUBLANES - TOP_K, tm), F32)
    lane_e = lax.broadcasted_iota(I32, (N_EXPERTS, LANES), 1)
    run_row = jnp.sum(jnp.where(lane_e == lax.broadcasted_iota(I32, (N_EXPERTS, LANES), 0),
                                jnp.broadcast_to(run_n, (N_EXPERTS, LANES)), 0.0), axis=0, keepdims=True)
    return jnp.concatenate([p0, p1, pad], axis=0), jnp.concatenate([w1, w2, pad], axis=0), run_row


MOE_CHUNK = 512
RUN_SLOT = 32


def _moe_chunk(rows, precise):
    return min(rows, 128 if precise else MOE_CHUNK)


def _proj_router_body(*refs, n_in, has_bias, precise, tm):
    a_refs = refs[:n_in]
    w_refs = refs[n_in:2 * n_in]
    k = 2 * n_in
    bias_ref = refs[k] if has_bias else None
    k += 1 if has_bias else 0
    x_ref, g_ref, wr_ref, br_ref, x1_ref, h8_ref, lpos_ref, gate_ref, runn_ref = refs[k:]
    acc = x_ref[...]
    if has_bias:
        acc = acc + bias_ref[...]
    for a_ref, w_ref in zip(a_refs, w_refs):
        acc = acc + _mm(a_ref[...], w_ref[...], precise)
    x1_ref[...] = acc
    h = _rms(acc, g_ref[...])
    _store_row_tiles(h8_ref, h, tm)
    wr = wr_ref[...]
    if precise:
        logits = jnp.dot(h, wr, precision=HI, preferred_element_type=F32)
    else:
        h_hi = h.astype(BF16)
        h_lo = (h - h_hi.astype(F32)).astype(BF16)
        w_hi = wr.astype(BF16)
        w_lo = (wr - w_hi.astype(F32)).astype(BF16)
        logits = (jnp.dot(h_hi, w_hi, preferred_element_type=F32) + jnp.dot(h_lo, w_hi, preferred_element_type=F32)
                  + jnp.dot(h_hi, w_lo, preferred_element_type=F32))
    lpos, gate, run_n = _route_sort(logits + br_ref[...])
    lpos_ref[...] = (lpos * ROW_TILES).astype(I32)
    gate_ref[...] = gate
    runn_ref[...] = jnp.broadcast_to(run_n, runn_ref.shape).astype(I32)


def proj_router(a_list, w_list, bias, x, g, wr, br, *, precise):
    rows, d = x.shape
    tm = _moe_chunk(rows, precise)
    n_in = len(a_list)
    row_spec = lambda width: pl.BlockSpec((tm, width), lambda i: (i, 0))
    const = lambda shape: pl.BlockSpec(shape, lambda i: (0,) * len(shape))
    in_specs = [row_spec(a.shape[1]) for a in a_list] + [const(w.shape) for w in w_list]
    args = list(a_list) + list(w_list)
    if bias is not None:
        in_specs.append(const((1, d)))
        args.append(bias)
    in_specs += [row_spec(d), const((1, d)), const((d, LANES)), const((1, LANES))]
    args += [x, g, wr, br]
    return pl.pallas_call(
        functools.partial(_proj_router_body, n_in=n_in, has_bias=bias is not None, precise=precise, tm=tm),
        grid=(rows // tm,),
        in_specs=in_specs,
        out_specs=[row_spec(d), pl.BlockSpec((tm * ROW_TILES, LANES), lambda i: (i, 0)),
                   pl.BlockSpec((SUBLANES, tm), lambda i: (i, 0)), pl.BlockSpec((SUBLANES, tm), lambda i: (i, 0)),
                   pl.BlockSpec((SUBLANES, LANES), lambda i: (i, 0))],
        out_shape=[jax.ShapeDtypeStruct((rows, d), F32), jax.ShapeDtypeStruct((rows * ROW_TILES, LANES), F32),
                   jax.ShapeDtypeStruct((rows // tm * SUBLANES, tm), I32),
                   jax.ShapeDtypeStruct((rows // tm * SUBLANES, tm), F32),
                   jax.ShapeDtypeStruct((rows // tm * SUBLANES, LANES), I32)],
        compiler_params=_cparams("parallel"),
        name="proj_router",
    )(*args)


def _rows_at(offset):
    return pl.ds(pl.multiple_of(offset, ROW_TILES), ROW_TILES)


def _tile_rows(r, n=1):
    return pl.ds(pl.multiple_of(r * ROW_TILES, ROW_TILES), n * ROW_TILES)


def _pow2_pieces(limit):
    p = 1
    while p * 2 <= limit:
        p *= 2
    out = []
    while p >= 1:
        out.append(p)
        p //= 2
    return out


COMMON_PIECE = 32


def _for_each_piece(n, pieces, fn):
    def emit(ps):
        for p in ps:
            @pl.when((n & p) != 0)
            def _(p=p):
                fn(n & ~(2 * p - 1), p)

    big = [p for p in pieces if p > COMMON_PIECE]
    if big:
        @pl.when(n > 2 * COMMON_PIECE - 1)
        def _():
            emit(big)
    emit([p for p in pieces if p <= COMMON_PIECE])


def _piece_rows(first_row, i):
    return _tile_rows(first_row + i * RUN_SLOT, RUN_SLOT)


def _dispatch_body(rg_ref, sl_ref, np_ref, npc_ref, ps_ref, pn_ref, tail_ref, lpos_ref, h8_ref, xs_hbm,
                   stage, zbuf, sem, zsem, *, chunk, nch, blk):
    c = pl.program_id(0)
    slot = c % 2
    pad_pieces = _pow2_pieces(blk + RUN_SLOT - 1)

    def wait_pieces(count, buf):
        def one(i, carry):
            pltpu.make_async_copy(stage.at[buf, _tile_rows(0, RUN_SLOT)], stage.at[buf, _tile_rows(0, RUN_SLOT)],
                                  sem).wait()
            return carry
        lax.fori_loop(0, count, one, 0)

    @pl.when(c == 0)
    def _():
        stage[...] = jnp.zeros(stage.shape, F32)

    def copy_tok(t, carry):
        row = h8_ref[_tile_rows(t), :]
        stage[slot, _rows_at(lpos_ref[0, 0, t]), :] = row
        stage[slot, _rows_at(lpos_ref[0, 0, chunk + t]), :] = row
        return carry
    lax.fori_loop(0, chunk, copy_tok, 0, unroll=8)

    @pl.when(c > 0)
    def _():
        wait_pieces(npc_ref[jnp.maximum(c - 1, 0)], 1 - slot)

    def send_runs(e, carry):
        k = c * N_EXPERTS + e

        def one(i, carry2):
            pltpu.make_async_copy(stage.at[slot, _piece_rows(sl_ref[k], i)], xs_hbm.at[_piece_rows(rg_ref[k], i)],
                                  sem).start()
            return carry2
        lax.fori_loop(0, np_ref[k], one, 0)
        return carry
    lax.fori_loop(0, N_EXPERTS, send_runs, 0)

    @pl.when(c == nch - 1)
    def _():
        wait_pieces(npc_ref[c], slot)
        zbuf[...] = jnp.zeros(zbuf.shape, F32)

        def pad_dmas(e, op):
            def one(off, p):
                cp = pltpu.make_async_copy(zbuf.at[_tile_rows(0, p)], xs_hbm.at[_tile_rows(ps_ref[e] + off, p)], zsem)
                cp.start() if op == 0 else cp.wait()
            _for_each_piece(pn_ref[e], pad_pieces, one)

        def issue(e, carry):
            pad_dmas(e, 0)
            return carry

        def wait(e, carry):
            pad_dmas(e, 1)
            return carry
        lax.fori_loop(0, N_EXPERTS, issue, 0)
        lax.fori_loop(0, N_EXPERTS, wait, 0)

        half = blk // 2

        def tail_dmas(i, op):
            cp = pltpu.make_async_copy(zbuf.at[_tile_rows(0, half)],
                                       xs_hbm.at[_tile_rows(tail_ref[0] + i * half, half)], zsem)
            cp.start() if op == 0 else cp.wait()

        def tail_issue(i, carry):
            tail_dmas(i, 0)
            return carry

        def tail_wait(i, carry):
            tail_dmas(i, 1)
            return carry
        lax.fori_loop(0, tail_ref[1], tail_issue, 0)
        lax.fori_loop(0, tail_ref[1], tail_wait, 0)


def _stage_rows(chunk):
    return TOP_K * chunk + N_EXPERTS * (RUN_SLOT - 1) // RUN_SLOT * RUN_SLOT + RUN_SLOT


def dispatch(h8, plan):
    chunk, nch, blk = plan['chunk'], plan['nch'], plan['blk']
    n_slots = plan['nblk'] * blk
    gs = pltpu.PrefetchScalarGridSpec(
        num_scalar_prefetch=7,
        grid=(nch,),
        in_specs=[pl.BlockSpec((1, 1, TOP_K * chunk), lambda c, *_: (c, 0, 0), memory_space=pltpu.SMEM),
                  pl.BlockSpec((chunk * ROW_TILES, LANES), lambda c, *_: (c, 0))],
        out_specs=pl.BlockSpec(memory_space=pl.ANY),
        scratch_shapes=[pltpu.VMEM((2, _stage_rows(chunk) * ROW_TILES, LANES), F32),
                        pltpu.VMEM((blk * ROW_TILES, LANES), F32),
                        pltpu.SemaphoreType.DMA(()), pltpu.SemaphoreType.DMA(())],
    )
    return pl.pallas_call(
        functools.partial(_dispatch_body, chunk=chunk, nch=nch, blk=blk),
        grid_spec=gs,
        out_shape=jax.ShapeDtypeStruct((n_slots * ROW_TILES, LANES), F32),
        compiler_params=_cparams("arbitrary"),
        name="dispatch",
    )(plan['run_g'], plan['slot_l'], plan['n_piece'], plan['n_piece_chunk'], plan['pad_start'], plan['pad_n'],
      plan['tail'], plan['lpos'], h8)


FF_CHUNK = 512


def _experts_body(be_ref, nv_ref, pe_ref, xs_ref, wg_ref, wu_ref, wd_ref, ys_ref, xb, *wscr, precise, blk, layer):
    j = pl.program_id(0)
    nv = nv_ref[0]
    slot = j % 2
    k = j - 1

    @pl.when(j == 0)
    def _():
        xb[1] = jnp.zeros(xb.shape[1:], xb.dtype)

    if not precise:
        wgb, wub, wdb, wgf, wuf, wdf, wsem = wscr

        def weight_copies(e, ws):
            return [pltpu.make_async_copy(src.at[layer, e], dst.at[ws], wsem.at[ws])
                    for src, dst in ((wg_ref, wgf), (wu_ref, wuf), (wd_ref, wdf))]

        @pl.when(j == 0)
        def _():
            for cp in weight_copies(be_ref[0], 0):
                cp.start()

        kc = jnp.clip(k, 0, nv - 1)
        e = be_ref[kc]
        ws = pe_ref[N_EXPERTS + e]

        @pl.when((k >= 0) & (k < nv) & ((k == 0) | (e != be_ref[jnp.maximum(kc - 1, 0)])))
        def _():
            for cp in weight_copies(e, ws):
                cp.wait()
            wgb[...] = wgf[ws].astype(BF16)
            wub[...] = wuf[ws].astype(BF16)
            wdb[...] = wdf[ws].astype(BF16)
            nxt = pe_ref[e]

            @pl.when(nxt < nv)
            def _():
                for cp in weight_copies(be_ref[jnp.minimum(nxt, nv - 1)], 1 - ws):
                    cp.start()

    @pl.when(j <= nv)
    def _():
        xb[slot] = _load_row_tiles(xs_ref, blk).astype(xb.dtype)
        if precise:
            wg, wu, wd = wg_ref[0], wu_ref[0], wd_ref[0]
        else:
            wg, wu, wd = wgb, wub, wdb
        xm = xb[1 - slot]
        gt = _mm(xm, wg[...], precise)
        up = _mm(xm, wu[...], precise)
        _store_row_tiles(ys_ref, _mm(gt * jax.nn.sigmoid(gt) * up, wd[...], precise), blk)

    @pl.when(j > nv)
    def _():
        ys_ref[...] = jnp.zeros(ys_ref.shape, F32)


def experts(xs, plan, wg, wu, wd, layer, *, precise):
    nblk, rows = plan['nblk'], plan['blk'] * ROW_TILES
    d, ff = wg.shape[2], wg.shape[3]
    blk = lambda j, be, nv, pe: (jnp.minimum(j, nv[0] - 1), 0)
    if precise:
        wspec = lambda shape: pl.BlockSpec((None, 1) + shape,
                                           lambda j, be, nv, pe: (layer, be[jnp.clip(j - 1, 0, nv[0] - 1)], 0, 0))
        wspecs = [wspec((d, ff)), wspec((d, ff)), wspec((ff, d))]
        wscratch = []
    else:
        wspecs = [pl.BlockSpec(memory_space=pl.ANY)] * 3
        wscratch = [pltpu.VMEM((d, ff), BF16), pltpu.VMEM((d, ff), BF16), pltpu.VMEM((ff, d), BF16),
                    pltpu.VMEM((2, d, ff), F32), pltpu.VMEM((2, d, ff), F32), pltpu.VMEM((2, ff, d), F32),
                    pltpu.SemaphoreType.DMA((2,))]
    gs = pltpu.PrefetchScalarGridSpec(
        num_scalar_prefetch=3,
        grid=(nblk + 1,),
        in_specs=[pl.BlockSpec((rows, LANES), blk)] + wspecs,
        out_specs=pl.BlockSpec((rows, LANES), lambda j, be, nv, pe: (jnp.maximum(j - 1, 0), 0)),
        scratch_shapes=[pltpu.VMEM((2, plan['blk'], d), F32 if precise else BF16)] + wscratch,
    )
    return pl.pallas_call(
        functools.partial(_experts_body, precise=precise, blk=plan['blk'], layer=layer),
        grid_spec=gs,
        out_shape=jax.ShapeDtypeStruct(xs.shape, F32),
        compiler_params=_cparams("arbitrary"),
        name="experts",
    )(plan['block_e'], plan['n_used'], plan['expert_tab'], xs, wg, wu, wd)


def moe_plan(lpos8, gate8, runn8, blk):
    nch = runn8.shape[0] // SUBLANES
    chunk = lpos8.shape[1]
    n_assign = nch * chunk * TOP_K
    nblk = (n_assign + N_EXPERTS * (blk - 1 + RUN_SLOT) + blk - 1) // blk
    run_n = runn8.reshape(nch, SUBLANES, LANES)[:, 0, :N_EXPERTS]
    per_chunk = lambda a: a.reshape(nch, 1, SUBLANES * chunk)[:, :, :TOP_K * chunk]
    counts = jnp.sum(run_n, axis=0)
    padded = (counts + RUN_SLOT + blk - 1) // blk * blk
    pends = jnp.cumsum(padded)
    pstarts = pends - padded
    run_g = pstarts[None, :] + jnp.cumsum(run_n, axis=0) - run_n
    n_piece = (run_n + RUN_SLOT - 1) // RUN_SLOT
    slot_l = (jnp.cumsum(n_piece, axis=1) - n_piece) * RUN_SLOT
    blk_start = jnp.arange(nblk, dtype=I32) * blk
    block_e = jnp.minimum(jnp.sum((pends[None, :] <= blk_start[:, None]).astype(I32), axis=1), N_EXPERTS - 1)
    return dict(chunk=chunk, nch=nch, nblk=nblk, blk=blk,
                run_g=run_g.reshape(-1).astype(I32), slot_l=slot_l.reshape(-1).astype(I32),
                n_piece=n_piece.reshape(-1).astype(I32), n_piece_chunk=jnp.sum(n_piece, axis=1).astype(I32),
                pad_start=(pstarts + counts).astype(I32), pad_n=(padded - counts).astype(I32),
                lpos=per_chunk(lpos8), gate=per_chunk(gate8), block_e=block_e.astype(I32),
                expert_tab=jnp.concatenate([pends // blk, jnp.arange(N_EXPERTS, dtype=I32) % 2]).astype(I32),
                n_used=(pends[-1:] // blk).astype(I32),
                tail=jnp.stack([pends[-1], 2 * (nblk - pends[-1] // blk)]).astype(I32))


def _combine(rg_ref, sl_ref, np_ref, npc_ref, lpos_ref, gate_ref, x_ref, ys_hbm, ystage, comb, sem, *, chunk, nch):
    c = pl.program_id(0)
    slot = c % 2

    def fetch(cc, sl):
        def per_e(e, carry):
            k = cc * N_EXPERTS + e

            def one(i, carry2):
                pltpu.make_async_copy(ys_hbm.at[_piece_rows(rg_ref[k], i)], ystage.at[sl, _piece_rows(sl_ref[k], i)],
                                      sem.at[sl]).start()
                return carry2
            lax.fori_loop(0, np_ref[k], one, 0)
            return carry
        lax.fori_loop(0, N_EXPERTS, per_e, 0)

    @pl.when(c == 0)
    def _():
        fetch(0, 0)

    @pl.when(c + 1 < nch)
    def _():
        fetch(c + 1, 1 - slot)

    def wait_piece(i, carry):
        pltpu.make_async_copy(ystage.at[slot, _tile_rows(0, RUN_SLOT)], ystage.at[slot, _tile_rows(0, RUN_SLOT)],
                              sem.at[slot]).wait()
        return carry
    lax.fori_loop(0, npc_ref[c], wait_piece, 0)

    def per_tok(t, carry):
        y0 = ystage[slot, _rows_at(lpos_ref[0, 0, t]), :]
        y1 = ystage[slot, _rows_at(lpos_ref[0, 0, chunk + t]), :]
        comb[_tile_rows(t), :] = gate_ref[0, 0, t] * y0 + gate_ref[0, 0, chunk + t] * y1
        return carry
    lax.fori_loop(0, chunk, per_tok, 0, unroll=8)
    return x_ref[...] + _load_row_tiles(comb, chunk)


def _combine_glu_body(rg_ref, sl_ref, np_ref, npc_ref, lpos_ref, gate_ref, x_ref, ys_hbm, g_ref, w_ref, b_ref,
                      x2_ref, u_ref, ystage, comb, sem, *, chunk, nch, precise):
    x2 = _combine(rg_ref, sl_ref, np_ref, npc_ref, lpos_ref, gate_ref, x_ref, ys_hbm, ystage, comb, sem,
                  chunk=chunk, nch=nch)
    x2_ref[...] = x2
    zz = _mm(_rms(x2, g_ref[...]), w_ref[...], precise) + b_ref[...]
    half = zz.shape[1] // 2
    u_ref[...] = zz[:, :half] * jax.nn.sigmoid(zz[:, half:])


def _combine_final_body(rg_ref, sl_ref, np_ref, npc_ref, lpos_ref, gate_ref, x_ref, ys_hbm, g_ref, o_ref,
                        ystage, comb, sem, *, chunk, nch):
    x2 = _combine(rg_ref, sl_ref, np_ref, npc_ref, lpos_ref, gate_ref, x_ref, ys_hbm, ystage, comb, sem,
                  chunk=chunk, nch=nch)
    o_ref[...] = _rms(x2, g_ref[...])


def _combine_call(body, plan, x, ys, extra, extra_specs, out_specs, out_shape, name):
    chunk, nch = plan['chunk'], plan['nch']
    d = x.shape[1]
    smem_blk = pl.BlockSpec((1, 1, TOP_K * chunk), lambda c, *_: (c, 0, 0), memory_space=pltpu.SMEM)
    gs = pltpu.PrefetchScalarGridSpec(
        num_scalar_prefetch=4,
        grid=(nch,),
        in_specs=[smem_blk, smem_blk, pl.BlockSpec((chunk, d), lambda c, *_: (c, 0)),
                  pl.BlockSpec(memory_space=pl.ANY)] + extra_specs,
        out_specs=out_specs,
        scratch_shapes=[pltpu.VMEM((2, _stage_rows(chunk) * ROW_TILES, LANES), F32),
                        pltpu.VMEM((chunk * ROW_TILES, LANES), F32), pltpu.SemaphoreType.DMA((2,))],
    )
    return pl.pallas_call(
        functools.partial(body, chunk=chunk, nch=nch),
        grid_spec=gs,
        out_shape=out_shape,
        compiler_params=_cparams("arbitrary"),
        name=name,
    )(plan['run_g'], plan['slot_l'], plan['n_piece'], plan['n_piece_chunk'], plan['lpos'], plan['gate'], x, ys,
      *extra)


def combine_glu(x, ys, plan, g, w, b, *, precise):
    rows, d = x.shape
    chunk = plan['chunk']
    cols = w.shape[1]
    const = lambda shape: pl.BlockSpec(shape, lambda c, *_: (0,) * len(shape))
    row_spec = lambda width: pl.BlockSpec((chunk, width), lambda c, *_: (c, 0))
    return _combine_call(
        functools.partial(_combine_glu_body, precise=precise), plan, x, ys, [g, w, b],
        [const((1, d)), const((d, cols)), const((1, cols))], [row_spec(d), row_spec(cols // 2)],
        [jax.ShapeDtypeStruct((rows, d), F32), jax.ShapeDtypeStruct((rows, cols // 2), F32)], "combine_glu")


def combine_final(x, ys, plan, g):
    rows, d = x.shape
    chunk = plan['chunk']
    return _combine_call(
        _combine_final_body, plan, x, ys, [g], [pl.BlockSpec((1, d), lambda c, *_: (0, 0))],
        pl.BlockSpec((chunk, d), lambda c, *_: (c, 0)), jax.ShapeDtypeStruct((rows, d), F32), "combine_final")


CONV_TILE = 512
CONV_HIST = 32


def _ln_swish(y, g, b):
    yc = y - jnp.mean(y, axis=-1, keepdims=True)
    yn = yc * lax.rsqrt(jnp.mean(yc * yc, axis=-1, keepdims=True) + EPS) * g + b
    return yn * jax.nn.sigmoid(yn)


CONV_ROWS = 64
LN_ROWS = 16
LN_UNROLL = 8


def _dwconv_prompt_body(u_ref, w_ref, bdw_ref, g_ref, b_ref, o_ref, ext, y_sc, *, tt):
    t = pl.program_id(1)
    n_lt = ext.shape[0]

    @pl.when(t == 0)
    def _():
        ext[:, 0:CONV_HIST, :] = jnp.zeros((n_lt, CONV_HIST, LANES), F32)

    @pl.when(t > 0)
    def _():
        ext[:, 0:CONV_HIST, :] = ext[:, tt:tt + CONV_HIST, :]

    for j in range(n_lt):
        ext[j, CONV_HIST:CONV_HIST + tt, :] = u_ref[:, j * LANES:(j + 1) * LANES]
    off = CONV_HIST - (C_KERNEL - 1)
    for j in range(n_lt):
        wj = w_ref[:, j * LANES:(j + 1) * LANES]
        bj = bdw_ref[:, j * LANES:(j + 1) * LANES]
        for c in range(tt // CONV_ROWS):
            acc = ext[j, pl.ds(off + c * CONV_ROWS, CONV_ROWS), :] * wj[0:1] + bj
            for k in range(1, C_KERNEL):
                acc = acc + ext[j, pl.ds(off + k + c * CONV_ROWS, CONV_ROWS), :] * wj[k:k + 1]
            y_sc[c * CONV_ROWS:(c + 1) * CONV_ROWS, j * LANES:(j + 1) * LANES] = acc

    def ln_rows(r, carry):
        rows = pl.ds(pl.multiple_of(r * LN_ROWS, LN_ROWS), LN_ROWS)
        o_ref[rows, :] = _ln_swish(y_sc[rows, :], g_ref[...], b_ref[...]).astype(o_ref.dtype)
        return carry
    lax.fori_loop(0, tt // LN_ROWS, ln_rows, 0, unroll=LN_UNROLL)


def dwconv_prompt(u, w, b_dw, ln_g, ln_b, bsz, seq):
    tt = min(CONV_TILE, seq)
    nt = seq // tt
    d = u.shape[1]
    const = lambda shape: pl.BlockSpec(shape, lambda b, t: (0,) * len(shape))
    return pl.pallas_call(
        functools.partial(_dwconv_prompt_body, tt=tt),
        grid=(bsz, nt),
        in_specs=[pl.BlockSpec((tt, d), lambda b, t: (b * nt + t, 0)), const((C_KERNEL, d)), const((1, d)),
                  const((1, d)), const((1, d))],
        out_specs=pl.BlockSpec((tt, d), lambda b, t: (b * nt + t, 0)),
        out_shape=jax.ShapeDtypeStruct((bsz * seq, d), BF16),
        scratch_shapes=[pltpu.VMEM((d // LANES, CONV_HIST + tt, LANES), F32), pltpu.VMEM((tt, d), F32)],
        compiler_params=_cparams("arbitrary", "arbitrary"),
        name="dwconv_prompt",
    )(u, w, b_dw, ln_g, ln_b)


def _dwconv_sample_body(u_ref, buf_ref, w_ref, bdw_ref, g_ref, b_ref, o_ref, nbuf_ref, *, tb):
    w = w_ref[...]
    rows = []
    for i in range(tb):
        hist = buf_ref[i]
        ur = u_ref[i:i + 1, :]
        rows.append(jnp.sum(hist * w[:C_KERNEL - 1], axis=0, keepdims=True) + ur * w[C_KERNEL - 1:C_KERNEL])
        nbuf_ref[i] = jnp.concatenate([hist[1:], ur], axis=0)
    y = jnp.concatenate(rows, axis=0) + bdw_ref[...]
    o_ref[...] = _ln_swish(y, g_ref[...], b_ref[...])


def dwconv_sample(u, buf, w, b_dw, ln_g, ln_b):
    nb, d = u.shape
    tb = min(SAMPLE_TILE, nb)
    const = lambda shape: pl.BlockSpec(shape, lambda i: (0,) * len(shape))
    return pl.pallas_call(
        functools.partial(_dwconv_sample_body, tb=tb),
        grid=(nb // tb,),
        in_specs=[pl.BlockSpec((tb, d), lambda i: (i, 0)), pl.BlockSpec((tb, C_KERNEL - 1, d), lambda i: (i, 0, 0)),
                  const((C_KERNEL, d)), const((1, d)), const((1, d)), const((1, d))],
        out_specs=[pl.BlockSpec((tb, d), lambda i: (i, 0)), pl.BlockSpec((tb, C_KERNEL - 1, d), lambda i: (i, 0, 0))],
        out_shape=[jax.ShapeDtypeStruct((nb, d), F32), jax.ShapeDtypeStruct(buf.shape, F32)],
        compiler_params=_cparams("parallel"),
        name="dwconv_sample",
    )(u, buf, w, b_dw, ln_g, ln_b)


def _moe(h8, lpos8, gate8, runn8, wg, wu, wd, layer, *, blk, precise):
    plan = moe_plan(lpos8, gate8, runn8, blk)
    xs = dispatch(h8, plan)
    return experts(xs, plan, wg, wu, wd, layer, precise=precise), plan


def _trunk(x, caches, p, *, prompt):
    bsz, seq, d = x.shape
    rows = bsz * seq
    precise = not prompt
    wdt = F32 if precise else BF16
    xf = x.reshape(rows, d)
    row = lambda v: v.reshape(1, -1).astype(F32)

    z = norm_proj(xf, row(p['norm_mix'][0]), p['w_in'].astype(wdt), precise=precise)
    if prompt:
        z, zb = z
        att = attn_prompt(z, zb, p['rel_table'], p['sinks'], bsz, seq)
        out_b, c1, n1, m1 = mlstm_prompt(z, zb, p['conv_w'], p['b_gates'], p['g_mnorm'], bsz, seq)
        z3 = z.reshape(bsz, seq, P_COLS)
        new_k = z3[:, seq - WINDOW:, P_KA:P_KA + A_KV].reshape(bsz, WINDOW, A_KV_HEADS, A_HEAD_DIM)
        new_v = z3[:, seq - WINDOW:, P_VA:P_VA + A_KV].reshape(bsz, WINDOW, A_KV_HEADS, A_HEAD_DIM)
        new_conv = z3[:, seq - (B_CONV - 1):, P_QK:P_QK + 2 * B_QK]
        n1 = n1[:, :B_HEADS]
        m1 = m1[:, :B_HEADS, 0]
    else:
        ck, cv, c0, n0, m0, cbuf = caches[:6]
        n_buf = ck.shape[1]
        att, out_b, new_k, new_v, c1, n1, m1, new_conv = mix_sample(
            z, p['rel_table'], p['sinks'], ck.reshape(bsz, n_buf, A_KV), cv.reshape(bsz, n_buf, A_KV),
            c0, n0, m0, cbuf, p['conv_w'], p['b_gates'], p['g_mnorm'])
        new_k = new_k.reshape(bsz, n_buf, A_KV_HEADS, A_HEAD_DIM)
        new_v = new_v.reshape(bsz, n_buf, A_KV_HEADS, A_HEAD_DIM)
        m1 = m1[:, :B_HEADS]
    w_out = p['w_out'].astype(wdt)
    x1, h8, lpos, gate, runn = proj_router([att, out_b], [w_out[:A_Q], w_out[A_Q:]], None, xf, row(p['norm_ffn'][0]),
                                           p['w_router'][0], p['b_router'][0], precise=precise)
    blk = EXPERT_BLOCK_PRECISE if precise else EXPERT_BLOCK
    ys, plan = _moe(h8, lpos, gate, runn, p['w_eg'], p['w_eu'], p['w_ed'], 0, blk=blk, precise=precise)

    x2, u = combine_glu(x1, ys, plan, row(p['norm_mix'][1]), p['w_pw1'].astype(wdt), row(p['b_pw1']),
                        precise=precise)
    if prompt:
        yc = dwconv_prompt(u, p['w_dw'], row(p['b_dw']), row(p['ln_g']), row(p['ln_b']), bsz, seq)
        new_cbuf = u.reshape(bsz, seq, d)[:, seq - (C_KERNEL - 1):]
    else:
        yc, new_cbuf = dwconv_sample(u, caches[6], p['w_dw'], row(p['b_dw']), row(p['ln_g']), row(p['ln_b']))
    x3, h8, lpos, gate, runn = proj_router([yc], [p['w_pw2'].astype(wdt)], row(p['b_pw2']), x2, row(p['norm_ffn'][1]),
                                           p['w_router'][1], p['b_router'][1], precise=precise)
    ys, plan = _moe(h8, lpos, gate, runn, p['w_eg'], p['w_eu'], p['w_ed'], 1, blk=blk, precise=precise)
    y = combine_final(x3, ys, plan, row(p['norm_final']))
    add_layer = lambda t: t[None]
    return (y.reshape(bsz, seq, d),) + tuple(add_layer(t) for t in (new_k, new_v, c1, n1, m1, new_conv, new_cbuf))


def kernel(x_prompt, x_sample, cache_win_k, cache_win_v, state_mlstm_c, state_mlstm_n, state_mlstm_m, state_mlstm_conv, state_conv, norm_mix, norm_ffn, norm_final, rel_bias_table, w_in_mix, b_mlstm_gates, w_mlstm_qk_conv, attn_sinks, g_mlstm_norm, w_out_mix, w_pw1, b_pw1, w_dw, b_dw, ln_conv_g, ln_conv_b, w_pw2, b_pw2, w_router_group, b_router_group, w_router_expert, b_router_expert, w_expert_gate, w_expert_up, w_expert_down):
    w_in = w_in_mix[0]
    s_q, s_k, s_v, s_qk, s_vb, s_g = A_Q, A_Q + A_KV, A_Q + 2 * A_KV, A_Q + 2 * A_KV + 2 * B_QK, \
        A_Q + 2 * A_KV + 2 * B_QK + B_V, A_Q + 2 * A_KV + 2 * B_QK + B_V + 2 * B_HEADS
    w_in_r = jnp.concatenate([w_in[:, :s_q], w_in[:, s_v:s_qk], w_in[:, s_qk:s_vb], w_in[:, s_g:],
                              w_in[:, s_q:s_k], w_in[:, s_k:s_v], w_in[:, s_vb:s_g],
                              jnp.zeros((D_MODEL, LANES - 2 * B_HEADS), F32)], axis=1)
    b_gates = jnp.concatenate([b_mlstm_gates[0], jnp.zeros((LANES - 2 * B_HEADS,), F32)]).reshape(1, LANES)
    depth = w_router_group.shape[0]
    w_re = jnp.transpose(w_router_expert, (0, 2, 1, 3)).reshape(depth, D_MODEL, N_EXPERTS)
    w_router = jnp.concatenate([w_router_group, w_re,
                                jnp.zeros((depth, D_MODEL, LANES - N_GROUPS - N_EXPERTS), F32)], axis=-1)
    b_router = jnp.concatenate([b_router_group, b_router_expert.reshape(depth, N_EXPERTS),
                                jnp.zeros((depth, LANES - N_GROUPS - N_EXPERTS), F32)], axis=-1)[:, None, :]
    p = dict(norm_mix=norm_mix, norm_ffn=norm_ffn, norm_final=norm_final, rel_table=rel_bias_table,
             sinks=attn_sinks[0], w_in=w_in_r, b_gates=b_gates, conv_w=w_mlstm_qk_conv[0],
             g_mnorm=g_mlstm_norm[0].reshape(1, B_V), w_out=w_out_mix[0], w_pw1=w_pw1[0], b_pw1=b_pw1[0],
             w_dw=w_dw[0], b_dw=b_dw[0], ln_g=ln_conv_g[0], ln_b=ln_conv_b[0], w_pw2=w_pw2[0], b_pw2=b_pw2[0],
             w_router=w_router, b_router=b_router, w_eg=w_expert_gate, w_eu=w_expert_up, w_ed=w_expert_down)
    caches = (cache_win_k[0], cache_win_v[0], state_mlstm_c[0], state_mlstm_n[0], state_mlstm_m[0],
              state_mlstm_conv[0], state_conv[0])
    out_p = _trunk(x_prompt, None, p, prompt=True)
    out_s = _trunk(x_sample, caches, p, prompt=False)
    return (out_p[0], out_s[0]) + out_p[1:] + out_s[1:]
```

```python
import functools
import math

import numpy as np
import jax
import jax.numpy as jnp
from jax import lax
from jax.experimental import pallas as pl
from jax.experimental.pallas import tpu as pltpu

F32 = jnp.float32
BF16 = jnp.bfloat16
I32 = jnp.int32
HI = lax.Precision.HIGHEST
NEG_INF = float("-inf")

LANES = 128
SUBLANES = 8
VMEM_LIMIT = 56 * 1024 * 1024

D_MODEL = 1024
A_HEADS = 8
A_KV_HEADS = 2
A_GROUP = A_HEADS // A_KV_HEADS
A_HEAD_DIM = 64
WINDOW = 128
REL_BUCKETS = 32
REL_MAX_DIST = 128
B_HEADS = 4
B_DK = 64
B_DV = 128
B_CONV = 4
C_KERNEL = 31
N_GROUPS = 4
EXPERTS_PER_GROUP = 8
N_EXPERTS = N_GROUPS * EXPERTS_PER_GROUP
TOP_K = 2
EXPERT_FF = D_MODEL // 2
EXPERT_BLOCK = 512
EXPERT_BLOCK_PRECISE = 128
EPS = 1e-6

A_Q = A_HEADS * A_HEAD_DIM
A_KV = A_KV_HEADS * A_HEAD_DIM
B_QK = B_HEADS * B_DK
B_V = B_HEADS * B_DV
ROW_TILES = D_MODEL // LANES

Z_QA, Z_QK, Z_VB, Z_OG, Z_KA, Z_VA, Z_GATES = 0, 512, 1024, 1536, 2048, 2176, 2304
Z_COLS = 2432
MLSTM_CHUNK = 128


def _cparams(*sem):
    return pltpu.CompilerParams(dimension_semantics=sem, vmem_limit_bytes=VMEM_LIMIT)


def _rms(x, g):
    return x * lax.rsqrt(jnp.mean(x * x, axis=-1, keepdims=True) + EPS) * g


def _mm(a, w, precise):
    if not precise:
        return jnp.dot(a.astype(BF16), w, preferred_element_type=F32)
    a = a.astype(F32)
    a_hi = a.astype(BF16)
    a_lo = (a - a_hi.astype(F32)).astype(BF16)
    w_hi = w.astype(BF16)
    w_lo = (w - w_hi.astype(F32)).astype(BF16)
    return (jnp.dot(a_hi, w_hi, preferred_element_type=F32) + jnp.dot(a_lo, w_hi, preferred_element_type=F32)
            + jnp.dot(a_hi, w_lo, preferred_element_type=F32))


def _t5_buckets(dist):
    exact = REL_BUCKETS // 2
    d = np.maximum(dist, 0)
    large = exact + (np.log(np.maximum(d, 1).astype(np.float32) / exact)
                     / math.log(REL_MAX_DIST / exact) * (REL_BUCKETS - exact)).astype(np.int32)
    return np.where(d < exact, d, np.minimum(large, REL_BUCKETS - 1)).astype(np.int32)


P_QK, P_OG, P_KA, P_VA, P_GATES = 0, 512, 1024, 1152, 1280
P_COLS = 1408
PB_QA, PB_VB = 0, 512
PB_COLS = 1024


def _norm_proj_body(x_ref, g_ref, w_ref, *o_refs, precise):
    h = _rms(x_ref[...], g_ref[...])
    z = _mm(h, w_ref[...], precise)
    if len(o_refs) == 1:
        o_refs[0][...] = z
        return
    z32_ref, zb_ref = o_refs
    z32_ref[...] = jnp.concatenate([z[:, Z_QK:Z_QK + 2 * B_QK], z[:, Z_OG:Z_OG + B_V], z[:, Z_KA:Z_GATES + LANES]],
                                   axis=1)
    zb_ref[...] = jnp.concatenate([z[:, Z_QA:Z_QA + A_Q], z[:, Z_VB:Z_VB + B_V]], axis=1).astype(BF16)


def norm_proj(x, g, w, *, precise):
    rows, d = x.shape
    cols = w.shape[1]
    tm = min(rows, 128 if precise else 512)
    row_spec = lambda width: pl.BlockSpec((tm, width), lambda i: (i, 0))
    if precise:
        out_specs, out_shape = row_spec(cols), jax.ShapeDtypeStruct((rows, cols), F32)
    else:
        out_specs = [row_spec(P_COLS), row_spec(PB_COLS)]
        out_shape = [jax.ShapeDtypeStruct((rows, P_COLS), F32), jax.ShapeDtypeStruct((rows, PB_COLS), BF16)]
    return pl.pallas_call(
        functools.partial(_norm_proj_body, precise=precise),
        grid=(rows // tm,),
        in_specs=[row_spec(d), pl.BlockSpec((1, d), lambda i: (0, 0)), pl.BlockSpec((d, cols), lambda i: (0, 0))],
        out_specs=out_specs,
        out_shape=out_shape,
        compiler_params=_cparams("parallel"),
        name="norm_proj",
    )(x, g, w)


def _attn_prompt_body(tab_ref, sink_ref, bkt_ref, q_ref, kp_ref, kc_ref, vp_ref, vc_ref, o_ref, bias_ref, *, nq):
    b = pl.program_id(0)
    n = pl.program_id(1)

    @pl.when((b == 0) & (n == 0))
    def _():
        bk = bkt_ref[...]
        first = lax.broadcasted_iota(I32, bk.shape, 1) >= WINDOW
        for h in range(A_HEADS):
            acc = jnp.full(bk.shape, NEG_INF, F32)
            for t in range(REL_BUCKETS):
                acc = jnp.where(bk == t, tab_ref[t, h], acc)
            bias_ref[h] = acc
            bias_ref[A_HEADS + h] = jnp.where(first, acc, NEG_INF)

    k_all = jnp.concatenate([kp_ref[...], kc_ref[...]], axis=0).astype(BF16)
    v_all = jnp.concatenate([vp_ref[...], vc_ref[...]], axis=0).astype(BF16)
    for sub in range(nq):
        rows = slice(sub * WINDOW, (sub + 1) * WINDOW)
        q = q_ref[rows, :] * (A_HEAD_DIM ** -0.5)
        kb = k_all[sub * WINDOW:(sub + 2) * WINDOW]
        vb = v_all[sub * WINDOW:(sub + 2) * WINDOW]
        table = jnp.where(n == 0, A_HEADS, 0) if sub == 0 else 0
        H = range(A_HEADS)
        nt = (((1,), (1,)), ((), ()))
        kv = [slice((h // A_GROUP) * A_HEAD_DIM, (h // A_GROUP + 1) * A_HEAD_DIM) for h in H]
        s = [lax.dot_general(q[:, h * A_HEAD_DIM:(h + 1) * A_HEAD_DIM].astype(BF16), kb[:, kv[h]], nt,
                             preferred_element_type=F32) + bias_ref[table + h] for h in H]
        mx = [jnp.maximum(jnp.max(s[h], axis=-1, keepdims=True), sink_ref[h]) for h in H]
        p = [jnp.exp(s[h] - mx[h]) for h in H]
        den = [jnp.sum(p[h], axis=-1, keepdims=True) + jnp.exp(sink_ref[h] - mx[h]) for h in H]
        outs = [jnp.dot(p[h].astype(BF16), vb[:, kv[h]], preferred_element_type=F32) / den[h] for h in H]
        o_ref[rows, :] = jnp.concatenate(outs, axis=1).astype(o_ref.dtype)


ATTN_BLOCKS = 2


def attn_prompt(z, zb, rel_table, sinks, bsz, seq):
    nb = seq // WINDOW
    dist = WINDOW + np.arange(WINDOW)[:, None] - np.arange(2 * WINDOW)[None, :]
    bkt = np.where((dist >= 0) & (dist <= WINDOW), _t5_buckets(dist), -1).astype(np.int32)
    kcol, vcol = P_KA // LANES, P_VA // LANES
    smem = pl.BlockSpec(memory_space=pltpu.SMEM)
    nq = ATTN_BLOCKS if nb % ATTN_BLOCKS == 0 else 1
    ns = nb // nq

    def cur(c):
        return pl.BlockSpec((nq * WINDOW, LANES), lambda b, n: (b * ns + n, c))

    def prev(c):
        return pl.BlockSpec((WINDOW, LANES), lambda b, n: (b * nb + jnp.maximum(n * nq - 1, 0), c))

    return pl.pallas_call(
        functools.partial(_attn_prompt_body, nq=nq),
        grid=(bsz, ns),
        in_specs=[smem, smem,
                  pl.BlockSpec((WINDOW, 2 * WINDOW), lambda b, n: (0, 0)),
                  pl.BlockSpec((nq * WINDOW, A_Q), lambda b, n: (b * ns + n, PB_QA // A_Q)),
                  prev(kcol), cur(kcol), prev(vcol), cur(vcol)],
        out_specs=pl.BlockSpec((nq * WINDOW, A_Q), lambda b, n: (b * ns + n, 0)),
        out_shape=jax.ShapeDtypeStruct((bsz * seq, A_Q), BF16),
        scratch_shapes=[pltpu.VMEM((2 * A_HEADS, WINDOW, 2 * WINDOW), F32)],
        compiler_params=_cparams("arbitrary", "arbitrary"),
        name="attn_prompt",
    )(rel_table, sinks, jnp.asarray(bkt), zb, z, z, z, z)


def _log_sigmoid(x):
    return -(jnp.maximum(-x, 0.0) + jnp.log1p(jnp.exp(-jnp.abs(x))))


def _mlstm_prompt_body(qk_ref, v_ref, g_ref, og_ref, cw_ref, bg_ref, gn_ref,
                       ob_ref, c_out, n_out, m_out,
                       c_sc, n_sc, m_sc, hist_sc, *, L, nc, nbat, nsub):
    ci = pl.program_id(1)

    @pl.when(ci == 0)
    def _():
        c_sc[...] = jnp.zeros(c_sc.shape, F32)
        n_sc[...] = jnp.zeros(n_sc.shape, F32)
        m_sc[...] = jnp.full(m_sc.shape, NEG_INF, F32)
        hist_sc[...] = jnp.zeros(hist_sc.shape, F32)

    row = lax.broadcasted_iota(I32, (L, L), 0)
    colm = lax.broadcasted_iota(I32, (L, L), 1)
    causal = colm <= row
    tril = causal.astype(F32)
    for sub in range(nsub):
        rows = pl.ds(sub * L, L)
        stores = []
        for bb in range(nbat):
            stores += _mlstm_chunk(qk_ref.at[bb, rows], v_ref.at[bb, rows], g_ref.at[bb, rows], og_ref.at[bb, rows],
                                   cw_ref, bg_ref, gn_ref, ob_ref.at[bb, rows], c_sc.at[bb], n_sc.at[bb],
                                   m_sc.at[bb], hist_sc.at[bb], causal, tril, L)
        for store in stores:
            store()

    @pl.when(ci == nc - 1)
    def _():
        c_out[...] = c_sc[...]
        n_out[...] = n_sc[...]
        m_out[...] = m_sc[...]


def _mlstm_chunk(qk_ref, v_ref, g_ref, og_ref, cw_ref, bg_ref, gn_ref, ob_ref, c_sc, n_sc, m_sc, hist_sc,
                 causal, tril, L):
    cur = qk_ref[...]
    ext = jnp.concatenate([hist_sc[...], cur], axis=0)
    cw = cw_ref[...]
    off = SUBLANES - (B_CONV - 1)
    conv = ext[off:off + L] * cw[0:1]
    for j in range(1, B_CONV):
        conv = conv + ext[off + j:off + j + L] * cw[j:j + 1]
    qk = conv * jax.nn.sigmoid(conv)
    q_all = qk[:, :B_QK]
    k_all = qk[:, B_QK:] * (B_DK ** -0.5)
    k_t = k_all.T
    v_all = v_ref[...]
    og = og_ref[...]
    gn = gn_ref[...]

    G = g_ref[...] + bg_ref[...]
    lf = _log_sigmoid(G)
    Bc = jnp.dot(tril, lf, precision=HI, preferred_element_type=F32)
    BT = Bc.T
    GT = G.T

    H = range(B_HEADS)
    nt = (((1,), (1,)), ((), ()))
    qh = [q_all[:, h * B_DK:(h + 1) * B_DK] for h in H]
    kh = [k_all[:, h * B_DK:(h + 1) * B_DK] for h in H]
    qb = [q.astype(BF16) for q in qh]
    vb = [v_all[:, h * B_DV:(h + 1) * B_DV].astype(BF16) for h in H]
    b_col = [Bc[:, B_HEADS + h:B_HEADS + h + 1] for h in H]
    c0 = [c_sc[h] for h in H]
    n0 = [n_sc[h:h + 1, :] for h in H]
    a = [b_col[h] + m_sc[h:h + 1, 0:1] for h in H]
    d = [jnp.where(causal, b_col[h] - BT[B_HEADS + h:B_HEADS + h + 1, :] + GT[h:h + 1, :], NEG_INF) for h in H]
    qk_raw = [lax.dot_general(qb[h], kh[h].astype(BF16), nt, preferred_element_type=F32) for h in H]
    qc = [jnp.dot(qb[h], c0[h].astype(BF16), preferred_element_type=F32) for h in H]
    m = [jnp.maximum(a[h], jnp.max(d[h], axis=-1, keepdims=True)) for h in H]
    dw = [jnp.exp(d[h] - m[h]) for h in H]
    aw = [jnp.exp(a[h] - m[h]) for h in H]
    s = [qk_raw[h] * dw[h] for h in H]
    sv = [jnp.dot(s[h].astype(BF16), vb[h], preferred_element_type=F32) for h in H]
    li = lax.broadcasted_iota(I32, (B_QK, LANES), 0) >> (B_DK.bit_length() - 1)
    ind = (li == lax.broadcasted_iota(I32, (B_QK, LANES), 1)).astype(F32)
    qn = jnp.dot(q_all * jnp.concatenate(n0, axis=1), ind, precision=HI, preferred_element_type=F32)
    den = [jnp.sum(s[h], axis=-1, keepdims=True) + aw[h] * qn[:, h:h + 1] for h in H]
    hh = [(sv[h] + aw[h] * qc[h]) / jnp.maximum(jnp.abs(den[h]), jnp.exp(-m[h])) for h in H]
    m_last = [m[h][L - 1:L, :] for h in H]
    wl = [jnp.exp(b_col[h][L - 1:L, :] - b_col[h] + G[:, h:h + 1] - m_last[h]) for h in H]
    decay = [aw[h][L - 1:L, :] for h in H]
    kw_t = [(k_t[h * B_DK:(h + 1) * B_DK, :] * dw[h][L - 1:L, :]).astype(BF16) for h in H]
    c1 = [decay[h] * c0[h] + jnp.dot(kw_t[h], vb[h], preferred_element_type=F32) for h in H]
    n1 = [decay[h] * n0[h] + jnp.sum(kh[h] * wl[h], axis=0, keepdims=True) for h in H]
    stores = [functools.partial(_store_state, c_sc, n_sc, m_sc, h, c1[h], n1[h], m_last[h]) for h in H]
    hn = [hh[h] * lax.rsqrt(jnp.mean(hh[h] * hh[h], axis=-1, keepdims=True) + EPS) * gn[:, h * B_DV:(h + 1) * B_DV]
          for h in H]
    outs = [jax.nn.sigmoid(og[:, h * B_DV:(h + 1) * B_DV]) * hn[h] for h in H]
    out = jnp.concatenate(outs, axis=1).astype(ob_ref.dtype)
    stores.append(functools.partial(_store_chunk, ob_ref, hist_sc, out, cur[L - SUBLANES:L]))
    return stores


def _store_state(c_sc, n_sc, m_sc, h, c1, n1, m_last):
    c_sc[h] = c1
    n_sc[h:h + 1, :] = n1
    m_sc[h:h + 1, :] = jnp.broadcast_to(m_last, (1, LANES))


def _store_chunk(ob_ref, hist_sc, out, tail):
    ob_ref[...] = out
    hist_sc[...] = tail


MLSTM_BATCH = 1
MLSTM_SUBCHUNKS = 1


def mlstm_prompt(z, zb, conv_w, b_gates_pad, g_norm, bsz, seq):
    L = MLSTM_CHUNK
    nsub = MLSTM_SUBCHUNKS if (seq // L) % MLSTM_SUBCHUNKS == 0 else 1
    nc = seq // (L * nsub)
    nbat = MLSTM_BATCH if bsz % MLSTM_BATCH == 0 else 1
    z3 = z.reshape(bsz, seq, P_COLS)
    zb3 = zb.reshape(bsz, seq, PB_COLS)

    def zspec(width, colblk):
        return pl.BlockSpec((nbat, nsub * L, width), lambda b, c: (b, c, colblk))

    const = lambda shape: pl.BlockSpec(shape, lambda b, c: (0,) * len(shape))
    state = lambda shape: pl.BlockSpec((nbat,) + shape, lambda b, c: (b,) + (0,) * len(shape))
    out_b, c1, n1, m1 = pl.pallas_call(
        functools.partial(_mlstm_prompt_body, L=L, nc=nc, nbat=nbat, nsub=nsub),
        grid=(bsz // nbat, nc),
        in_specs=[zspec(2 * B_QK, P_QK // (2 * B_QK)), zspec(B_V, PB_VB // B_V), zspec(LANES, P_GATES // LANES),
                  zspec(B_V, P_OG // B_V), const((B_CONV, 2 * B_QK)), const((1, LANES)), const((1, B_V))],
        out_specs=[pl.BlockSpec((nbat, nsub * L, B_V), lambda b, c: (b, c, 0)),
                   state((B_HEADS, B_DK, B_DV)), state((SUBLANES, B_DK)), state((SUBLANES, LANES))],
        out_shape=[jax.ShapeDtypeStruct((bsz, seq, B_V), BF16),
                   jax.ShapeDtypeStruct((bsz, B_HEADS, B_DK, B_DV), F32),
                   jax.ShapeDtypeStruct((bsz, SUBLANES, B_DK), F32),
                   jax.ShapeDtypeStruct((bsz, SUBLANES, LANES), F32)],
        scratch_shapes=[pltpu.VMEM((nbat, B_HEADS, B_DK, B_DV), F32), pltpu.VMEM((nbat, SUBLANES, B_DK), F32),
                        pltpu.VMEM((nbat, SUBLANES, LANES), F32), pltpu.VMEM((nbat, SUBLANES, 2 * B_QK), F32)],
        compiler_params=_cparams("arbitrary", "arbitrary"),
        name="mlstm_prompt",
    )(z3, zb3, z3, z3, conv_w, b_gates_pad, g_norm)
    return out_b.reshape(bsz * seq, B_V), c1, n1, m1


SAMPLE_TILE = 8


def _mix_sample_body(tab_ref, sink_ref, bkt_ref, z_ref, ck_ref, cv_ref, c0_ref, n0_ref, m0_ref, cb_ref,
                     cw_ref, bg_ref, gn_ref,
                     att_ref, ob_ref, nk_ref, nv_ref, c1_ref, n1_ref, m1_ref, ncb_ref, *, tb):
    bk = bkt_ref[...]
    cw = cw_ref[...]
    gn = gn_ref[...]
    bias_rows = []
    for h in range(A_HEADS):
        bias = jnp.zeros(bk.shape, F32)
        for t in range(REL_BUCKETS):
            bias = jnp.where(bk == t, tab_ref[t, h], bias)
        bias_rows.append(bias)
    n_keys = bk.shape[1]
    rows, cols = A_HEADS * tb, tb * n_keys
    z_all = z_ref[...]
    q_all = z_all[:, Z_QA:Z_QA + A_Q] * (A_HEAD_DIM ** -0.5)
    ka_all = z_all[:, Z_KA:Z_KA + A_KV]
    va_all = z_all[:, Z_VA:Z_VA + A_KV]
    zero_half = jnp.zeros((tb, A_HEAD_DIM), F32)
    q_rows = []
    for h in range(A_HEADS):
        qh = q_all[:, h * A_HEAD_DIM:(h + 1) * A_HEAD_DIM]
        q_rows.append(jnp.concatenate([qh, zero_half] if h // A_GROUP == 0 else [zero_half, qh], axis=1))
    qm = jnp.concatenate(q_rows, axis=0)
    ka_rep = jnp.concatenate([ka_all] * A_HEADS, axis=0)
    va_rep = jnp.concatenate([va_all] * A_HEADS, axis=0)
    k_cat = jnp.concatenate([ck_ref[i] for i in range(tb)], axis=0)
    v_cat = jnp.concatenate([cv_ref[i] for i in range(tb)], axis=0)
    bias_c = jnp.concatenate([jnp.broadcast_to(b, (tb, n_keys)) for b in bias_rows], axis=0)
    bias_c = jnp.concatenate([bias_c] * tb, axis=1)
    r_id = lax.broadcasted_iota(I32, (rows, cols), 0)
    c_id = lax.broadcasted_iota(I32, (rows, cols), 1)
    own = (r_id & (tb - 1)) == (c_id >> (n_keys.bit_length() - 1))
    head_id = lax.broadcasted_iota(I32, (rows, 1), 0) >> (tb.bit_length() - 1)
    bias_n = jnp.zeros((rows, 1), F32)
    sinks = jnp.zeros((rows, 1), F32)
    for h in range(A_HEADS):
        bias_n = jnp.where(head_id == h, tab_ref[0, h], bias_n)
        sinks = jnp.where(head_id == h, sink_ref[h], sinks)
    lc = lax.dot_general(qm, k_cat, (((1,), (1,)), ((), ())), precision=HI, preferred_element_type=F32)
    lc = jnp.where(own, lc + bias_c, NEG_INF)
    ln = jnp.sum(qm * ka_rep, axis=-1, keepdims=True) + bias_n
    mx = jnp.maximum(jnp.maximum(jnp.max(lc, axis=-1, keepdims=True), ln), sinks)
    pc = jnp.exp(lc - mx)
    pn = jnp.exp(ln - mx)
    den = jnp.sum(pc, axis=-1, keepdims=True) + pn + jnp.exp(sinks - mx)
    o = (jnp.dot(pc, v_cat, precision=HI, preferred_element_type=F32) + pn * va_rep) / den
    att_ref[...] = jnp.concatenate(
        [o[h * tb:(h + 1) * tb, (h // A_GROUP) * A_HEAD_DIM:(h // A_GROUP + 1) * A_HEAD_DIM]
         for h in range(A_HEADS)], axis=1)

    for i in range(tb):
        nk_ref[i] = jnp.concatenate([ck_ref[i][1:], ka_all[i:i + 1]], axis=0)
        nv_ref[i] = jnp.concatenate([cv_ref[i][1:], va_all[i:i + 1]], axis=0)

    qk_pre = z_all[:, Z_QK:Z_QK + 2 * B_QK]
    hist = [cb_ref[i] for i in range(tb)]
    conv = qk_pre * cw[B_CONV - 1:B_CONV]
    for j in range(B_CONV - 1):
        conv = conv + jnp.concatenate([hs[j:j + 1] for hs in hist], axis=0) * cw[j:j + 1]
    for i in range(tb):
        ncb_ref[i] = jnp.concatenate([hist[i][1:], qk_pre[i:i + 1]], axis=0)
    qk = conv * jax.nn.sigmoid(conv)
    q_m = qk[:, :B_QK]
    k_m = qk[:, B_QK:] * (B_DK ** -0.5)
    v_m = z_all[:, Z_VB:Z_VB + B_V]
    og = z_all[:, Z_OG:Z_OG + B_V]
    G = z_all[:, Z_GATES:Z_GATES + LANES] + bg_ref[...]
    ig = G[:, :B_HEADS]
    a = _log_sigmoid(G)[:, B_HEADS:2 * B_HEADS] + m0_ref[...]
    m = jnp.maximum(a, ig)
    dw = jnp.exp(ig - m)
    aw = jnp.exp(a - m)
    li = lax.broadcasted_iota(I32, (B_QK, LANES), 0) >> (B_DK.bit_length() - 1)
    ind = (li == lax.broadcasted_iota(I32, (B_QK, LANES), 1)).astype(F32)
    n0_m = jnp.concatenate([jnp.concatenate([n0_ref[i, h:h + 1, :] for h in range(B_HEADS)], axis=1)
                            for i in range(tb)], axis=0)
    qk_dot = jnp.dot(q_m * k_m, ind, precision=HI, preferred_element_type=F32)[:, :B_HEADS]
    qn_dot = jnp.dot(q_m * n0_m, ind, precision=HI, preferred_element_type=F32)[:, :B_HEADS]
    s = qk_dot * dw
    inv = 1.0 / jnp.maximum(jnp.abs(s + aw * qn_dot), jnp.exp(-m))
    m1_ref[...] = jnp.concatenate([m, jnp.zeros((tb, LANES - B_HEADS), F32)], axis=1)
    stack = jnp.concatenate([q_m[:, h * B_DK:(h + 1) * B_DK] for h in range(B_HEADS)]
                            + [k_m[:, h * B_DK:(h + 1) * B_DK] for h in range(B_HEADS)], axis=0)
    cols = jnp.concatenate([stack, jnp.zeros_like(stack)], axis=1).T[:B_DK]
    hh_rows = [[] for _ in range(B_HEADS)]
    for i in range(tb):
        for h in range(B_HEADS):
            jq = h * tb + i
            jk = (B_HEADS + h) * tb + i
            c0 = c0_ref[i, h]
            vh = v_m[i:i + 1, h * B_DV:(h + 1) * B_DV]
            aw_p, dw_p = aw[i:i + 1, h:h + 1], dw[i:i + 1, h:h + 1]
            num = s[i:i + 1, h:h + 1] * vh + aw_p * jnp.sum(cols[:, jq:jq + 1] * c0, axis=0, keepdims=True)
            hh_rows[h].append(num * inv[i:i + 1, h:h + 1])
            c1_ref[i, h] = aw_p * c0 + dw_p * (cols[:, jk:jk + 1] * vh)
            n1_ref[i, h:h + 1, :] = aw_p * n0_ref[i, h:h + 1, :] + dw_p * k_m[i:i + 1, h * B_DK:(h + 1) * B_DK]
    obs = []
    for h in range(B_HEADS):
        hh = jnp.concatenate(hh_rows[h], axis=0)
        hn = hh * lax.rsqrt(jnp.mean(hh * hh, axis=-1, keepdims=True) + EPS) * gn[:, h * B_DV:(h + 1) * B_DV]
        obs.append(jax.nn.sigmoid(og[:, h * B_DV:(h + 1) * B_DV]) * hn)
    ob_ref[...] = jnp.concatenate(obs, axis=1)


def mix_sample(z, rel_table, sinks, ck, cv, c0, n0, m0, conv_buf, conv_w, b_gates_pad, g_norm):
    nb = z.shape[0]
    tb = min(SAMPLE_TILE, nb)
    n_buf = ck.shape[1]
    bkt = _t5_buckets(n_buf - np.arange(n_buf))[None, :]
    smem = pl.BlockSpec(memory_space=pltpu.SMEM)
    const = lambda shape: pl.BlockSpec(shape, lambda i: (0,) * len(shape))
    lead = lambda shape: pl.BlockSpec((tb,) + shape, lambda i: (i,) + (0,) * len(shape))
    return pl.pallas_call(
        functools.partial(_mix_sample_body, tb=tb),
        grid=(nb // tb,),
        in_specs=[smem, smem, const((1, n_buf)), lead((Z_COLS,)), lead((n_buf, A_KV)), lead((n_buf, A_KV)),
                  lead((B_HEADS, B_DK, B_DV)), lead((B_HEADS, B_DK)), lead((B_HEADS,)),
                  lead((B_CONV - 1, 2 * B_QK)), const((B_CONV, 2 * B_QK)), const((1, LANES)), const((1, B_V))],
        out_specs=[lead((A_Q,)), lead((B_V,)), lead((n_buf, A_KV)), lead((n_buf, A_KV)),
                   lead((B_HEADS, B_DK, B_DV)), lead((B_HEADS, B_DK)), lead((LANES,)),
                   lead((B_CONV - 1, 2 * B_QK))],
        out_shape=[jax.ShapeDtypeStruct((nb, A_Q), F32), jax.ShapeDtypeStruct((nb, B_V), F32),
                   jax.ShapeDtypeStruct(ck.shape, F32), jax.ShapeDtypeStruct(cv.shape, F32),
                   jax.ShapeDtypeStruct(c0.shape, F32), jax.ShapeDtypeStruct(n0.shape, F32),
                   jax.ShapeDtypeStruct((nb, LANES), F32), jax.ShapeDtypeStruct(conv_buf.shape, F32)],
        compiler_params=_cparams("parallel"),
        name="mix_sample",
    )(rel_table, sinks, jnp.asarray(bkt), z, ck, cv, c0, n0, m0, conv_buf, conv_w, b_gates_pad, g_norm)


def _store_row_tiles(ref, val, rows):
    for s in range(ROW_TILES):
        ref[pl.ds(s, rows, stride=ROW_TILES), :] = val[:, s * LANES:(s + 1) * LANES]


def _load_row_tiles(ref, rows, start=0, stride=ROW_TILES):
    return jnp.concatenate([ref[pl.ds(start + s, rows, stride=stride), :] for s in range(ROW_TILES)], axis=1)


def _route_sort(logits):
    tm = logits.shape[0]
    lt = logits.T
    big = jnp.int32(1 << 20)
    n_rows = SUBLANES * ((N_GROUPS + N_EXPERTS + SUBLANES - 1) // SUBLANES)
    row = lax.broadcasted_iota(I32, (n_rows, tm), 0)
    x = lt[:n_rows]
    gl = jnp.where(row < N_GROUPS, x, NEG_INF)
    gmax = jnp.max(gl, axis=0, keepdims=True)
    gidx = jnp.min(jnp.where(gl == gmax, row, big), axis=0, keepdims=True)
    g_gate = 1.0 / jnp.sum(jnp.exp(gl - gmax), axis=0, keepdims=True)
    lo = N_GROUPS + gidx * EXPERTS_PER_GROUP
    el = jnp.where((row >= lo) & (row < lo + EXPERTS_PER_GROUP), x, NEG_INF)
    v1 = jnp.max(el, axis=0, keepdims=True)
    i1 = jnp.min(jnp.where(el == v1, row, big), axis=0, keepdims=True)
    el2 = jnp.where(row == i1, NEG_INF, el)
    v2 = jnp.max(el2, axis=0, keepdims=True)
    i2 = jnp.min(jnp.where(el2 == v2, row, big), axis=0, keepdims=True)
    t = jnp.exp(v2 - v1)
    w1 = g_gate / (1.0 + t)
    w2 = g_gate * t / (1.0 + t)
    e0 = i1 - N_GROUPS
    e1 = i2 - N_GROUPS

    erow = lax.broadcasted_iota(I32, (N_EXPERTS, tm), 0)
    oh0 = erow == e0
    oh1 = erow == e1
    r = lax.broadcasted_iota(I32, (tm, tm), 0)
    c = lax.broadcasted_iota(I32, (tm, tm), 1)
    earlier = (r < c).astype(BF16)
    rank0 = jnp.dot(oh0.astype(BF16), earlier, preferred_element_type=F32)
    rank1 = jnp.dot(oh1.astype(BF16), earlier, preferred_element_type=F32)
    cnt0 = jnp.sum(oh0.astype(F32), axis=1, keepdims=True)
    run_n = cnt0 + jnp.sum(oh1.astype(F32), axis=1, keepdims=True)
    slot_n = jnp.floor((run_n + (RUN_SLOT - 1)) * (1.0 / RUN_SLOT)) * RUN_SLOT
    er = lax.broadcasted_iota(I32, (N_EXPERTS, N_EXPERTS), 0)
    ec = lax.broadcasted_iota(I32, (N_EXPERTS, N_EXPERTS), 1)
    run_l = jnp.dot((ec < er).astype(F32), jnp.broadcast_to(slot_n, (N_EXPERTS, LANES)), precision=HI,
                    preferred_element_type=F32)[:, 0:1]
    p0 = jnp.sum(jnp.where(oh0, rank0 + run_l, 0.0), axis=0, keepdims=True)
    p1 = jnp.sum(jnp.where(oh1, rank1 + run_l + cnt0, 0.0), axis=0, keepdims=True)
    pad = jnp.zeros((SUBLANES - TOP_K, tm), F32)
    lane_e = lax.broadcasted_iota(I32, (N_EXPERTS, LANES), 1)
    run_row = jnp.sum(jnp.where(lane_e == lax.broadcasted_iota(I32, (N_EXPERTS, LANES), 0),
                                jnp.broadcast_to(run_n, (N_EXPERTS, LANES)), 0.0), axis=0, keepdims=True)
    return jnp.concatenate([p0, p1, pad], axis=0), jnp.concatenate([w1, w2, pad], axis=0), run_row


MOE_CHUNK = 512
RUN_SLOT = 32


def _moe_chunk(rows, precise):
    return min(rows, 128 if precise else MOE_CHUNK)


def _proj_router_body(*refs, n_in, has_bias, precise, tm):
    a_refs = refs[:n_in]
    w_refs = refs[n_in:2 * n_in]
    k = 2 * n_in
    bias_ref = refs[k] if has_bias else None
    k += 1 if has_bias else 0
    x_ref, g_ref, wr_ref, br_ref, x1_ref, h8_ref, lpos_ref, gate_ref, runn_ref = refs[k:]
    acc = x_ref[...]
    if has_bias:
        acc = acc + bias_ref[...]
    for a_ref, w_ref in zip(a_refs, w_refs):
        acc = acc + _mm(a_ref[...], w_ref[...], precise)
    x1_ref[...] = acc
    h = _rms(acc, g_ref[...])
    _store_row_tiles(h8_ref, h, tm)
    wr = wr_ref[...]
    if precise:
        logits = jnp.dot(h, wr, precision=HI, preferred_element_type=F32)
    else:
        h_hi = h.astype(BF16)
        h_lo = (h - h_hi.astype(F32)).astype(BF16)
        w_hi = wr.astype(BF16)
        w_lo = (wr - w_hi.astype(F32)).astype(BF16)
        logits = (jnp.dot(h_hi, w_hi, preferred_element_type=F32) + jnp.dot(h_lo, w_hi, preferred_element_type=F32)
                  + jnp.dot(h_hi, w_lo, preferred_element_type=F32))
    lpos, gate, run_n = _route_sort(logits + br_ref[...])
    lpos_ref[...] = (lpos * ROW_TILES).astype(I32)
    gate_ref[...] = gate
    runn_ref[...] = jnp.broadcast_to(run_n, runn_ref.shape).astype(I32)


def proj_router(a_list, w_list, bias, x, g, wr, br, *, precise):
    rows, d = x.shape
    tm = _moe_chunk(rows, precise)
    n_in = len(a_list)
    row_spec = lambda width: pl.BlockSpec((tm, width), lambda i: (i, 0))
    const = lambda shape: pl.BlockSpec(shape, lambda i: (0,) * len(shape))
    in_specs = [row_spec(a.shape[1]) for a in a_list] + [const(w.shape) for w in w_list]
    args = list(a_list) + list(w_list)
    if bias is not None:
        in_specs.append(const((1, d)))
        args.append(bias)
    in_specs += [row_spec(d), const((1, d)), const((d, LANES)), const((1, LANES))]
    args += [x, g, wr, br]
    return pl.pallas_call(
        functools.partial(_proj_router_body, n_in=n_in, has_bias=bias is not None, precise=precise, tm=tm),
        grid=(rows // tm,),
        in_specs=in_specs,
        out_specs=[row_spec(d), pl.BlockSpec((tm * ROW_TILES, LANES), lambda i: (i, 0)),
                   pl.BlockSpec((SUBLANES, tm), lambda i: (i, 0)), pl.BlockSpec((SUBLANES, tm), lambda i: (i, 0)),
                   pl.BlockSpec((SUBLANES, LANES), lambda i: (i, 0))],
        out_shape=[jax.ShapeDtypeStruct((rows, d), F32), jax.ShapeDtypeStruct((rows * ROW_TILES, LANES), F32),
                   jax.ShapeDtypeStruct((rows // tm * SUBLANES, tm), I32),
                   jax.ShapeDtypeStruct((rows // tm * SUBLANES, tm), F32),
                   jax.ShapeDtypeStruct((rows // tm * SUBLANES, LANES), I32)],
        compiler_params=_cparams("parallel"),
        name="proj_router",
    )(*args)


def _rows_at(offset):
    return pl.ds(pl.multiple_of(offset, ROW_TILES), ROW_TILES)


def _tile_rows(r, n=1):
    return pl.ds(pl.multiple_of(r * ROW_TILES, ROW_TILES), n * ROW_TILES)


def _pow2_pieces(limit):
    p = 1
    while p * 2 <= limit:
        p *= 2
    out = []
    while p >= 1:
        out.append(p)
        p //= 2
    return out


COMMON_PIECE = 32


def _for_each_piece(n, pieces, fn):
    def emit(ps):
        for p in ps:
            @pl.when((n & p) != 0)
            def _(p=p):
                fn(n & ~(2 * p - 1), p)

    big = [p for p in pieces if p > COMMON_PIECE]
    if big:
        @pl.when(n > 2 * COMMON_PIECE - 1)
        def _():
            emit(big)
    emit([p for p in pieces if p <= COMMON_PIECE])


def _piece_rows(first_row, i):
    return _tile_rows(first_row + i * RUN_SLOT, RUN_SLOT)


def _dispatch_body(rg_ref, sl_ref, np_ref, npc_ref, ps_ref, pn_ref, tail_ref, lpos_ref, h8_ref, xs_hbm,
                   stage, zbuf, sem, zsem, *, chunk, nch, blk):
    c = pl.program_id(0)
    slot = c % 2
    pad_pieces = _pow2_pieces(blk + RUN_SLOT - 1)

    def wait_pieces(count, buf):
        def one(i, carry):
            pltpu.make_async_copy(stage.at[buf, _tile_rows(0, RUN_SLOT)], stage.at[buf, _tile_rows(0, RUN_SLOT)],
                                  sem).wait()
            return carry
        lax.fori_loop(0, count, one, 0)

    @pl.when(c == 0)
    def _():
        stage[...] = jnp.zeros(stage.shape, F32)

    def copy_tok(t, carry):
        row = h8_ref[_tile_rows(t), :]
        stage[slot, _rows_at(lpos_ref[0, 0, t]), :] = row
        stage[slot, _rows_at(lpos_ref[0, 0, chunk + t]), :] = row
        return carry
    lax.fori_loop(0, chunk, copy_tok, 0, unroll=8)

    @pl.when(c > 0)
    def _():
        wait_pieces(npc_ref[jnp.maximum(c - 1, 0)], 1 - slot)

    def send_runs(e, carry):
        k = c * N_EXPERTS + e

        def one(i, carry2):
            pltpu.make_async_copy(stage.at[slot, _piece_rows(sl_ref[k], i)], xs_hbm.at[_piece_rows(rg_ref[k], i)],
                                  sem).start()
            return carry2
        lax.fori_loop(0, np_ref[k], one, 0)
        return carry
    lax.fori_loop(0, N_EXPERTS, send_runs, 0)

    @pl.when(c == nch - 1)
    def _():
        wait_pieces(npc_ref[c], slot)
        zbuf[...] = jnp.zeros(zbuf.shape, F32)

        def pad_dmas(e, op):
            def one(off, p):
                cp = pltpu.make_async_copy(zbuf.at[_tile_rows(0, p)], xs_hbm.at[_tile_rows(ps_ref[e] + off, p)], zsem)
                cp.start() if op == 0 else cp.wait()
            _for_each_piece(pn_ref[e], pad_pieces, one)

        def issue(e, carry):
            pad_dmas(e, 0)
            return carry

        def wait(e, carry):
            pad_dmas(e, 1)
            return carry
        lax.fori_loop(0, N_EXPERTS, issue, 0)
        lax.fori_loop(0, N_EXPERTS, wait, 0)

        half = blk // 2

        def tail_dmas(i, op):
            cp = pltpu.make_async_copy(zbuf.at[_tile_rows(0, half)],
                                       xs_hbm.at[_tile_rows(tail_ref[0] + i * half, half)], zsem)
            cp.start() if op == 0 else cp.wait()

        def tail_issue(i, carry):
            tail_dmas(i, 0)
            return carry

        def tail_wait(i, carry):
            tail_dmas(i, 1)
            return carry
        lax.fori_loop(0, tail_ref[1], tail_issue, 0)
        lax.fori_loop(0, tail_ref[1], tail_wait, 0)


def _stage_rows(chunk):
    return TOP_K * chunk + N_EXPERTS * (RUN_SLOT - 1) // RUN_SLOT * RUN_SLOT + RUN_SLOT


def dispatch(h8, plan):
    chunk, nch, blk = plan['chunk'], plan['nch'], plan['blk']
    n_slots = plan['nblk'] * blk
    gs = pltpu.PrefetchScalarGridSpec(
        num_scalar_prefetch=7,
        grid=(nch,),
        in_specs=[pl.BlockSpec((1, 1, TOP_K * chunk), lambda c, *_: (c, 0, 0), memory_space=pltpu.SMEM),
                  pl.BlockSpec((chunk * ROW_TILES, LANES), lambda c, *_: (c, 0))],
        out_specs=pl.BlockSpec(memory_space=pl.ANY),
        scratch_shapes=[pltpu.VMEM((2, _stage_rows(chunk) * ROW_TILES, LANES), F32),
                        pltpu.VMEM((blk * ROW_TILES, LANES), F32),
                        pltpu.SemaphoreType.DMA(()), pltpu.SemaphoreType.DMA(())],
    )
    return pl.pallas_call(
        functools.partial(_dispatch_body, chunk=chunk, nch=nch, blk=blk),
        grid_spec=gs,
        out_shape=jax.ShapeDtypeStruct((n_slots * ROW_TILES, LANES), F32),
        compiler_params=_cparams("arbitrary"),
        name="dispatch",
    )(plan['run_g'], plan['slot_l'], plan['n_piece'], plan['n_piece_chunk'], plan['pad_start'], plan['pad_n'],
      plan['tail'], plan['lpos'], h8)


def _experts_body(be_ref, nv_ref, pe_ref, xs_ref, wg_ref, wu_ref, wd_ref, ys_ref, xb, *wscr, precise, blk, layer):
    j = pl.program_id(0)
    nv = nv_ref[0]
    slot = j % 2
    k = j - 1

    @pl.when(j == 0)
    def _():
        xb[1] = jnp.zeros(xb.shape[1:], xb.dtype)

    if not precise:
        wgb, wub, wdb, wgf, wuf, wdf, wsem = wscr

        def weight_copies(e, ws):
            return [pltpu.make_async_copy(src.at[layer, e], dst.at[ws], wsem.at[ws])
                    for src, dst in ((wg_ref, wgf), (wu_ref, wuf), (wd_ref, wdf))]

        @pl.when(j == 0)
        def _():
            for cp in weight_copies(be_ref[0], 0):
                cp.start()

        kc = jnp.clip(k, 0, nv - 1)
        e = be_ref[kc]
        ws = pe_ref[N_EXPERTS + e]

        @pl.when((k >= 0) & (k < nv) & ((k == 0) | (e != be_ref[jnp.maximum(kc - 1, 0)])))
        def _():
            for cp in weight_copies(e, ws):
                cp.wait()
            wgb[...] = wgf[ws].astype(BF16)
            wub[...] = wuf[ws].astype(BF16)
            wdb[...] = wdf[ws].astype(BF16)
            nxt = pe_ref[e]

            @pl.when(nxt < nv)
            def _():
                for cp in weight_copies(be_ref[jnp.minimum(nxt, nv - 1)], 1 - ws):
                    cp.start()

    @pl.when(j <= nv)
    def _():
        xb[slot] = _load_row_tiles(xs_ref, blk).astype(xb.dtype)
        if precise:
            wg, wu, wd = wg_ref[0], wu_ref[0], wd_ref[0]
        else:
            wg, wu, wd = wgb, wub, wdb
        xm = xb[1 - slot]
        gt = _mm(xm, wg[...], precise)
        up = _mm(xm, wu[...], precise)
        _store_row_tiles(ys_ref, _mm(gt * jax.nn.sigmoid(gt) * up, wd[...], precise), blk)

    @pl.when(j > nv)
    def _():
        ys_ref[...] = jnp.zeros(ys_ref.shape, F32)


def experts(xs, plan, wg, wu, wd, layer, *, precise):
    nblk, rows = plan['nblk'], plan['blk'] * ROW_TILES
    d, ff = wg.shape[2], wg.shape[3]
    blk = lambda j, be, nv, pe: (jnp.minimum(j, nv[0] - 1), 0)
    if precise:
        wspec = lambda shape: pl.BlockSpec((None, 1) + shape,
                                           lambda j, be, nv, pe: (layer, be[jnp.clip(j - 1, 0, nv[0] - 1)], 0, 0))
        wspecs = [wspec((d, ff)), wspec((d, ff)), wspec((ff, d))]
        wscratch = []
    else:
        wspecs = [pl.BlockSpec(memory_space=pl.ANY)] * 3
        wscratch = [pltpu.VMEM((d, ff), BF16), pltpu.VMEM((d, ff), BF16), pltpu.VMEM((ff, d), BF16),
                    pltpu.VMEM((2, d, ff), F32), pltpu.VMEM((2, d, ff), F32), pltpu.VMEM((2, ff, d), F32),
                    pltpu.SemaphoreType.DMA((2,))]
    gs = pltpu.PrefetchScalarGridSpec(
        num_scalar_prefetch=3,
        grid=(nblk + 1,),
        in_specs=[pl.BlockSpec((rows, LANES), blk)] + wspecs,
        out_specs=pl.BlockSpec((rows, LANES), lambda j, be, nv, pe: (jnp.maximum(j - 1, 0), 0)),
        scratch_shapes=[pltpu.VMEM((2, plan['blk'], d), F32 if precise else BF16)] + wscratch,
    )
    return pl.pallas_call(
        functools.partial(_experts_body, precise=precise, blk=plan['blk'], layer=layer),
        grid_spec=gs,
        out_shape=jax.ShapeDtypeStruct(xs.shape, F32),
        compiler_params=_cparams("arbitrary"),
        name="experts",
    )(plan['block_e'], plan['n_used'], plan['expert_tab'], xs, wg, wu, wd)


def moe_plan(lpos8, gate8, runn8, blk):
    nch = runn8.shape[0] // SUBLANES
    chunk = lpos8.shape[1]
    n_assign = nch * chunk * TOP_K
    nblk = (n_assign + N_EXPERTS * (blk - 1 + RUN_SLOT) + blk - 1) // blk
    run_n = runn8.reshape(nch, SUBLANES, LANES)[:, 0, :N_EXPERTS]
    per_chunk = lambda a: a.reshape(nch, 1, SUBLANES * chunk)[:, :, :TOP_K * chunk]
    counts = jnp.sum(run_n, axis=0)
    padded = (counts + RUN_SLOT + blk - 1) // blk * blk
    pends = jnp.cumsum(padded)
    pstarts = pends - padded
    run_g = pstarts[None, :] + jnp.cumsum(run_n, axis=0) - run_n
    n_piece = (run_n + RUN_SLOT - 1) // RUN_SLOT
    slot_l = (jnp.cumsum(n_piece, axis=1) - n_piece) * RUN_SLOT
    blk_start = jnp.arange(nblk, dtype=I32) * blk
    block_e = jnp.minimum(jnp.sum((pends[None, :] <= blk_start[:, None]).astype(I32), axis=1), N_EXPERTS - 1)
    return dict(chunk=chunk, nch=nch, nblk=nblk, blk=blk,
                run_g=run_g.reshape(-1).astype(I32), slot_l=slot_l.reshape(-1).astype(I32),
                n_piece=n_piece.reshape(-1).astype(I32), n_piece_chunk=jnp.sum(n_piece, axis=1).astype(I32),
                pad_start=(pstarts + counts).astype(I32), pad_n=(padded - counts).astype(I32),
                lpos=per_chunk(lpos8), gate=per_chunk(gate8), block_e=block_e.astype(I32),
                expert_tab=jnp.concatenate([pends // blk, jnp.arange(N_EXPERTS, dtype=I32) % 2]).astype(I32),
                n_used=(pends[-1:] // blk).astype(I32),
                tail=jnp.stack([pends[-1], 2 * (nblk - pends[-1] // blk)]).astype(I32))


def _combine(rg_ref, sl_ref, np_ref, npc_ref, lpos_ref, gate_ref, x_ref, ys_hbm, ystage, comb, sem, *, chunk, nch):
    c = pl.program_id(0)
    slot = c % 2

    def fetch(cc, sl):
        def per_e(e, carry):
            k = cc * N_EXPERTS + e

            def one(i, carry2):
                pltpu.make_async_copy(ys_hbm.at[_piece_rows(rg_ref[k], i)], ystage.at[sl, _piece_rows(sl_ref[k], i)],
                                      sem.at[sl]).start()
                return carry2
            lax.fori_loop(0, np_ref[k], one, 0)
            return carry
        lax.fori_loop(0, N_EXPERTS, per_e, 0)

    @pl.when(c == 0)
    def _():
        fetch(0, 0)

    @pl.when(c + 1 < nch)
    def _():
        fetch(c + 1, 1 - slot)

    def wait_piece(i, carry):
        pltpu.make_async_copy(ystage.at[slot, _tile_rows(0, RUN_SLOT)], ystage.at[slot, _tile_rows(0, RUN_SLOT)],
                              sem.at[slot]).wait()
        return carry
    lax.fori_loop(0, npc_ref[c], wait_piece, 0)

    def per_tok(t, carry):
        y0 = ystage[slot, _rows_at(lpos_ref[0, 0, t]), :]
        y1 = ystage[slot, _rows_at(lpos_ref[0, 0, chunk + t]), :]
        comb[_tile_rows(t), :] = gate_ref[0, 0, t] * y0 + gate_ref[0, 0, chunk + t] * y1
        return carry
    lax.fori_loop(0, chunk, per_tok, 0, unroll=8)
    return x_ref[...] + _load_row_tiles(comb, chunk)


def _combine_glu_body(rg_ref, sl_ref, np_ref, npc_ref, lpos_ref, gate_ref, x_ref, ys_hbm, g_ref, w_ref, b_ref,
                      x2_ref, u_ref, ystage, comb, sem, *, chunk, nch, precise):
    x2 = _combine(rg_ref, sl_ref, np_ref, npc_ref, lpos_ref, gate_ref, x_ref, ys_hbm, ystage, comb, sem,
                  chunk=chunk, nch=nch)
    x2_ref[...] = x2
    zz = _mm(_rms(x2, g_ref[...]), w_ref[...], precise) + b_ref[...]
    half = zz.shape[1] // 2
    u_ref[...] = zz[:, :half] * jax.nn.sigmoid(zz[:, half:])


def _combine_final_body(rg_ref, sl_ref, np_ref, npc_ref, lpos_ref, gate_ref, x_ref, ys_hbm, g_ref, o_ref,
                        ystage, comb, sem, *, chunk, nch):
    x2 = _combine(rg_ref, sl_ref, np_ref, npc_ref, lpos_ref, gate_ref, x_ref, ys_hbm, ystage, comb, sem,
                  chunk=chunk, nch=nch)
    o_ref[...] = _rms(x2, g_ref[...])


def _combine_call(body, plan, x, ys, extra, extra_specs, out_specs, out_shape, name):
    chunk, nch = plan['chunk'], plan['nch']
    d = x.shape[1]
    smem_blk = pl.BlockSpec((1, 1, TOP_K * chunk), lambda c, *_: (c, 0, 0), memory_space=pltpu.SMEM)
    gs = pltpu.PrefetchScalarGridSpec(
        num_scalar_prefetch=4,
        grid=(nch,),
        in_specs=[smem_blk, smem_blk, pl.BlockSpec((chunk, d), lambda c, *_: (c, 0)),
                  pl.BlockSpec(memory_space=pl.ANY)] + extra_specs,
        out_specs=out_specs,
        scratch_shapes=[pltpu.VMEM((2, _stage_rows(chunk) * ROW_TILES, LANES), F32),
                        pltpu.VMEM((chunk * ROW_TILES, LANES), F32), pltpu.SemaphoreType.DMA((2,))],
    )
    return pl.pallas_call(
        functools.partial(body, chunk=chunk, nch=nch),
        grid_spec=gs,
        out_shape=out_shape,
        compiler_params=_cparams("arbitrary"),
        name=name,
    )(plan['run_g'], plan['slot_l'], plan['n_piece'], plan['n_piece_chunk'], plan['lpos'], plan['gate'], x, ys,
      *extra)


def combine_glu(x, ys, plan, g, w, b, *, precise):
    rows, d = x.shape
    chunk = plan['chunk']
    cols = w.shape[1]
    const = lambda shape: pl.BlockSpec(shape, lambda c, *_: (0,) * len(shape))
    row_spec = lambda width: pl.BlockSpec((chunk, width), lambda c, *_: (c, 0))
    return _combine_call(
        functools.partial(_combine_glu_body, precise=precise), plan, x, ys, [g, w, b],
        [const((1, d)), const((d, cols)), const((1, cols))], [row_spec(d), row_spec(cols // 2)],
        [jax.ShapeDtypeStruct((rows, d), F32), jax.ShapeDtypeStruct((rows, cols // 2), F32)], "combine_glu")


def combine_final(x, ys, plan, g):
    rows, d = x.shape
    chunk = plan['chunk']
    return _combine_call(
        _combine_final_body, plan, x, ys, [g], [pl.BlockSpec((1, d), lambda c, *_: (0, 0))],
        pl.BlockSpec((chunk, d), lambda c, *_: (c, 0)), jax.ShapeDtypeStruct((rows, d), F32), "combine_final")


CONV_TILE = 512
CONV_HIST = 32


def _ln_swish(y, g, b):
    yc = y - jnp.mean(y, axis=-1, keepdims=True)
    yn = yc * lax.rsqrt(jnp.mean(yc * yc, axis=-1, keepdims=True) + EPS) * g + b
    return yn * jax.nn.sigmoid(yn)


CONV_ROWS = 64
LN_ROWS = 16
LN_UNROLL = 16


def _dwconv_prompt_body(u_ref, w_ref, bdw_ref, g_ref, b_ref, o_ref, ext, y_sc, *, tt):
    t = pl.program_id(1)
    n_lt = ext.shape[0]

    @pl.when(t == 0)
    def _():
        ext[:, 0:CONV_HIST, :] = jnp.zeros((n_lt, CONV_HIST, LANES), F32)

    @pl.when(t > 0)
    def _():
        ext[:, 0:CONV_HIST, :] = ext[:, tt:tt + CONV_HIST, :]

    for j in range(n_lt):
        ext[j, CONV_HIST:CONV_HIST + tt, :] = u_ref[:, j * LANES:(j + 1) * LANES]
    off = CONV_HIST - (C_KERNEL - 1)
    for j in range(n_lt):
        wj = w_ref[:, j * LANES:(j + 1) * LANES]
        bj = bdw_ref[:, j * LANES:(j + 1) * LANES]
        for c in range(tt // CONV_ROWS):
            acc = ext[j, pl.ds(off + c * CONV_ROWS, CONV_ROWS), :] * wj[0:1] + bj
            for k in range(1, C_KERNEL):
                acc = acc + ext[j, pl.ds(off + k + c * CONV_ROWS, CONV_ROWS), :] * wj[k:k + 1]
            y_sc[c * CONV_ROWS:(c + 1) * CONV_ROWS, j * LANES:(j + 1) * LANES] = acc

    def ln_rows(r, carry):
        rows = pl.ds(pl.multiple_of(r * LN_ROWS, LN_ROWS), LN_ROWS)
        o_ref[rows, :] = _ln_swish(y_sc[rows, :], g_ref[...], b_ref[...]).astype(o_ref.dtype)
        return carry
    lax.fori_loop(0, tt // LN_ROWS, ln_rows, 0, unroll=LN_UNROLL)


def dwconv_prompt(u, w, b_dw, ln_g, ln_b, bsz, seq):
    tt = min(CONV_TILE, seq)
    nt = seq // tt
    d = u.shape[1]
    const = lambda shape: pl.BlockSpec(shape, lambda b, t: (0,) * len(shape))
    return pl.pallas_call(
        functools.partial(_dwconv_prompt_body, tt=tt),
        grid=(bsz, nt),
        in_specs=[pl.BlockSpec((tt, d), lambda b, t: (b * nt + t, 0)), const((C_KERNEL, d)), const((1, d)),
                  const((1, d)), const((1, d))],
        out_specs=pl.BlockSpec((tt, d), lambda b, t: (b * nt + t, 0)),
        out_shape=jax.ShapeDtypeStruct((bsz * seq, d), BF16),
        scratch_shapes=[pltpu.VMEM((d // LANES, CONV_HIST + tt, LANES), F32), pltpu.VMEM((tt, d), F32)],
        compiler_params=_cparams("arbitrary", "arbitrary"),
        name="dwconv_prompt",
    )(u, w, b_dw, ln_g, ln_b)


def _dwconv_sample_body(u_ref, buf_ref, w_ref, bdw_ref, g_ref, b_ref, o_ref, nbuf_ref, *, tb):
    w = w_ref[...]
    rows = []
    for i in range(tb):
        hist = buf_ref[i]
        ur = u_ref[i:i + 1, :]
        rows.append(jnp.sum(hist * w[:C_KERNEL - 1], axis=0, keepdims=True) + ur * w[C_KERNEL - 1:C_KERNEL])
        nbuf_ref[i] = jnp.concatenate([hist[1:], ur], axis=0)
    y = jnp.concatenate(rows, axis=0) + bdw_ref[...]
    o_ref[...] = _ln_swish(y, g_ref[...], b_ref[...])


def dwconv_sample(u, buf, w, b_dw, ln_g, ln_b):
    nb, d = u.shape
    tb = min(SAMPLE_TILE, nb)
    const = lambda shape: pl.BlockSpec(shape, lambda i: (0,) * len(shape))
    return pl.pallas_call(
        functools.partial(_dwconv_sample_body, tb=tb),
        grid=(nb // tb,),
        in_specs=[pl.BlockSpec((tb, d), lambda i: (i, 0)), pl.BlockSpec((tb, C_KERNEL - 1, d), lambda i: (i, 0, 0)),
                  const((C_KERNEL, d)), const((1, d)), const((1, d)), const((1, d))],
        out_specs=[pl.BlockSpec((tb, d), lambda i: (i, 0)), pl.BlockSpec((tb, C_KERNEL - 1, d), lambda i: (i, 0, 0))],
        out_shape=[jax.ShapeDtypeStruct((nb, d), F32), jax.ShapeDtypeStruct(buf.shape, F32)],
        compiler_params=_cparams("parallel"),
        name="dwconv_sample",
    )(u, buf, w, b_dw, ln_g, ln_b)


def _moe(h8, lpos8, gate8, runn8, wg, wu, wd, layer, *, blk, precise):
    plan = moe_plan(lpos8, gate8, runn8, blk)
    xs = dispatch(h8, plan)
    return experts(xs, plan, wg, wu, wd, layer, precise=precise), plan


def _trunk(x, caches, p, *, prompt):
    bsz, seq, d = x.shape
    rows = bsz * seq
    precise = not prompt
    wdt = F32 if precise else BF16
    xf = x.reshape(rows, d)
    row = lambda v: v.reshape(1, -1).astype(F32)

    z = norm_proj(xf, row(p['norm_mix'][0]), p['w_in'].astype(wdt), precise=precise)
    if prompt:
        z, zb = z
        att = attn_prompt(z, zb, p['rel_table'], p['sinks'], bsz, seq)
        out_b, c1, n1, m1 = mlstm_prompt(z, zb, p['conv_w'], p['b_gates'], p['g_mnorm'], bsz, seq)
        z3 = z.reshape(bsz, seq, P_COLS)
        new_k = z3[:, seq - WINDOW:, P_KA:P_KA + A_KV].reshape(bsz, WINDOW, A_KV_HEADS, A_HEAD_DIM)
        new_v = z3[:, seq - WINDOW:, P_VA:P_VA + A_KV].reshape(bsz, WINDOW, A_KV_HEADS, A_HEAD_DIM)
        new_conv = z3[:, seq - (B_CONV - 1):, P_QK:P_QK + 2 * B_QK]
        n1 = n1[:, :B_HEADS]
        m1 = m1[:, :B_HEADS, 0]
    else:
        ck, cv, c0, n0, m0, cbuf = caches[:6]
        n_buf = ck.shape[1]
        att, out_b, new_k, new_v, c1, n1, m1, new_conv = mix_sample(
            z, p['rel_table'], p['sinks'], ck.reshape(bsz, n_buf, A_KV), cv.reshape(bsz, n_buf, A_KV),
            c0, n0, m0, cbuf, p['conv_w'], p['b_gates'], p['g_mnorm'])
        new_k = new_k.reshape(bsz, n_buf, A_KV_HEADS, A_HEAD_DIM)
        new_v = new_v.reshape(bsz, n_buf, A_KV_HEADS, A_HEAD_DIM)
        m1 = m1[:, :B_HEADS]
    w_out = p['w_out'].astype(wdt)
    x1, h8, lpos, gate, runn = proj_router([att, out_b], [w_out[:A_Q], w_out[A_Q:]], None, xf, row(p['norm_ffn'][0]),
                                           p['w_router'][0], p['b_router'][0], precise=precise)
    blk = EXPERT_BLOCK_PRECISE if precise else EXPERT_BLOCK
    ys, plan = _moe(h8, lpos, gate, runn, p['w_eg'], p['w_eu'], p['w_ed'], 0, blk=blk, precise=precise)

    x2, u = combine_glu(x1, ys, plan, row(p['norm_mix'][1]), p['w_pw1'].astype(wdt), row(p['b_pw1']),
                        precise=precise)
    if prompt:
        yc = dwconv_prompt(u, p['w_dw'], row(p['b_dw']), row(p['ln_g']), row(p['ln_b']), bsz, seq)
        new_cbuf = u.reshape(bsz, seq, d)[:, seq - (C_KERNEL - 1):]
    else:
        yc, new_cbuf = dwconv_sample(u, caches[6], p['w_dw'], row(p['b_dw']), row(p['ln_g']), row(p['ln_b']))
    x3, h8, lpos, gate, runn = proj_router([yc], [p['w_pw2'].astype(wdt)], row(p['b_pw2']), x2, row(p['norm_ffn'][1]),
                                           p['w_router'][1], p['b_router'][1], precise=precise)
    ys, plan = _moe(h8, lpos, gate, runn, p['w_eg'], p['w_eu'], p['w_ed'], 1, blk=blk, precise=precise)
    y = combine_final(x3, ys, plan, row(p['norm_final']))
    add_layer = lambda t: t[None]
    return (y.reshape(bsz, seq, d),) + tuple(add_layer(t) for t in (new_k, new_v, c1, n1, m1, new_conv, new_cbuf))


def kernel(x_prompt, x_sample, cache_win_k, cache_win_v, state_mlstm_c, state_mlstm_n, state_mlstm_m, state_mlstm_conv, state_conv, norm_mix, norm_ffn, norm_final, rel_bias_table, w_in_mix, b_mlstm_gates, w_mlstm_qk_conv, attn_sinks, g_mlstm_norm, w_out_mix, w_pw1, b_pw1, w_dw, b_dw, ln_conv_g, ln_conv_b, w_pw2, b_pw2, w_router_group, b_router_group, w_router_expert, b_router_expert, w_expert_gate, w_expert_up, w_expert_down):
    w_in = w_in_mix[0]
    s_q, s_k, s_v, s_qk, s_vb, s_g = A_Q, A_Q + A_KV, A_Q + 2 * A_KV, A_Q + 2 * A_KV + 2 * B_QK, \
        A_Q + 2 * A_KV + 2 * B_QK + B_V, A_Q + 2 * A_KV + 2 * B_QK + B_V + 2 * B_HEADS
    w_in_r = jnp.concatenate([w_in[:, :s_q], w_in[:, s_v:s_qk], w_in[:, s_qk:s_vb], w_in[:, s_g:],
                              w_in[:, s_q:s_k], w_in[:, s_k:s_v], w_in[:, s_vb:s_g],
                              jnp.zeros((D_MODEL, LANES - 2 * B_HEADS), F32)], axis=1)
    b_gates = jnp.concatenate([b_mlstm_gates[0], jnp.zeros((LANES - 2 * B_HEADS,), F32)]).reshape(1, LANES)
    depth = w_router_group.shape[0]
    w_re = jnp.transpose(w_router_expert, (0, 2, 1, 3)).reshape(depth, D_MODEL, N_EXPERTS)
    w_router = jnp.concatenate([w_router_group, w_re,
                                jnp.zeros((depth, D_MODEL, LANES - N_GROUPS - N_EXPERTS), F32)], axis=-1)
    b_router = jnp.concatenate([b_router_group, b_router_expert.reshape(depth, N_EXPERTS),
                                jnp.zeros((depth, LANES - N_GROUPS - N_EXPERTS), F32)], axis=-1)[:, None, :]
    p = dict(norm_mix=norm_mix, norm_ffn=norm_ffn, norm_final=norm_final, rel_table=rel_bias_table,
             sinks=attn_sinks[0], w_in=w_in_r, b_gates=b_gates, conv_w=w_mlstm_qk_conv[0],
             g_mnorm=g_mlstm_norm[0].reshape(1, B_V), w_out=w_out_mix[0], w_pw1=w_pw1[0], b_pw1=b_pw1[0],
             w_dw=w_dw[0], b_dw=b_dw[0], ln_g=ln_conv_g[0], ln_b=ln_conv_b[0], w_pw2=w_pw2[0], b_pw2=b_pw2[0],
             w_router=w_router, b_router=b_router, w_eg=w_expert_gate, w_eu=w_expert_up, w_ed=w_expert_down)
    caches = (cache_win_k[0], cache_win_v[0], state_mlstm_c[0], state_mlstm_n[0], state_mlstm_m[0],
              state_mlstm_conv[0], state_conv[0])
    out_p = _trunk(x_prompt, None, p, prompt=True)
    out_s = _trunk(x_sample, caches, p, prompt=False)
    return (out_p[0], out_s[0]) + out_p[1:] + out_s[1:]
```

```python
import functools
import math

import numpy as np
import jax
import jax.numpy as jnp
from jax import lax
from jax.experimental import pallas as pl
from jax.experimental.pallas import tpu as pltpu

F32 = jnp.float32
BF16 = jnp.bfloat16
I32 = jnp.int32
HI = lax.Precision.HIGHEST
NEG_INF = float("-inf")

LANES = 128
SUBLANES = 8
VMEM_LIMIT = 56 * 1024 * 1024

D_MODEL = 1024
A_HEADS = 8
A_KV_HEADS = 2
A_GROUP = A_HEADS // A_KV_HEADS
A_HEAD_DIM = 64
WINDOW = 128
REL_BUCKETS = 32
REL_MAX_DIST = 128
B_HEADS = 4
B_DK = 64
B_DV = 128
B_CONV = 4
C_KERNEL = 31
N_GROUPS = 4
EXPERTS_PER_GROUP = 8
N_EXPERTS = N_GROUPS * EXPERTS_PER_GROUP
TOP_K = 2
EXPERT_FF = D_MODEL // 2
EXPERT_BLOCK = 512
EXPERT_BLOCK_PRECISE = 128
EPS = 1e-6

A_Q = A_HEADS * A_HEAD_DIM
A_KV = A_KV_HEADS * A_HEAD_DIM
B_QK = B_HEADS * B_DK
B_V = B_HEADS * B_DV
ROW_TILES = D_MODEL // LANES

Z_QA, Z_QK, Z_VB, Z_OG, Z_KA, Z_VA, Z_GATES = 0, 512, 1024, 1536, 2048, 2176, 2304
Z_COLS = 2432
MLSTM_CHUNK = 128


def _cparams(*sem):
    return pltpu.CompilerParams(dimension_semantics=sem, vmem_limit_bytes=VMEM_LIMIT)


def _rms(x, g):
    return x * lax.rsqrt(jnp.mean(x * x, axis=-1, keepdims=True) + EPS) * g


def _mm(a, w, precise):
    if not precise:
        return jnp.dot(a.astype(BF16), w, preferred_element_type=F32)
    a = a.astype(F32)
    a_hi = a.astype(BF16)
    a_lo = (a - a_hi.astype(F32)).astype(BF16)
    w_hi = w.astype(BF16)
    w_lo = (w - w_hi.astype(F32)).astype(BF16)
    return (jnp.dot(a_hi, w_hi, preferred_element_type=F32) + jnp.dot(a_lo, w_hi, preferred_element_type=F32)
            + jnp.dot(a_hi, w_lo, preferred_element_type=F32))


def _t5_buckets(dist):
    exact = REL_BUCKETS // 2
    d = np.maximum(dist, 0)
    large = exact + (np.log(np.maximum(d, 1).astype(np.float32) / exact)
                     / math.log(REL_MAX_DIST / exact) * (REL_BUCKETS - exact)).astype(np.int32)
    return np.where(d < exact, d, np.minimum(large, REL_BUCKETS - 1)).astype(np.int32)


P_QK, P_OG, P_KA, P_VA, P_GATES = 0, 512, 1024, 1152, 1280
P_COLS = 1408
PB_QA, PB_VB = 0, 512
PB_COLS = 1024


def _norm_proj_body(x_ref, g_ref, w_ref, *o_refs, precise):
    h = _rms(x_ref[...], g_ref[...])
    z = _mm(h, w_ref[...], precise)
    if len(o_refs) == 1:
        o_refs[0][...] = z
        return
    z32_ref, zb_ref = o_refs
    z32_ref[...] = jnp.concatenate([z[:, Z_QK:Z_QK + 2 * B_QK], z[:, Z_OG:Z_OG + B_V], z[:, Z_KA:Z_GATES + LANES]],
                                   axis=1)
    zb_ref[...] = jnp.concatenate([z[:, Z_QA:Z_QA + A_Q], z[:, Z_VB:Z_VB + B_V]], axis=1).astype(BF16)


def norm_proj(x, g, w, *, precise):
    rows, d = x.shape
    cols = w.shape[1]
    tm = min(rows, 128 if precise else 512)
    row_spec = lambda width: pl.BlockSpec((tm, width), lambda i: (i, 0))
    if precise:
        out_specs, out_shape = row_spec(cols), jax.ShapeDtypeStruct((rows, cols), F32)
    else:
        out_specs = [row_spec(P_COLS), row_spec(PB_COLS)]
        out_shape = [jax.ShapeDtypeStruct((rows, P_COLS), F32), jax.ShapeDtypeStruct((rows, PB_COLS), BF16)]
    return pl.pallas_call(
        functools.partial(_norm_proj_body, precise=precise),
        grid=(rows // tm,),
        in_specs=[row_spec(d), pl.BlockSpec((1, d), lambda i: (0, 0)), pl.BlockSpec((d, cols), lambda i: (0, 0))],
        out_specs=out_specs,
        out_shape=out_shape,
        compiler_params=_cparams("parallel"),
        name="norm_proj",
    )(x, g, w)


def _attn_prompt_body(tab_ref, sink_ref, bkt_ref, q_ref, kp_ref, kc_ref, vp_ref, vc_ref, o_ref, bias_ref, *, nq):
    b = pl.program_id(0)
    n = pl.program_id(1)

    @pl.when((b == 0) & (n == 0))
    def _():
        bk = bkt_ref[...]
        first = lax.broadcasted_iota(I32, bk.shape, 1) >= WINDOW
        for h in range(A_HEADS):
            acc = jnp.full(bk.shape, NEG_INF, F32)
            for t in range(REL_BUCKETS):
                acc = jnp.where(bk == t, tab_ref[t, h], acc)
            bias_ref[h] = acc
            bias_ref[A_HEADS + h] = jnp.where(first, acc, NEG_INF)

    k_all = jnp.concatenate([kp_ref[...], kc_ref[...]], axis=0).astype(BF16)
    v_all = jnp.concatenate([vp_ref[...], vc_ref[...]], axis=0).astype(BF16)
    for sub in range(nq):
        rows = slice(sub * WINDOW, (sub + 1) * WINDOW)
        q = q_ref[rows, :] * (A_HEAD_DIM ** -0.5)
        kb = k_all[sub * WINDOW:(sub + 2) * WINDOW]
        vb = v_all[sub * WINDOW:(sub + 2) * WINDOW]
        table = jnp.where(n == 0, A_HEADS, 0) if sub == 0 else 0
        H = range(A_HEADS)
        nt = (((1,), (1,)), ((), ()))
        kv = [slice((h // A_GROUP) * A_HEAD_DIM, (h // A_GROUP + 1) * A_HEAD_DIM) for h in H]
        s = [lax.dot_general(q[:, h * A_HEAD_DIM:(h + 1) * A_HEAD_DIM].astype(BF16), kb[:, kv[h]], nt,
                             preferred_element_type=F32) + bias_ref[table + h] for h in H]
        mx = [jnp.maximum(jnp.max(s[h], axis=-1, keepdims=True), sink_ref[h]) for h in H]
        p = [jnp.exp(s[h] - mx[h]) for h in H]
        den = [jnp.sum(p[h], axis=-1, keepdims=True) + jnp.exp(sink_ref[h] - mx[h]) for h in H]
        outs = [jnp.dot(p[h].astype(BF16), vb[:, kv[h]], preferred_element_type=F32) / den[h] for h in H]
        o_ref[rows, :] = jnp.concatenate(outs, axis=1).astype(o_ref.dtype)


ATTN_BLOCKS = 2


def attn_prompt(z, zb, rel_table, sinks, bsz, seq):
    nb = seq // WINDOW
    dist = WINDOW + np.arange(WINDOW)[:, None] - np.arange(2 * WINDOW)[None, :]
    bkt = np.where((dist >= 0) & (dist <= WINDOW), _t5_buckets(dist), -1).astype(np.int32)
    kcol, vcol = P_KA // LANES, P_VA // LANES
    smem = pl.BlockSpec(memory_space=pltpu.SMEM)
    nq = ATTN_BLOCKS if nb % ATTN_BLOCKS == 0 else 1
    ns = nb // nq

    def cur(c):
        return pl.BlockSpec((nq * WINDOW, LANES), lambda b, n: (b * ns + n, c))

    def prev(c):
        return pl.BlockSpec((WINDOW, LANES), lambda b, n: (b * nb + jnp.maximum(n * nq - 1, 0), c))

    return pl.pallas_call(
        functools.partial(_attn_prompt_body, nq=nq),
        grid=(bsz, ns),
        in_specs=[smem, smem,
                  pl.BlockSpec((WINDOW, 2 * WINDOW), lambda b, n: (0, 0)),
                  pl.BlockSpec((nq * WINDOW, A_Q), lambda b, n: (b * ns + n, PB_QA // A_Q)),
                  prev(kcol), cur(kcol), prev(vcol), cur(vcol)],
        out_specs=pl.BlockSpec((nq * WINDOW, A_Q), lambda b, n: (b * ns + n, 0)),
        out_shape=jax.ShapeDtypeStruct((bsz * seq, A_Q), BF16),
        scratch_shapes=[pltpu.VMEM((2 * A_HEADS, WINDOW, 2 * WINDOW), F32)],
        compiler_params=_cparams("arbitrary", "arbitrary"),
        name="attn_prompt",
    )(rel_table, sinks, jnp.asarray(bkt), zb, z, z, z, z)


def _log_sigmoid(x):
    return -(jnp.maximum(-x, 0.0) + jnp.log1p(jnp.exp(-jnp.abs(x))))


def _mlstm_prompt_body(qk_ref, v_ref, g_ref, og_ref, cw_ref, bg_ref, gn_ref,
                       ob_ref, c_out, n_out, m_out,
                       c_sc, n_sc, m_sc, hist_sc, *, L, nc, nbat, nsub):
    ci = pl.program_id(1)

    @pl.when(ci == 0)
    def _():
        c_sc[...] = jnp.zeros(c_sc.shape, F32)
        n_sc[...] = jnp.zeros(n_sc.shape, F32)
        m_sc[...] = jnp.full(m_sc.shape, NEG_INF, F32)
        hist_sc[...] = jnp.zeros(hist_sc.shape, F32)

    row = lax.broadcasted_iota(I32, (L, L), 0)
    colm = lax.broadcasted_iota(I32, (L, L), 1)
    causal = colm <= row
    tril = causal.astype(F32)
    for sub in range(nsub):
        rows = pl.ds(sub * L, L)
        stores = []
        for bb in range(nbat):
            stores += _mlstm_chunk(qk_ref.at[bb, rows], v_ref.at[bb, rows], g_ref.at[bb, rows], og_ref.at[bb, rows],
                                   cw_ref, bg_ref, gn_ref, ob_ref.at[bb, rows], c_sc.at[bb], n_sc.at[bb],
                                   m_sc.at[bb], hist_sc.at[bb], causal, tril, L)
        for store in stores:
            store()

    @pl.when(ci == nc - 1)
    def _():
        c_out[...] = c_sc[...]
        n_out[...] = n_sc[...]
        m_out[...] = m_sc[...]


def _mlstm_chunk(qk_ref, v_ref, g_ref, og_ref, cw_ref, bg_ref, gn_ref, ob_ref, c_sc, n_sc, m_sc, hist_sc,
                 causal, tril, L):
    cur = qk_ref[...]
    ext = jnp.concatenate([hist_sc[...], cur], axis=0)
    cw = cw_ref[...]
    off = SUBLANES - (B_CONV - 1)
    conv = ext[off:off + L] * cw[0:1]
    for j in range(1, B_CONV):
        conv = conv + ext[off + j:off + j + L] * cw[j:j + 1]
    qk = conv * jax.nn.sigmoid(conv)
    q_all = qk[:, :B_QK]
    k_all = qk[:, B_QK:] * (B_DK ** -0.5)
    k_t = k_all.T
    v_all = v_ref[...]
    og = og_ref[...]
    gn = gn_ref[...]

    G = g_ref[...] + bg_ref[...]
    lf = _log_sigmoid(G)
    Bc = jnp.dot(tril, lf, precision=HI, preferred_element_type=F32)
    BT = Bc.T
    GT = G.T

    H = range(B_HEADS)
    nt = (((1,), (1,)), ((), ()))
    qh = [q_all[:, h * B_DK:(h + 1) * B_DK] for h in H]
    kh = [k_all[:, h * B_DK:(h + 1) * B_DK] for h in H]
    qb = [q.astype(BF16) for q in qh]
    vb = [v_all[:, h * B_DV:(h + 1) * B_DV].astype(BF16) for h in H]
    b_col = [Bc[:, B_HEADS + h:B_HEADS + h + 1] for h in H]
    c0 = [c_sc[h] for h in H]
    n0 = [n_sc[h:h + 1, :] for h in H]
    a = [b_col[h] + m_sc[h:h + 1, 0:1] for h in H]
    d = [jnp.where(causal, b_col[h] - BT[B_HEADS + h:B_HEADS + h + 1, :] + GT[h:h + 1, :], NEG_INF) for h in H]
    qk_raw = [lax.dot_general(qb[h], kh[h].astype(BF16), nt, preferred_element_type=F32) for h in H]
    qc = [jnp.dot(qb[h], c0[h].astype(BF16), preferred_element_type=F32) for h in H]
    m = [jnp.maximum(a[h], jnp.max(d[h], axis=-1, keepdims=True)) for h in H]
    dw = [jnp.exp(d[h] - m[h]) for h in H]
    aw = [jnp.exp(a[h] - m[h]) for h in H]
    s = [qk_raw[h] * dw[h] for h in H]
    sv = [jnp.dot(s[h].astype(BF16), vb[h], preferred_element_type=F32) for h in H]
    li = lax.broadcasted_iota(I32, (B_QK, LANES), 0) >> (B_DK.bit_length() - 1)
    ind = (li == lax.broadcasted_iota(I32, (B_QK, LANES), 1)).astype(F32)
    qn = jnp.dot(q_all * jnp.concatenate(n0, axis=1), ind, precision=HI, preferred_element_type=F32)
    den = [jnp.sum(s[h], axis=-1, keepdims=True) + aw[h] * qn[:, h:h + 1] for h in H]
    hh = [(sv[h] + aw[h] * qc[h]) / jnp.maximum(jnp.abs(den[h]), jnp.exp(-m[h])) for h in H]
    m_last = [m[h][L - 1:L, :] for h in H]
    wl = [jnp.exp(b_col[h][L - 1:L, :] - b_col[h] + G[:, h:h + 1] - m_last[h]) for h in H]
    decay = [aw[h][L - 1:L, :] for h in H]
    kw_t = [(k_t[h * B_DK:(h + 1) * B_DK, :] * dw[h][L - 1:L, :]).astype(BF16) for h in H]
    c1 = [decay[h] * c0[h] + jnp.dot(kw_t[h], vb[h], preferred_element_type=F32) for h in H]
    n1 = [decay[h] * n0[h] + jnp.sum(kh[h] * wl[h], axis=0, keepdims=True) for h in H]
    stores = [functools.partial(_store_state, c_sc, n_sc, m_sc, h, c1[h], n1[h], m_last[h]) for h in H]
    hn = [hh[h] * lax.rsqrt(jnp.mean(hh[h] * hh[h], axis=-1, keepdims=True) + EPS) * gn[:, h * B_DV:(h + 1) * B_DV]
          for h in H]
    outs = [jax.nn.sigmoid(og[:, h * B_DV:(h + 1) * B_DV]) * hn[h] for h in H]
    out = jnp.concatenate(outs, axis=1).astype(ob_ref.dtype)
    stores.append(functools.partial(_store_chunk, ob_ref, hist_sc, out, cur[L - SUBLANES:L]))
    return stores


def _store_state(c_sc, n_sc, m_sc, h, c1, n1, m_last):
    c_sc[h] = c1
    n_sc[h:h + 1, :] = n1
    m_sc[h:h + 1, :] = jnp.broadcast_to(m_last, (1, LANES))


def _store_chunk(ob_ref, hist_sc, out, tail):
    ob_ref[...] = out
    hist_sc[...] = tail


MLSTM_BATCH = 1
MLSTM_SUBCHUNKS = 1


def mlstm_prompt(z, zb, conv_w, b_gates_pad, g_norm, bsz, seq):
    L = MLSTM_CHUNK
    nsub = MLSTM_SUBCHUNKS if (seq // L) % MLSTM_SUBCHUNKS == 0 else 1
    nc = seq // (L * nsub)
    nbat = MLSTM_BATCH if bsz % MLSTM_BATCH == 0 else 1
    z3 = z.reshape(bsz, seq, P_COLS)
    zb3 = zb.reshape(bsz, seq, PB_COLS)

    def zspec(width, colblk):
        return pl.BlockSpec((nbat, nsub * L, width), lambda b, c: (b, c, colblk))

    const = lambda shape: pl.BlockSpec(shape, lambda b, c: (0,) * len(shape))
    state = lambda shape: pl.BlockSpec((nbat,) + shape, lambda b, c: (b,) + (0,) * len(shape))
    out_b, c1, n1, m1 = pl.pallas_call(
        functools.partial(_mlstm_prompt_body, L=L, nc=nc, nbat=nbat, nsub=nsub),
        grid=(bsz // nbat, nc),
        in_specs=[zspec(2 * B_QK, P_QK // (2 * B_QK)), zspec(B_V, PB_VB // B_V), zspec(LANES, P_GATES // LANES),
                  zspec(B_V, P_OG // B_V), const((B_CONV, 2 * B_QK)), const((1, LANES)), const((1, B_V))],
        out_specs=[pl.BlockSpec((nbat, nsub * L, B_V), lambda b, c: (b, c, 0)),
                   state((B_HEADS, B_DK, B_DV)), state((SUBLANES, B_DK)), state((SUBLANES, LANES))],
        out_shape=[jax.ShapeDtypeStruct((bsz, seq, B_V), BF16),
                   jax.ShapeDtypeStruct((bsz, B_HEADS, B_DK, B_DV), F32),
                   jax.ShapeDtypeStruct((bsz, SUBLANES, B_DK), F32),
                   jax.ShapeDtypeStruct((bsz, SUBLANES, LANES), F32)],
        scratch_shapes=[pltpu.VMEM((nbat, B_HEADS, B_DK, B_DV), F32), pltpu.VMEM((nbat, SUBLANES, B_DK), F32),
                        pltpu.VMEM((nbat, SUBLANES, LANES), F32), pltpu.VMEM((nbat, SUBLANES, 2 * B_QK), F32)],
        compiler_params=_cparams("arbitrary", "arbitrary"),
        name="mlstm_prompt",
    )(z3, zb3, z3, z3, conv_w, b_gates_pad, g_norm)
    return out_b.reshape(bsz * seq, B_V), c1, n1, m1


SAMPLE_TILE = 8


def _mix_sample_body(tab_ref, sink_ref, bkt_ref, z_ref, ck_ref, cv_ref, c0_ref, n0_ref, m0_ref, cb_ref,
                     cw_ref, bg_ref, gn_ref,
                     att_ref, ob_ref, nk_ref, nv_ref, c1_ref, n1_ref, m1_ref, ncb_ref, *, tb):
    bk = bkt_ref[...]
    cw = cw_ref[...]
    gn = gn_ref[...]
    bias_rows = []
    for h in range(A_HEADS):
        bias = jnp.zeros(bk.shape, F32)
        for t in range(REL_BUCKETS):
            bias = jnp.where(bk == t, tab_ref[t, h], bias)
        bias_rows.append(bias)
    n_keys = bk.shape[1]
    rows, cols = A_HEADS * tb, tb * n_keys
    z_all = z_ref[...]
    q_all = z_all[:, Z_QA:Z_QA + A_Q] * (A_HEAD_DIM ** -0.5)
    ka_all = z_all[:, Z_KA:Z_KA + A_KV]
    va_all = z_all[:, Z_VA:Z_VA + A_KV]
    zero_half = jnp.zeros((tb, A_HEAD_DIM), F32)
    q_rows = []
    for h in range(A_HEADS):
        qh = q_all[:, h * A_HEAD_DIM:(h + 1) * A_HEAD_DIM]
        q_rows.append(jnp.concatenate([qh, zero_half] if h // A_GROUP == 0 else [zero_half, qh], axis=1))
    qm = jnp.concatenate(q_rows, axis=0)
    ka_rep = jnp.concatenate([ka_all] * A_HEADS, axis=0)
    va_rep = jnp.concatenate([va_all] * A_HEADS, axis=0)
    k_cat = jnp.concatenate([ck_ref[i] for i in range(tb)], axis=0)
    v_cat = jnp.concatenate([cv_ref[i] for i in range(tb)], axis=0)
    bias_c = jnp.concatenate([jnp.broadcast_to(b, (tb, n_keys)) for b in bias_rows], axis=0)
    bias_c = jnp.concatenate([bias_c] * tb, axis=1)
    r_id = lax.broadcasted_iota(I32, (rows, cols), 0)
    c_id = lax.broadcasted_iota(I32, (rows, cols), 1)
    own = (r_id & (tb - 1)) == (c_id >> (n_keys.bit_length() - 1))
    head_id = lax.broadcasted_iota(I32, (rows, 1), 0) >> (tb.bit_length() - 1)
    bias_n = jnp.zeros((rows, 1), F32)
    sinks = jnp.zeros((rows, 1), F32)
    for h in range(A_HEADS):
        bias_n = jnp.where(head_id == h, tab_ref[0, h], bias_n)
        sinks = jnp.where(head_id == h, sink_ref[h], sinks)
    lc = lax.dot_general(qm, k_cat, (((1,), (1,)), ((), ())), precision=HI, preferred_element_type=F32)
    lc = jnp.where(own, lc + bias_c, NEG_INF)
    ln = jnp.sum(qm * ka_rep, axis=-1, keepdims=True) + bias_n
    mx = jnp.maximum(jnp.maximum(jnp.max(lc, axis=-1, keepdims=True), ln), sinks)
    pc = jnp.exp(lc - mx)
    pn = jnp.exp(ln - mx)
    den = jnp.sum(pc, axis=-1, keepdims=True) + pn + jnp.exp(sinks - mx)
    o = (jnp.dot(pc, v_cat, precision=HI, preferred_element_type=F32) + pn * va_rep) / den
    att_ref[...] = jnp.concatenate(
        [o[h * tb:(h + 1) * tb, (h // A_GROUP) * A_HEAD_DIM:(h // A_GROUP + 1) * A_HEAD_DIM]
         for h in range(A_HEADS)], axis=1)

    for i in range(tb):
        nk_ref[i] = jnp.concatenate([ck_ref[i][1:], ka_all[i:i + 1]], axis=0)
        nv_ref[i] = jnp.concatenate([cv_ref[i][1:], va_all[i:i + 1]], axis=0)

    qk_pre = z_all[:, Z_QK:Z_QK + 2 * B_QK]
    hist = [cb_ref[i] for i in range(tb)]
    conv = qk_pre * cw[B_CONV - 1:B_CONV]
    for j in range(B_CONV - 1):
        conv = conv + jnp.concatenate([hs[j:j + 1] for hs in hist], axis=0) * cw[j:j + 1]
    for i in range(tb):
        ncb_ref[i] = jnp.concatenate([hist[i][1:], qk_pre[i:i + 1]], axis=0)
    qk = conv * jax.nn.sigmoid(conv)
    q_m = qk[:, :B_QK]
    k_m = qk[:, B_QK:] * (B_DK ** -0.5)
    v_m = z_all[:, Z_VB:Z_VB + B_V]
    og = z_all[:, Z_OG:Z_OG + B_V]
    G = z_all[:, Z_GATES:Z_GATES + LANES] + bg_ref[...]
    ig = G[:, :B_HEADS]
    a = _log_sigmoid(G)[:, B_HEADS:2 * B_HEADS] + m0_ref[...]
    m = jnp.maximum(a, ig)
    dw = jnp.exp(ig - m)
    aw = jnp.exp(a - m)
    li = lax.broadcasted_iota(I32, (B_QK, LANES), 0) >> (B_DK.bit_length() - 1)
    ind = (li == lax.broadcasted_iota(I32, (B_QK, LANES), 1)).astype(F32)
    n0_m = jnp.concatenate([jnp.concatenate([n0_ref[i, h:h + 1, :] for h in range(B_HEADS)], axis=1)
                            for i in range(tb)], axis=0)
    qk_dot = jnp.dot(q_m * k_m, ind, precision=HI, preferred_element_type=F32)[:, :B_HEADS]
    qn_dot = jnp.dot(q_m * n0_m, ind, precision=HI, preferred_element_type=F32)[:, :B_HEADS]
    s = qk_dot * dw
    inv = 1.0 / jnp.maximum(jnp.abs(s + aw * qn_dot), jnp.exp(-m))
    m1_ref[...] = jnp.concatenate([m, jnp.zeros((tb, LANES - B_HEADS), F32)], axis=1)
    stack = jnp.concatenate([q_m[:, h * B_DK:(h + 1) * B_DK] for h in range(B_HEADS)]
                            + [k_m[:, h * B_DK:(h + 1) * B_DK] for h in range(B_HEADS)], axis=0)
    cols = jnp.concatenate([stack, jnp.zeros_like(stack)], axis=1).T[:B_DK]
    hh_rows = [[] for _ in range(B_HEADS)]
    for i in range(tb):
        for h in range(B_HEADS):
            jq = h * tb + i
            jk = (B_HEADS + h) * tb + i
            c0 = c0_ref[i, h]
            vh = v_m[i:i + 1, h * B_DV:(h + 1) * B_DV]
            aw_p, dw_p = aw[i:i + 1, h:h + 1], dw[i:i + 1, h:h + 1]
            num = s[i:i + 1, h:h + 1] * vh + aw_p * jnp.sum(cols[:, jq:jq + 1] * c0, axis=0, keepdims=True)
            hh_rows[h].append(num * inv[i:i + 1, h:h + 1])
            c1_ref[i, h] = aw_p * c0 + dw_p * (cols[:, jk:jk + 1] * vh)
            n1_ref[i, h:h + 1, :] = aw_p * n0_ref[i, h:h + 1, :] + dw_p * k_m[i:i + 1, h * B_DK:(h + 1) * B_DK]
    obs = []
    for h in range(B_HEADS):
        hh = jnp.concatenate(hh_rows[h], axis=0)
        hn = hh * lax.rsqrt(jnp.mean(hh * hh, axis=-1, keepdims=True) + EPS) * gn[:, h * B_DV:(h + 1) * B_DV]
        obs.append(jax.nn.sigmoid(og[:, h * B_DV:(h + 1) * B_DV]) * hn)
    ob_ref[...] = jnp.concatenate(obs, axis=1)


def mix_sample(z, rel_table, sinks, ck, cv, c0, n0, m0, conv_buf, conv_w, b_gates_pad, g_norm):
    nb = z.shape[0]
    tb = min(SAMPLE_TILE, nb)
    n_buf = ck.shape[1]
    bkt = _t5_buckets(n_buf - np.arange(n_buf))[None, :]
    smem = pl.BlockSpec(memory_space=pltpu.SMEM)
    const = lambda shape: pl.BlockSpec(shape, lambda i: (0,) * len(shape))
    lead = lambda shape: pl.BlockSpec((tb,) + shape, lambda i: (i,) + (0,) * len(shape))
    return pl.pallas_call(
        functools.partial(_mix_sample_body, tb=tb),
        grid=(nb // tb,),
        in_specs=[smem, smem, const((1, n_buf)), lead((Z_COLS,)), lead((n_buf, A_KV)), lead((n_buf, A_KV)),
                  lead((B_HEADS, B_DK, B_DV)), lead((B_HEADS, B_DK)), lead((B_HEADS,)),
                  lead((B_CONV - 1, 2 * B_QK)), const((B_CONV, 2 * B_QK)), const((1, LANES)), const((1, B_V))],
        out_specs=[lead((A_Q,)), lead((B_V,)), lead((n_buf, A_KV)), lead((n_buf, A_KV)),
                   lead((B_HEADS, B_DK, B_DV)), lead((B_HEADS, B_DK)), lead((LANES,)),
                   lead((B_CONV - 1, 2 * B_QK))],
        out_shape=[jax.ShapeDtypeStruct((nb, A_Q), F32), jax.ShapeDtypeStruct((nb, B_V), F32),
                   jax.ShapeDtypeStruct(ck.shape, F32), jax.ShapeDtypeStruct(cv.shape, F32),
                   jax.ShapeDtypeStruct(c0.shape, F32), jax.ShapeDtypeStruct(n0.shape, F32),
                   jax.ShapeDtypeStruct((nb, LANES), F32), jax.ShapeDtypeStruct(conv_buf.shape, F32)],
        compiler_params=_cparams("parallel"),
        name="mix_sample",
    )(rel_table, sinks, jnp.asarray(bkt), z, ck, cv, c0, n0, m0, conv_buf, conv_w, b_gates_pad, g_norm)


def _store_row_tiles(ref, val, rows):
    for s in range(ROW_TILES):
        ref[pl.ds(s, rows, stride=ROW_TILES), :] = val[:, s * LANES:(s + 1) * LANES]


def _load_row_tiles(ref, rows, start=0, stride=ROW_TILES):
    return jnp.concatenate([ref[pl.ds(start + s, rows, stride=stride), :] for s in range(ROW_TILES)], axis=1)


def _route_sort(logits):
    tm = logits.shape[0]
    lt = logits.T
    big = jnp.int32(1 << 20)
    n_rows = SUBLANES * ((N_GROUPS + N_EXPERTS + SUBLANES - 1) // SUBLANES)
    row = lax.broadcasted_iota(I32, (n_rows, tm), 0)
    x = lt[:n_rows]
    gl = jnp.where(row < N_GROUPS, x, NEG_INF)
    gmax = jnp.max(gl, axis=0, keepdims=True)
    gidx = jnp.min(jnp.where(gl == gmax, row, big), axis=0, keepdims=True)
    g_gate = 1.0 / jnp.sum(jnp.exp(gl - gmax), axis=0, keepdims=True)
    lo = N_GROUPS + gidx * EXPERTS_PER_GROUP
    el = jnp.where((row >= lo) & (row < lo + EXPERTS_PER_GROUP), x, NEG_INF)
    v1 = jnp.max(el, axis=0, keepdims=True)
    i1 = jnp.min(jnp.where(el == v1, row, big), axis=0, keepdims=True)
    el2 = jnp.where(row == i1, NEG_INF, el)
    v2 = jnp.max(el2, axis=0, keepdims=True)
    i2 = jnp.min(jnp.where(el2 == v2, row, big), axis=0, keepdims=True)
    t = jnp.exp(v2 - v1)
    w1 = g_gate / (1.0 + t)
    w2 = g_gate * t / (1.0 + t)
    e0 = i1 - N_GROUPS
    e1 = i2 - N_GROUPS

    erow = lax.broadcasted_iota(I32, (N_EXPERTS, tm), 0)
    oh0 = erow == e0
    oh1 = erow == e1
    r = lax.broadcasted_iota(I32, (tm, tm), 0)
    c = lax.broadcasted_iota(I32, (tm, tm), 1)
    earlier = (r < c).astype(BF16)
    rank0 = jnp.dot(oh0.astype(BF16), earlier, preferred_element_type=F32)
    rank1 = jnp.dot(oh1.astype(BF16), earlier, preferred_element_type=F32)
    cnt0 = jnp.sum(oh0.astype(F32), axis=1, keepdims=True)
    run_n = cnt0 + jnp.sum(oh1.astype(F32), axis=1, keepdims=True)
    slot_n = jnp.floor((run_n + (RUN_SLOT - 1)) * (1.0 / RUN_SLOT)) * RUN_SLOT
    er = lax.broadcasted_iota(I32, (N_EXPERTS, N_EXPERTS), 0)
    ec = lax.broadcasted_iota(I32, (N_EXPERTS, N_EXPERTS), 1)
    run_l = jnp.dot((ec < er).astype(F32), jnp.broadcast_to(slot_n, (N_EXPERTS, LANES)), precision=HI,
                    preferred_element_type=F32)[:, 0:1]
    p0 = jnp.sum(jnp.where(oh0, rank0 + run_l, 0.0), axis=0, keepdims=True)
    p1 = jnp.sum(jnp.where(oh1, rank1 + run_l + cnt0, 0.0), axis=0, keepdims=True)
    pad = jnp.zeros((SUBLANES - TOP_K, tm), F32)
    lane_e = lax.broadcasted_iota(I32, (N_EXPERTS, LANES), 1)
    run_row = jnp.sum(jnp.where(lane_e == lax.broadcasted_iota(I32, (N_EXPERTS, LANES), 0),
                                jnp.broadcast_to(run_n, (N_EXPERTS, LANES)), 0.0), axis=0, keepdims=True)
    gate_col = jnp.concatenate([w1, w2, jnp.zeros((LANES - TOP_K, tm), F32)], axis=0).T
    return jnp.concatenate([p0, p1, pad], axis=0), gate_col, run_row


MOE_CHUNK = 512
RUN_SLOT = 32


def _moe_chunk(rows, precise):
    return min(rows, 128 if precise else MOE_CHUNK)


def _proj_router_body(*refs, n_in, has_bias, precise, tm):
    a_refs = refs[:n_in]
    w_refs = refs[n_in:2 * n_in]
    k = 2 * n_in
    bias_ref = refs[k] if has_bias else None
    k += 1 if has_bias else 0
    x_ref, g_ref, wr_ref, br_ref, x1_ref, h8_ref, lpos_ref, gate_ref, runn_ref = refs[k:]
    acc = x_ref[...]
    if has_bias:
        acc = acc + bias_ref[...]
    for a_ref, w_ref in zip(a_refs, w_refs):
        acc = acc + _mm(a_ref[...], w_ref[...], precise)
    x1_ref[...] = acc
    h = _rms(acc, g_ref[...])
    _store_row_tiles(h8_ref, h, tm)
    wr = wr_ref[...]
    if precise:
        logits = jnp.dot(h, wr, precision=HI, preferred_element_type=F32)
    else:
        h_hi = h.astype(BF16)
        h_lo = (h - h_hi.astype(F32)).astype(BF16)
        w_hi = wr.astype(BF16)
        w_lo = (wr - w_hi.astype(F32)).astype(BF16)
        logits = (jnp.dot(h_hi, w_hi, preferred_element_type=F32) + jnp.dot(h_lo, w_hi, preferred_element_type=F32)
                  + jnp.dot(h_hi, w_lo, preferred_element_type=F32))
    lpos, gate, run_n = _route_sort(logits + br_ref[...])
    lpos_ref[...] = (lpos * ROW_TILES).astype(I32)
    gate_ref[...] = gate
    runn_ref[...] = jnp.broadcast_to(run_n, runn_ref.shape).astype(I32)


def proj_router(a_list, w_list, bias, x, g, wr, br, *, precise):
    rows, d = x.shape
    tm = _moe_chunk(rows, precise)
    n_in = len(a_list)
    row_spec = lambda width: pl.BlockSpec((tm, width), lambda i: (i, 0))
    const = lambda shape: pl.BlockSpec(shape, lambda i: (0,) * len(shape))
    in_specs = [row_spec(a.shape[1]) for a in a_list] + [const(w.shape) for w in w_list]
    args = list(a_list) + list(w_list)
    if bias is not None:
        in_specs.append(const((1, d)))
        args.append(bias)
    in_specs += [row_spec(d), const((1, d)), const((d, LANES)), const((1, LANES))]
    args += [x, g, wr, br]
    return pl.pallas_call(
        functools.partial(_proj_router_body, n_in=n_in, has_bias=bias is not None, precise=precise, tm=tm),
        grid=(rows // tm,),
        in_specs=in_specs,
        out_specs=[row_spec(d), pl.BlockSpec((tm * ROW_TILES, LANES), lambda i: (i, 0)),
                   pl.BlockSpec((SUBLANES, tm), lambda i: (i, 0)), row_spec(LANES),
                   pl.BlockSpec((SUBLANES, LANES), lambda i: (i, 0))],
        out_shape=[jax.ShapeDtypeStruct((rows, d), F32), jax.ShapeDtypeStruct((rows * ROW_TILES, LANES), F32),
                   jax.ShapeDtypeStruct((rows // tm * SUBLANES, tm), I32),
                   jax.ShapeDtypeStruct((rows, LANES), F32),
                   jax.ShapeDtypeStruct((rows // tm * SUBLANES, LANES), I32)],
        compiler_params=_cparams("parallel"),
        name="proj_router",
    )(*args)


def _rows_at(offset):
    return pl.ds(pl.multiple_of(offset, ROW_TILES), ROW_TILES)


def _tile_rows(r, n=1):
    return pl.ds(pl.multiple_of(r * ROW_TILES, ROW_TILES), n * ROW_TILES)


def _pow2_pieces(limit):
    p = 1
    while p * 2 <= limit:
        p *= 2
    out = []
    while p >= 1:
        out.append(p)
        p //= 2
    return out


COMMON_PIECE = 32


def _for_each_piece(n, pieces, fn):
    def emit(ps):
        for p in ps:
            @pl.when((n & p) != 0)
            def _(p=p):
                fn(n & ~(2 * p - 1), p)

    big = [p for p in pieces if p > COMMON_PIECE]
    if big:
        @pl.when(n > 2 * COMMON_PIECE - 1)
        def _():
            emit(big)
    emit([p for p in pieces if p <= COMMON_PIECE])


def _piece_rows(first_row, i):
    return _tile_rows(first_row + i * RUN_SLOT, RUN_SLOT)


def _dispatch_body(rg_ref, sl_ref, np_ref, npc_ref, ps_ref, pn_ref, tail_ref, lpos_ref, h8_ref, xs_hbm,
                   stage, zbuf, sem, zsem, *, chunk, nch, blk):
    c = pl.program_id(0)
    slot = c % 2
    pad_pieces = _pow2_pieces(blk + RUN_SLOT - 1)

    def wait_pieces(count, buf):
        def one(i, carry):
            pltpu.make_async_copy(stage.at[buf, _tile_rows(0, RUN_SLOT)], stage.at[buf, _tile_rows(0, RUN_SLOT)],
                                  sem).wait()
            return carry
        lax.fori_loop(0, count, one, 0)

    @pl.when(c == 0)
    def _():
        stage[...] = jnp.zeros(stage.shape, F32)

    def copy_tok(t, carry):
        row = h8_ref[_tile_rows(t), :]
        stage[slot, _rows_at(lpos_ref[0, 0, t]), :] = row
        stage[slot, _rows_at(lpos_ref[0, 0, chunk + t]), :] = row
        return carry
    lax.fori_loop(0, chunk, copy_tok, 0, unroll=8)

    @pl.when(c > 0)
    def _():
        wait_pieces(npc_ref[jnp.maximum(c - 1, 0)], 1 - slot)

    def send_runs(e, carry):
        k = c * N_EXPERTS + e

        def one(i, carry2):
            pltpu.make_async_copy(stage.at[slot, _piece_rows(sl_ref[k], i)], xs_hbm.at[_piece_rows(rg_ref[k], i)],
                                  sem).start()
            return carry2
        lax.fori_loop(0, np_ref[k], one, 0)
        return carry
    lax.fori_loop(0, N_EXPERTS, send_runs, 0)

    @pl.when(c == nch - 1)
    def _():
        wait_pieces(npc_ref[c], slot)
        zbuf[...] = jnp.zeros(zbuf.shape, F32)

        def pad_dmas(e, op):
            def one(off, p):
                cp = pltpu.make_async_copy(zbuf.at[_tile_rows(0, p)], xs_hbm.at[_tile_rows(ps_ref[e] + off, p)], zsem)
                cp.start() if op == 0 else cp.wait()
            _for_each_piece(pn_ref[e], pad_pieces, one)

        def issue(e, carry):
            pad_dmas(e, 0)
            return carry

        def wait(e, carry):
            pad_dmas(e, 1)
            return carry
        lax.fori_loop(0, N_EXPERTS, issue, 0)
        lax.fori_loop(0, N_EXPERTS, wait, 0)

        half = blk // 2

        def tail_dmas(i, op):
            cp = pltpu.make_async_copy(zbuf.at[_tile_rows(0, half)],
                                       xs_hbm.at[_tile_rows(tail_ref[0] + i * half, half)], zsem)
            cp.start() if op == 0 else cp.wait()

        def tail_issue(i, carry):
            tail_dmas(i, 0)
            return carry

        def tail_wait(i, carry):
            tail_dmas(i, 1)
            return carry
        lax.fori_loop(0, tail_ref[1], tail_issue, 0)
        lax.fori_loop(0, tail_ref[1], tail_wait, 0)


def _stage_rows(chunk):
    return TOP_K * chunk + N_EXPERTS * (RUN_SLOT - 1) // RUN_SLOT * RUN_SLOT + RUN_SLOT


def dispatch(h8, plan):
    chunk, nch, blk = plan['chunk'], plan['nch'], plan['blk']
    n_slots = plan['nblk'] * blk
    gs = pltpu.PrefetchScalarGridSpec(
        num_scalar_prefetch=7,
        grid=(nch,),
        in_specs=[pl.BlockSpec((1, 1, TOP_K * chunk), lambda c, *_: (c, 0, 0), memory_space=pltpu.SMEM),
                  pl.BlockSpec((chunk * ROW_TILES, LANES), lambda c, *_: (c, 0))],
        out_specs=pl.BlockSpec(memory_space=pl.ANY),
        scratch_shapes=[pltpu.VMEM((2, _stage_rows(chunk) * ROW_TILES, LANES), F32),
                        pltpu.VMEM((blk * ROW_TILES, LANES), F32),
                        pltpu.SemaphoreType.DMA(()), pltpu.SemaphoreType.DMA(())],
    )
    return pl.pallas_call(
        functools.partial(_dispatch_body, chunk=chunk, nch=nch, blk=blk),
        grid_spec=gs,
        out_shape=jax.ShapeDtypeStruct((n_slots * ROW_TILES, LANES), F32),
        compiler_params=_cparams("arbitrary"),
        name="dispatch",
    )(plan['run_g'], plan['slot_l'], plan['n_piece'], plan['n_piece_chunk'], plan['pad_start'], plan['pad_n'],
      plan['tail'], plan['lpos'], h8)


def _experts_body(be_ref, nv_ref, pe_ref, xs_ref, wg_ref, wu_ref, wd_ref, ys_ref, xb, *wscr, precise, blk, layer):
    j = pl.program_id(0)
    nv = nv_ref[0]
    slot = j % 2
    k = j - 1

    @pl.when(j == 0)
    def _():
        xb[1] = jnp.zeros(xb.shape[1:], xb.dtype)

    if not precise:
        wgb, wub, wdb, wgf, wuf, wdf, wsem = wscr

        def weight_copies(e, ws):
            return [pltpu.make_async_copy(src.at[layer, e], dst.at[ws], wsem.at[ws])
                    for src, dst in ((wg_ref, wgf), (wu_ref, wuf), (wd_ref, wdf))]

        @pl.when(j == 0)
        def _():
            for cp in weight_copies(be_ref[0], 0):
                cp.start()

        kc = jnp.clip(k, 0, nv - 1)
        e = be_ref[kc]
        ws = pe_ref[N_EXPERTS + e]

        @pl.when((k >= 0) & (k < nv) & ((k == 0) | (e != be_ref[jnp.maximum(kc - 1, 0)])))
        def _():
            for cp in weight_copies(e, ws):
                cp.wait()
            wgb[...] = wgf[ws].astype(BF16)
            wub[...] = wuf[ws].astype(BF16)
            wdb[...] = wdf[ws].astype(BF16)
            nxt = pe_ref[e]

            @pl.when(nxt < nv)
            def _():
                for cp in weight_copies(be_ref[jnp.minimum(nxt, nv - 1)], 1 - ws):
                    cp.start()

    @pl.when(j <= nv)
    def _():
        xb[slot] = _load_row_tiles(xs_ref, blk).astype(xb.dtype)
        if precise:
            wg, wu, wd = wg_ref[0], wu_ref[0], wd_ref[0]
        else:
            wg, wu, wd = wgb, wub, wdb
        xm = xb[1 - slot]
        gt = _mm(xm, wg[...], precise)
        up = _mm(xm, wu[...], precise)
        _store_row_tiles(ys_ref, _mm(gt * jax.nn.sigmoid(gt) * up, wd[...], precise), blk)

    @pl.when(j > nv)
    def _():
        ys_ref[...] = jnp.zeros(ys_ref.shape, F32)


def experts(xs, plan, wg, wu, wd, layer, *, precise):
    nblk, rows = plan['nblk'], plan['blk'] * ROW_TILES
    d, ff = wg.shape[2], wg.shape[3]
    blk = lambda j, be, nv, pe: (jnp.minimum(j, nv[0] - 1), 0)
    if precise:
        wspec = lambda shape: pl.BlockSpec((None, 1) + shape,
                                           lambda j, be, nv, pe: (layer, be[jnp.clip(j - 1, 0, nv[0] - 1)], 0, 0))
        wspecs = [wspec((d, ff)), wspec((d, ff)), wspec((ff, d))]
        wscratch = []
    else:
        wspecs = [pl.BlockSpec(memory_space=pl.ANY)] * 3
        wscratch = [pltpu.VMEM((d, ff), BF16), pltpu.VMEM((d, ff), BF16), pltpu.VMEM((ff, d), BF16),
                    pltpu.VMEM((2, d, ff), F32), pltpu.VMEM((2, d, ff), F32), pltpu.VMEM((2, ff, d), F32),
                    pltpu.SemaphoreType.DMA((2,))]
    gs = pltpu.PrefetchScalarGridSpec(
        num_scalar_prefetch=3,
        grid=(nblk + 1,),
        in_specs=[pl.BlockSpec((rows, LANES), blk)] + wspecs,
        out_specs=pl.BlockSpec((rows, LANES), lambda j, be, nv, pe: (jnp.maximum(j - 1, 0), 0)),
        scratch_shapes=[pltpu.VMEM((2, plan['blk'], d), F32 if precise else BF16)] + wscratch,
    )
    return pl.pallas_call(
        functools.partial(_experts_body, precise=precise, blk=plan['blk'], layer=layer),
        grid_spec=gs,
        out_shape=jax.ShapeDtypeStruct(xs.shape, F32),
        compiler_params=_cparams("arbitrary"),
        name="experts",
    )(plan['block_e'], plan['n_used'], plan['expert_tab'], xs, wg, wu, wd)


def moe_plan(lpos8, gate8, runn8, blk):
    nch = runn8.shape[0] // SUBLANES
    chunk = lpos8.shape[1]
    n_assign = nch * chunk * TOP_K
    nblk = (n_assign + N_EXPERTS * (blk - 1 + RUN_SLOT) + blk - 1) // blk
    run_n = runn8.reshape(nch, SUBLANES, LANES)[:, 0, :N_EXPERTS]
    per_chunk = lambda a: a.reshape(nch, 1, SUBLANES * chunk)[:, :, :TOP_K * chunk]
    counts = jnp.sum(run_n, axis=0)
    padded = (counts + RUN_SLOT + blk - 1) // blk * blk
    pends = jnp.cumsum(padded)
    pstarts = pends - padded
    run_g = pstarts[None, :] + jnp.cumsum(run_n, axis=0) - run_n
    n_piece = (run_n + RUN_SLOT - 1) // RUN_SLOT
    slot_l = (jnp.cumsum(n_piece, axis=1) - n_piece) * RUN_SLOT
    blk_start = jnp.arange(nblk, dtype=I32) * blk
    block_e = jnp.minimum(jnp.sum((pends[None, :] <= blk_start[:, None]).astype(I32), axis=1), N_EXPERTS - 1)
    return dict(chunk=chunk, nch=nch, nblk=nblk, blk=blk,
                run_g=run_g.reshape(-1).astype(I32), slot_l=slot_l.reshape(-1).astype(I32),
                n_piece=n_piece.reshape(-1).astype(I32), n_piece_chunk=jnp.sum(n_piece, axis=1).astype(I32),
                pad_start=(pstarts + counts).astype(I32), pad_n=(padded - counts).astype(I32),
                lpos=per_chunk(lpos8), gate=gate8, block_e=block_e.astype(I32),
                expert_tab=jnp.concatenate([pends // blk, jnp.arange(N_EXPERTS, dtype=I32) % 2]).astype(I32),
                n_used=(pends[-1:] // blk).astype(I32),
                tail=jnp.stack([pends[-1], 2 * (nblk - pends[-1] // blk)]).astype(I32))


def _combine(rg_ref, sl_ref, np_ref, npc_ref, lpos_ref, gate_ref, x_ref, ys_hbm, ystage, comb, sem, *, chunk, nch):
    c = pl.program_id(0)
    slot = c % 2

    def fetch(cc, sl):
        def per_e(e, carry):
            k = cc * N_EXPERTS + e

            def one(i, carry2):
                pltpu.make_async_copy(ys_hbm.at[_piece_rows(rg_ref[k], i)], ystage.at[sl, _piece_rows(sl_ref[k], i)],
                                      sem.at[sl]).start()
                return carry2
            lax.fori_loop(0, np_ref[k], one, 0)
            return carry
        lax.fori_loop(0, N_EXPERTS, per_e, 0)

    @pl.when(c == 0)
    def _():
        fetch(0, 0)

    @pl.when(c + 1 < nch)
    def _():
        fetch(c + 1, 1 - slot)

    def wait_piece(i, carry):
        pltpu.make_async_copy(ystage.at[slot, _tile_rows(0, RUN_SLOT)], ystage.at[slot, _tile_rows(0, RUN_SLOT)],
                              sem.at[slot]).wait()
        return carry
    lax.fori_loop(0, npc_ref[c], wait_piece, 0)

    def per_tok(t, carry):
        y0 = ystage[slot, _rows_at(lpos_ref[0, 0, t]), :]
        y1 = ystage[slot, _rows_at(lpos_ref[0, 0, chunk + t]), :]
        comb[0, _tile_rows(t), :] = y0
        comb[1, _tile_rows(t), :] = y1
        return carry
    lax.fori_loop(0, chunk, per_tok, 0, unroll=8)
    gate = gate_ref[...]
    return x_ref[...] + (gate[:, 0:1] * _load_row_tiles(comb.at[0], chunk)
                         + gate[:, 1:2] * _load_row_tiles(comb.at[1], chunk))


def _combine_glu_body(rg_ref, sl_ref, np_ref, npc_ref, lpos_ref, gate_ref, x_ref, ys_hbm, g_ref, w_ref, b_ref,
                      x2_ref, u_ref, ystage, comb, sem, *, chunk, nch, precise):
    x2 = _combine(rg_ref, sl_ref, np_ref, npc_ref, lpos_ref, gate_ref, x_ref, ys_hbm, ystage, comb, sem,
                  chunk=chunk, nch=nch)
    x2_ref[...] = x2
    zz = _mm(_rms(x2, g_ref[...]), w_ref[...], precise) + b_ref[...]
    half = zz.shape[1] // 2
    u_ref[...] = zz[:, :half] * jax.nn.sigmoid(zz[:, half:])


def _combine_final_body(rg_ref, sl_ref, np_ref, npc_ref, lpos_ref, gate_ref, x_ref, ys_hbm, g_ref, o_ref,
                        ystage, comb, sem, *, chunk, nch):
    x2 = _combine(rg_ref, sl_ref, np_ref, npc_ref, lpos_ref, gate_ref, x_ref, ys_hbm, ystage, comb, sem,
                  chunk=chunk, nch=nch)
    o_ref[...] = _rms(x2, g_ref[...])


def _combine_call(body, plan, x, ys, extra, extra_specs, out_specs, out_shape, name):
    chunk, nch = plan['chunk'], plan['nch']
    d = x.shape[1]
    smem_blk = pl.BlockSpec((1, 1, TOP_K * chunk), lambda c, *_: (c, 0, 0), memory_space=pltpu.SMEM)
    gs = pltpu.PrefetchScalarGridSpec(
        num_scalar_prefetch=4,
        grid=(nch,),
        in_specs=[smem_blk, pl.BlockSpec((chunk, LANES), lambda c, *_: (c, 0)),
                  pl.BlockSpec((chunk, d), lambda c, *_: (c, 0)),
                  pl.BlockSpec(memory_space=pl.ANY)] + extra_specs,
        out_specs=out_specs,
        scratch_shapes=[pltpu.VMEM((2, _stage_rows(chunk) * ROW_TILES, LANES), F32),
                        pltpu.VMEM((TOP_K, chunk * ROW_TILES, LANES), F32), pltpu.SemaphoreType.DMA((2,))],
    )
    return pl.pallas_call(
        functools.partial(body, chunk=chunk, nch=nch),
        grid_spec=gs,
        out_shape=out_shape,
        compiler_params=_cparams("arbitrary"),
        name=name,
    )(plan['run_g'], plan['slot_l'], plan['n_piece'], plan['n_piece_chunk'], plan['lpos'], plan['gate'], x, ys,
      *extra)


def combine_glu(x, ys, plan, g, w, b, *, precise):
    rows, d = x.shape
    chunk = plan['chunk']
    cols = w.shape[1]
    const = lambda shape: pl.BlockSpec(shape, lambda c, *_: (0,) * len(shape))
    row_spec = lambda width: pl.BlockSpec((chunk, width), lambda c, *_: (c, 0))
    return _combine_call(
        functools.partial(_combine_glu_body, precise=precise), plan, x, ys, [g, w, b],
        [const((1, d)), const((d, cols)), const((1, cols))], [row_spec(d), row_spec(cols // 2)],
        [jax.ShapeDtypeStruct((rows, d), F32), jax.ShapeDtypeStruct((rows, cols // 2), F32)], "combine_glu")


def combine_final(x, ys, plan, g):
    rows, d = x.shape
    chunk = plan['chunk']
    return _combine_call(
        _combine_final_body, plan, x, ys, [g], [pl.BlockSpec((1, d), lambda c, *_: (0, 0))],
        pl.BlockSpec((chunk, d), lambda c, *_: (c, 0)), jax.ShapeDtypeStruct((rows, d), F32), "combine_final")


CONV_TILE = 512
CONV_HIST = 32


def _ln_swish(y, g, b):
    yc = y - jnp.mean(y, axis=-1, keepdims=True)
    yn = yc * lax.rsqrt(jnp.mean(yc * yc, axis=-1, keepdims=True) + EPS) * g + b
    return yn * jax.nn.sigmoid(yn)


CONV_ROWS = 64
LN_ROWS = 16
LN_UNROLL = 16


def _dwconv_prompt_body(u_ref, w_ref, bdw_ref, g_ref, b_ref, o_ref, ext, y_sc, *, tt):
    t = pl.program_id(1)
    n_lt = ext.shape[0]

    @pl.when(t == 0)
    def _():
        ext[:, 0:CONV_HIST, :] = jnp.zeros((n_lt, CONV_HIST, LANES), F32)

    @pl.when(t > 0)
    def _():
        ext[:, 0:CONV_HIST, :] = ext[:, tt:tt + CONV_HIST, :]

    for j in range(n_lt):
        ext[j, CONV_HIST:CONV_HIST + tt, :] = u_ref[:, j * LANES:(j + 1) * LANES]
    off = CONV_HIST - (C_KERNEL - 1)
    for j in range(n_lt):
        wj = w_ref[:, j * LANES:(j + 1) * LANES]
        bj = bdw_ref[:, j * LANES:(j + 1) * LANES]
        for c in range(tt // CONV_ROWS):
            acc = ext[j, pl.ds(off + c * CONV_ROWS, CONV_ROWS), :] * wj[0:1] + bj
            for k in range(1, C_KERNEL):
                acc = acc + ext[j, pl.ds(off + k + c * CONV_ROWS, CONV_ROWS), :] * wj[k:k + 1]
            y_sc[c * CONV_ROWS:(c + 1) * CONV_ROWS, j * LANES:(j + 1) * LANES] = acc

    def ln_rows(r, carry):
        rows = pl.ds(pl.multiple_of(r * LN_ROWS, LN_ROWS), LN_ROWS)
        o_ref[rows, :] = _ln_swish(y_sc[rows, :], g_ref[...], b_ref[...]).astype(o_ref.dtype)
        return carry
    lax.fori_loop(0, tt // LN_ROWS, ln_rows, 0, unroll=LN_UNROLL)


def dwconv_prompt(u, w, b_dw, ln_g, ln_b, bsz, seq):
    tt = min(CONV_TILE, seq)
    nt = seq // tt
    d = u.shape[1]
    const = lambda shape: pl.BlockSpec(shape, lambda b, t: (0,) * len(shape))
    return pl.pallas_call(
        functools.partial(_dwconv_prompt_body, tt=tt),
        grid=(bsz, nt),
        in_specs=[pl.BlockSpec((tt, d), lambda b, t: (b * nt + t, 0)), const((C_KERNEL, d)), const((1, d)),
                  const((1, d)), const((1, d))],
        out_specs=pl.BlockSpec((tt, d), lambda b, t: (b * nt + t, 0)),
        out_shape=jax.ShapeDtypeStruct((bsz * seq, d), BF16),
        scratch_shapes=[pltpu.VMEM((d // LANES, CONV_HIST + tt, LANES), F32), pltpu.VMEM((tt, d), F32)],
        compiler_params=_cparams("arbitrary", "arbitrary"),
        name="dwconv_prompt",
    )(u, w, b_dw, ln_g, ln_b)


def _dwconv_sample_body(u_ref, buf_ref, w_ref, bdw_ref, g_ref, b_ref, o_ref, nbuf_ref, *, tb):
    w = w_ref[...]
    rows = []
    for i in range(tb):
        hist = buf_ref[i]
        ur = u_ref[i:i + 1, :]
        rows.append(jnp.sum(hist * w[:C_KERNEL - 1], axis=0, keepdims=True) + ur * w[C_KERNEL - 1:C_KERNEL])
        nbuf_ref[i] = jnp.concatenate([hist[1:], ur], axis=0)
    y = jnp.concatenate(rows, axis=0) + bdw_ref[...]
    o_ref[...] = _ln_swish(y, g_ref[...], b_ref[...])


def dwconv_sample(u, buf, w, b_dw, ln_g, ln_b):
    nb, d = u.shape
    tb = min(SAMPLE_TILE, nb)
    const = lambda shape: pl.BlockSpec(shape, lambda i: (0,) * len(shape))
    return pl.pallas_call(
        functools.partial(_dwconv_sample_body, tb=tb),
        grid=(nb // tb,),
        in_specs=[pl.BlockSpec((tb, d), lambda i: (i, 0)), pl.BlockSpec((tb, C_KERNEL - 1, d), lambda i: (i, 0, 0)),
                  const((C_KERNEL, d)), const((1, d)), const((1, d)), const((1, d))],
        out_specs=[pl.BlockSpec((tb, d), lambda i: (i, 0)), pl.BlockSpec((tb, C_KERNEL - 1, d), lambda i: (i, 0, 0))],
        out_shape=[jax.ShapeDtypeStruct((nb, d), F32), jax.ShapeDtypeStruct(buf.shape, F32)],
        compiler_params=_cparams("parallel"),
        name="dwconv_sample",
    )(u, buf, w, b_dw, ln_g, ln_b)


def _moe(h8, lpos8, gate8, runn8, wg, wu, wd, layer, *, blk, precise):
    plan = moe_plan(lpos8, gate8, runn8, blk)
    xs = dispatch(h8, plan)
    return experts(xs, plan, wg, wu, wd, layer, precise=precise), plan


def _trunk(x, caches, p, *, prompt):
    bsz, seq, d = x.shape
    rows = bsz * seq
    precise = not prompt
    wdt = F32 if precise else BF16
    xf = x.reshape(rows, d)
    row = lambda v: v.reshape(1, -1).astype(F32)

    z = norm_proj(xf, row(p['norm_mix'][0]), p['w_in'].astype(wdt), precise=precise)
    if prompt:
        z, zb = z
        att = attn_prompt(z, zb, p['rel_table'], p['sinks'], bsz, seq)
        out_b, c1, n1, m1 = mlstm_prompt(z, zb, p['conv_w'], p['b_gates'], p['g_mnorm'], bsz, seq)
        z3 = z.reshape(bsz, seq, P_COLS)
        new_k = z3[:, seq - WINDOW:, P_KA:P_KA + A_KV].reshape(bsz, WINDOW, A_KV_HEADS, A_HEAD_DIM)
        new_v = z3[:, seq - WINDOW:, P_VA:P_VA + A_KV].reshape(bsz, WINDOW, A_KV_HEADS, A_HEAD_DIM)
        new_conv = z3[:, seq - (B_CONV - 1):, P_QK:P_QK + 2 * B_QK]
        n1 = n1[:, :B_HEADS]
        m1 = m1[:, :B_HEADS, 0]
    else:
        ck, cv, c0, n0, m0, cbuf = caches[:6]
        n_buf = ck.shape[1]
        att, out_b, new_k, new_v, c1, n1, m1, new_conv = mix_sample(
            z, p['rel_table'], p['sinks'], ck.reshape(bsz, n_buf, A_KV), cv.reshape(bsz, n_buf, A_KV),
            c0, n0, m0, cbuf, p['conv_w'], p['b_gates'], p['g_mnorm'])
        new_k = new_k.reshape(bsz, n_buf, A_KV_HEADS, A_HEAD_DIM)
        new_v = new_v.reshape(bsz, n_buf, A_KV_HEADS, A_HEAD_DIM)
        m1 = m1[:, :B_HEADS]
    w_out = p['w_out'].astype(wdt)
    x1, h8, lpos, gate, runn = proj_router([att, out_b], [w_out[:A_Q], w_out[A_Q:]], None, xf, row(p['norm_ffn'][0]),
                                           p['w_router'][0], p['b_router'][0], precise=precise)
    blk = EXPERT_BLOCK_PRECISE if precise else EXPERT_BLOCK
    ys, plan = _moe(h8, lpos, gate, runn, p['w_eg'], p['w_eu'], p['w_ed'], 0, blk=blk, precise=precise)

    x2, u = combine_glu(x1, ys, plan, row(p['norm_mix'][1]), p['w_pw1'].astype(wdt), row(p['b_pw1']),
                        precise=precise)
    if prompt:
        yc = dwconv_prompt(u, p['w_dw'], row(p['b_dw']), row(p['ln_g']), row(p['ln_b']), bsz, seq)
        new_cbuf = u.reshape(bsz, seq, d)[:, seq - (C_KERNEL - 1):]
    else:
        yc, new_cbuf = dwconv_sample(u, caches[6], p['w_dw'], row(p['b_dw']), row(p['ln_g']), row(p['ln_b']))
    x3, h8, lpos, gate, runn = proj_router([yc], [p['w_pw2'].astype(wdt)], row(p['b_pw2']), x2, row(p['norm_ffn'][1]),
                                           p['w_router'][1], p['b_router'][1], precise=precise)
    ys, plan = _moe(h8, lpos, gate, runn, p['w_eg'], p['w_eu'], p['w_ed'], 1, blk=blk, precise=precise)
    y = combine_final(x3, ys, plan, row(p['norm_final']))
    add_layer = lambda t: t[None]
    return (y.reshape(bsz, seq, d),) + tuple(add_layer(t) for t in (new_k, new_v, c1, n1, m1, new_conv, new_cbuf))


def kernel(x_prompt, x_sample, cache_win_k, cache_win_v, state_mlstm_c, state_mlstm_n, state_mlstm_m, state_mlstm_conv, state_conv, norm_mix, norm_ffn, norm_final, rel_bias_table, w_in_mix, b_mlstm_gates, w_mlstm_qk_conv, attn_sinks, g_mlstm_norm, w_out_mix, w_pw1, b_pw1, w_dw, b_dw, ln_conv_g, ln_conv_b, w_pw2, b_pw2, w_router_group, b_router_group, w_router_expert, b_router_expert, w_expert_gate, w_expert_up, w_expert_down):
    w_in = w_in_mix[0]
    s_q, s_k, s_v, s_qk, s_vb, s_g = A_Q, A_Q + A_KV, A_Q + 2 * A_KV, A_Q + 2 * A_KV + 2 * B_QK, \
        A_Q + 2 * A_KV + 2 * B_QK + B_V, A_Q + 2 * A_KV + 2 * B_QK + B_V + 2 * B_HEADS
    w_in_r = jnp.concatenate([w_in[:, :s_q], w_in[:, s_v:s_qk], w_in[:, s_qk:s_vb], w_in[:, s_g:],
                              w_in[:, s_q:s_k], w_in[:, s_k:s_v], w_in[:, s_vb:s_g],
                              jnp.zeros((D_MODEL, LANES - 2 * B_HEADS), F32)], axis=1)
    b_gates = jnp.concatenate([b_mlstm_gates[0], jnp.zeros((LANES - 2 * B_HEADS,), F32)]).reshape(1, LANES)
    depth = w_router_group.shape[0]
    w_re = jnp.transpose(w_router_expert, (0, 2, 1, 3)).reshape(depth, D_MODEL, N_EXPERTS)
    w_router = jnp.concatenate([w_router_group, w_re,
                                jnp.zeros((depth, D_MODEL, LANES - N_GROUPS - N_EXPERTS), F32)], axis=-1)
    b_router = jnp.concatenate([b_router_group, b_router_expert.reshape(depth, N_EXPERTS),
                                jnp.zeros((depth, LANES - N_GROUPS - N_EXPERTS), F32)], axis=-1)[:, None, :]
    p = dict(norm_mix=norm_mix, norm_ffn=norm_ffn, norm_final=norm_final, rel_table=rel_bias_table,
             sinks=attn_sinks[0], w_in=w_in_r, b_gates=b_gates, conv_w=w_mlstm_qk_conv[0],
             g_mnorm=g_mlstm_norm[0].reshape(1, B_V), w_out=w_out_mix[0], w_pw1=w_pw1[0], b_pw1=b_pw1[0],
             w_dw=w_dw[0], b_dw=b_dw[0], ln_g=ln_conv_g[0], ln_b=ln_conv_b[0], w_pw2=w_pw2[0], b_pw2=b_pw2[0],
             w_router=w_router, b_router=b_router, w_eg=w_expert_gate, w_eu=w_expert_up, w_ed=w_expert_down)
    caches = (cache_win_k[0], cache_win_v[0], state_mlstm_c[0], state_mlstm_n[0], state_mlstm_m[0],
              state_mlstm_conv[0], state_conv[0])
    out_p = _trunk(x_prompt, None, p, prompt=True)
    out_s = _trunk(x_sample, caches, p, prompt=False)
    return (out_p[0], out_s[0]) + out_p[1:] + out_s[1:]
```
